```python
import jax, jax.numpy as jnp
from jax import lax
import numpy as np

D_MODEL = 1024
BATCH = 16
SEQ = 256
DEPTH = 2
DEC_BATCH = 4
DEC_SEQ = 4096
PAST_LEN = 512

GRID_W = 64
N_MIXERS = 2
N_ATTN_LAYERS = (DEPTH + 1) // 2
N_MLSTM_LAYERS = DEPTH // 2
HEAD_DIM = 64
N_HEADS = D_MODEL // HEAD_DIM
N_KV_HEADS = N_HEADS // 4
GROUP = N_HEADS // N_KV_HEADS
WINDOW = 128
QBLK = 128
KWIN = QBLK + 2 * WINDOW
ROPE_BASE = 10000.0
ATTN_Q_DIM = N_HEADS * HEAD_DIM
ATTN_KV_DIM = N_KV_HEADS * HEAD_DIM
ATTN_IN_DIM = 2 * ATTN_Q_DIM + 2 * ATTN_KV_DIM
M_HEADS = 8
M_HD = D_MODEL // M_HEADS
M_DIM = M_HEADS * M_HD
M_CHUNK = 64
M_IN_DIM = 5 * M_DIM + 4 * M_HEADS
EPS = 1e-6

kernel_name = 'hybrid_dit_swa_mlstm_step'


def _rmsnorm(x, w):
    x32 = x.astype(jnp.float32)
    y = x32 * lax.rsqrt(jnp.mean(x32 * x32, axis=-1, keepdims=True) + EPS)
    return (y * w.astype(jnp.float32)).astype(x.dtype)


def _pre(x, cond, norm_w, ada_w, ada_b):
    mod = jax.nn.silu(cond) @ ada_w + ada_b
    shift, scale, gate = jnp.split(mod, 3, axis=-1)
    h = _rmsnorm(x, norm_w) * (1 + scale[:, None, :]) + shift[:, None, :]
    return h, gate[:, None, :]


def _branch_out(o, g, w_out):
    return (o * jax.nn.silu(g)) @ w_out


def _axial_rope(x):
    T = x.shape[1]
    rows = T // GRID_W
    row = jnp.repeat(jnp.arange(rows), GRID_W).astype(jnp.float32)
    col = jnp.tile(jnp.arange(GRID_W), rows).astype(jnp.float32)
    nf = HEAD_DIM // 4
    inv = ROPE_BASE ** (-jnp.arange(nf, dtype=jnp.float32) / nf)
    shape = (1, T) + (1,) * (x.ndim - 3) + (nf,)
    x32 = x.astype(jnp.float32)

    def rot(xh, pos):
        ang = (pos[:, None] * inv[None, :]).reshape(shape)
        cos, sin = jnp.cos(ang), jnp.sin(ang)
        x1, x2 = xh[..., :nf], xh[..., nf:]
        return jnp.concatenate([x1 * cos - x2 * sin, x1 * sin + x2 * cos], axis=-1)

    half = HEAD_DIM // 2
    out = jnp.concatenate([rot(x32[..., :half], row), rot(x32[..., half:], col)], axis=-1)
    return out.astype(x.dtype)


def _sink_attend(q, k, v, mask, sink):
    s = jnp.einsum('bqhgd,bkhd->bhgqk', q.astype(jnp.float32), k.astype(jnp.float32)) * (HEAD_DIM ** -0.5)
    if mask is not None:
        s = jnp.where(mask, s, -jnp.inf)
    sk = sink.astype(jnp.float32).reshape(1, N_KV_HEADS, GROUP, 1)
    m = jnp.maximum(jnp.max(s, axis=-1), sk)
    p = jnp.exp(s - m[..., None])
    den = jnp.sum(p, axis=-1) + jnp.exp(sk - m)
    o = jnp.einsum('bhgqk,bkhd->bqhgd', p, v.astype(jnp.float32))
    return (o / jnp.transpose(den, (0, 3, 1, 2))[..., None]).astype(q.dtype)


def _attn_project(h, w_in):
    B, T, _ = h.shape
    p = h @ w_in
    q = p[..., :ATTN_Q_DIM].reshape(B, T, N_KV_HEADS, GROUP, HEAD_DIM)
    g = p[..., ATTN_Q_DIM:2 * ATTN_Q_DIM]
    k = p[..., 2 * ATTN_Q_DIM:2 * ATTN_Q_DIM + ATTN_KV_DIM].reshape(B, T, N_KV_HEADS, HEAD_DIM)
    v = p[..., 2 * ATTN_Q_DIM + ATTN_KV_DIM:].reshape(B, T, N_KV_HEADS, HEAD_DIM)
    return q, g, k, v


def _ctx_attention(q, k, v, sink):
    B, S = q.shape[:2]
    nb = S // QBLK
    qb = jnp.moveaxis(q.reshape(B, nb, QBLK, N_KV_HEADS, GROUP, HEAD_DIM), 1, 0)
    o = lax.map(lambda qi: _sink_attend(qi, k, v, None, sink), qb)
    return jnp.moveaxis(o, 0, 1).reshape(B, S, ATTN_Q_DIM)


def _latent_attention(q, k, v, ck, cv, sink):
    B, T = q.shape[:2]
    P = ck.shape[1]
    nb = T // QBLK
    pad = ((0, 0), (WINDOW, WINDOW), (0, 0), (0, 0))
    kp = jnp.pad(k, pad)
    vp = jnp.pad(v, pad)
    qi = jnp.arange(QBLK)[:, None]
    kj = jnp.arange(KWIN)[None, :]
    band = (kj >= qi) & (kj - qi <= 2 * WINDOW)
    ctx_ok = jnp.ones((QBLK, P), dtype=bool)
    qb = jnp.moveaxis(q.reshape(B, nb, QBLK, N_KV_HEADS, GROUP, HEAD_DIM), 1, 0)

    def block(args):
        qblk, b = args
        start = b * QBLK
        kb = lax.dynamic_slice_in_dim(kp, start, KWIN, axis=1)
        vb = lax.dynamic_slice_in_dim(vp, start, KWIN, axis=1)
        pos = start - WINDOW + kj
        valid = band & (pos >= 0) & (pos < T)
        mask = jnp.concatenate([ctx_ok, valid], axis=1)
        keys = jnp.concatenate([ck.astype(kb.dtype), kb], axis=1)
        vals = jnp.concatenate([cv.astype(vb.dtype), vb], axis=1)
        return _sink_attend(qblk, keys, vals, mask, sink)

    o = lax.map(block, (qb, jnp.arange(nb)))
    return jnp.moveaxis(o, 0, 1).reshape(B, T, ATTN_Q_DIM)


def _attn_context_layer(x, cond, norm_w, ada_w, ada_b, w_in, sink, w_out):
    h, gate = _pre(x, cond, norm_w, ada_w, ada_b)
    q, g, k, v = _attn_project(h, w_in)
    o = _ctx_attention(q, k, v, sink)
    return x + gate * _branch_out(o, g, w_out), k, v


def _attn_latent_layer(x, cond, ck, cv, norm_w, ada_w, ada_b, w_in, sink, w_out):
    h, gate = _pre(x, cond, norm_w, ada_w, ada_b)
    q, g, k, v = _attn_project(h, w_in)
    q = _axial_rope(q)
    k = _axial_rope(k)
    o = _latent_attention(q, k, v, ck, cv, sink)
    return x + gate * _branch_out(o, g, w_out)


def _mlstm_scan(q, k, v, ig, fg, init):
    C0, n0, m0 = init
    B, T, H, DK = q.shape
    DV = v.shape[-1]
    nc = T // M_CHUNK

    def chunks(a):
        return jnp.moveaxis(a.astype(jnp.float32).reshape((B, nc, M_CHUNK) + a.shape[2:]), 1, 0)

    xs = (chunks(q), chunks(k) * (DK ** -0.5), chunks(v), chunks(ig),
          chunks(jax.nn.log_sigmoid(fg.astype(jnp.float32))))
    causal = jnp.tril(jnp.ones((M_CHUNK, M_CHUNK), dtype=bool))

    def step(carry, inp):
        C, n, m = carry
        qc, kc, vc, ic, lfc = inp
        b = jnp.transpose(jnp.cumsum(lfc, axis=1), (0, 2, 1))
        it = jnp.transpose(ic, (0, 2, 1))
        a = jnp.where(causal, b[..., :, None] - b[..., None, :] + it[..., None, :], -jnp.inf)
        a0 = b + m[..., None]
        mt = jnp.maximum(a0, jnp.max(a, axis=-1))
        w = jnp.exp(a - mt[..., None])
        w0 = jnp.exp(a0 - mt)
        s = jnp.einsum('blhd,bshd->bhls', qc, kc) * w
        w0t = jnp.transpose(w0, (0, 2, 1))[..., None]
        num = jnp.einsum('bhls,bshe->blhe', s, vc) + w0t * jnp.einsum('blhd,bhde->blhe', qc, C)
        den = jnp.sum(s, axis=-1) + w0 * jnp.einsum('blhd,bhd->bhl', qc, n)
        den = jnp.maximum(jnp.abs(den), jnp.exp(-mt))
        hc = num / jnp.transpose(den, (0, 2, 1))[..., None]
        bl = b[..., -1]
        ak = it + bl[..., None] - b
        m_new = jnp.maximum(bl + m, jnp.max(ak, axis=-1))
        wk = jnp.transpose(jnp.exp(ak - m_new[..., None]), (0, 2, 1))[..., None]
        decay = jnp.exp(bl + m - m_new)
        kw = kc * wk
        C_new = decay[..., None, None] * C + jnp.einsum('bshd,bshe->bhde', kw, vc)
        n_new = decay[..., None] * n + jnp.sum(kw, axis=1)
        return (C_new, n_new, m_new), hc

    init32 = (C0.astype(jnp.float32), n0.astype(jnp.float32), m0.astype(jnp.float32))
    state, hs = lax.scan(step, init32, xs)
    h = jnp.moveaxis(hs, 0, 1).reshape(B, T, H, DV)
    return h, state


def _mlstm_layer(x, cond, init_f, init_b, norm_w, ada_w, ada_b, w_in, b_gates, w_out):
    B, T, _ = x.shape
    h, gate = _pre(x, cond, norm_w, ada_w, ada_b)
    p = h @ w_in
    q = p[..., :M_DIM].reshape(B, T, M_HEADS, M_HD)
    k = p[..., M_DIM:2 * M_DIM].reshape(B, T, M_HEADS, M_HD)
    v = p[..., 2 * M_DIM:3 * M_DIM].reshape(B, T, M_HEADS, M_HD)
    o = p[..., 3 * M_DIM:4 * M_DIM]
    g = p[..., 4 * M_DIM:5 * M_DIM]
    gates = (p[..., 5 * M_DIM:] + b_gates).astype(jnp.float32).reshape(B, T, 4, M_HEADS)
    i_f, f_f, i_b, f_b = gates[..., 0, :], gates[..., 1, :], gates[..., 2, :], gates[..., 3, :]
    hf, sf = _mlstm_scan(q, k, v, i_f, f_f, init_f)
    fl = lambda a: jnp.flip(a, axis=1)
    hb, sb = _mlstm_scan(fl(q), fl(k), fl(v), fl(i_b), fl(f_b), init_b)
    hm = (hf + fl(hb)).reshape(B, T, M_DIM) * jax.nn.sigmoid(o.astype(jnp.float32))
    y = _branch_out(hm.astype(x.dtype), g, w_out)
    return x + gate * y, sf, sb


def setup_inputs(seed: int = 0) -> dict:
    key = jax.random.key(seed)
    ks = jax.random.split(key, 24)

    def nrm(k, shape, s):
        return s * jax.random.normal(k, shape, dtype=jnp.float32)

    D = D_MODEL
    i_bias = nrm(ks[19], (N_MLSTM_LAYERS, 2, M_HEADS), 0.1)
    f_bias = 3.0 + nrm(ks[20], (N_MLSTM_LAYERS, 2, M_HEADS), 0.5)
    b_gates = jnp.stack([i_bias[:, 0], f_bias[:, 0], i_bias[:, 1], f_bias[:, 1]], axis=1).reshape(N_MLSTM_LAYERS, 4 * M_HEADS)
    return {
        'x_prompt': nrm(ks[0], (BATCH, SEQ, D), 1.0),
        'x_sample': nrm(ks[1], (DEC_BATCH, DEC_SEQ, D), 1.0),
        'cache_k': nrm(ks[2], (DEC_BATCH, N_ATTN_LAYERS, PAST_LEN, N_KV_HEADS, HEAD_DIM), 1.0),
        'cache_v': nrm(ks[3], (DEC_BATCH, N_ATTN_LAYERS, PAST_LEN, N_KV_HEADS, HEAD_DIM), 1.0),
        'state_C': nrm(ks[4], (DEC_BATCH, N_MLSTM_LAYERS, 2, M_HEADS, M_HD, M_HD), 0.5),
        'state_n': nrm(ks[5], (DEC_BATCH, N_MLSTM_LAYERS, 2, M_HEADS, M_HD), 0.5),
        'state_m': nrm(ks[6], (DEC_BATCH, N_MLSTM_LAYERS, 2, M_HEADS), 0.5),
        'c': nrm(ks[7], (DEC_BATCH, D), 1.0),
        'c_ctx': nrm(ks[8], (D,), 1.0),
        'attn_norm_w': 1.0 + nrm(ks[9], (N_ATTN_LAYERS, D), 0.02),
        'attn_ada_w': nrm(ks[10], (N_ATTN_LAYERS, D, 3 * D), 0.5 * D ** -0.5),
        'attn_ada_b': nrm(ks[11], (N_ATTN_LAYERS, 3 * D), 0.02),
        'attn_w_in': nrm(ks[12], (N_ATTN_LAYERS, D, ATTN_IN_DIM), D ** -0.5),
        'attn_sink': nrm(ks[13], (N_ATTN_LAYERS, N_HEADS), 0.5),
        'attn_w_out': nrm(ks[14], (N_ATTN_LAYERS, ATTN_Q_DIM, D), ATTN_Q_DIM ** -0.5),
        'mlstm_norm_w': 1.0 + nrm(ks[15], (N_MLSTM_LAYERS, D), 0.02),
        'mlstm_ada_w': nrm(ks[16], (N_MLSTM_LAYERS, D, 3 * D), 0.5 * D ** -0.5),
        'mlstm_ada_b': nrm(ks[17], (N_MLSTM_LAYERS, 3 * D), 0.02),
        'mlstm_w_in': nrm(ks[18], (N_MLSTM_LAYERS, D, M_IN_DIM), D ** -0.5),
        'mlstm_b_gates': b_gates,
        'mlstm_w_out': nrm(ks[21], (N_MLSTM_LAYERS, M_DIM, D), M_DIM ** -0.5),
        'final_norm_w': 1.0 + nrm(ks[22], (D,), 0.02),
    }


def reference(x_prompt, x_sample, cache_k, cache_v, state_C, state_n, state_m, c, c_ctx,
              attn_norm_w, attn_ada_w, attn_ada_b, attn_w_in, attn_sink, attn_w_out,
              mlstm_norm_w, mlstm_ada_w, mlstm_ada_b, mlstm_w_in, mlstm_b_gates, mlstm_w_out,
              final_norm_w):
    B = x_prompt.shape[0]
    ctx_cond = c_ctx[None, :]
    zero_init = (jnp.zeros((B, M_HEADS, M_HD, M_HD), jnp.float32),
                 jnp.zeros((B, M_HEADS, M_HD), jnp.float32),
                 jnp.zeros((B, M_HEADS), jnp.float32))

    x = x_prompt
    ks_, vs_, Cs, ns, ms = [], [], [], [], []
    for i in range(DEPTH):
        j = i // N_MIXERS
        if i % N_MIXERS == 0:
            x, k, v = _attn_context_layer(x, ctx_cond, attn_norm_w[j], attn_ada_w[j], attn_ada_b[j],
                                          attn_w_in[j], attn_sink[j], attn_w_out[j])
            ks_.append(k)
            vs_.append(v)
        else:
            x, sf, sb = _mlstm_layer(x, ctx_cond, zero_init, zero_init, mlstm_norm_w[j], mlstm_ada_w[j],
                                     mlstm_ada_b[j], mlstm_w_in[j], mlstm_b_gates[j], mlstm_w_out[j])
            Cs.append(jnp.stack([sf[0], sb[0]], axis=1))
            ns.append(jnp.stack([sf[1], sb[1]], axis=1))
            ms.append(jnp.stack([sf[2], sb[2]], axis=1))
    y_prompt = _rmsnorm(x, final_norm_w)

    x = x_sample
    for i in range(DEPTH):
        j = i // N_MIXERS
        if i % N_MIXERS == 0:
            x = _attn_latent_layer(x, c, cache_k[:, j], cache_v[:, j], attn_norm_w[j], attn_ada_w[j],
                                   attn_ada_b[j], attn_w_in[j], attn_sink[j], attn_w_out[j])
        else:
            init_f = (state_C[:, j, 0], state_n[:, j, 0], state_m[:, j, 0])
            init_b = (state_C[:, j, 1], state_n[:, j, 1], state_m[:, j, 1])
            x, _, _ = _mlstm_layer(x, c, init_f, init_b, mlstm_norm_w[j], mlstm_ada_w[j], mlstm_ada_b[j],
                                   mlstm_w_in[j], mlstm_b_gates[j], mlstm_w_out[j])
    y_sample = _rmsnorm(x, final_norm_w)

    new_cache_k = jnp.stack(ks_, axis=1)
    new_cache_v = jnp.stack(vs_, axis=1)
    new_state_C = jnp.stack(Cs, axis=1)
    new_state_n = jnp.stack(ns, axis=1)
    new_state_m = jnp.stack(ms, axis=1)
    return (y_prompt, y_sample, new_cache_k, new_cache_v, new_state_C, new_state_n, new_state_m)
```

```python
import functools

import jax
import jax.numpy as jnp
from jax import lax
from jax.experimental import pallas as pl
from jax.experimental.pallas import tpu as pltpu

F32 = jnp.float32
BF16 = jnp.bfloat16

HEAD_DIM = 64
N_KV_HEADS = 4
GROUP = 4
N_HEADS = N_KV_HEADS * GROUP
QBLK = 128
GRID_W = 64
ROPE_BASE = 10000.0
M_HEADS = 8
M_HD = 128
EPS = 1e-6

LANES = 128
VMEM_LIMIT = 48 * 1024 * 1024

MCHUNK = 128
ROW_TILE = 256

NEG_INF = float("-inf")


def _cparams(sem):
    return pltpu.CompilerParams(dimension_semantics=sem, vmem_limit_bytes=VMEM_LIMIT)


def _silu(x):
    return x * jax.nn.sigmoid(x)


def _log_sigmoid(x):
    return jnp.minimum(x, 0.0) - jnp.log1p(jnp.exp(-jnp.abs(x)))


def _dot(a, b):
    return jnp.dot(a, b, preferred_element_type=F32)


def _dot_nt(a, b):
    return lax.dot_general(a, b, (((1,), (1,)), ((), ())), preferred_element_type=F32)


def _dot_tn(a, b):
    return lax.dot_general(a, b, (((0,), (0,)), ((), ())), preferred_element_type=F32)


def _split3(x):
    hi = x.astype(BF16)
    r = x - hi.astype(F32)
    mid = r.astype(BF16)
    lo = (r - mid.astype(F32)).astype(BF16)
    return hi, mid, lo


def _prenorm(x, norm_w, mod):
    ms = jnp.mean(x * x, axis=-1, keepdims=True)
    y = x * lax.rsqrt(ms + EPS) * norm_w
    return y * (1.0 + mod[1:2, :]) + mod[0:1, :]


def _ada_kernel(cond_ref, w_ref, b_ref, o_ref):
    a = _silu(cond_ref[...]).astype(BF16)
    o_ref[...] = _dot(a, w_ref[...].astype(BF16)) + b_ref[...]


def _ada(cond8, w, b):
    d, n = w.shape
    tn = 512
    return pl.pallas_call(
        _ada_kernel,
        out_shape=jax.ShapeDtypeStruct((cond8.shape[0], n), F32),
        grid=(n // tn,),
        in_specs=[pl.BlockSpec(cond8.shape, lambda j: (0, 0)),
                  pl.BlockSpec((d, tn), lambda j: (0, j)),
                  pl.BlockSpec((1, tn), lambda j: (0, j))],
        out_specs=pl.BlockSpec((cond8.shape[0], tn), lambda j: (0, j)),
        compiler_params=_cparams(("parallel",)),
        name="ada_mod",
    )(cond8, w, b.reshape(1, n))


def _rope(x, cos, sin, lane):
    first = (lane & 31) < 16
    outs = []
    for c in range(x.shape[1] // LANES):
        xc = x[:, c * LANES:(c + 1) * LANES]
        sw = jnp.where(first, pltpu.roll(xc, LANES - 16, 1), pltpu.roll(xc, 16, 1))
        outs.append(xc * cos + sw * sin)
    return jnp.concatenate(outs, axis=1)


def _attn_in_kernel(*refs, rope):
    if rope:
        x_ref, mod_ref, nw_ref, w_ref, cos_ref, sin_ref, q_ref, sg_ref, k_ref, v_ref = refs
    else:
        x_ref, mod_ref, nw_ref, w_ref, q_ref, sg_ref, k_ref, v_ref = refs
    dq = q_ref.shape[-1]
    dkv = k_ref.shape[-1]
    hb = _prenorm(x_ref[0], nw_ref[...], mod_ref[0]).astype(BF16)
    q = _dot(hb, w_ref[:, 0:dq])
    g = _dot(hb, w_ref[:, dq:2 * dq])
    k = _dot(hb, w_ref[:, 2 * dq:2 * dq + dkv])
    v = _dot(hb, w_ref[:, 2 * dq + dkv:2 * dq + 2 * dkv])
    if rope:
        cos = cos_ref[...]
        sin = sin_ref[...]
        lane = lax.broadcasted_iota(jnp.int32, cos.shape, 1)
        q = _rope(q, cos, sin, lane)
        k = _rope(k, cos, sin, lane)
    q_ref[0] = (q * (HEAD_DIM ** -0.5)).astype(q_ref.dtype)
    sg_ref[0] = _silu(g).astype(sg_ref.dtype)
    k_ref[0] = k.astype(k_ref.dtype)
    v_ref[0] = v.astype(v_ref.dtype)


def _attn_in(x, mod3, mod_row, norm_w, w_bf, rope_tabs, kv_dtype):
    bsz, t, d = x.shape
    dq = N_HEADS * HEAD_DIM
    dkv = N_KV_HEADS * HEAD_DIM
    tm = ROW_TILE
    rope = rope_tabs is not None
    in_specs = [pl.BlockSpec((1, tm, d), lambda b, i: (b, i, 0)),
                pl.BlockSpec((1, 3, d), lambda b, i: (mod_row(b), 0, 0)),
                pl.BlockSpec((1, d), lambda b, i: (0, 0)),
                pl.BlockSpec(w_bf.shape, lambda b, i: (0, 0))]
    args = [x, mod3, norm_w.reshape(1, d), w_bf]
    if rope:
        in_specs += [pl.BlockSpec((tm, LANES), lambda b, i: (i, 0))] * 2
        args += list(rope_tabs)
    return pl.pallas_call(
        functools.partial(_attn_in_kernel, rope=rope),
        out_shape=(jax.ShapeDtypeStruct((bsz, t, dq), BF16),
                   jax.ShapeDtypeStruct((bsz, t, dq), BF16),
                   jax.ShapeDtypeStruct((bsz, t, dkv), kv_dtype),
                   jax.ShapeDtypeStruct((bsz, t, dkv), kv_dtype)),
        grid=(bsz, t // tm),
        in_specs=in_specs,
        out_specs=(pl.BlockSpec((1, tm, dq), lambda b, i: (b, i, 0)),
                   pl.BlockSpec((1, tm, dq), lambda b, i: (b, i, 0)),
                   pl.BlockSpec((1, tm, dkv), lambda b, i: (b, i, 0)),
                   pl.BlockSpec((1, tm, dkv), lambda b, i: (b, i, 0))),
        compiler_params=_cparams(("parallel", "parallel")),
        name="attn_in_rope" if rope else "attn_in",
    )(*args)


def _attn_kernel(*refs, window, nb):
    if window:
        (q_ref, sg_ref, x_ref, mod_ref, kc_ref, vc_ref, kp_ref, kq_ref, kn_ref,
         vp_ref, vq_ref, vn_ref, sink_ref, wo_ref, o_ref, oscr) = refs
    else:
        q_ref, sg_ref, x_ref, mod_ref, kc_ref, vc_ref, sink_ref, wo_ref, o_ref, oscr = refs
    blk = pl.program_id(1)
    q = q_ref[0]
    p_len = kc_ref.shape[1]
    rows = GROUP * QBLK
    if window:
        qi = lax.broadcasted_iota(jnp.int32, (rows, QBLK), 0) & (QBLK - 1)
        kj = lax.broadcasted_iota(jnp.int32, (rows, QBLK), 1)
        prev_ok = (kj >= qi) & (blk > 0)
        next_ok = (kj <= qi) & (blk < nb - 1)
        bias = jnp.concatenate(
            [jnp.zeros((rows, p_len), F32), jnp.where(prev_ok, 0.0, NEG_INF),
             jnp.zeros((rows, QBLK), F32), jnp.where(next_ok, 0.0, NEG_INF)], axis=1)
    for kvh in range(N_KV_HEADS):
        cs = slice(kvh * HEAD_DIM, (kvh + 1) * HEAD_DIM)
        keys = [kc_ref[0][:, cs].astype(BF16)]
        vals = [vc_ref[0][:, cs].astype(BF16)]
        if window:
            keys += [kp_ref[0][:, cs], kq_ref[0][:, cs], kn_ref[0][:, cs]]
            vals += [vp_ref[0][:, cs], vq_ref[0][:, cs], vn_ref[0][:, cs]]
        kk = jnp.concatenate(keys, axis=0)
        vv = jnp.concatenate(vals, axis=0)
        heads = [kvh * GROUP + j for j in range(GROUP)]
        q4 = jnp.concatenate([q[:, h * HEAD_DIM:(h + 1) * HEAD_DIM] for h in heads], axis=0)
        sk = jnp.concatenate(
            [jnp.broadcast_to(sink_ref[0:1, h:h + 1], (QBLK, 1)) for h in heads], axis=0)
        s = _dot_nt(q4, kk)
        if window:
            s = s + bias
        m = jnp.maximum(jnp.max(s, axis=1, keepdims=True), sk)
        p = jnp.exp(s - m)
        den = jnp.sum(p, axis=1, keepdims=True) + jnp.exp(sk - m)
        o = _dot(p.astype(BF16), vv) / den
        for j, h in enumerate(heads):
            oscr[:, h * HEAD_DIM:(h + 1) * HEAD_DIM] = o[j * QBLK:(j + 1) * QBLK, :]
    z = (oscr[...] * sg_ref[0].astype(F32)).astype(BF16)
    y = _dot(z, wo_ref[...])
    o_ref[0] = x_ref[0] + mod_ref[0][2:3, :] * y


def _attn(q, sg, x, mod3, mod_row, kc, vc, k_lat, v_lat, sink, wo_bf):
    bsz, t, d = x.shape
    dq = q.shape[-1]
    dkv = kc.shape[-1]
    nb = t // QBLK
    window = k_lat is not None
    tok = lambda b, i: (b, i, 0)
    in_specs = [pl.BlockSpec((1, QBLK, dq), tok),
                pl.BlockSpec((1, QBLK, dq), tok),
                pl.BlockSpec((1, QBLK, d), tok),
                pl.BlockSpec((1, 3, d), lambda b, i: (mod_row(b), 0, 0)),
                pl.BlockSpec((1, kc.shape[1], dkv), lambda b, i: (b, 0, 0)),
                pl.BlockSpec((1, vc.shape[1], dkv), lambda b, i: (b, 0, 0))]
    args = [q, sg, x, mod3, kc, vc]
    if window:
        prev = lambda b, i: (b, jnp.maximum(i - 1, 0), 0)
        nxt = lambda b, i: (b, jnp.minimum(i + 1, nb - 1), 0)
        for arr in (k_lat, v_lat):
            in_specs += [pl.BlockSpec((1, QBLK, dkv), prev), pl.BlockSpec((1, QBLK, dkv), tok),
                         pl.BlockSpec((1, QBLK, dkv), nxt)]
            args += [arr, arr, arr]
    in_specs += [pl.BlockSpec((1, N_HEADS), lambda b, i: (0, 0)),
                 pl.BlockSpec(wo_bf.shape, lambda b, i: (0, 0))]
    args += [sink.reshape(1, N_HEADS), wo_bf]
    return pl.pallas_call(
        functools.partial(_attn_kernel, window=window, nb=nb),
        out_shape=jax.ShapeDtypeStruct((bsz, t, d), F32),
        grid=(bsz, nb),
        in_specs=in_specs,
        out_specs=pl.BlockSpec((1, QBLK, d), tok),
        scratch_shapes=[pltpu.VMEM((QBLK, dq), F32)],
        compiler_params=_cparams(("parallel", "parallel")),
        name="attn_window" if window else "attn_ctx",
    )(*args)


def _mlstm_in_kernel(x_ref, mod_ref, nw_ref, w_ref, wg_ref, wgt_ref, bg_ref, bgt_ref,
                     q_ref, k_ref, v_ref, og_ref, gc_ref, gr_ref):
    dm = q_ref.shape[-1]
    nh2 = gc_ref.shape[-1]
    hb = _prenorm(x_ref[0], nw_ref[...], mod_ref[0]).astype(BF16)
    q_ref[0] = _dot(hb, w_ref[:, 0:dm]).astype(q_ref.dtype)
    k_ref[0] = _dot(hb, w_ref[:, dm:2 * dm]).astype(k_ref.dtype)
    v_ref[0] = _dot(hb, w_ref[:, 2 * dm:3 * dm]).astype(v_ref.dtype)
    o = _dot(hb, w_ref[:, 3 * dm:4 * dm])
    g = _dot(hb, w_ref[:, 4 * dm:5 * dm])
    og_ref[0] = (jax.nn.sigmoid(o) * _silu(g)).astype(og_ref.dtype)
    gc = _dot(hb, wg_ref[...]) + bg_ref[...]
    gr = _dot_nt(wgt_ref[...], hb) + bgt_ref[...]
    for dr in range(2):
        gc_ref[0, dr] = gc[:, dr * nh2:(dr + 1) * nh2]
        gr_ref[0, dr] = gr[dr * nh2:(dr + 1) * nh2, :]


def _mlstm_in(x, mod3, mod_row, norm_w, w_main_bf, wg_bf, b_gates):
    bsz, t, d = x.shape
    dm = M_HEADS * M_HD
    ng = 4 * M_HEADS
    tm = ROW_TILE
    tok = lambda b, i: (b, i, 0)
    const = lambda b, i: (0, 0)
    big = jax.ShapeDtypeStruct((bsz, t, dm), BF16)
    return pl.pallas_call(
        _mlstm_in_kernel,
        out_shape=(big, big, big, big,
                   jax.ShapeDtypeStruct((bsz, 2, t, ng // 2), F32),
                   jax.ShapeDtypeStruct((bsz, 2, ng // 2, t), F32)),
        grid=(bsz, t // tm),
        in_specs=[pl.BlockSpec((1, tm, d), tok),
                  pl.BlockSpec((1, 3, d), lambda b, i: (mod_row(b), 0, 0)),
                  pl.BlockSpec((1, d), const),
                  pl.BlockSpec(w_main_bf.shape, const),
                  pl.BlockSpec((d, ng), const),
                  pl.BlockSpec((ng, d), const),
                  pl.BlockSpec((1, ng), const),
                  pl.BlockSpec((ng, 1), const)],
        out_specs=(pl.BlockSpec((1, tm, dm), tok),) * 4
                  + (pl.BlockSpec((1, 2, tm, ng // 2), lambda b, i: (b, 0, i, 0)),
                     pl.BlockSpec((1, 2, ng // 2, tm), lambda b, i: (b, 0, 0, i))),
        compiler_params=_cparams(("parallel", "parallel")),
        name="mlstm_in",
    )(x, mod3, norm_w.reshape(1, d), w_main_bf, wg_bf, wg_bf.T,
      b_gates.reshape(1, ng), b_gates.reshape(ng, 1))


def _mlstm_scan_kernel(*refs, has_init, write_state, nc):
    refs = list(refs)
    q_ref, k_ref, v_ref, gc_ref, gr_ref = refs[:5]
    pos = 5
    if has_init:
        c0_ref, n0_ref, m0_ref = refs[pos:pos + 3]
        pos += 3
    h_ref = refs[pos]
    pos += 1
    if write_state:
        cout_ref, nout_ref, mout_ref = refs[pos:pos + 3]
        pos += 3
    cext, mscr = refs[pos:pos + 2]

    drn = pl.program_id(1)
    c = pl.program_id(2)
    L = q_ref.shape[1]
    nh = M_HEADS
    scale = M_HD ** -0.5

    ri = lax.broadcasted_iota(jnp.int32, (L, L), 0)
    ci = lax.broadcasted_iota(jnp.int32, (L, L), 1)
    eye = ri == ci
    lane0 = lax.broadcasted_iota(jnp.int32, (L, LANES), 1) == 0

    @pl.when(c == 0)
    def _init():
        if has_init:
            for h in range(nh):
                cext[h, :, 0:M_HD] = c0_ref[0, 0, h]
                n_row = n0_ref[0, 0, h:h + 1, :]
                n_col = jnp.sum(jnp.where(eye, n_row, 0.0), axis=1, keepdims=True)
                cext[h, :, M_HD:2 * M_HD] = jnp.where(lane0, n_col, 0.0)
            mscr[...] = m0_ref[0, 0]
        else:
            cext[...] = jnp.zeros(cext.shape, F32)
            mscr[...] = jnp.zeros(mscr.shape, F32)

    seen = (ci - ri) * (1 - 2 * drn) <= 0
    tri = jnp.where(seen, 1.0, 0.0).astype(BF16)

    gcb = gc_ref[0, 0]
    grb = gr_ref[0, 0]
    lf_c = _log_sigmoid(gcb[:, nh:2 * nh])
    lf_r = _log_sigmoid(grb[nh:2 * nh, :])
    b_col = sum(_dot(tri, piece) for piece in _split3(lf_c))
    b_row = sum(_dot_nt(piece, tri) for piece in _split3(lf_r))
    g_col = gcb[:, 0:nh] - b_col
    g_row = grb[0:nh, :] - b_row
    b_last = jnp.sum(lf_r, axis=1, keepdims=True)
    g_max = jnp.max(g_row, axis=1, keepdims=True)

    q = q_ref[0]
    k = k_ref[0]
    v = v_ref[0]
    ones_col = jnp.where(lane0, 1.0, 0.0).astype(BF16)

    for h in range(nh):
        hs = slice(h * M_HD, (h + 1) * M_HD)
        qh, kh, vh = q[:, hs], k[:, hs], v[:, hs]
        vext = jnp.concatenate([vh, ones_col], axis=1)
        m_prev = mscr[h:h + 1, 0:1]
        a = jnp.where(seen, g_row[h:h + 1, :], NEG_INF)
        m_rows = jnp.maximum(jnp.max(a, axis=1, keepdims=True), m_prev)
        w = jnp.exp(a - m_rows)
        s = (_dot_nt(qh, kh) * w * scale).astype(BF16)
        intra = _dot(s, vext)
        cprev = cext[h]
        inter = _dot(qh, cprev.astype(BF16))
        w0 = jnp.exp(m_prev - m_rows)
        tot = intra + w0 * inter
        num = tot[:, 0:M_HD]
        den = tot[:, M_HD:M_HD + 1]
        floor = jnp.exp(-(b_col[:, h:h + 1] + m_rows))
        h_ref[0, 0, :, hs] = num / jnp.maximum(jnp.abs(den), floor)

        m_last = jnp.maximum(g_max[h:h + 1, :], m_prev)
        wk = jnp.exp(g_col[:, h:h + 1] - m_last)
        decay = jnp.exp(m_prev - m_last)
        kw = (kh.astype(F32) * (wk * scale)).astype(BF16)
        cext[h] = decay * cprev + _dot_tn(kw, vext)
        mscr[h:h + 1, :] = jnp.broadcast_to(b_last[h:h + 1, :] + m_last, (1, LANES))

    if write_state:
        @pl.when(c == nc - 1)
        def _final():
            for h in range(nh):
                cfin = cext[h]
                cout_ref[0, 0, h] = cfin[:, 0:M_HD]
                n_col = cfin[:, M_HD:M_HD + 1]
                nout_ref[0, 0, h:h + 1, :] = jnp.sum(jnp.where(eye, n_col, 0.0), axis=0, keepdims=True)
            mout_ref[0, 0] = mscr[...]


def _mlstm_scan(q, k, v, gc, gr, init, write_state):
    bsz, t, dm = q.shape
    L = MCHUNK
    nc = t // L
    nh2 = gc.shape[-1]
    chunk = lambda b, d, c: c + d * (nc - 1 - 2 * c)
    tok = lambda b, d, c: (b, chunk(b, d, c), 0)
    in_specs = [pl.BlockSpec((1, L, dm), tok)] * 3 + [
        pl.BlockSpec((1, 1, L, nh2), lambda b, d, c: (b, d, chunk(b, d, c), 0)),
        pl.BlockSpec((1, 1, nh2, L), lambda b, d, c: (b, d, 0, chunk(b, d, c)))]
    args = [q, k, v, gc, gr]
    st = lambda b, d, c: (b, d, 0, 0)
    st5 = lambda b, d, c: (b, d, 0, 0, 0)
    if init is not None:
        c0, n0, m0 = init
        in_specs += [pl.BlockSpec((1, 1, M_HEADS, M_HD, M_HD), st5),
                     pl.BlockSpec((1, 1, M_HEADS, M_HD), st),
                     pl.BlockSpec((1, 1, M_HEADS, LANES), st)]
        args += [c0, n0, jnp.broadcast_to(m0[..., None], m0.shape + (LANES,))]
    out_shape = [jax.ShapeDtypeStruct((2, bsz, t, dm), F32)]
    out_specs = [pl.BlockSpec((1, 1, L, dm), lambda b, d, c: (d, b, chunk(b, d, c), 0))]
    if write_state:
        out_shape += [jax.ShapeDtypeStruct((bsz, 2, M_HEADS, M_HD, M_HD), F32),
                      jax.ShapeDtypeStruct((bsz, 2, M_HEADS, M_HD), F32),
                      jax.ShapeDtypeStruct((bsz, 2, M_HEADS, LANES), F32)]
        out_specs += [pl.BlockSpec((1, 1, M_HEADS, M_HD, M_HD), st5),
                      pl.BlockSpec((1, 1, M_HEADS, M_HD), st),
                      pl.BlockSpec((1, 1, M_HEADS, LANES), st)]
    return pl.pallas_call(
        functools.partial(_mlstm_scan_kernel, has_init=init is not None,
                          write_state=write_state, nc=nc),
        out_shape=tuple(out_shape),
        grid=(bsz, 2, nc),
        in_specs=in_specs,
        out_specs=tuple(out_specs),
        scratch_shapes=[pltpu.VMEM((M_HEADS, M_HD, 2 * M_HD), F32),
                        pltpu.VMEM((M_HEADS, LANES), F32)],
        compiler_params=_cparams(("parallel", "parallel", "arbitrary")),
        name="mlstm_scan",
    )(*args)


def _mlstm_out_kernel(h_ref, og_ref, x_ref, mod_ref, wo_ref, fw_ref, o_ref):
    hm = (h_ref[0, 0] + h_ref[1, 0]) * og_ref[0].astype(F32)
    y = _dot(hm.astype(BF16), wo_ref[...])
    x2 = x_ref[0] + mod_ref[0][2:3, :] * y
    ms = jnp.mean(x2 * x2, axis=-1, keepdims=True)
    o_ref[0] = x2 * lax.rsqrt(ms + EPS) * fw_ref[...]


def _mlstm_out(hdir, og, x, mod3, mod_row, wo_bf, final_w):
    bsz, t, d = x.shape
    dm = og.shape[-1]
    tm = ROW_TILE
    tok = lambda b, i: (b, i, 0)
    return pl.pallas_call(
        _mlstm_out_kernel,
        out_shape=jax.ShapeDtypeStruct((bsz, t, d), F32),
        grid=(bsz, t // tm),
        in_specs=[pl.BlockSpec((2, 1, tm, dm), lambda b, i: (0, b, i, 0)),
                  pl.BlockSpec((1, tm, dm), tok),
                  pl.BlockSpec((1, tm, d), tok),
                  pl.BlockSpec((1, 3, d), lambda b, i: (mod_row(b), 0, 0)),
                  pl.BlockSpec(wo_bf.shape, lambda b, i: (0, 0)),
                  pl.BlockSpec((1, d), lambda b, i: (0, 0))],
        out_specs=pl.BlockSpec((1, tm, d), tok),
        compiler_params=_cparams(("parallel", "parallel")),
        name="mlstm_out",
    )(hdir, og, x, mod3, wo_bf, final_w.reshape(1, d))


def _rope_tables(t):
    nf = HEAD_DIM // 4
    pos = jnp.arange(t)
    row = (pos // GRID_W).astype(F32)
    col = (pos % GRID_W).astype(F32)
    inv = ROPE_BASE ** (-jnp.arange(nf, dtype=F32) / nf)
    ar = row[:, None] * inv[None, :]
    ac = col[:, None] * inv[None, :]
    cos = jnp.concatenate([jnp.cos(ar), jnp.cos(ar), jnp.cos(ac), jnp.cos(ac)], axis=1)
    sin = jnp.concatenate([-jnp.sin(ar), jnp.sin(ar), -jnp.sin(ac), jnp.sin(ac)], axis=1)
    reps = LANES // HEAD_DIM
    return jnp.tile(cos, (1, reps)), jnp.tile(sin, (1, reps))


def kernel(x_prompt, x_sample, cache_k, cache_v, state_C, state_n, state_m, c, c_ctx,
           attn_norm_w, attn_ada_w, attn_ada_b, attn_w_in, attn_sink, attn_w_out,
           mlstm_norm_w, mlstm_ada_w, mlstm_ada_b, mlstm_w_in, mlstm_b_gates, mlstm_w_out,
           final_norm_w):
    assert attn_w_in.shape[0] == 1 and mlstm_w_in.shape[0] == 1, "one layer of each mixer"
    bsz, seq, d = x_prompt.shape
    dbsz, dseq, _ = x_sample.shape
    dkv = N_KV_HEADS * HEAD_DIM
    dm = M_HEADS * M_HD

    n_cond = 1 + dbsz
    cond = jnp.concatenate([c_ctx[None, :], c, jnp.zeros((-n_cond % 8, d), F32)], axis=0)
    attn_mod = _ada(cond, attn_ada_w[0], attn_ada_b[0]).reshape(-1, 3, d)
    mlstm_mod = _ada(cond, mlstm_ada_w[0], mlstm_ada_b[0]).reshape(-1, 3, d)
    ctx_row = lambda b: 0
    lat_row = lambda b: b + 1

    attn_w_in_bf = attn_w_in[0].astype(BF16)
    attn_w_out_bf = attn_w_out[0].astype(BF16)
    w_main_bf = mlstm_w_in[0, :, :5 * dm].astype(BF16)
    wg_bf = mlstm_w_in[0, :, 5 * dm:].astype(BF16)
    mlstm_w_out_bf = mlstm_w_out[0].astype(BF16)

    def mlstm_layer(x, mod_row, init, write_state):
        q, k, v, og, gc, gr = _mlstm_in(x, mlstm_mod, mod_row, mlstm_norm_w[0], w_main_bf, wg_bf,
                                        mlstm_b_gates[0])
        outs = _mlstm_scan(q, k, v, gc, gr, init, write_state)
        y = _mlstm_out(outs[0], og, x, mlstm_mod, mod_row, mlstm_w_out_bf, final_norm_w)
        return y, outs[1:]

    q, sg, k_ctx, v_ctx = _attn_in(x_prompt, attn_mod, ctx_row, attn_norm_w[0], attn_w_in_bf, None, F32)
    x1 = _attn(q, sg, x_prompt, attn_mod, ctx_row, k_ctx, v_ctx, None, None, attn_sink[0], attn_w_out_bf)
    y_prompt, (c_fin, n_fin, m_fin) = mlstm_layer(x1, ctx_row, None, True)

    q, sg, k_lat, v_lat = _attn_in(x_sample, attn_mod, lat_row, attn_norm_w[0], attn_w_in_bf,
                                   _rope_tables(dseq), BF16)
    kc = cache_k[:, 0].reshape(dbsz, -1, dkv)
    vc = cache_v[:, 0].reshape(dbsz, -1, dkv)
    x1 = _attn(q, sg, x_sample, attn_mod, lat_row, kc, vc, k_lat, v_lat, attn_sink[0], attn_w_out_bf)
    y_sample, _ = mlstm_layer(x1, lat_row, (state_C[:, 0], state_n[:, 0], state_m[:, 0]), False)

    new_cache_k = k_ctx.reshape(bsz, 1, seq, N_KV_HEADS, HEAD_DIM)
    new_cache_v = v_ctx.reshape(bsz, 1, seq, N_KV_HEADS, HEAD_DIM)
    return (y_prompt, y_sample, new_cache_k, new_cache_v,
            c_fin[:, None], n_fin[:, None], m_fin[:, None, :, :, 0])
```

```python
import functools

import jax
import jax.numpy as jnp
from jax import lax
from jax.experimental import pallas as pl
from jax.experimental.pallas import tpu as pltpu

F32 = jnp.float32
BF16 = jnp.bfloat16

HEAD_DIM = 64
N_KV_HEADS = 4
GROUP = 4
N_HEADS = N_KV_HEADS * GROUP
QBLK = 128
GRID_W = 64
ROPE_BASE = 10000.0
M_HEADS = 8
M_HD = 128
EPS = 1e-6

LANES = 128
VMEM_LIMIT = 48 * 1024 * 1024

MCHUNK = 128
ROW_TILE = 256

NEG_INF = float("-inf")
LOG2E = 1.4426950408889634


def _cparams(sem):
    return pltpu.CompilerParams(dimension_semantics=sem, vmem_limit_bytes=VMEM_LIMIT)


def _silu(x):
    return x * jax.nn.sigmoid(x)


def _log_sigmoid(x):
    return jnp.minimum(x, 0.0) - jnp.log1p(jnp.exp(-jnp.abs(x)))


def _dot(a, b):
    return jnp.dot(a, b, preferred_element_type=F32)


def _dot_nt(a, b):
    return lax.dot_general(a, b, (((1,), (1,)), ((), ())), preferred_element_type=F32)


def _dot_tn(a, b):
    return lax.dot_general(a, b, (((0,), (0,)), ((), ())), preferred_element_type=F32)


def _split3(x):
    hi = x.astype(BF16)
    r = x - hi.astype(F32)
    mid = r.astype(BF16)
    lo = (r - mid.astype(F32)).astype(BF16)
    return hi, mid, lo


def _prenorm(x, norm_w, mod):
    ms = jnp.mean(x * x, axis=-1, keepdims=True)
    y = x * lax.rsqrt(ms + EPS) * norm_w
    return y * (1.0 + mod[1:2, :]) + mod[0:1, :]


def _ada_kernel(cond_ref, w_ref, b_ref, o_ref):
    a = _silu(cond_ref[...]).astype(BF16)
    o_ref[...] = _dot(a, w_ref[...].astype(BF16)) + b_ref[...]


def _ada(cond8, w, b):
    d, n = w.shape
    tn = 512
    return pl.pallas_call(
        _ada_kernel,
        out_shape=jax.ShapeDtypeStruct((cond8.shape[0], n), F32),
        grid=(n // tn,),
        in_specs=[pl.BlockSpec(cond8.shape, lambda j: (0, 0)),
                  pl.BlockSpec((d, tn), lambda j: (0, j)),
                  pl.BlockSpec((1, tn), lambda j: (0, j))],
        out_specs=pl.BlockSpec((cond8.shape[0], tn), lambda j: (0, j)),
        compiler_params=_cparams(("parallel",)),
        name="ada_mod",
    )(cond8, w, b.reshape(1, n))


def _rope(x, cos, sin, lane):
    first = (lane & 31) < 16
    outs = []
    for c in range(x.shape[1] // LANES):
        xc = x[:, c * LANES:(c + 1) * LANES]
        sw = jnp.where(first, pltpu.roll(xc, LANES - 16, 1), pltpu.roll(xc, 16, 1))
        outs.append(xc * cos + sw * sin)
    return jnp.concatenate(outs, axis=1)


def _attn_in_kernel(*refs, rope, emit_v):
    refs = list(refs)
    x_ref, mod_ref, nw_ref, w_ref, wvt_ref = refs[:5]
    pos = 5
    if rope:
        cos_ref, sin_ref = refs[pos:pos + 2]
        pos += 2
    q_ref, sg_ref, k_ref, vt_ref = refs[pos:pos + 4]
    dq = q_ref.shape[-1]
    dkv = k_ref.shape[-1]
    hb = _prenorm(x_ref[0], nw_ref[...], mod_ref[0]).astype(BF16)
    q = _dot(hb, w_ref[:, 0:dq])
    g = _dot(hb, w_ref[:, dq:2 * dq])
    k = _dot(hb, w_ref[:, 2 * dq:2 * dq + dkv])
    if rope:
        cos = cos_ref[...]
        sin = sin_ref[...]
        lane = lax.broadcasted_iota(jnp.int32, cos.shape, 1)
        q = _rope(q, cos, sin, lane)
        k = _rope(k, cos, sin, lane)
    q_ref[0] = (q * (HEAD_DIM ** -0.5 * LOG2E)).astype(q_ref.dtype)
    sg_ref[0] = _silu(g).astype(sg_ref.dtype)
    k_ref[0] = k.astype(k_ref.dtype)
    vt_ref[0] = _dot_nt(wvt_ref[...], hb).astype(vt_ref.dtype)
    if emit_v:
        v_ref = refs[pos + 4]
        v_ref[0] = _dot(hb, w_ref[:, 2 * dq + dkv:2 * dq + 2 * dkv]).astype(v_ref.dtype)


def _attn_in(x, mod3, mod_row, norm_w, w_bf, wvt_bf, rope_tabs, k_dtype, emit_v):
    bsz, t, d = x.shape
    dq = N_HEADS * HEAD_DIM
    dkv = N_KV_HEADS * HEAD_DIM
    tm = ROW_TILE
    rope = rope_tabs is not None
    tok = lambda b, i: (b, i, 0)
    const = lambda b, i: (0, 0)
    in_specs = [pl.BlockSpec((1, tm, d), tok),
                pl.BlockSpec((1, 3, d), lambda b, i: (mod_row(b), 0, 0)),
                pl.BlockSpec((1, d), const),
                pl.BlockSpec(w_bf.shape, const),
                pl.BlockSpec(wvt_bf.shape, const)]
    args = [x, mod3, norm_w.reshape(1, d), w_bf, wvt_bf]
    if rope:
        in_specs += [pl.BlockSpec((tm, LANES), lambda b, i: (i, 0))] * 2
        args += list(rope_tabs)
    out_shape = [jax.ShapeDtypeStruct((bsz, t, dq), BF16),
                 jax.ShapeDtypeStruct((bsz, t, dq), BF16),
                 jax.ShapeDtypeStruct((bsz, t, dkv), k_dtype),
                 jax.ShapeDtypeStruct((bsz, dkv, t), BF16)]
    out_specs = [pl.BlockSpec((1, tm, dq), tok), pl.BlockSpec((1, tm, dq), tok),
                 pl.BlockSpec((1, tm, dkv), tok),
                 pl.BlockSpec((1, dkv, tm), lambda b, i: (b, 0, i))]
    if emit_v:
        out_shape.append(jax.ShapeDtypeStruct((bsz, t, dkv), F32))
        out_specs.append(pl.BlockSpec((1, tm, dkv), tok))
    return pl.pallas_call(
        functools.partial(_attn_in_kernel, rope=rope, emit_v=emit_v),
        out_shape=tuple(out_shape),
        grid=(bsz, t // tm),
        in_specs=in_specs,
        out_specs=tuple(out_specs),
        compiler_params=_cparams(("parallel", "parallel")),
        name="attn_in_rope" if rope else "attn_in",
    )(*args)


def _attn_kernel(*refs, window, nb):
    if window:
        (q_ref, sg_ref, x_ref, mod_ref, kc_ref, vct_ref, kp_ref, kq_ref, kn_ref,
         vpt_ref, vqt_ref, vnt_ref, sink_ref, wo_ref, o_ref, s_scr, ot_scr) = refs
    else:
        q_ref, sg_ref, x_ref, mod_ref, kc_ref, vct_ref, sink_ref, wo_ref, o_ref, s_scr, ot_scr = refs
    blk = pl.program_id(1)
    q = q_ref[0]
    n_ctx = kc_ref.shape[1] // QBLK
    cols = GROUP * QBLK
    if window:
        kj = lax.broadcasted_iota(jnp.int32, (QBLK, cols), 0)
        qi = lax.broadcasted_iota(jnp.int32, (QBLK, cols), 1) & (QBLK - 1)
        prev_ok = (kj >= qi) & (blk > 0)
        next_ok = (kj <= qi) & (blk < nb - 1)
    ones_rows = jnp.where(lax.broadcasted_iota(jnp.int32, (16, QBLK), 0) == 0, 1.0, 0.0).astype(BF16)
    for kvh in range(N_KV_HEADS):
        cs = slice(kvh * HEAD_DIM, (kvh + 1) * HEAD_DIM)
        heads = [kvh * GROUP + j for j in range(GROUP)]
        q4 = jnp.concatenate([q[:, h * HEAD_DIM:(h + 1) * HEAD_DIM] for h in heads], axis=0)
        sink_row = jnp.concatenate(
            [jnp.broadcast_to(sink_ref[0:1, h:h + 1], (1, QBLK)) for h in heads], axis=1) * LOG2E
        blocks = [(kc_ref[0, j * QBLK:(j + 1) * QBLK, cs].astype(BF16),
                   vct_ref[0, cs, j * QBLK:(j + 1) * QBLK], None) for j in range(n_ctx)]
        if window:
            blocks += [(kp_ref[0][:, cs], vpt_ref[0, cs, :], prev_ok),
                       (kq_ref[0][:, cs], vqt_ref[0, cs, :], None),
                       (kn_ref[0][:, cs], vnt_ref[0, cs, :], next_ok)]
        macc = jnp.full((8, cols), NEG_INF, F32)
        for j, (kb, _, ok) in enumerate(blocks):
            st = _dot_nt(kb, q4)
            if ok is not None:
                st = jnp.where(ok, st, NEG_INF)
            s_scr[j] = st
            macc = jnp.maximum(macc, jnp.max(st.reshape(QBLK // 8, 8, cols), axis=0))
        m_row = jnp.maximum(jnp.max(macc, axis=0, keepdims=True), sink_row)
        acc = jnp.zeros((HEAD_DIM + 16, cols), F32)
        for j, (_, vt, _) in enumerate(blocks):
            p = jnp.exp2(s_scr[j] - m_row).astype(BF16)
            acc = acc + _dot(jnp.concatenate([vt, ones_rows], axis=0), p)
        den = acc[HEAD_DIM:HEAD_DIM + 1, :] + jnp.exp2(sink_row - m_row)
        o_t = acc[0:HEAD_DIM, :] / den
        for j, h in enumerate(heads):
            ot_scr[h * HEAD_DIM:(h + 1) * HEAD_DIM, :] = o_t[:, j * QBLK:(j + 1) * QBLK]
    z = (ot_scr[...].T * sg_ref[0].astype(F32)).astype(BF16)
    y = _dot(z, wo_ref[...])
    o_ref[0] = x_ref[0] + mod_ref[0][2:3, :] * y


def _attn(q, sg, x, mod3, mod_row, kc, vct, k_lat, vt_lat, sink, wo_bf):
    bsz, t, d = x.shape
    dq = q.shape[-1]
    dkv = kc.shape[-1]
    p_len = kc.shape[1]
    nb = t // QBLK
    window = k_lat is not None
    tok = lambda b, i: (b, i, 0)
    in_specs = [pl.BlockSpec((1, QBLK, dq), tok),
                pl.BlockSpec((1, QBLK, dq), tok),
                pl.BlockSpec((1, QBLK, d), tok),
                pl.BlockSpec((1, 3, d), lambda b, i: (mod_row(b), 0, 0)),
                pl.BlockSpec((1, p_len, dkv), lambda b, i: (b, 0, 0)),
                pl.BlockSpec((1, dkv, p_len), lambda b, i: (b, 0, 0))]
    args = [q, sg, x, mod3, kc, vct]
    n_blocks = p_len // QBLK
    if window:
        prev = lambda i: jnp.maximum(i - 1, 0)
        nxt = lambda i: jnp.minimum(i + 1, nb - 1)
        in_specs += [pl.BlockSpec((1, QBLK, dkv), lambda b, i: (b, prev(i), 0)),
                     pl.BlockSpec((1, QBLK, dkv), tok),
                     pl.BlockSpec((1, QBLK, dkv), lambda b, i: (b, nxt(i), 0)),
                     pl.BlockSpec((1, dkv, QBLK), lambda b, i: (b, 0, prev(i))),
                     pl.BlockSpec((1, dkv, QBLK), lambda b, i: (b, 0, i)),
                     pl.BlockSpec((1, dkv, QBLK), lambda b, i: (b, 0, nxt(i)))]
        args += [k_lat] * 3 + [vt_lat] * 3
        n_blocks += 3
    in_specs += [pl.BlockSpec((1, N_HEADS), lambda b, i: (0, 0)),
                 pl.BlockSpec(wo_bf.shape, lambda b, i: (0, 0))]
    args += [sink.reshape(1, N_HEADS), wo_bf]
    return pl.pallas_call(
        functools.partial(_attn_kernel, window=window, nb=nb),
        out_shape=jax.ShapeDtypeStruct((bsz, t, d), F32),
        grid=(bsz, nb),
        in_specs=in_specs,
        out_specs=pl.BlockSpec((1, QBLK, d), tok),
        scratch_shapes=[pltpu.VMEM((n_blocks, QBLK, GROUP * QBLK), F32),
                        pltpu.VMEM((dq, QBLK), F32)],
        compiler_params=_cparams(("parallel", "parallel")),
        name="attn_window" if window else "attn_ctx",
    )(*args)


def _mlstm_in_kernel(x_ref, mod_ref, nw_ref, w_ref, wg_ref, wgt_ref, bg_ref, bgt_ref,
                     q_ref, k_ref, v_ref, og_ref, gc_ref, gr_ref):
    dm = q_ref.shape[-1]
    nh2 = gc_ref.shape[-1]
    hb = _prenorm(x_ref[0], nw_ref[...], mod_ref[0]).astype(BF16)
    q_ref[0] = _dot(hb, w_ref[:, 0:dm]).astype(q_ref.dtype)
    k_ref[0] = _dot(hb, w_ref[:, dm:2 * dm]).astype(k_ref.dtype)
    v_ref[0] = _dot(hb, w_ref[:, 2 * dm:3 * dm]).astype(v_ref.dtype)
    o = _dot(hb, w_ref[:, 3 * dm:4 * dm])
    g = _dot(hb, w_ref[:, 4 * dm:5 * dm])
    og_ref[0] = (jax.nn.sigmoid(o) * _silu(g)).astype(og_ref.dtype)
    gc = _dot(hb, wg_ref[...]) + bg_ref[...]
    gr = _dot_nt(wgt_ref[...], hb) + bgt_ref[...]
    for dr in range(2):
        gc_ref[0, dr] = gc[:, dr * nh2:(dr + 1) * nh2]
        gr_ref[0, dr] = gr[dr * nh2:(dr + 1) * nh2, :]


def _mlstm_in(x, mod3, mod_row, norm_w, w_main_bf, wg_bf, b_gates):
    bsz, t, d = x.shape
    dm = M_HEADS * M_HD
    ng = 4 * M_HEADS
    tm = ROW_TILE
    tok = lambda b, i: (b, i, 0)
    const = lambda b, i: (0, 0)
    big = jax.ShapeDtypeStruct((bsz, t, dm), BF16)
    return pl.pallas_call(
        _mlstm_in_kernel,
        out_shape=(big, big, big, big,
                   jax.ShapeDtypeStruct((bsz, 2, t, ng // 2), F32),
                   jax.ShapeDtypeStruct((bsz, 2, ng // 2, t), F32)),
        grid=(bsz, t // tm),
        in_specs=[pl.BlockSpec((1, tm, d), tok),
                  pl.BlockSpec((1, 3, d), lambda b, i: (mod_row(b), 0, 0)),
                  pl.BlockSpec((1, d), const),
                  pl.BlockSpec(w_main_bf.shape, const),
                  pl.BlockSpec((d, ng), const),
                  pl.BlockSpec((ng, d), const),
                  pl.BlockSpec((1, ng), const),
                  pl.BlockSpec((ng, 1), const)],
        out_specs=(pl.BlockSpec((1, tm, dm), tok),) * 4
                  + (pl.BlockSpec((1, 2, tm, ng // 2), lambda b, i: (b, 0, i, 0)),
                     pl.BlockSpec((1, 2, ng // 2, tm), lambda b, i: (b, 0, 0, i))),
        compiler_params=_cparams(("parallel", "parallel")),
        name="mlstm_in",
    )(x, mod3, norm_w.reshape(1, d), w_main_bf, wg_bf, wg_bf.T,
      b_gates.reshape(1, ng), b_gates.reshape(ng, 1))


def _mlstm_scan_kernel(*refs, has_init, write_state, nc):
    refs = list(refs)
    q_ref, k_ref, v_ref, gc_ref, gr_ref = refs[:5]
    pos = 5
    if has_init:
        c0_ref, n0_ref, m0_ref = refs[pos:pos + 3]
        pos += 3
    h_ref = refs[pos]
    pos += 1
    if write_state:
        cout_ref, nout_ref, mout_ref = refs[pos:pos + 3]
        pos += 3
    cext, mscr = refs[pos:pos + 2]

    drn = pl.program_id(1)
    c = pl.program_id(2)
    L = q_ref.shape[1]
    nh = M_HEADS
    scale = M_HD ** -0.5

    ri = lax.broadcasted_iota(jnp.int32, (L, L), 0)
    ci = lax.broadcasted_iota(jnp.int32, (L, L), 1)
    eye = ri == ci
    lane0 = lax.broadcasted_iota(jnp.int32, (L, LANES), 1) == 0

    @pl.when(c == 0)
    def _init():
        if has_init:
            for h in range(nh):
                cext[h, :, 0:M_HD] = c0_ref[0, 0, h]
                n_row = n0_ref[0, 0, h:h + 1, :]
                n_col = jnp.sum(jnp.where(eye, n_row, 0.0), axis=1, keepdims=True)
                cext[h, :, M_HD:2 * M_HD] = jnp.where(lane0, n_col, 0.0)
            mscr[...] = m0_ref[0, 0]
        else:
            cext[...] = jnp.zeros(cext.shape, F32)
            mscr[...] = jnp.zeros(mscr.shape, F32)

    seen = (ci - ri) * (1 - 2 * drn) <= 0
    tri = jnp.where(seen, 1.0, 0.0).astype(BF16)

    gcb = gc_ref[0, 0]
    grb = gr_ref[0, 0]
    lf_c = _log_sigmoid(gcb[:, nh:2 * nh])
    lf_r = _log_sigmoid(grb[nh:2 * nh, :])
    b_col = sum(_dot(tri, piece) for piece in _split3(lf_c))
    b_row = sum(_dot_nt(piece, tri) for piece in _split3(lf_r))
    g_col = gcb[:, 0:nh] - b_col
    g_row = grb[0:nh, :] - b_row
    b_last = jnp.sum(lf_r, axis=1, keepdims=True)
    g_max = jnp.max(g_row, axis=1, keepdims=True)

    q = q_ref[0]
    k = k_ref[0]
    v = v_ref[0]
    ones_col = jnp.where(lane0, 1.0, 0.0).astype(BF16)

    for h in range(nh):
        hs = slice(h * M_HD, (h + 1) * M_HD)
        qh, kh, vh = q[:, hs], k[:, hs], v[:, hs]
        vext = jnp.concatenate([vh, ones_col], axis=1)
        m_prev = mscr[h:h + 1, 0:1]
        a = jnp.where(seen, g_row[h:h + 1, :], NEG_INF)
        m_rows = jnp.maximum(jnp.max(a, axis=1, keepdims=True), m_prev)
        w = jnp.exp(a - m_rows)
        s = (_dot_nt(qh, kh) * w * scale).astype(BF16)
        intra = _dot(s, vext)
        cprev = cext[h]
        inter = _dot(qh, cprev.astype(BF16))
        w0 = jnp.exp(m_prev - m_rows)
        tot = intra + w0 * inter
        num = tot[:, 0:M_HD]
        den = tot[:, M_HD:M_HD + 1]
        floor = jnp.exp(-(b_col[:, h:h + 1] + m_rows))
        h_ref[0, 0, :, hs] = num / jnp.maximum(jnp.abs(den), floor)

        m_last = jnp.maximum(g_max[h:h + 1, :], m_prev)
        wk = jnp.exp(g_col[:, h:h + 1] - m_last)
        decay = jnp.exp(m_prev - m_last)
        kw = (kh.astype(F32) * (wk * scale)).astype(BF16)
        cext[h] = decay * cprev + _dot_tn(kw, vext)
        mscr[h:h + 1, :] = jnp.broadcast_to(b_last[h:h + 1, :] + m_last, (1, LANES))

    if write_state:
        @pl.when(c == nc - 1)
        def _final():
            for h in range(nh):
                cfin = cext[h]
                cout_ref[0, 0, h] = cfin[:, 0:M_HD]
                n_col = cfin[:, M_HD:M_HD + 1]
                nout_ref[0, 0, h:h + 1, :] = jnp.sum(jnp.where(eye, n_col, 0.0), axis=0, keepdims=True)
            mout_ref[0, 0] = mscr[...]


def _mlstm_scan(q, k, v, gc, gr, init, write_state):
    bsz, t, dm = q.shape
    L = MCHUNK
    nc = t // L
    nh2 = gc.shape[-1]
    chunk = lambda b, d, c: c + d * (nc - 1 - 2 * c)
    tok = lambda b, d, c: (b, chunk(b, d, c), 0)
    in_specs = [pl.BlockSpec((1, L, dm), tok)] * 3 + [
        pl.BlockSpec((1, 1, L, nh2), lambda b, d, c: (b, d, chunk(b, d, c), 0)),
        pl.BlockSpec((1, 1, nh2, L), lambda b, d, c: (b, d, 0, chunk(b, d, c)))]
    args = [q, k, v, gc, gr]
    st = lambda b, d, c: (b, d, 0, 0)
    st5 = lambda b, d, c: (b, d, 0, 0, 0)
    if init is not None:
        c0, n0, m0 = init
        in_specs += [pl.BlockSpec((1, 1, M_HEADS, M_HD, M_HD), st5),
                     pl.BlockSpec((1, 1, M_HEADS, M_HD), st),
                     pl.BlockSpec((1, 1, M_HEADS, LANES), st)]
        args += [c0, n0, jnp.broadcast_to(m0[..., None], m0.shape + (LANES,))]
    out_shape = [jax.ShapeDtypeStruct((2, bsz, t, dm), F32)]
    out_specs = [pl.BlockSpec((1, 1, L, dm), lambda b, d, c: (d, b, chunk(b, d, c), 0))]
    if write_state:
        out_shape += [jax.ShapeDtypeStruct((bsz, 2, M_HEADS, M_HD, M_HD), F32),
                      jax.ShapeDtypeStruct((bsz, 2, M_HEADS, M_HD), F32),
                      jax.ShapeDtypeStruct((bsz, 2, M_HEADS, LANES), F32)]
        out_specs += [pl.BlockSpec((1, 1, M_HEADS, M_HD, M_HD), st5),
                      pl.BlockSpec((1, 1, M_HEADS, M_HD), st),
                      pl.BlockSpec((1, 1, M_HEADS, LANES), st)]
    return pl.pallas_call(
        functools.partial(_mlstm_scan_kernel, has_init=init is not None,
                          write_state=write_state, nc=nc),
        out_shape=tuple(out_shape),
        grid=(bsz, 2, nc),
        in_specs=in_specs,
        out_specs=tuple(out_specs),
        scratch_shapes=[pltpu.VMEM((M_HEADS, M_HD, 2 * M_HD), F32),
                        pltpu.VMEM((M_HEADS, LANES), F32)],
        compiler_params=_cparams(("parallel", "parallel", "arbitrary")),
        name="mlstm_scan",
    )(*args)


def _mlstm_out_kernel(h_ref, og_ref, x_ref, mod_ref, wo_ref, fw_ref, o_ref):
    hm = (h_ref[0, 0] + h_ref[1, 0]) * og_ref[0].astype(F32)
    y = _dot(hm.astype(BF16), wo_ref[...])
    x2 = x_ref[0] + mod_ref[0][2:3, :] * y
    ms = jnp.mean(x2 * x2, axis=-1, keepdims=True)
    o_ref[0] = x2 * lax.rsqrt(ms + EPS) * fw_ref[...]


def _mlstm_out(hdir, og, x, mod3, mod_row, wo_bf, final_w):
    bsz, t, d = x.shape
    dm = og.shape[-1]
    tm = ROW_TILE
    tok = lambda b, i: (b, i, 0)
    return pl.pallas_call(
        _mlstm_out_kernel,
        out_shape=jax.ShapeDtypeStruct((bsz, t, d), F32),
        grid=(bsz, t // tm),
        in_specs=[pl.BlockSpec((2, 1, tm, dm), lambda b, i: (0, b, i, 0)),
                  pl.BlockSpec((1, tm, dm), tok),
                  pl.BlockSpec((1, tm, d), tok),
                  pl.BlockSpec((1, 3, d), lambda b, i: (mod_row(b), 0, 0)),
                  pl.BlockSpec(wo_bf.shape, lambda b, i: (0, 0)),
                  pl.BlockSpec((1, d), lambda b, i: (0, 0))],
        out_specs=pl.BlockSpec((1, tm, d), tok),
        compiler_params=_cparams(("parallel", "parallel")),
        name="mlstm_out",
    )(hdir, og, x, mod3, wo_bf, final_w.reshape(1, d))


def _rope_tables(t):
    nf = HEAD_DIM // 4
    pos = jnp.arange(t)
    row = (pos // GRID_W).astype(F32)
    col = (pos % GRID_W).astype(F32)
    inv = ROPE_BASE ** (-jnp.arange(nf, dtype=F32) / nf)
    ar = row[:, None] * inv[None, :]
    ac = col[:, None] * inv[None, :]
    cos = jnp.concatenate([jnp.cos(ar), jnp.cos(ar), jnp.cos(ac), jnp.cos(ac)], axis=1)
    sin = jnp.concatenate([-jnp.sin(ar), jnp.sin(ar), -jnp.sin(ac), jnp.sin(ac)], axis=1)
    reps = LANES // HEAD_DIM
    return jnp.tile(cos, (1, reps)), jnp.tile(sin, (1, reps))


def kernel(x_prompt, x_sample, cache_k, cache_v, state_C, state_n, state_m, c, c_ctx,
           attn_norm_w, attn_ada_w, attn_ada_b, attn_w_in, attn_sink, attn_w_out,
           mlstm_norm_w, mlstm_ada_w, mlstm_ada_b, mlstm_w_in, mlstm_b_gates, mlstm_w_out,
           final_norm_w):
    assert attn_w_in.shape[0] == 1 and mlstm_w_in.shape[0] == 1, "one layer of each mixer"
    bsz, seq, d = x_prompt.shape
    dbsz, dseq, _ = x_sample.shape
    dkv = N_KV_HEADS * HEAD_DIM
    dm = M_HEADS * M_HD

    n_cond = 1 + dbsz
    cond = jnp.concatenate([c_ctx[None, :], c, jnp.zeros((-n_cond % 8, d), F32)], axis=0)
    attn_mod = _ada(cond, attn_ada_w[0], attn_ada_b[0]).reshape(-1, 3, d)
    mlstm_mod = _ada(cond, mlstm_ada_w[0], mlstm_ada_b[0]).reshape(-1, 3, d)
    ctx_row = lambda b: 0
    lat_row = lambda b: b + 1

    attn_w_in_bf = attn_w_in[0].astype(BF16)
    attn_w_out_bf = attn_w_out[0].astype(BF16)
    attn_wvt_bf = attn_w_in[0, :, 2 * N_HEADS * HEAD_DIM + dkv:].T.astype(BF16)
    w_main_bf = mlstm_w_in[0, :, :5 * dm].astype(BF16)
    wg_bf = mlstm_w_in[0, :, 5 * dm:].astype(BF16)
    mlstm_w_out_bf = mlstm_w_out[0].astype(BF16)

    def mlstm_layer(x, mod_row, init, write_state):
        q, k, v, og, gc, gr = _mlstm_in(x, mlstm_mod, mod_row, mlstm_norm_w[0], w_main_bf, wg_bf,
                                        mlstm_b_gates[0])
        outs = _mlstm_scan(q, k, v, gc, gr, init, write_state)
        y = _mlstm_out(outs[0], og, x, mlstm_mod, mod_row, mlstm_w_out_bf, final_norm_w)
        return y, outs[1:]

    q, sg, k_ctx, vt_ctx, v_ctx = _attn_in(x_prompt, attn_mod, ctx_row, attn_norm_w[0], attn_w_in_bf,
                                           attn_wvt_bf, None, F32, True)
    x1 = _attn(q, sg, x_prompt, attn_mod, ctx_row, k_ctx, vt_ctx, None, None, attn_sink[0], attn_w_out_bf)
    y_prompt, (c_fin, n_fin, m_fin) = mlstm_layer(x1, ctx_row, None, True)

    q, sg, k_lat, vt_lat = _attn_in(x_sample, attn_mod, lat_row, attn_norm_w[0], attn_w_in_bf,
                                    attn_wvt_bf, _rope_tables(dseq), BF16, False)
    kc = cache_k[:, 0].reshape(dbsz, -1, dkv).astype(BF16)
    vct = jnp.swapaxes(cache_v[:, 0].reshape(dbsz, -1, dkv), 1, 2).astype(BF16)
    x1 = _attn(q, sg, x_sample, attn_mod, lat_row, kc, vct, k_lat, vt_lat, attn_sink[0], attn_w_out_bf)
    y_sample, _ = mlstm_layer(x1, lat_row, (state_C[:, 0], state_n[:, 0], state_m[:, 0]), False)

    new_cache_k = k_ctx.reshape(bsz, 1, seq, N_KV_HEADS, HEAD_DIM)
    new_cache_v = v_ctx.reshape(bsz, 1, seq, N_KV_HEADS, HEAD_DIM)
    return (y_prompt, y_sample, new_cache_k, new_cache_v,
            c_fin[:, None], n_fin[:, None], m_fin[:, None, :, :, 0])
```

```python
import functools

import jax
import jax.numpy as jnp
from jax import lax
from jax.experimental import pallas as pl
from jax.experimental.pallas import tpu as pltpu

F32 = jnp.float32
BF16 = jnp.bfloat16

HEAD_DIM = 64
N_KV_HEADS = 4
GROUP = 4
N_HEADS = N_KV_HEADS * GROUP
QBLK = 128
GRID_W = 64
ROPE_BASE = 10000.0
M_HEADS = 8
M_HD = 128
EPS = 1e-6

LANES = 128
VMEM_LIMIT = 48 * 1024 * 1024

MCHUNK = 128
ROW_TILE = 256

NEG_INF = float("-inf")
LOG2E = 1.4426950408889634


def _cparams(sem):
    return pltpu.CompilerParams(dimension_semantics=sem, vmem_limit_bytes=VMEM_LIMIT)


def _silu(x):
    return x * jax.nn.sigmoid(x)


def _log_sigmoid(x):
    return jnp.minimum(x, 0.0) - jnp.log1p(jnp.exp(-jnp.abs(x)))


def _dot(a, b):
    return jnp.dot(a, b, preferred_element_type=F32)


def _dot_nt(a, b):
    return lax.dot_general(a, b, (((1,), (1,)), ((), ())), preferred_element_type=F32)


def _dot_tn(a, b):
    return lax.dot_general(a, b, (((0,), (0,)), ((), ())), preferred_element_type=F32)


def _split3(x):
    hi = x.astype(BF16)
    r = x - hi.astype(F32)
    mid = r.astype(BF16)
    lo = (r - mid.astype(F32)).astype(BF16)
    return hi, mid, lo


def _prenorm(x, norm_w, mod):
    ms = jnp.mean(x * x, axis=-1, keepdims=True)
    y = x * lax.rsqrt(ms + EPS) * norm_w
    return y * (1.0 + mod[1:2, :]) + mod[0:1, :]


def _ada_kernel(cond_ref, w_ref, b_ref, o_ref):
    a = _silu(cond_ref[...]).astype(BF16)
    o_ref[...] = _dot(a, w_ref[...].astype(BF16)) + b_ref[...]


def _ada(cond8, w, b):
    d, n = w.shape
    tn = 512
    return pl.pallas_call(
        _ada_kernel,
        out_shape=jax.ShapeDtypeStruct((cond8.shape[0], n), F32),
        grid=(n // tn,),
        in_specs=[pl.BlockSpec(cond8.shape, lambda j: (0, 0)),
                  pl.BlockSpec((d, tn), lambda j: (0, j)),
                  pl.BlockSpec((1, tn), lambda j: (0, j))],
        out_specs=pl.BlockSpec((cond8.shape[0], tn), lambda j: (0, j)),
        compiler_params=_cparams(("parallel",)),
        name="ada_mod",
    )(cond8, w, b.reshape(1, n))


def _rope(x, cos, sin, lane):
    first = (lane & 31) < 16
    outs = []
    for c in range(x.shape[1] // LANES):
        xc = x[:, c * LANES:(c + 1) * LANES]
        sw = jnp.where(first, pltpu.roll(xc, LANES - 16, 1), pltpu.roll(xc, 16, 1))
        outs.append(xc * cos + sw * sin)
    return jnp.concatenate(outs, axis=1)


def _attn_in_kernel(*refs, rope, emit_v):
    refs = list(refs)
    x_ref, mod_ref, nw_ref, w_ref, wvt_ref = refs[:5]
    pos = 5
    if rope:
        cos_ref, sin_ref = refs[pos:pos + 2]
        pos += 2
    q_ref, sg_ref, k_ref, vt_ref = refs[pos:pos + 4]
    dq = q_ref.shape[-1]
    dkv = k_ref.shape[-1]
    hb = _prenorm(x_ref[0], nw_ref[...], mod_ref[0]).astype(BF16)
    q = _dot(hb, w_ref[:, 0:dq])
    g = _dot(hb, w_ref[:, dq:2 * dq])
    k = _dot(hb, w_ref[:, 2 * dq:2 * dq + dkv])
    if rope:
        cos = cos_ref[...]
        sin = sin_ref[...]
        lane = lax.broadcasted_iota(jnp.int32, cos.shape, 1)
        q = _rope(q, cos, sin, lane)
        k = _rope(k, cos, sin, lane)
    q_ref[0] = (q * (HEAD_DIM ** -0.5 * LOG2E)).astype(q_ref.dtype)
    sg_ref[0] = _silu(g).astype(sg_ref.dtype)
    k_ref[0] = k.astype(k_ref.dtype)
    vt_ref[0] = _dot_nt(wvt_ref[...], hb).astype(vt_ref.dtype)
    if emit_v:
        v_ref = refs[pos + 4]
        v_ref[0] = _dot(hb, w_ref[:, 2 * dq + dkv:2 * dq + 2 * dkv]).astype(v_ref.dtype)


def _attn_in(x, mod3, mod_row, norm_w, w_bf, wvt_bf, rope_tabs, k_dtype, emit_v):
    bsz, t, d = x.shape
    dq = N_HEADS * HEAD_DIM
    dkv = N_KV_HEADS * HEAD_DIM
    tm = ROW_TILE
    rope = rope_tabs is not None
    tok = lambda b, i: (b, i, 0)
    const = lambda b, i: (0, 0)
    in_specs = [pl.BlockSpec((1, tm, d), tok),
                pl.BlockSpec((1, 3, d), lambda b, i: (mod_row(b), 0, 0)),
                pl.BlockSpec((1, d), const),
                pl.BlockSpec(w_bf.shape, const),
                pl.BlockSpec(wvt_bf.shape, const)]
    args = [x, mod3, norm_w.reshape(1, d), w_bf, wvt_bf]
    if rope:
        in_specs += [pl.BlockSpec((tm, LANES), lambda b, i: (i, 0))] * 2
        args += list(rope_tabs)
    out_shape = [jax.ShapeDtypeStruct((bsz, t, dq), BF16),
                 jax.ShapeDtypeStruct((bsz, t, dq), BF16),
                 jax.ShapeDtypeStruct((bsz, t, dkv), k_dtype),
                 jax.ShapeDtypeStruct((bsz, dkv, t), BF16)]
    out_specs = [pl.BlockSpec((1, tm, dq), tok), pl.BlockSpec((1, tm, dq), tok),
                 pl.BlockSpec((1, tm, dkv), tok),
                 pl.BlockSpec((1, dkv, tm), lambda b, i: (b, 0, i))]
    if emit_v:
        out_shape.append(jax.ShapeDtypeStruct((bsz, t, dkv), F32))
        out_specs.append(pl.BlockSpec((1, tm, dkv), tok))
    return pl.pallas_call(
        functools.partial(_attn_in_kernel, rope=rope, emit_v=emit_v),
        out_shape=tuple(out_shape),
        grid=(bsz, t // tm),
        in_specs=in_specs,
        out_specs=tuple(out_specs),
        compiler_params=_cparams(("parallel", "parallel")),
        name="attn_in_rope" if rope else "attn_in",
    )(*args)


def _attn_kernel(*refs, window, nb):
    if window:
        (q_ref, sg_ref, x_ref, mod_ref, kc_ref, vct_ref, kp_ref, kq_ref, kn_ref,
         vpt_ref, vqt_ref, vnt_ref, sink_ref, wo_ref, o_ref, s_scr, ot_scr) = refs
    else:
        q_ref, sg_ref, x_ref, mod_ref, kc_ref, vct_ref, sink_ref, wo_ref, o_ref, s_scr, ot_scr = refs
    blk = pl.program_id(1)
    q = q_ref[0]
    n_ctx = kc_ref.shape[1] // QBLK
    cols = GROUP * QBLK
    if window:
        kj = lax.broadcasted_iota(jnp.int32, (QBLK, cols), 0)
        qi = lax.broadcasted_iota(jnp.int32, (QBLK, cols), 1) & (QBLK - 1)
        prev_ok = (kj >= qi) & (blk > 0)
        next_ok = (kj <= qi) & (blk < nb - 1)
    ones_rows = jnp.where(lax.broadcasted_iota(jnp.int32, (16, QBLK), 0) == 0, 1.0, 0.0).astype(BF16)
    for kvh in range(N_KV_HEADS):
        cs = slice(kvh * HEAD_DIM, (kvh + 1) * HEAD_DIM)
        heads = [kvh * GROUP + j for j in range(GROUP)]
        q4 = jnp.concatenate([q[:, h * HEAD_DIM:(h + 1) * HEAD_DIM] for h in heads], axis=0)
        sink_row = jnp.concatenate(
            [jnp.broadcast_to(sink_ref[0:1, h:h + 1], (1, QBLK)) for h in heads], axis=1) * LOG2E
        blocks = [(kc_ref[0, j * QBLK:(j + 1) * QBLK, cs].astype(BF16),
                   vct_ref[0, cs, j * QBLK:(j + 1) * QBLK], None) for j in range(n_ctx)]
        if window:
            blocks += [(kp_ref[0][:, cs], vpt_ref[0, cs, :], prev_ok),
                       (kq_ref[0][:, cs], vqt_ref[0, cs, :], None),
                       (kn_ref[0][:, cs], vnt_ref[0, cs, :], next_ok)]
        macc = jnp.full((8, cols), NEG_INF, F32)
        for j, (kb, _, ok) in enumerate(blocks):
            st = _dot_nt(kb, q4)
            if ok is not None:
                st = jnp.where(ok, st, NEG_INF)
            s_scr[j] = st
            macc = jnp.maximum(macc, jnp.max(st.reshape(QBLK // 8, 8, cols), axis=0))
        m_row = jnp.maximum(jnp.max(macc, axis=0, keepdims=True), sink_row)
        acc = jnp.zeros((HEAD_DIM + 16, cols), F32)
        for j, (_, vt, _) in enumerate(blocks):
            p = jnp.exp2(s_scr[j] - m_row).astype(BF16)
            acc = acc + _dot(jnp.concatenate([vt, ones_rows], axis=0), p)
        den = acc[HEAD_DIM:HEAD_DIM + 1, :] + jnp.exp2(sink_row - m_row)
        o_t = acc[0:HEAD_DIM, :] / den
        for j, h in enumerate(heads):
            ot_scr[h * HEAD_DIM:(h + 1) * HEAD_DIM, :] = o_t[:, j * QBLK:(j + 1) * QBLK]
    z = (ot_scr[...].T * sg_ref[0].astype(F32)).astype(BF16)
    y = _dot(z, wo_ref[...])
    o_ref[0] = x_ref[0] + mod_ref[0][2:3, :] * y


def _attn(q, sg, x, mod3, mod_row, kc, vct, k_lat, vt_lat, sink, wo_bf):
    bsz, t, d = x.shape
    dq = q.shape[-1]
    dkv = kc.shape[-1]
    p_len = kc.shape[1]
    nb = t // QBLK
    window = k_lat is not None
    tok = lambda b, i: (b, i, 0)
    in_specs = [pl.BlockSpec((1, QBLK, dq), tok),
                pl.BlockSpec((1, QBLK, dq), tok),
                pl.BlockSpec((1, QBLK, d), tok),
                pl.BlockSpec((1, 3, d), lambda b, i: (mod_row(b), 0, 0)),
                pl.BlockSpec((1, p_len, dkv), lambda b, i: (b, 0, 0)),
                pl.BlockSpec((1, dkv, p_len), lambda b, i: (b, 0, 0))]
    args = [q, sg, x, mod3, kc, vct]
    n_blocks = p_len // QBLK
    if window:
        prev = lambda i: jnp.maximum(i - 1, 0)
        nxt = lambda i: jnp.minimum(i + 1, nb - 1)
        in_specs += [pl.BlockSpec((1, QBLK, dkv), lambda b, i: (b, prev(i), 0)),
                     pl.BlockSpec((1, QBLK, dkv), tok),
                     pl.BlockSpec((1, QBLK, dkv), lambda b, i: (b, nxt(i), 0)),
                     pl.BlockSpec((1, dkv, QBLK), lambda b, i: (b, 0, prev(i))),
                     pl.BlockSpec((1, dkv, QBLK), lambda b, i: (b, 0, i)),
                     pl.BlockSpec((1, dkv, QBLK), lambda b, i: (b, 0, nxt(i)))]
        args += [k_lat] * 3 + [vt_lat] * 3
        n_blocks += 3
    in_specs += [pl.BlockSpec((1, N_HEADS), lambda b, i: (0, 0)),
                 pl.BlockSpec(wo_bf.shape, lambda b, i: (0, 0))]
    args += [sink.reshape(1, N_HEADS), wo_bf]
    return pl.pallas_call(
        functools.partial(_attn_kernel, window=window, nb=nb),
        out_shape=jax.ShapeDtypeStruct((bsz, t, d), F32),
        grid=(bsz, nb),
        in_specs=in_specs,
        out_specs=pl.BlockSpec((1, QBLK, d), tok),
        scratch_shapes=[pltpu.VMEM((n_blocks, QBLK, GROUP * QBLK), F32),
                        pltpu.VMEM((dq, QBLK), F32)],
        compiler_params=_cparams(("parallel", "parallel")),
        name="attn_window" if window else "attn_ctx",
    )(*args)


def _mlstm_in_kernel(x_ref, mod_ref, nw_ref, w_ref, wvt_ref, wg_ref, wgt_ref, bg_ref, bgt_ref,
                     q_ref, k_ref, vt_ref, og_ref, gc_ref, gr_ref):
    dm = q_ref.shape[-1]
    nh = M_HEADS
    L = MCHUNK
    hb = _prenorm(x_ref[0], nw_ref[...], mod_ref[0]).astype(BF16)
    q_ref[0] = _dot(hb, w_ref[:, 0:dm]).astype(q_ref.dtype)
    k_ref[0] = (_dot(hb, w_ref[:, dm:2 * dm]) * (M_HD ** -0.5)).astype(k_ref.dtype)
    vt_ref[0] = _dot_nt(wvt_ref[...], hb).astype(vt_ref.dtype)
    o = _dot(hb, w_ref[:, 2 * dm:3 * dm])
    g = _dot(hb, w_ref[:, 3 * dm:4 * dm])
    og_ref[0] = (jax.nn.sigmoid(o) * _silu(g)).astype(og_ref.dtype)

    gc = _dot(hb, wg_ref[...]) + bg_ref[...]
    gr = _dot_nt(wgt_ref[...], hb) + bgt_ref[...]
    ri = lax.broadcasted_iota(jnp.int32, (L, L), 0)
    ci = lax.broadcasted_iota(jnp.int32, (L, L), 1)
    for dr in range(2):
        seen = (ci <= ri) if dr == 0 else (ci >= ri)
        tri = jnp.where(seen, 1.0, 0.0).astype(BF16)
        base = dr * 2 * nh
        for cidx in range(x_ref.shape[1] // L):
            rows = slice(cidx * L, (cidx + 1) * L)
            lf_c = _log_sigmoid(gc[rows, base + nh:base + 2 * nh])
            lf_r = _log_sigmoid(gr[base + nh:base + 2 * nh, rows])
            b_c = sum(_dot(tri, piece) for piece in _split3(lf_c))
            b_r = sum(_dot_nt(piece, tri) for piece in _split3(lf_r))
            g_r = gr[base:base + nh, rows] - b_r
            b_last = jnp.sum(lf_r, axis=1, keepdims=True)
            g_max = jnp.max(g_r, axis=1, keepdims=True)
            gc_ref[0, dr, rows, :] = gc[rows, base:base + nh] - b_c
            gr_ref[0, dr, :, rows] = jnp.concatenate(
                [g_r, b_r, jnp.broadcast_to(b_last, (nh, L)), jnp.broadcast_to(g_max, (nh, L))], axis=0)


def _mlstm_in(x, mod3, mod_row, norm_w, w_main_bf, wvt_bf, wg_bf, b_gates):
    bsz, t, d = x.shape
    dm = M_HEADS * M_HD
    ng = 4 * M_HEADS
    tm = ROW_TILE
    tok = lambda b, i: (b, i, 0)
    const = lambda b, i: (0, 0)
    big = jax.ShapeDtypeStruct((bsz, t, dm), BF16)
    return pl.pallas_call(
        _mlstm_in_kernel,
        out_shape=(big, big, jax.ShapeDtypeStruct((bsz, dm, t), BF16), big,
                   jax.ShapeDtypeStruct((bsz, 2, t, M_HEADS), F32),
                   jax.ShapeDtypeStruct((bsz, 2, 4 * M_HEADS, t), F32)),
        grid=(bsz, t // tm),
        in_specs=[pl.BlockSpec((1, tm, d), tok),
                  pl.BlockSpec((1, 3, d), lambda b, i: (mod_row(b), 0, 0)),
                  pl.BlockSpec((1, d), const),
                  pl.BlockSpec(w_main_bf.shape, const),
                  pl.BlockSpec(wvt_bf.shape, const),
                  pl.BlockSpec((d, ng), const),
                  pl.BlockSpec((ng, d), const),
                  pl.BlockSpec((1, ng), const),
                  pl.BlockSpec((ng, 1), const)],
        out_specs=(pl.BlockSpec((1, tm, dm), tok), pl.BlockSpec((1, tm, dm), tok),
                   pl.BlockSpec((1, dm, tm), lambda b, i: (b, 0, i)), pl.BlockSpec((1, tm, dm), tok),
                   pl.BlockSpec((1, 2, tm, M_HEADS), lambda b, i: (b, 0, i, 0)),
                   pl.BlockSpec((1, 2, 4 * M_HEADS, tm), lambda b, i: (b, 0, 0, i))),
        compiler_params=_cparams(("parallel", "parallel")),
        name="mlstm_in",
    )(x, mod3, norm_w.reshape(1, d), w_main_bf, wvt_bf, wg_bf, wg_bf.T,
      b_gates.reshape(1, ng), b_gates.reshape(ng, 1))


def _mlstm_scan_kernel(*refs, has_init, write_state, nc):
    refs = list(refs)
    q_ref, k_ref, vt_ref, gc_ref, gr_ref = refs[:5]
    pos = 5
    if has_init:
        c0_ref, n0_ref, m0_ref = refs[pos:pos + 3]
        pos += 3
    ht_ref = refs[pos]
    pos += 1
    if write_state:
        cout_ref, nout_ref, mout_ref = refs[pos:pos + 3]
        pos += 3
    ct_scr, mscr = refs[pos:pos + 2]

    drn = pl.program_id(1)
    c = pl.program_id(2)
    L = q_ref.shape[1]
    nh = M_HEADS
    pad = ct_scr.shape[1] - M_HD

    @pl.when(c == 0)
    def _init():
        if has_init:
            for h in range(nh):
                ct_scr[h, 0:M_HD, :] = c0_ref[0, 0, h].T
                ct_scr[h, M_HD:M_HD + pad, :] = jnp.concatenate(
                    [n0_ref[0, 0, h:h + 1, :], jnp.zeros((pad - 1, M_HD), F32)], axis=0)
            mscr[...] = m0_ref[0, 0]
        else:
            ct_scr[...] = jnp.zeros(ct_scr.shape, F32)
            mscr[...] = jnp.zeros(mscr.shape, F32)

    si = lax.broadcasted_iota(jnp.int32, (L, L), 0)
    li = lax.broadcasted_iota(jnp.int32, (L, L), 1)
    seen_t = (si - li) * (1 - 2 * drn) <= 0

    gcb = gc_ref[0, 0]
    grb = gr_ref[0, 0]
    q = q_ref[0]
    k = k_ref[0]
    vt = vt_ref[0]
    ones_rows = jnp.where(lax.broadcasted_iota(jnp.int32, (pad, L), 0) == 0, 1.0, 0.0).astype(BF16)

    for h in range(nh):
        hs = slice(h * M_HD, (h + 1) * M_HD)
        qh, kh = q[:, hs], k[:, hs]
        vext = jnp.concatenate([vt[hs, :], ones_rows], axis=0)
        g_r = grb[h:h + 1, :]
        b_r = grb[nh + h:nh + h + 1, :]
        b_last = grb[2 * nh + h:2 * nh + h + 1, 0:1]
        g_max = grb[3 * nh + h:3 * nh + h + 1, 0:1]
        m_prev = mscr[h:h + 1, 0:1]
        ct = ct_scr[h]

        a_t = jnp.where(seen_t, gcb[:, h:h + 1], NEG_INF)
        m_row = jnp.maximum(jnp.max(a_t, axis=0, keepdims=True), m_prev)
        w_t = jnp.exp(a_t - m_row)
        r1 = _dot_nt(jnp.concatenate([kh, ct.astype(BF16)], axis=0), qh)
        s_t = (r1[0:L, :] * w_t).astype(BF16)
        w0 = jnp.exp(m_prev - m_row)
        tot = _dot(vext, s_t) + w0 * r1[L:, :]
        den = tot[M_HD:M_HD + 1, :]
        floor = jnp.exp(-(b_r + m_row))
        ht_ref[0, 0, hs, :] = tot[0:M_HD, :] / jnp.maximum(jnp.abs(den), floor)

        m_last = jnp.maximum(g_max, m_prev)
        wk = jnp.exp(g_r - m_last)
        decay = jnp.exp(m_prev - m_last)
        vw = (vext.astype(F32) * wk).astype(BF16)
        ct_scr[h] = decay * ct + _dot(vw, kh)
        mscr[h:h + 1, :] = jnp.broadcast_to(b_last + m_last, (1, LANES))

    if write_state:
        @pl.when(c == nc - 1)
        def _final():
            for h in range(nh):
                cfin = ct_scr[h]
                cout_ref[0, 0, h] = cfin[0:M_HD, :].T
                nout_ref[0, 0, h:h + 1, :] = cfin[M_HD:M_HD + 1, :]
            mout_ref[0, 0] = mscr[...]


def _mlstm_scan(q, k, vt, gc, gr, init, write_state):
    bsz, t, dm = q.shape
    L = MCHUNK
    nc = t // L
    chunk = lambda b, d, c: c + d * (nc - 1 - 2 * c)
    tok = lambda b, d, c: (b, chunk(b, d, c), 0)
    in_specs = [pl.BlockSpec((1, L, dm), tok)] * 2 + [
        pl.BlockSpec((1, dm, L), lambda b, d, c: (b, 0, chunk(b, d, c))),
        pl.BlockSpec((1, 1, L, gc.shape[-1]), lambda b, d, c: (b, d, chunk(b, d, c), 0)),
        pl.BlockSpec((1, 1, gr.shape[2], L), lambda b, d, c: (b, d, 0, chunk(b, d, c)))]
    args = [q, k, vt, gc, gr]
    st = lambda b, d, c: (b, d, 0, 0)
    st5 = lambda b, d, c: (b, d, 0, 0, 0)
    if init is not None:
        c0, n0, m0 = init
        in_specs += [pl.BlockSpec((1, 1, M_HEADS, M_HD, M_HD), st5),
                     pl.BlockSpec((1, 1, M_HEADS, M_HD), st),
                     pl.BlockSpec((1, 1, M_HEADS, LANES), st)]
        args += [c0, n0, jnp.broadcast_to(m0[..., None], m0.shape + (LANES,))]
    out_shape = [jax.ShapeDtypeStruct((2, bsz, dm, t), F32)]
    out_specs = [pl.BlockSpec((1, 1, dm, L), lambda b, d, c: (d, b, 0, chunk(b, d, c)))]
    if write_state:
        out_shape += [jax.ShapeDtypeStruct((bsz, 2, M_HEADS, M_HD, M_HD), F32),
                      jax.ShapeDtypeStruct((bsz, 2, M_HEADS, M_HD), F32),
                      jax.ShapeDtypeStruct((bsz, 2, M_HEADS, LANES), F32)]
        out_specs += [pl.BlockSpec((1, 1, M_HEADS, M_HD, M_HD), st5),
                      pl.BlockSpec((1, 1, M_HEADS, M_HD), st),
                      pl.BlockSpec((1, 1, M_HEADS, LANES), st)]
    return pl.pallas_call(
        functools.partial(_mlstm_scan_kernel, has_init=init is not None,
                          write_state=write_state, nc=nc),
        out_shape=tuple(out_shape),
        grid=(bsz, 2, nc),
        in_specs=in_specs,
        out_specs=tuple(out_specs),
        scratch_shapes=[pltpu.VMEM((M_HEADS, M_HD + 16, M_HD), F32),
                        pltpu.VMEM((M_HEADS, LANES), F32)],
        compiler_params=_cparams(("parallel", "parallel", "arbitrary")),
        name="mlstm_scan",
    )(*args)


def _mlstm_out_kernel(ht_ref, og_ref, x_ref, mod_ref, wo_ref, fw_ref, o_ref):
    hm = (ht_ref[0, 0] + ht_ref[1, 0]).T * og_ref[0].astype(F32)
    y = _dot(hm.astype(BF16), wo_ref[...])
    x2 = x_ref[0] + mod_ref[0][2:3, :] * y
    ms = jnp.mean(x2 * x2, axis=-1, keepdims=True)
    o_ref[0] = x2 * lax.rsqrt(ms + EPS) * fw_ref[...]


def _mlstm_out(hdir, og, x, mod3, mod_row, wo_bf, final_w):
    bsz, t, d = x.shape
    dm = og.shape[-1]
    tm = ROW_TILE
    tok = lambda b, i: (b, i, 0)
    return pl.pallas_call(
        _mlstm_out_kernel,
        out_shape=jax.ShapeDtypeStruct((bsz, t, d), F32),
        grid=(bsz, t // tm),
        in_specs=[pl.BlockSpec((2, 1, dm, tm), lambda b, i: (0, b, 0, i)),
                  pl.BlockSpec((1, tm, dm), tok),
                  pl.BlockSpec((1, tm, d), tok),
                  pl.BlockSpec((1, 3, d), lambda b, i: (mod_row(b), 0, 0)),
                  pl.BlockSpec(wo_bf.shape, lambda b, i: (0, 0)),
                  pl.BlockSpec((1, d), lambda b, i: (0, 0))],
        out_specs=pl.BlockSpec((1, tm, d), tok),
        compiler_params=_cparams(("parallel", "parallel")),
        name="mlstm_out",
    )(hdir, og, x, mod3, wo_bf, final_w.reshape(1, d))


def _rope_tables(t):
    nf = HEAD_DIM // 4
    pos = jnp.arange(t)
    row = (pos // GRID_W).astype(F32)
    col = (pos % GRID_W).astype(F32)
    inv = ROPE_BASE ** (-jnp.arange(nf, dtype=F32) / nf)
    ar = row[:, None] * inv[None, :]
    ac = col[:, None] * inv[None, :]
    cos = jnp.concatenate([jnp.cos(ar), jnp.cos(ar), jnp.cos(ac), jnp.cos(ac)], axis=1)
    sin = jnp.concatenate([-jnp.sin(ar), jnp.sin(ar), -jnp.sin(ac), jnp.sin(ac)], axis=1)
    reps = LANES // HEAD_DIM
    return jnp.tile(cos, (1, reps)), jnp.tile(sin, (1, reps))


def kernel(x_prompt, x_sample, cache_k, cache_v, state_C, state_n, state_m, c, c_ctx,
           attn_norm_w, attn_ada_w, attn_ada_b, attn_w_in, attn_sink, attn_w_out,
           mlstm_norm_w, mlstm_ada_w, mlstm_ada_b, mlstm_w_in, mlstm_b_gates, mlstm_w_out,
           final_norm_w):
    assert attn_w_in.shape[0] == 1 and mlstm_w_in.shape[0] == 1, "one layer of each mixer"
    bsz, seq, d = x_prompt.shape
    dbsz, dseq, _ = x_sample.shape
    dkv = N_KV_HEADS * HEAD_DIM
    dm = M_HEADS * M_HD

    n_cond = 1 + dbsz
    cond = jnp.concatenate([c_ctx[None, :], c, jnp.zeros((-n_cond % 8, d), F32)], axis=0)
    attn_mod = _ada(cond, attn_ada_w[0], attn_ada_b[0]).reshape(-1, 3, d)
    mlstm_mod = _ada(cond, mlstm_ada_w[0], mlstm_ada_b[0]).reshape(-1, 3, d)
    ctx_row = lambda b: 0
    lat_row = lambda b: b + 1

    attn_w_in_bf = attn_w_in[0].astype(BF16)
    attn_w_out_bf = attn_w_out[0].astype(BF16)
    attn_wvt_bf = attn_w_in[0, :, 2 * N_HEADS * HEAD_DIM + dkv:].T.astype(BF16)
    w_main_bf = jnp.concatenate([mlstm_w_in[0, :, :2 * dm], mlstm_w_in[0, :, 3 * dm:5 * dm]],
                                axis=1).astype(BF16)
    mlstm_wvt_bf = mlstm_w_in[0, :, 2 * dm:3 * dm].T.astype(BF16)
    wg_bf = mlstm_w_in[0, :, 5 * dm:].astype(BF16)
    mlstm_w_out_bf = mlstm_w_out[0].astype(BF16)

    def mlstm_layer(x, mod_row, init, write_state):
        q, k, vt, og, gc, gr = _mlstm_in(x, mlstm_mod, mod_row, mlstm_norm_w[0], w_main_bf,
                                         mlstm_wvt_bf, wg_bf, mlstm_b_gates[0])
        outs = _mlstm_scan(q, k, vt, gc, gr, init, write_state)
        y = _mlstm_out(outs[0], og, x, mlstm_mod, mod_row, mlstm_w_out_bf, final_norm_w)
        return y, outs[1:]

    q, sg, k_ctx, vt_ctx, v_ctx = _attn_in(x_prompt, attn_mod, ctx_row, attn_norm_w[0], attn_w_in_bf,
                                           attn_wvt_bf, None, F32, True)
    x1 = _attn(q, sg, x_prompt, attn_mod, ctx_row, k_ctx, vt_ctx, None, None, attn_sink[0], attn_w_out_bf)
    y_prompt, (c_fin, n_fin, m_fin) = mlstm_layer(x1, ctx_row, None, True)

    q, sg, k_lat, vt_lat = _attn_in(x_sample, attn_mod, lat_row, attn_norm_w[0], attn_w_in_bf,
                                    attn_wvt_bf, _rope_tables(dseq), BF16, False)
    kc = cache_k[:, 0].reshape(dbsz, -1, dkv).astype(BF16)
    vct = jnp.swapaxes(cache_v[:, 0].reshape(dbsz, -1, dkv), 1, 2).astype(BF16)
    x1 = _attn(q, sg, x_sample, attn_mod, lat_row, kc, vct, k_lat, vt_lat, attn_sink[0], attn_w_out_bf)
    y_sample, _ = mlstm_layer(x1, lat_row, (state_C[:, 0], state_n[:, 0], state_m[:, 0]), False)

    new_cache_k = k_ctx.reshape(bsz, 1, seq, N_KV_HEADS, HEAD_DIM)
    new_cache_v = v_ctx.reshape(bsz, 1, seq, N_KV_HEADS, HEAD_DIM)
    return (y_prompt, y_sample, new_cache_k, new_cache_v,
            c_fin[:, None], n_fin[:, None], m_fin[:, None, :, :, 0])
```

```python
import functools

import jax
import jax.numpy as jnp
from jax import lax
from jax.experimental import pallas as pl
from jax.experimental.pallas import tpu as pltpu

F32 = jnp.float32
BF16 = jnp.bfloat16

HEAD_DIM = 64
N_KV_HEADS = 4
GROUP = 4
N_HEADS = N_KV_HEADS * GROUP
QBLK = 128
GRID_W = 64
ROPE_BASE = 10000.0
M_HEADS = 8
M_HD = 128
EPS = 1e-6

LANES = 128
VMEM_LIMIT = 48 * 1024 * 1024

MCHUNK = 128
SCAN_AHEAD = 4
ROW_TILE = 256

NEG_INF = float("-inf")
LOG2E = 1.4426950408889634


def _cparams(sem):
    return pltpu.CompilerParams(dimension_semantics=sem, vmem_limit_bytes=VMEM_LIMIT)


def _silu(x):
    return x * jax.nn.sigmoid(x)


def _log_sigmoid(x):
    return jnp.minimum(x, 0.0) - jnp.log1p(jnp.exp(-jnp.abs(x)))


def _dot(a, b):
    return jnp.dot(a, b, preferred_element_type=F32)


def _dot_nt(a, b):
    return lax.dot_general(a, b, (((1,), (1,)), ((), ())), preferred_element_type=F32)


def _dot_tn(a, b):
    return lax.dot_general(a, b, (((0,), (0,)), ((), ())), preferred_element_type=F32)


def _split3(x):
    hi = x.astype(BF16)
    r = x - hi.astype(F32)
    mid = r.astype(BF16)
    lo = (r - mid.astype(F32)).astype(BF16)
    return hi, mid, lo


def _prenorm(x, norm_w, mod):
    ms = jnp.mean(x * x, axis=-1, keepdims=True)
    y = x * lax.rsqrt(ms + EPS) * norm_w
    return y * (1.0 + mod[1:2, :]) + mod[0:1, :]


def _ada_kernel(cond_ref, w_ref, b_ref, o_ref):
    a = _silu(cond_ref[...]).astype(BF16)
    o_ref[...] = _dot(a, w_ref[...].astype(BF16)) + b_ref[...]


def _ada(cond8, w, b):
    d, n = w.shape
    tn = 512
    return pl.pallas_call(
        _ada_kernel,
        out_shape=jax.ShapeDtypeStruct((cond8.shape[0], n), F32),
        grid=(n // tn,),
        in_specs=[pl.BlockSpec(cond8.shape, lambda j: (0, 0)),
                  pl.BlockSpec((d, tn), lambda j: (0, j)),
                  pl.BlockSpec((1, tn), lambda j: (0, j))],
        out_specs=pl.BlockSpec((cond8.shape[0], tn), lambda j: (0, j)),
        compiler_params=_cparams(("parallel",)),
        name="ada_mod",
    )(cond8, w, b.reshape(1, n))


def _rope(x, cos, sin, lane):
    first = (lane & 31) < 16
    outs = []
    for c in range(x.shape[1] // LANES):
        xc = x[:, c * LANES:(c + 1) * LANES]
        sw = jnp.where(first, pltpu.roll(xc, LANES - 16, 1), pltpu.roll(xc, 16, 1))
        outs.append(xc * cos + sw * sin)
    return jnp.concatenate(outs, axis=1)


def _attn_in_kernel(*refs, rope, emit_v):
    refs = list(refs)
    x_ref, mod_ref, nw_ref, w_ref, wvt_ref = refs[:5]
    pos = 5
    if rope:
        cos_ref, sin_ref = refs[pos:pos + 2]
        pos += 2
    q_ref, sg_ref, k_ref, vt_ref = refs[pos:pos + 4]
    dq = q_ref.shape[-1]
    dkv = k_ref.shape[-1]
    hb = _prenorm(x_ref[0], nw_ref[...], mod_ref[0]).astype(BF16)
    q = _dot(hb, w_ref[:, 0:dq])
    g = _dot(hb, w_ref[:, dq:2 * dq])
    k = _dot(hb, w_ref[:, 2 * dq:2 * dq + dkv])
    if rope:
        cos = cos_ref[...]
        sin = sin_ref[...]
        lane = lax.broadcasted_iota(jnp.int32, cos.shape, 1)
        q = _rope(q, cos, sin, lane)
        k = _rope(k, cos, sin, lane)
    q_ref[0] = (q * (HEAD_DIM ** -0.5 * LOG2E)).astype(q_ref.dtype)
    sg_ref[0] = _silu(g).astype(sg_ref.dtype)
    k_ref[0] = k.astype(k_ref.dtype)
    vt_ref[0] = _dot_nt(wvt_ref[...], hb).astype(vt_ref.dtype)
    if emit_v:
        v_ref = refs[pos + 4]
        v_ref[0] = _dot(hb, w_ref[:, 2 * dq + dkv:2 * dq + 2 * dkv]).astype(v_ref.dtype)


def _attn_in(x, mod3, mod_row, norm_w, w_bf, wvt_bf, rope_tabs, k_dtype, emit_v):
    bsz, t, d = x.shape
    dq = N_HEADS * HEAD_DIM
    dkv = N_KV_HEADS * HEAD_DIM
    tm = ROW_TILE
    rope = rope_tabs is not None
    tok = lambda b, i: (b, i, 0)
    const = lambda b, i: (0, 0)
    in_specs = [pl.BlockSpec((1, tm, d), tok),
                pl.BlockSpec((1, 3, d), lambda b, i: (mod_row(b), 0, 0)),
                pl.BlockSpec((1, d), const),
                pl.BlockSpec(w_bf.shape, const),
                pl.BlockSpec(wvt_bf.shape, const)]
    args = [x, mod3, norm_w.reshape(1, d), w_bf, wvt_bf]
    if rope:
        in_specs += [pl.BlockSpec((tm, LANES), lambda b, i: (i, 0))] * 2
        args += list(rope_tabs)
    out_shape = [jax.ShapeDtypeStruct((bsz, t, dq), BF16),
                 jax.ShapeDtypeStruct((bsz, t, dq), BF16),
                 jax.ShapeDtypeStruct((bsz, t, dkv), k_dtype),
                 jax.ShapeDtypeStruct((bsz, dkv, t), BF16)]
    out_specs = [pl.BlockSpec((1, tm, dq), tok), pl.BlockSpec((1, tm, dq), tok),
                 pl.BlockSpec((1, tm, dkv), tok),
                 pl.BlockSpec((1, dkv, tm), lambda b, i: (b, 0, i))]
    if emit_v:
        out_shape.append(jax.ShapeDtypeStruct((bsz, t, dkv), F32))
        out_specs.append(pl.BlockSpec((1, tm, dkv), tok))
    return pl.pallas_call(
        functools.partial(_attn_in_kernel, rope=rope, emit_v=emit_v),
        out_shape=tuple(out_shape),
        grid=(bsz, t // tm),
        in_specs=in_specs,
        out_specs=tuple(out_specs),
        compiler_params=_cparams(("parallel", "parallel")),
        name="attn_in_rope" if rope else "attn_in",
    )(*args)


def _attn_kernel(*refs, window, nb):
    if window:
        (q_ref, sg_ref, x_ref, mod_ref, kc_ref, vct_ref, kp_ref, kq_ref, kn_ref,
         vpt_ref, vqt_ref, vnt_ref, sink_ref, wo_ref, o_ref, s_scr, p_scr, ot_scr) = refs
    else:
        q_ref, sg_ref, x_ref, mod_ref, kc_ref, vct_ref, sink_ref, wo_ref, o_ref, s_scr, p_scr, ot_scr = refs
    blk = pl.program_id(1)
    q = q_ref[0]
    n_ctx = kc_ref.shape[1] // QBLK
    cols = GROUP * QBLK
    if window:
        kj = lax.broadcasted_iota(jnp.int32, (QBLK, cols), 0)
        qi = lax.broadcasted_iota(jnp.int32, (QBLK, cols), 1) & (QBLK - 1)
        prev_ok = (kj >= qi) & (blk > 0)
        next_ok = (kj <= qi) & (blk < nb - 1)
    ones_rows = jnp.where(lax.broadcasted_iota(jnp.int32, (16, QBLK), 0) == 0, 1.0, 0.0).astype(BF16)
    masks = [None] * n_ctx + ([prev_ok, None, next_ok] if window else [])
    n_blk = len(masks)

    def scores(kvh):
        cs = slice(kvh * HEAD_DIM, (kvh + 1) * HEAD_DIM)
        heads = [kvh * GROUP + j for j in range(GROUP)]
        q4 = jnp.concatenate([q[:, h * HEAD_DIM:(h + 1) * HEAD_DIM] for h in heads], axis=0)
        sink_row = jnp.concatenate(
            [jnp.broadcast_to(sink_ref[0:1, h:h + 1], (1, QBLK)) for h in heads], axis=1) * LOG2E
        keys = [kc_ref[0, j * QBLK:(j + 1) * QBLK, cs].astype(BF16) for j in range(n_ctx)]
        if window:
            keys += [kp_ref[0][:, cs], kq_ref[0][:, cs], kn_ref[0][:, cs]]
        st_all = _dot_nt(jnp.concatenate(keys, axis=0), q4)
        macc = jnp.full((8, cols), NEG_INF, F32)
        for j, ok in enumerate(masks):
            st = st_all[j * QBLK:(j + 1) * QBLK, :]
            if ok is not None:
                st = jnp.where(ok, st, NEG_INF)
            s_scr[kvh, j] = st
            macc = jnp.maximum(macc, jnp.max(st.reshape(QBLK // 8, 8, cols), axis=0))
        return jnp.maximum(jnp.max(macc, axis=0, keepdims=True), sink_row), sink_row

    def weighted_values(kvh, m_row, sink_row):
        cs = slice(kvh * HEAD_DIM, (kvh + 1) * HEAD_DIM)
        for j in range(n_blk):
            p_scr[kvh, j * QBLK:(j + 1) * QBLK, :] = jnp.exp2(s_scr[kvh, j] - m_row).astype(BF16)
        vts = [vct_ref[0, cs, j * QBLK:(j + 1) * QBLK] for j in range(n_ctx)]
        if window:
            vts += [vpt_ref[0, cs, :], vqt_ref[0, cs, :], vnt_ref[0, cs, :]]
        vt_ext = jnp.concatenate(
            [jnp.concatenate(vts, axis=1), jnp.tile(ones_rows, (1, n_blk))], axis=0)
        acc = _dot(vt_ext, p_scr[kvh])
        den = acc[HEAD_DIM:HEAD_DIM + 1, :] + jnp.exp2(sink_row - m_row)
        o_t = acc[0:HEAD_DIM, :] / den
        for j in range(GROUP):
            h = kvh * GROUP + j
            ot_scr[h * HEAD_DIM:(h + 1) * HEAD_DIM, :] = o_t[:, j * QBLK:(j + 1) * QBLK]

    stats = scores(0)
    for kvh in range(N_KV_HEADS):
        nxt = scores(kvh + 1) if kvh + 1 < N_KV_HEADS else None
        weighted_values(kvh, *stats)
        stats = nxt
    z = (ot_scr[...].T * sg_ref[0].astype(F32)).astype(BF16)
    y = _dot(z, wo_ref[...])
    o_ref[0] = x_ref[0] + mod_ref[0][2:3, :] * y


def _attn(q, sg, x, mod3, mod_row, kc, vct, k_lat, vt_lat, sink, wo_bf):
    bsz, t, d = x.shape
    dq = q.shape[-1]
    dkv = kc.shape[-1]
    p_len = kc.shape[1]
    nb = t // QBLK
    window = k_lat is not None
    tok = lambda b, i: (b, i, 0)
    in_specs = [pl.BlockSpec((1, QBLK, dq), tok),
                pl.BlockSpec((1, QBLK, dq), tok),
                pl.BlockSpec((1, QBLK, d), tok),
                pl.BlockSpec((1, 3, d), lambda b, i: (mod_row(b), 0, 0)),
                pl.BlockSpec((1, p_len, dkv), lambda b, i: (b, 0, 0)),
                pl.BlockSpec((1, dkv, p_len), lambda b, i: (b, 0, 0))]
    args = [q, sg, x, mod3, kc, vct]
    n_blocks = p_len // QBLK
    if window:
        prev = lambda i: jnp.maximum(i - 1, 0)
        nxt = lambda i: jnp.minimum(i + 1, nb - 1)
        in_specs += [pl.BlockSpec((1, QBLK, dkv), lambda b, i: (b, prev(i), 0)),
                     pl.BlockSpec((1, QBLK, dkv), tok),
                     pl.BlockSpec((1, QBLK, dkv), lambda b, i: (b, nxt(i), 0)),
                     pl.BlockSpec((1, dkv, QBLK), lambda b, i: (b, 0, prev(i))),
                     pl.BlockSpec((1, dkv, QBLK), lambda b, i: (b, 0, i)),
                     pl.BlockSpec((1, dkv, QBLK), lambda b, i: (b, 0, nxt(i)))]
        args += [k_lat] * 3 + [vt_lat] * 3
        n_blocks += 3
    in_specs += [pl.BlockSpec((1, N_HEADS), lambda b, i: (0, 0)),
                 pl.BlockSpec(wo_bf.shape, lambda b, i: (0, 0))]
    args += [sink.reshape(1, N_HEADS), wo_bf]
    return pl.pallas_call(
        functools.partial(_attn_kernel, window=window, nb=nb),
        out_shape=jax.ShapeDtypeStruct((bsz, t, d), F32),
        grid=(bsz, nb),
        in_specs=in_specs,
        out_specs=pl.BlockSpec((1, QBLK, d), tok),
        scratch_shapes=[pltpu.VMEM((N_KV_HEADS, n_blocks, QBLK, GROUP * QBLK), F32),
                        pltpu.VMEM((N_KV_HEADS, n_blocks * QBLK, GROUP * QBLK), BF16),
                        pltpu.VMEM((dq, QBLK), F32)],
        compiler_params=_cparams(("parallel", "parallel")),
        name="attn_window" if window else "attn_ctx",
    )(*args)


def _mlstm_in_kernel(x_ref, mod_ref, nw_ref, w_ref, wvt_ref, wg_ref, wgt_ref, bg_ref, bgt_ref,
                     q_ref, k_ref, vt_ref, og_ref, gc_ref, gr_ref):
    dm = q_ref.shape[-1]
    nh = M_HEADS
    L = MCHUNK
    hb = _prenorm(x_ref[0], nw_ref[...], mod_ref[0]).astype(BF16)
    q_ref[0] = _dot(hb, w_ref[:, 0:dm]).astype(q_ref.dtype)
    k_ref[0] = (_dot(hb, w_ref[:, dm:2 * dm]) * (M_HD ** -0.5)).astype(k_ref.dtype)
    vt_ref[0] = _dot_nt(wvt_ref[...], hb).astype(vt_ref.dtype)
    o = _dot(hb, w_ref[:, 2 * dm:3 * dm])
    g = _dot(hb, w_ref[:, 3 * dm:4 * dm])
    og_ref[0] = (jax.nn.sigmoid(o) * _silu(g)).astype(og_ref.dtype)

    gc = _dot(hb, wg_ref[...]) + bg_ref[...]
    gr = _dot_nt(wgt_ref[...], hb) + bgt_ref[...]
    ri = lax.broadcasted_iota(jnp.int32, (L, L), 0)
    ci = lax.broadcasted_iota(jnp.int32, (L, L), 1)
    for dr in range(2):
        seen = (ci <= ri) if dr == 0 else (ci >= ri)
        tri = jnp.where(seen, 1.0, 0.0).astype(BF16)
        base = dr * 2 * nh
        for cidx in range(x_ref.shape[1] // L):
            rows = slice(cidx * L, (cidx + 1) * L)
            lf_c = _log_sigmoid(gc[rows, base + nh:base + 2 * nh])
            lf_r = _log_sigmoid(gr[base + nh:base + 2 * nh, rows])
            b_c = sum(_dot(tri, piece) for piece in _split3(lf_c))
            b_r = sum(_dot_nt(piece, tri) for piece in _split3(lf_r))
            g_r = gr[base:base + nh, rows] - b_r
            b_last = jnp.sum(lf_r, axis=1, keepdims=True)
            g_max = jnp.max(g_r, axis=1, keepdims=True)
            gc_ref[0, dr, rows, :] = gc[rows, base:base + nh] - b_c
            gr_ref[0, dr, :, rows] = jnp.concatenate(
                [g_r, b_r, jnp.broadcast_to(b_last, (nh, L)), jnp.broadcast_to(g_max, (nh, L))], axis=0)


def _mlstm_in(x, mod3, mod_row, norm_w, w_main_bf, wvt_bf, wg_bf, b_gates):
    bsz, t, d = x.shape
    dm = M_HEADS * M_HD
    ng = 4 * M_HEADS
    tm = ROW_TILE
    tok = lambda b, i: (b, i, 0)
    const = lambda b, i: (0, 0)
    big = jax.ShapeDtypeStruct((bsz, t, dm), BF16)
    return pl.pallas_call(
        _mlstm_in_kernel,
        out_shape=(big, big, jax.ShapeDtypeStruct((bsz, dm, t), BF16), big,
                   jax.ShapeDtypeStruct((bsz, 2, t, M_HEADS), F32),
                   jax.ShapeDtypeStruct((bsz, 2, 4 * M_HEADS, t), F32)),
        grid=(bsz, t // tm),
        in_specs=[pl.BlockSpec((1, tm, d), tok),
                  pl.BlockSpec((1, 3, d), lambda b, i: (mod_row(b), 0, 0)),
                  pl.BlockSpec((1, d), const),
                  pl.BlockSpec(w_main_bf.shape, const),
                  pl.BlockSpec(wvt_bf.shape, const),
                  pl.BlockSpec((d, ng), const),
                  pl.BlockSpec((ng, d), const),
                  pl.BlockSpec((1, ng), const),
                  pl.BlockSpec((ng, 1), const)],
        out_specs=(pl.BlockSpec((1, tm, dm), tok), pl.BlockSpec((1, tm, dm), tok),
                   pl.BlockSpec((1, dm, tm), lambda b, i: (b, 0, i)), pl.BlockSpec((1, tm, dm), tok),
                   pl.BlockSpec((1, 2, tm, M_HEADS), lambda b, i: (b, 0, i, 0)),
                   pl.BlockSpec((1, 2, 4 * M_HEADS, tm), lambda b, i: (b, 0, 0, i))),
        compiler_params=_cparams(("parallel", "parallel")),
        name="mlstm_in",
    )(x, mod3, norm_w.reshape(1, d), w_main_bf, wvt_bf, wg_bf, wg_bf.T,
      b_gates.reshape(1, ng), b_gates.reshape(ng, 1))


def _mlstm_scan_kernel(*refs, has_init, write_state, nc):
    refs = list(refs)
    q_ref, k_ref, vt_ref, gc_ref, gr_ref = refs[:5]
    pos = 5
    if has_init:
        c0_ref, n0_ref, m0_ref = refs[pos:pos + 3]
        pos += 3
    ht_ref = refs[pos]
    pos += 1
    if write_state:
        cout_ref, nout_ref, mout_ref = refs[pos:pos + 3]
        pos += 3
    ct_scr, mscr = refs[pos:pos + 2]

    drn = pl.program_id(1)
    c = pl.program_id(2)
    L = q_ref.shape[1]
    nh = M_HEADS
    pad = ct_scr.shape[1] - M_HD

    @pl.when(c == 0)
    def _init():
        if has_init:
            for h in range(nh):
                ct_scr[h, 0:M_HD, :] = c0_ref[0, 0, h].T
                ct_scr[h, M_HD:M_HD + pad, :] = jnp.concatenate(
                    [n0_ref[0, 0, h:h + 1, :], jnp.zeros((pad - 1, M_HD), F32)], axis=0)
            mscr[...] = m0_ref[0, 0]
        else:
            ct_scr[...] = jnp.zeros(ct_scr.shape, F32)
            mscr[...] = jnp.zeros(mscr.shape, F32)

    si = lax.broadcasted_iota(jnp.int32, (L, L), 0)
    li = lax.broadcasted_iota(jnp.int32, (L, L), 1)
    seen_t = (si - li) * (1 - 2 * drn) <= 0

    gcb = gc_ref[0, 0]
    grb = gr_ref[0, 0]
    q = q_ref[0]
    k = k_ref[0]
    vt = vt_ref[0]
    ones_rows = jnp.where(lax.broadcasted_iota(jnp.int32, (pad, L), 0) == 0, 1.0, 0.0).astype(BF16)

    def head_scores(h):
        hs = slice(h * M_HD, (h + 1) * M_HD)
        m_prev = mscr[h:h + 1, 0:1]
        ct = ct_scr[h]
        a_t = jnp.where(seen_t, gcb[:, h:h + 1], NEG_INF)
        m_row = jnp.maximum(jnp.max(a_t, axis=0, keepdims=True), m_prev)
        w_t = jnp.exp(a_t - m_row)
        r1 = _dot_nt(jnp.concatenate([k[:, hs], ct.astype(BF16)], axis=0), q[:, hs])
        s_t = (r1[0:L, :] * w_t).astype(BF16)
        return m_prev, ct, m_row, s_t, r1[L:, :]

    def head_finish(h, m_prev, ct, m_row, s_t, inter):
        hs = slice(h * M_HD, (h + 1) * M_HD)
        vext = jnp.concatenate([vt[hs, :], ones_rows], axis=0)
        g_r = grb[h:h + 1, :]
        b_r = grb[nh + h:nh + h + 1, :]
        b_last = grb[2 * nh + h:2 * nh + h + 1, 0:1]
        g_max = grb[3 * nh + h:3 * nh + h + 1, 0:1]
        w0 = jnp.exp(m_prev - m_row)
        tot = _dot(vext, s_t) + w0 * inter
        den = tot[M_HD:M_HD + 1, :]
        floor = jnp.exp(-(b_r + m_row))
        ht_ref[0, 0, hs, :] = tot[0:M_HD, :] / jnp.maximum(jnp.abs(den), floor)

        m_last = jnp.maximum(g_max, m_prev)
        wk = jnp.exp(g_r - m_last)
        decay = jnp.exp(m_prev - m_last)
        vw = (vext.astype(F32) * wk).astype(BF16)
        ct_scr[h] = decay * ct + _dot(vw, k[:, hs])
        mscr[h:h + 1, :] = jnp.broadcast_to(b_last + m_last, (1, LANES))

    pending = [head_scores(h) for h in range(min(SCAN_AHEAD, nh))]
    for h in range(nh):
        if h + SCAN_AHEAD < nh:
            pending.append(head_scores(h + SCAN_AHEAD))
        head_finish(h, *pending.pop(0))

    if write_state:
        @pl.when(c == nc - 1)
        def _final():
            for h in range(nh):
                cfin = ct_scr[h]
                cout_ref[0, 0, h] = cfin[0:M_HD, :].T
                nout_ref[0, 0, h:h + 1, :] = cfin[M_HD:M_HD + 1, :]
            mout_ref[0, 0] = mscr[...]


def _mlstm_scan(q, k, vt, gc, gr, init, write_state):
    bsz, t, dm = q.shape
    L = MCHUNK
    nc = t // L
    chunk = lambda b, d, c: c + d * (nc - 1 - 2 * c)
    tok = lambda b, d, c: (b, chunk(b, d, c), 0)
    in_specs = [pl.BlockSpec((1, L, dm), tok)] * 2 + [
        pl.BlockSpec((1, dm, L), lambda b, d, c: (b, 0, chunk(b, d, c))),
        pl.BlockSpec((1, 1, L, gc.shape[-1]), lambda b, d, c: (b, d, chunk(b, d, c), 0)),
        pl.BlockSpec((1, 1, gr.shape[2], L), lambda b, d, c: (b, d, 0, chunk(b, d, c)))]
    args = [q, k, vt, gc, gr]
    st = lambda b, d, c: (b, d, 0, 0)
    st5 = lambda b, d, c: (b, d, 0, 0, 0)
    if init is not None:
        c0, n0, m0 = init
        in_specs += [pl.BlockSpec((1, 1, M_HEADS, M_HD, M_HD), st5),
                     pl.BlockSpec((1, 1, M_HEADS, M_HD), st),
                     pl.BlockSpec((1, 1, M_HEADS, LANES), st)]
        args += [c0, n0, jnp.broadcast_to(m0[..., None], m0.shape + (LANES,))]
    out_shape = [jax.ShapeDtypeStruct((2, bsz, dm, t), F32)]
    out_specs = [pl.BlockSpec((1, 1, dm, L), lambda b, d, c: (d, b, 0, chunk(b, d, c)))]
    if write_state:
        out_shape += [jax.ShapeDtypeStruct((bsz, 2, M_HEADS, M_HD, M_HD), F32),
                      jax.ShapeDtypeStruct((bsz, 2, M_HEADS, M_HD), F32),
                      jax.ShapeDtypeStruct((bsz, 2, M_HEADS, LANES), F32)]
        out_specs += [pl.BlockSpec((1, 1, M_HEADS, M_HD, M_HD), st5),
                      pl.BlockSpec((1, 1, M_HEADS, M_HD), st),
                      pl.BlockSpec((1, 1, M_HEADS, LANES), st)]
    return pl.pallas_call(
        functools.partial(_mlstm_scan_kernel, has_init=init is not None,
                          write_state=write_state, nc=nc),
        out_shape=tuple(out_shape),
        grid=(bsz, 2, nc),
        in_specs=in_specs,
        out_specs=tuple(out_specs),
        scratch_shapes=[pltpu.VMEM((M_HEADS, M_HD + 16, M_HD), F32),
                        pltpu.VMEM((M_HEADS, LANES), F32)],
        compiler_params=_cparams(("parallel", "parallel", "arbitrary")),
        name="mlstm_scan",
    )(*args)


def _mlstm_out_kernel(ht_ref, og_ref, x_ref, mod_ref, wo_ref, fw_ref, o_ref):
    hm = (ht_ref[0, 0] + ht_ref[1, 0]).T * og_ref[0].astype(F32)
    y = _dot(hm.astype(BF16), wo_ref[...])
    x2 = x_ref[0] + mod_ref[0][2:3, :] * y
    ms = jnp.mean(x2 * x2, axis=-1, keepdims=True)
    o_ref[0] = x2 * lax.rsqrt(ms + EPS) * fw_ref[...]


def _mlstm_out(hdir, og, x, mod3, mod_row, wo_bf, final_w):
    bsz, t, d = x.shape
    dm = og.shape[-1]
    tm = ROW_TILE
    tok = lambda b, i: (b, i, 0)
    return pl.pallas_call(
        _mlstm_out_kernel,
        out_shape=jax.ShapeDtypeStruct((bsz, t, d), F32),
        grid=(bsz, t // tm),
        in_specs=[pl.BlockSpec((2, 1, dm, tm), lambda b, i: (0, b, 0, i)),
                  pl.BlockSpec((1, tm, dm), tok),
                  pl.BlockSpec((1, tm, d), tok),
                  pl.BlockSpec((1, 3, d), lambda b, i: (mod_row(b), 0, 0)),
                  pl.BlockSpec(wo_bf.shape, lambda b, i: (0, 0)),
                  pl.BlockSpec((1, d), lambda b, i: (0, 0))],
        out_specs=pl.BlockSpec((1, tm, d), tok),
        compiler_params=_cparams(("parallel", "parallel")),
        name="mlstm_out",
    )(hdir, og, x, mod3, wo_bf, final_w.reshape(1, d))


def _rope_tables(t):
    nf = HEAD_DIM // 4
    pos = jnp.arange(t)
    row = (pos // GRID_W).astype(F32)
    col = (pos % GRID_W).astype(F32)
    inv = ROPE_BASE ** (-jnp.arange(nf, dtype=F32) / nf)
    ar = row[:, None] * inv[None, :]
    ac = col[:, None] * inv[None, :]
    cos = jnp.concatenate([jnp.cos(ar), jnp.cos(ar), jnp.cos(ac), jnp.cos(ac)], axis=1)
    sin = jnp.concatenate([-jnp.sin(ar), jnp.sin(ar), -jnp.sin(ac), jnp.sin(ac)], axis=1)
    reps = LANES // HEAD_DIM
    return jnp.tile(cos, (1, reps)), jnp.tile(sin, (1, reps))


def kernel(x_prompt, x_sample, cache_k, cache_v, state_C, state_n, state_m, c, c_ctx,
           attn_norm_w, attn_ada_w, attn_ada_b, attn_w_in, attn_sink, attn_w_out,
           mlstm_norm_w, mlstm_ada_w, mlstm_ada_b, mlstm_w_in, mlstm_b_gates, mlstm_w_out,
           final_norm_w):
    assert attn_w_in.shape[0] == 1 and mlstm_w_in.shape[0] == 1, "one layer of each mixer"
    bsz, seq, d = x_prompt.shape
    dbsz, dseq, _ = x_sample.shape
    dkv = N_KV_HEADS * HEAD_DIM
    dm = M_HEADS * M_HD

    n_cond = 1 + dbsz
    cond = jnp.concatenate([c_ctx[None, :], c, jnp.zeros((-n_cond % 8, d), F32)], axis=0)
    attn_mod = _ada(cond, attn_ada_w[0], attn_ada_b[0]).reshape(-1, 3, d)
    mlstm_mod = _ada(cond, mlstm_ada_w[0], mlstm_ada_b[0]).reshape(-1, 3, d)
    ctx_row = lambda b: 0
    lat_row = lambda b: b + 1

    attn_w_in_bf = attn_w_in[0].astype(BF16)
    attn_w_out_bf = attn_w_out[0].astype(BF16)
    attn_wvt_bf = attn_w_in[0, :, 2 * N_HEADS * HEAD_DIM + dkv:].T.astype(BF16)
    w_main_bf = jnp.concatenate([mlstm_w_in[0, :, :2 * dm], mlstm_w_in[0, :, 3 * dm:5 * dm]],
                                axis=1).astype(BF16)
    mlstm_wvt_bf = mlstm_w_in[0, :, 2 * dm:3 * dm].T.astype(BF16)
    wg_bf = mlstm_w_in[0, :, 5 * dm:].astype(BF16)
    mlstm_w_out_bf = mlstm_w_out[0].astype(BF16)

    def mlstm_layer(x, mod_row, init, write_state):
        q, k, vt, og, gc, gr = _mlstm_in(x, mlstm_mod, mod_row, mlstm_norm_w[0], w_main_bf,
                                         mlstm_wvt_bf, wg_bf, mlstm_b_gates[0])
        outs = _mlstm_scan(q, k, vt, gc, gr, init, write_state)
        y = _mlstm_out(outs[0], og, x, mlstm_mod, mod_row, mlstm_w_out_bf, final_norm_w)
        return y, outs[1:]

    q, sg, k_ctx, vt_ctx, v_ctx = _attn_in(x_prompt, attn_mod, ctx_row, attn_norm_w[0], attn_w_in_bf,
                                           attn_wvt_bf, None, F32, True)
    x1 = _attn(q, sg, x_prompt, attn_mod, ctx_row, k_ctx, vt_ctx, None, None, attn_sink[0], attn_w_out_bf)
    y_prompt, (c_fin, n_fin, m_fin) = mlstm_layer(x1, ctx_row, None, True)

    q, sg, k_lat, vt_lat = _attn_in(x_sample, attn_mod, lat_row, attn_norm_w[0], attn_w_in_bf,
                                    attn_wvt_bf, _rope_tables(dseq), BF16, False)
    kc = cache_k[:, 0].reshape(dbsz, -1, dkv).astype(BF16)
    vct = jnp.swapaxes(cache_v[:, 0].reshape(dbsz, -1, dkv), 1, 2).astype(BF16)
    x1 = _attn(q, sg, x_sample, attn_mod, lat_row, kc, vct, k_lat, vt_lat, attn_sink[0], attn_w_out_bf)
    y_sample, _ = mlstm_layer(x1, lat_row, (state_C[:, 0], state_n[:, 0], state_m[:, 0]), False)

    new_cache_k = k_ctx.reshape(bsz, 1, seq, N_KV_HEADS, HEAD_DIM)
    new_cache_v = v_ctx.reshape(bsz, 1, seq, N_KV_HEADS, HEAD_DIM)
    return (y_prompt, y_sample, new_cache_k, new_cache_v,
            c_fin[:, None], n_fin[:, None], m_fin[:, None, :, :, 0])
```

```python
import functools

import jax
import jax.numpy as jnp
from jax import lax
from jax.experimental import pallas as pl
from jax.experimental.pallas import tpu as pltpu

F32 = jnp.float32
BF16 = jnp.bfloat16

HEAD_DIM = 64
N_KV_HEADS = 4
GROUP = 4
N_HEADS = N_KV_HEADS * GROUP
QBLK = 128
GRID_W = 64
ROPE_BASE = 10000.0
M_HEADS = 8
M_HD = 128
EPS = 1e-6

LANES = 128
VMEM_LIMIT = 48 * 1024 * 1024

MCHUNK = 128
ATTN_AHEAD = 4
SCAN_AHEAD = 4
ROW_TILE = 256

NEG_INF = float("-inf")
LOG2E = 1.4426950408889634


def _cparams(sem):
    return pltpu.CompilerParams(dimension_semantics=sem, vmem_limit_bytes=VMEM_LIMIT)


def _silu(x):
    return x * jax.nn.sigmoid(x)


def _log_sigmoid(x):
    return jnp.minimum(x, 0.0) - jnp.log1p(jnp.exp(-jnp.abs(x)))


def _dot(a, b):
    return jnp.dot(a, b, preferred_element_type=F32)


def _dot_nt(a, b):
    return lax.dot_general(a, b, (((1,), (1,)), ((), ())), preferred_element_type=F32)


def _dot_tn(a, b):
    return lax.dot_general(a, b, (((0,), (0,)), ((), ())), preferred_element_type=F32)


def _split3(x):
    hi = x.astype(BF16)
    r = x - hi.astype(F32)
    mid = r.astype(BF16)
    lo = (r - mid.astype(F32)).astype(BF16)
    return hi, mid, lo


def _prenorm(x, norm_w, mod):
    ms = jnp.mean(x * x, axis=-1, keepdims=True)
    y = x * lax.rsqrt(ms + EPS) * norm_w
    return y * (1.0 + mod[1:2, :]) + mod[0:1, :]


def _ada_kernel(cond_ref, w_ref, b_ref, o_ref):
    a = _silu(cond_ref[...]).astype(BF16)
    o_ref[...] = _dot(a, w_ref[...].astype(BF16)) + b_ref[...]


def _ada(cond8, w, b):
    d, n = w.shape
    tn = 512
    return pl.pallas_call(
        _ada_kernel,
        out_shape=jax.ShapeDtypeStruct((cond8.shape[0], n), F32),
        grid=(n // tn,),
        in_specs=[pl.BlockSpec(cond8.shape, lambda j: (0, 0)),
                  pl.BlockSpec((d, tn), lambda j: (0, j)),
                  pl.BlockSpec((1, tn), lambda j: (0, j))],
        out_specs=pl.BlockSpec((cond8.shape[0], tn), lambda j: (0, j)),
        compiler_params=_cparams(("parallel",)),
        name="ada_mod",
    )(cond8, w, b.reshape(1, n))


def _rope(x, cos, sin, lane):
    first = (lane & 31) < 16
    outs = []
    for c in range(x.shape[1] // LANES):
        xc = x[:, c * LANES:(c + 1) * LANES]
        sw = jnp.where(first, pltpu.roll(xc, LANES - 16, 1), pltpu.roll(xc, 16, 1))
        outs.append(xc * cos + sw * sin)
    return jnp.concatenate(outs, axis=1)


def _attn_in_kernel(*refs, rope, emit_v):
    refs = list(refs)
    x_ref, mod_ref, nw_ref, w_ref, wvt_ref = refs[:5]
    pos = 5
    if rope:
        cos_ref, sin_ref = refs[pos:pos + 2]
        pos += 2
    q_ref, sg_ref, k_ref, vt_ref = refs[pos:pos + 4]
    dq = q_ref.shape[-1]
    dkv = k_ref.shape[-1]
    hb = _prenorm(x_ref[0], nw_ref[...], mod_ref[0]).astype(BF16)
    q = _dot(hb, w_ref[:, 0:dq])
    g = _dot(hb, w_ref[:, dq:2 * dq])
    k = _dot(hb, w_ref[:, 2 * dq:2 * dq + dkv])
    if rope:
        cos = cos_ref[...]
        sin = sin_ref[...]
        lane = lax.broadcasted_iota(jnp.int32, cos.shape, 1)
        q = _rope(q, cos, sin, lane)
        k = _rope(k, cos, sin, lane)
    q_ref[0] = (q * (HEAD_DIM ** -0.5 * LOG2E)).astype(q_ref.dtype)
    sg_ref[0] = _silu(g).astype(sg_ref.dtype)
    k_ref[0] = k.astype(k_ref.dtype)
    vt_ref[0] = _dot_nt(wvt_ref[...], hb).astype(vt_ref.dtype)
    if emit_v:
        v_ref = refs[pos + 4]
        v_ref[0] = _dot(hb, w_ref[:, 2 * dq + dkv:2 * dq + 2 * dkv]).astype(v_ref.dtype)


def _attn_in(x, mod3, mod_row, norm_w, w_bf, wvt_bf, rope_tabs, k_dtype, emit_v):
    bsz, t, d = x.shape
    dq = N_HEADS * HEAD_DIM
    dkv = N_KV_HEADS * HEAD_DIM
    tm = min(2 * ROW_TILE, t)
    rope = rope_tabs is not None
    tok = lambda b, i: (b, i, 0)
    const = lambda b, i: (0, 0)
    in_specs = [pl.BlockSpec((1, tm, d), tok),
                pl.BlockSpec((1, 3, d), lambda b, i: (mod_row(b), 0, 0)),
                pl.BlockSpec((1, d), const),
                pl.BlockSpec(w_bf.shape, const),
                pl.BlockSpec(wvt_bf.shape, const)]
    args = [x, mod3, norm_w.reshape(1, d), w_bf, wvt_bf]
    if rope:
        in_specs += [pl.BlockSpec((tm, LANES), lambda b, i: (i, 0))] * 2
        args += list(rope_tabs)
    out_shape = [jax.ShapeDtypeStruct((bsz, t, dq), BF16),
                 jax.ShapeDtypeStruct((bsz, t, dq), BF16),
                 jax.ShapeDtypeStruct((bsz, t, dkv), k_dtype),
                 jax.ShapeDtypeStruct((bsz, dkv, t), BF16)]
    out_specs = [pl.BlockSpec((1, tm, dq), tok), pl.BlockSpec((1, tm, dq), tok),
                 pl.BlockSpec((1, tm, dkv), tok),
                 pl.BlockSpec((1, dkv, tm), lambda b, i: (b, 0, i))]
    if emit_v:
        out_shape.append(jax.ShapeDtypeStruct((bsz, t, dkv), F32))
        out_specs.append(pl.BlockSpec((1, tm, dkv), tok))
    return pl.pallas_call(
        functools.partial(_attn_in_kernel, rope=rope, emit_v=emit_v),
        out_shape=tuple(out_shape),
        grid=(bsz, t // tm),
        in_specs=in_specs,
        out_specs=tuple(out_specs),
        compiler_params=_cparams(("parallel", "parallel")),
        name="attn_in_rope" if rope else "attn_in",
    )(*args)


def _attn_kernel(*refs, window, nb):
    if window:
        (q_ref, sg_ref, x_ref, mod_ref, kc_ref, vct_ref, kp_ref, kq_ref, kn_ref,
         vpt_ref, vqt_ref, vnt_ref, sink_ref, wo_ref, o_ref, s_scr, p_scr, ot_scr) = refs
    else:
        q_ref, sg_ref, x_ref, mod_ref, kc_ref, vct_ref, sink_ref, wo_ref, o_ref, s_scr, p_scr, ot_scr = refs
    blk = pl.program_id(1)
    q = q_ref[0]
    n_ctx = kc_ref.shape[1] // QBLK
    cols = GROUP * QBLK
    if window:
        kj = lax.broadcasted_iota(jnp.int32, (QBLK, cols), 0)
        qi = lax.broadcasted_iota(jnp.int32, (QBLK, cols), 1) & (QBLK - 1)
        prev_ok = (kj >= qi) & (blk > 0)
        next_ok = (kj <= qi) & (blk < nb - 1)
    ones_rows = jnp.where(lax.broadcasted_iota(jnp.int32, (16, QBLK), 0) == 0, 1.0, 0.0).astype(BF16)
    masks = [None] * n_ctx + ([prev_ok, None, next_ok] if window else [])
    n_blk = len(masks)

    def scores(kvh):
        cs = slice(kvh * HEAD_DIM, (kvh + 1) * HEAD_DIM)
        heads = [kvh * GROUP + j for j in range(GROUP)]
        q4 = jnp.concatenate([q[:, h * HEAD_DIM:(h + 1) * HEAD_DIM] for h in heads], axis=0)
        sink_row = jnp.concatenate(
            [jnp.broadcast_to(sink_ref[0:1, h:h + 1], (1, QBLK)) for h in heads], axis=1) * LOG2E
        keys = [kc_ref[0, j * QBLK:(j + 1) * QBLK, cs].astype(BF16) for j in range(n_ctx)]
        if window:
            keys += [kp_ref[0][:, cs], kq_ref[0][:, cs], kn_ref[0][:, cs]]
        st_all = _dot_nt(jnp.concatenate(keys, axis=0), q4)
        macc = jnp.full((8, cols), NEG_INF, F32)
        for j, ok in enumerate(masks):
            st = st_all[j * QBLK:(j + 1) * QBLK, :]
            if ok is not None:
                st = jnp.where(ok, st, NEG_INF)
            s_scr[kvh, j] = st
            macc = jnp.maximum(macc, jnp.max(st.reshape(QBLK // 8, 8, cols), axis=0))
        return jnp.maximum(jnp.max(macc, axis=0, keepdims=True), sink_row), sink_row

    def weighted_values(kvh, m_row, sink_row):
        cs = slice(kvh * HEAD_DIM, (kvh + 1) * HEAD_DIM)
        for j in range(n_blk):
            p_scr[kvh, j * QBLK:(j + 1) * QBLK, :] = jnp.exp2(s_scr[kvh, j] - m_row).astype(BF16)
        vts = [vct_ref[0, cs, j * QBLK:(j + 1) * QBLK] for j in range(n_ctx)]
        if window:
            vts += [vpt_ref[0, cs, :], vqt_ref[0, cs, :], vnt_ref[0, cs, :]]
        vt_ext = jnp.concatenate(
            [jnp.concatenate(vts, axis=1), jnp.tile(ones_rows, (1, n_blk))], axis=0)
        acc = _dot(vt_ext, p_scr[kvh])
        den = acc[HEAD_DIM:HEAD_DIM + 1, :] + jnp.exp2(sink_row - m_row)
        o_t = acc[0:HEAD_DIM, :] / den
        for j in range(GROUP):
            h = kvh * GROUP + j
            ot_scr[h * HEAD_DIM:(h + 1) * HEAD_DIM, :] = o_t[:, j * QBLK:(j + 1) * QBLK]

    pending = [scores(kvh) for kvh in range(min(ATTN_AHEAD, N_KV_HEADS))]
    for kvh in range(N_KV_HEADS):
        if kvh + ATTN_AHEAD < N_KV_HEADS:
            pending.append(scores(kvh + ATTN_AHEAD))
        weighted_values(kvh, *pending.pop(0))
    z = (ot_scr[...].T * sg_ref[0].astype(F32)).astype(BF16)
    y = _dot(z, wo_ref[...])
    o_ref[0] = x_ref[0] + mod_ref[0][2:3, :] * y


def _attn(q, sg, x, mod3, mod_row, kc, vct, k_lat, vt_lat, sink, wo_bf):
    bsz, t, d = x.shape
    dq = q.shape[-1]
    dkv = kc.shape[-1]
    p_len = kc.shape[1]
    nb = t // QBLK
    window = k_lat is not None
    tok = lambda b, i: (b, i, 0)
    in_specs = [pl.BlockSpec((1, QBLK, dq), tok),
                pl.BlockSpec((1, QBLK, dq), tok),
                pl.BlockSpec((1, QBLK, d), tok),
                pl.BlockSpec((1, 3, d), lambda b, i: (mod_row(b), 0, 0)),
                pl.BlockSpec((1, p_len, dkv), lambda b, i: (b, 0, 0)),
                pl.BlockSpec((1, dkv, p_len), lambda b, i: (b, 0, 0))]
    args = [q, sg, x, mod3, kc, vct]
    n_blocks = p_len // QBLK
    if window:
        prev = lambda i: jnp.maximum(i - 1, 0)
        nxt = lambda i: jnp.minimum(i + 1, nb - 1)
        in_specs += [pl.BlockSpec((1, QBLK, dkv), lambda b, i: (b, prev(i), 0)),
                     pl.BlockSpec((1, QBLK, dkv), tok),
                     pl.BlockSpec((1, QBLK, dkv), lambda b, i: (b, nxt(i), 0)),
                     pl.BlockSpec((1, dkv, QBLK), lambda b, i: (b, 0, prev(i))),
                     pl.BlockSpec((1, dkv, QBLK), lambda b, i: (b, 0, i)),
                     pl.BlockSpec((1, dkv, QBLK), lambda b, i: (b, 0, nxt(i)))]
        args += [k_lat] * 3 + [vt_lat] * 3
        n_blocks += 3
    in_specs += [pl.BlockSpec((1, N_HEADS), lambda b, i: (0, 0)),
                 pl.BlockSpec(wo_bf.shape, lambda b, i: (0, 0))]
    args += [sink.reshape(1, N_HEADS), wo_bf]
    return pl.pallas_call(
        functools.partial(_attn_kernel, window=window, nb=nb),
        out_shape=jax.ShapeDtypeStruct((bsz, t, d), F32),
        grid=(bsz, nb),
        in_specs=in_specs,
        out_specs=pl.BlockSpec((1, QBLK, d), tok),
        scratch_shapes=[pltpu.VMEM((N_KV_HEADS, n_blocks, QBLK, GROUP * QBLK), F32),
                        pltpu.VMEM((N_KV_HEADS, n_blocks * QBLK, GROUP * QBLK), BF16),
                        pltpu.VMEM((dq, QBLK), F32)],
        compiler_params=_cparams(("parallel", "parallel")),
        name="attn_window" if window else "attn_ctx",
    )(*args)


def _mlstm_in_kernel(x_ref, mod_ref, nw_ref, w_ref, wvt_ref, wgt_ref, bgt_ref,
                     q_ref, k_ref, vt_ref, og_ref, gc_ref, gr_ref):
    dm = q_ref.shape[-1]
    nh = M_HEADS
    L = MCHUNK
    hb = _prenorm(x_ref[0], nw_ref[...], mod_ref[0]).astype(BF16)

    gr = _dot_nt(wgt_ref[...], hb) + bgt_ref[...]
    n_chunks = x_ref.shape[1] // L
    ri = lax.broadcasted_iota(jnp.int32, (L, L), 0)
    ci = lax.broadcasted_iota(jnp.int32, (L, L), 1)
    g_rows = []
    for dr in range(2):
        before = (ri <= ci) if dr == 0 else (ri >= ci)
        tri = jnp.where(before, 1.0, 0.0).astype(BF16)
        base = dr * 2 * nh
        lf = _log_sigmoid(gr[base + nh:base + 2 * nh, :])
        lf_st = jnp.concatenate([lf[:, c * L:(c + 1) * L] for c in range(n_chunks)], axis=0)
        b_st = sum(_dot(piece, tri) for piece in _split3(lf_st))
        for cidx in range(n_chunks):
            rows = slice(cidx * L, (cidx + 1) * L)
            b_r = b_st[cidx * nh:(cidx + 1) * nh, :]
            g_r = gr[base:base + nh, rows] - b_r
            b_last = jnp.sum(lf[:, rows], axis=1, keepdims=True)
            g_max = jnp.max(g_r, axis=1, keepdims=True)
            g_rows.append(g_r)
            gr_ref[0, dr, :, rows] = jnp.concatenate(
                [g_r, b_r, jnp.broadcast_to(b_last, (nh, L)), jnp.broadcast_to(g_max, (nh, L))], axis=0)
    g_sq = jnp.concatenate(g_rows + [jnp.zeros((L - len(g_rows) * nh, L), F32)], axis=0).T
    for dr in range(2):
        for cidx in range(n_chunks):
            idx = dr * n_chunks + cidx
            gc_ref[0, dr, cidx * L:(cidx + 1) * L, :] = g_sq[:, idx * nh:(idx + 1) * nh]

    q_ref[0] = _dot(hb, w_ref[:, 0:dm]).astype(q_ref.dtype)
    k_ref[0] = (_dot(hb, w_ref[:, dm:2 * dm]) * (M_HD ** -0.5)).astype(k_ref.dtype)
    vt_ref[0] = _dot_nt(wvt_ref[...], hb).astype(vt_ref.dtype)
    o = _dot(hb, w_ref[:, 2 * dm:3 * dm])
    g = _dot(hb, w_ref[:, 3 * dm:4 * dm])
    og_ref[0] = (jax.nn.sigmoid(o) * _silu(g)).astype(og_ref.dtype)


def _mlstm_in(x, mod3, mod_row, norm_w, w_main_bf, wvt_bf, wgt_bf, b_gates):
    bsz, t, d = x.shape
    dm = M_HEADS * M_HD
    ng = 4 * M_HEADS
    tm = min(2 * ROW_TILE, t)
    tok = lambda b, i: (b, i, 0)
    const = lambda b, i: (0, 0)
    big = jax.ShapeDtypeStruct((bsz, t, dm), BF16)
    return pl.pallas_call(
        _mlstm_in_kernel,
        out_shape=(big, big, jax.ShapeDtypeStruct((bsz, dm, t), BF16), big,
                   jax.ShapeDtypeStruct((bsz, 2, t, M_HEADS), F32),
                   jax.ShapeDtypeStruct((bsz, 2, 4 * M_HEADS, t), F32)),
        grid=(bsz, t // tm),
        in_specs=[pl.BlockSpec((1, tm, d), tok),
                  pl.BlockSpec((1, 3, d), lambda b, i: (mod_row(b), 0, 0)),
                  pl.BlockSpec((1, d), const),
                  pl.BlockSpec(w_main_bf.shape, const),
                  pl.BlockSpec(wvt_bf.shape, const),
                  pl.BlockSpec((ng, d), const),
                  pl.BlockSpec((ng, 1), const)],
        out_specs=(pl.BlockSpec((1, tm, dm), tok), pl.BlockSpec((1, tm, dm), tok),
                   pl.BlockSpec((1, dm, tm), lambda b, i: (b, 0, i)), pl.BlockSpec((1, tm, dm), tok),
                   pl.BlockSpec((1, 2, tm, M_HEADS), lambda b, i: (b, 0, i, 0)),
                   pl.BlockSpec((1, 2, 4 * M_HEADS, tm), lambda b, i: (b, 0, 0, i))),
        compiler_params=_cparams(("parallel", "parallel")),
        name="mlstm_in",
    )(x, mod3, norm_w.reshape(1, d), w_main_bf, wvt_bf, wgt_bf, b_gates.reshape(ng, 1))


def _mlstm_scan_kernel(*refs, has_init, write_state, nc):
    refs = list(refs)
    q_ref, k_ref, vt_ref, gc_ref, gr_ref = refs[:5]
    pos = 5
    if has_init:
        c0_ref, n0_ref, m0_ref = refs[pos:pos + 3]
        pos += 3
    ht_ref = refs[pos]
    pos += 1
    if write_state:
        cout_ref, nout_ref, mout_ref = refs[pos:pos + 3]
        pos += 3
    ct_scr, mscr = refs[pos:pos + 2]

    drn = pl.program_id(1)
    c = pl.program_id(2)
    L = q_ref.shape[1]
    nh = M_HEADS
    pad = ct_scr.shape[1] - M_HD

    @pl.when(c == 0)
    def _init():
        if has_init:
            for h in range(nh):
                ct_scr[h, 0:M_HD, :] = c0_ref[0, 0, h].T
                ct_scr[h, M_HD:M_HD + pad, :] = jnp.concatenate(
                    [n0_ref[0, 0, h:h + 1, :], jnp.zeros((pad - 1, M_HD), F32)], axis=0)
            mscr[...] = m0_ref[0, 0]
        else:
            ct_scr[...] = jnp.zeros(ct_scr.shape, F32)
            mscr[...] = jnp.zeros(mscr.shape, F32)

    si = lax.broadcasted_iota(jnp.int32, (L, L), 0)
    li = lax.broadcasted_iota(jnp.int32, (L, L), 1)
    seen_t = (si - li) * (1 - 2 * drn) <= 0

    gcb = gc_ref[0, 0]
    grb = gr_ref[0, 0]
    q = q_ref[0]
    k = k_ref[0]
    vt = vt_ref[0]
    ones_rows = jnp.where(lax.broadcasted_iota(jnp.int32, (pad, L), 0) == 0, 1.0, 0.0).astype(BF16)

    def head_scores(h):
        hs = slice(h * M_HD, (h + 1) * M_HD)
        m_prev = mscr[h:h + 1, 0:1]
        ct = ct_scr[h]
        a_t = jnp.where(seen_t, gcb[:, h:h + 1], NEG_INF)
        m_row = jnp.maximum(jnp.max(a_t, axis=0, keepdims=True), m_prev)
        w_t = jnp.exp(a_t - m_row)
        r1 = _dot_nt(jnp.concatenate([k[:, hs], ct.astype(BF16)], axis=0), q[:, hs])
        s_t = (r1[0:L, :] * w_t).astype(BF16)
        return m_prev, ct, m_row, s_t, r1[L:, :]

    def head_finish(h, m_prev, ct, m_row, s_t, inter):
        hs = slice(h * M_HD, (h + 1) * M_HD)
        vext = jnp.concatenate([vt[hs, :], ones_rows], axis=0)
        g_r = grb[h:h + 1, :]
        b_r = grb[nh + h:nh + h + 1, :]
        b_last = grb[2 * nh + h:2 * nh + h + 1, 0:1]
        g_max = grb[3 * nh + h:3 * nh + h + 1, 0:1]
        w0 = jnp.exp(m_prev - m_row)
        tot = _dot(vext, s_t) + w0 * inter
        den = tot[M_HD:M_HD + 1, :]
        floor = jnp.exp(-(b_r + m_row))
        ht_ref[0, 0, hs, :] = tot[0:M_HD, :] / jnp.maximum(jnp.abs(den), floor)

        m_last = jnp.maximum(g_max, m_prev)
        wk = jnp.exp(g_r - m_last)
        decay = jnp.exp(m_prev - m_last)
        vw = (vext.astype(F32) * wk).astype(BF16)
        ct_scr[h] = decay * ct + _dot(vw, k[:, hs])
        mscr[h:h + 1, :] = jnp.broadcast_to(b_last + m_last, (1, LANES))

    pending = [head_scores(h) for h in range(min(SCAN_AHEAD, nh))]
    for h in range(nh):
        if h + SCAN_AHEAD < nh:
            pending.append(head_scores(h + SCAN_AHEAD))
        head_finish(h, *pending.pop(0))

    if write_state:
        @pl.when(c == nc - 1)
        def _final():
            for h in range(nh):
                cfin = ct_scr[h]
                cout_ref[0, 0, h] = cfin[0:M_HD, :].T
                nout_ref[0, 0, h:h + 1, :] = cfin[M_HD:M_HD + 1, :]
            mout_ref[0, 0] = mscr[...]


def _mlstm_scan(q, k, vt, gc, gr, init, write_state):
    bsz, t, dm = q.shape
    L = MCHUNK
    nc = t // L
    chunk = lambda b, d, c: c + d * (nc - 1 - 2 * c)
    tok = lambda b, d, c: (b, chunk(b, d, c), 0)
    in_specs = [pl.BlockSpec((1, L, dm), tok)] * 2 + [
        pl.BlockSpec((1, dm, L), lambda b, d, c: (b, 0, chunk(b, d, c))),
        pl.BlockSpec((1, 1, L, gc.shape[-1]), lambda b, d, c: (b, d, chunk(b, d, c), 0)),
        pl.BlockSpec((1, 1, gr.shape[2], L), lambda b, d, c: (b, d, 0, chunk(b, d, c)))]
    args = [q, k, vt, gc, gr]
    st = lambda b, d, c: (b, d, 0, 0)
    st5 = lambda b, d, c: (b, d, 0, 0, 0)
    if init is not None:
        c0, n0, m0 = init
        in_specs += [pl.BlockSpec((1, 1, M_HEADS, M_HD, M_HD), st5),
                     pl.BlockSpec((1, 1, M_HEADS, M_HD), st),
                     pl.BlockSpec((1, 1, M_HEADS, LANES), st)]
        args += [c0, n0, jnp.broadcast_to(m0[..., None], m0.shape + (LANES,))]
    out_shape = [jax.ShapeDtypeStruct((2, bsz, dm, t), F32)]
    out_specs = [pl.BlockSpec((1, 1, dm, L), lambda b, d, c: (d, b, 0, chunk(b, d, c)))]
    if write_state:
        out_shape += [jax.ShapeDtypeStruct((bsz, 2, M_HEADS, M_HD, M_HD), F32),
                      jax.ShapeDtypeStruct((bsz, 2, M_HEADS, M_HD), F32),
                      jax.ShapeDtypeStruct((bsz, 2, M_HEADS, LANES), F32)]
        out_specs += [pl.BlockSpec((1, 1, M_HEADS, M_HD, M_HD), st5),
                      pl.BlockSpec((1, 1, M_HEADS, M_HD), st),
                      pl.BlockSpec((1, 1, M_HEADS, LANES), st)]
    return pl.pallas_call(
        functools.partial(_mlstm_scan_kernel, has_init=init is not None,
                          write_state=write_state, nc=nc),
        out_shape=tuple(out_shape),
        grid=(bsz, 2, nc),
        in_specs=in_specs,
        out_specs=tuple(out_specs),
        scratch_shapes=[pltpu.VMEM((M_HEADS, M_HD + 16, M_HD), F32),
                        pltpu.VMEM((M_HEADS, LANES), F32)],
        compiler_params=_cparams(("parallel", "parallel", "arbitrary")),
        name="mlstm_scan",
    )(*args)


def _mlstm_out_kernel(ht_ref, og_ref, x_ref, mod_ref, wo_ref, fw_ref, o_ref):
    hm = (ht_ref[0, 0] + ht_ref[1, 0]).T * og_ref[0].astype(F32)
    y = _dot(hm.astype(BF16), wo_ref[...])
    x2 = x_ref[0] + mod_ref[0][2:3, :] * y
    ms = jnp.mean(x2 * x2, axis=-1, keepdims=True)
    o_ref[0] = x2 * lax.rsqrt(ms + EPS) * fw_ref[...]


def _mlstm_out(hdir, og, x, mod3, mod_row, wo_bf, final_w):
    bsz, t, d = x.shape
    dm = og.shape[-1]
    tm = ROW_TILE
    tok = lambda b, i: (b, i, 0)
    return pl.pallas_call(
        _mlstm_out_kernel,
        out_shape=jax.ShapeDtypeStruct((bsz, t, d), F32),
        grid=(bsz, t // tm),
        in_specs=[pl.BlockSpec((2, 1, dm, tm), lambda b, i: (0, b, 0, i)),
                  pl.BlockSpec((1, tm, dm), tok),
                  pl.BlockSpec((1, tm, d), tok),
                  pl.BlockSpec((1, 3, d), lambda b, i: (mod_row(b), 0, 0)),
                  pl.BlockSpec(wo_bf.shape, lambda b, i: (0, 0)),
                  pl.BlockSpec((1, d), lambda b, i: (0, 0))],
        out_specs=pl.BlockSpec((1, tm, d), tok),
        compiler_params=_cparams(("parallel", "parallel")),
        name="mlstm_out",
    )(hdir, og, x, mod3, wo_bf, final_w.reshape(1, d))


def _rope_tables(t):
    nf = HEAD_DIM // 4
    pos = jnp.arange(t)
    row = (pos // GRID_W).astype(F32)
    col = (pos % GRID_W).astype(F32)
    inv = ROPE_BASE ** (-jnp.arange(nf, dtype=F32) / nf)
    ar = row[:, None] * inv[None, :]
    ac = col[:, None] * inv[None, :]
    cos = jnp.concatenate([jnp.cos(ar), jnp.cos(ar), jnp.cos(ac), jnp.cos(ac)], axis=1)
    sin = jnp.concatenate([-jnp.sin(ar), jnp.sin(ar), -jnp.sin(ac), jnp.sin(ac)], axis=1)
    reps = LANES // HEAD_DIM
    return jnp.tile(cos, (1, reps)), jnp.tile(sin, (1, reps))


def kernel(x_prompt, x_sample, cache_k, cache_v, state_C, state_n, state_m, c, c_ctx,
           attn_norm_w, attn_ada_w, attn_ada_b, attn_w_in, attn_sink, attn_w_out,
           mlstm_norm_w, mlstm_ada_w, mlstm_ada_b, mlstm_w_in, mlstm_b_gates, mlstm_w_out,
           final_norm_w):
    assert attn_w_in.shape[0] == 1 and mlstm_w_in.shape[0] == 1, "one layer of each mixer"
    bsz, seq, d = x_prompt.shape
    dbsz, dseq, _ = x_sample.shape
    dkv = N_KV_HEADS * HEAD_DIM
    dm = M_HEADS * M_HD

    n_cond = 1 + dbsz
    cond = jnp.concatenate([c_ctx[None, :], c, jnp.zeros((-n_cond % 8, d), F32)], axis=0)
    attn_mod = _ada(cond, attn_ada_w[0], attn_ada_b[0]).reshape(-1, 3, d)
    mlstm_mod = _ada(cond, mlstm_ada_w[0], mlstm_ada_b[0]).reshape(-1, 3, d)
    ctx_row = lambda b: 0
    lat_row = lambda b: b + 1

    attn_w_in_bf = attn_w_in[0].astype(BF16)
    attn_w_out_bf = attn_w_out[0].astype(BF16)
    attn_wvt_bf = attn_w_in[0, :, 2 * N_HEADS * HEAD_DIM + dkv:].T.astype(BF16)
    w_main_bf = jnp.concatenate([mlstm_w_in[0, :, :2 * dm], mlstm_w_in[0, :, 3 * dm:5 * dm]],
                                axis=1).astype(BF16)
    mlstm_wvt_bf = mlstm_w_in[0, :, 2 * dm:3 * dm].T.astype(BF16)
    wgt_bf = mlstm_w_in[0, :, 5 * dm:].T.astype(BF16)
    mlstm_w_out_bf = mlstm_w_out[0].astype(BF16)

    def mlstm_layer(x, mod_row, init, write_state):
        q, k, vt, og, gc, gr = _mlstm_in(x, mlstm_mod, mod_row, mlstm_norm_w[0], w_main_bf,
                                         mlstm_wvt_bf, wgt_bf, mlstm_b_gates[0])
        outs = _mlstm_scan(q, k, vt, gc, gr, init, write_state)
        y = _mlstm_out(outs[0], og, x, mlstm_mod, mod_row, mlstm_w_out_bf, final_norm_w)
        return y, outs[1:]

    q, sg, k_ctx, vt_ctx, v_ctx = _attn_in(x_prompt, attn_mod, ctx_row, attn_norm_w[0], attn_w_in_bf,
                                           attn_wvt_bf, None, F32, True)
    x1 = _attn(q, sg, x_prompt, attn_mod, ctx_row, k_ctx, vt_ctx, None, None, attn_sink[0], attn_w_out_bf)
    y_prompt, (c_fin, n_fin, m_fin) = mlstm_layer(x1, ctx_row, None, True)

    q, sg, k_lat, vt_lat = _attn_in(x_sample, attn_mod, lat_row, attn_norm_w[0], attn_w_in_bf,
                                    attn_wvt_bf, _rope_tables(dseq), BF16, False)
    kc = cache_k[:, 0].reshape(dbsz, -1, dkv).astype(BF16)
    vct = jnp.swapaxes(cache_v[:, 0].reshape(dbsz, -1, dkv), 1, 2).astype(BF16)
    x1 = _attn(q, sg, x_sample, attn_mod, lat_row, kc, vct, k_lat, vt_lat, attn_sink[0], attn_w_out_bf)
    y_sample, _ = mlstm_layer(x1, lat_row, (state_C[:, 0], state_n[:, 0], state_m[:, 0]), False)

    new_cache_k = k_ctx.reshape(bsz, 1, seq, N_KV_HEADS, HEAD_DIM)
    new_cache_v = v_ctx.reshape(bsz, 1, seq, N_KV_HEADS, HEAD_DIM)
    return (y_prompt, y_sample, new_cache_k, new_cache_v,
            c_fin[:, None], n_fin[:, None], m_fin[:, None, :, :, 0])
```

```python
import functools

import jax
import jax.numpy as jnp
from jax import lax
from jax.experimental import pallas as pl
from jax.experimental.pallas import tpu as pltpu

F32 = jnp.float32
BF16 = jnp.bfloat16

HEAD_DIM = 64
N_KV_HEADS = 4
GROUP = 4
N_HEADS = N_KV_HEADS * GROUP
QBLK = 128
GRID_W = 64
ROPE_BASE = 10000.0
M_HEADS = 8
M_HD = 128
EPS = 1e-6

LANES = 128
VMEM_LIMIT = 48 * 1024 * 1024

MCHUNK = 128
ATTN_AHEAD = 4
SCAN_AHEAD = 4
ROW_TILE = 256

NEG_INF = float("-inf")
LOG2E = 1.4426950408889634


def _cparams(sem):
    return pltpu.CompilerParams(dimension_semantics=sem, vmem_limit_bytes=VMEM_LIMIT)


def _silu(x):
    return x * jax.nn.sigmoid(x)


def _log_sigmoid(x):
    return jnp.minimum(x, 0.0) - jnp.log1p(jnp.exp(-jnp.abs(x)))


def _dot(a, b):
    return jnp.dot(a, b, preferred_element_type=F32)


def _dot_nt(a, b):
    return lax.dot_general(a, b, (((1,), (1,)), ((), ())), preferred_element_type=F32)


def _dot_tn(a, b):
    return lax.dot_general(a, b, (((0,), (0,)), ((), ())), preferred_element_type=F32)


def _split3(x):
    hi = x.astype(BF16)
    r = x - hi.astype(F32)
    mid = r.astype(BF16)
    lo = (r - mid.astype(F32)).astype(BF16)
    return hi, mid, lo


def _prenorm(x, norm_w, mod):
    ms = jnp.mean(x * x, axis=-1, keepdims=True)
    y = x * lax.rsqrt(ms + EPS) * norm_w
    return y * (1.0 + mod[1:2, :]) + mod[0:1, :]


def _ada_kernel(cond_ref, w_ref, b_ref, o_ref):
    a = _silu(cond_ref[...]).astype(BF16)
    o_ref[...] = _dot(a, w_ref[...].astype(BF16)) + b_ref[...]


def _ada(cond8, w, b):
    d, n = w.shape
    tn = 512
    return pl.pallas_call(
        _ada_kernel,
        out_shape=jax.ShapeDtypeStruct((cond8.shape[0], n), F32),
        grid=(n // tn,),
        in_specs=[pl.BlockSpec(cond8.shape, lambda j: (0, 0)),
                  pl.BlockSpec((d, tn), lambda j: (0, j)),
                  pl.BlockSpec((1, tn), lambda j: (0, j))],
        out_specs=pl.BlockSpec((cond8.shape[0], tn), lambda j: (0, j)),
        compiler_params=_cparams(("parallel",)),
        name="ada_mod",
    )(cond8, w, b.reshape(1, n))


def _rope(x, cos, sin, lane):
    first = (lane & 31) < 16
    outs = []
    for c in range(x.shape[1] // LANES):
        xc = x[:, c * LANES:(c + 1) * LANES]
        sw = jnp.where(first, pltpu.roll(xc, LANES - 16, 1), pltpu.roll(xc, 16, 1))
        outs.append(xc * cos + sw * sin)
    return jnp.concatenate(outs, axis=1)


def _attn_in_kernel(*refs, rope, emit_v):
    refs = list(refs)
    x_ref, mod_ref, nw_ref, w_ref, wvt_ref = refs[:5]
    pos = 5
    if rope:
        cos_ref, sin_ref = refs[pos:pos + 2]
        pos += 2
    q_ref, sg_ref, k_ref, vt_ref = refs[pos:pos + 4]
    dq = q_ref.shape[-1]
    dkv = k_ref.shape[-1]
    hb = _prenorm(x_ref[0], nw_ref[...], mod_ref[0]).astype(BF16)
    q = _dot(hb, w_ref[:, 0:dq])
    g = _dot(hb, w_ref[:, dq:2 * dq])
    k = _dot(hb, w_ref[:, 2 * dq:2 * dq + dkv])
    if rope:
        cos = cos_ref[...]
        sin = sin_ref[...]
        lane = lax.broadcasted_iota(jnp.int32, cos.shape, 1)
        q = _rope(q, cos, sin, lane)
        k = _rope(k, cos, sin, lane)
    q_ref[0] = (q * (HEAD_DIM ** -0.5 * LOG2E)).astype(q_ref.dtype)
    sg_ref[0] = _silu(g).astype(sg_ref.dtype)
    k_ref[0] = k.astype(k_ref.dtype)
    vt_ref[0] = _dot_nt(wvt_ref[...], hb).astype(vt_ref.dtype)
    if emit_v:
        v_ref = refs[pos + 4]
        v_ref[0] = _dot(hb, w_ref[:, 2 * dq + dkv:2 * dq + 2 * dkv]).astype(v_ref.dtype)


def _attn_in(x, mod3, mod_row, norm_w, w_bf, wvt_bf, rope_tabs, k_dtype, emit_v):
    bsz, t, d = x.shape
    dq = N_HEADS * HEAD_DIM
    dkv = N_KV_HEADS * HEAD_DIM
    tm = min(2 * ROW_TILE, t)
    rope = rope_tabs is not None
    tok = lambda b, i: (b, i, 0)
    const = lambda b, i: (0, 0)
    in_specs = [pl.BlockSpec((1, tm, d), tok),
                pl.BlockSpec((1, 3, d), lambda b, i: (mod_row(b), 0, 0)),
                pl.BlockSpec((1, d), const),
                pl.BlockSpec(w_bf.shape, const),
                pl.BlockSpec(wvt_bf.shape, const)]
    args = [x, mod3, norm_w.reshape(1, d), w_bf, wvt_bf]
    if rope:
        in_specs += [pl.BlockSpec((tm, LANES), lambda b, i: (i, 0))] * 2
        args += list(rope_tabs)
    out_shape = [jax.ShapeDtypeStruct((bsz, t, dq), BF16),
                 jax.ShapeDtypeStruct((bsz, t, dq), BF16),
                 jax.ShapeDtypeStruct((bsz, t, dkv), k_dtype),
                 jax.ShapeDtypeStruct((bsz, dkv, t), BF16)]
    out_specs = [pl.BlockSpec((1, tm, dq), tok), pl.BlockSpec((1, tm, dq), tok),
                 pl.BlockSpec((1, tm, dkv), tok),
                 pl.BlockSpec((1, dkv, tm), lambda b, i: (b, 0, i))]
    if emit_v:
        out_shape.append(jax.ShapeDtypeStruct((bsz, t, dkv), F32))
        out_specs.append(pl.BlockSpec((1, tm, dkv), tok))
    return pl.pallas_call(
        functools.partial(_attn_in_kernel, rope=rope, emit_v=emit_v),
        out_shape=tuple(out_shape),
        grid=(bsz, t // tm),
        in_specs=in_specs,
        out_specs=tuple(out_specs),
        compiler_params=_cparams(("parallel", "parallel")),
        name="attn_in_rope" if rope else "attn_in",
    )(*args)


def _attn_kernel(*refs, window, nb):
    if window:
        (q_ref, sg_ref, x_ref, mod_ref, kc_ref, vct_ref, kp_ref, kq_ref, kn_ref,
         vpt_ref, vqt_ref, vnt_ref, sink_ref, wo_ref, o_ref, s_scr, p_scr, ot_scr) = refs
    else:
        q_ref, sg_ref, x_ref, mod_ref, kc_ref, vct_ref, sink_ref, wo_ref, o_ref, s_scr, p_scr, ot_scr = refs
    blk = pl.program_id(1)
    q = q_ref[0]
    n_ctx = kc_ref.shape[1] // QBLK
    cols = GROUP * QBLK
    if window:
        kj = lax.broadcasted_iota(jnp.int32, (QBLK, cols), 0)
        qi = lax.broadcasted_iota(jnp.int32, (QBLK, cols), 1) & (QBLK - 1)
        prev_ok = (kj >= qi) & (blk > 0)
        next_ok = (kj <= qi) & (blk < nb - 1)
    ones_rows = jnp.where(lax.broadcasted_iota(jnp.int32, (16, QBLK), 0) == 0, 1.0, 0.0).astype(BF16)
    masks = [None] * n_ctx + ([prev_ok, None, next_ok] if window else [])
    n_blk = len(masks)

    def scores(kvh):
        cs = slice(kvh * HEAD_DIM, (kvh + 1) * HEAD_DIM)
        heads = [kvh * GROUP + j for j in range(GROUP)]
        q4 = jnp.concatenate([q[:, h * HEAD_DIM:(h + 1) * HEAD_DIM] for h in heads], axis=0)
        sink_row = jnp.concatenate(
            [jnp.broadcast_to(sink_ref[0:1, h:h + 1], (1, QBLK)) for h in heads], axis=1) * LOG2E
        keys = [kc_ref[0, j * QBLK:(j + 1) * QBLK, cs].astype(BF16) for j in range(n_ctx)]
        if window:
            keys += [kp_ref[0][:, cs], kq_ref[0][:, cs], kn_ref[0][:, cs]]
        st_all = _dot_nt(jnp.concatenate(keys, axis=0), q4)
        macc = jnp.full((8, cols), NEG_INF, F32)
        for j, ok in enumerate(masks):
            st = st_all[j * QBLK:(j + 1) * QBLK, :]
            if ok is not None:
                st = jnp.where(ok, st, NEG_INF)
            s_scr[kvh, j] = st
            macc = jnp.maximum(macc, jnp.max(st.reshape(QBLK // 8, 8, cols), axis=0))
        return jnp.maximum(jnp.max(macc, axis=0, keepdims=True), sink_row), sink_row

    def weighted_values(kvh, m_row, sink_row):
        cs = slice(kvh * HEAD_DIM, (kvh + 1) * HEAD_DIM)
        for j in range(n_blk):
            p_scr[kvh, j * QBLK:(j + 1) * QBLK, :] = jnp.exp2(s_scr[kvh, j] - m_row).astype(BF16)
        vts = [vct_ref[0, cs, j * QBLK:(j + 1) * QBLK] for j in range(n_ctx)]
        if window:
            vts += [vpt_ref[0, cs, :], vqt_ref[0, cs, :], vnt_ref[0, cs, :]]
        vt_ext = jnp.concatenate(
            [jnp.concatenate(vts, axis=1), jnp.tile(ones_rows, (1, n_blk))], axis=0)
        acc = _dot(vt_ext, p_scr[kvh])
        den = acc[HEAD_DIM:HEAD_DIM + 1, :] + jnp.exp2(sink_row - m_row)
        o_t = acc[0:HEAD_DIM, :] / den
        for j in range(GROUP):
            h = kvh * GROUP + j
            ot_scr[h * HEAD_DIM:(h + 1) * HEAD_DIM, :] = o_t[:, j * QBLK:(j + 1) * QBLK]

    pending = [scores(kvh) for kvh in range(min(ATTN_AHEAD, N_KV_HEADS))]
    for kvh in range(N_KV_HEADS):
        if kvh + ATTN_AHEAD < N_KV_HEADS:
            pending.append(scores(kvh + ATTN_AHEAD))
        weighted_values(kvh, *pending.pop(0))
    z = (ot_scr[...].T * sg_ref[0].astype(F32)).astype(BF16)
    y = _dot(z, wo_ref[...])
    o_ref[0] = x_ref[0] + mod_ref[0][2:3, :] * y


def _attn(q, sg, x, mod3, mod_row, kc, vct, k_lat, vt_lat, sink, wo_bf):
    bsz, t, d = x.shape
    dq = q.shape[-1]
    dkv = kc.shape[-1]
    p_len = kc.shape[1]
    nb = t // QBLK
    window = k_lat is not None
    tok = lambda b, i: (b, i, 0)
    in_specs = [pl.BlockSpec((1, QBLK, dq), tok),
                pl.BlockSpec((1, QBLK, dq), tok),
                pl.BlockSpec((1, QBLK, d), tok),
                pl.BlockSpec((1, 3, d), lambda b, i: (mod_row(b), 0, 0)),
                pl.BlockSpec((1, p_len, dkv), lambda b, i: (b, 0, 0)),
                pl.BlockSpec((1, dkv, p_len), lambda b, i: (b, 0, 0))]
    args = [q, sg, x, mod3, kc, vct]
    n_blocks = p_len // QBLK
    if window:
        prev = lambda i: jnp.maximum(i - 1, 0)
        nxt = lambda i: jnp.minimum(i + 1, nb - 1)
        in_specs += [pl.BlockSpec((1, QBLK, dkv), lambda b, i: (b, prev(i), 0)),
                     pl.BlockSpec((1, QBLK, dkv), tok),
                     pl.BlockSpec((1, QBLK, dkv), lambda b, i: (b, nxt(i), 0)),
                     pl.BlockSpec((1, dkv, QBLK), lambda b, i: (b, 0, prev(i))),
                     pl.BlockSpec((1, dkv, QBLK), lambda b, i: (b, 0, i)),
                     pl.BlockSpec((1, dkv, QBLK), lambda b, i: (b, 0, nxt(i)))]
        args += [k_lat] * 3 + [vt_lat] * 3
        n_blocks += 3
    in_specs += [pl.BlockSpec((1, N_HEADS), lambda b, i: (0, 0)),
                 pl.BlockSpec(wo_bf.shape, lambda b, i: (0, 0))]
    args += [sink.reshape(1, N_HEADS), wo_bf]
    return pl.pallas_call(
        functools.partial(_attn_kernel, window=window, nb=nb),
        out_shape=jax.ShapeDtypeStruct((bsz, t, d), F32),
        grid=(bsz, nb),
        in_specs=in_specs,
        out_specs=pl.BlockSpec((1, QBLK, d), tok),
        scratch_shapes=[pltpu.VMEM((N_KV_HEADS, n_blocks, QBLK, GROUP * QBLK), F32),
                        pltpu.VMEM((N_KV_HEADS, n_blocks * QBLK, GROUP * QBLK), BF16),
                        pltpu.VMEM((dq, QBLK), F32)],
        compiler_params=_cparams(("parallel", "parallel")),
        name="attn_window" if window else "attn_ctx",
    )(*args)


def _mlstm_in_kernel(x_ref, mod_ref, nw_ref, w_ref, wvt_ref, wgt_ref, bgt_ref,
                     q_ref, k_ref, vt_ref, og_ref, gc_ref, gr_ref):
    dm = q_ref.shape[-1]
    nh = M_HEADS
    L = MCHUNK
    hb = _prenorm(x_ref[0], nw_ref[...], mod_ref[0]).astype(BF16)

    gr = _dot_nt(wgt_ref[...], hb) + bgt_ref[...]
    n_chunks = x_ref.shape[1] // L
    ri = lax.broadcasted_iota(jnp.int32, (L, L), 0)
    ci = lax.broadcasted_iota(jnp.int32, (L, L), 1)
    g_rows = []
    for dr in range(2):
        before = (ri <= ci) if dr == 0 else (ri >= ci)
        tri = jnp.where(before, 1.0, 0.0).astype(BF16)
        base = dr * 2 * nh
        lf = _log_sigmoid(gr[base + nh:base + 2 * nh, :])
        lf_st = jnp.concatenate([lf[:, c * L:(c + 1) * L] for c in range(n_chunks)], axis=0)
        b_st = sum(_dot(piece, tri) for piece in _split3(lf_st))
        for cidx in range(n_chunks):
            rows = slice(cidx * L, (cidx + 1) * L)
            b_r = b_st[cidx * nh:(cidx + 1) * nh, :]
            g_r = gr[base:base + nh, rows] - b_r
            b_last = jnp.sum(lf[:, rows], axis=1, keepdims=True)
            g_max = jnp.max(g_r, axis=1, keepdims=True)
            g_rows.append(g_r)
            gr_ref[0, dr, :, rows] = jnp.concatenate(
                [g_r, b_r, jnp.broadcast_to(b_last, (nh, L)), jnp.broadcast_to(g_max, (nh, L))], axis=0)
    g_sq = jnp.concatenate(g_rows + [jnp.zeros((L - len(g_rows) * nh, L), F32)], axis=0).T
    for dr in range(2):
        for cidx in range(n_chunks):
            idx = dr * n_chunks + cidx
            gc_ref[0, dr, cidx * L:(cidx + 1) * L, :] = g_sq[:, idx * nh:(idx + 1) * nh]

    q_ref[0] = _dot(hb, w_ref[:, 0:dm]).astype(q_ref.dtype)
    k_ref[0] = (_dot(hb, w_ref[:, dm:2 * dm]) * (M_HD ** -0.5)).astype(k_ref.dtype)
    vt_ref[0] = _dot_nt(wvt_ref[...], hb).astype(vt_ref.dtype)
    o = _dot(hb, w_ref[:, 2 * dm:3 * dm])
    g = _dot(hb, w_ref[:, 3 * dm:4 * dm])
    og_ref[0] = (jax.nn.sigmoid(o) * _silu(g)).astype(og_ref.dtype)


def _mlstm_in(x, mod3, mod_row, norm_w, w_main_bf, wvt_bf, wgt_bf, b_gates):
    bsz, t, d = x.shape
    dm = M_HEADS * M_HD
    ng = 4 * M_HEADS
    tm = min(2 * ROW_TILE, t)
    tok = lambda b, i: (b, i, 0)
    const = lambda b, i: (0, 0)
    big = jax.ShapeDtypeStruct((bsz, t, dm), BF16)
    return pl.pallas_call(
        _mlstm_in_kernel,
        out_shape=(big, big, jax.ShapeDtypeStruct((bsz, dm, t), BF16), big,
                   jax.ShapeDtypeStruct((bsz, 2, t, M_HEADS), F32),
                   jax.ShapeDtypeStruct((bsz, 2, 4 * M_HEADS, t), F32)),
        grid=(bsz, t // tm),
        in_specs=[pl.BlockSpec((1, tm, d), tok),
                  pl.BlockSpec((1, 3, d), lambda b, i: (mod_row(b), 0, 0)),
                  pl.BlockSpec((1, d), const),
                  pl.BlockSpec(w_main_bf.shape, const),
                  pl.BlockSpec(wvt_bf.shape, const),
                  pl.BlockSpec((ng, d), const),
                  pl.BlockSpec((ng, 1), const)],
        out_specs=(pl.BlockSpec((1, tm, dm), tok), pl.BlockSpec((1, tm, dm), tok),
                   pl.BlockSpec((1, dm, tm), lambda b, i: (b, 0, i)), pl.BlockSpec((1, tm, dm), tok),
                   pl.BlockSpec((1, 2, tm, M_HEADS), lambda b, i: (b, 0, i, 0)),
                   pl.BlockSpec((1, 2, 4 * M_HEADS, tm), lambda b, i: (b, 0, 0, i))),
        compiler_params=_cparams(("parallel", "parallel")),
        name="mlstm_in",
    )(x, mod3, norm_w.reshape(1, d), w_main_bf, wvt_bf, wgt_bf, b_gates.reshape(ng, 1))


def _mlstm_scan_kernel(*refs, has_init, write_state, nc):
    refs = list(refs)
    q_ref, k_ref, vt_ref, gc_ref, gr_ref = refs[:5]
    pos = 5
    if has_init:
        c0_ref, n0_ref, m0_ref = refs[pos:pos + 3]
        pos += 3
    ht_ref = refs[pos]
    pos += 1
    if write_state:
        cout_ref, nout_ref, mout_ref = refs[pos:pos + 3]
        pos += 3
    ct_scr, mscr, hcur, hfwd = refs[pos:pos + 4]

    drn = pl.program_id(1)
    c = pl.program_id(2)
    L = q_ref.shape[1]
    nh = M_HEADS
    pad = ct_scr.shape[1] - M_HD

    @pl.when(c == 0)
    def _init():
        if has_init:
            for h in range(nh):
                ct_scr[h, 0:M_HD, :] = c0_ref[0, 0, h].T
                ct_scr[h, M_HD:M_HD + pad, :] = jnp.concatenate(
                    [n0_ref[0, 0, h:h + 1, :], jnp.zeros((pad - 1, M_HD), F32)], axis=0)
            mscr[...] = m0_ref[0, 0]
        else:
            ct_scr[...] = jnp.zeros(ct_scr.shape, F32)
            mscr[...] = jnp.zeros(mscr.shape, F32)

    si = lax.broadcasted_iota(jnp.int32, (L, L), 0)
    li = lax.broadcasted_iota(jnp.int32, (L, L), 1)
    seen_t = (si - li) * (1 - 2 * drn) <= 0

    gcb = gc_ref[0, 0]
    grb = gr_ref[0, 0]
    q = q_ref[0]
    k = k_ref[0]
    vt = vt_ref[0]
    ones_rows = jnp.where(lax.broadcasted_iota(jnp.int32, (pad, L), 0) == 0, 1.0, 0.0).astype(BF16)

    def head_scores(h):
        hs = slice(h * M_HD, (h + 1) * M_HD)
        m_prev = mscr[h:h + 1, 0:1]
        ct = ct_scr[h]
        a_t = jnp.where(seen_t, gcb[:, h:h + 1], NEG_INF)
        m_row = jnp.maximum(jnp.max(a_t, axis=0, keepdims=True), m_prev)
        w_t = jnp.exp(a_t - m_row)
        r1 = _dot_nt(jnp.concatenate([k[:, hs], ct.astype(BF16)], axis=0), q[:, hs])
        s_t = (r1[0:L, :] * w_t).astype(BF16)
        return m_prev, ct, m_row, s_t, r1[L:, :]

    def head_finish(h, m_prev, ct, m_row, s_t, inter):
        hs = slice(h * M_HD, (h + 1) * M_HD)
        vext = jnp.concatenate([vt[hs, :], ones_rows], axis=0)
        g_r = grb[h:h + 1, :]
        b_r = grb[nh + h:nh + h + 1, :]
        b_last = grb[2 * nh + h:2 * nh + h + 1, 0:1]
        g_max = grb[3 * nh + h:3 * nh + h + 1, 0:1]
        w0 = jnp.exp(m_prev - m_row)
        tot = _dot(vext, s_t) + w0 * inter
        den = tot[M_HD:M_HD + 1, :]
        floor = jnp.exp(-(b_r + m_row))
        hcur[hs, :] = tot[0:M_HD, :] / jnp.maximum(jnp.abs(den), floor)

        m_last = jnp.maximum(g_max, m_prev)
        wk = jnp.exp(g_r - m_last)
        decay = jnp.exp(m_prev - m_last)
        vw = (vext.astype(F32) * wk).astype(BF16)
        ct_scr[h] = decay * ct + _dot(vw, k[:, hs])
        mscr[h:h + 1, :] = jnp.broadcast_to(b_last + m_last, (1, LANES))

    pending = [head_scores(h) for h in range(min(SCAN_AHEAD, nh))]
    for h in range(nh):
        if h + SCAN_AHEAD < nh:
            pending.append(head_scores(h + SCAN_AHEAD))
        head_finish(h, *pending.pop(0))

    @pl.when(drn == 0)
    def _park():
        hfwd[c] = hcur[...]

    @pl.when(drn == 1)
    def _emit():
        ht_ref[0] = (hcur[...] + hfwd[nc - 1 - c]).astype(ht_ref.dtype)

    if write_state:
        @pl.when(c == nc - 1)
        def _final():
            for h in range(nh):
                cfin = ct_scr[h]
                cout_ref[0, 0, h] = cfin[0:M_HD, :].T
                nout_ref[0, 0, h:h + 1, :] = cfin[M_HD:M_HD + 1, :]
            mout_ref[0, 0] = mscr[...]


def _mlstm_scan(q, k, vt, gc, gr, init, write_state):
    bsz, t, dm = q.shape
    L = MCHUNK
    nc = t // L
    chunk = lambda b, d, c: c + d * (nc - 1 - 2 * c)
    tok = lambda b, d, c: (b, chunk(b, d, c), 0)
    in_specs = [pl.BlockSpec((1, L, dm), tok)] * 2 + [
        pl.BlockSpec((1, dm, L), lambda b, d, c: (b, 0, chunk(b, d, c))),
        pl.BlockSpec((1, 1, L, gc.shape[-1]), lambda b, d, c: (b, d, chunk(b, d, c), 0)),
        pl.BlockSpec((1, 1, gr.shape[2], L), lambda b, d, c: (b, d, 0, chunk(b, d, c)))]
    args = [q, k, vt, gc, gr]
    st = lambda b, d, c: (b, d, 0, 0)
    st5 = lambda b, d, c: (b, d, 0, 0, 0)
    if init is not None:
        c0, n0, m0 = init
        in_specs += [pl.BlockSpec((1, 1, M_HEADS, M_HD, M_HD), st5),
                     pl.BlockSpec((1, 1, M_HEADS, M_HD), st),
                     pl.BlockSpec((1, 1, M_HEADS, LANES), st)]
        args += [c0, n0, jnp.broadcast_to(m0[..., None], m0.shape + (LANES,))]
    out_shape = [jax.ShapeDtypeStruct((bsz, dm, t), BF16)]
    out_specs = [pl.BlockSpec((1, dm, L), lambda b, d, c: (b, 0, nc - 1 - d * c))]
    if write_state:
        out_shape += [jax.ShapeDtypeStruct((bsz, 2, M_HEADS, M_HD, M_HD), F32),
                      jax.ShapeDtypeStruct((bsz, 2, M_HEADS, M_HD), F32),
                      jax.ShapeDtypeStruct((bsz, 2, M_HEADS, LANES), F32)]
        out_specs += [pl.BlockSpec((1, 1, M_HEADS, M_HD, M_HD), st5),
                      pl.BlockSpec((1, 1, M_HEADS, M_HD), st),
                      pl.BlockSpec((1, 1, M_HEADS, LANES), st)]
    return pl.pallas_call(
        functools.partial(_mlstm_scan_kernel, has_init=init is not None,
                          write_state=write_state, nc=nc),
        out_shape=tuple(out_shape),
        grid=(bsz, 2, nc),
        in_specs=in_specs,
        out_specs=tuple(out_specs),
        scratch_shapes=[pltpu.VMEM((M_HEADS, M_HD + 16, M_HD), F32),
                        pltpu.VMEM((M_HEADS, LANES), F32),
                        pltpu.VMEM((dm, L), F32),
                        pltpu.VMEM((nc, dm, L), F32)],
        compiler_params=_cparams(("parallel", "arbitrary", "arbitrary")),
        name="mlstm_scan",
    )(*args)


def _mlstm_out_kernel(ht_ref, og_ref, x_ref, mod_ref, wo_ref, fw_ref, o_ref):
    hm = ht_ref[0].astype(F32).T * og_ref[0].astype(F32)
    y = _dot(hm.astype(BF16), wo_ref[...])
    x2 = x_ref[0] + mod_ref[0][2:3, :] * y
    ms = jnp.mean(x2 * x2, axis=-1, keepdims=True)
    o_ref[0] = x2 * lax.rsqrt(ms + EPS) * fw_ref[...]


def _mlstm_out(ht, og, x, mod3, mod_row, wo_bf, final_w):
    bsz, t, d = x.shape
    dm = og.shape[-1]
    tm = ROW_TILE
    tok = lambda b, i: (b, i, 0)
    return pl.pallas_call(
        _mlstm_out_kernel,
        out_shape=jax.ShapeDtypeStruct((bsz, t, d), F32),
        grid=(bsz, t // tm),
        in_specs=[pl.BlockSpec((1, dm, tm), lambda b, i: (b, 0, i)),
                  pl.BlockSpec((1, tm, dm), tok),
                  pl.BlockSpec((1, tm, d), tok),
                  pl.BlockSpec((1, 3, d), lambda b, i: (mod_row(b), 0, 0)),
                  pl.BlockSpec(wo_bf.shape, lambda b, i: (0, 0)),
                  pl.BlockSpec((1, d), lambda b, i: (0, 0))],
        out_specs=pl.BlockSpec((1, tm, d), tok),
        compiler_params=_cparams(("parallel", "parallel")),
        name="mlstm_out",
    )(ht, og, x, mod3, wo_bf, final_w.reshape(1, d))


def _rope_tables(t):
    nf = HEAD_DIM // 4
    pos = jnp.arange(t)
    row = (pos // GRID_W).astype(F32)
    col = (pos % GRID_W).astype(F32)
    inv = ROPE_BASE ** (-jnp.arange(nf, dtype=F32) / nf)
    ar = row[:, None] * inv[None, :]
    ac = col[:, None] * inv[None, :]
    cos = jnp.concatenate([jnp.cos(ar), jnp.cos(ar), jnp.cos(ac), jnp.cos(ac)], axis=1)
    sin = jnp.concatenate([-jnp.sin(ar), jnp.sin(ar), -jnp.sin(ac), jnp.sin(ac)], axis=1)
    reps = LANES // HEAD_DIM
    return jnp.tile(cos, (1, reps)), jnp.tile(sin, (1, reps))


def kernel(x_prompt, x_sample, cache_k, cache_v, state_C, state_n, state_m, c, c_ctx,
           attn_norm_w, attn_ada_w, attn_ada_b, attn_w_in, attn_sink, attn_w_out,
           mlstm_norm_w, mlstm_ada_w, mlstm_ada_b, mlstm_w_in, mlstm_b_gates, mlstm_w_out,
           final_norm_w):
    assert attn_w_in.shape[0] == 1 and mlstm_w_in.shape[0] == 1, "one layer of each mixer"
    bsz, seq, d = x_prompt.shape
    dbsz, dseq, _ = x_sample.shape
    dkv = N_KV_HEADS * HEAD_DIM
    dm = M_HEADS * M_HD

    n_cond = 1 + dbsz
    cond = jnp.concatenate([c_ctx[None, :], c, jnp.zeros((-n_cond % 8, d), F32)], axis=0)
    attn_mod = _ada(cond, attn_ada_w[0], attn_ada_b[0]).reshape(-1, 3, d)
    mlstm_mod = _ada(cond, mlstm_ada_w[0], mlstm_ada_b[0]).reshape(-1, 3, d)
    ctx_row = lambda b: 0
    lat_row = lambda b: b + 1

    attn_w_in_bf = attn_w_in[0].astype(BF16)
    attn_w_out_bf = attn_w_out[0].astype(BF16)
    attn_wvt_bf = attn_w_in[0, :, 2 * N_HEADS * HEAD_DIM + dkv:].T.astype(BF16)
    w_main_bf = jnp.concatenate([mlstm_w_in[0, :, :2 * dm], mlstm_w_in[0, :, 3 * dm:5 * dm]],
                                axis=1).astype(BF16)
    mlstm_wvt_bf = mlstm_w_in[0, :, 2 * dm:3 * dm].T.astype(BF16)
    wgt_bf = mlstm_w_in[0, :, 5 * dm:].T.astype(BF16)
    mlstm_w_out_bf = mlstm_w_out[0].astype(BF16)

    def mlstm_layer(x, mod_row, init, write_state):
        q, k, vt, og, gc, gr = _mlstm_in(x, mlstm_mod, mod_row, mlstm_norm_w[0], w_main_bf,
                                         mlstm_wvt_bf, wgt_bf, mlstm_b_gates[0])
        outs = _mlstm_scan(q, k, vt, gc, gr, init, write_state)
        y = _mlstm_out(outs[0], og, x, mlstm_mod, mod_row, mlstm_w_out_bf, final_norm_w)
        return y, outs[1:]

    q, sg, k_ctx, vt_ctx, v_ctx = _attn_in(x_prompt, attn_mod, ctx_row, attn_norm_w[0], attn_w_in_bf,
                                           attn_wvt_bf, None, F32, True)
    x1 = _attn(q, sg, x_prompt, attn_mod, ctx_row, k_ctx, vt_ctx, None, None, attn_sink[0], attn_w_out_bf)
    y_prompt, (c_fin, n_fin, m_fin) = mlstm_layer(x1, ctx_row, None, True)

    q, sg, k_lat, vt_lat = _attn_in(x_sample, attn_mod, lat_row, attn_norm_w[0], attn_w_in_bf,
                                    attn_wvt_bf, _rope_tables(dseq), BF16, False)
    kc = cache_k[:, 0].reshape(dbsz, -1, dkv).astype(BF16)
    vct = jnp.swapaxes(cache_v[:, 0].reshape(dbsz, -1, dkv), 1, 2).astype(BF16)
    x1 = _attn(q, sg, x_sample, attn_mod, lat_row, kc, vct, k_lat, vt_lat, attn_sink[0], attn_w_out_bf)
    y_sample, _ = mlstm_layer(x1, lat_row, (state_C[:, 0], state_n[:, 0], state_m[:, 0]), False)

    new_cache_k = k_ctx.reshape(bsz, 1, seq, N_KV_HEADS, HEAD_DIM)
    new_cache_v = v_ctx.reshape(bsz, 1, seq, N_KV_HEADS, HEAD_DIM)
    return (y_prompt, y_sample, new_cache_k, new_cache_v,
            c_fin[:, None], n_fin[:, None], m_fin[:, None, :, :, 0])
```

```python
import functools

import jax
import jax.numpy as jnp
from jax import lax
from jax.experimental import pallas as pl
from jax.experimental.pallas import tpu as pltpu

F32 = jnp.float32
BF16 = jnp.bfloat16

HEAD_DIM = 64
N_KV_HEADS = 4
GROUP = 4
N_HEADS = N_KV_HEADS * GROUP
QBLK = 128
GRID_W = 64
ROPE_BASE = 10000.0
M_HEADS = 8
M_HD = 128
EPS = 1e-6

LANES = 128
VMEM_LIMIT = 48 * 1024 * 1024

MCHUNK = 256
ATTN_AHEAD = 4
SCAN_AHEAD = 4
ROW_TILE = 256

NEG_INF = float("-inf")
LOG2E = 1.4426950408889634


def _cparams(sem):
    return pltpu.CompilerParams(dimension_semantics=sem, vmem_limit_bytes=VMEM_LIMIT)


def _silu(x):
    return x * jax.nn.sigmoid(x)


def _log_sigmoid(x):
    return jnp.minimum(x, 0.0) - jnp.log1p(jnp.exp(-jnp.abs(x)))


def _dot(a, b):
    return jnp.dot(a, b, preferred_element_type=F32)


def _dot_nt(a, b):
    return lax.dot_general(a, b, (((1,), (1,)), ((), ())), preferred_element_type=F32)


def _dot_tn(a, b):
    return lax.dot_general(a, b, (((0,), (0,)), ((), ())), preferred_element_type=F32)


def _split3(x):
    hi = x.astype(BF16)
    r = x - hi.astype(F32)
    mid = r.astype(BF16)
    lo = (r - mid.astype(F32)).astype(BF16)
    return hi, mid, lo


def _prenorm(x, norm_w, mod):
    ms = jnp.mean(x * x, axis=-1, keepdims=True)
    y = x * lax.rsqrt(ms + EPS) * norm_w
    return y * (1.0 + mod[1:2, :]) + mod[0:1, :]


def _ada_kernel(cond_ref, w_ref, b_ref, o_ref):
    a = _silu(cond_ref[...]).astype(BF16)
    o_ref[...] = _dot(a, w_ref[...].astype(BF16)) + b_ref[...]


def _ada(cond8, w, b):
    d, n = w.shape
    tn = 512
    return pl.pallas_call(
        _ada_kernel,
        out_shape=jax.ShapeDtypeStruct((cond8.shape[0], n), F32),
        grid=(n // tn,),
        in_specs=[pl.BlockSpec(cond8.shape, lambda j: (0, 0)),
                  pl.BlockSpec((d, tn), lambda j: (0, j)),
                  pl.BlockSpec((1, tn), lambda j: (0, j))],
        out_specs=pl.BlockSpec((cond8.shape[0], tn), lambda j: (0, j)),
        compiler_params=_cparams(("parallel",)),
        name="ada_mod",
    )(cond8, w, b.reshape(1, n))


def _rope(x, cos, sin, lane):
    first = (lane & 31) < 16
    outs = []
    for c in range(x.shape[1] // LANES):
        xc = x[:, c * LANES:(c + 1) * LANES]
        sw = jnp.where(first, pltpu.roll(xc, LANES - 16, 1), pltpu.roll(xc, 16, 1))
        outs.append(xc * cos + sw * sin)
    return jnp.concatenate(outs, axis=1)


def _attn_in_kernel(*refs, rope, emit_v):
    refs = list(refs)
    x_ref, mod_ref, nw_ref, w_ref, wvt_ref = refs[:5]
    pos = 5
    if rope:
        cos_ref, sin_ref = refs[pos:pos + 2]
        pos += 2
    q_ref, sg_ref, k_ref, vt_ref = refs[pos:pos + 4]
    dq = q_ref.shape[-1]
    dkv = k_ref.shape[-1]
    hb = _prenorm(x_ref[0], nw_ref[...], mod_ref[0]).astype(BF16)
    q = _dot(hb, w_ref[:, 0:dq])
    g = _dot(hb, w_ref[:, dq:2 * dq])
    k = _dot(hb, w_ref[:, 2 * dq:2 * dq + dkv])
    if rope:
        cos = cos_ref[...]
        sin = sin_ref[...]
        lane = lax.broadcasted_iota(jnp.int32, cos.shape, 1)
        q = _rope(q, cos, sin, lane)
        k = _rope(k, cos, sin, lane)
    q_ref[0] = (q * (HEAD_DIM ** -0.5 * LOG2E)).astype(q_ref.dtype)
    sg_ref[0] = _silu(g).astype(sg_ref.dtype)
    k_ref[0] = k.astype(k_ref.dtype)
    vt_ref[0] = _dot_nt(wvt_ref[...], hb).astype(vt_ref.dtype)
    if emit_v:
        v_ref = refs[pos + 4]
        v_ref[0] = _dot(hb, w_ref[:, 2 * dq + dkv:2 * dq + 2 * dkv]).astype(v_ref.dtype)


def _attn_in(x, mod3, mod_row, norm_w, w_bf, wvt_bf, rope_tabs, k_dtype, emit_v):
    bsz, t, d = x.shape
    dq = N_HEADS * HEAD_DIM
    dkv = N_KV_HEADS * HEAD_DIM
    tm = min(2 * ROW_TILE, t)
    rope = rope_tabs is not None
    tok = lambda b, i: (b, i, 0)
    const = lambda b, i: (0, 0)
    in_specs = [pl.BlockSpec((1, tm, d), tok),
                pl.BlockSpec((1, 3, d), lambda b, i: (mod_row(b), 0, 0)),
                pl.BlockSpec((1, d), const),
                pl.BlockSpec(w_bf.shape, const),
                pl.BlockSpec(wvt_bf.shape, const)]
    args = [x, mod3, norm_w.reshape(1, d), w_bf, wvt_bf]
    if rope:
        in_specs += [pl.BlockSpec((tm, LANES), lambda b, i: (i, 0))] * 2
        args += list(rope_tabs)
    out_shape = [jax.ShapeDtypeStruct((bsz, t, dq), BF16),
                 jax.ShapeDtypeStruct((bsz, t, dq), BF16),
                 jax.ShapeDtypeStruct((bsz, t, dkv), k_dtype),
                 jax.ShapeDtypeStruct((bsz, dkv, t), BF16)]
    out_specs = [pl.BlockSpec((1, tm, dq), tok), pl.BlockSpec((1, tm, dq), tok),
                 pl.BlockSpec((1, tm, dkv), tok),
                 pl.BlockSpec((1, dkv, tm), lambda b, i: (b, 0, i))]
    if emit_v:
        out_shape.append(jax.ShapeDtypeStruct((bsz, t, dkv), F32))
        out_specs.append(pl.BlockSpec((1, tm, dkv), tok))
    return pl.pallas_call(
        functools.partial(_attn_in_kernel, rope=rope, emit_v=emit_v),
        out_shape=tuple(out_shape),
        grid=(bsz, t // tm),
        in_specs=in_specs,
        out_specs=tuple(out_specs),
        compiler_params=_cparams(("parallel", "parallel")),
        name="attn_in_rope" if rope else "attn_in",
    )(*args)


def _attn_kernel(*refs, window, nb):
    if window:
        (q_ref, sg_ref, x_ref, mod_ref, kc_ref, vct_ref, kp_ref, kq_ref, kn_ref,
         vpt_ref, vqt_ref, vnt_ref, sink_ref, wo_ref, o_ref, s_scr, p_scr, ot_scr) = refs
    else:
        q_ref, sg_ref, x_ref, mod_ref, kc_ref, vct_ref, sink_ref, wo_ref, o_ref, s_scr, p_scr, ot_scr = refs
    blk = pl.program_id(1)
    q = q_ref[0]
    n_ctx = kc_ref.shape[1] // QBLK
    cols = GROUP * QBLK
    if window:
        kj = lax.broadcasted_iota(jnp.int32, (QBLK, cols), 0)
        qi = lax.broadcasted_iota(jnp.int32, (QBLK, cols), 1) & (QBLK - 1)
        prev_ok = (kj >= qi) & (blk > 0)
        next_ok = (kj <= qi) & (blk < nb - 1)
    ones_rows = jnp.where(lax.broadcasted_iota(jnp.int32, (16, QBLK), 0) == 0, 1.0, 0.0).astype(BF16)
    masks = [None] * n_ctx + ([prev_ok, None, next_ok] if window else [])
    n_blk = len(masks)

    def scores(kvh):
        cs = slice(kvh * HEAD_DIM, (kvh + 1) * HEAD_DIM)
        heads = [kvh * GROUP + j for j in range(GROUP)]
        q4 = jnp.concatenate([q[:, h * HEAD_DIM:(h + 1) * HEAD_DIM] for h in heads], axis=0)
        sink_row = jnp.concatenate(
            [jnp.broadcast_to(sink_ref[0:1, h:h + 1], (1, QBLK)) for h in heads], axis=1) * LOG2E
        keys = [kc_ref[0, j * QBLK:(j + 1) * QBLK, cs].astype(BF16) for j in range(n_ctx)]
        if window:
            keys += [kp_ref[0][:, cs], kq_ref[0][:, cs], kn_ref[0][:, cs]]
        st_all = _dot_nt(jnp.concatenate(keys, axis=0), q4)
        macc = jnp.full((8, cols), NEG_INF, F32)
        for j, ok in enumerate(masks):
            st = st_all[j * QBLK:(j + 1) * QBLK, :]
            if ok is not None:
                st = jnp.where(ok, st, NEG_INF)
            s_scr[kvh, j] = st
            macc = jnp.maximum(macc, jnp.max(st.reshape(QBLK // 8, 8, cols), axis=0))
        return jnp.maximum(jnp.max(macc, axis=0, keepdims=True), sink_row), sink_row

    def weighted_values(kvh, m_row, sink_row):
        cs = slice(kvh * HEAD_DIM, (kvh + 1) * HEAD_DIM)
        for j in range(n_blk):
            p_scr[kvh, j * QBLK:(j + 1) * QBLK, :] = jnp.exp2(s_scr[kvh, j] - m_row).astype(BF16)
        vts = [vct_ref[0, cs, j * QBLK:(j + 1) * QBLK] for j in range(n_ctx)]
        if window:
            vts += [vpt_ref[0, cs, :], vqt_ref[0, cs, :], vnt_ref[0, cs, :]]
        vt_ext = jnp.concatenate(
            [jnp.concatenate(vts, axis=1), jnp.tile(ones_rows, (1, n_blk))], axis=0)
        acc = _dot(vt_ext, p_scr[kvh])
        den = acc[HEAD_DIM:HEAD_DIM + 1, :] + jnp.exp2(sink_row - m_row)
        o_t = acc[0:HEAD_DIM, :] / den
        for j in range(GROUP):
            h = kvh * GROUP + j
            ot_scr[h * HEAD_DIM:(h + 1) * HEAD_DIM, :] = o_t[:, j * QBLK:(j + 1) * QBLK]

    pending = [scores(kvh) for kvh in range(min(ATTN_AHEAD, N_KV_HEADS))]
    for kvh in range(N_KV_HEADS):
        if kvh + ATTN_AHEAD < N_KV_HEADS:
            pending.append(scores(kvh + ATTN_AHEAD))
        weighted_values(kvh, *pending.pop(0))
    z = (ot_scr[...].T * sg_ref[0].astype(F32)).astype(BF16)
    y = _dot(z, wo_ref[...])
    o_ref[0] = x_ref[0] + mod_ref[0][2:3, :] * y


def _attn(q, sg, x, mod3, mod_row, kc, vct, k_lat, vt_lat, sink, wo_bf):
    bsz, t, d = x.shape
    dq = q.shape[-1]
    dkv = kc.shape[-1]
    p_len = kc.shape[1]
    nb = t // QBLK
    window = k_lat is not None
    tok = lambda b, i: (b, i, 0)
    in_specs = [pl.BlockSpec((1, QBLK, dq), tok),
                pl.BlockSpec((1, QBLK, dq), tok),
                pl.BlockSpec((1, QBLK, d), tok),
                pl.BlockSpec((1, 3, d), lambda b, i: (mod_row(b), 0, 0)),
                pl.BlockSpec((1, p_len, dkv), lambda b, i: (b, 0, 0)),
                pl.BlockSpec((1, dkv, p_len), lambda b, i: (b, 0, 0))]
    args = [q, sg, x, mod3, kc, vct]
    n_blocks = p_len // QBLK
    if window:
        prev = lambda i: jnp.maximum(i - 1, 0)
        nxt = lambda i: jnp.minimum(i + 1, nb - 1)
        in_specs += [pl.BlockSpec((1, QBLK, dkv), lambda b, i: (b, prev(i), 0)),
                     pl.BlockSpec((1, QBLK, dkv), tok),
                     pl.BlockSpec((1, QBLK, dkv), lambda b, i: (b, nxt(i), 0)),
                     pl.BlockSpec((1, dkv, QBLK), lambda b, i: (b, 0, prev(i))),
                     pl.BlockSpec((1, dkv, QBLK), lambda b, i: (b, 0, i)),
                     pl.BlockSpec((1, dkv, QBLK), lambda b, i: (b, 0, nxt(i)))]
        args += [k_lat] * 3 + [vt_lat] * 3
        n_blocks += 3
    in_specs += [pl.BlockSpec((1, N_HEADS), lambda b, i: (0, 0)),
                 pl.BlockSpec(wo_bf.shape, lambda b, i: (0, 0))]
    args += [sink.reshape(1, N_HEADS), wo_bf]
    return pl.pallas_call(
        functools.partial(_attn_kernel, window=window, nb=nb),
        out_shape=jax.ShapeDtypeStruct((bsz, t, d), F32),
        grid=(bsz, nb),
        in_specs=in_specs,
        out_specs=pl.BlockSpec((1, QBLK, d), tok),
        scratch_shapes=[pltpu.VMEM((N_KV_HEADS, n_blocks, QBLK, GROUP * QBLK), F32),
                        pltpu.VMEM((N_KV_HEADS, n_blocks * QBLK, GROUP * QBLK), BF16),
                        pltpu.VMEM((dq, QBLK), F32)],
        compiler_params=_cparams(("parallel", "parallel")),
        name="attn_window" if window else "attn_ctx",
    )(*args)


def _mlstm_in_kernel(x_ref, mod_ref, nw_ref, w_ref, wvt_ref, wgt_ref, bgt_ref,
                     q_ref, k_ref, vt_ref, og_ref, gc_ref, gr_ref):
    dm = q_ref.shape[-1]
    nh = M_HEADS
    L = MCHUNK
    hb = _prenorm(x_ref[0], nw_ref[...], mod_ref[0]).astype(BF16)

    gr = _dot_nt(wgt_ref[...], hb) + bgt_ref[...]
    n_chunks = x_ref.shape[1] // L
    ri = lax.broadcasted_iota(jnp.int32, (L, L), 0)
    ci = lax.broadcasted_iota(jnp.int32, (L, L), 1)
    g_rows = []
    for dr in range(2):
        before = (ri <= ci) if dr == 0 else (ri >= ci)
        tri = jnp.where(before, 1.0, 0.0).astype(BF16)
        base = dr * 2 * nh
        lf = _log_sigmoid(gr[base + nh:base + 2 * nh, :])
        lf_st = jnp.concatenate([lf[:, c * L:(c + 1) * L] for c in range(n_chunks)], axis=0)
        b_st = sum(_dot(piece, tri) for piece in _split3(lf_st))
        for cidx in range(n_chunks):
            rows = slice(cidx * L, (cidx + 1) * L)
            b_r = b_st[cidx * nh:(cidx + 1) * nh, :]
            g_r = gr[base:base + nh, rows] - b_r
            b_last = jnp.sum(lf[:, rows], axis=1, keepdims=True)
            g_max = jnp.max(g_r, axis=1, keepdims=True)
            g_rows.append(g_r)
            gr_ref[0, dr, :, rows] = jnp.concatenate(
                [g_r, b_r, jnp.broadcast_to(b_last, (nh, L)), jnp.broadcast_to(g_max, (nh, L))], axis=0)
    g_sq = jnp.concatenate(g_rows + [jnp.zeros((L - len(g_rows) * nh, L), F32)], axis=0).T
    for dr in range(2):
        for cidx in range(n_chunks):
            idx = dr * n_chunks + cidx
            gc_ref[0, dr, cidx * L:(cidx + 1) * L, :] = g_sq[:, idx * nh:(idx + 1) * nh]

    q_ref[0] = _dot(hb, w_ref[:, 0:dm]).astype(q_ref.dtype)
    k_ref[0] = (_dot(hb, w_ref[:, dm:2 * dm]) * (M_HD ** -0.5)).astype(k_ref.dtype)
    vt_ref[0] = _dot_nt(wvt_ref[...], hb).astype(vt_ref.dtype)
    o = _dot(hb, w_ref[:, 2 * dm:3 * dm])
    g = _dot(hb, w_ref[:, 3 * dm:4 * dm])
    og_ref[0] = (jax.nn.sigmoid(o) * _silu(g)).astype(og_ref.dtype)


def _mlstm_in(x, mod3, mod_row, norm_w, w_main_bf, wvt_bf, wgt_bf, b_gates):
    bsz, t, d = x.shape
    dm = M_HEADS * M_HD
    ng = 4 * M_HEADS
    tm = min(2 * ROW_TILE, t)
    tok = lambda b, i: (b, i, 0)
    const = lambda b, i: (0, 0)
    big = jax.ShapeDtypeStruct((bsz, t, dm), BF16)
    return pl.pallas_call(
        _mlstm_in_kernel,
        out_shape=(big, big, jax.ShapeDtypeStruct((bsz, dm, t), BF16), big,
                   jax.ShapeDtypeStruct((bsz, 2, t, M_HEADS), F32),
                   jax.ShapeDtypeStruct((bsz, 2, 4 * M_HEADS, t), F32)),
        grid=(bsz, t // tm),
        in_specs=[pl.BlockSpec((1, tm, d), tok),
                  pl.BlockSpec((1, 3, d), lambda b, i: (mod_row(b), 0, 0)),
                  pl.BlockSpec((1, d), const),
                  pl.BlockSpec(w_main_bf.shape, const),
                  pl.BlockSpec(wvt_bf.shape, const),
                  pl.BlockSpec((ng, d), const),
                  pl.BlockSpec((ng, 1), const)],
        out_specs=(pl.BlockSpec((1, tm, dm), tok), pl.BlockSpec((1, tm, dm), tok),
                   pl.BlockSpec((1, dm, tm), lambda b, i: (b, 0, i)), pl.BlockSpec((1, tm, dm), tok),
                   pl.BlockSpec((1, 2, tm, M_HEADS), lambda b, i: (b, 0, i, 0)),
                   pl.BlockSpec((1, 2, 4 * M_HEADS, tm), lambda b, i: (b, 0, 0, i))),
        compiler_params=_cparams(("parallel", "parallel")),
        name="mlstm_in",
    )(x, mod3, norm_w.reshape(1, d), w_main_bf, wvt_bf, wgt_bf, b_gates.reshape(ng, 1))


def _mlstm_scan_kernel(*refs, has_init, write_state, nc):
    refs = list(refs)
    q_ref, k_ref, vt_ref, gc_ref, gr_ref = refs[:5]
    pos = 5
    if has_init:
        c0_ref, n0_ref, m0_ref = refs[pos:pos + 3]
        pos += 3
    ht_ref = refs[pos]
    pos += 1
    if write_state:
        cout_ref, nout_ref, mout_ref = refs[pos:pos + 3]
        pos += 3
    ct_scr, mscr, hcur, hfwd = refs[pos:pos + 4]

    drn = pl.program_id(1)
    c = pl.program_id(2)
    L = q_ref.shape[1]
    nh = M_HEADS
    pad = ct_scr.shape[1] - M_HD

    @pl.when(c == 0)
    def _init():
        if has_init:
            for h in range(nh):
                ct_scr[h, 0:M_HD, :] = c0_ref[0, 0, h].T
                ct_scr[h, M_HD:M_HD + pad, :] = jnp.concatenate(
                    [n0_ref[0, 0, h:h + 1, :], jnp.zeros((pad - 1, M_HD), F32)], axis=0)
            mscr[...] = m0_ref[0, 0]
        else:
            ct_scr[...] = jnp.zeros(ct_scr.shape, F32)
            mscr[...] = jnp.zeros(mscr.shape, F32)

    si = lax.broadcasted_iota(jnp.int32, (L, L), 0)
    li = lax.broadcasted_iota(jnp.int32, (L, L), 1)
    seen_t = (si - li) * (1 - 2 * drn) <= 0

    gcb = gc_ref[0, 0]
    grb = gr_ref[0, 0]
    q = q_ref[0]
    k = k_ref[0]
    vt = vt_ref[0]
    ones_rows = jnp.where(lax.broadcasted_iota(jnp.int32, (pad, L), 0) == 0, 1.0, 0.0).astype(BF16)

    def head_scores(h):
        hs = slice(h * M_HD, (h + 1) * M_HD)
        m_prev = mscr[h:h + 1, 0:1]
        ct = ct_scr[h]
        a_t = jnp.where(seen_t, gcb[:, h:h + 1], NEG_INF)
        m_row = jnp.maximum(jnp.max(a_t, axis=0, keepdims=True), m_prev)
        w_t = jnp.exp(a_t - m_row)
        r1 = _dot_nt(jnp.concatenate([k[:, hs], ct.astype(BF16)], axis=0), q[:, hs])
        s_t = (r1[0:L, :] * w_t).astype(BF16)
        return m_prev, ct, m_row, s_t, r1[L:, :]

    def head_finish(h, m_prev, ct, m_row, s_t, inter):
        hs = slice(h * M_HD, (h + 1) * M_HD)
        vext = jnp.concatenate([vt[hs, :], ones_rows], axis=0)
        g_r = grb[h:h + 1, :]
        b_r = grb[nh + h:nh + h + 1, :]
        b_last = grb[2 * nh + h:2 * nh + h + 1, 0:1]
        g_max = grb[3 * nh + h:3 * nh + h + 1, 0:1]
        w0 = jnp.exp(m_prev - m_row)
        tot = _dot(vext, s_t) + w0 * inter
        den = tot[M_HD:M_HD + 1, :]
        floor = jnp.exp(-(b_r + m_row))
        hcur[hs, :] = tot[0:M_HD, :] / jnp.maximum(jnp.abs(den), floor)

        m_last = jnp.maximum(g_max, m_prev)
        wk = jnp.exp(g_r - m_last)
        decay = jnp.exp(m_prev - m_last)
        vw = (vext.astype(F32) * wk).astype(BF16)
        ct_scr[h] = decay * ct + _dot(vw, k[:, hs])
        mscr[h:h + 1, :] = jnp.broadcast_to(b_last + m_last, (1, LANES))

    pending = [head_scores(h) for h in range(min(SCAN_AHEAD, nh))]
    for h in range(nh):
        if h + SCAN_AHEAD < nh:
            pending.append(head_scores(h + SCAN_AHEAD))
        head_finish(h, *pending.pop(0))

    @pl.when(drn == 0)
    def _park():
        hfwd[c] = hcur[...]

    @pl.when(drn == 1)
    def _emit():
        ht_ref[0] = (hcur[...] + hfwd[nc - 1 - c]).astype(ht_ref.dtype)

    if write_state:
        @pl.when(c == nc - 1)
        def _final():
            for h in range(nh):
                cfin = ct_scr[h]
                cout_ref[0, 0, h] = cfin[0:M_HD, :].T
                nout_ref[0, 0, h:h + 1, :] = cfin[M_HD:M_HD + 1, :]
            mout_ref[0, 0] = mscr[...]


def _mlstm_scan(q, k, vt, gc, gr, init, write_state):
    bsz, t, dm = q.shape
    L = MCHUNK
    nc = t // L
    chunk = lambda b, d, c: c + d * (nc - 1 - 2 * c)
    tok = lambda b, d, c: (b, chunk(b, d, c), 0)
    in_specs = [pl.BlockSpec((1, L, dm), tok)] * 2 + [
        pl.BlockSpec((1, dm, L), lambda b, d, c: (b, 0, chunk(b, d, c))),
        pl.BlockSpec((1, 1, L, gc.shape[-1]), lambda b, d, c: (b, d, chunk(b, d, c), 0)),
        pl.BlockSpec((1, 1, gr.shape[2], L), lambda b, d, c: (b, d, 0, chunk(b, d, c)))]
    args = [q, k, vt, gc, gr]
    st = lambda b, d, c: (b, d, 0, 0)
    st5 = lambda b, d, c: (b, d, 0, 0, 0)
    if init is not None:
        c0, n0, m0 = init
        in_specs += [pl.BlockSpec((1, 1, M_HEADS, M_HD, M_HD), st5),
                     pl.BlockSpec((1, 1, M_HEADS, M_HD), st),
                     pl.BlockSpec((1, 1, M_HEADS, LANES), st)]
        args += [c0, n0, jnp.broadcast_to(m0[..., None], m0.shape + (LANES,))]
    out_shape = [jax.ShapeDtypeStruct((bsz, dm, t), BF16)]
    out_specs = [pl.BlockSpec((1, dm, L), lambda b, d, c: (b, 0, nc - 1 - d * c))]
    if write_state:
        out_shape += [jax.ShapeDtypeStruct((bsz, 2, M_HEADS, M_HD, M_HD), F32),
                      jax.ShapeDtypeStruct((bsz, 2, M_HEADS, M_HD), F32),
                      jax.ShapeDtypeStruct((bsz, 2, M_HEADS, LANES), F32)]
        out_specs += [pl.BlockSpec((1, 1, M_HEADS, M_HD, M_HD), st5),
                      pl.BlockSpec((1, 1, M_HEADS, M_HD), st),
                      pl.BlockSpec((1, 1, M_HEADS, LANES), st)]
    return pl.pallas_call(
        functools.partial(_mlstm_scan_kernel, has_init=init is not None,
                          write_state=write_state, nc=nc),
        out_shape=tuple(out_shape),
        grid=(bsz, 2, nc),
        in_specs=in_specs,
        out_specs=tuple(out_specs),
        scratch_shapes=[pltpu.VMEM((M_HEADS, M_HD + 16, M_HD), F32),
                        pltpu.VMEM((M_HEADS, LANES), F32),
                        pltpu.VMEM((dm, L), F32),
                        pltpu.VMEM((nc, dm, L), F32)],
        compiler_params=_cparams(("parallel", "arbitrary", "arbitrary")),
        name="mlstm_scan",
    )(*args)


def _mlstm_out_kernel(ht_ref, og_ref, x_ref, mod_ref, wo_ref, fw_ref, o_ref):
    hm = ht_ref[0].astype(F32).T * og_ref[0].astype(F32)
    y = _dot(hm.astype(BF16), wo_ref[...])
    x2 = x_ref[0] + mod_ref[0][2:3, :] * y
    ms = jnp.mean(x2 * x2, axis=-1, keepdims=True)
    o_ref[0] = x2 * lax.rsqrt(ms + EPS) * fw_ref[...]


def _mlstm_out(ht, og, x, mod3, mod_row, wo_bf, final_w):
    bsz, t, d = x.shape
    dm = og.shape[-1]
    tm = min(2 * ROW_TILE, t)
    tok = lambda b, i: (b, i, 0)
    return pl.pallas_call(
        _mlstm_out_kernel,
        out_shape=jax.ShapeDtypeStruct((bsz, t, d), F32),
        grid=(bsz, t // tm),
        in_specs=[pl.BlockSpec((1, dm, tm), lambda b, i: (b, 0, i)),
                  pl.BlockSpec((1, tm, dm), tok),
                  pl.BlockSpec((1, tm, d), tok),
                  pl.BlockSpec((1, 3, d), lambda b, i: (mod_row(b), 0, 0)),
                  pl.BlockSpec(wo_bf.shape, lambda b, i: (0, 0)),
                  pl.BlockSpec((1, d), lambda b, i: (0, 0))],
        out_specs=pl.BlockSpec((1, tm, d), tok),
        compiler_params=_cparams(("parallel", "parallel")),
        name="mlstm_out",
    )(ht, og, x, mod3, wo_bf, final_w.reshape(1, d))


def _rope_tables(t):
    nf = HEAD_DIM // 4
    pos = jnp.arange(t)
    row = (pos // GRID_W).astype(F32)
    col = (pos % GRID_W).astype(F32)
    inv = ROPE_BASE ** (-jnp.arange(nf, dtype=F32) / nf)
    ar = row[:, None] * inv[None, :]
    ac = col[:, None] * inv[None, :]
    cos = jnp.concatenate([jnp.cos(ar), jnp.cos(ar), jnp.cos(ac), jnp.cos(ac)], axis=1)
    sin = jnp.concatenate([-jnp.sin(ar), jnp.sin(ar), -jnp.sin(ac), jnp.sin(ac)], axis=1)
    reps = LANES // HEAD_DIM
    return jnp.tile(cos, (1, reps)), jnp.tile(sin, (1, reps))


def kernel(x_prompt, x_sample, cache_k, cache_v, state_C, state_n, state_m, c, c_ctx,
           attn_norm_w, attn_ada_w, attn_ada_b, attn_w_in, attn_sink, attn_w_out,
           mlstm_norm_w, mlstm_ada_w, mlstm_ada_b, mlstm_w_in, mlstm_b_gates, mlstm_w_out,
           final_norm_w):
    assert attn_w_in.shape[0] == 1 and mlstm_w_in.shape[0] == 1, "one layer of each mixer"
    bsz, seq, d = x_prompt.shape
    dbsz, dseq, _ = x_sample.shape
    dkv = N_KV_HEADS * HEAD_DIM
    dm = M_HEADS * M_HD

    n_cond = 1 + dbsz
    cond = jnp.concatenate([c_ctx[None, :], c, jnp.zeros((-n_cond % 8, d), F32)], axis=0)
    attn_mod = _ada(cond, attn_ada_w[0], attn_ada_b[0]).reshape(-1, 3, d)
    mlstm_mod = _ada(cond, mlstm_ada_w[0], mlstm_ada_b[0]).reshape(-1, 3, d)
    ctx_row = lambda b: 0
    lat_row = lambda b: b + 1

    attn_w_in_bf = attn_w_in[0].astype(BF16)
    attn_w_out_bf = attn_w_out[0].astype(BF16)
    attn_wvt_bf = attn_w_in[0, :, 2 * N_HEADS * HEAD_DIM + dkv:].T.astype(BF16)
    w_main_bf = jnp.concatenate([mlstm_w_in[0, :, :2 * dm], mlstm_w_in[0, :, 3 * dm:5 * dm]],
                                axis=1).astype(BF16)
    mlstm_wvt_bf = mlstm_w_in[0, :, 2 * dm:3 * dm].T.astype(BF16)
    wgt_bf = mlstm_w_in[0, :, 5 * dm:].T.astype(BF16)
    mlstm_w_out_bf = mlstm_w_out[0].astype(BF16)

    def mlstm_layer(x, mod_row, init, write_state):
        q, k, vt, og, gc, gr = _mlstm_in(x, mlstm_mod, mod_row, mlstm_norm_w[0], w_main_bf,
                                         mlstm_wvt_bf, wgt_bf, mlstm_b_gates[0])
        outs = _mlstm_scan(q, k, vt, gc, gr, init, write_state)
        y = _mlstm_out(outs[0], og, x, mlstm_mod, mod_row, mlstm_w_out_bf, final_norm_w)
        return y, outs[1:]

    q, sg, k_ctx, vt_ctx, v_ctx = _attn_in(x_prompt, attn_mod, ctx_row, attn_norm_w[0], attn_w_in_bf,
                                           attn_wvt_bf, None, F32, True)
    x1 = _attn(q, sg, x_prompt, attn_mod, ctx_row, k_ctx, vt_ctx, None, None, attn_sink[0], attn_w_out_bf)
    y_prompt, (c_fin, n_fin, m_fin) = mlstm_layer(x1, ctx_row, None, True)

    q, sg, k_lat, vt_lat = _attn_in(x_sample, attn_mod, lat_row, attn_norm_w[0], attn_w_in_bf,
                                    attn_wvt_bf, _rope_tables(dseq), BF16, False)
    kc = cache_k[:, 0].reshape(dbsz, -1, dkv).astype(BF16)
    vct = jnp.swapaxes(cache_v[:, 0].reshape(dbsz, -1, dkv), 1, 2).astype(BF16)
    x1 = _attn(q, sg, x_sample, attn_mod, lat_row, kc, vct, k_lat, vt_lat, attn_sink[0], attn_w_out_bf)
    y_sample, _ = mlstm_layer(x1, lat_row, (state_C[:, 0], state_n[:, 0], state_m[:, 0]), False)

    new_cache_k = k_ctx.reshape(bsz, 1, seq, N_KV_HEADS, HEAD_DIM)
    new_cache_v = v_ctx.reshape(bsz, 1, seq, N_KV_HEADS, HEAD_DIM)
    return (y_prompt, y_sample, new_cache_k, new_cache_v,
            c_fin[:, None], n_fin[:, None], m_fin[:, None, :, :, 0])
```

```python
import functools

import jax
import jax.numpy as jnp
from jax import lax
from jax.experimental import pallas as pl
from jax.experimental.pallas import tpu as pltpu

F32 = jnp.float32
BF16 = jnp.bfloat16

HEAD_DIM = 64
N_KV_HEADS = 4
GROUP = 4
N_HEADS = N_KV_HEADS * GROUP
QBLK = 128
GRID_W = 64
ROPE_BASE = 10000.0
M_HEADS = 8
M_HD = 128
EPS = 1e-6

LANES = 128
VMEM_LIMIT = 48 * 1024 * 1024

MCHUNK = 256
SCAN_AHEAD = 4
ROW_TILE = 256

NEG_INF = float("-inf")
LOG2E = 1.4426950408889634
LN2 = 0.6931471805599453


def _cparams(sem):
    return pltpu.CompilerParams(dimension_semantics=sem, vmem_limit_bytes=VMEM_LIMIT)


def _silu(x):
    return x * jax.nn.sigmoid(x)


def _log_sigmoid(x):
    return jnp.minimum(x, 0.0) - jnp.log1p(jnp.exp(-jnp.abs(x)))


def _dot(a, b):
    return jnp.dot(a, b, preferred_element_type=F32)


def _dot_nt(a, b):
    return lax.dot_general(a, b, (((1,), (1,)), ((), ())), preferred_element_type=F32)


def _dot_tn(a, b):
    return lax.dot_general(a, b, (((0,), (0,)), ((), ())), preferred_element_type=F32)


def _split3(x):
    hi = x.astype(BF16)
    r = x - hi.astype(F32)
    mid = r.astype(BF16)
    lo = (r - mid.astype(F32)).astype(BF16)
    return hi, mid, lo


def _prenorm(x, norm_w, mod):
    ms = jnp.mean(x * x, axis=-1, keepdims=True)
    y = x * lax.rsqrt(ms + EPS) * norm_w
    return y * (1.0 + mod[1:2, :]) + mod[0:1, :]


def _ada_kernel(cond_ref, w_ref, b_ref, o_ref):
    a = _silu(cond_ref[...]).astype(BF16)
    o_ref[...] = _dot(a, w_ref[...].astype(BF16)) + b_ref[...]


def _ada(cond8, w, b):
    d, n = w.shape
    tn = 512
    return pl.pallas_call(
        _ada_kernel,
        out_shape=jax.ShapeDtypeStruct((cond8.shape[0], n), F32),
        grid=(n // tn,),
        in_specs=[pl.BlockSpec(cond8.shape, lambda j: (0, 0)),
                  pl.BlockSpec((d, tn), lambda j: (0, j)),
                  pl.BlockSpec((1, tn), lambda j: (0, j))],
        out_specs=pl.BlockSpec((cond8.shape[0], tn), lambda j: (0, j)),
        compiler_params=_cparams(("parallel",)),
        name="ada_mod",
    )(cond8, w, b.reshape(1, n))


def _rope(x, cos, sin, lane):
    first = (lane & 31) < 16
    outs = []
    for c in range(x.shape[1] // LANES):
        xc = x[:, c * LANES:(c + 1) * LANES]
        sw = jnp.where(first, pltpu.roll(xc, LANES - 16, 1), pltpu.roll(xc, 16, 1))
        outs.append(xc * cos + sw * sin)
    return jnp.concatenate(outs, axis=1)


def _attn_in_kernel(*refs, rope, emit_v):
    refs = list(refs)
    x_ref, mod_ref, nw_ref, w_ref, wvt_ref = refs[:5]
    pos = 5
    if rope:
        cos_ref, sin_ref = refs[pos:pos + 2]
        pos += 2
    q_ref, sg_ref, k_ref, vt_ref = refs[pos:pos + 4]
    dq = q_ref.shape[-1]
    dkv = k_ref.shape[-1]
    hb = _prenorm(x_ref[0], nw_ref[...], mod_ref[0]).astype(BF16)
    q = _dot(hb, w_ref[:, 0:dq])
    g = _dot(hb, w_ref[:, dq:2 * dq])
    k = _dot(hb, w_ref[:, 2 * dq:2 * dq + dkv])
    if rope:
        cos = cos_ref[...]
        sin = sin_ref[...]
        lane = lax.broadcasted_iota(jnp.int32, cos.shape, 1)
        q = _rope(q, cos, sin, lane)
        k = _rope(k, cos, sin, lane)
    q_ref[0] = (q * (HEAD_DIM ** -0.5 * LOG2E)).astype(q_ref.dtype)
    sg_ref[0] = _silu(g).astype(sg_ref.dtype)
    k_ref[0] = k.astype(k_ref.dtype)
    vt_ref[0] = _dot_nt(wvt_ref[...], hb).astype(vt_ref.dtype)
    if emit_v:
        v_ref = refs[pos + 4]
        v_ref[0] = _dot(hb, w_ref[:, 2 * dq + dkv:2 * dq + 2 * dkv]).astype(v_ref.dtype)


def _attn_in(x, mod3, mod_row, norm_w, w_bf, wvt_bf, rope_tabs, k_dtype, emit_v):
    bsz, t, d = x.shape
    dq = N_HEADS * HEAD_DIM
    dkv = N_KV_HEADS * HEAD_DIM
    tm = min(2 * ROW_TILE, t)
    rope = rope_tabs is not None
    tok = lambda b, i: (b, i, 0)
    const = lambda b, i: (0, 0)
    in_specs = [pl.BlockSpec((1, tm, d), tok),
                pl.BlockSpec((1, 3, d), lambda b, i: (mod_row(b), 0, 0)),
                pl.BlockSpec((1, d), const),
                pl.BlockSpec(w_bf.shape, const),
                pl.BlockSpec(wvt_bf.shape, const)]
    args = [x, mod3, norm_w.reshape(1, d), w_bf, wvt_bf]
    if rope:
        in_specs += [pl.BlockSpec((tm, LANES), lambda b, i: (i, 0))] * 2
        args += list(rope_tabs)
    out_shape = [jax.ShapeDtypeStruct((bsz, t, dq), BF16),
                 jax.ShapeDtypeStruct((bsz, t, dq), BF16),
                 jax.ShapeDtypeStruct((bsz, t, dkv), k_dtype),
                 jax.ShapeDtypeStruct((bsz, dkv, t), BF16)]
    out_specs = [pl.BlockSpec((1, tm, dq), tok), pl.BlockSpec((1, tm, dq), tok),
                 pl.BlockSpec((1, tm, dkv), tok),
                 pl.BlockSpec((1, dkv, tm), lambda b, i: (b, 0, i))]
    if emit_v:
        out_shape.append(jax.ShapeDtypeStruct((bsz, t, dkv), F32))
        out_specs.append(pl.BlockSpec((1, tm, dkv), tok))
    return pl.pallas_call(
        functools.partial(_attn_in_kernel, rope=rope, emit_v=emit_v),
        out_shape=tuple(out_shape),
        grid=(bsz, t // tm),
        in_specs=in_specs,
        out_specs=tuple(out_specs),
        compiler_params=_cparams(("parallel", "parallel")),
        name="attn_in_rope" if rope else "attn_in",
    )(*args)


def _attn_kernel(*refs, window, nb):
    if window:
        (q_ref, sg_ref, x_ref, mod_ref, kc_ref, vct_ref, kp_ref, kq_ref, kn_ref,
         vpt_ref, vqt_ref, vnt_ref, sink_ref, wo_ref, o_ref, s_scr, p_scr, ot_scr) = refs
    else:
        q_ref, sg_ref, x_ref, mod_ref, kc_ref, vct_ref, sink_ref, wo_ref, o_ref, s_scr, p_scr, ot_scr = refs
    blk = pl.program_id(1)
    q = q_ref[0]
    n_ctx = kc_ref.shape[1] // QBLK
    cols = GROUP * QBLK
    if window:
        kj = lax.broadcasted_iota(jnp.int32, (QBLK, cols), 0)
        qi = lax.broadcasted_iota(jnp.int32, (QBLK, cols), 1) & (QBLK - 1)
        prev_ok = (kj >= qi) & (blk > 0)
        next_ok = (kj <= qi) & (blk < nb - 1)
    ones_rows = jnp.where(lax.broadcasted_iota(jnp.int32, (16, QBLK), 0) == 0, 1.0, 0.0).astype(BF16)
    masks = [None] * n_ctx + ([prev_ok, None, next_ok] if window else [])
    n_blk = len(masks)

    def scores_start(kvh):
        cs = slice(kvh * HEAD_DIM, (kvh + 1) * HEAD_DIM)
        heads = [kvh * GROUP + j for j in range(GROUP)]
        q4 = jnp.concatenate([q[:, h * HEAD_DIM:(h + 1) * HEAD_DIM] for h in heads], axis=0)
        sink_row = jnp.concatenate(
            [jnp.broadcast_to(sink_ref[0:1, h:h + 1], (1, QBLK)) for h in heads], axis=1) * LOG2E
        keys = [kc_ref[0, j * QBLK:(j + 1) * QBLK, cs].astype(BF16) for j in range(n_ctx)]
        if window:
            keys += [kp_ref[0][:, cs], kq_ref[0][:, cs], kn_ref[0][:, cs]]
        st_all = _dot_nt(jnp.concatenate(keys, axis=0), q4)
        return dict(kvh=kvh, st_all=st_all, sink_row=sink_row, macc=jnp.full((8, cols), NEG_INF, F32))

    def scores_block(st, j):
        s_blk = st["st_all"][j * QBLK:(j + 1) * QBLK, :]
        if masks[j] is not None:
            s_blk = jnp.where(masks[j], s_blk, NEG_INF)
        s_scr[st["kvh"], j] = s_blk
        st["macc"] = jnp.maximum(st["macc"], jnp.max(s_blk.reshape(QBLK // 8, 8, cols), axis=0))

    def scores_end(st):
        return jnp.maximum(jnp.max(st["macc"], axis=0, keepdims=True), st["sink_row"]), st["sink_row"]

    def weights_block(kvh, j, m_row):
        p_scr[kvh, j * QBLK:(j + 1) * QBLK, :] = jnp.exp2(s_scr[kvh, j] - m_row).astype(BF16)

    def weighted_values(kvh, m_row, sink_row):
        cs = slice(kvh * HEAD_DIM, (kvh + 1) * HEAD_DIM)
        vts = [vct_ref[0, cs, j * QBLK:(j + 1) * QBLK] for j in range(n_ctx)]
        if window:
            vts += [vpt_ref[0, cs, :], vqt_ref[0, cs, :], vnt_ref[0, cs, :]]
        vt_ext = jnp.concatenate(
            [jnp.concatenate(vts, axis=1), jnp.tile(ones_rows, (1, n_blk))], axis=0)
        acc = _dot(vt_ext, p_scr[kvh])
        den = acc[HEAD_DIM:HEAD_DIM + 1, :] + jnp.exp2(sink_row - m_row)
        o_t = acc[0:HEAD_DIM, :] / den
        for j in range(GROUP):
            h = kvh * GROUP + j
            ot_scr[h * HEAD_DIM:(h + 1) * HEAD_DIM, :] = o_t[:, j * QBLK:(j + 1) * QBLK]

    stats = []
    for kvh in range(N_KV_HEADS):
        st = scores_start(kvh)
        for j in range(n_blk):
            scores_block(st, j)
        stats.append(scores_end(st))
    for kvh in range(N_KV_HEADS):
        for j in range(n_blk):
            weights_block(kvh, j, stats[kvh][0])
        weighted_values(kvh, *stats[kvh])
    z = (ot_scr[...].T * sg_ref[0].astype(F32)).astype(BF16)
    y = _dot(z, wo_ref[...])
    o_ref[0] = x_ref[0] + mod_ref[0][2:3, :] * y


def _attn(q, sg, x, mod3, mod_row, kc, vct, k_lat, vt_lat, sink, wo_bf):
    bsz, t, d = x.shape
    dq = q.shape[-1]
    dkv = kc.shape[-1]
    p_len = kc.shape[1]
    nb = t // QBLK
    window = k_lat is not None
    tok = lambda b, i: (b, i, 0)
    in_specs = [pl.BlockSpec((1, QBLK, dq), tok),
                pl.BlockSpec((1, QBLK, dq), tok),
                pl.BlockSpec((1, QBLK, d), tok),
                pl.BlockSpec((1, 3, d), lambda b, i: (mod_row(b), 0, 0)),
                pl.BlockSpec((1, p_len, dkv), lambda b, i: (b, 0, 0)),
                pl.BlockSpec((1, dkv, p_len), lambda b, i: (b, 0, 0))]
    args = [q, sg, x, mod3, kc, vct]
    n_blocks = p_len // QBLK
    if window:
        prev = lambda i: jnp.maximum(i - 1, 0)
        nxt = lambda i: jnp.minimum(i + 1, nb - 1)
        in_specs += [pl.BlockSpec((1, QBLK, dkv), lambda b, i: (b, prev(i), 0)),
                     pl.BlockSpec((1, QBLK, dkv), tok),
                     pl.BlockSpec((1, QBLK, dkv), lambda b, i: (b, nxt(i), 0)),
                     pl.BlockSpec((1, dkv, QBLK), lambda b, i: (b, 0, prev(i))),
                     pl.BlockSpec((1, dkv, QBLK), lambda b, i: (b, 0, i)),
                     pl.BlockSpec((1, dkv, QBLK), lambda b, i: (b, 0, nxt(i)))]
        args += [k_lat] * 3 + [vt_lat] * 3
        n_blocks += 3
    in_specs += [pl.BlockSpec((1, N_HEADS), lambda b, i: (0, 0)),
                 pl.BlockSpec(wo_bf.shape, lambda b, i: (0, 0))]
    args += [sink.reshape(1, N_HEADS), wo_bf]
    return pl.pallas_call(
        functools.partial(_attn_kernel, window=window, nb=nb),
        out_shape=jax.ShapeDtypeStruct((bsz, t, d), F32),
        grid=(bsz, nb),
        in_specs=in_specs,
        out_specs=pl.BlockSpec((1, QBLK, d), tok),
        scratch_shapes=[pltpu.VMEM((N_KV_HEADS, n_blocks, QBLK, GROUP * QBLK), F32),
                        pltpu.VMEM((N_KV_HEADS, n_blocks * QBLK, GROUP * QBLK), BF16),
                        pltpu.VMEM((dq, QBLK), F32)],
        compiler_params=_cparams(("parallel", "parallel")),
        name="attn_window" if window else "attn_ctx",
    )(*args)


def _mlstm_in_kernel(x_ref, mod_ref, nw_ref, w_ref, wvt_ref, wgt_ref, bgt_ref,
                     q_ref, k_ref, vt_ref, og_ref, gc_ref, gr_ref):
    dm = q_ref.shape[-1]
    nh = M_HEADS
    L = MCHUNK
    hb = _prenorm(x_ref[0], nw_ref[...], mod_ref[0]).astype(BF16)

    gr = _dot_nt(wgt_ref[...], hb) + bgt_ref[...]
    n_chunks = x_ref.shape[1] // L
    ri = lax.broadcasted_iota(jnp.int32, (L, L), 0)
    ci = lax.broadcasted_iota(jnp.int32, (L, L), 1)
    lane = lax.broadcasted_iota(jnp.int32, (n_chunks * nh, L), 1)
    g_rows = []
    for dr in range(2):
        before = (ri <= ci) if dr == 0 else (ri >= ci)
        tri = jnp.where(before, 1.0, 0.0).astype(BF16)
        base = dr * 2 * nh
        lf = _log_sigmoid(gr[base + nh:base + 2 * nh, :]) * LOG2E
        gi = gr[base:base + nh, :] * LOG2E
        lf_st = jnp.concatenate([lf[:, c * L:(c + 1) * L] for c in range(n_chunks)], axis=0)
        b_st = sum(_dot(piece, tri) for piece in _split3(lf_st))
        g_st = jnp.concatenate([gi[:, c * L:(c + 1) * L] for c in range(n_chunks)], axis=0) - b_st
        run = g_st
        step = 1
        while step < L:
            if dr == 0:
                run = jnp.where(lane >= step, jnp.maximum(run, pltpu.roll(run, step, 1)), run)
            else:
                run = jnp.where(lane < L - step, jnp.maximum(run, pltpu.roll(run, L - step, 1)), run)
            step *= 2
        for cidx in range(n_chunks):
            rows = slice(cidx * L, (cidx + 1) * L)
            blk = slice(cidx * nh, (cidx + 1) * nh)
            b_last = jnp.sum(lf[:, rows], axis=1, keepdims=True)
            g_max = jnp.max(g_st[blk, :], axis=1, keepdims=True)
            g_rows.append(g_st[blk, :])
            gr_ref[0, dr, :, rows] = jnp.concatenate(
                [g_st[blk, :], b_st[blk, :], jnp.broadcast_to(b_last, (nh, L)),
                 jnp.broadcast_to(g_max, (nh, L)), run[blk, :]], axis=0)
    g_sq = jnp.concatenate(g_rows + [jnp.zeros((L - len(g_rows) * nh, L), F32)], axis=0).T
    for dr in range(2):
        for cidx in range(n_chunks):
            idx = dr * n_chunks + cidx
            gc_ref[0, dr, cidx * L:(cidx + 1) * L, :] = g_sq[:, idx * nh:(idx + 1) * nh]

    q_ref[0] = _dot(hb, w_ref[:, 0:dm]).astype(q_ref.dtype)
    k_ref[0] = (_dot(hb, w_ref[:, dm:2 * dm]) * (M_HD ** -0.5)).astype(k_ref.dtype)
    vt_ref[0] = _dot_nt(wvt_ref[...], hb).astype(vt_ref.dtype)
    o = _dot(hb, w_ref[:, 2 * dm:3 * dm])
    g = _dot(hb, w_ref[:, 3 * dm:4 * dm])
    og_ref[0] = (jax.nn.sigmoid(o) * _silu(g)).astype(og_ref.dtype)


def _mlstm_in(x, mod3, mod_row, norm_w, w_main_bf, wvt_bf, wgt_bf, b_gates):
    bsz, t, d = x.shape
    dm = M_HEADS * M_HD
    ng = 4 * M_HEADS
    tm = min(2 * ROW_TILE, t)
    tok = lambda b, i: (b, i, 0)
    const = lambda b, i: (0, 0)
    big = jax.ShapeDtypeStruct((bsz, t, dm), BF16)
    return pl.pallas_call(
        _mlstm_in_kernel,
        out_shape=(big, big, jax.ShapeDtypeStruct((bsz, dm, t), BF16), big,
                   jax.ShapeDtypeStruct((bsz, 2, t, M_HEADS), F32),
                   jax.ShapeDtypeStruct((bsz, 2, 5 * M_HEADS, t), F32)),
        grid=(bsz, t // tm),
        in_specs=[pl.BlockSpec((1, tm, d), tok),
                  pl.BlockSpec((1, 3, d), lambda b, i: (mod_row(b), 0, 0)),
                  pl.BlockSpec((1, d), const),
                  pl.BlockSpec(w_main_bf.shape, const),
                  pl.BlockSpec(wvt_bf.shape, const),
                  pl.BlockSpec((ng, d), const),
                  pl.BlockSpec((ng, 1), const)],
        out_specs=(pl.BlockSpec((1, tm, dm), tok), pl.BlockSpec((1, tm, dm), tok),
                   pl.BlockSpec((1, dm, tm), lambda b, i: (b, 0, i)), pl.BlockSpec((1, tm, dm), tok),
                   pl.BlockSpec((1, 2, tm, M_HEADS), lambda b, i: (b, 0, i, 0)),
                   pl.BlockSpec((1, 2, 5 * M_HEADS, tm), lambda b, i: (b, 0, 0, i))),
        compiler_params=_cparams(("parallel", "parallel")),
        name="mlstm_in",
    )(x, mod3, norm_w.reshape(1, d), w_main_bf, wvt_bf, wgt_bf, b_gates.reshape(ng, 1))


def _mlstm_scan_kernel(*refs, has_init, write_state, nc):
    refs = list(refs)
    q_ref, k_ref, vt_ref, gc_ref, gr_ref = refs[:5]
    pos = 5
    if has_init:
        c0_ref, n0_ref, m0_ref = refs[pos:pos + 3]
        pos += 3
    ht_ref = refs[pos]
    pos += 1
    if write_state:
        cout_ref, nout_ref, mout_ref = refs[pos:pos + 3]
        pos += 3
    ct_scr, mscr, hcur, hfwd = refs[pos:pos + 4]

    drn = pl.program_id(1)
    c = pl.program_id(2)
    L = q_ref.shape[1]
    nh = M_HEADS
    pad = ct_scr.shape[1] - M_HD

    @pl.when(c == 0)
    def _init():
        if has_init:
            for h in range(nh):
                ct_scr[h, 0:M_HD, :] = c0_ref[0, 0, h].T
                ct_scr[h, M_HD:M_HD + pad, :] = jnp.concatenate(
                    [n0_ref[0, 0, h:h + 1, :], jnp.zeros((pad - 1, M_HD), F32)], axis=0)
            mscr[...] = m0_ref[0, 0] * LOG2E
        else:
            ct_scr[...] = jnp.zeros(ct_scr.shape, F32)
            mscr[...] = jnp.zeros(mscr.shape, F32)

    si = lax.broadcasted_iota(jnp.int32, (L, L), 0)
    li = lax.broadcasted_iota(jnp.int32, (L, L), 1)
    seen_t = (si - li) * (1 - 2 * drn) <= 0

    gcb = gc_ref[0, 0]
    grb = gr_ref[0, 0]
    q = q_ref[0]
    k = k_ref[0]
    vt = vt_ref[0]
    ones_rows = jnp.where(lax.broadcasted_iota(jnp.int32, (pad, L), 0) == 0, 1.0, 0.0).astype(BF16)

    def head_scores(h):
        hs = slice(h * M_HD, (h + 1) * M_HD)
        m_prev = mscr[h:h + 1, 0:1]
        ct = ct_scr[h]
        m_row = jnp.maximum(grb[4 * nh + h:4 * nh + h + 1, :], m_prev)
        w_t = jnp.exp2(jnp.where(seen_t, gcb[:, h:h + 1], NEG_INF) - m_row)
        r1 = _dot_nt(jnp.concatenate([k[:, hs], ct.astype(BF16)], axis=0), q[:, hs])
        s_t = (r1[0:L, :] * w_t).astype(BF16)
        return m_prev, ct, m_row, s_t, r1[L:, :]

    def head_finish(h, m_prev, ct, m_row, s_t, inter):
        hs = slice(h * M_HD, (h + 1) * M_HD)
        vext = jnp.concatenate([vt[hs, :], ones_rows], axis=0)
        g_r = grb[h:h + 1, :]
        b_r = grb[nh + h:nh + h + 1, :]
        b_last = grb[2 * nh + h:2 * nh + h + 1, 0:1]
        g_max = grb[3 * nh + h:3 * nh + h + 1, 0:1]
        w0 = jnp.exp2(m_prev - m_row)
        tot = _dot(vext, s_t) + w0 * inter
        den = tot[M_HD:M_HD + 1, :]
        floor = jnp.exp2(-(b_r + m_row))
        hcur[hs, :] = tot[0:M_HD, :] / jnp.maximum(jnp.abs(den), floor)

        m_last = jnp.maximum(g_max, m_prev)
        wk = jnp.exp2(g_r - m_last)
        decay = jnp.exp2(m_prev - m_last)
        vw = (vext.astype(F32) * wk).astype(BF16)
        ct_scr[h] = decay * ct + _dot(vw, k[:, hs])
        mscr[h:h + 1, :] = jnp.broadcast_to(b_last + m_last, (1, LANES))

    pending = [head_scores(h) for h in range(min(SCAN_AHEAD, nh))]
    for h in range(nh):
        if h + SCAN_AHEAD < nh:
            pending.append(head_scores(h + SCAN_AHEAD))
        head_finish(h, *pending.pop(0))

    @pl.when(drn == 0)
    def _park():
        hfwd[c] = hcur[...]

    @pl.when(drn == 1)
    def _emit():
        ht_ref[0] = (hcur[...] + hfwd[nc - 1 - c]).astype(ht_ref.dtype)

    if write_state:
        @pl.when(c == nc - 1)
        def _final():
            for h in range(nh):
                cfin = ct_scr[h]
                cout_ref[0, 0, h] = cfin[0:M_HD, :].T
                nout_ref[0, 0, h:h + 1, :] = cfin[M_HD:M_HD + 1, :]
            mout_ref[0, 0] = mscr[...] * LN2


def _mlstm_scan(q, k, vt, gc, gr, init, write_state):
    bsz, t, dm = q.shape
    L = MCHUNK
    nc = t // L
    chunk = lambda b, d, c: c + d * (nc - 1 - 2 * c)
    tok = lambda b, d, c: (b, chunk(b, d, c), 0)
    in_specs = [pl.BlockSpec((1, L, dm), tok)] * 2 + [
        pl.BlockSpec((1, dm, L), lambda b, d, c: (b, 0, chunk(b, d, c))),
        pl.BlockSpec((1, 1, L, gc.shape[-1]), lambda b, d, c: (b, d, chunk(b, d, c), 0)),
        pl.BlockSpec((1, 1, gr.shape[2], L), lambda b, d, c: (b, d, 0, chunk(b, d, c)))]
    args = [q, k, vt, gc, gr]
    st = lambda b, d, c: (b, d, 0, 0)
    st5 = lambda b, d, c: (b, d, 0, 0, 0)
    if init is not None:
        c0, n0, m0 = init
        in_specs += [pl.BlockSpec((1, 1, M_HEADS, M_HD, M_HD), st5),
                     pl.BlockSpec((1, 1, M_HEADS, M_HD), st),
                     pl.BlockSpec((1, 1, M_HEADS, LANES), st)]
        args += [c0, n0, jnp.broadcast_to(m0[..., None], m0.shape + (LANES,))]
    out_shape = [jax.ShapeDtypeStruct((bsz, dm, t), BF16)]
    out_specs = [pl.BlockSpec((1, dm, L), lambda b, d, c: (b, 0, nc - 1 - d * c))]
    if write_state:
        out_shape += [jax.ShapeDtypeStruct((bsz, 2, M_HEADS, M_HD, M_HD), F32),
                      jax.ShapeDtypeStruct((bsz, 2, M_HEADS, M_HD), F32),
                      jax.ShapeDtypeStruct((bsz, 2, M_HEADS, LANES), F32)]
        out_specs += [pl.BlockSpec((1, 1, M_HEADS, M_HD, M_HD), st5),
                      pl.BlockSpec((1, 1, M_HEADS, M_HD), st),
                      pl.BlockSpec((1, 1, M_HEADS, LANES), st)]
    return pl.pallas_call(
        functools.partial(_mlstm_scan_kernel, has_init=init is not None,
                          write_state=write_state, nc=nc),
        out_shape=tuple(out_shape),
        grid=(bsz, 2, nc),
        in_specs=in_specs,
        out_specs=tuple(out_specs),
        scratch_shapes=[pltpu.VMEM((M_HEADS, M_HD + 16, M_HD), F32),
                        pltpu.VMEM((M_HEADS, LANES), F32),
                        pltpu.VMEM((dm, L), F32),
                        pltpu.VMEM((nc, dm, L), F32)],
        compiler_params=_cparams(("parallel", "arbitrary", "arbitrary")),
        name="mlstm_scan",
    )(*args)


def _mlstm_out_kernel(ht_ref, og_ref, x_ref, mod_ref, wo_ref, fw_ref, o_ref):
    hm = ht_ref[0].astype(F32).T * og_ref[0].astype(F32)
    y = _dot(hm.astype(BF16), wo_ref[...])
    x2 = x_ref[0] + mod_ref[0][2:3, :] * y
    ms = jnp.mean(x2 * x2, axis=-1, keepdims=True)
    o_ref[0] = x2 * lax.rsqrt(ms + EPS) * fw_ref[...]


def _mlstm_out(ht, og, x, mod3, mod_row, wo_bf, final_w):
    bsz, t, d = x.shape
    dm = og.shape[-1]
    tm = min(2 * ROW_TILE, t)
    tok = lambda b, i: (b, i, 0)
    return pl.pallas_call(
        _mlstm_out_kernel,
        out_shape=jax.ShapeDtypeStruct((bsz, t, d), F32),
        grid=(bsz, t // tm),
        in_specs=[pl.BlockSpec((1, dm, tm), lambda b, i: (b, 0, i)),
                  pl.BlockSpec((1, tm, dm), tok),
                  pl.BlockSpec((1, tm, d), tok),
                  pl.BlockSpec((1, 3, d), lambda b, i: (mod_row(b), 0, 0)),
                  pl.BlockSpec(wo_bf.shape, lambda b, i: (0, 0)),
                  pl.BlockSpec((1, d), lambda b, i: (0, 0))],
        out_specs=pl.BlockSpec((1, tm, d), tok),
        compiler_params=_cparams(("parallel", "parallel")),
        name="mlstm_out",
    )(ht, og, x, mod3, wo_bf, final_w.reshape(1, d))


def _rope_tables(t):
    nf = HEAD_DIM // 4
    pos = jnp.arange(t)
    row = (pos // GRID_W).astype(F32)
    col = (pos % GRID_W).astype(F32)
    inv = ROPE_BASE ** (-jnp.arange(nf, dtype=F32) / nf)
    ar = row[:, None] * inv[None, :]
    ac = col[:, None] * inv[None, :]
    cos = jnp.concatenate([jnp.cos(ar), jnp.cos(ar), jnp.cos(ac), jnp.cos(ac)], axis=1)
    sin = jnp.concatenate([-jnp.sin(ar), jnp.sin(ar), -jnp.sin(ac), jnp.sin(ac)], axis=1)
    reps = LANES // HEAD_DIM
    return jnp.tile(cos, (1, reps)), jnp.tile(sin, (1, reps))


def kernel(x_prompt, x_sample, cache_k, cache_v, state_C, state_n, state_m, c, c_ctx,
           attn_norm_w, attn_ada_w, attn_ada_b, attn_w_in, attn_sink, attn_w_out,
           mlstm_norm_w, mlstm_ada_w, mlstm_ada_b, mlstm_w_in, mlstm_b_gates, mlstm_w_out,
           final_norm_w):
    assert attn_w_in.shape[0] == 1 and mlstm_w_in.shape[0] == 1, "one layer of each mixer"
    bsz, seq, d = x_prompt.shape
    dbsz, dseq, _ = x_sample.shape
    dkv = N_KV_HEADS * HEAD_DIM
    dm = M_HEADS * M_HD

    n_cond = 1 + dbsz
    cond = jnp.concatenate([c_ctx[None, :], c, jnp.zeros((-n_cond % 8, d), F32)], axis=0)
    attn_mod = _ada(cond, attn_ada_w[0], attn_ada_b[0]).reshape(-1, 3, d)
    mlstm_mod = _ada(cond, mlstm_ada_w[0], mlstm_ada_b[0]).reshape(-1, 3, d)
    ctx_row = lambda b: 0
    lat_row = lambda b: b + 1

    attn_w_in_bf = attn_w_in[0].astype(BF16)
    attn_w_out_bf = attn_w_out[0].astype(BF16)
    attn_wvt_bf = attn_w_in[0, :, 2 * N_HEADS * HEAD_DIM + dkv:].T.astype(BF16)
    w_main_bf = jnp.concatenate([mlstm_w_in[0, :, :2 * dm], mlstm_w_in[0, :, 3 * dm:5 * dm]],
                                axis=1).astype(BF16)
    mlstm_wvt_bf = mlstm_w_in[0, :, 2 * dm:3 * dm].T.astype(BF16)
    wgt_bf = mlstm_w_in[0, :, 5 * dm:].T.astype(BF16)
    mlstm_w_out_bf = mlstm_w_out[0].astype(BF16)

    def mlstm_layer(x, mod_row, init, write_state):
        q, k, vt, og, gc, gr = _mlstm_in(x, mlstm_mod, mod_row, mlstm_norm_w[0], w_main_bf,
                                         mlstm_wvt_bf, wgt_bf, mlstm_b_gates[0])
        outs = _mlstm_scan(q, k, vt, gc, gr, init, write_state)
        y = _mlstm_out(outs[0], og, x, mlstm_mod, mod_row, mlstm_w_out_bf, final_norm_w)
        return y, outs[1:]

    q, sg, k_ctx, vt_ctx, v_ctx = _attn_in(x_prompt, attn_mod, ctx_row, attn_norm_w[0], attn_w_in_bf,
                                           attn_wvt_bf, None, F32, True)
    x1 = _attn(q, sg, x_prompt, attn_mod, ctx_row, k_ctx, vt_ctx, None, None, attn_sink[0], attn_w_out_bf)
    y_prompt, (c_fin, n_fin, m_fin) = mlstm_layer(x1, ctx_row, None, True)

    q, sg, k_lat, vt_lat = _attn_in(x_sample, attn_mod, lat_row, attn_norm_w[0], attn_w_in_bf,
                                    attn_wvt_bf, _rope_tables(dseq), BF16, False)
    kc = cache_k[:, 0].reshape(dbsz, -1, dkv).astype(BF16)
    vct = jnp.swapaxes(cache_v[:, 0].reshape(dbsz, -1, dkv), 1, 2).astype(BF16)
    x1 = _attn(q, sg, x_sample, attn_mod, lat_row, kc, vct, k_lat, vt_lat, attn_sink[0], attn_w_out_bf)
    y_sample, _ = mlstm_layer(x1, lat_row, (state_C[:, 0], state_n[:, 0], state_m[:, 0]), False)

    new_cache_k = k_ctx.reshape(bsz, 1, seq, N_KV_HEADS, HEAD_DIM)
    new_cache_v = v_ctx.reshape(bsz, 1, seq, N_KV_HEADS, HEAD_DIM)
    return (y_prompt, y_sample, new_cache_k, new_cache_v,
            c_fin[:, None], n_fin[:, None], m_fin[:, None, :, :, 0])
```

```python
import functools

import jax
import jax.numpy as jnp
from jax import lax
from jax.experimental import pallas as pl
from jax.experimental.pallas import tpu as pltpu

F32 = jnp.float32
BF16 = jnp.bfloat16

HEAD_DIM = 64
N_KV_HEADS = 4
GROUP = 4
N_HEADS = N_KV_HEADS * GROUP
QBLK = 128
GRID_W = 64
ROPE_BASE = 10000.0
M_HEADS = 8
M_HD = 128
EPS = 1e-6

LANES = 128
VMEM_LIMIT = 48 * 1024 * 1024

MCHUNK = 256
ATTN_QB = 2
SCAN_AHEAD = 4
ROW_TILE = 256

NEG_INF = float("-inf")
LOG2E = 1.4426950408889634
LN2 = 0.6931471805599453


def _cparams(sem):
    return pltpu.CompilerParams(dimension_semantics=sem, vmem_limit_bytes=VMEM_LIMIT)


def _silu(x):
    return x * jax.nn.sigmoid(x)


def _log_sigmoid(x):
    return jnp.minimum(x, 0.0) - jnp.log1p(jnp.exp(-jnp.abs(x)))


def _dot(a, b):
    return jnp.dot(a, b, preferred_element_type=F32)


def _dot_nt(a, b):
    return lax.dot_general(a, b, (((1,), (1,)), ((), ())), preferred_element_type=F32)


def _dot_tn(a, b):
    return lax.dot_general(a, b, (((0,), (0,)), ((), ())), preferred_element_type=F32)


def _split3(x):
    hi = x.astype(BF16)
    r = x - hi.astype(F32)
    mid = r.astype(BF16)
    lo = (r - mid.astype(F32)).astype(BF16)
    return hi, mid, lo


def _prenorm(x, norm_w, mod):
    ms = jnp.mean(x * x, axis=-1, keepdims=True)
    y = x * lax.rsqrt(ms + EPS) * norm_w
    return y * (1.0 + mod[1:2, :]) + mod[0:1, :]


def _ada_kernel(cond_ref, w_ref, b_ref, o_ref):
    a = _silu(cond_ref[...]).astype(BF16)
    o_ref[...] = _dot(a, w_ref[...].astype(BF16)) + b_ref[...]


def _ada(cond8, w, b):
    d, n = w.shape
    tn = 512
    return pl.pallas_call(
        _ada_kernel,
        out_shape=jax.ShapeDtypeStruct((cond8.shape[0], n), F32),
        grid=(n // tn,),
        in_specs=[pl.BlockSpec(cond8.shape, lambda j: (0, 0)),
                  pl.BlockSpec((d, tn), lambda j: (0, j)),
                  pl.BlockSpec((1, tn), lambda j: (0, j))],
        out_specs=pl.BlockSpec((cond8.shape[0], tn), lambda j: (0, j)),
        compiler_params=_cparams(("parallel",)),
        name="ada_mod",
    )(cond8, w, b.reshape(1, n))


def _rope(x, cos, sin, lane):
    first = (lane & 31) < 16
    outs = []
    for c in range(x.shape[1] // LANES):
        xc = x[:, c * LANES:(c + 1) * LANES]
        sw = jnp.where(first, pltpu.roll(xc, LANES - 16, 1), pltpu.roll(xc, 16, 1))
        outs.append(xc * cos + sw * sin)
    return jnp.concatenate(outs, axis=1)


def _attn_in_kernel(*refs, rope, emit_v):
    refs = list(refs)
    x_ref, mod_ref, nw_ref, w_ref, wvt_ref = refs[:5]
    pos = 5
    if rope:
        cos_ref, sin_ref = refs[pos:pos + 2]
        pos += 2
    q_ref, sg_ref, k_ref, vt_ref = refs[pos:pos + 4]
    dq = q_ref.shape[-1]
    dkv = k_ref.shape[-1]
    hb = _prenorm(x_ref[0], nw_ref[...], mod_ref[0]).astype(BF16)
    q = _dot(hb, w_ref[:, 0:dq])
    g = _dot(hb, w_ref[:, dq:2 * dq])
    k = _dot(hb, w_ref[:, 2 * dq:2 * dq + dkv])
    if rope:
        cos = cos_ref[...]
        sin = sin_ref[...]
        lane = lax.broadcasted_iota(jnp.int32, cos.shape, 1)
        q = _rope(q, cos, sin, lane)
        k = _rope(k, cos, sin, lane)
    q_ref[0] = (q * (HEAD_DIM ** -0.5 * LOG2E)).astype(q_ref.dtype)
    sg_ref[0] = _silu(g).astype(sg_ref.dtype)
    k_ref[0] = k.astype(k_ref.dtype)
    vt_ref[0] = _dot_nt(wvt_ref[...], hb).astype(vt_ref.dtype)
    if emit_v:
        v_ref = refs[pos + 4]
        v_ref[0] = _dot(hb, w_ref[:, 2 * dq + dkv:2 * dq + 2 * dkv]).astype(v_ref.dtype)


def _attn_in(x, mod3, mod_row, norm_w, w_bf, wvt_bf, rope_tabs, k_dtype, emit_v):
    bsz, t, d = x.shape
    dq = N_HEADS * HEAD_DIM
    dkv = N_KV_HEADS * HEAD_DIM
    tm = min(2 * ROW_TILE, t)
    rope = rope_tabs is not None
    tok = lambda b, i: (b, i, 0)
    const = lambda b, i: (0, 0)
    in_specs = [pl.BlockSpec((1, tm, d), tok),
                pl.BlockSpec((1, 3, d), lambda b, i: (mod_row(b), 0, 0)),
                pl.BlockSpec((1, d), const),
                pl.BlockSpec(w_bf.shape, const),
                pl.BlockSpec(wvt_bf.shape, const)]
    args = [x, mod3, norm_w.reshape(1, d), w_bf, wvt_bf]
    if rope:
        in_specs += [pl.BlockSpec((tm, LANES), lambda b, i: (i, 0))] * 2
        args += list(rope_tabs)
    out_shape = [jax.ShapeDtypeStruct((bsz, t, dq), BF16),
                 jax.ShapeDtypeStruct((bsz, t, dq), BF16),
                 jax.ShapeDtypeStruct((bsz, t, dkv), k_dtype),
                 jax.ShapeDtypeStruct((bsz, dkv, t), BF16)]
    out_specs = [pl.BlockSpec((1, tm, dq), tok), pl.BlockSpec((1, tm, dq), tok),
                 pl.BlockSpec((1, tm, dkv), tok),
                 pl.BlockSpec((1, dkv, tm), lambda b, i: (b, 0, i))]
    if emit_v:
        out_shape.append(jax.ShapeDtypeStruct((bsz, t, dkv), F32))
        out_specs.append(pl.BlockSpec((1, tm, dkv), tok))
    return pl.pallas_call(
        functools.partial(_attn_in_kernel, rope=rope, emit_v=emit_v),
        out_shape=tuple(out_shape),
        grid=(bsz, t // tm),
        in_specs=in_specs,
        out_specs=tuple(out_specs),
        compiler_params=_cparams(("parallel", "parallel")),
        name="attn_in_rope" if rope else "attn_in",
    )(*args)


def _attn_kernel(*refs, window, nb):
    if window:
        (q_ref, sg_ref, x_ref, mod_ref, kc_ref, vct_ref, kp_ref, km_ref, kn_ref,
         vpt_ref, vmt_ref, vnt_ref, sink_ref, wo_ref, o_ref, s_scr, p_scr, ot_scr) = refs
    else:
        q_ref, sg_ref, x_ref, mod_ref, kc_ref, vct_ref, sink_ref, wo_ref, o_ref, s_scr, p_scr, ot_scr = refs
    step = pl.program_id(1)
    nqb = q_ref.shape[1] // QBLK
    n_ctx = kc_ref.shape[1] // QBLK
    cols = GROUP * QBLK
    if window:
        kj = lax.broadcasted_iota(jnp.int32, (QBLK, cols), 0)
        qi = lax.broadcasted_iota(jnp.int32, (QBLK, cols), 1) & (QBLK - 1)
        after_diag = kj >= qi
        before_diag = kj <= qi
    ones_rows = jnp.where(lax.broadcasted_iota(jnp.int32, (16, QBLK), 0) == 0, 1.0, 0.0).astype(BF16)
    n_blk = n_ctx + (3 if window else 0)

    def window_blocks(qb, cs, kp, km, kn, lanes):
        def mid(j):
            sl = slice(j * QBLK, (j + 1) * QBLK)
            return km[0, cs, sl] if lanes else km[0, sl, cs]
        first = kp[0, cs, :] if lanes else kp[0][:, cs]
        last = kn[0, cs, :] if lanes else kn[0][:, cs]
        return [first if qb == 0 else mid(qb - 1), mid(qb), last if qb == nqb - 1 else mid(qb + 1)]

    def block_masks(qb):
        if not window:
            return [None] * n_ctx
        prev_ok = after_diag & (step > 0) if qb == 0 else after_diag
        next_ok = before_diag & (step < nb // nqb - 1) if qb == nqb - 1 else before_diag
        return [None] * n_ctx + [prev_ok, None, next_ok]

    def scores(qb, kvh):
        u = qb * N_KV_HEADS + kvh
        cs = slice(kvh * HEAD_DIM, (kvh + 1) * HEAD_DIM)
        heads = [kvh * GROUP + j for j in range(GROUP)]
        qq = q_ref[0, qb * QBLK:(qb + 1) * QBLK, :]
        q4 = jnp.concatenate([qq[:, h * HEAD_DIM:(h + 1) * HEAD_DIM] for h in heads], axis=0)
        sink_row = jnp.concatenate(
            [jnp.broadcast_to(sink_ref[0:1, h:h + 1], (1, QBLK)) for h in heads], axis=1) * LOG2E
        keys = [kc_ref[0, j * QBLK:(j + 1) * QBLK, cs].astype(BF16) for j in range(n_ctx)]
        if window:
            keys += window_blocks(qb, cs, kp_ref, km_ref, kn_ref, False)
        st_all = _dot_nt(jnp.concatenate(keys, axis=0), q4)
        macc = jnp.full((8, cols), NEG_INF, F32)
        for j, ok in enumerate(block_masks(qb)):
            s_blk = st_all[j * QBLK:(j + 1) * QBLK, :]
            if ok is not None:
                s_blk = jnp.where(ok, s_blk, NEG_INF)
            s_scr[u, j] = s_blk
            macc = jnp.maximum(macc, jnp.max(s_blk.reshape(QBLK // 8, 8, cols), axis=0))
        return jnp.maximum(jnp.max(macc, axis=0, keepdims=True), sink_row), sink_row

    def weighted_values(qb, kvh, m_row, sink_row):
        u = qb * N_KV_HEADS + kvh
        cs = slice(kvh * HEAD_DIM, (kvh + 1) * HEAD_DIM)
        for j in range(n_blk):
            p_scr[u, j * QBLK:(j + 1) * QBLK, :] = jnp.exp2(s_scr[u, j] - m_row).astype(BF16)
        vts = [vct_ref[0, cs, j * QBLK:(j + 1) * QBLK] for j in range(n_ctx)]
        if window:
            vts += window_blocks(qb, cs, vpt_ref, vmt_ref, vnt_ref, True)
        vt_ext = jnp.concatenate(
            [jnp.concatenate(vts, axis=1), jnp.tile(ones_rows, (1, n_blk))], axis=0)
        acc = _dot(vt_ext, p_scr[u])
        den = acc[HEAD_DIM:HEAD_DIM + 1, :] + jnp.exp2(sink_row - m_row)
        o_t = acc[0:HEAD_DIM, :] / den
        for j in range(GROUP):
            h = kvh * GROUP + j
            ot_scr[h * HEAD_DIM:(h + 1) * HEAD_DIM, qb * QBLK:(qb + 1) * QBLK] = o_t[:, j * QBLK:(j + 1) * QBLK]

    units = [(qb, kvh) for qb in range(nqb) for kvh in range(N_KV_HEADS)]
    stats = [scores(qb, kvh) for qb, kvh in units]
    for (qb, kvh), st in zip(units, stats):
        weighted_values(qb, kvh, *st)
    z = (ot_scr[...].T * sg_ref[0].astype(F32)).astype(BF16)
    y = _dot(z, wo_ref[...])
    o_ref[0] = x_ref[0] + mod_ref[0][2:3, :] * y


def _attn(q, sg, x, mod3, mod_row, kc, vct, k_lat, vt_lat, sink, wo_bf):
    bsz, t, d = x.shape
    dq = q.shape[-1]
    dkv = kc.shape[-1]
    p_len = kc.shape[1]
    nb = t // QBLK
    nqb = ATTN_QB
    rows = nqb * QBLK
    window = k_lat is not None
    tok = lambda b, i: (b, i, 0)
    in_specs = [pl.BlockSpec((1, rows, dq), tok),
                pl.BlockSpec((1, rows, dq), tok),
                pl.BlockSpec((1, rows, d), tok),
                pl.BlockSpec((1, 3, d), lambda b, i: (mod_row(b), 0, 0)),
                pl.BlockSpec((1, p_len, dkv), lambda b, i: (b, 0, 0)),
                pl.BlockSpec((1, dkv, p_len), lambda b, i: (b, 0, 0))]
    args = [q, sg, x, mod3, kc, vct]
    n_blocks = p_len // QBLK
    if window:
        prev = lambda i: jnp.maximum(i * nqb - 1, 0)
        nxt = lambda i: jnp.minimum((i + 1) * nqb, nb - 1)
        in_specs += [pl.BlockSpec((1, QBLK, dkv), lambda b, i: (b, prev(i), 0)),
                     pl.BlockSpec((1, rows, dkv), tok),
                     pl.BlockSpec((1, QBLK, dkv), lambda b, i: (b, nxt(i), 0)),
                     pl.BlockSpec((1, dkv, QBLK), lambda b, i: (b, 0, prev(i))),
                     pl.BlockSpec((1, dkv, rows), lambda b, i: (b, 0, i)),
                     pl.BlockSpec((1, dkv, QBLK), lambda b, i: (b, 0, nxt(i)))]
        args += [k_lat] * 3 + [vt_lat] * 3
        n_blocks += 3
    in_specs += [pl.BlockSpec((1, N_HEADS), lambda b, i: (0, 0)),
                 pl.BlockSpec(wo_bf.shape, lambda b, i: (0, 0))]
    args += [sink.reshape(1, N_HEADS), wo_bf]
    units = nqb * N_KV_HEADS
    return pl.pallas_call(
        functools.partial(_attn_kernel, window=window, nb=nb),
        out_shape=jax.ShapeDtypeStruct((bsz, t, d), F32),
        grid=(bsz, nb // nqb),
        in_specs=in_specs,
        out_specs=pl.BlockSpec((1, rows, d), tok),
        scratch_shapes=[pltpu.VMEM((units, n_blocks, QBLK, GROUP * QBLK), F32),
                        pltpu.VMEM((units, n_blocks * QBLK, GROUP * QBLK), BF16),
                        pltpu.VMEM((dq, rows), F32)],
        compiler_params=_cparams(("parallel", "parallel")),
        name="attn_window" if window else "attn_ctx",
    )(*args)


def _mlstm_in_kernel(x_ref, mod_ref, nw_ref, w_ref, wvt_ref, wgt_ref, bgt_ref,
                     q_ref, k_ref, vt_ref, og_ref, gc_ref, gr_ref):
    dm = q_ref.shape[-1]
    nh = M_HEADS
    L = MCHUNK
    hb = _prenorm(x_ref[0], nw_ref[...], mod_ref[0]).astype(BF16)

    gr = _dot_nt(wgt_ref[...], hb) + bgt_ref[...]
    n_chunks = x_ref.shape[1] // L
    ri = lax.broadcasted_iota(jnp.int32, (L, L), 0)
    ci = lax.broadcasted_iota(jnp.int32, (L, L), 1)
    lane = lax.broadcasted_iota(jnp.int32, (n_chunks * nh, L), 1)
    g_rows = []
    for dr in range(2):
        before = (ri <= ci) if dr == 0 else (ri >= ci)
        tri = jnp.where(before, 1.0, 0.0).astype(BF16)
        base = dr * 2 * nh
        lf = _log_sigmoid(gr[base + nh:base + 2 * nh, :]) * LOG2E
        gi = gr[base:base + nh, :] * LOG2E
        lf_st = jnp.concatenate([lf[:, c * L:(c + 1) * L] for c in range(n_chunks)], axis=0)
        b_st = sum(_dot(piece, tri) for piece in _split3(lf_st))
        g_st = jnp.concatenate([gi[:, c * L:(c + 1) * L] for c in range(n_chunks)], axis=0) - b_st
        run = g_st
        step = 1
        while step < L:
            if dr == 0:
                run = jnp.where(lane >= step, jnp.maximum(run, pltpu.roll(run, step, 1)), run)
            else:
                run = jnp.where(lane < L - step, jnp.maximum(run, pltpu.roll(run, L - step, 1)), run)
            step *= 2
        for cidx in range(n_chunks):
            rows = slice(cidx * L, (cidx + 1) * L)
            blk = slice(cidx * nh, (cidx + 1) * nh)
            b_last = jnp.sum(lf[:, rows], axis=1, keepdims=True)
            g_max = jnp.max(g_st[blk, :], axis=1, keepdims=True)
            g_rows.append(g_st[blk, :])
            gr_ref[0, dr, :, rows] = jnp.concatenate(
                [g_st[blk, :], b_st[blk, :], jnp.broadcast_to(b_last, (nh, L)),
                 jnp.broadcast_to(g_max, (nh, L)), run[blk, :]], axis=0)
    g_sq = jnp.concatenate(g_rows + [jnp.zeros((L - len(g_rows) * nh, L), F32)], axis=0).T
    for dr in range(2):
        for cidx in range(n_chunks):
            idx = dr * n_chunks + cidx
            gc_ref[0, dr, cidx * L:(cidx + 1) * L, :] = g_sq[:, idx * nh:(idx + 1) * nh]

    q_ref[0] = _dot(hb, w_ref[:, 0:dm]).astype(q_ref.dtype)
    k_ref[0] = (_dot(hb, w_ref[:, dm:2 * dm]) * (M_HD ** -0.5)).astype(k_ref.dtype)
    vt_ref[0] = _dot_nt(wvt_ref[...], hb).astype(vt_ref.dtype)
    o = _dot(hb, w_ref[:, 2 * dm:3 * dm])
    g = _dot(hb, w_ref[:, 3 * dm:4 * dm])
    og_ref[0] = (jax.nn.sigmoid(o) * _silu(g)).astype(og_ref.dtype)


def _mlstm_in(x, mod3, mod_row, norm_w, w_main_bf, wvt_bf, wgt_bf, b_gates):
    bsz, t, d = x.shape
    dm = M_HEADS * M_HD
    ng = 4 * M_HEADS
    tm = min(2 * ROW_TILE, t)
    tok = lambda b, i: (b, i, 0)
    const = lambda b, i: (0, 0)
    big = jax.ShapeDtypeStruct((bsz, t, dm), BF16)
    return pl.pallas_call(
        _mlstm_in_kernel,
        out_shape=(big, big, jax.ShapeDtypeStruct((bsz, dm, t), BF16), big,
                   jax.ShapeDtypeStruct((bsz, 2, t, M_HEADS), F32),
                   jax.ShapeDtypeStruct((bsz, 2, 5 * M_HEADS, t), F32)),
        grid=(bsz, t // tm),
        in_specs=[pl.BlockSpec((1, tm, d), tok),
                  pl.BlockSpec((1, 3, d), lambda b, i: (mod_row(b), 0, 0)),
                  pl.BlockSpec((1, d), const),
                  pl.BlockSpec(w_main_bf.shape, const),
                  pl.BlockSpec(wvt_bf.shape, const),
                  pl.BlockSpec((ng, d), const),
                  pl.BlockSpec((ng, 1), const)],
        out_specs=(pl.BlockSpec((1, tm, dm), tok), pl.BlockSpec((1, tm, dm), tok),
                   pl.BlockSpec((1, dm, tm), lambda b, i: (b, 0, i)), pl.BlockSpec((1, tm, dm), tok),
                   pl.BlockSpec((1, 2, tm, M_HEADS), lambda b, i: (b, 0, i, 0)),
                   pl.BlockSpec((1, 2, 5 * M_HEADS, tm), lambda b, i: (b, 0, 0, i))),
        compiler_params=_cparams(("parallel", "parallel")),
        name="mlstm_in",
    )(x, mod3, norm_w.reshape(1, d), w_main_bf, wvt_bf, wgt_bf, b_gates.reshape(ng, 1))


def _mlstm_scan_kernel(*refs, has_init, write_state, nc):
    refs = list(refs)
    q_ref, k_ref, vt_ref, gc_ref, gr_ref = refs[:5]
    pos = 5
    if has_init:
        c0_ref, n0_ref, m0_ref = refs[pos:pos + 3]
        pos += 3
    ht_ref = refs[pos]
    pos += 1
    if write_state:
        cout_ref, nout_ref, mout_ref = refs[pos:pos + 3]
        pos += 3
    ct_scr, mscr, hcur, hfwd = refs[pos:pos + 4]

    drn = pl.program_id(1)
    c = pl.program_id(2)
    L = q_ref.shape[1]
    nh = M_HEADS
    pad = ct_scr.shape[1] - M_HD

    @pl.when(c == 0)
    def _init():
        if has_init:
            for h in range(nh):
                ct_scr[h, 0:M_HD, :] = c0_ref[0, 0, h].T
                ct_scr[h, M_HD:M_HD + pad, :] = jnp.concatenate(
                    [n0_ref[0, 0, h:h + 1, :], jnp.zeros((pad - 1, M_HD), F32)], axis=0)
            mscr[...] = m0_ref[0, 0] * LOG2E
        else:
            ct_scr[...] = jnp.zeros(ct_scr.shape, F32)
            mscr[...] = jnp.zeros(mscr.shape, F32)

    si = lax.broadcasted_iota(jnp.int32, (L, L), 0)
    li = lax.broadcasted_iota(jnp.int32, (L, L), 1)
    seen_t = (si - li) * (1 - 2 * drn) <= 0

    gcb = gc_ref[0, 0]
    grb = gr_ref[0, 0]
    q = q_ref[0]
    k = k_ref[0]
    vt = vt_ref[0]
    ones_rows = jnp.where(lax.broadcasted_iota(jnp.int32, (pad, L), 0) == 0, 1.0, 0.0).astype(BF16)

    def head_scores(h):
        hs = slice(h * M_HD, (h + 1) * M_HD)
        m_prev = mscr[h:h + 1, 0:1]
        ct = ct_scr[h]
        m_row = jnp.maximum(grb[4 * nh + h:4 * nh + h + 1, :], m_prev)
        w_t = jnp.exp2(jnp.where(seen_t, gcb[:, h:h + 1], NEG_INF) - m_row)
        r1 = _dot_nt(jnp.concatenate([k[:, hs], ct.astype(BF16)], axis=0), q[:, hs])
        s_t = (r1[0:L, :] * w_t).astype(BF16)
        return m_prev, ct, m_row, s_t, r1[L:, :]

    def head_finish(h, m_prev, ct, m_row, s_t, inter):
        hs = slice(h * M_HD, (h + 1) * M_HD)
        vext = jnp.concatenate([vt[hs, :], ones_rows], axis=0)
        g_r = grb[h:h + 1, :]
        b_r = grb[nh + h:nh + h + 1, :]
        b_last = grb[2 * nh + h:2 * nh + h + 1, 0:1]
        g_max = grb[3 * nh + h:3 * nh + h + 1, 0:1]
        w0 = jnp.exp2(m_prev - m_row)
        tot = _dot(vext, s_t) + w0 * inter
        den = tot[M_HD:M_HD + 1, :]
        floor = jnp.exp2(-(b_r + m_row))
        hcur[hs, :] = tot[0:M_HD, :] / jnp.maximum(jnp.abs(den), floor)

        m_last = jnp.maximum(g_max, m_prev)
        wk = jnp.exp2(g_r - m_last)
        decay = jnp.exp2(m_prev - m_last)
        vw = (vext.astype(F32) * wk).astype(BF16)
        ct_scr[h] = decay * ct + _dot(vw, k[:, hs])
        mscr[h:h + 1, :] = jnp.broadcast_to(b_last + m_last, (1, LANES))

    pending = [head_scores(h) for h in range(min(SCAN_AHEAD, nh))]
    for h in range(nh):
        if h + SCAN_AHEAD < nh:
            pending.append(head_scores(h + SCAN_AHEAD))
        head_finish(h, *pending.pop(0))

    @pl.when(drn == 0)
    def _park():
        hfwd[c] = hcur[...]

    @pl.when(drn == 1)
    def _emit():
        ht_ref[0] = (hcur[...] + hfwd[nc - 1 - c]).astype(ht_ref.dtype)

    if write_state:
        @pl.when(c == nc - 1)
        def _final():
            for h in range(nh):
                cfin = ct_scr[h]
                cout_ref[0, 0, h] = cfin[0:M_HD, :].T
                nout_ref[0, 0, h:h + 1, :] = cfin[M_HD:M_HD + 1, :]
            mout_ref[0, 0] = mscr[...] * LN2


def _mlstm_scan(q, k, vt, gc, gr, init, write_state):
    bsz, t, dm = q.shape
    L = MCHUNK
    nc = t // L
    chunk = lambda b, d, c: c + d * (nc - 1 - 2 * c)
    tok = lambda b, d, c: (b, chunk(b, d, c), 0)
    in_specs = [pl.BlockSpec((1, L, dm), tok)] * 2 + [
        pl.BlockSpec((1, dm, L), lambda b, d, c: (b, 0, chunk(b, d, c))),
        pl.BlockSpec((1, 1, L, gc.shape[-1]), lambda b, d, c: (b, d, chunk(b, d, c), 0)),
        pl.BlockSpec((1, 1, gr.shape[2], L), lambda b, d, c: (b, d, 0, chunk(b, d, c)))]
    args = [q, k, vt, gc, gr]
    st = lambda b, d, c: (b, d, 0, 0)
    st5 = lambda b, d, c: (b, d, 0, 0, 0)
    if init is not None:
        c0, n0, m0 = init
        in_specs += [pl.BlockSpec((1, 1, M_HEADS, M_HD, M_HD), st5),
                     pl.BlockSpec((1, 1, M_HEADS, M_HD), st),
                     pl.BlockSpec((1, 1, M_HEADS, LANES), st)]
        args += [c0, n0, jnp.broadcast_to(m0[..., None], m0.shape + (LANES,))]
    out_shape = [jax.ShapeDtypeStruct((bsz, dm, t), BF16)]
    out_specs = [pl.BlockSpec((1, dm, L), lambda b, d, c: (b, 0, nc - 1 - d * c))]
    if write_state:
        out_shape += [jax.ShapeDtypeStruct((bsz, 2, M_HEADS, M_HD, M_HD), F32),
                      jax.ShapeDtypeStruct((bsz, 2, M_HEADS, M_HD), F32),
                      jax.ShapeDtypeStruct((bsz, 2, M_HEADS, LANES), F32)]
        out_specs += [pl.BlockSpec((1, 1, M_HEADS, M_HD, M_HD), st5),
                      pl.BlockSpec((1, 1, M_HEADS, M_HD), st),
                      pl.BlockSpec((1, 1, M_HEADS, LANES), st)]
    return pl.pallas_call(
        functools.partial(_mlstm_scan_kernel, has_init=init is not None,
                          write_state=write_state, nc=nc),
        out_shape=tuple(out_shape),
        grid=(bsz, 2, nc),
        in_specs=in_specs,
        out_specs=tuple(out_specs),
        scratch_shapes=[pltpu.VMEM((M_HEADS, M_HD + 16, M_HD), F32),
                        pltpu.VMEM((M_HEADS, LANES), F32),
                        pltpu.VMEM((dm, L), F32),
                        pltpu.VMEM((nc, dm, L), F32)],
        compiler_params=_cparams(("parallel", "arbitrary", "arbitrary")),
        name="mlstm_scan",
    )(*args)


def _mlstm_out_kernel(ht_ref, og_ref, x_ref, mod_ref, wo_ref, fw_ref, o_ref):
    hm = ht_ref[0].astype(F32).T * og_ref[0].astype(F32)
    y = _dot(hm.astype(BF16), wo_ref[...])
    x2 = x_ref[0] + mod_ref[0][2:3, :] * y
    ms = jnp.mean(x2 * x2, axis=-1, keepdims=True)
    o_ref[0] = x2 * lax.rsqrt(ms + EPS) * fw_ref[...]


def _mlstm_out(ht, og, x, mod3, mod_row, wo_bf, final_w):
    bsz, t, d = x.shape
    dm = og.shape[-1]
    tm = min(2 * ROW_TILE, t)
    tok = lambda b, i: (b, i, 0)
    return pl.pallas_call(
        _mlstm_out_kernel,
        out_shape=jax.ShapeDtypeStruct((bsz, t, d), F32),
        grid=(bsz, t // tm),
        in_specs=[pl.BlockSpec((1, dm, tm), lambda b, i: (b, 0, i)),
                  pl.BlockSpec((1, tm, dm), tok),
                  pl.BlockSpec((1, tm, d), tok),
                  pl.BlockSpec((1, 3, d), lambda b, i: (mod_row(b), 0, 0)),
                  pl.BlockSpec(wo_bf.shape, lambda b, i: (0, 0)),
                  pl.BlockSpec((1, d), lambda b, i: (0, 0))],
        out_specs=pl.BlockSpec((1, tm, d), tok),
        compiler_params=_cparams(("parallel", "parallel")),
        name="mlstm_out",
    )(ht, og, x, mod3, wo_bf, final_w.reshape(1, d))


def _rope_tables(t):
    nf = HEAD_DIM // 4
    pos = jnp.arange(t)
    row = (pos // GRID_W).astype(F32)
    col = (pos % GRID_W).astype(F32)
    inv = ROPE_BASE ** (-jnp.arange(nf, dtype=F32) / nf)
    ar = row[:, None] * inv[None, :]
    ac = col[:, None] * inv[None, :]
    cos = jnp.concatenate([jnp.cos(ar), jnp.cos(ar), jnp.cos(ac), jnp.cos(ac)], axis=1)
    sin = jnp.concatenate([-jnp.sin(ar), jnp.sin(ar), -jnp.sin(ac), jnp.sin(ac)], axis=1)
    reps = LANES // HEAD_DIM
    return jnp.tile(cos, (1, reps)), jnp.tile(sin, (1, reps))


def kernel(x_prompt, x_sample, cache_k, cache_v, state_C, state_n, state_m, c, c_ctx,
           attn_norm_w, attn_ada_w, attn_ada_b, attn_w_in, attn_sink, attn_w_out,
           mlstm_norm_w, mlstm_ada_w, mlstm_ada_b, mlstm_w_in, mlstm_b_gates, mlstm_w_out,
           final_norm_w):
    assert attn_w_in.shape[0] == 1 and mlstm_w_in.shape[0] == 1, "one layer of each mixer"
    bsz, seq, d = x_prompt.shape
    dbsz, dseq, _ = x_sample.shape
    dkv = N_KV_HEADS * HEAD_DIM
    dm = M_HEADS * M_HD

    n_cond = 1 + dbsz
    cond = jnp.concatenate([c_ctx[None, :], c, jnp.zeros((-n_cond % 8, d), F32)], axis=0)
    attn_mod = _ada(cond, attn_ada_w[0], attn_ada_b[0]).reshape(-1, 3, d)
    mlstm_mod = _ada(cond, mlstm_ada_w[0], mlstm_ada_b[0]).reshape(-1, 3, d)
    ctx_row = lambda b: 0
    lat_row = lambda b: b + 1

    attn_w_in_bf = attn_w_in[0].astype(BF16)
    attn_w_out_bf = attn_w_out[0].astype(BF16)
    attn_wvt_bf = attn_w_in[0, :, 2 * N_HEADS * HEAD_DIM + dkv:].T.astype(BF16)
    w_main_bf = jnp.concatenate([mlstm_w_in[0, :, :2 * dm], mlstm_w_in[0, :, 3 * dm:5 * dm]],
                                axis=1).astype(BF16)
    mlstm_wvt_bf = mlstm_w_in[0, :, 2 * dm:3 * dm].T.astype(BF16)
    wgt_bf = mlstm_w_in[0, :, 5 * dm:].T.astype(BF16)
    mlstm_w_out_bf = mlstm_w_out[0].astype(BF16)

    def mlstm_layer(x, mod_row, init, write_state):
        q, k, vt, og, gc, gr = _mlstm_in(x, mlstm_mod, mod_row, mlstm_norm_w[0], w_main_bf,
                                         mlstm_wvt_bf, wgt_bf, mlstm_b_gates[0])
        outs = _mlstm_scan(q, k, vt, gc, gr, init, write_state)
        y = _mlstm_out(outs[0], og, x, mlstm_mod, mod_row, mlstm_w_out_bf, final_norm_w)
        return y, outs[1:]

    q, sg, k_ctx, vt_ctx, v_ctx = _attn_in(x_prompt, attn_mod, ctx_row, attn_norm_w[0], attn_w_in_bf,
                                           attn_wvt_bf, None, F32, True)
    x1 = _attn(q, sg, x_prompt, attn_mod, ctx_row, k_ctx, vt_ctx, None, None, attn_sink[0], attn_w_out_bf)
    y_prompt, (c_fin, n_fin, m_fin) = mlstm_layer(x1, ctx_row, None, True)

    q, sg, k_lat, vt_lat = _attn_in(x_sample, attn_mod, lat_row, attn_norm_w[0], attn_w_in_bf,
                                    attn_wvt_bf, _rope_tables(dseq), BF16, False)
    kc = cache_k[:, 0].reshape(dbsz, -1, dkv).astype(BF16)
    vct = jnp.swapaxes(cache_v[:, 0].reshape(dbsz, -1, dkv), 1, 2).astype(BF16)
    x1 = _attn(q, sg, x_sample, attn_mod, lat_row, kc, vct, k_lat, vt_lat, attn_sink[0], attn_w_out_bf)
    y_sample, _ = mlstm_layer(x1, lat_row, (state_C[:, 0], state_n[:, 0], state_m[:, 0]), False)

    new_cache_k = k_ctx.reshape(bsz, 1, seq, N_KV_HEADS, HEAD_DIM)
    new_cache_v = v_ctx.reshape(bsz, 1, seq, N_KV_HEADS, HEAD_DIM)
    return (y_prompt, y_sample, new_cache_k, new_cache_v,
            c_fin[:, None], n_fin[:, None], m_fin[:, None, :, :, 0])
```

```python
import functools

import jax
import jax.numpy as jnp
from jax import lax
from jax.experimental import pallas as pl
from jax.experimental.pallas import tpu as pltpu

F32 = jnp.float32
BF16 = jnp.bfloat16

HEAD_DIM = 64
N_KV_HEADS = 4
GROUP = 4
N_HEADS = N_KV_HEADS * GROUP
QBLK = 128
GRID_W = 64
ROPE_BASE = 10000.0
M_HEADS = 8
M_HD = 128
EPS = 1e-6

LANES = 128
VMEM_LIMIT = 48 * 1024 * 1024

MCHUNK = 256
ATTN_QB = 2
SCAN_AHEAD = 4
ROW_TILE = 256

NEG_INF = float("-inf")
LOG2E = 1.4426950408889634
LN2 = 0.6931471805599453


def _cparams(sem):
    return pltpu.CompilerParams(dimension_semantics=sem, vmem_limit_bytes=VMEM_LIMIT)


def _silu(x):
    return x * jax.nn.sigmoid(x)


def _log_sigmoid(x):
    return jnp.minimum(x, 0.0) - jnp.log1p(jnp.exp(-jnp.abs(x)))


def _dot(a, b):
    return jnp.dot(a, b, preferred_element_type=F32)


def _dot_nt(a, b):
    return lax.dot_general(a, b, (((1,), (1,)), ((), ())), preferred_element_type=F32)


def _dot_tn(a, b):
    return lax.dot_general(a, b, (((0,), (0,)), ((), ())), preferred_element_type=F32)


def _split3(x):
    hi = x.astype(BF16)
    r = x - hi.astype(F32)
    mid = r.astype(BF16)
    lo = (r - mid.astype(F32)).astype(BF16)
    return hi, mid, lo


def _prenorm(x, norm_w, mod):
    ms = jnp.mean(x * x, axis=-1, keepdims=True)
    y = x * lax.rsqrt(ms + EPS) * norm_w
    return y * (1.0 + mod[1:2, :]) + mod[0:1, :]


def _ada_kernel(cond_ref, w_ref, b_ref, o_ref):
    a = _silu(cond_ref[...]).astype(BF16)
    o_ref[...] = _dot(a, w_ref[...].astype(BF16)) + b_ref[...]


def _ada(cond8, w, b):
    d, n = w.shape
    tn = 512
    return pl.pallas_call(
        _ada_kernel,
        out_shape=jax.ShapeDtypeStruct((cond8.shape[0], n), F32),
        grid=(n // tn,),
        in_specs=[pl.BlockSpec(cond8.shape, lambda j: (0, 0)),
                  pl.BlockSpec((d, tn), lambda j: (0, j)),
                  pl.BlockSpec((1, tn), lambda j: (0, j))],
        out_specs=pl.BlockSpec((cond8.shape[0], tn), lambda j: (0, j)),
        compiler_params=_cparams(("parallel",)),
        name="ada_mod",
    )(cond8, w, b.reshape(1, n))


def _rope(x, cos, sin, lane):
    first = (lane & 31) < 16
    outs = []
    for c in range(x.shape[1] // LANES):
        xc = x[:, c * LANES:(c + 1) * LANES]
        sw = jnp.where(first, pltpu.roll(xc, LANES - 16, 1), pltpu.roll(xc, 16, 1))
        outs.append(xc * cos + sw * sin)
    return jnp.concatenate(outs, axis=1)


def _attn_in_kernel(*refs, rope, emit_v):
    refs = list(refs)
    x_ref, mod_ref, nw_ref, w_ref, wvt_ref = refs[:5]
    pos = 5
    if rope:
        cos_ref, sin_ref = refs[pos:pos + 2]
        pos += 2
    q_ref, sg_ref, k_ref, vt_ref = refs[pos:pos + 4]
    dq = q_ref.shape[-1]
    dkv = k_ref.shape[-1]
    hb = _prenorm(x_ref[0], nw_ref[...], mod_ref[0]).astype(BF16)
    q = _dot(hb, w_ref[:, 0:dq])
    g = _dot(hb, w_ref[:, dq:2 * dq])
    k = _dot(hb, w_ref[:, 2 * dq:2 * dq + dkv])
    if rope:
        cos = cos_ref[...]
        sin = sin_ref[...]
        lane = lax.broadcasted_iota(jnp.int32, cos.shape, 1)
        q = _rope(q, cos, sin, lane)
        k = _rope(k, cos, sin, lane)
    q_ref[0] = (q * (HEAD_DIM ** -0.5 * LOG2E)).astype(q_ref.dtype)
    sg_ref[0] = _silu(g).astype(sg_ref.dtype)
    k_ref[0] = k.astype(k_ref.dtype)
    vt_ref[0] = _dot_nt(wvt_ref[...], hb).astype(vt_ref.dtype)
    if emit_v:
        v_ref = refs[pos + 4]
        v_ref[0] = _dot(hb, w_ref[:, 2 * dq + dkv:2 * dq + 2 * dkv]).astype(v_ref.dtype)


def _attn_in(x, mod3, mod_row, norm_w, w_bf, wvt_bf, rope_tabs, k_dtype, emit_v):
    bsz, t, d = x.shape
    dq = N_HEADS * HEAD_DIM
    dkv = N_KV_HEADS * HEAD_DIM
    tm = min(4 * ROW_TILE, t)
    rope = rope_tabs is not None
    tok = lambda b, i: (b, i, 0)
    const = lambda b, i: (0, 0)
    in_specs = [pl.BlockSpec((1, tm, d), tok),
                pl.BlockSpec((1, 3, d), lambda b, i: (mod_row(b), 0, 0)),
                pl.BlockSpec((1, d), const),
                pl.BlockSpec(w_bf.shape, const),
                pl.BlockSpec(wvt_bf.shape, const)]
    args = [x, mod3, norm_w.reshape(1, d), w_bf, wvt_bf]
    if rope:
        in_specs += [pl.BlockSpec((tm, LANES), lambda b, i: (i, 0))] * 2
        args += list(rope_tabs)
    out_shape = [jax.ShapeDtypeStruct((bsz, t, dq), BF16),
                 jax.ShapeDtypeStruct((bsz, t, dq), BF16),
                 jax.ShapeDtypeStruct((bsz, t, dkv), k_dtype),
                 jax.ShapeDtypeStruct((bsz, dkv, t), BF16)]
    out_specs = [pl.BlockSpec((1, tm, dq), tok), pl.BlockSpec((1, tm, dq), tok),
                 pl.BlockSpec((1, tm, dkv), tok),
                 pl.BlockSpec((1, dkv, tm), lambda b, i: (b, 0, i))]
    if emit_v:
        out_shape.append(jax.ShapeDtypeStruct((bsz, t, dkv), F32))
        out_specs.append(pl.BlockSpec((1, tm, dkv), tok))
    return pl.pallas_call(
        functools.partial(_attn_in_kernel, rope=rope, emit_v=emit_v),
        out_shape=tuple(out_shape),
        grid=(bsz, t // tm),
        in_specs=in_specs,
        out_specs=tuple(out_specs),
        compiler_params=_cparams(("parallel", "parallel")),
        name="attn_in_rope" if rope else "attn_in",
    )(*args)


def _attn_kernel(*refs, window, nb):
    if window:
        (q_ref, sg_ref, x_ref, mod_ref, kc_ref, vct_ref, kp_ref, km_ref, kn_ref,
         vpt_ref, vmt_ref, vnt_ref, sink_ref, wo_ref, o_ref, s_scr, p_scr, ot_scr) = refs
    else:
        q_ref, sg_ref, x_ref, mod_ref, kc_ref, vct_ref, sink_ref, wo_ref, o_ref, s_scr, p_scr, ot_scr = refs
    step = pl.program_id(1)
    nqb = q_ref.shape[1] // QBLK
    n_ctx = kc_ref.shape[1] // QBLK
    cols = GROUP * QBLK
    if window:
        kj = lax.broadcasted_iota(jnp.int32, (QBLK, cols), 0)
        qi = lax.broadcasted_iota(jnp.int32, (QBLK, cols), 1) & (QBLK - 1)
        after_diag = kj >= qi
        before_diag = kj <= qi
    ones_rows = jnp.where(lax.broadcasted_iota(jnp.int32, (16, QBLK), 0) == 0, 1.0, 0.0).astype(BF16)
    n_blk = n_ctx + (3 if window else 0)

    def window_blocks(qb, cs, kp, km, kn, lanes):
        def mid(j):
            sl = slice(j * QBLK, (j + 1) * QBLK)
            return km[0, cs, sl] if lanes else km[0, sl, cs]
        first = kp[0, cs, :] if lanes else kp[0][:, cs]
        last = kn[0, cs, :] if lanes else kn[0][:, cs]
        return [first if qb == 0 else mid(qb - 1), mid(qb), last if qb == nqb - 1 else mid(qb + 1)]

    def block_masks(qb):
        if not window:
            return [None] * n_ctx
        prev_ok = after_diag & (step > 0) if qb == 0 else after_diag
        next_ok = before_diag & (step < nb // nqb - 1) if qb == nqb - 1 else before_diag
        return [None] * n_ctx + [prev_ok, None, next_ok]

    def scores(qb, kvh):
        u = qb * N_KV_HEADS + kvh
        cs = slice(kvh * HEAD_DIM, (kvh + 1) * HEAD_DIM)
        heads = [kvh * GROUP + j for j in range(GROUP)]
        qq = q_ref[0, qb * QBLK:(qb + 1) * QBLK, :]
        q4 = jnp.concatenate([qq[:, h * HEAD_DIM:(h + 1) * HEAD_DIM] for h in heads], axis=0)
        sink_row = jnp.concatenate(
            [jnp.broadcast_to(sink_ref[0:1, h:h + 1], (1, QBLK)) for h in heads], axis=1) * LOG2E
        keys = [kc_ref[0, j * QBLK:(j + 1) * QBLK, cs].astype(BF16) for j in range(n_ctx)]
        if window:
            keys += window_blocks(qb, cs, kp_ref, km_ref, kn_ref, False)
        st_all = _dot_nt(jnp.concatenate(keys, axis=0), q4)
        macc = jnp.full((8, cols), NEG_INF, F32)
        for j, ok in enumerate(block_masks(qb)):
            s_blk = st_all[j * QBLK:(j + 1) * QBLK, :]
            if ok is not None:
                s_blk = jnp.where(ok, s_blk, NEG_INF)
            s_scr[u, j] = s_blk
            macc = jnp.maximum(macc, jnp.max(s_blk.reshape(QBLK // 8, 8, cols), axis=0))
        return jnp.maximum(jnp.max(macc, axis=0, keepdims=True), sink_row), sink_row

    def weighted_values(qb, kvh, m_row, sink_row):
        u = qb * N_KV_HEADS + kvh
        cs = slice(kvh * HEAD_DIM, (kvh + 1) * HEAD_DIM)
        for j in range(n_blk):
            p_scr[u, j * QBLK:(j + 1) * QBLK, :] = jnp.exp2(s_scr[u, j] - m_row).astype(BF16)
        vts = [vct_ref[0, cs, j * QBLK:(j + 1) * QBLK] for j in range(n_ctx)]
        if window:
            vts += window_blocks(qb, cs, vpt_ref, vmt_ref, vnt_ref, True)
        vt_ext = jnp.concatenate(
            [jnp.concatenate(vts, axis=1), jnp.tile(ones_rows, (1, n_blk))], axis=0)
        acc = _dot(vt_ext, p_scr[u])
        den = acc[HEAD_DIM:HEAD_DIM + 1, :] + jnp.exp2(sink_row - m_row)
        o_t = acc[0:HEAD_DIM, :] / den
        for j in range(GROUP):
            h = kvh * GROUP + j
            ot_scr[h * HEAD_DIM:(h + 1) * HEAD_DIM, qb * QBLK:(qb + 1) * QBLK] = o_t[:, j * QBLK:(j + 1) * QBLK]

    units = [(qb, kvh) for qb in range(nqb) for kvh in range(N_KV_HEADS)]
    stats = [scores(qb, kvh) for qb, kvh in units]
    for (qb, kvh), st in zip(units, stats):
        weighted_values(qb, kvh, *st)
    z = (ot_scr[...].T * sg_ref[0].astype(F32)).astype(BF16)
    y = _dot(z, wo_ref[...])
    o_ref[0] = x_ref[0] + mod_ref[0][2:3, :] * y


def _attn(q, sg, x, mod3, mod_row, kc, vct, k_lat, vt_lat, sink, wo_bf):
    bsz, t, d = x.shape
    dq = q.shape[-1]
    dkv = kc.shape[-1]
    p_len = kc.shape[1]
    nb = t // QBLK
    nqb = ATTN_QB
    rows = nqb * QBLK
    window = k_lat is not None
    tok = lambda b, i: (b, i, 0)
    in_specs = [pl.BlockSpec((1, rows, dq), tok),
                pl.BlockSpec((1, rows, dq), tok),
                pl.BlockSpec((1, rows, d), tok),
                pl.BlockSpec((1, 3, d), lambda b, i: (mod_row(b), 0, 0)),
                pl.BlockSpec((1, p_len, dkv), lambda b, i: (b, 0, 0)),
                pl.BlockSpec((1, dkv, p_len), lambda b, i: (b, 0, 0))]
    args = [q, sg, x, mod3, kc, vct]
    n_blocks = p_len // QBLK
    if window:
        prev = lambda i: jnp.maximum(i * nqb - 1, 0)
        nxt = lambda i: jnp.minimum((i + 1) * nqb, nb - 1)
        in_specs += [pl.BlockSpec((1, QBLK, dkv), lambda b, i: (b, prev(i), 0)),
                     pl.BlockSpec((1, rows, dkv), tok),
                     pl.BlockSpec((1, QBLK, dkv), lambda b, i: (b, nxt(i), 0)),
                     pl.BlockSpec((1, dkv, QBLK), lambda b, i: (b, 0, prev(i))),
                     pl.BlockSpec((1, dkv, rows), lambda b, i: (b, 0, i)),
                     pl.BlockSpec((1, dkv, QBLK), lambda b, i: (b, 0, nxt(i)))]
        args += [k_lat] * 3 + [vt_lat] * 3
        n_blocks += 3
    in_specs += [pl.BlockSpec((1, N_HEADS), lambda b, i: (0, 0)),
                 pl.BlockSpec(wo_bf.shape, lambda b, i: (0, 0))]
    args += [sink.reshape(1, N_HEADS), wo_bf]
    units = nqb * N_KV_HEADS
    return pl.pallas_call(
        functools.partial(_attn_kernel, window=window, nb=nb),
        out_shape=jax.ShapeDtypeStruct((bsz, t, d), F32),
        grid=(bsz, nb // nqb),
        in_specs=in_specs,
        out_specs=pl.BlockSpec((1, rows, d), tok),
        scratch_shapes=[pltpu.VMEM((units, n_blocks, QBLK, GROUP * QBLK), F32),
                        pltpu.VMEM((units, n_blocks * QBLK, GROUP * QBLK), BF16),
                        pltpu.VMEM((dq, rows), F32)],
        compiler_params=_cparams(("parallel", "parallel")),
        name="attn_window" if window else "attn_ctx",
    )(*args)


def _mlstm_in_kernel(x_ref, mod_ref, nw_ref, w_ref, wvt_ref, wgt_ref, bgt_ref,
                     q_ref, k_ref, vt_ref, og_ref, gc_ref, gr_ref):
    dm = q_ref.shape[-1]
    nh = M_HEADS
    L = MCHUNK
    hb = _prenorm(x_ref[0], nw_ref[...], mod_ref[0]).astype(BF16)

    gr = _dot_nt(wgt_ref[...], hb) + bgt_ref[...]
    n_chunks = x_ref.shape[1] // L
    ri = lax.broadcasted_iota(jnp.int32, (L, L), 0)
    ci = lax.broadcasted_iota(jnp.int32, (L, L), 1)
    lane = lax.broadcasted_iota(jnp.int32, (n_chunks * nh, L), 1)
    g_rows = []
    for dr in range(2):
        before = (ri <= ci) if dr == 0 else (ri >= ci)
        tri = jnp.where(before, 1.0, 0.0).astype(BF16)
        base = dr * 2 * nh
        lf = _log_sigmoid(gr[base + nh:base + 2 * nh, :]) * LOG2E
        gi = gr[base:base + nh, :] * LOG2E
        lf_st = jnp.concatenate([lf[:, c * L:(c + 1) * L] for c in range(n_chunks)], axis=0)
        b_st = sum(_dot(piece, tri) for piece in _split3(lf_st))
        g_st = jnp.concatenate([gi[:, c * L:(c + 1) * L] for c in range(n_chunks)], axis=0) - b_st
        run = g_st
        step = 1
        while step < L:
            if dr == 0:
                run = jnp.where(lane >= step, jnp.maximum(run, pltpu.roll(run, step, 1)), run)
            else:
                run = jnp.where(lane < L - step, jnp.maximum(run, pltpu.roll(run, L - step, 1)), run)
            step *= 2
        for cidx in range(n_chunks):
            rows = slice(cidx * L, (cidx + 1) * L)
            blk = slice(cidx * nh, (cidx + 1) * nh)
            b_last = jnp.sum(lf[:, rows], axis=1, keepdims=True)
            g_max = jnp.max(g_st[blk, :], axis=1, keepdims=True)
            g_rows.append(g_st[blk, :])
            gr_ref[0, dr, :, rows] = jnp.concatenate(
                [g_st[blk, :], b_st[blk, :], jnp.broadcast_to(b_last, (nh, L)),
                 jnp.broadcast_to(g_max, (nh, L)), run[blk, :]], axis=0)
    g_sq = jnp.concatenate(g_rows + [jnp.zeros((L - len(g_rows) * nh, L), F32)], axis=0).T
    for dr in range(2):
        for cidx in range(n_chunks):
            idx = dr * n_chunks + cidx
            gc_ref[0, dr, cidx * L:(cidx + 1) * L, :] = g_sq[:, idx * nh:(idx + 1) * nh]

    o = _dot(hb, w_ref[:, 3 * dm:4 * dm])
    g = _dot(hb, w_ref[:, 4 * dm:5 * dm])
    og_ref[0] = (jax.nn.sigmoid(o) * _silu(g)).astype(og_ref.dtype)
    q_ref[0] = _dot(hb, w_ref[:, 0:dm]).astype(q_ref.dtype)
    k_ref[0] = (_dot(hb, w_ref[:, dm:2 * dm]) * (M_HD ** -0.5)).astype(k_ref.dtype)
    vt_ref[0] = _dot_nt(wvt_ref[...], hb).astype(vt_ref.dtype)


def _mlstm_in(x, mod3, mod_row, norm_w, w_main_bf, wvt_bf, wgt_bf, b_gates):
    bsz, t, d = x.shape
    dm = M_HEADS * M_HD
    ng = 4 * M_HEADS
    tm = min(2 * ROW_TILE, t)
    tok = lambda b, i: (b, i, 0)
    const = lambda b, i: (0, 0)
    big = jax.ShapeDtypeStruct((bsz, t, dm), BF16)
    return pl.pallas_call(
        _mlstm_in_kernel,
        out_shape=(big, big, jax.ShapeDtypeStruct((bsz, dm, t), BF16), big,
                   jax.ShapeDtypeStruct((bsz, 2, t, M_HEADS), F32),
                   jax.ShapeDtypeStruct((bsz, 2, 5 * M_HEADS, t), F32)),
        grid=(bsz, t // tm),
        in_specs=[pl.BlockSpec((1, tm, d), tok),
                  pl.BlockSpec((1, 3, d), lambda b, i: (mod_row(b), 0, 0)),
                  pl.BlockSpec((1, d), const),
                  pl.BlockSpec(w_main_bf.shape, const),
                  pl.BlockSpec(wvt_bf.shape, const),
                  pl.BlockSpec((ng, d), const),
                  pl.BlockSpec((ng, 1), const)],
        out_specs=(pl.BlockSpec((1, tm, dm), tok), pl.BlockSpec((1, tm, dm), tok),
                   pl.BlockSpec((1, dm, tm), lambda b, i: (b, 0, i)), pl.BlockSpec((1, tm, dm), tok),
                   pl.BlockSpec((1, 2, tm, M_HEADS), lambda b, i: (b, 0, i, 0)),
                   pl.BlockSpec((1, 2, 5 * M_HEADS, tm), lambda b, i: (b, 0, 0, i))),
        compiler_params=_cparams(("parallel", "parallel")),
        name="mlstm_in",
    )(x, mod3, norm_w.reshape(1, d), w_main_bf, wvt_bf, wgt_bf, b_gates.reshape(ng, 1))


def _mlstm_scan_kernel(*refs, has_init, write_state, nc):
    refs = list(refs)
    q_ref, k_ref, vt_ref, gc_ref, gr_ref = refs[:5]
    pos = 5
    if has_init:
        c0_ref, n0_ref, m0_ref = refs[pos:pos + 3]
        pos += 3
    ht_ref = refs[pos]
    pos += 1
    if write_state:
        cout_ref, nout_ref, mout_ref = refs[pos:pos + 3]
        pos += 3
    ct_scr, mscr, hcur, hfwd = refs[pos:pos + 4]

    drn = pl.program_id(1)
    c = pl.program_id(2)
    L = q_ref.shape[1]
    nh = M_HEADS
    pad = ct_scr.shape[1] - M_HD

    @pl.when(c == 0)
    def _init():
        if has_init:
            for h in range(nh):
                ct_scr[h, 0:M_HD, :] = c0_ref[0, 0, h].T
                ct_scr[h, M_HD:M_HD + pad, :] = jnp.concatenate(
                    [n0_ref[0, 0, h:h + 1, :], jnp.zeros((pad - 1, M_HD), F32)], axis=0)
            mscr[...] = m0_ref[0, 0] * LOG2E
        else:
            ct_scr[...] = jnp.zeros(ct_scr.shape, F32)
            mscr[...] = jnp.zeros(mscr.shape, F32)

    si = lax.broadcasted_iota(jnp.int32, (L, L), 0)
    li = lax.broadcasted_iota(jnp.int32, (L, L), 1)
    seen_t = (si - li) * (1 - 2 * drn) <= 0

    gcb = gc_ref[0, 0]
    grb = gr_ref[0, 0]
    q = q_ref[0]
    k = k_ref[0]
    vt = vt_ref[0]
    ones_rows = jnp.where(lax.broadcasted_iota(jnp.int32, (pad, L), 0) == 0, 1.0, 0.0).astype(BF16)

    def head_scores(h):
        hs = slice(h * M_HD, (h + 1) * M_HD)
        m_prev = mscr[h:h + 1, 0:1]
        ct = ct_scr[h]
        m_row = jnp.maximum(grb[4 * nh + h:4 * nh + h + 1, :], m_prev)
        w_t = jnp.exp2(jnp.where(seen_t, gcb[:, h:h + 1], NEG_INF) - m_row)
        r1 = _dot_nt(jnp.concatenate([k[:, hs], ct.astype(BF16)], axis=0), q[:, hs])
        s_t = (r1[0:L, :] * w_t).astype(BF16)
        return m_prev, ct, m_row, s_t, r1[L:, :]

    def head_finish(h, m_prev, ct, m_row, s_t, inter):
        hs = slice(h * M_HD, (h + 1) * M_HD)
        vext = jnp.concatenate([vt[hs, :], ones_rows], axis=0)
        g_r = grb[h:h + 1, :]
        b_r = grb[nh + h:nh + h + 1, :]
        b_last = grb[2 * nh + h:2 * nh + h + 1, 0:1]
        g_max = grb[3 * nh + h:3 * nh + h + 1, 0:1]
        w0 = jnp.exp2(m_prev - m_row)
        tot = _dot(vext, s_t) + w0 * inter
        den = tot[M_HD:M_HD + 1, :]
        floor = jnp.exp2(-(b_r + m_row))
        hcur[hs, :] = tot[0:M_HD, :] / jnp.maximum(jnp.abs(den), floor)

        m_last = jnp.maximum(g_max, m_prev)
        wk = jnp.exp2(g_r - m_last)
        decay = jnp.exp2(m_prev - m_last)
        vw = (vext.astype(F32) * wk).astype(BF16)
        ct_scr[h] = decay * ct + _dot(vw, k[:, hs])
        mscr[h:h + 1, :] = jnp.broadcast_to(b_last + m_last, (1, LANES))

    pending = [head_scores(h) for h in range(min(SCAN_AHEAD, nh))]
    for h in range(nh):
        if h + SCAN_AHEAD < nh:
            pending.append(head_scores(h + SCAN_AHEAD))
        head_finish(h, *pending.pop(0))

    @pl.when(drn == 0)
    def _park():
        hfwd[c] = hcur[...]

    @pl.when(drn == 1)
    def _emit():
        ht_ref[0] = (hcur[...] + hfwd[nc - 1 - c]).astype(ht_ref.dtype)

    if write_state:
        @pl.when(c == nc - 1)
        def _final():
            for h in range(nh):
                cfin = ct_scr[h]
                cout_ref[0, 0, h] = cfin[0:M_HD, :].T
                nout_ref[0, 0, h:h + 1, :] = cfin[M_HD:M_HD + 1, :]
            mout_ref[0, 0] = mscr[...] * LN2


def _mlstm_scan(q, k, vt, gc, gr, init, write_state):
    bsz, t, dm = q.shape
    L = MCHUNK
    nc = t // L
    chunk = lambda b, d, c: c + d * (nc - 1 - 2 * c)
    tok = lambda b, d, c: (b, chunk(b, d, c), 0)
    in_specs = [pl.BlockSpec((1, L, dm), tok)] * 2 + [
        pl.BlockSpec((1, dm, L), lambda b, d, c: (b, 0, chunk(b, d, c))),
        pl.BlockSpec((1, 1, L, gc.shape[-1]), lambda b, d, c: (b, d, chunk(b, d, c), 0)),
        pl.BlockSpec((1, 1, gr.shape[2], L), lambda b, d, c: (b, d, 0, chunk(b, d, c)))]
    args = [q, k, vt, gc, gr]
    st = lambda b, d, c: (b, d, 0, 0)
    st5 = lambda b, d, c: (b, d, 0, 0, 0)
    if init is not None:
        c0, n0, m0 = init
        in_specs += [pl.BlockSpec((1, 1, M_HEADS, M_HD, M_HD), st5),
                     pl.BlockSpec((1, 1, M_HEADS, M_HD), st),
                     pl.BlockSpec((1, 1, M_HEADS, LANES), st)]
        args += [c0, n0, jnp.broadcast_to(m0[..., None], m0.shape + (LANES,))]
    out_shape = [jax.ShapeDtypeStruct((bsz, dm, t), BF16)]
    out_specs = [pl.BlockSpec((1, dm, L), lambda b, d, c: (b, 0, nc - 1 - d * c))]
    if write_state:
        out_shape += [jax.ShapeDtypeStruct((bsz, 2, M_HEADS, M_HD, M_HD), F32),
                      jax.ShapeDtypeStruct((bsz, 2, M_HEADS, M_HD), F32),
                      jax.ShapeDtypeStruct((bsz, 2, M_HEADS, LANES), F32)]
        out_specs += [pl.BlockSpec((1, 1, M_HEADS, M_HD, M_HD), st5),
                      pl.BlockSpec((1, 1, M_HEADS, M_HD), st),
                      pl.BlockSpec((1, 1, M_HEADS, LANES), st)]
    return pl.pallas_call(
        functools.partial(_mlstm_scan_kernel, has_init=init is not None,
                          write_state=write_state, nc=nc),
        out_shape=tuple(out_shape),
        grid=(bsz, 2, nc),
        in_specs=in_specs,
        out_specs=tuple(out_specs),
        scratch_shapes=[pltpu.VMEM((M_HEADS, M_HD + 16, M_HD), F32),
                        pltpu.VMEM((M_HEADS, LANES), F32),
                        pltpu.VMEM((dm, L), F32),
                        pltpu.VMEM((nc, dm, L), F32)],
        compiler_params=_cparams(("parallel", "arbitrary", "arbitrary")),
        name="mlstm_scan",
    )(*args)


def _mlstm_out_kernel(ht_ref, og_ref, x_ref, mod_ref, wo_ref, fw_ref, o_ref):
    hm = ht_ref[0].astype(F32).T * og_ref[0].astype(F32)
    y = _dot(hm.astype(BF16), wo_ref[...])
    x2 = x_ref[0] + mod_ref[0][2:3, :] * y
    ms = jnp.mean(x2 * x2, axis=-1, keepdims=True)
    o_ref[0] = x2 * lax.rsqrt(ms + EPS) * fw_ref[...]


def _mlstm_out(ht, og, x, mod3, mod_row, wo_bf, final_w):
    bsz, t, d = x.shape
    dm = og.shape[-1]
    tm = min(2 * ROW_TILE, t)
    tok = lambda b, i: (b, i, 0)
    return pl.pallas_call(
        _mlstm_out_kernel,
        out_shape=jax.ShapeDtypeStruct((bsz, t, d), F32),
        grid=(bsz, t // tm),
        in_specs=[pl.BlockSpec((1, dm, tm), lambda b, i: (b, 0, i)),
                  pl.BlockSpec((1, tm, dm), tok),
                  pl.BlockSpec((1, tm, d), tok),
                  pl.BlockSpec((1, 3, d), lambda b, i: (mod_row(b), 0, 0)),
                  pl.BlockSpec(wo_bf.shape, lambda b, i: (0, 0)),
                  pl.BlockSpec((1, d), lambda b, i: (0, 0))],
        out_specs=pl.BlockSpec((1, tm, d), tok),
        compiler_params=_cparams(("parallel", "parallel")),
        name="mlstm_out",
    )(ht, og, x, mod3, wo_bf, final_w.reshape(1, d))


def _rope_tables(t):
    nf = HEAD_DIM // 4
    pos = jnp.arange(t)
    row = (pos // GRID_W).astype(F32)
    col = (pos % GRID_W).astype(F32)
    inv = ROPE_BASE ** (-jnp.arange(nf, dtype=F32) / nf)
    ar = row[:, None] * inv[None, :]
    ac = col[:, None] * inv[None, :]
    cos = jnp.concatenate([jnp.cos(ar), jnp.cos(ar), jnp.cos(ac), jnp.cos(ac)], axis=1)
    sin = jnp.concatenate([-jnp.sin(ar), jnp.sin(ar), -jnp.sin(ac), jnp.sin(ac)], axis=1)
    reps = LANES // HEAD_DIM
    return jnp.tile(cos, (1, reps)), jnp.tile(sin, (1, reps))


def kernel(x_prompt, x_sample, cache_k, cache_v, state_C, state_n, state_m, c, c_ctx,
           attn_norm_w, attn_ada_w, attn_ada_b, attn_w_in, attn_sink, attn_w_out,
           mlstm_norm_w, mlstm_ada_w, mlstm_ada_b, mlstm_w_in, mlstm_b_gates, mlstm_w_out,
           final_norm_w):
    assert attn_w_in.shape[0] == 1 and mlstm_w_in.shape[0] == 1, "one layer of each mixer"
    bsz, seq, d = x_prompt.shape
    dbsz, dseq, _ = x_sample.shape
    dkv = N_KV_HEADS * HEAD_DIM
    dm = M_HEADS * M_HD

    n_cond = 1 + dbsz
    cond = jnp.concatenate([c_ctx[None, :], c, jnp.zeros((-n_cond % 8, d), F32)], axis=0)
    attn_mod = _ada(cond, attn_ada_w[0], attn_ada_b[0]).reshape(-1, 3, d)
    mlstm_mod = _ada(cond, mlstm_ada_w[0], mlstm_ada_b[0]).reshape(-1, 3, d)
    ctx_row = lambda b: 0
    lat_row = lambda b: b + 1

    attn_w_in_bf = attn_w_in[0].astype(BF16)
    attn_w_out_bf = attn_w_out[0].astype(BF16)
    attn_wvt_bf = attn_w_in_bf[:, 2 * N_HEADS * HEAD_DIM + dkv:].T
    w_main_bf = mlstm_w_in[0].astype(BF16)
    mlstm_wvt_bf = w_main_bf[:, 2 * dm:3 * dm].T
    wgt_bf = w_main_bf[:, 5 * dm:].T
    mlstm_w_out_bf = mlstm_w_out[0].astype(BF16)

    def mlstm_layer(x, mod_row, init, write_state):
        q, k, vt, og, gc, gr = _mlstm_in(x, mlstm_mod, mod_row, mlstm_norm_w[0], w_main_bf,
                                         mlstm_wvt_bf, wgt_bf, mlstm_b_gates[0])
        outs = _mlstm_scan(q, k, vt, gc, gr, init, write_state)
        y = _mlstm_out(outs[0], og, x, mlstm_mod, mod_row, mlstm_w_out_bf, final_norm_w)
        return y, outs[1:]

    q, sg, k_ctx, vt_ctx, v_ctx = _attn_in(x_prompt, attn_mod, ctx_row, attn_norm_w[0], attn_w_in_bf,
                                           attn_wvt_bf, None, F32, True)
    x1 = _attn(q, sg, x_prompt, attn_mod, ctx_row, k_ctx, vt_ctx, None, None, attn_sink[0], attn_w_out_bf)
    y_prompt, (c_fin, n_fin, m_fin) = mlstm_layer(x1, ctx_row, None, True)

    q, sg, k_lat, vt_lat = _attn_in(x_sample, attn_mod, lat_row, attn_norm_w[0], attn_w_in_bf,
                                    attn_wvt_bf, _rope_tables(dseq), BF16, False)
    kc = cache_k[:, 0].reshape(dbsz, -1, dkv).astype(BF16)
    vct = jnp.swapaxes(cache_v[:, 0].reshape(dbsz, -1, dkv), 1, 2).astype(BF16)
    x1 = _attn(q, sg, x_sample, attn_mod, lat_row, kc, vct, k_lat, vt_lat, attn_sink[0], attn_w_out_bf)
    y_sample, _ = mlstm_layer(x1, lat_row, (state_C[:, 0], state_n[:, 0], state_m[:, 0]), False)

    new_cache_k = k_ctx.reshape(bsz, 1, seq, N_KV_HEADS, HEAD_DIM)
    new_cache_v = v_ctx.reshape(bsz, 1, seq, N_KV_HEADS, HEAD_DIM)
    return (y_prompt, y_sample, new_cache_k, new_cache_v,
            c_fin[:, None], n_fin[:, None], m_fin[:, None, :, :, 0])
```

```python
import functools

import jax
import jax.numpy as jnp
from jax import lax
from jax.experimental import pallas as pl
from jax.experimental.pallas import tpu as pltpu

F32 = jnp.float32
BF16 = jnp.bfloat16

HEAD_DIM = 64
N_KV_HEADS = 4
GROUP = 4
N_HEADS = N_KV_HEADS * GROUP
QBLK = 128
GRID_W = 64
ROPE_BASE = 10000.0
M_HEADS = 8
M_HD = 128
EPS = 1e-6

LANES = 128
VMEM_LIMIT = 48 * 1024 * 1024

MCHUNK = 256
ATTN_QB = 2
SCAN_AHEAD = 4
ROW_TILE = 256

NEG_INF = float("-inf")
LOG2E = 1.4426950408889634
LN2 = 0.6931471805599453


def _cparams(sem):
    return pltpu.CompilerParams(dimension_semantics=sem, vmem_limit_bytes=VMEM_LIMIT)


def _silu(x):
    return x * jax.nn.sigmoid(x)


def _log_sigmoid(x):
    return jnp.minimum(x, 0.0) - jnp.log1p(jnp.exp(-jnp.abs(x)))


def _dot(a, b):
    return jnp.dot(a, b, preferred_element_type=F32)


def _dot_nt(a, b):
    return lax.dot_general(a, b, (((1,), (1,)), ((), ())), preferred_element_type=F32)


def _dot_tn(a, b):
    return lax.dot_general(a, b, (((0,), (0,)), ((), ())), preferred_element_type=F32)


def _split3(x):
    hi = x.astype(BF16)
    r = x - hi.astype(F32)
    mid = r.astype(BF16)
    lo = (r - mid.astype(F32)).astype(BF16)
    return hi, mid, lo


def _prenorm(x, norm_w, mod):
    ms = jnp.mean(x * x, axis=-1, keepdims=True)
    y = x * lax.rsqrt(ms + EPS) * norm_w
    return y * (1.0 + mod[1:2, :]) + mod[0:1, :]


def _ada_kernel(cond_ref, w_ref, b_ref, o_ref):
    a = _silu(cond_ref[...]).astype(BF16)
    o_ref[...] = _dot(a, w_ref[...].astype(BF16)) + b_ref[...]


def _ada(cond8, w, b):
    d, n = w.shape
    tn = 512
    return pl.pallas_call(
        _ada_kernel,
        out_shape=jax.ShapeDtypeStruct((cond8.shape[0], n), F32),
        grid=(n // tn,),
        in_specs=[pl.BlockSpec(cond8.shape, lambda j: (0, 0)),
                  pl.BlockSpec((d, tn), lambda j: (0, j)),
                  pl.BlockSpec((1, tn), lambda j: (0, j))],
        out_specs=pl.BlockSpec((cond8.shape[0], tn), lambda j: (0, j)),
        compiler_params=_cparams(("parallel",)),
        name="ada_mod",
    )(cond8, w, b.reshape(1, n))


def _rope(x, cos, sin, lane):
    first = (lane & 31) < 16
    outs = []
    for c in range(x.shape[1] // LANES):
        xc = x[:, c * LANES:(c + 1) * LANES]
        sw = jnp.where(first, pltpu.roll(xc, LANES - 16, 1), pltpu.roll(xc, 16, 1))
        outs.append(xc * cos + sw * sin)
    return jnp.concatenate(outs, axis=1)


def _attn_in_kernel(*refs, rope, emit_v):
    refs = list(refs)
    x_ref, mod_ref, nw_ref, w_ref, wvt_ref = refs[:5]
    pos = 5
    if rope:
        cos_ref, sin_ref = refs[pos:pos + 2]
        pos += 2
    q_ref, sg_ref, k_ref, vt_ref = refs[pos:pos + 4]
    dq = q_ref.shape[-1]
    dkv = k_ref.shape[-1]
    hb = _prenorm(x_ref[0], nw_ref[...], mod_ref[0]).astype(BF16)
    q = _dot(hb, w_ref[:, 0:dq])
    g = _dot(hb, w_ref[:, dq:2 * dq])
    k = _dot(hb, w_ref[:, 2 * dq:2 * dq + dkv])
    if rope:
        cos = cos_ref[...]
        sin = sin_ref[...]
        lane = lax.broadcasted_iota(jnp.int32, cos.shape, 1)
        q = _rope(q, cos, sin, lane)
        k = _rope(k, cos, sin, lane)
    q_ref[0] = (q * (HEAD_DIM ** -0.5 * LOG2E)).astype(q_ref.dtype)
    sg_ref[0] = _silu(g).astype(sg_ref.dtype)
    k_ref[0] = k.astype(k_ref.dtype)
    vt_ref[0] = _dot_nt(wvt_ref[...], hb).astype(vt_ref.dtype)
    if emit_v:
        v_ref = refs[pos + 4]
        v_ref[0] = _dot(hb, w_ref[:, 2 * dq + dkv:2 * dq + 2 * dkv]).astype(v_ref.dtype)


def _attn_in(x, mod3, mod_row, norm_w, w_bf, wvt_bf, rope_tabs, k_dtype, emit_v):
    bsz, t, d = x.shape
    dq = N_HEADS * HEAD_DIM
    dkv = N_KV_HEADS * HEAD_DIM
    tm = min(4 * ROW_TILE, t)
    rope = rope_tabs is not None
    tok = lambda b, i: (b, i, 0)
    const = lambda b, i: (0, 0)
    in_specs = [pl.BlockSpec((1, tm, d), tok),
                pl.BlockSpec((1, 3, d), lambda b, i: (mod_row(b), 0, 0)),
                pl.BlockSpec((1, d), const),
                pl.BlockSpec(w_bf.shape, const),
                pl.BlockSpec(wvt_bf.shape, const)]
    args = [x, mod3, norm_w.reshape(1, d), w_bf, wvt_bf]
    if rope:
        in_specs += [pl.BlockSpec((tm, LANES), lambda b, i: (i, 0))] * 2
        args += list(rope_tabs)
    out_shape = [jax.ShapeDtypeStruct((bsz, t, dq), BF16),
                 jax.ShapeDtypeStruct((bsz, t, dq), BF16),
                 jax.ShapeDtypeStruct((bsz, t, dkv), k_dtype),
                 jax.ShapeDtypeStruct((bsz, dkv, t), BF16)]
    out_specs = [pl.BlockSpec((1, tm, dq), tok), pl.BlockSpec((1, tm, dq), tok),
                 pl.BlockSpec((1, tm, dkv), tok),
                 pl.BlockSpec((1, dkv, tm), lambda b, i: (b, 0, i))]
    if emit_v:
        out_shape.append(jax.ShapeDtypeStruct((bsz, t, dkv), F32))
        out_specs.append(pl.BlockSpec((1, tm, dkv), tok))
    return pl.pallas_call(
        functools.partial(_attn_in_kernel, rope=rope, emit_v=emit_v),
        out_shape=tuple(out_shape),
        grid=(bsz, t // tm),
        in_specs=in_specs,
        out_specs=tuple(out_specs),
        compiler_params=_cparams(("parallel", "parallel")),
        name="attn_in_rope" if rope else "attn_in",
    )(*args)


def _attn_kernel(*refs, window, nb):
    if window:
        (q_ref, sg_ref, x_ref, mod_ref, kc_ref, vct_ref, kp_ref, km_ref, kn_ref,
         vpt_ref, vmt_ref, vnt_ref, sink_ref, wo_ref, o_ref, s_scr, p_scr, ot_scr) = refs
    else:
        q_ref, sg_ref, x_ref, mod_ref, kc_ref, vct_ref, sink_ref, wo_ref, o_ref, s_scr, p_scr, ot_scr = refs
    step = pl.program_id(1)
    nqb = q_ref.shape[1] // QBLK
    n_ctx = kc_ref.shape[1] // QBLK
    cols = GROUP * QBLK
    if window:
        kj = lax.broadcasted_iota(jnp.int32, (QBLK, cols), 0)
        qi = lax.broadcasted_iota(jnp.int32, (QBLK, cols), 1) & (QBLK - 1)
        after_diag = kj >= qi
        before_diag = kj <= qi
    ones_rows = jnp.where(lax.broadcasted_iota(jnp.int32, (16, QBLK), 0) == 0, 1.0, 0.0).astype(BF16)
    n_blk = n_ctx + (3 if window else 0)

    def window_blocks(qb, cs, kp, km, kn, lanes):
        def mid(j):
            sl = slice(j * QBLK, (j + 1) * QBLK)
            return km[0, cs, sl] if lanes else km[0, sl, cs]
        first = kp[0, cs, :] if lanes else kp[0][:, cs]
        last = kn[0, cs, :] if lanes else kn[0][:, cs]
        return [first if qb == 0 else mid(qb - 1), mid(qb), last if qb == nqb - 1 else mid(qb + 1)]

    def block_masks(qb):
        if not window:
            return [None] * n_ctx
        prev_ok = after_diag & (step > 0) if qb == 0 else after_diag
        next_ok = before_diag & (step < nb // nqb - 1) if qb == nqb - 1 else before_diag
        return [None] * n_ctx + [prev_ok, None, next_ok]

    def scores(qb, kvh):
        u = qb * N_KV_HEADS + kvh
        cs = slice(kvh * HEAD_DIM, (kvh + 1) * HEAD_DIM)
        heads = [kvh * GROUP + j for j in range(GROUP)]
        qq = q_ref[0, qb * QBLK:(qb + 1) * QBLK, :]
        q4 = jnp.concatenate([qq[:, h * HEAD_DIM:(h + 1) * HEAD_DIM] for h in heads], axis=0)
        sink_row = jnp.concatenate(
            [jnp.broadcast_to(sink_ref[0:1, h:h + 1], (1, QBLK)) for h in heads], axis=1) * LOG2E
        keys = [kc_ref[0, j * QBLK:(j + 1) * QBLK, cs].astype(BF16) for j in range(n_ctx)]
        if window:
            keys += window_blocks(qb, cs, kp_ref, km_ref, kn_ref, False)
        st_all = _dot_nt(jnp.concatenate(keys, axis=0), q4)
        macc = jnp.full((8, cols), NEG_INF, F32)
        for j, ok in enumerate(block_masks(qb)):
            s_blk = st_all[j * QBLK:(j + 1) * QBLK, :]
            if ok is not None:
                s_blk = jnp.where(ok, s_blk, NEG_INF)
            s_scr[u, j] = s_blk
            macc = jnp.maximum(macc, jnp.max(s_blk.reshape(QBLK // 8, 8, cols), axis=0))
        return jnp.maximum(jnp.max(macc, axis=0, keepdims=True), sink_row), sink_row

    def weighted_values(qb, kvh, m_row, sink_row):
        u = qb * N_KV_HEADS + kvh
        cs = slice(kvh * HEAD_DIM, (kvh + 1) * HEAD_DIM)
        for j in range(n_blk):
            p_scr[u, j * QBLK:(j + 1) * QBLK, :] = jnp.exp2(s_scr[u, j] - m_row).astype(BF16)
        vts = [vct_ref[0, cs, j * QBLK:(j + 1) * QBLK] for j in range(n_ctx)]
        if window:
            vts += window_blocks(qb, cs, vpt_ref, vmt_ref, vnt_ref, True)
        vt_ext = jnp.concatenate(
            [jnp.concatenate(vts, axis=1), jnp.tile(ones_rows, (1, n_blk))], axis=0)
        acc = _dot(vt_ext, p_scr[u])
        den = acc[HEAD_DIM:HEAD_DIM + 1, :] + jnp.exp2(sink_row - m_row)
        o_t = acc[0:HEAD_DIM, :] / den
        for j in range(GROUP):
            h = kvh * GROUP + j
            ot_scr[h * HEAD_DIM:(h + 1) * HEAD_DIM, qb * QBLK:(qb + 1) * QBLK] = o_t[:, j * QBLK:(j + 1) * QBLK]

    units = [(qb, kvh) for qb in range(nqb) for kvh in range(N_KV_HEADS)]
    stats = [scores(qb, kvh) for qb, kvh in units]
    for (qb, kvh), st in zip(units, stats):
        weighted_values(qb, kvh, *st)
    z = (ot_scr[...].T * sg_ref[0].astype(F32)).astype(BF16)
    y = _dot(z, wo_ref[...])
    o_ref[0] = x_ref[0] + mod_ref[0][2:3, :] * y


def _attn(q, sg, x, mod3, mod_row, kc, vct, k_lat, vt_lat, sink, wo_bf):
    bsz, t, d = x.shape
    dq = q.shape[-1]
    dkv = kc.shape[-1]
    p_len = kc.shape[1]
    nb = t // QBLK
    nqb = ATTN_QB
    rows = nqb * QBLK
    window = k_lat is not None
    tok = lambda b, i: (b, i, 0)
    in_specs = [pl.BlockSpec((1, rows, dq), tok),
                pl.BlockSpec((1, rows, dq), tok),
                pl.BlockSpec((1, rows, d), tok),
                pl.BlockSpec((1, 3, d), lambda b, i: (mod_row(b), 0, 0)),
                pl.BlockSpec((1, p_len, dkv), lambda b, i: (b, 0, 0)),
                pl.BlockSpec((1, dkv, p_len), lambda b, i: (b, 0, 0))]
    args = [q, sg, x, mod3, kc, vct]
    n_blocks = p_len // QBLK
    if window:
        prev = lambda i: jnp.maximum(i * nqb - 1, 0)
        nxt = lambda i: jnp.minimum((i + 1) * nqb, nb - 1)
        in_specs += [pl.BlockSpec((1, QBLK, dkv), lambda b, i: (b, prev(i), 0)),
                     pl.BlockSpec((1, rows, dkv), tok),
                     pl.BlockSpec((1, QBLK, dkv), lambda b, i: (b, nxt(i), 0)),
                     pl.BlockSpec((1, dkv, QBLK), lambda b, i: (b, 0, prev(i))),
                     pl.BlockSpec((1, dkv, rows), lambda b, i: (b, 0, i)),
                     pl.BlockSpec((1, dkv, QBLK), lambda b, i: (b, 0, nxt(i)))]
        args += [k_lat] * 3 + [vt_lat] * 3
        n_blocks += 3
    in_specs += [pl.BlockSpec((1, N_HEADS), lambda b, i: (0, 0)),
                 pl.BlockSpec(wo_bf.shape, lambda b, i: (0, 0))]
    args += [sink.reshape(1, N_HEADS), wo_bf]
    units = nqb * N_KV_HEADS
    return pl.pallas_call(
        functools.partial(_attn_kernel, window=window, nb=nb),
        out_shape=jax.ShapeDtypeStruct((bsz, t, d), F32),
        grid=(bsz, nb // nqb),
        in_specs=in_specs,
        out_specs=pl.BlockSpec((1, rows, d), tok),
        scratch_shapes=[pltpu.VMEM((units, n_blocks, QBLK, GROUP * QBLK), F32),
                        pltpu.VMEM((units, n_blocks * QBLK, GROUP * QBLK), BF16),
                        pltpu.VMEM((dq, rows), F32)],
        compiler_params=_cparams(("parallel", "parallel")),
        name="attn_window" if window else "attn_ctx",
    )(*args)


def _mlstm_in_kernel(x_ref, mod_ref, nw_ref, w_ref, wvt_ref, wgt_ref, bgt_ref,
                     q_ref, k_ref, vt_ref, og_ref, gc_ref, gr_ref):
    dm = q_ref.shape[-1]
    nh = M_HEADS
    L = MCHUNK
    hb = _prenorm(x_ref[0], nw_ref[...], mod_ref[0]).astype(BF16)

    gr = _dot_nt(wgt_ref[...], hb) + bgt_ref[...]
    n_chunks = x_ref.shape[1] // L
    ri = lax.broadcasted_iota(jnp.int32, (L, L), 0)
    ci = lax.broadcasted_iota(jnp.int32, (L, L), 1)
    lane = lax.broadcasted_iota(jnp.int32, (n_chunks * nh, L), 1)
    g_rows = []
    for dr in range(2):
        before = (ri <= ci) if dr == 0 else (ri >= ci)
        tri = jnp.where(before, 1.0, 0.0).astype(BF16)
        base = dr * 2 * nh
        lf = _log_sigmoid(gr[base + nh:base + 2 * nh, :]) * LOG2E
        gi = gr[base:base + nh, :] * LOG2E
        lf_st = jnp.concatenate([lf[:, c * L:(c + 1) * L] for c in range(n_chunks)], axis=0)
        b_st = sum(_dot(piece, tri) for piece in _split3(lf_st))
        g_st = jnp.concatenate([gi[:, c * L:(c + 1) * L] for c in range(n_chunks)], axis=0) - b_st
        run = g_st
        step = 1
        while step < L:
            if dr == 0:
                run = jnp.where(lane >= step, jnp.maximum(run, pltpu.roll(run, step, 1)), run)
            else:
                run = jnp.where(lane < L - step, jnp.maximum(run, pltpu.roll(run, L - step, 1)), run)
            step *= 2
        for cidx in range(n_chunks):
            rows = slice(cidx * L, (cidx + 1) * L)
            blk = slice(cidx * nh, (cidx + 1) * nh)
            b_last = jnp.sum(lf[:, rows], axis=1, keepdims=True)
            g_max = jnp.max(g_st[blk, :], axis=1, keepdims=True)
            g_rows.append(g_st[blk, :])
            gr_ref[0, dr, :, rows] = jnp.concatenate(
                [g_st[blk, :], b_st[blk, :], jnp.broadcast_to(b_last, (nh, L)),
                 jnp.broadcast_to(g_max, (nh, L)), run[blk, :]], axis=0)
    g_sq = jnp.concatenate(g_rows + [jnp.zeros((L - len(g_rows) * nh, L), F32)], axis=0).T
    for dr in range(2):
        for cidx in range(n_chunks):
            idx = dr * n_chunks + cidx
            gc_ref[0, dr, cidx * L:(cidx + 1) * L, :] = g_sq[:, idx * nh:(idx + 1) * nh]

    o = _dot(hb, w_ref[:, 3 * dm:4 * dm])
    g = _dot(hb, w_ref[:, 4 * dm:5 * dm])
    og_ref[0] = (jax.nn.sigmoid(o) * _silu(g)).astype(og_ref.dtype)
    q_ref[0] = _dot(hb, w_ref[:, 0:dm]).astype(q_ref.dtype)
    k_ref[0] = (_dot(hb, w_ref[:, dm:2 * dm]) * (M_HD ** -0.5)).astype(k_ref.dtype)
    vt_ref[0] = _dot_nt(wvt_ref[...], hb).astype(vt_ref.dtype)


def _mlstm_in(x, mod3, mod_row, norm_w, w_main_bf, wvt_bf, wgt_bf, b_gates):
    bsz, t, d = x.shape
    dm = M_HEADS * M_HD
    ng = 4 * M_HEADS
    tm = min(4 * ROW_TILE, t)
    tok = lambda b, i: (b, i, 0)
    const = lambda b, i: (0, 0)
    big = jax.ShapeDtypeStruct((bsz, t, dm), BF16)
    once = pl.Buffered(1)
    return pl.pallas_call(
        _mlstm_in_kernel,
        out_shape=(big, big, jax.ShapeDtypeStruct((bsz, dm, t), BF16), big,
                   jax.ShapeDtypeStruct((bsz, 2, t, M_HEADS), F32),
                   jax.ShapeDtypeStruct((bsz, 2, 5 * M_HEADS, t), F32)),
        grid=(bsz, t // tm),
        in_specs=[pl.BlockSpec((1, tm, d), tok),
                  pl.BlockSpec((1, 3, d), lambda b, i: (mod_row(b), 0, 0)),
                  pl.BlockSpec((1, d), const),
                  pl.BlockSpec(w_main_bf.shape, const, pipeline_mode=once),
                  pl.BlockSpec(wvt_bf.shape, const, pipeline_mode=once),
                  pl.BlockSpec((ng, d), const),
                  pl.BlockSpec((ng, 1), const)],
        out_specs=(pl.BlockSpec((1, tm, dm), tok), pl.BlockSpec((1, tm, dm), tok),
                   pl.BlockSpec((1, dm, tm), lambda b, i: (b, 0, i)), pl.BlockSpec((1, tm, dm), tok),
                   pl.BlockSpec((1, 2, tm, M_HEADS), lambda b, i: (b, 0, i, 0)),
                   pl.BlockSpec((1, 2, 5 * M_HEADS, tm), lambda b, i: (b, 0, 0, i))),
        compiler_params=_cparams(("parallel", "parallel")),
        name="mlstm_in",
    )(x, mod3, norm_w.reshape(1, d), w_main_bf, wvt_bf, wgt_bf, b_gates.reshape(ng, 1))


def _mlstm_scan_kernel(*refs, has_init, write_state, nc):
    refs = list(refs)
    q_ref, k_ref, vt_ref, gc_ref, gr_ref, og_ref, x_ref, mod_ref, wo_ref, fw_ref = refs[:10]
    pos = 10
    if has_init:
        c0_ref, n0_ref, m0_ref = refs[pos:pos + 3]
        pos += 3
    y_ref = refs[pos]
    pos += 1
    if write_state:
        cout_ref, nout_ref, mout_ref = refs[pos:pos + 3]
        pos += 3
    ct_scr, mscr, hcur, hfwd = refs[pos:pos + 4]

    drn = pl.program_id(1)
    c = pl.program_id(2)
    L = q_ref.shape[1]
    nh = M_HEADS
    pad = ct_scr.shape[1] - M_HD

    @pl.when(c == 0)
    def _init():
        if has_init:
            for h in range(nh):
                ct_scr[h, 0:M_HD, :] = c0_ref[0, 0, h].T
                ct_scr[h, M_HD:M_HD + pad, :] = jnp.concatenate(
                    [n0_ref[0, 0, h:h + 1, :], jnp.zeros((pad - 1, M_HD), F32)], axis=0)
            mscr[...] = m0_ref[0, 0] * LOG2E
        else:
            ct_scr[...] = jnp.zeros(ct_scr.shape, F32)
            mscr[...] = jnp.zeros(mscr.shape, F32)

    si = lax.broadcasted_iota(jnp.int32, (L, L), 0)
    li = lax.broadcasted_iota(jnp.int32, (L, L), 1)
    seen_t = (si - li) * (1 - 2 * drn) <= 0

    gcb = gc_ref[0, 0]
    grb = gr_ref[0, 0]
    q = q_ref[0]
    k = k_ref[0]
    vt = vt_ref[0]
    ones_rows = jnp.where(lax.broadcasted_iota(jnp.int32, (pad, L), 0) == 0, 1.0, 0.0).astype(BF16)

    def head_scores(h):
        hs = slice(h * M_HD, (h + 1) * M_HD)
        m_prev = mscr[h:h + 1, 0:1]
        ct = ct_scr[h]
        m_row = jnp.maximum(grb[4 * nh + h:4 * nh + h + 1, :], m_prev)
        w_t = jnp.exp2(jnp.where(seen_t, gcb[:, h:h + 1], NEG_INF) - m_row)
        r1 = _dot_nt(jnp.concatenate([k[:, hs], ct.astype(BF16)], axis=0), q[:, hs])
        s_t = (r1[0:L, :] * w_t).astype(BF16)
        return m_prev, ct, m_row, s_t, r1[L:, :]

    def head_finish(h, m_prev, ct, m_row, s_t, inter):
        hs = slice(h * M_HD, (h + 1) * M_HD)
        vext = jnp.concatenate([vt[hs, :], ones_rows], axis=0)
        g_r = grb[h:h + 1, :]
        b_r = grb[nh + h:nh + h + 1, :]
        b_last = grb[2 * nh + h:2 * nh + h + 1, 0:1]
        g_max = grb[3 * nh + h:3 * nh + h + 1, 0:1]
        w0 = jnp.exp2(m_prev - m_row)
        tot = _dot(vext, s_t) + w0 * inter
        den = tot[M_HD:M_HD + 1, :]
        floor = jnp.exp2(-(b_r + m_row))
        hcur[hs, :] = tot[0:M_HD, :] / jnp.maximum(jnp.abs(den), floor)

        m_last = jnp.maximum(g_max, m_prev)
        wk = jnp.exp2(g_r - m_last)
        decay = jnp.exp2(m_prev - m_last)
        vw = (vext.astype(F32) * wk).astype(BF16)
        ct_scr[h] = decay * ct + _dot(vw, k[:, hs])
        mscr[h:h + 1, :] = jnp.broadcast_to(b_last + m_last, (1, LANES))

    pending = [head_scores(h) for h in range(min(SCAN_AHEAD, nh))]
    for h in range(nh):
        if h + SCAN_AHEAD < nh:
            pending.append(head_scores(h + SCAN_AHEAD))
        head_finish(h, *pending.pop(0))

    @pl.when(drn == 0)
    def _park():
        hfwd[c] = hcur[...]

    @pl.when(drn == 1)
    def _emit():
        hm = (hcur[...] + hfwd[nc - 1 - c]).T * og_ref[0].astype(F32)
        y = _dot(hm.astype(BF16), wo_ref[...])
        x2 = x_ref[0] + mod_ref[0][2:3, :] * y
        ms = jnp.mean(x2 * x2, axis=-1, keepdims=True)
        y_ref[0] = x2 * lax.rsqrt(ms + EPS) * fw_ref[...]

    if write_state:
        @pl.when(c == nc - 1)
        def _final():
            for h in range(nh):
                cfin = ct_scr[h]
                cout_ref[0, 0, h] = cfin[0:M_HD, :].T
                nout_ref[0, 0, h:h + 1, :] = cfin[M_HD:M_HD + 1, :]
            mout_ref[0, 0] = mscr[...] * LN2


def _mlstm_scan(q, k, vt, gc, gr, og, x, mod3, mod_row, wo_bf, final_w, init, write_state):
    bsz, t, dm = q.shape
    d_model = x.shape[-1]
    L = MCHUNK
    nc = t // L
    chunk = lambda b, d, c: c + d * (nc - 1 - 2 * c)
    tok = lambda b, d, c: (b, chunk(b, d, c), 0)
    tail = lambda b, d, c: (b, nc - 1 - d * c, 0)
    const = lambda b, d, c: (0, 0)
    in_specs = [pl.BlockSpec((1, L, dm), tok)] * 2 + [
        pl.BlockSpec((1, dm, L), lambda b, d, c: (b, 0, chunk(b, d, c))),
        pl.BlockSpec((1, 1, L, gc.shape[-1]), lambda b, d, c: (b, d, chunk(b, d, c), 0)),
        pl.BlockSpec((1, 1, gr.shape[2], L), lambda b, d, c: (b, d, 0, chunk(b, d, c))),
        pl.BlockSpec((1, L, dm), tail),
        pl.BlockSpec((1, L, d_model), tail),
        pl.BlockSpec((1, 3, d_model), lambda b, d, c: (mod_row(b), 0, 0)),
        pl.BlockSpec(wo_bf.shape, const, pipeline_mode=pl.Buffered(1)),
        pl.BlockSpec((1, d_model), const)]
    args = [q, k, vt, gc, gr, og, x, mod3, wo_bf, final_w.reshape(1, d_model)]
    st = lambda b, d, c: (b, d, 0, 0)
    st5 = lambda b, d, c: (b, d, 0, 0, 0)
    if init is not None:
        c0, n0, m0 = init
        in_specs += [pl.BlockSpec((1, 1, M_HEADS, M_HD, M_HD), st5),
                     pl.BlockSpec((1, 1, M_HEADS, M_HD), st),
                     pl.BlockSpec((1, 1, M_HEADS, LANES), st)]
        args += [c0, n0, jnp.broadcast_to(m0[..., None], m0.shape + (LANES,))]
    out_shape = [jax.ShapeDtypeStruct((bsz, t, d_model), F32)]
    out_specs = [pl.BlockSpec((1, L, d_model), tail)]
    if write_state:
        out_shape += [jax.ShapeDtypeStruct((bsz, 2, M_HEADS, M_HD, M_HD), F32),
                      jax.ShapeDtypeStruct((bsz, 2, M_HEADS, M_HD), F32),
                      jax.ShapeDtypeStruct((bsz, 2, M_HEADS, LANES), F32)]
        out_specs += [pl.BlockSpec((1, 1, M_HEADS, M_HD, M_HD), st5),
                      pl.BlockSpec((1, 1, M_HEADS, M_HD), st),
                      pl.BlockSpec((1, 1, M_HEADS, LANES), st)]
    return pl.pallas_call(
        functools.partial(_mlstm_scan_kernel, has_init=init is not None,
                          write_state=write_state, nc=nc),
        out_shape=tuple(out_shape),
        grid=(bsz, 2, nc),
        in_specs=in_specs,
        out_specs=tuple(out_specs),
        scratch_shapes=[pltpu.VMEM((M_HEADS, M_HD + 16, M_HD), F32),
                        pltpu.VMEM((M_HEADS, LANES), F32),
                        pltpu.VMEM((dm, L), F32),
                        pltpu.VMEM((nc, dm, L), F32)],
        compiler_params=_cparams(("parallel", "arbitrary", "arbitrary")),
        name="mlstm_scan",
    )(*args)


def _rope_tables(t):
    nf = HEAD_DIM // 4
    pos = jnp.arange(t)
    row = (pos // GRID_W).astype(F32)
    col = (pos % GRID_W).astype(F32)
    inv = ROPE_BASE ** (-jnp.arange(nf, dtype=F32) / nf)
    ar = row[:, None] * inv[None, :]
    ac = col[:, None] * inv[None, :]
    cos = jnp.concatenate([jnp.cos(ar), jnp.cos(ar), jnp.cos(ac), jnp.cos(ac)], axis=1)
    sin = jnp.concatenate([-jnp.sin(ar), jnp.sin(ar), -jnp.sin(ac), jnp.sin(ac)], axis=1)
    reps = LANES // HEAD_DIM
    return jnp.tile(cos, (1, reps)), jnp.tile(sin, (1, reps))


def kernel(x_prompt, x_sample, cache_k, cache_v, state_C, state_n, state_m, c, c_ctx,
           attn_norm_w, attn_ada_w, attn_ada_b, attn_w_in, attn_sink, attn_w_out,
           mlstm_norm_w, mlstm_ada_w, mlstm_ada_b, mlstm_w_in, mlstm_b_gates, mlstm_w_out,
           final_norm_w):
    assert attn_w_in.shape[0] == 1 and mlstm_w_in.shape[0] == 1, "one layer of each mixer"
    bsz, seq, d = x_prompt.shape
    dbsz, dseq, _ = x_sample.shape
    dkv = N_KV_HEADS * HEAD_DIM
    dm = M_HEADS * M_HD

    n_cond = 1 + dbsz
    cond = jnp.concatenate([c_ctx[None, :], c, jnp.zeros((-n_cond % 8, d), F32)], axis=0)
    attn_mod = _ada(cond, attn_ada_w[0], attn_ada_b[0]).reshape(-1, 3, d)
    mlstm_mod = _ada(cond, mlstm_ada_w[0], mlstm_ada_b[0]).reshape(-1, 3, d)
    ctx_row = lambda b: 0
    lat_row = lambda b: b + 1

    attn_w_in_bf = attn_w_in[0].astype(BF16)
    attn_w_out_bf = attn_w_out[0].astype(BF16)
    attn_wvt_bf = attn_w_in_bf[:, 2 * N_HEADS * HEAD_DIM + dkv:].T
    w_main_bf = mlstm_w_in[0].astype(BF16)
    mlstm_wvt_bf = w_main_bf[:, 2 * dm:3 * dm].T
    wgt_bf = w_main_bf[:, 5 * dm:].T
    mlstm_w_out_bf = mlstm_w_out[0].astype(BF16)

    def mlstm_layer(x, mod_row, init, write_state):
        q, k, vt, og, gc, gr = _mlstm_in(x, mlstm_mod, mod_row, mlstm_norm_w[0], w_main_bf,
                                         mlstm_wvt_bf, wgt_bf, mlstm_b_gates[0])
        outs = _mlstm_scan(q, k, vt, gc, gr, og, x, mlstm_mod, mod_row, mlstm_w_out_bf, final_norm_w,
                           init, write_state)
        return outs[0], outs[1:]

    q, sg, k_ctx, vt_ctx, v_ctx = _attn_in(x_prompt, attn_mod, ctx_row, attn_norm_w[0], attn_w_in_bf,
                                           attn_wvt_bf, None, F32, True)
    x1 = _attn(q, sg, x_prompt, attn_mod, ctx_row, k_ctx, vt_ctx, None, None, attn_sink[0], attn_w_out_bf)
    y_prompt, (c_fin, n_fin, m_fin) = mlstm_layer(x1, ctx_row, None, True)

    q, sg, k_lat, vt_lat = _attn_in(x_sample, attn_mod, lat_row, attn_norm_w[0], attn_w_in_bf,
                                    attn_wvt_bf, _rope_tables(dseq), BF16, False)
    kc = cache_k[:, 0].reshape(dbsz, -1, dkv).astype(BF16)
    vct = jnp.swapaxes(cache_v[:, 0].reshape(dbsz, -1, dkv), 1, 2).astype(BF16)
    x1 = _attn(q, sg, x_sample, attn_mod, lat_row, kc, vct, k_lat, vt_lat, attn_sink[0], attn_w_out_bf)
    y_sample, _ = mlstm_layer(x1, lat_row, (state_C[:, 0], state_n[:, 0], state_m[:, 0]), False)

    new_cache_k = k_ctx.reshape(bsz, 1, seq, N_KV_HEADS, HEAD_DIM)
    new_cache_v = v_ctx.reshape(bsz, 1, seq, N_KV_HEADS, HEAD_DIM)
    return (y_prompt, y_sample, new_cache_k, new_cache_v,
            c_fin[:, None], n_fin[:, None], m_fin[:, None, :, :, 0])
```

```python
import functools

import jax
import jax.numpy as jnp
from jax import lax
from jax.experimental import pallas as pl
from jax.experimental.pallas import tpu as pltpu

F32 = jnp.float32
BF16 = jnp.bfloat16

HEAD_DIM = 64
N_KV_HEADS = 4
GROUP = 4
N_HEADS = N_KV_HEADS * GROUP
QBLK = 128
GRID_W = 64
ROPE_BASE = 10000.0
M_HEADS = 8
M_HD = 128
EPS = 1e-6

LANES = 128
VMEM_LIMIT = 48 * 1024 * 1024

MCHUNK = 256
ATTN_QB = 2
SCAN_AHEAD = 4
ROW_TILE = 256

NEG_INF = float("-inf")
LOG2E = 1.4426950408889634
LN2 = 0.6931471805599453


def _cparams(sem):
    return pltpu.CompilerParams(dimension_semantics=sem, vmem_limit_bytes=VMEM_LIMIT)


def _silu(x):
    return x * jax.nn.sigmoid(x)


def _log_sigmoid(x):
    return jnp.minimum(x, 0.0) - jnp.log1p(jnp.exp(-jnp.abs(x)))


def _dot(a, b):
    return jnp.dot(a, b, preferred_element_type=F32)


def _dot_nt(a, b):
    return lax.dot_general(a, b, (((1,), (1,)), ((), ())), preferred_element_type=F32)


def _dot_tn(a, b):
    return lax.dot_general(a, b, (((0,), (0,)), ((), ())), preferred_element_type=F32)


def _split3(x):
    hi = x.astype(BF16)
    r = x - hi.astype(F32)
    mid = r.astype(BF16)
    lo = (r - mid.astype(F32)).astype(BF16)
    return hi, mid, lo


def _prenorm(x, norm_w, mod):
    ms = jnp.mean(x * x, axis=-1, keepdims=True)
    y = x * lax.rsqrt(ms + EPS) * norm_w
    return y * (1.0 + mod[1:2, :]) + mod[0:1, :]


def _ada_kernel(cond_ref, w_ref, b_ref, o_ref):
    a = _silu(cond_ref[...]).astype(BF16)
    o_ref[...] = _dot(a, w_ref[...].astype(BF16)) + b_ref[...]


def _ada(cond8, w, b):
    d, n = w.shape
    tn = 512
    return pl.pallas_call(
        _ada_kernel,
        out_shape=jax.ShapeDtypeStruct((cond8.shape[0], n), F32),
        grid=(n // tn,),
        in_specs=[pl.BlockSpec(cond8.shape, lambda j: (0, 0)),
                  pl.BlockSpec((d, tn), lambda j: (0, j)),
                  pl.BlockSpec((1, tn), lambda j: (0, j))],
        out_specs=pl.BlockSpec((cond8.shape[0], tn), lambda j: (0, j)),
        compiler_params=_cparams(("parallel",)),
        name="ada_mod",
    )(cond8, w, b.reshape(1, n))


def _rope(x, cos, sin, lane):
    first = (lane & 31) < 16
    outs = []
    for c in range(x.shape[1] // LANES):
        xc = x[:, c * LANES:(c + 1) * LANES]
        sw = jnp.where(first, pltpu.roll(xc, LANES - 16, 1), pltpu.roll(xc, 16, 1))
        outs.append(xc * cos + sw * sin)
    return jnp.concatenate(outs, axis=1)


def _attn_in_kernel(*refs, rope, emit_v):
    refs = list(refs)
    x_ref, mod_ref, nw_ref, w_ref, wvt_ref = refs[:5]
    pos = 5
    if rope:
        cos_ref, sin_ref = refs[pos:pos + 2]
        pos += 2
    q_ref, sg_ref, k_ref, vt_ref = refs[pos:pos + 4]
    dq = q_ref.shape[-1]
    dkv = k_ref.shape[-1]
    hb = _prenorm(x_ref[0], nw_ref[...], mod_ref[0]).astype(w_ref.dtype)
    q = _dot(hb, w_ref[:, 0:dq])
    g = _dot(hb, w_ref[:, dq:2 * dq])
    k = _dot(hb, w_ref[:, 2 * dq:2 * dq + dkv])
    if rope:
        cos = cos_ref[...]
        sin = sin_ref[...]
        lane = lax.broadcasted_iota(jnp.int32, cos.shape, 1)
        q = _rope(q, cos, sin, lane)
        k = _rope(k, cos, sin, lane)
    q_ref[0] = (q * (HEAD_DIM ** -0.5 * LOG2E)).astype(q_ref.dtype)
    sg_ref[0] = _silu(g).astype(sg_ref.dtype)
    k_ref[0] = k.astype(k_ref.dtype)
    vt_ref[0] = _dot_nt(wvt_ref[...], hb).astype(vt_ref.dtype)
    if emit_v:
        v_ref = refs[pos + 4]
        v_ref[0] = _dot(hb, w_ref[:, 2 * dq + dkv:2 * dq + 2 * dkv]).astype(v_ref.dtype)


def _attn_in(x, mod3, mod_row, norm_w, w_bf, wvt_bf, rope_tabs, k_dtype, emit_v):
    bsz, t, d = x.shape
    dq = N_HEADS * HEAD_DIM
    dkv = N_KV_HEADS * HEAD_DIM
    tm = min(4 * ROW_TILE, t)
    rope = rope_tabs is not None
    tok = lambda b, i: (b, i, 0)
    const = lambda b, i: (0, 0)
    in_specs = [pl.BlockSpec((1, tm, d), tok),
                pl.BlockSpec((1, 3, d), lambda b, i: (mod_row(b), 0, 0)),
                pl.BlockSpec((1, d), const),
                pl.BlockSpec(w_bf.shape, const),
                pl.BlockSpec(wvt_bf.shape, const)]
    args = [x, mod3, norm_w.reshape(1, d), w_bf, wvt_bf]
    if rope:
        in_specs += [pl.BlockSpec((tm, LANES), lambda b, i: (i, 0))] * 2
        args += list(rope_tabs)
    out_shape = [jax.ShapeDtypeStruct((bsz, t, dq), BF16),
                 jax.ShapeDtypeStruct((bsz, t, dq), BF16),
                 jax.ShapeDtypeStruct((bsz, t, dkv), k_dtype),
                 jax.ShapeDtypeStruct((bsz, dkv, t), BF16)]
    out_specs = [pl.BlockSpec((1, tm, dq), tok), pl.BlockSpec((1, tm, dq), tok),
                 pl.BlockSpec((1, tm, dkv), tok),
                 pl.BlockSpec((1, dkv, tm), lambda b, i: (b, 0, i))]
    if emit_v:
        out_shape.append(jax.ShapeDtypeStruct((bsz, t, dkv), F32))
        out_specs.append(pl.BlockSpec((1, tm, dkv), tok))
    return pl.pallas_call(
        functools.partial(_attn_in_kernel, rope=rope, emit_v=emit_v),
        out_shape=tuple(out_shape),
        grid=(bsz, t // tm),
        in_specs=in_specs,
        out_specs=tuple(out_specs),
        compiler_params=_cparams(("parallel", "parallel")),
        name="attn_in_rope" if rope else "attn_in",
    )(*args)


def _attn_kernel(*refs, window, nb):
    if window:
        (q_ref, sg_ref, x_ref, mod_ref, kc_ref, vct_ref, kp_ref, km_ref, kn_ref,
         vpt_ref, vmt_ref, vnt_ref, sink_ref, wo_ref, o_ref, s_scr, p_scr, ot_scr) = refs
    else:
        q_ref, sg_ref, x_ref, mod_ref, kc_ref, vct_ref, sink_ref, wo_ref, o_ref, s_scr, p_scr, ot_scr = refs
    step = pl.program_id(1)
    nqb = q_ref.shape[1] // QBLK
    n_ctx = kc_ref.shape[1] // QBLK
    cols = GROUP * QBLK
    if window:
        kj = lax.broadcasted_iota(jnp.int32, (QBLK, cols), 0)
        qi = lax.broadcasted_iota(jnp.int32, (QBLK, cols), 1) & (QBLK - 1)
        after_diag = kj >= qi
        before_diag = kj <= qi
    ones_rows = jnp.where(lax.broadcasted_iota(jnp.int32, (16, QBLK), 0) == 0, 1.0, 0.0).astype(BF16)
    n_blk = n_ctx + (3 if window else 0)

    def window_blocks(qb, cs, kp, km, kn, lanes):
        def mid(j):
            sl = slice(j * QBLK, (j + 1) * QBLK)
            return km[0, cs, sl] if lanes else km[0, sl, cs]
        first = kp[0, cs, :] if lanes else kp[0][:, cs]
        last = kn[0, cs, :] if lanes else kn[0][:, cs]
        return [first if qb == 0 else mid(qb - 1), mid(qb), last if qb == nqb - 1 else mid(qb + 1)]

    def block_masks(qb):
        if not window:
            return [None] * n_ctx
        prev_ok = after_diag & (step > 0) if qb == 0 else after_diag
        next_ok = before_diag & (step < nb // nqb - 1) if qb == nqb - 1 else before_diag
        return [None] * n_ctx + [prev_ok, None, next_ok]

    def scores(qb, kvh):
        u = qb * N_KV_HEADS + kvh
        cs = slice(kvh * HEAD_DIM, (kvh + 1) * HEAD_DIM)
        heads = [kvh * GROUP + j for j in range(GROUP)]
        qq = q_ref[0, qb * QBLK:(qb + 1) * QBLK, :]
        q4 = jnp.concatenate([qq[:, h * HEAD_DIM:(h + 1) * HEAD_DIM] for h in heads], axis=0)
        sink_row = jnp.concatenate(
            [jnp.broadcast_to(sink_ref[0:1, h:h + 1], (1, QBLK)) for h in heads], axis=1) * LOG2E
        keys = [kc_ref[0, j * QBLK:(j + 1) * QBLK, cs].astype(BF16) for j in range(n_ctx)]
        if window:
            keys += window_blocks(qb, cs, kp_ref, km_ref, kn_ref, False)
        st_all = _dot_nt(jnp.concatenate(keys, axis=0), q4)
        macc = jnp.full((8, cols), NEG_INF, F32)
        for j, ok in enumerate(block_masks(qb)):
            s_blk = st_all[j * QBLK:(j + 1) * QBLK, :]
            if ok is not None:
                s_blk = jnp.where(ok, s_blk, NEG_INF)
            s_scr[u, j] = s_blk
            macc = jnp.maximum(macc, jnp.max(s_blk.reshape(QBLK // 8, 8, cols), axis=0))
        return jnp.maximum(jnp.max(macc, axis=0, keepdims=True), sink_row), sink_row

    def weighted_values(qb, kvh, m_row, sink_row):
        u = qb * N_KV_HEADS + kvh
        cs = slice(kvh * HEAD_DIM, (kvh + 1) * HEAD_DIM)
        for j in range(n_blk):
            p_scr[u, j * QBLK:(j + 1) * QBLK, :] = jnp.exp2(s_scr[u, j] - m_row).astype(BF16)
        vts = [vct_ref[0, cs, j * QBLK:(j + 1) * QBLK] for j in range(n_ctx)]
        if window:
            vts += window_blocks(qb, cs, vpt_ref, vmt_ref, vnt_ref, True)
        vt_ext = jnp.concatenate(
            [jnp.concatenate(vts, axis=1), jnp.tile(ones_rows, (1, n_blk))], axis=0)
        acc = _dot(vt_ext, p_scr[u])
        den = acc[HEAD_DIM:HEAD_DIM + 1, :] + jnp.exp2(sink_row - m_row)
        o_t = acc[0:HEAD_DIM, :] / den
        for j in range(GROUP):
            h = kvh * GROUP + j
            ot_scr[h * HEAD_DIM:(h + 1) * HEAD_DIM, qb * QBLK:(qb + 1) * QBLK] = o_t[:, j * QBLK:(j + 1) * QBLK]

    units = [(qb, kvh) for qb in range(nqb) for kvh in range(N_KV_HEADS)]
    stats = [scores(qb, kvh) for qb, kvh in units]
    for (qb, kvh), st in zip(units, stats):
        weighted_values(qb, kvh, *st)
    z = (ot_scr[...].T * sg_ref[0].astype(F32)).astype(wo_ref.dtype)
    y = _dot(z, wo_ref[...])
    o_ref[0] = x_ref[0] + mod_ref[0][2:3, :] * y


def _attn(q, sg, x, mod3, mod_row, kc, vct, k_lat, vt_lat, sink, wo_bf):
    bsz, t, d = x.shape
    dq = q.shape[-1]
    dkv = kc.shape[-1]
    p_len = kc.shape[1]
    nb = t // QBLK
    nqb = ATTN_QB
    rows = nqb * QBLK
    window = k_lat is not None
    tok = lambda b, i: (b, i, 0)
    in_specs = [pl.BlockSpec((1, rows, dq), tok),
                pl.BlockSpec((1, rows, dq), tok),
                pl.BlockSpec((1, rows, d), tok),
                pl.BlockSpec((1, 3, d), lambda b, i: (mod_row(b), 0, 0)),
                pl.BlockSpec((1, p_len, dkv), lambda b, i: (b, 0, 0)),
                pl.BlockSpec((1, dkv, p_len), lambda b, i: (b, 0, 0))]
    args = [q, sg, x, mod3, kc, vct]
    n_blocks = p_len // QBLK
    if window:
        prev = lambda i: jnp.maximum(i * nqb - 1, 0)
        nxt = lambda i: jnp.minimum((i + 1) * nqb, nb - 1)
        in_specs += [pl.BlockSpec((1, QBLK, dkv), lambda b, i: (b, prev(i), 0)),
                     pl.BlockSpec((1, rows, dkv), tok),
                     pl.BlockSpec((1, QBLK, dkv), lambda b, i: (b, nxt(i), 0)),
                     pl.BlockSpec((1, dkv, QBLK), lambda b, i: (b, 0, prev(i))),
                     pl.BlockSpec((1, dkv, rows), lambda b, i: (b, 0, i)),
                     pl.BlockSpec((1, dkv, QBLK), lambda b, i: (b, 0, nxt(i)))]
        args += [k_lat] * 3 + [vt_lat] * 3
        n_blocks += 3
    in_specs += [pl.BlockSpec((1, N_HEADS), lambda b, i: (0, 0)),
                 pl.BlockSpec(wo_bf.shape, lambda b, i: (0, 0))]
    args += [sink.reshape(1, N_HEADS), wo_bf]
    units = nqb * N_KV_HEADS
    return pl.pallas_call(
        functools.partial(_attn_kernel, window=window, nb=nb),
        out_shape=jax.ShapeDtypeStruct((bsz, t, d), F32),
        grid=(bsz, nb // nqb),
        in_specs=in_specs,
        out_specs=pl.BlockSpec((1, rows, d), tok),
        scratch_shapes=[pltpu.VMEM((units, n_blocks, QBLK, GROUP * QBLK), F32),
                        pltpu.VMEM((units, n_blocks * QBLK, GROUP * QBLK), BF16),
                        pltpu.VMEM((dq, rows), F32)],
        compiler_params=_cparams(("parallel", "parallel")),
        name="attn_window" if window else "attn_ctx",
    )(*args)


def _mlstm_in_kernel(x_ref, mod_ref, nw_ref, w_ref, wvt_ref, wgt_ref, bgt_ref,
                     q_ref, k_ref, vt_ref, og_ref, gc_ref, gr_ref):
    dm = q_ref.shape[-1]
    nh = M_HEADS
    L = MCHUNK
    hb = _prenorm(x_ref[0], nw_ref[...], mod_ref[0]).astype(w_ref.dtype)

    gr = _dot_nt(wgt_ref[...], hb) + bgt_ref[...]
    n_chunks = x_ref.shape[1] // L
    ri = lax.broadcasted_iota(jnp.int32, (L, L), 0)
    ci = lax.broadcasted_iota(jnp.int32, (L, L), 1)
    lane = lax.broadcasted_iota(jnp.int32, (n_chunks * nh, L), 1)
    g_rows = []
    for dr in range(2):
        before = (ri <= ci) if dr == 0 else (ri >= ci)
        tri = jnp.where(before, 1.0, 0.0).astype(BF16)
        base = dr * 2 * nh
        lf = _log_sigmoid(gr[base + nh:base + 2 * nh, :]) * LOG2E
        gi = gr[base:base + nh, :] * LOG2E
        lf_st = jnp.concatenate([lf[:, c * L:(c + 1) * L] for c in range(n_chunks)], axis=0)
        b_st = sum(_dot(piece, tri) for piece in _split3(lf_st))
        g_st = jnp.concatenate([gi[:, c * L:(c + 1) * L] for c in range(n_chunks)], axis=0) - b_st
        run = g_st
        step = 1
        while step < L:
            if dr == 0:
                run = jnp.where(lane >= step, jnp.maximum(run, pltpu.roll(run, step, 1)), run)
            else:
                run = jnp.where(lane < L - step, jnp.maximum(run, pltpu.roll(run, L - step, 1)), run)
            step *= 2
        for cidx in range(n_chunks):
            rows = slice(cidx * L, (cidx + 1) * L)
            blk = slice(cidx * nh, (cidx + 1) * nh)
            b_last = jnp.sum(lf[:, rows], axis=1, keepdims=True)
            g_max = jnp.max(g_st[blk, :], axis=1, keepdims=True)
            g_rows.append(g_st[blk, :])
            gr_ref[0, dr, :, rows] = jnp.concatenate(
                [g_st[blk, :], b_st[blk, :], jnp.broadcast_to(b_last, (nh, L)),
                 jnp.broadcast_to(g_max, (nh, L)), run[blk, :]], axis=0)
    g_sq = jnp.concatenate(g_rows + [jnp.zeros((L - len(g_rows) * nh, L), F32)], axis=0).T
    for dr in range(2):
        for cidx in range(n_chunks):
            idx = dr * n_chunks + cidx
            gc_ref[0, dr, cidx * L:(cidx + 1) * L, :] = g_sq[:, idx * nh:(idx + 1) * nh]

    o = _dot(hb, w_ref[:, 3 * dm:4 * dm])
    g = _dot(hb, w_ref[:, 4 * dm:5 * dm])
    og_ref[0] = (jax.nn.sigmoid(o) * _silu(g)).astype(og_ref.dtype)
    q_ref[0] = _dot(hb, w_ref[:, 0:dm]).astype(q_ref.dtype)
    k_ref[0] = (_dot(hb, w_ref[:, dm:2 * dm]) * (M_HD ** -0.5)).astype(k_ref.dtype)
    vt_ref[0] = _dot_nt(wvt_ref[...], hb).astype(vt_ref.dtype)


def _mlstm_in(x, mod3, mod_row, norm_w, w_main_bf, wvt_bf, wgt_bf, b_gates):
    bsz, t, d = x.shape
    dm = M_HEADS * M_HD
    ng = 4 * M_HEADS
    tm = min(2 * ROW_TILE, t)
    tok = lambda b, i: (b, i, 0)
    const = lambda b, i: (0, 0)
    big = jax.ShapeDtypeStruct((bsz, t, dm), BF16)
    once = pl.Buffered(1)
    return pl.pallas_call(
        _mlstm_in_kernel,
        out_shape=(big, big, jax.ShapeDtypeStruct((bsz, dm, t), BF16), big,
                   jax.ShapeDtypeStruct((bsz, 2, t, M_HEADS), F32),
                   jax.ShapeDtypeStruct((bsz, 2, 5 * M_HEADS, t), F32)),
        grid=(bsz, t // tm),
        in_specs=[pl.BlockSpec((1, tm, d), tok),
                  pl.BlockSpec((1, 3, d), lambda b, i: (mod_row(b), 0, 0)),
                  pl.BlockSpec((1, d), const),
                  pl.BlockSpec(w_main_bf.shape, const, pipeline_mode=once),
                  pl.BlockSpec(wvt_bf.shape, const, pipeline_mode=once),
                  pl.BlockSpec((ng, d), const),
                  pl.BlockSpec((ng, 1), const)],
        out_specs=(pl.BlockSpec((1, tm, dm), tok), pl.BlockSpec((1, tm, dm), tok),
                   pl.BlockSpec((1, dm, tm), lambda b, i: (b, 0, i)), pl.BlockSpec((1, tm, dm), tok),
                   pl.BlockSpec((1, 2, tm, M_HEADS), lambda b, i: (b, 0, i, 0)),
                   pl.BlockSpec((1, 2, 5 * M_HEADS, tm), lambda b, i: (b, 0, 0, i))),
        compiler_params=_cparams(("parallel", "parallel")),
        name="mlstm_in",
    )(x, mod3, norm_w.reshape(1, d), w_main_bf, wvt_bf, wgt_bf, b_gates.reshape(ng, 1))


def _mlstm_scan_kernel(*refs, has_init, write_state, nc):
    refs = list(refs)
    q_ref, k_ref, vt_ref, gc_ref, gr_ref, og_ref, x_ref, mod_ref, wo_ref, fw_ref = refs[:10]
    pos = 10
    if has_init:
        c0_ref, n0_ref, m0_ref = refs[pos:pos + 3]
        pos += 3
    y_ref = refs[pos]
    pos += 1
    if write_state:
        cout_ref, nout_ref, mout_ref = refs[pos:pos + 3]
        pos += 3
    ct_scr, mscr, hcur, hfwd = refs[pos:pos + 4]

    drn = pl.program_id(1)
    c = pl.program_id(2)
    L = q_ref.shape[1]
    nh = M_HEADS
    pad = ct_scr.shape[1] - M_HD

    @pl.when(c == 0)
    def _init():
        if has_init:
            for h in range(nh):
                ct_scr[h, 0:M_HD, :] = c0_ref[0, 0, h].T
                ct_scr[h, M_HD:M_HD + pad, :] = jnp.concatenate(
                    [n0_ref[0, 0, h:h + 1, :], jnp.zeros((pad - 1, M_HD), F32)], axis=0)
            mscr[...] = m0_ref[0, 0] * LOG2E
        else:
            ct_scr[...] = jnp.zeros(ct_scr.shape, F32)
            mscr[...] = jnp.zeros(mscr.shape, F32)

    si = lax.broadcasted_iota(jnp.int32, (L, L), 0)
    li = lax.broadcasted_iota(jnp.int32, (L, L), 1)
    seen_t = (si - li) * (1 - 2 * drn) <= 0

    gcb = gc_ref[0, 0]
    grb = gr_ref[0, 0]
    q = q_ref[0]
    k = k_ref[0]
    vt = vt_ref[0]
    ones_rows = jnp.where(lax.broadcasted_iota(jnp.int32, (pad, L), 0) == 0, 1.0, 0.0).astype(BF16)

    def head_scores(h):
        hs = slice(h * M_HD, (h + 1) * M_HD)
        m_prev = mscr[h:h + 1, 0:1]
        ct = ct_scr[h]
        m_row = jnp.maximum(grb[4 * nh + h:4 * nh + h + 1, :], m_prev)
        w_t = jnp.exp2(jnp.where(seen_t, gcb[:, h:h + 1], NEG_INF) - m_row)
        r1 = _dot_nt(jnp.concatenate([k[:, hs], ct.astype(BF16)], axis=0), q[:, hs])
        s_t = (r1[0:L, :] * w_t).astype(BF16)
        return m_prev, ct, m_row, s_t, r1[L:, :]

    def head_finish(h, m_prev, ct, m_row, s_t, inter):
        hs = slice(h * M_HD, (h + 1) * M_HD)
        vext = jnp.concatenate([vt[hs, :], ones_rows], axis=0)
        g_r = grb[h:h + 1, :]
        b_r = grb[nh + h:nh + h + 1, :]
        b_last = grb[2 * nh + h:2 * nh + h + 1, 0:1]
        g_max = grb[3 * nh + h:3 * nh + h + 1, 0:1]
        w0 = jnp.exp2(m_prev - m_row)
        tot = _dot(vext, s_t) + w0 * inter
        den = tot[M_HD:M_HD + 1, :]
        floor = jnp.exp2(-(b_r + m_row))
        hcur[hs, :] = tot[0:M_HD, :] / jnp.maximum(jnp.abs(den), floor)

        m_last = jnp.maximum(g_max, m_prev)
        wk = jnp.exp2(g_r - m_last)
        decay = jnp.exp2(m_prev - m_last)
        vw = (vext.astype(F32) * wk).astype(BF16)
        ct_scr[h] = decay * ct + _dot(vw, k[:, hs])
        mscr[h:h + 1, :] = jnp.broadcast_to(b_last + m_last, (1, LANES))

    pending = [head_scores(h) for h in range(min(SCAN_AHEAD, nh))]
    for h in range(nh):
        if h + SCAN_AHEAD < nh:
            pending.append(head_scores(h + SCAN_AHEAD))
        head_finish(h, *pending.pop(0))

    @pl.when(drn == 0)
    def _park():
        hfwd[c] = hcur[...]

    @pl.when(drn == 1)
    def _emit():
        hm = (hcur[...] + hfwd[nc - 1 - c]).T * og_ref[0].astype(F32)
        y = _dot(hm.astype(wo_ref.dtype), wo_ref[...])
        x2 = x_ref[0] + mod_ref[0][2:3, :] * y
        ms = jnp.mean(x2 * x2, axis=-1, keepdims=True)
        y_ref[0] = x2 * lax.rsqrt(ms + EPS) * fw_ref[...]

    if write_state:
        @pl.when(c == nc - 1)
        def _final():
            for h in range(nh):
                cfin = ct_scr[h]
                cout_ref[0, 0, h] = cfin[0:M_HD, :].T
                nout_ref[0, 0, h:h + 1, :] = cfin[M_HD:M_HD + 1, :]
            mout_ref[0, 0] = mscr[...] * LN2


def _mlstm_scan(q, k, vt, gc, gr, og, x, mod3, mod_row, wo_bf, final_w, init, write_state):
    bsz, t, dm = q.shape
    d_model = x.shape[-1]
    L = MCHUNK
    nc = t // L
    chunk = lambda b, d, c: c + d * (nc - 1 - 2 * c)
    tok = lambda b, d, c: (b, chunk(b, d, c), 0)
    tail = lambda b, d, c: (b, nc - 1 - d * c, 0)
    const = lambda b, d, c: (0, 0)
    in_specs = [pl.BlockSpec((1, L, dm), tok)] * 2 + [
        pl.BlockSpec((1, dm, L), lambda b, d, c: (b, 0, chunk(b, d, c))),
        pl.BlockSpec((1, 1, L, gc.shape[-1]), lambda b, d, c: (b, d, chunk(b, d, c), 0)),
        pl.BlockSpec((1, 1, gr.shape[2], L), lambda b, d, c: (b, d, 0, chunk(b, d, c))),
        pl.BlockSpec((1, L, dm), tail),
        pl.BlockSpec((1, L, d_model), tail),
        pl.BlockSpec((1, 3, d_model), lambda b, d, c: (mod_row(b), 0, 0)),
        pl.BlockSpec(wo_bf.shape, const, pipeline_mode=pl.Buffered(1)),
        pl.BlockSpec((1, d_model), const)]
    args = [q, k, vt, gc, gr, og, x, mod3, wo_bf, final_w.reshape(1, d_model)]
    st = lambda b, d, c: (b, d, 0, 0)
    st5 = lambda b, d, c: (b, d, 0, 0, 0)
    if init is not None:
        c0, n0, m0 = init
        in_specs += [pl.BlockSpec((1, 1, M_HEADS, M_HD, M_HD), st5),
                     pl.BlockSpec((1, 1, M_HEADS, M_HD), st),
                     pl.BlockSpec((1, 1, M_HEADS, LANES), st)]
        args += [c0, n0, jnp.broadcast_to(m0[..., None], m0.shape + (LANES,))]
    out_shape = [jax.ShapeDtypeStruct((bsz, t, d_model), F32)]
    out_specs = [pl.BlockSpec((1, L, d_model), tail)]
    if write_state:
        out_shape += [jax.ShapeDtypeStruct((bsz, 2, M_HEADS, M_HD, M_HD), F32),
                      jax.ShapeDtypeStruct((bsz, 2, M_HEADS, M_HD), F32),
                      jax.ShapeDtypeStruct((bsz, 2, M_HEADS, LANES), F32)]
        out_specs += [pl.BlockSpec((1, 1, M_HEADS, M_HD, M_HD), st5),
                      pl.BlockSpec((1, 1, M_HEADS, M_HD), st),
                      pl.BlockSpec((1, 1, M_HEADS, LANES), st)]
    return pl.pallas_call(
        functools.partial(_mlstm_scan_kernel, has_init=init is not None,
                          write_state=write_state, nc=nc),
        out_shape=tuple(out_shape),
        grid=(bsz, 2, nc),
        in_specs=in_specs,
        out_specs=tuple(out_specs),
        scratch_shapes=[pltpu.VMEM((M_HEADS, M_HD + 16, M_HD), F32),
                        pltpu.VMEM((M_HEADS, LANES), F32),
                        pltpu.VMEM((dm, L), F32),
                        pltpu.VMEM((nc, dm, L), F32)],
        compiler_params=_cparams(("parallel", "arbitrary", "arbitrary")),
        name="mlstm_scan",
    )(*args)


def _rope_tables(t):
    nf = HEAD_DIM // 4
    pos = jnp.arange(t)
    row = (pos // GRID_W).astype(F32)
    col = (pos % GRID_W).astype(F32)
    inv = ROPE_BASE ** (-jnp.arange(nf, dtype=F32) / nf)
    ar = row[:, None] * inv[None, :]
    ac = col[:, None] * inv[None, :]
    cos = jnp.concatenate([jnp.cos(ar), jnp.cos(ar), jnp.cos(ac), jnp.cos(ac)], axis=1)
    sin = jnp.concatenate([-jnp.sin(ar), jnp.sin(ar), -jnp.sin(ac), jnp.sin(ac)], axis=1)
    reps = LANES // HEAD_DIM
    return jnp.tile(cos, (1, reps)), jnp.tile(sin, (1, reps))


def kernel(x_prompt, x_sample, cache_k, cache_v, state_C, state_n, state_m, c, c_ctx,
           attn_norm_w, attn_ada_w, attn_ada_b, attn_w_in, attn_sink, attn_w_out,
           mlstm_norm_w, mlstm_ada_w, mlstm_ada_b, mlstm_w_in, mlstm_b_gates, mlstm_w_out,
           final_norm_w):
    assert attn_w_in.shape[0] == 1 and mlstm_w_in.shape[0] == 1, "one layer of each mixer"
    bsz, seq, d = x_prompt.shape
    dbsz, dseq, _ = x_sample.shape
    dkv = N_KV_HEADS * HEAD_DIM
    dm = M_HEADS * M_HD

    n_cond = 1 + dbsz
    cond = jnp.concatenate([c_ctx[None, :], c, jnp.zeros((-n_cond % 8, d), F32)], axis=0)
    attn_mod = _ada(cond, attn_ada_w[0], attn_ada_b[0]).reshape(-1, 3, d)
    mlstm_mod = _ada(cond, mlstm_ada_w[0], mlstm_ada_b[0]).reshape(-1, 3, d)
    ctx_row = lambda b: 0
    lat_row = lambda b: b + 1

    attn_w_in_bf = attn_w_in[0]
    attn_w_out_bf = attn_w_out[0]
    attn_wvt_bf = attn_w_in[0, :, 2 * N_HEADS * HEAD_DIM + dkv:].T
    w_main_bf = mlstm_w_in[0]
    mlstm_wvt_bf = w_main_bf[:, 2 * dm:3 * dm].T
    wgt_bf = w_main_bf[:, 5 * dm:].T
    mlstm_w_out_bf = mlstm_w_out[0]

    def mlstm_layer(x, mod_row, init, write_state):
        q, k, vt, og, gc, gr = _mlstm_in(x, mlstm_mod, mod_row, mlstm_norm_w[0], w_main_bf,
                                         mlstm_wvt_bf, wgt_bf, mlstm_b_gates[0])
        outs = _mlstm_scan(q, k, vt, gc, gr, og, x, mlstm_mod, mod_row, mlstm_w_out_bf, final_norm_w,
                           init, write_state)
        return outs[0], outs[1:]

    q, sg, k_ctx, vt_ctx, v_ctx = _attn_in(x_prompt, attn_mod, ctx_row, attn_norm_w[0], attn_w_in_bf,
                                           attn_wvt_bf, None, F32, True)
    x1 = _attn(q, sg, x_prompt, attn_mod, ctx_row, k_ctx, vt_ctx, None, None, attn_sink[0], attn_w_out_bf)
    y_prompt, (c_fin, n_fin, m_fin) = mlstm_layer(x1, ctx_row, None, True)

    q, sg, k_lat, vt_lat = _attn_in(x_sample, attn_mod, lat_row, attn_norm_w[0], attn_w_in_bf,
                                    attn_wvt_bf, _rope_tables(dseq), BF16, False)
    kc = cache_k[:, 0].reshape(dbsz, -1, dkv).astype(BF16)
    vct = jnp.swapaxes(cache_v[:, 0].reshape(dbsz, -1, dkv), 1, 2).astype(BF16)
    x1 = _attn(q, sg, x_sample, attn_mod, lat_row, kc, vct, k_lat, vt_lat, attn_sink[0], attn_w_out_bf)
    y_sample, _ = mlstm_layer(x1, lat_row, (state_C[:, 0], state_n[:, 0], state_m[:, 0]), False)

    new_cache_k = k_ctx.reshape(bsz, 1, seq, N_KV_HEADS, HEAD_DIM)
    new_cache_v = v_ctx.reshape(bsz, 1, seq, N_KV_HEADS, HEAD_DIM)
    return (y_prompt, y_sample, new_cache_k, new_cache_v,
            c_fin[:, None], n_fin[:, None], m_fin[:, None, :, :, 0])
```

```python
import functools

import jax
import jax.numpy as jnp
from jax import lax
from jax.experimental import pallas as pl
from jax.experimental.pallas import tpu as pltpu

F32 = jnp.float32
BF16 = jnp.bfloat16

HEAD_DIM = 64
N_KV_HEADS = 4
GROUP = 4
N_HEADS = N_KV_HEADS * GROUP
QBLK = 128
GRID_W = 64
ROPE_BASE = 10000.0
M_HEADS = 8
M_HD = 128
EPS = 1e-6

LANES = 128
VMEM_LIMIT = 48 * 1024 * 1024

MCHUNK = 256
ATTN_QB = 2
SCAN_AHEAD = 4
ROW_TILE = 256

NEG_INF = float("-inf")
LOG2E = 1.4426950408889634
LN2 = 0.6931471805599453


def _cparams(sem):
    return pltpu.CompilerParams(dimension_semantics=sem, vmem_limit_bytes=VMEM_LIMIT)


def _silu(x):
    return x * jax.nn.sigmoid(x)


def _log_sigmoid(x):
    return jnp.minimum(x, 0.0) - jnp.log1p(jnp.exp(-jnp.abs(x)))


def _dot(a, b):
    return jnp.dot(a, b, preferred_element_type=F32)


def _dot_nt(a, b):
    return lax.dot_general(a, b, (((1,), (1,)), ((), ())), preferred_element_type=F32)


def _dot_tn(a, b):
    return lax.dot_general(a, b, (((0,), (0,)), ((), ())), preferred_element_type=F32)


def _split3(x):
    hi = x.astype(BF16)
    r = x - hi.astype(F32)
    mid = r.astype(BF16)
    lo = (r - mid.astype(F32)).astype(BF16)
    return hi, mid, lo


def _prenorm(x, norm_w, mod):
    ms = jnp.mean(x * x, axis=-1, keepdims=True)
    y = x * lax.rsqrt(ms + EPS) * norm_w
    return y * (1.0 + mod[1:2, :]) + mod[0:1, :]


def _ada_kernel(cond_ref, w_ref, b_ref, o_ref):
    a = _silu(cond_ref[...]).astype(BF16)
    o_ref[...] = _dot(a, w_ref[...].astype(BF16)) + b_ref[...]


def _ada(cond8, w, b):
    d, n = w.shape
    tn = 512
    return pl.pallas_call(
        _ada_kernel,
        out_shape=jax.ShapeDtypeStruct((cond8.shape[0], n), F32),
        grid=(n // tn,),
        in_specs=[pl.BlockSpec(cond8.shape, lambda j: (0, 0)),
                  pl.BlockSpec((d, tn), lambda j: (0, j)),
                  pl.BlockSpec((1, tn), lambda j: (0, j))],
        out_specs=pl.BlockSpec((cond8.shape[0], tn), lambda j: (0, j)),
        compiler_params=_cparams(("parallel",)),
        name="ada_mod",
    )(cond8, w, b.reshape(1, n))


def _rope(x, cos, sin, lane):
    first = (lane & 31) < 16
    outs = []
    for c in range(x.shape[1] // LANES):
        xc = x[:, c * LANES:(c + 1) * LANES]
        sw = jnp.where(first, pltpu.roll(xc, LANES - 16, 1), pltpu.roll(xc, 16, 1))
        outs.append(xc * cos + sw * sin)
    return jnp.concatenate(outs, axis=1)


def _attn_in_kernel(*refs, rope, emit_v):
    refs = list(refs)
    x_ref, mod_ref, nw_ref, w_ref, wvt_ref = refs[:5]
    pos = 5
    if rope:
        cos_ref, sin_ref = refs[pos:pos + 2]
        pos += 2
    q_ref, sg_ref, k_ref, vt_ref = refs[pos:pos + 4]
    dq = q_ref.shape[-1]
    dkv = k_ref.shape[-1]
    hb = _prenorm(x_ref[0], nw_ref[...], mod_ref[0]).astype(w_ref.dtype)
    q = _dot(hb, w_ref[:, 0:dq])
    g = _dot(hb, w_ref[:, dq:2 * dq])
    k = _dot(hb, w_ref[:, 2 * dq:2 * dq + dkv])
    if rope:
        cos = cos_ref[...]
        sin = sin_ref[...]
        lane = lax.broadcasted_iota(jnp.int32, cos.shape, 1)
        q = _rope(q, cos, sin, lane)
        k = _rope(k, cos, sin, lane)
    q_ref[0] = (q * (HEAD_DIM ** -0.5 * LOG2E)).astype(q_ref.dtype)
    sg_ref[0] = _silu(g).astype(sg_ref.dtype)
    k_ref[0] = k.astype(k_ref.dtype)
    vt_ref[0] = _dot_nt(wvt_ref[...], hb).astype(vt_ref.dtype)
    if emit_v:
        v_ref = refs[pos + 4]
        v_ref[0] = _dot(hb, w_ref[:, 2 * dq + dkv:2 * dq + 2 * dkv]).astype(v_ref.dtype)


def _attn_in(x, mod3, mod_row, norm_w, w_bf, wvt_bf, rope_tabs, k_dtype, emit_v):
    bsz, t, d = x.shape
    dq = N_HEADS * HEAD_DIM
    dkv = N_KV_HEADS * HEAD_DIM
    tm = min(4 * ROW_TILE, t)
    rope = rope_tabs is not None
    tok = lambda b, i: (b, i, 0)
    const = lambda b, i: (0, 0)
    in_specs = [pl.BlockSpec((1, tm, d), tok),
                pl.BlockSpec((1, 3, d), lambda b, i: (mod_row(b), 0, 0)),
                pl.BlockSpec((1, d), const),
                pl.BlockSpec(w_bf.shape, const),
                pl.BlockSpec(wvt_bf.shape, const)]
    args = [x, mod3, norm_w.reshape(1, d), w_bf, wvt_bf]
    if rope:
        in_specs += [pl.BlockSpec((tm, LANES), lambda b, i: (i, 0))] * 2
        args += list(rope_tabs)
    out_shape = [jax.ShapeDtypeStruct((bsz, t, dq), BF16),
                 jax.ShapeDtypeStruct((bsz, t, dq), BF16),
                 jax.ShapeDtypeStruct((bsz, t, dkv), k_dtype),
                 jax.ShapeDtypeStruct((bsz, dkv, t), BF16)]
    out_specs = [pl.BlockSpec((1, tm, dq), tok), pl.BlockSpec((1, tm, dq), tok),
                 pl.BlockSpec((1, tm, dkv), tok),
                 pl.BlockSpec((1, dkv, tm), lambda b, i: (b, 0, i))]
    if emit_v:
        out_shape.append(jax.ShapeDtypeStruct((bsz, t, dkv), F32))
        out_specs.append(pl.BlockSpec((1, tm, dkv), tok))
    return pl.pallas_call(
        functools.partial(_attn_in_kernel, rope=rope, emit_v=emit_v),
        out_shape=tuple(out_shape),
        grid=(bsz, t // tm),
        in_specs=in_specs,
        out_specs=tuple(out_specs),
        compiler_params=_cparams(("parallel", "parallel")),
        name="attn_in_rope" if rope else "attn_in",
    )(*args)


def _attn_kernel(*refs, window, nb):
    if window:
        (q_ref, sg_ref, x_ref, mod_ref, kc_ref, vct_ref, kp_ref, km_ref, kn_ref,
         vpt_ref, vmt_ref, vnt_ref, sink_ref, wo_ref, o_ref, s_scr, p_scr, ot_scr) = refs
    else:
        q_ref, sg_ref, x_ref, mod_ref, kc_ref, vct_ref, sink_ref, wo_ref, o_ref, s_scr, p_scr, ot_scr = refs
    step = pl.program_id(1)
    nqb = q_ref.shape[1] // QBLK
    n_ctx = kc_ref.shape[1] // QBLK
    cols = GROUP * QBLK
    if window:
        kj = lax.broadcasted_iota(jnp.int32, (QBLK, cols), 0)
        qi = lax.broadcasted_iota(jnp.int32, (QBLK, cols), 1) & (QBLK - 1)
        after_diag = kj >= qi
        before_diag = kj <= qi
    ones_rows = jnp.where(lax.broadcasted_iota(jnp.int32, (16, QBLK), 0) == 0, 1.0, 0.0).astype(BF16)
    n_blk = n_ctx + (3 if window else 0)

    def window_blocks(qb, cs, kp, km, kn, lanes):
        def mid(j):
            sl = slice(j * QBLK, (j + 1) * QBLK)
            return km[0, cs, sl] if lanes else km[0, sl, cs]
        first = kp[0, cs, :] if lanes else kp[0][:, cs]
        last = kn[0, cs, :] if lanes else kn[0][:, cs]
        return [first if qb == 0 else mid(qb - 1), mid(qb), last if qb == nqb - 1 else mid(qb + 1)]

    def block_masks(qb):
        if not window:
            return [None] * n_ctx
        prev_ok = after_diag & (step > 0) if qb == 0 else after_diag
        next_ok = before_diag & (step < nb // nqb - 1) if qb == nqb - 1 else before_diag
        return [None] * n_ctx + [prev_ok, None, next_ok]

    def scores(qb, kvh):
        u = qb * N_KV_HEADS + kvh
        cs = slice(kvh * HEAD_DIM, (kvh + 1) * HEAD_DIM)
        heads = [kvh * GROUP + j for j in range(GROUP)]
        qq = q_ref[0, qb * QBLK:(qb + 1) * QBLK, :]
        q4 = jnp.concatenate([qq[:, h * HEAD_DIM:(h + 1) * HEAD_DIM] for h in heads], axis=0)
        sink_row = jnp.concatenate(
            [jnp.broadcast_to(sink_ref[0:1, h:h + 1], (1, QBLK)) for h in heads], axis=1) * LOG2E
        keys = [kc_ref[0, j * QBLK:(j + 1) * QBLK, cs].astype(BF16) for j in range(n_ctx)]
        if window:
            keys += window_blocks(qb, cs, kp_ref, km_ref, kn_ref, False)
        st_all = _dot_nt(jnp.concatenate(keys, axis=0), q4)
        macc = jnp.full((8, cols), NEG_INF, F32)
        for j, ok in enumerate(block_masks(qb)):
            s_blk = st_all[j * QBLK:(j + 1) * QBLK, :]
            if ok is not None:
                s_blk = jnp.where(ok, s_blk, NEG_INF)
            s_scr[u, j] = s_blk
            macc = jnp.maximum(macc, jnp.max(s_blk.reshape(QBLK // 8, 8, cols), axis=0))
        return jnp.maximum(jnp.max(macc, axis=0, keepdims=True), sink_row), sink_row

    def weighted_values(qb, kvh, m_row, sink_row):
        u = qb * N_KV_HEADS + kvh
        cs = slice(kvh * HEAD_DIM, (kvh + 1) * HEAD_DIM)
        for j in range(n_blk):
            p_scr[u, j * QBLK:(j + 1) * QBLK, :] = jnp.exp2(s_scr[u, j] - m_row).astype(BF16)
        vts = [vct_ref[0, cs, j * QBLK:(j + 1) * QBLK] for j in range(n_ctx)]
        if window:
            vts += window_blocks(qb, cs, vpt_ref, vmt_ref, vnt_ref, True)
        vt_ext = jnp.concatenate(
            [jnp.concatenate(vts, axis=1), jnp.tile(ones_rows, (1, n_blk))], axis=0)
        acc = _dot(vt_ext, p_scr[u])
        den = acc[HEAD_DIM:HEAD_DIM + 1, :] + jnp.exp2(sink_row - m_row)
        o_t = acc[0:HEAD_DIM, :] / den
        for j in range(GROUP):
            h = kvh * GROUP + j
            ot_scr[h * HEAD_DIM:(h + 1) * HEAD_DIM, qb * QBLK:(qb + 1) * QBLK] = o_t[:, j * QBLK:(j + 1) * QBLK]

    units = [(qb, kvh) for qb in range(nqb) for kvh in range(N_KV_HEADS)]
    stats = [scores(qb, kvh) for qb, kvh in units]
    for (qb, kvh), st in zip(units, stats):
        weighted_values(qb, kvh, *st)
    z = (ot_scr[...].T * sg_ref[0].astype(F32)).astype(wo_ref.dtype)
    y = _dot(z, wo_ref[...])
    o_ref[0] = x_ref[0] + mod_ref[0][2:3, :] * y


def _attn(q, sg, x, mod3, mod_row, kc, vct, k_lat, vt_lat, sink, wo_bf):
    bsz, t, d = x.shape
    dq = q.shape[-1]
    dkv = kc.shape[-1]
    p_len = kc.shape[1]
    nb = t // QBLK
    nqb = ATTN_QB
    rows = nqb * QBLK
    window = k_lat is not None
    tok = lambda b, i: (b, i, 0)
    in_specs = [pl.BlockSpec((1, rows, dq), tok),
                pl.BlockSpec((1, rows, dq), tok),
                pl.BlockSpec((1, rows, d), tok),
                pl.BlockSpec((1, 3, d), lambda b, i: (mod_row(b), 0, 0)),
                pl.BlockSpec((1, p_len, dkv), lambda b, i: (b, 0, 0)),
                pl.BlockSpec((1, dkv, p_len), lambda b, i: (b, 0, 0))]
    args = [q, sg, x, mod3, kc, vct]
    n_blocks = p_len // QBLK
    if window:
        prev = lambda i: jnp.maximum(i * nqb - 1, 0)
        nxt = lambda i: jnp.minimum((i + 1) * nqb, nb - 1)
        in_specs += [pl.BlockSpec((1, QBLK, dkv), lambda b, i: (b, prev(i), 0)),
                     pl.BlockSpec((1, rows, dkv), tok),
                     pl.BlockSpec((1, QBLK, dkv), lambda b, i: (b, nxt(i), 0)),
                     pl.BlockSpec((1, dkv, QBLK), lambda b, i: (b, 0, prev(i))),
                     pl.BlockSpec((1, dkv, rows), lambda b, i: (b, 0, i)),
                     pl.BlockSpec((1, dkv, QBLK), lambda b, i: (b, 0, nxt(i)))]
        args += [k_lat] * 3 + [vt_lat] * 3
        n_blocks += 3
    in_specs += [pl.BlockSpec((1, N_HEADS), lambda b, i: (0, 0)),
                 pl.BlockSpec(wo_bf.shape, lambda b, i: (0, 0))]
    args += [sink.reshape(1, N_HEADS), wo_bf]
    units = nqb * N_KV_HEADS
    return pl.pallas_call(
        functools.partial(_attn_kernel, window=window, nb=nb),
        out_shape=jax.ShapeDtypeStruct((bsz, t, d), F32),
        grid=(bsz, nb // nqb),
        in_specs=in_specs,
        out_specs=pl.BlockSpec((1, rows, d), tok),
        scratch_shapes=[pltpu.VMEM((units, n_blocks, QBLK, GROUP * QBLK), F32),
                        pltpu.VMEM((units, n_blocks * QBLK, GROUP * QBLK), BF16),
                        pltpu.VMEM((dq, rows), F32)],
        compiler_params=_cparams(("parallel", "parallel")),
        name="attn_window" if window else "attn_ctx",
    )(*args)


def _mlstm_in_kernel(x_ref, mod_ref, nw_ref, wt_ref, bgt_ref,
                     q_ref, k_ref, vt_ref, og_ref, gc_ref, gr_ref):
    dm = q_ref.shape[-1]
    nh = M_HEADS
    L = MCHUNK
    hb = _prenorm(x_ref[0], nw_ref[...], mod_ref[0]).astype(wt_ref.dtype)

    gr = _dot_nt(wt_ref[5 * dm:, :], hb) + bgt_ref[...]
    n_chunks = x_ref.shape[1] // L
    ri = lax.broadcasted_iota(jnp.int32, (L, L), 0)
    ci = lax.broadcasted_iota(jnp.int32, (L, L), 1)
    lane = lax.broadcasted_iota(jnp.int32, (n_chunks * nh, L), 1)
    g_rows = []
    for dr in range(2):
        before = (ri <= ci) if dr == 0 else (ri >= ci)
        tri = jnp.where(before, 1.0, 0.0).astype(BF16)
        base = dr * 2 * nh
        lf = _log_sigmoid(gr[base + nh:base + 2 * nh, :]) * LOG2E
        gi = gr[base:base + nh, :] * LOG2E
        lf_st = jnp.concatenate([lf[:, c * L:(c + 1) * L] for c in range(n_chunks)], axis=0)
        b_st = sum(_dot(piece, tri) for piece in _split3(lf_st))
        g_st = jnp.concatenate([gi[:, c * L:(c + 1) * L] for c in range(n_chunks)], axis=0) - b_st
        run = g_st
        step = 1
        while step < L:
            if dr == 0:
                run = jnp.where(lane >= step, jnp.maximum(run, pltpu.roll(run, step, 1)), run)
            else:
                run = jnp.where(lane < L - step, jnp.maximum(run, pltpu.roll(run, L - step, 1)), run)
            step *= 2
        for cidx in range(n_chunks):
            rows = slice(cidx * L, (cidx + 1) * L)
            blk = slice(cidx * nh, (cidx + 1) * nh)
            b_last = jnp.sum(lf[:, rows], axis=1, keepdims=True)
            g_max = jnp.max(g_st[blk, :], axis=1, keepdims=True)
            g_rows.append(g_st[blk, :])
            gr_ref[0, dr, :, rows] = jnp.concatenate(
                [g_st[blk, :], b_st[blk, :], jnp.broadcast_to(b_last, (nh, L)),
                 jnp.broadcast_to(g_max, (nh, L)), run[blk, :]], axis=0)
    g_sq = jnp.concatenate(g_rows + [jnp.zeros((L - len(g_rows) * nh, L), F32)], axis=0).T
    for dr in range(2):
        for cidx in range(n_chunks):
            idx = dr * n_chunks + cidx
            gc_ref[0, dr, cidx * L:(cidx + 1) * L, :] = g_sq[:, idx * nh:(idx + 1) * nh]

    o = _dot_nt(hb, wt_ref[3 * dm:4 * dm, :])
    g = _dot_nt(hb, wt_ref[4 * dm:5 * dm, :])
    og_ref[0] = (jax.nn.sigmoid(o) * _silu(g)).astype(og_ref.dtype)
    q_ref[0] = _dot_nt(hb, wt_ref[0:dm, :]).astype(q_ref.dtype)
    k_ref[0] = (_dot_nt(hb, wt_ref[dm:2 * dm, :]) * (M_HD ** -0.5)).astype(k_ref.dtype)
    vt_ref[0] = _dot_nt(wt_ref[2 * dm:3 * dm, :], hb).astype(vt_ref.dtype)


def _mlstm_in(x, mod3, mod_row, norm_w, w_t, b_gates):
    bsz, t, d = x.shape
    dm = M_HEADS * M_HD
    ng = 4 * M_HEADS
    tm = min(2 * ROW_TILE, t)
    tok = lambda b, i: (b, i, 0)
    const = lambda b, i: (0, 0)
    big = jax.ShapeDtypeStruct((bsz, t, dm), BF16)
    once = pl.Buffered(1)
    return pl.pallas_call(
        _mlstm_in_kernel,
        out_shape=(big, big, jax.ShapeDtypeStruct((bsz, dm, t), BF16), big,
                   jax.ShapeDtypeStruct((bsz, 2, t, M_HEADS), F32),
                   jax.ShapeDtypeStruct((bsz, 2, 5 * M_HEADS, t), F32)),
        grid=(bsz, t // tm),
        in_specs=[pl.BlockSpec((1, tm, d), tok),
                  pl.BlockSpec((1, 3, d), lambda b, i: (mod_row(b), 0, 0)),
                  pl.BlockSpec((1, d), const),
                  pl.BlockSpec(w_t.shape, const, pipeline_mode=once),
                  pl.BlockSpec((ng, 1), const)],
        out_specs=(pl.BlockSpec((1, tm, dm), tok), pl.BlockSpec((1, tm, dm), tok),
                   pl.BlockSpec((1, dm, tm), lambda b, i: (b, 0, i)), pl.BlockSpec((1, tm, dm), tok),
                   pl.BlockSpec((1, 2, tm, M_HEADS), lambda b, i: (b, 0, i, 0)),
                   pl.BlockSpec((1, 2, 5 * M_HEADS, tm), lambda b, i: (b, 0, 0, i))),
        compiler_params=_cparams(("parallel", "parallel")),
        name="mlstm_in",
    )(x, mod3, norm_w.reshape(1, d), w_t, b_gates.reshape(ng, 1))


def _mlstm_scan_kernel(*refs, has_init, write_state, nc):
    refs = list(refs)
    q_ref, k_ref, vt_ref, gc_ref, gr_ref, og_ref, x_ref, mod_ref, wo_ref, fw_ref = refs[:10]
    pos = 10
    if has_init:
        c0_ref, n0_ref, m0_ref = refs[pos:pos + 3]
        pos += 3
    y_ref = refs[pos]
    pos += 1
    if write_state:
        cout_ref, nout_ref, mout_ref = refs[pos:pos + 3]
        pos += 3
    ct_scr, mscr, hcur, hfwd = refs[pos:pos + 4]

    drn = pl.program_id(1)
    c = pl.program_id(2)
    L = q_ref.shape[1]
    nh = M_HEADS
    pad = ct_scr.shape[1] - M_HD

    @pl.when(c == 0)
    def _init():
        if has_init:
            for h in range(nh):
                ct_scr[h, 0:M_HD, :] = c0_ref[0, 0, h].T
                ct_scr[h, M_HD:M_HD + pad, :] = jnp.concatenate(
                    [n0_ref[0, 0, h:h + 1, :], jnp.zeros((pad - 1, M_HD), F32)], axis=0)
            mscr[...] = m0_ref[0, 0] * LOG2E
        else:
            ct_scr[...] = jnp.zeros(ct_scr.shape, F32)
            mscr[...] = jnp.zeros(mscr.shape, F32)

    si = lax.broadcasted_iota(jnp.int32, (L, L), 0)
    li = lax.broadcasted_iota(jnp.int32, (L, L), 1)
    seen_t = (si - li) * (1 - 2 * drn) <= 0

    gcb = gc_ref[0, 0]
    grb = gr_ref[0, 0]
    q = q_ref[0]
    k = k_ref[0]
    vt = vt_ref[0]
    ones_rows = jnp.where(lax.broadcasted_iota(jnp.int32, (pad, L), 0) == 0, 1.0, 0.0).astype(BF16)

    def head_scores(h):
        hs = slice(h * M_HD, (h + 1) * M_HD)
        m_prev = mscr[h:h + 1, 0:1]
        ct = ct_scr[h]
        m_row = jnp.maximum(grb[4 * nh + h:4 * nh + h + 1, :], m_prev)
        w_t = jnp.exp2(jnp.where(seen_t, gcb[:, h:h + 1], NEG_INF) - m_row)
        r1 = _dot_nt(jnp.concatenate([k[:, hs], ct.astype(BF16)], axis=0), q[:, hs])
        s_t = (r1[0:L, :] * w_t).astype(BF16)
        return m_prev, ct, m_row, s_t, r1[L:, :]

    def head_finish(h, m_prev, ct, m_row, s_t, inter):
        hs = slice(h * M_HD, (h + 1) * M_HD)
        vext = jnp.concatenate([vt[hs, :], ones_rows], axis=0)
        g_r = grb[h:h + 1, :]
        b_r = grb[nh + h:nh + h + 1, :]
        b_last = grb[2 * nh + h:2 * nh + h + 1, 0:1]
        g_max = grb[3 * nh + h:3 * nh + h + 1, 0:1]
        w0 = jnp.exp2(m_prev - m_row)
        tot = _dot(vext, s_t) + w0 * inter
        den = tot[M_HD:M_HD + 1, :]
        floor = jnp.exp2(-(b_r + m_row))
        hcur[hs, :] = tot[0:M_HD, :] / jnp.maximum(jnp.abs(den), floor)

        m_last = jnp.maximum(g_max, m_prev)
        wk = jnp.exp2(g_r - m_last)
        decay = jnp.exp2(m_prev - m_last)
        vw = (vext.astype(F32) * wk).astype(BF16)
        ct_scr[h] = decay * ct + _dot(vw, k[:, hs])
        mscr[h:h + 1, :] = jnp.broadcast_to(b_last + m_last, (1, LANES))

    pending = [head_scores(h) for h in range(min(SCAN_AHEAD, nh))]
    for h in range(nh):
        if h + SCAN_AHEAD < nh:
            pending.append(head_scores(h + SCAN_AHEAD))
        head_finish(h, *pending.pop(0))

    @pl.when(drn == 0)
    def _park():
        hfwd[c] = hcur[...]

    @pl.when(drn == 1)
    def _emit():
        hm = (hcur[...] + hfwd[nc - 1 - c]).T * og_ref[0].astype(F32)
        y = _dot(hm.astype(wo_ref.dtype), wo_ref[...])
        x2 = x_ref[0] + mod_ref[0][2:3, :] * y
        ms = jnp.mean(x2 * x2, axis=-1, keepdims=True)
        y_ref[0] = x2 * lax.rsqrt(ms + EPS) * fw_ref[...]

    if write_state:
        @pl.when(c == nc - 1)
        def _final():
            for h in range(nh):
                cfin = ct_scr[h]
                cout_ref[0, 0, h] = cfin[0:M_HD, :].T
                nout_ref[0, 0, h:h + 1, :] = cfin[M_HD:M_HD + 1, :]
            mout_ref[0, 0] = mscr[...] * LN2


def _mlstm_scan(q, k, vt, gc, gr, og, x, mod3, mod_row, wo_bf, final_w, init, write_state):
    bsz, t, dm = q.shape
    d_model = x.shape[-1]
    L = MCHUNK
    nc = t // L
    chunk = lambda b, d, c: c + d * (nc - 1 - 2 * c)
    tok = lambda b, d, c: (b, chunk(b, d, c), 0)
    tail = lambda b, d, c: (b, nc - 1 - d * c, 0)
    const = lambda b, d, c: (0, 0)
    in_specs = [pl.BlockSpec((1, L, dm), tok)] * 2 + [
        pl.BlockSpec((1, dm, L), lambda b, d, c: (b, 0, chunk(b, d, c))),
        pl.BlockSpec((1, 1, L, gc.shape[-1]), lambda b, d, c: (b, d, chunk(b, d, c), 0)),
        pl.BlockSpec((1, 1, gr.shape[2], L), lambda b, d, c: (b, d, 0, chunk(b, d, c))),
        pl.BlockSpec((1, L, dm), tail),
        pl.BlockSpec((1, L, d_model), tail),
        pl.BlockSpec((1, 3, d_model), lambda b, d, c: (mod_row(b), 0, 0)),
        pl.BlockSpec(wo_bf.shape, const, pipeline_mode=pl.Buffered(1)),
        pl.BlockSpec((1, d_model), const)]
    args = [q, k, vt, gc, gr, og, x, mod3, wo_bf, final_w.reshape(1, d_model)]
    st = lambda b, d, c: (b, d, 0, 0)
    st5 = lambda b, d, c: (b, d, 0, 0, 0)
    if init is not None:
        c0, n0, m0 = init
        in_specs += [pl.BlockSpec((1, 1, M_HEADS, M_HD, M_HD), st5),
                     pl.BlockSpec((1, 1, M_HEADS, M_HD), st),
                     pl.BlockSpec((1, 1, M_HEADS, LANES), st)]
        args += [c0, n0, jnp.broadcast_to(m0[..., None], m0.shape + (LANES,))]
    out_shape = [jax.ShapeDtypeStruct((bsz, t, d_model), F32)]
    out_specs = [pl.BlockSpec((1, L, d_model), tail)]
    if write_state:
        out_shape += [jax.ShapeDtypeStruct((bsz, 2, M_HEADS, M_HD, M_HD), F32),
                      jax.ShapeDtypeStruct((bsz, 2, M_HEADS, M_HD), F32),
                      jax.ShapeDtypeStruct((bsz, 2, M_HEADS, LANES), F32)]
        out_specs += [pl.BlockSpec((1, 1, M_HEADS, M_HD, M_HD), st5),
                      pl.BlockSpec((1, 1, M_HEADS, M_HD), st),
                      pl.BlockSpec((1, 1, M_HEADS, LANES), st)]
    return pl.pallas_call(
        functools.partial(_mlstm_scan_kernel, has_init=init is not None,
                          write_state=write_state, nc=nc),
        out_shape=tuple(out_shape),
        grid=(bsz, 2, nc),
        in_specs=in_specs,
        out_specs=tuple(out_specs),
        scratch_shapes=[pltpu.VMEM((M_HEADS, M_HD + 16, M_HD), F32),
                        pltpu.VMEM((M_HEADS, LANES), F32),
                        pltpu.VMEM((dm, L), F32),
                        pltpu.VMEM((nc, dm, L), F32)],
        compiler_params=_cparams(("parallel", "arbitrary", "arbitrary")),
        name="mlstm_scan",
    )(*args)


def _rope_tables(t):
    nf = HEAD_DIM // 4
    pos = jnp.arange(t)
    row = (pos // GRID_W).astype(F32)
    col = (pos % GRID_W).astype(F32)
    inv = ROPE_BASE ** (-jnp.arange(nf, dtype=F32) / nf)
    ar = row[:, None] * inv[None, :]
    ac = col[:, None] * inv[None, :]
    cos = jnp.concatenate([jnp.cos(ar), jnp.cos(ar), jnp.cos(ac), jnp.cos(ac)], axis=1)
    sin = jnp.concatenate([-jnp.sin(ar), jnp.sin(ar), -jnp.sin(ac), jnp.sin(ac)], axis=1)
    reps = LANES // HEAD_DIM
    return jnp.tile(cos, (1, reps)), jnp.tile(sin, (1, reps))


def kernel(x_prompt, x_sample, cache_k, cache_v, state_C, state_n, state_m, c, c_ctx,
           attn_norm_w, attn_ada_w, attn_ada_b, attn_w_in, attn_sink, attn_w_out,
           mlstm_norm_w, mlstm_ada_w, mlstm_ada_b, mlstm_w_in, mlstm_b_gates, mlstm_w_out,
           final_norm_w):
    assert attn_w_in.shape[0] == 1 and mlstm_w_in.shape[0] == 1, "one layer of each mixer"
    bsz, seq, d = x_prompt.shape
    dbsz, dseq, _ = x_sample.shape
    dkv = N_KV_HEADS * HEAD_DIM
    dm = M_HEADS * M_HD

    n_cond = 1 + dbsz
    cond = jnp.concatenate([c_ctx[None, :], c, jnp.zeros((-n_cond % 8, d), F32)], axis=0)
    attn_mod = _ada(cond, attn_ada_w[0], attn_ada_b[0]).reshape(-1, 3, d)
    mlstm_mod = _ada(cond, mlstm_ada_w[0], mlstm_ada_b[0]).reshape(-1, 3, d)
    ctx_row = lambda b: 0
    lat_row = lambda b: b + 1

    attn_w_in_bf = attn_w_in[0]
    attn_w_out_bf = attn_w_out[0]
    attn_wvt_bf = attn_w_in[0, :, 2 * N_HEADS * HEAD_DIM + dkv:].T
    mlstm_w_in_t = mlstm_w_in[0].T
    mlstm_w_out_bf = mlstm_w_out[0]

    def mlstm_layer(x, mod_row, init, write_state):
        q, k, vt, og, gc, gr = _mlstm_in(x, mlstm_mod, mod_row, mlstm_norm_w[0], mlstm_w_in_t,
                                         mlstm_b_gates[0])
        outs = _mlstm_scan(q, k, vt, gc, gr, og, x, mlstm_mod, mod_row, mlstm_w_out_bf, final_norm_w,
                           init, write_state)
        return outs[0], outs[1:]

    q, sg, k_ctx, vt_ctx, v_ctx = _attn_in(x_prompt, attn_mod, ctx_row, attn_norm_w[0], attn_w_in_bf,
                                           attn_wvt_bf, None, F32, True)
    x1 = _attn(q, sg, x_prompt, attn_mod, ctx_row, k_ctx, vt_ctx, None, None, attn_sink[0], attn_w_out_bf)
    y_prompt, (c_fin, n_fin, m_fin) = mlstm_layer(x1, ctx_row, None, True)

    q, sg, k_lat, vt_lat = _attn_in(x_sample, attn_mod, lat_row, attn_norm_w[0], attn_w_in_bf,
                                    attn_wvt_bf, _rope_tables(dseq), BF16, False)
    kc = cache_k[:, 0].reshape(dbsz, -1, dkv).astype(BF16)
    vct = jnp.swapaxes(cache_v[:, 0].reshape(dbsz, -1, dkv), 1, 2).astype(BF16)
    x1 = _attn(q, sg, x_sample, attn_mod, lat_row, kc, vct, k_lat, vt_lat, attn_sink[0], attn_w_out_bf)
    y_sample, _ = mlstm_layer(x1, lat_row, (state_C[:, 0], state_n[:, 0], state_m[:, 0]), False)

    new_cache_k = k_ctx.reshape(bsz, 1, seq, N_KV_HEADS, HEAD_DIM)
    new_cache_v = v_ctx.reshape(bsz, 1, seq, N_KV_HEADS, HEAD_DIM)
    return (y_prompt, y_sample, new_cache_k, new_cache_v,
            c_fin[:, None], n_fin[:, None], m_fin[:, None, :, :, 0])
```

```python
import functools

import jax
import jax.numpy as jnp
from jax import lax
from jax.experimental import pallas as pl
from jax.experimental.pallas import tpu as pltpu

F32 = jnp.float32
BF16 = jnp.bfloat16

HEAD_DIM = 64
N_KV_HEADS = 4
GROUP = 4
N_HEADS = N_KV_HEADS * GROUP
QBLK = 128
GRID_W = 64
ROPE_BASE = 10000.0
M_HEADS = 8
M_HD = 128
EPS = 1e-6

LANES = 128
VMEM_LIMIT = 48 * 1024 * 1024
MLSTM_IN_VMEM_LIMIT = 60 * 1024 * 1024

MCHUNK = 256
ATTN_QB = 2
SCAN_AHEAD = 4
ROW_TILE = 256

NEG_INF = float("-inf")
LOG2E = 1.4426950408889634
LN2 = 0.6931471805599453


def _cparams(sem):
    return pltpu.CompilerParams(dimension_semantics=sem, vmem_limit_bytes=VMEM_LIMIT)


def _silu(x):
    return x * jax.nn.sigmoid(x)


def _log_sigmoid(x):
    return jnp.minimum(x, 0.0) - jnp.log1p(jnp.exp(-jnp.abs(x)))


def _dot(a, b):
    return jnp.dot(a, b, preferred_element_type=F32)


def _dot_nt(a, b):
    return lax.dot_general(a, b, (((1,), (1,)), ((), ())), preferred_element_type=F32)


def _dot_tn(a, b):
    return lax.dot_general(a, b, (((0,), (0,)), ((), ())), preferred_element_type=F32)


def _split3(x):
    hi = x.astype(BF16)
    r = x - hi.astype(F32)
    mid = r.astype(BF16)
    lo = (r - mid.astype(F32)).astype(BF16)
    return hi, mid, lo


def _prenorm(x, norm_w, mod):
    ms = jnp.mean(x * x, axis=-1, keepdims=True)
    y = x * lax.rsqrt(ms + EPS) * norm_w
    return y * (1.0 + mod[1:2, :]) + mod[0:1, :]


def _ada_kernel(cond_ref, w_ref, b_ref, o_ref):
    a = _silu(cond_ref[...]).astype(BF16)
    o_ref[...] = _dot(a, w_ref[...].astype(BF16)) + b_ref[...]


def _ada(cond8, w, b):
    d, n = w.shape
    tn = 512
    return pl.pallas_call(
        _ada_kernel,
        out_shape=jax.ShapeDtypeStruct((cond8.shape[0], n), F32),
        grid=(n // tn,),
        in_specs=[pl.BlockSpec(cond8.shape, lambda j: (0, 0)),
                  pl.BlockSpec((d, tn), lambda j: (0, j)),
                  pl.BlockSpec((1, tn), lambda j: (0, j))],
        out_specs=pl.BlockSpec((cond8.shape[0], tn), lambda j: (0, j)),
        compiler_params=_cparams(("parallel",)),
        name="ada_mod",
    )(cond8, w, b.reshape(1, n))


def _rope(x, cos, sin, lane):
    first = (lane & 31) < 16
    outs = []
    for c in range(x.shape[1] // LANES):
        xc = x[:, c * LANES:(c + 1) * LANES]
        sw = jnp.where(first, pltpu.roll(xc, LANES - 16, 1), pltpu.roll(xc, 16, 1))
        outs.append(xc * cos + sw * sin)
    return jnp.concatenate(outs, axis=1)


def _attn_in_kernel(*refs, rope, emit_v):
    refs = list(refs)
    x_ref, mod_ref, nw_ref, w_ref, wvt_ref = refs[:5]
    pos = 5
    if rope:
        cos_ref, sin_ref = refs[pos:pos + 2]
        pos += 2
    q_ref, sg_ref, k_ref, vt_ref = refs[pos:pos + 4]
    dq = q_ref.shape[-1]
    dkv = k_ref.shape[-1]
    hb = _prenorm(x_ref[0], nw_ref[...], mod_ref[0]).astype(w_ref.dtype)
    q = _dot(hb, w_ref[:, 0:dq])
    g = _dot(hb, w_ref[:, dq:2 * dq])
    k = _dot(hb, w_ref[:, 2 * dq:2 * dq + dkv])
    if rope:
        cos = cos_ref[...]
        sin = sin_ref[...]
        lane = lax.broadcasted_iota(jnp.int32, cos.shape, 1)
        q = _rope(q, cos, sin, lane)
        k = _rope(k, cos, sin, lane)
    q_ref[0] = (q * (HEAD_DIM ** -0.5 * LOG2E)).astype(q_ref.dtype)
    sg_ref[0] = _silu(g).astype(sg_ref.dtype)
    k_ref[0] = k.astype(k_ref.dtype)
    vt_ref[0] = _dot_nt(wvt_ref[...], hb).astype(vt_ref.dtype)
    if emit_v:
        v_ref = refs[pos + 4]
        v_ref[0] = _dot(hb, w_ref[:, 2 * dq + dkv:2 * dq + 2 * dkv]).astype(v_ref.dtype)


def _attn_in(x, mod3, mod_row, norm_w, w_bf, wvt_bf, rope_tabs, k_dtype, emit_v):
    bsz, t, d = x.shape
    dq = N_HEADS * HEAD_DIM
    dkv = N_KV_HEADS * HEAD_DIM
    tm = min(4 * ROW_TILE, t)
    rope = rope_tabs is not None
    tok = lambda b, i: (b, i, 0)
    const = lambda b, i: (0, 0)
    in_specs = [pl.BlockSpec((1, tm, d), tok),
                pl.BlockSpec((1, 3, d), lambda b, i: (mod_row(b), 0, 0)),
                pl.BlockSpec((1, d), const),
                pl.BlockSpec(w_bf.shape, const),
                pl.BlockSpec(wvt_bf.shape, const)]
    args = [x, mod3, norm_w.reshape(1, d), w_bf, wvt_bf]
    if rope:
        in_specs += [pl.BlockSpec((tm, LANES), lambda b, i: (i, 0))] * 2
        args += list(rope_tabs)
    out_shape = [jax.ShapeDtypeStruct((bsz, t, dq), BF16),
                 jax.ShapeDtypeStruct((bsz, t, dq), BF16),
                 jax.ShapeDtypeStruct((bsz, t, dkv), k_dtype),
                 jax.ShapeDtypeStruct((bsz, dkv, t), BF16)]
    out_specs = [pl.BlockSpec((1, tm, dq), tok), pl.BlockSpec((1, tm, dq), tok),
                 pl.BlockSpec((1, tm, dkv), tok),
                 pl.BlockSpec((1, dkv, tm), lambda b, i: (b, 0, i))]
    if emit_v:
        out_shape.append(jax.ShapeDtypeStruct((bsz, t, dkv), F32))
        out_specs.append(pl.BlockSpec((1, tm, dkv), tok))
    return pl.pallas_call(
        functools.partial(_attn_in_kernel, rope=rope, emit_v=emit_v),
        out_shape=tuple(out_shape),
        grid=(bsz, t // tm),
        in_specs=in_specs,
        out_specs=tuple(out_specs),
        compiler_params=_cparams(("parallel", "parallel")),
        name="attn_in_rope" if rope else "attn_in",
    )(*args)


def _attn_kernel(*refs, window, nb):
    if window:
        (q_ref, sg_ref, x_ref, mod_ref, kc_ref, vct_ref, kp_ref, km_ref, kn_ref,
         vpt_ref, vmt_ref, vnt_ref, sink_ref, wo_ref, o_ref, s_scr, p_scr, ot_scr) = refs
    else:
        q_ref, sg_ref, x_ref, mod_ref, kc_ref, vct_ref, sink_ref, wo_ref, o_ref, s_scr, p_scr, ot_scr = refs
    step = pl.program_id(1)
    nqb = q_ref.shape[1] // QBLK
    n_ctx = kc_ref.shape[1] // QBLK
    cols = GROUP * QBLK
    if window:
        kj = lax.broadcasted_iota(jnp.int32, (QBLK, cols), 0)
        qi = lax.broadcasted_iota(jnp.int32, (QBLK, cols), 1) & (QBLK - 1)
        after_diag = kj >= qi
        before_diag = kj <= qi
    ones_rows = jnp.where(lax.broadcasted_iota(jnp.int32, (16, QBLK), 0) == 0, 1.0, 0.0).astype(BF16)
    n_blk = n_ctx + (3 if window else 0)

    def window_blocks(qb, cs, kp, km, kn, lanes):
        def mid(j):
            sl = slice(j * QBLK, (j + 1) * QBLK)
            return km[0, cs, sl] if lanes else km[0, sl, cs]
        first = kp[0, cs, :] if lanes else kp[0][:, cs]
        last = kn[0, cs, :] if lanes else kn[0][:, cs]
        return [first if qb == 0 else mid(qb - 1), mid(qb), last if qb == nqb - 1 else mid(qb + 1)]

    def block_masks(qb):
        if not window:
            return [None] * n_ctx
        prev_ok = after_diag & (step > 0) if qb == 0 else after_diag
        next_ok = before_diag & (step < nb // nqb - 1) if qb == nqb - 1 else before_diag
        return [None] * n_ctx + [prev_ok, None, next_ok]

    def scores(qb, kvh):
        u = qb * N_KV_HEADS + kvh
        cs = slice(kvh * HEAD_DIM, (kvh + 1) * HEAD_DIM)
        heads = [kvh * GROUP + j for j in range(GROUP)]
        qq = q_ref[0, qb * QBLK:(qb + 1) * QBLK, :]
        q4 = jnp.concatenate([qq[:, h * HEAD_DIM:(h + 1) * HEAD_DIM] for h in heads], axis=0)
        sink_row = jnp.concatenate(
            [jnp.broadcast_to(sink_ref[0:1, h:h + 1], (1, QBLK)) for h in heads], axis=1) * LOG2E
        keys = [kc_ref[0, j * QBLK:(j + 1) * QBLK, cs].astype(BF16) for j in range(n_ctx)]
        if window:
            keys += window_blocks(qb, cs, kp_ref, km_ref, kn_ref, False)
        st_all = _dot_nt(jnp.concatenate(keys, axis=0), q4)
        macc = jnp.full((8, cols), NEG_INF, F32)
        for j, ok in enumerate(block_masks(qb)):
            s_blk = st_all[j * QBLK:(j + 1) * QBLK, :]
            if ok is not None:
                s_blk = jnp.where(ok, s_blk, NEG_INF)
            s_scr[u, j] = s_blk
            macc = jnp.maximum(macc, jnp.max(s_blk.reshape(QBLK // 8, 8, cols), axis=0))
        return jnp.maximum(jnp.max(macc, axis=0, keepdims=True), sink_row), sink_row

    def weighted_values(qb, kvh, m_row, sink_row):
        u = qb * N_KV_HEADS + kvh
        cs = slice(kvh * HEAD_DIM, (kvh + 1) * HEAD_DIM)
        for j in range(n_blk):
            p_scr[u, j * QBLK:(j + 1) * QBLK, :] = jnp.exp2(s_scr[u, j] - m_row).astype(BF16)
        vts = [vct_ref[0, cs, j * QBLK:(j + 1) * QBLK] for j in range(n_ctx)]
        if window:
            vts += window_blocks(qb, cs, vpt_ref, vmt_ref, vnt_ref, True)
        vt_ext = jnp.concatenate(
            [jnp.concatenate(vts, axis=1), jnp.tile(ones_rows, (1, n_blk))], axis=0)
        acc = _dot(vt_ext, p_scr[u])
        den = acc[HEAD_DIM:HEAD_DIM + 1, :] + jnp.exp2(sink_row - m_row)
        o_t = acc[0:HEAD_DIM, :] / den
        for j in range(GROUP):
            h = kvh * GROUP + j
            ot_scr[h * HEAD_DIM:(h + 1) * HEAD_DIM, qb * QBLK:(qb + 1) * QBLK] = o_t[:, j * QBLK:(j + 1) * QBLK]

    units = [(qb, kvh) for qb in range(nqb) for kvh in range(N_KV_HEADS)]
    stats = [scores(qb, kvh) for qb, kvh in units]
    for (qb, kvh), st in zip(units, stats):
        weighted_values(qb, kvh, *st)
    z = (ot_scr[...].T * sg_ref[0].astype(F32)).astype(wo_ref.dtype)
    y = _dot(z, wo_ref[...])
    o_ref[0] = x_ref[0] + mod_ref[0][2:3, :] * y


def _attn(q, sg, x, mod3, mod_row, kc, vct, k_lat, vt_lat, sink, wo_bf):
    bsz, t, d = x.shape
    dq = q.shape[-1]
    dkv = kc.shape[-1]
    p_len = kc.shape[1]
    nb = t // QBLK
    nqb = ATTN_QB
    rows = nqb * QBLK
    window = k_lat is not None
    tok = lambda b, i: (b, i, 0)
    in_specs = [pl.BlockSpec((1, rows, dq), tok),
                pl.BlockSpec((1, rows, dq), tok),
                pl.BlockSpec((1, rows, d), tok),
                pl.BlockSpec((1, 3, d), lambda b, i: (mod_row(b), 0, 0)),
                pl.BlockSpec((1, p_len, dkv), lambda b, i: (b, 0, 0)),
                pl.BlockSpec((1, dkv, p_len), lambda b, i: (b, 0, 0))]
    args = [q, sg, x, mod3, kc, vct]
    n_blocks = p_len // QBLK
    if window:
        prev = lambda i: jnp.maximum(i * nqb - 1, 0)
        nxt = lambda i: jnp.minimum((i + 1) * nqb, nb - 1)
        in_specs += [pl.BlockSpec((1, QBLK, dkv), lambda b, i: (b, prev(i), 0)),
                     pl.BlockSpec((1, rows, dkv), tok),
                     pl.BlockSpec((1, QBLK, dkv), lambda b, i: (b, nxt(i), 0)),
                     pl.BlockSpec((1, dkv, QBLK), lambda b, i: (b, 0, prev(i))),
                     pl.BlockSpec((1, dkv, rows), lambda b, i: (b, 0, i)),
                     pl.BlockSpec((1, dkv, QBLK), lambda b, i: (b, 0, nxt(i)))]
        args += [k_lat] * 3 + [vt_lat] * 3
        n_blocks += 3
    in_specs += [pl.BlockSpec((1, N_HEADS), lambda b, i: (0, 0)),
                 pl.BlockSpec(wo_bf.shape, lambda b, i: (0, 0))]
    args += [sink.reshape(1, N_HEADS), wo_bf]
    units = nqb * N_KV_HEADS
    return pl.pallas_call(
        functools.partial(_attn_kernel, window=window, nb=nb),
        out_shape=jax.ShapeDtypeStruct((bsz, t, d), F32),
        grid=(bsz, nb // nqb),
        in_specs=in_specs,
        out_specs=pl.BlockSpec((1, rows, d), tok),
        scratch_shapes=[pltpu.VMEM((units, n_blocks, QBLK, GROUP * QBLK), F32),
                        pltpu.VMEM((units, n_blocks * QBLK, GROUP * QBLK), BF16),
                        pltpu.VMEM((dq, rows), F32)],
        compiler_params=_cparams(("parallel", "parallel")),
        name="attn_window" if window else "attn_ctx",
    )(*args)


def _mlstm_in_kernel(x_ref, mod_ref, nw_ref, wt_ref, bgt_ref,
                     q_ref, k_ref, vt_ref, og_ref, gc_ref, gr_ref):
    dm = q_ref.shape[-1]
    nh = M_HEADS
    L = MCHUNK
    hb = _prenorm(x_ref[0], nw_ref[...], mod_ref[0]).astype(wt_ref.dtype)

    gr = _dot_nt(wt_ref[5 * dm:, :], hb) + bgt_ref[...]
    n_chunks = x_ref.shape[1] // L
    ri = lax.broadcasted_iota(jnp.int32, (L, L), 0)
    ci = lax.broadcasted_iota(jnp.int32, (L, L), 1)
    lane = lax.broadcasted_iota(jnp.int32, (n_chunks * nh, L), 1)
    g_rows = []
    for dr in range(2):
        before = (ri <= ci) if dr == 0 else (ri >= ci)
        tri = jnp.where(before, 1.0, 0.0).astype(BF16)
        base = dr * 2 * nh
        lf = _log_sigmoid(gr[base + nh:base + 2 * nh, :]) * LOG2E
        gi = gr[base:base + nh, :] * LOG2E
        lf_st = jnp.concatenate([lf[:, c * L:(c + 1) * L] for c in range(n_chunks)], axis=0)
        b_st = sum(_dot(piece, tri) for piece in _split3(lf_st))
        g_st = jnp.concatenate([gi[:, c * L:(c + 1) * L] for c in range(n_chunks)], axis=0) - b_st
        run = g_st
        step = 1
        while step < L:
            if dr == 0:
                run = jnp.where(lane >= step, jnp.maximum(run, pltpu.roll(run, step, 1)), run)
            else:
                run = jnp.where(lane < L - step, jnp.maximum(run, pltpu.roll(run, L - step, 1)), run)
            step *= 2
        for cidx in range(n_chunks):
            rows = slice(cidx * L, (cidx + 1) * L)
            blk = slice(cidx * nh, (cidx + 1) * nh)
            b_last = jnp.sum(lf[:, rows], axis=1, keepdims=True)
            g_max = jnp.max(g_st[blk, :], axis=1, keepdims=True)
            g_rows.append(g_st[blk, :])
            gr_ref[0, dr, :, rows] = jnp.concatenate(
                [g_st[blk, :], b_st[blk, :], jnp.broadcast_to(b_last, (nh, L)),
                 jnp.broadcast_to(g_max, (nh, L)), run[blk, :]], axis=0)
    g_sq = jnp.concatenate(g_rows + [jnp.zeros((L - len(g_rows) * nh, L), F32)], axis=0).T
    for dr in range(2):
        for cidx in range(n_chunks):
            idx = dr * n_chunks + cidx
            gc_ref[0, dr, cidx * L:(cidx + 1) * L, :] = g_sq[:, idx * nh:(idx + 1) * nh]

    o = _dot_nt(hb, wt_ref[3 * dm:4 * dm, :])
    g = _dot_nt(hb, wt_ref[4 * dm:5 * dm, :])
    og_ref[0] = (jax.nn.sigmoid(o) * _silu(g)).astype(og_ref.dtype)
    q_ref[0] = _dot_nt(hb, wt_ref[0:dm, :]).astype(q_ref.dtype)
    k_ref[0] = (_dot_nt(hb, wt_ref[dm:2 * dm, :]) * (M_HD ** -0.5)).astype(k_ref.dtype)
    vt_ref[0] = _dot_nt(wt_ref[2 * dm:3 * dm, :], hb).astype(vt_ref.dtype)


def _mlstm_in(x, mod3, mod_row, norm_w, w_t, b_gates):
    bsz, t, d = x.shape
    dm = M_HEADS * M_HD
    ng = 4 * M_HEADS
    tm = min(4 * ROW_TILE, t)
    tok = lambda b, i: (b, i, 0)
    const = lambda b, i: (0, 0)
    big = jax.ShapeDtypeStruct((bsz, t, dm), BF16)
    once = pl.Buffered(1)
    return pl.pallas_call(
        _mlstm_in_kernel,
        out_shape=(big, big, jax.ShapeDtypeStruct((bsz, dm, t), BF16), big,
                   jax.ShapeDtypeStruct((bsz, 2, t, M_HEADS), F32),
                   jax.ShapeDtypeStruct((bsz, 2, 5 * M_HEADS, t), F32)),
        grid=(bsz, t // tm),
        in_specs=[pl.BlockSpec((1, tm, d), tok),
                  pl.BlockSpec((1, 3, d), lambda b, i: (mod_row(b), 0, 0)),
                  pl.BlockSpec((1, d), const),
                  pl.BlockSpec(w_t.shape, const, pipeline_mode=once),
                  pl.BlockSpec((ng, 1), const)],
        out_specs=(pl.BlockSpec((1, tm, dm), tok), pl.BlockSpec((1, tm, dm), tok),
                   pl.BlockSpec((1, dm, tm), lambda b, i: (b, 0, i)), pl.BlockSpec((1, tm, dm), tok),
                   pl.BlockSpec((1, 2, tm, M_HEADS), lambda b, i: (b, 0, i, 0)),
                   pl.BlockSpec((1, 2, 5 * M_HEADS, tm), lambda b, i: (b, 0, 0, i))),
        compiler_params=pltpu.CompilerParams(dimension_semantics=("parallel", "parallel"),
                                             vmem_limit_bytes=MLSTM_IN_VMEM_LIMIT),
        name="mlstm_in",
    )(x, mod3, norm_w.reshape(1, d), w_t, b_gates.reshape(ng, 1))


def _mlstm_scan_kernel(*refs, has_init, write_state, nc):
    refs = list(refs)
    q_ref, k_ref, vt_ref, gc_ref, gr_ref, og_ref, x_ref, mod_ref, wo_ref, fw_ref = refs[:10]
    pos = 10
    if has_init:
        c0_ref, n0_ref, m0_ref = refs[pos:pos + 3]
        pos += 3
    y_ref = refs[pos]
    pos += 1
    if write_state:
        cout_ref, nout_ref, mout_ref = refs[pos:pos + 3]
        pos += 3
    ct_scr, mscr, hcur, hfwd = refs[pos:pos + 4]

    drn = pl.program_id(1)
    c = pl.program_id(2)
    L = q_ref.shape[1]
    nh = M_HEADS
    pad = ct_scr.shape[1] - M_HD

    @pl.when(c == 0)
    def _init():
        if has_init:
            for h in range(nh):
                ct_scr[h, 0:M_HD, :] = c0_ref[0, 0, h].T
                ct_scr[h, M_HD:M_HD + pad, :] = jnp.concatenate(
                    [n0_ref[0, 0, h:h + 1, :], jnp.zeros((pad - 1, M_HD), F32)], axis=0)
            mscr[...] = m0_ref[0, 0] * LOG2E
        else:
            ct_scr[...] = jnp.zeros(ct_scr.shape, F32)
            mscr[...] = jnp.zeros(mscr.shape, F32)

    si = lax.broadcasted_iota(jnp.int32, (L, L), 0)
    li = lax.broadcasted_iota(jnp.int32, (L, L), 1)
    seen_t = (si - li) * (1 - 2 * drn) <= 0

    gcb = gc_ref[0, 0]
    grb = gr_ref[0, 0]
    q = q_ref[0]
    k = k_ref[0]
    vt = vt_ref[0]
    ones_rows = jnp.where(lax.broadcasted_iota(jnp.int32, (pad, L), 0) == 0, 1.0, 0.0).astype(BF16)

    def head_scores(h):
        hs = slice(h * M_HD, (h + 1) * M_HD)
        m_prev = mscr[h:h + 1, 0:1]
        ct = ct_scr[h]
        m_row = jnp.maximum(grb[4 * nh + h:4 * nh + h + 1, :], m_prev)
        w_t = jnp.exp2(jnp.where(seen_t, gcb[:, h:h + 1], NEG_INF) - m_row)
        r1 = _dot_nt(jnp.concatenate([k[:, hs], ct.astype(BF16)], axis=0), q[:, hs])
        s_t = (r1[0:L, :] * w_t).astype(BF16)
        return m_prev, ct, m_row, s_t, r1[L:, :]

    def head_finish(h, m_prev, ct, m_row, s_t, inter):
        hs = slice(h * M_HD, (h + 1) * M_HD)
        vext = jnp.concatenate([vt[hs, :], ones_rows], axis=0)
        g_r = grb[h:h + 1, :]
        b_r = grb[nh + h:nh + h + 1, :]
        b_last = grb[2 * nh + h:2 * nh + h + 1, 0:1]
        g_max = grb[3 * nh + h:3 * nh + h + 1, 0:1]
        w0 = jnp.exp2(m_prev - m_row)
        tot = _dot(vext, s_t) + w0 * inter
        den = tot[M_HD:M_HD + 1, :]
        floor = jnp.exp2(-(b_r + m_row))
        hcur[hs, :] = tot[0:M_HD, :] / jnp.maximum(jnp.abs(den), floor)

        m_last = jnp.maximum(g_max, m_prev)
        wk = jnp.exp2(g_r - m_last)
        decay = jnp.exp2(m_prev - m_last)
        vw = (vext.astype(F32) * wk).astype(BF16)
        ct_scr[h] = decay * ct + _dot(vw, k[:, hs])
        mscr[h:h + 1, :] = jnp.broadcast_to(b_last + m_last, (1, LANES))

    pending = [head_scores(h) for h in range(min(SCAN_AHEAD, nh))]
    for h in range(nh):
        if h + SCAN_AHEAD < nh:
            pending.append(head_scores(h + SCAN_AHEAD))
        head_finish(h, *pending.pop(0))

    @pl.when(drn == 0)
    def _park():
        hfwd[c] = hcur[...]

    @pl.when(drn == 1)
    def _emit():
        hm = (hcur[...] + hfwd[nc - 1 - c]).T * og_ref[0].astype(F32)
        y = _dot(hm.astype(wo_ref.dtype), wo_ref[...])
        x2 = x_ref[0] + mod_ref[0][2:3, :] * y
        ms = jnp.mean(x2 * x2, axis=-1, keepdims=True)
        y_ref[0] = x2 * lax.rsqrt(ms + EPS) * fw_ref[...]

    if write_state:
        @pl.when(c == nc - 1)
        def _final():
            for h in range(nh):
                cfin = ct_scr[h]
                cout_ref[0, 0, h] = cfin[0:M_HD, :].T
                nout_ref[0, 0, h:h + 1, :] = cfin[M_HD:M_HD + 1, :]
            mout_ref[0, 0] = mscr[...] * LN2


def _mlstm_scan(q, k, vt, gc, gr, og, x, mod3, mod_row, wo_bf, final_w, init, write_state):
    bsz, t, dm = q.shape
    d_model = x.shape[-1]
    L = MCHUNK
    nc = t // L
    chunk = lambda b, d, c: c + d * (nc - 1 - 2 * c)
    tok = lambda b, d, c: (b, chunk(b, d, c), 0)
    tail = lambda b, d, c: (b, nc - 1 - d * c, 0)
    const = lambda b, d, c: (0, 0)
    in_specs = [pl.BlockSpec((1, L, dm), tok)] * 2 + [
        pl.BlockSpec((1, dm, L), lambda b, d, c: (b, 0, chunk(b, d, c))),
        pl.BlockSpec((1, 1, L, gc.shape[-1]), lambda b, d, c: (b, d, chunk(b, d, c), 0)),
        pl.BlockSpec((1, 1, gr.shape[2], L), lambda b, d, c: (b, d, 0, chunk(b, d, c))),
        pl.BlockSpec((1, L, dm), tail),
        pl.BlockSpec((1, L, d_model), tail),
        pl.BlockSpec((1, 3, d_model), lambda b, d, c: (mod_row(b), 0, 0)),
        pl.BlockSpec(wo_bf.shape, const, pipeline_mode=pl.Buffered(1)),
        pl.BlockSpec((1, d_model), const)]
    args = [q, k, vt, gc, gr, og, x, mod3, wo_bf, final_w.reshape(1, d_model)]
    st = lambda b, d, c: (b, d, 0, 0)
    st5 = lambda b, d, c: (b, d, 0, 0, 0)
    if init is not None:
        c0, n0, m0 = init
        in_specs += [pl.BlockSpec((1, 1, M_HEADS, M_HD, M_HD), st5),
                     pl.BlockSpec((1, 1, M_HEADS, M_HD), st),
                     pl.BlockSpec((1, 1, M_HEADS, LANES), st)]
        args += [c0, n0, jnp.broadcast_to(m0[..., None], m0.shape + (LANES,))]
    out_shape = [jax.ShapeDtypeStruct((bsz, t, d_model), F32)]
    out_specs = [pl.BlockSpec((1, L, d_model), tail)]
    if write_state:
        out_shape += [jax.ShapeDtypeStruct((bsz, 2, M_HEADS, M_HD, M_HD), F32),
                      jax.ShapeDtypeStruct((bsz, 2, M_HEADS, M_HD), F32),
                      jax.ShapeDtypeStruct((bsz, 2, M_HEADS, LANES), F32)]
        out_specs += [pl.BlockSpec((1, 1, M_HEADS, M_HD, M_HD), st5),
                      pl.BlockSpec((1, 1, M_HEADS, M_HD), st),
                      pl.BlockSpec((1, 1, M_HEADS, LANES), st)]
    return pl.pallas_call(
        functools.partial(_mlstm_scan_kernel, has_init=init is not None,
                          write_state=write_state, nc=nc),
        out_shape=tuple(out_shape),
        grid=(bsz, 2, nc),
        in_specs=in_specs,
        out_specs=tuple(out_specs),
        scratch_shapes=[pltpu.VMEM((M_HEADS, M_HD + 16, M_HD), F32),
                        pltpu.VMEM((M_HEADS, LANES), F32),
                        pltpu.VMEM((dm, L), F32),
                        pltpu.VMEM((nc, dm, L), F32)],
        compiler_params=_cparams(("parallel", "arbitrary", "arbitrary")),
        name="mlstm_scan",
    )(*args)


def _rope_tables(t):
    nf = HEAD_DIM // 4
    pos = jnp.arange(t)
    row = (pos // GRID_W).astype(F32)
    col = (pos % GRID_W).astype(F32)
    inv = ROPE_BASE ** (-jnp.arange(nf, dtype=F32) / nf)
    ar = row[:, None] * inv[None, :]
    ac = col[:, None] * inv[None, :]
    cos = jnp.concatenate([jnp.cos(ar), jnp.cos(ar), jnp.cos(ac), jnp.cos(ac)], axis=1)
    sin = jnp.concatenate([-jnp.sin(ar), jnp.sin(ar), -jnp.sin(ac), jnp.sin(ac)], axis=1)
    reps = LANES // HEAD_DIM
    return jnp.tile(cos, (1, reps)), jnp.tile(sin, (1, reps))


def kernel(x_prompt, x_sample, cache_k, cache_v, state_C, state_n, state_m, c, c_ctx,
           attn_norm_w, attn_ada_w, attn_ada_b, attn_w_in, attn_sink, attn_w_out,
           mlstm_norm_w, mlstm_ada_w, mlstm_ada_b, mlstm_w_in, mlstm_b_gates, mlstm_w_out,
           final_norm_w):
    assert attn_w_in.shape[0] == 1 and mlstm_w_in.shape[0] == 1, "one layer of each mixer"
    bsz, seq, d = x_prompt.shape
    dbsz, dseq, _ = x_sample.shape
    dkv = N_KV_HEADS * HEAD_DIM
    dm = M_HEADS * M_HD

    n_cond = 1 + dbsz
    cond = jnp.concatenate([c_ctx[None, :], c, jnp.zeros((-n_cond % 8, d), F32)], axis=0)
    attn_mod = _ada(cond, attn_ada_w[0], attn_ada_b[0]).reshape(-1, 3, d)
    mlstm_mod = _ada(cond, mlstm_ada_w[0], mlstm_ada_b[0]).reshape(-1, 3, d)
    ctx_row = lambda b: 0
    lat_row = lambda b: b + 1

    attn_w_in_bf = attn_w_in[0]
    attn_w_out_bf = attn_w_out[0]
    attn_wvt_bf = attn_w_in[0, :, 2 * N_HEADS * HEAD_DIM + dkv:].T
    mlstm_w_in_t = mlstm_w_in[0].T
    mlstm_w_out_bf = mlstm_w_out[0]

    def mlstm_layer(x, mod_row, init, write_state):
        q, k, vt, og, gc, gr = _mlstm_in(x, mlstm_mod, mod_row, mlstm_norm_w[0], mlstm_w_in_t,
                                         mlstm_b_gates[0])
        outs = _mlstm_scan(q, k, vt, gc, gr, og, x, mlstm_mod, mod_row, mlstm_w_out_bf, final_norm_w,
                           init, write_state)
        return outs[0], outs[1:]

    q, sg, k_ctx, vt_ctx, v_ctx = _attn_in(x_prompt, attn_mod, ctx_row, attn_norm_w[0], attn_w_in_bf,
                                           attn_wvt_bf, None, F32, True)
    x1 = _attn(q, sg, x_prompt, attn_mod, ctx_row, k_ctx, vt_ctx, None, None, attn_sink[0], attn_w_out_bf)
    y_prompt, (c_fin, n_fin, m_fin) = mlstm_layer(x1, ctx_row, None, True)

    q, sg, k_lat, vt_lat = _attn_in(x_sample, attn_mod, lat_row, attn_norm_w[0], attn_w_in_bf,
                                    attn_wvt_bf, _rope_tables(dseq), BF16, False)
    kc = cache_k[:, 0].reshape(dbsz, -1, dkv).astype(BF16)
    vct = jnp.swapaxes(cache_v[:, 0].reshape(dbsz, -1, dkv), 1, 2).astype(BF16)
    x1 = _attn(q, sg, x_sample, attn_mod, lat_row, kc, vct, k_lat, vt_lat, attn_sink[0], attn_w_out_bf)
    y_sample, _ = mlstm_layer(x1, lat_row, (state_C[:, 0], state_n[:, 0], state_m[:, 0]), False)

    new_cache_k = k_ctx.reshape(bsz, 1, seq, N_KV_HEADS, HEAD_DIM)
    new_cache_v = v_ctx.reshape(bsz, 1, seq, N_KV_HEADS, HEAD_DIM)
    return (y_prompt, y_sample, new_cache_k, new_cache_v,
            c_fin[:, None], n_fin[:, None], m_fin[:, None, :, :, 0])
```

```python
import functools

import jax
import jax.numpy as jnp
from jax import lax
from jax.experimental import pallas as pl
from jax.experimental.pallas import tpu as pltpu

F32 = jnp.float32
BF16 = jnp.bfloat16

HEAD_DIM = 64
N_KV_HEADS = 4
GROUP = 4
N_HEADS = N_KV_HEADS * GROUP
QBLK = 128
GRID_W = 64
ROPE_BASE = 10000.0
M_HEADS = 8
M_HD = 128
EPS = 1e-6

LANES = 128
VMEM_LIMIT = 48 * 1024 * 1024
MLSTM_IN_VMEM_LIMIT = 60 * 1024 * 1024

MCHUNK = 256
ATTN_QB = 2
SCAN_AHEAD = 4
ROW_TILE = 256

NEG_INF = float("-inf")
LOG2E = 1.4426950408889634
LN2 = 0.6931471805599453


def _cparams(sem):
    return pltpu.CompilerParams(dimension_semantics=sem, vmem_limit_bytes=VMEM_LIMIT)


def _silu(x):
    return x * jax.nn.sigmoid(x)


def _log_sigmoid(x):
    return jnp.minimum(x, 0.0) - jnp.log1p(jnp.exp(-jnp.abs(x)))


def _dot(a, b):
    return jnp.dot(a, b, preferred_element_type=F32)


def _dot_nt(a, b):
    return lax.dot_general(a, b, (((1,), (1,)), ((), ())), preferred_element_type=F32)


def _dot_tn(a, b):
    return lax.dot_general(a, b, (((0,), (0,)), ((), ())), preferred_element_type=F32)


def _split3(x):
    hi = x.astype(BF16)
    r = x - hi.astype(F32)
    mid = r.astype(BF16)
    lo = (r - mid.astype(F32)).astype(BF16)
    return hi, mid, lo


def _prenorm(x, norm_w, mod):
    ms = jnp.mean(x * x, axis=-1, keepdims=True)
    y = x * lax.rsqrt(ms + EPS) * norm_w
    return y * (1.0 + mod[1:2, :]) + mod[0:1, :]


def _ada_kernel(cond_ref, w_ref, b_ref, o_ref):
    a = _silu(cond_ref[...]).astype(BF16)
    o_ref[...] = _dot(a, w_ref[...].astype(BF16)) + b_ref[...]


def _ada(cond8, w, b):
    d, n = w.shape
    tn = 512
    return pl.pallas_call(
        _ada_kernel,
        out_shape=jax.ShapeDtypeStruct((cond8.shape[0], n), F32),
        grid=(n // tn,),
        in_specs=[pl.BlockSpec(cond8.shape, lambda j: (0, 0)),
                  pl.BlockSpec((d, tn), lambda j: (0, j)),
                  pl.BlockSpec((1, tn), lambda j: (0, j))],
        out_specs=pl.BlockSpec((cond8.shape[0], tn), lambda j: (0, j)),
        compiler_params=_cparams(("parallel",)),
        name="ada_mod",
    )(cond8, w, b.reshape(1, n))


def _rope(x, cos, sin, lane):
    first = (lane & 31) < 16
    outs = []
    for c in range(x.shape[1] // LANES):
        xc = x[:, c * LANES:(c + 1) * LANES]
        sw = jnp.where(first, pltpu.roll(xc, LANES - 16, 1), pltpu.roll(xc, 16, 1))
        outs.append(xc * cos + sw * sin)
    return jnp.concatenate(outs, axis=1)


def _attn_in_kernel(*refs, rope, emit_v):
    refs = list(refs)
    x_ref, mod_ref, nw_ref, w_ref, wvt_ref = refs[:5]
    pos = 5
    if rope:
        cos_ref, sin_ref = refs[pos:pos + 2]
        pos += 2
    q_ref, sg_ref, k_ref, vt_ref = refs[pos:pos + 4]
    dq = q_ref.shape[-1]
    dkv = k_ref.shape[-1]
    hb = _prenorm(x_ref[0], nw_ref[...], mod_ref[0]).astype(w_ref.dtype)
    q = _dot(hb, w_ref[:, 0:dq])
    g = _dot(hb, w_ref[:, dq:2 * dq])
    k = _dot(hb, w_ref[:, 2 * dq:2 * dq + dkv])
    if rope:
        cos = cos_ref[...]
        sin = sin_ref[...]
        lane = lax.broadcasted_iota(jnp.int32, cos.shape, 1)
        q = _rope(q, cos, sin, lane)
        k = _rope(k, cos, sin, lane)
    q_ref[0] = (q * (HEAD_DIM ** -0.5 * LOG2E)).astype(q_ref.dtype)
    sg_ref[0] = _silu(g).astype(sg_ref.dtype)
    k_ref[0] = k.astype(k_ref.dtype)
    vt_ref[0] = _dot_nt(wvt_ref[...], hb).astype(vt_ref.dtype)
    if emit_v:
        v_ref = refs[pos + 4]
        v_ref[0] = _dot(hb, w_ref[:, 2 * dq + dkv:2 * dq + 2 * dkv]).astype(v_ref.dtype)


def _attn_in(x, mod3, mod_row, norm_w, w_in, wv_t, rope_tabs, k_dtype, emit_v):
    bsz, t, d = x.shape
    dq = N_HEADS * HEAD_DIM
    dkv = N_KV_HEADS * HEAD_DIM
    tm = min(4 * ROW_TILE, t)
    rope = rope_tabs is not None
    tok = lambda b, i: (b, i, 0)
    const = lambda b, i: (0, 0)
    in_specs = [pl.BlockSpec((1, tm, d), tok),
                pl.BlockSpec((1, 3, d), lambda b, i: (mod_row(b), 0, 0)),
                pl.BlockSpec((1, d), const),
                pl.BlockSpec(w_in.shape, const),
                pl.BlockSpec(wv_t.shape, const)]
    args = [x, mod3, norm_w.reshape(1, d), w_in, wv_t]
    if rope:
        in_specs += [pl.BlockSpec((tm, LANES), lambda b, i: (i, 0))] * 2
        args += list(rope_tabs)
    out_shape = [jax.ShapeDtypeStruct((bsz, t, dq), BF16),
                 jax.ShapeDtypeStruct((bsz, t, dq), BF16),
                 jax.ShapeDtypeStruct((bsz, t, dkv), k_dtype),
                 jax.ShapeDtypeStruct((bsz, dkv, t), BF16)]
    out_specs = [pl.BlockSpec((1, tm, dq), tok), pl.BlockSpec((1, tm, dq), tok),
                 pl.BlockSpec((1, tm, dkv), tok),
                 pl.BlockSpec((1, dkv, tm), lambda b, i: (b, 0, i))]
    if emit_v:
        out_shape.append(jax.ShapeDtypeStruct((bsz, t, dkv), F32))
        out_specs.append(pl.BlockSpec((1, tm, dkv), tok))
    return pl.pallas_call(
        functools.partial(_attn_in_kernel, rope=rope, emit_v=emit_v),
        out_shape=tuple(out_shape),
        grid=(bsz, t // tm),
        in_specs=in_specs,
        out_specs=tuple(out_specs),
        compiler_params=_cparams(("parallel", "parallel")),
        name="attn_in_rope" if rope else "attn_in",
    )(*args)


def _attn_kernel(*refs, window, nb):
    if window:
        (q_ref, sg_ref, x_ref, mod_ref, kc_ref, vct_ref, kp_ref, km_ref, kn_ref,
         vpt_ref, vmt_ref, vnt_ref, sink_ref, wo_ref, o_ref, s_scr, p_scr, ot_scr) = refs
    else:
        q_ref, sg_ref, x_ref, mod_ref, kc_ref, vct_ref, sink_ref, wo_ref, o_ref, s_scr, p_scr, ot_scr = refs
    step = pl.program_id(1)
    nqb = q_ref.shape[1] // QBLK
    n_ctx = kc_ref.shape[1] // QBLK
    cols = GROUP * QBLK
    if window:
        kj = lax.broadcasted_iota(jnp.int32, (QBLK, cols), 0)
        qi = lax.broadcasted_iota(jnp.int32, (QBLK, cols), 1) & (QBLK - 1)
        after_diag = kj >= qi
        before_diag = kj <= qi
    ones_rows = jnp.where(lax.broadcasted_iota(jnp.int32, (16, QBLK), 0) == 0, 1.0, 0.0).astype(BF16)
    n_blk = n_ctx + (3 if window else 0)

    def window_blocks(qb, cs, kp, km, kn, lanes):
        def mid(j):
            sl = slice(j * QBLK, (j + 1) * QBLK)
            return km[0, cs, sl] if lanes else km[0, sl, cs]
        first = kp[0, cs, :] if lanes else kp[0][:, cs]
        last = kn[0, cs, :] if lanes else kn[0][:, cs]
        return [first if qb == 0 else mid(qb - 1), mid(qb), last if qb == nqb - 1 else mid(qb + 1)]

    def block_masks(qb):
        if not window:
            return [None] * n_ctx
        prev_ok = after_diag & (step > 0) if qb == 0 else after_diag
        next_ok = before_diag & (step < nb // nqb - 1) if qb == nqb - 1 else before_diag
        return [None] * n_ctx + [prev_ok, None, next_ok]

    def scores(qb, kvh):
        u = qb * N_KV_HEADS + kvh
        cs = slice(kvh * HEAD_DIM, (kvh + 1) * HEAD_DIM)
        heads = [kvh * GROUP + j for j in range(GROUP)]
        qq = q_ref[0, qb * QBLK:(qb + 1) * QBLK, :]
        q4 = jnp.concatenate([qq[:, h * HEAD_DIM:(h + 1) * HEAD_DIM] for h in heads], axis=0)
        sink_row = jnp.concatenate(
            [jnp.broadcast_to(sink_ref[0:1, h:h + 1], (1, QBLK)) for h in heads], axis=1) * LOG2E
        keys = [kc_ref[0, j * QBLK:(j + 1) * QBLK, cs].astype(BF16) for j in range(n_ctx)]
        if window:
            keys += window_blocks(qb, cs, kp_ref, km_ref, kn_ref, False)
        st_all = _dot_nt(jnp.concatenate(keys, axis=0), q4)
        macc = jnp.full((8, cols), NEG_INF, F32)
        for j, ok in enumerate(block_masks(qb)):
            s_blk = st_all[j * QBLK:(j + 1) * QBLK, :]
            if ok is not None:
                s_blk = jnp.where(ok, s_blk, NEG_INF)
            s_scr[u, j] = s_blk
            macc = jnp.maximum(macc, jnp.max(s_blk.reshape(QBLK // 8, 8, cols), axis=0))
        return jnp.maximum(jnp.max(macc, axis=0, keepdims=True), sink_row), sink_row

    def weighted_values(qb, kvh, m_row, sink_row):
        u = qb * N_KV_HEADS + kvh
        cs = slice(kvh * HEAD_DIM, (kvh + 1) * HEAD_DIM)
        for j in range(n_blk):
            p_scr[u, j * QBLK:(j + 1) * QBLK, :] = jnp.exp2(s_scr[u, j] - m_row).astype(BF16)
        vts = [vct_ref[0, cs, j * QBLK:(j + 1) * QBLK] for j in range(n_ctx)]
        if window:
            vts += window_blocks(qb, cs, vpt_ref, vmt_ref, vnt_ref, True)
        vt_ext = jnp.concatenate(
            [jnp.concatenate(vts, axis=1), jnp.tile(ones_rows, (1, n_blk))], axis=0)
        acc = _dot(vt_ext, p_scr[u])
        den = acc[HEAD_DIM:HEAD_DIM + 1, :] + jnp.exp2(sink_row - m_row)
        o_t = acc[0:HEAD_DIM, :] / den
        for j in range(GROUP):
            h = kvh * GROUP + j
            ot_scr[h * HEAD_DIM:(h + 1) * HEAD_DIM, qb * QBLK:(qb + 1) * QBLK] = o_t[:, j * QBLK:(j + 1) * QBLK]

    units = [(qb, kvh) for qb in range(nqb) for kvh in range(N_KV_HEADS)]
    stats = [scores(qb, kvh) for qb, kvh in units]
    for (qb, kvh), st in zip(units, stats):
        weighted_values(qb, kvh, *st)
    z = (ot_scr[...].T * sg_ref[0].astype(F32)).astype(wo_ref.dtype)
    y = _dot(z, wo_ref[...])
    o_ref[0] = x_ref[0] + mod_ref[0][2:3, :] * y


def _attn(q, sg, x, mod3, mod_row, kc, vct, k_lat, vt_lat, sink, w_out):
    bsz, t, d = x.shape
    dq = q.shape[-1]
    dkv = kc.shape[-1]
    p_len = kc.shape[1]
    nb = t // QBLK
    nqb = ATTN_QB
    rows = nqb * QBLK
    window = k_lat is not None
    tok = lambda b, i: (b, i, 0)
    in_specs = [pl.BlockSpec((1, rows, dq), tok),
                pl.BlockSpec((1, rows, dq), tok),
                pl.BlockSpec((1, rows, d), tok),
                pl.BlockSpec((1, 3, d), lambda b, i: (mod_row(b), 0, 0)),
                pl.BlockSpec((1, p_len, dkv), lambda b, i: (b, 0, 0)),
                pl.BlockSpec((1, dkv, p_len), lambda b, i: (b, 0, 0))]
    args = [q, sg, x, mod3, kc, vct]
    n_blocks = p_len // QBLK
    if window:
        prev = lambda i: jnp.maximum(i * nqb - 1, 0)
        nxt = lambda i: jnp.minimum((i + 1) * nqb, nb - 1)
        in_specs += [pl.BlockSpec((1, QBLK, dkv), lambda b, i: (b, prev(i), 0)),
                     pl.BlockSpec((1, rows, dkv), tok),
                     pl.BlockSpec((1, QBLK, dkv), lambda b, i: (b, nxt(i), 0)),
                     pl.BlockSpec((1, dkv, QBLK), lambda b, i: (b, 0, prev(i))),
                     pl.BlockSpec((1, dkv, rows), lambda b, i: (b, 0, i)),
                     pl.BlockSpec((1, dkv, QBLK), lambda b, i: (b, 0, nxt(i)))]
        args += [k_lat] * 3 + [vt_lat] * 3
        n_blocks += 3
    in_specs += [pl.BlockSpec((1, N_HEADS), lambda b, i: (0, 0)),
                 pl.BlockSpec(w_out.shape, lambda b, i: (0, 0))]
    args += [sink.reshape(1, N_HEADS), w_out]
    units = nqb * N_KV_HEADS
    return pl.pallas_call(
        functools.partial(_attn_kernel, window=window, nb=nb),
        out_shape=jax.ShapeDtypeStruct((bsz, t, d), F32),
        grid=(bsz, nb // nqb),
        in_specs=in_specs,
        out_specs=pl.BlockSpec((1, rows, d), tok),
        scratch_shapes=[pltpu.VMEM((units, n_blocks, QBLK, GROUP * QBLK), F32),
                        pltpu.VMEM((units, n_blocks * QBLK, GROUP * QBLK), BF16),
                        pltpu.VMEM((dq, rows), F32)],
        compiler_params=_cparams(("parallel", "parallel")),
        name="attn_window" if window else "attn_ctx",
    )(*args)


def _mlstm_in_kernel(x_ref, mod_ref, nw_ref, wt_ref, bgt_ref,
                     q_ref, k_ref, vt_ref, og_ref, gc_ref, gr_ref):
    dm = q_ref.shape[-1]
    nh = M_HEADS
    L = MCHUNK
    hb = _prenorm(x_ref[0], nw_ref[...], mod_ref[0]).astype(wt_ref.dtype)

    gr = _dot_nt(wt_ref[5 * dm:, :], hb) + bgt_ref[...]
    n_chunks = x_ref.shape[1] // L
    ri = lax.broadcasted_iota(jnp.int32, (L, L), 0)
    ci = lax.broadcasted_iota(jnp.int32, (L, L), 1)
    lane = lax.broadcasted_iota(jnp.int32, (n_chunks * nh, L), 1)
    g_rows = []
    for dr in range(2):
        before = (ri <= ci) if dr == 0 else (ri >= ci)
        tri = jnp.where(before, 1.0, 0.0).astype(BF16)
        base = dr * 2 * nh
        lf = _log_sigmoid(gr[base + nh:base + 2 * nh, :]) * LOG2E
        gi = gr[base:base + nh, :] * LOG2E
        lf_st = jnp.concatenate([lf[:, c * L:(c + 1) * L] for c in range(n_chunks)], axis=0)
        b_st = sum(_dot(piece, tri) for piece in _split3(lf_st))
        g_st = jnp.concatenate([gi[:, c * L:(c + 1) * L] for c in range(n_chunks)], axis=0) - b_st
        run = g_st
        step = 1
        while step < L:
            if dr == 0:
                run = jnp.where(lane >= step, jnp.maximum(run, pltpu.roll(run, step, 1)), run)
            else:
                run = jnp.where(lane < L - step, jnp.maximum(run, pltpu.roll(run, L - step, 1)), run)
            step *= 2
        for cidx in range(n_chunks):
            rows = slice(cidx * L, (cidx + 1) * L)
            blk = slice(cidx * nh, (cidx + 1) * nh)
            b_last = jnp.sum(lf[:, rows], axis=1, keepdims=True)
            g_max = jnp.max(g_st[blk, :], axis=1, keepdims=True)
            g_rows.append(g_st[blk, :])
            gr_ref[0, dr, :, rows] = jnp.concatenate(
                [g_st[blk, :], b_st[blk, :], jnp.broadcast_to(b_last, (nh, L)),
                 jnp.broadcast_to(g_max, (nh, L)), run[blk, :]], axis=0)
    g_sq = jnp.concatenate(g_rows + [jnp.zeros((L - len(g_rows) * nh, L), F32)], axis=0).T
    for dr in range(2):
        for cidx in range(n_chunks):
            idx = dr * n_chunks + cidx
            gc_ref[0, dr, cidx * L:(cidx + 1) * L, :] = g_sq[:, idx * nh:(idx + 1) * nh]

    o = _dot_nt(hb, wt_ref[3 * dm:4 * dm, :])
    g = _dot_nt(hb, wt_ref[4 * dm:5 * dm, :])
    og_ref[0] = (jax.nn.sigmoid(o) * _silu(g)).astype(og_ref.dtype)
    q_ref[0] = _dot_nt(hb, wt_ref[0:dm, :]).astype(q_ref.dtype)
    k_ref[0] = (_dot_nt(hb, wt_ref[dm:2 * dm, :]) * (M_HD ** -0.5)).astype(k_ref.dtype)
    vt_ref[0] = _dot_nt(wt_ref[2 * dm:3 * dm, :], hb).astype(vt_ref.dtype)


def _mlstm_in(x, mod3, mod_row, norm_w, w_t, b_gates):
    bsz, t, d = x.shape
    dm = M_HEADS * M_HD
    ng = 4 * M_HEADS
    tm = min(4 * ROW_TILE, t)
    tok = lambda b, i: (b, i, 0)
    const = lambda b, i: (0, 0)
    big = jax.ShapeDtypeStruct((bsz, t, dm), BF16)
    once = pl.Buffered(1)
    return pl.pallas_call(
        _mlstm_in_kernel,
        out_shape=(big, big, jax.ShapeDtypeStruct((bsz, dm, t), BF16), big,
                   jax.ShapeDtypeStruct((bsz, 2, t, M_HEADS), F32),
                   jax.ShapeDtypeStruct((bsz, 2, 5 * M_HEADS, t), F32)),
        grid=(bsz, t // tm),
        in_specs=[pl.BlockSpec((1, tm, d), tok),
                  pl.BlockSpec((1, 3, d), lambda b, i: (mod_row(b), 0, 0)),
                  pl.BlockSpec((1, d), const),
                  pl.BlockSpec(w_t.shape, const, pipeline_mode=once),
                  pl.BlockSpec((ng, 1), const)],
        out_specs=(pl.BlockSpec((1, tm, dm), tok), pl.BlockSpec((1, tm, dm), tok),
                   pl.BlockSpec((1, dm, tm), lambda b, i: (b, 0, i)), pl.BlockSpec((1, tm, dm), tok),
                   pl.BlockSpec((1, 2, tm, M_HEADS), lambda b, i: (b, 0, i, 0)),
                   pl.BlockSpec((1, 2, 5 * M_HEADS, tm), lambda b, i: (b, 0, 0, i))),
        compiler_params=pltpu.CompilerParams(dimension_semantics=("parallel", "parallel"),
                                             vmem_limit_bytes=MLSTM_IN_VMEM_LIMIT),
        name="mlstm_in",
    )(x, mod3, norm_w.reshape(1, d), w_t, b_gates.reshape(ng, 1))


def _mlstm_scan_kernel(*refs, drn, has_init, write_state, nc):
    refs = list(refs)
    q_ref, k_ref, vt_ref, gc_ref, gr_ref = refs[:5]
    pos = 5
    if drn == 1:
        hf_ref, og_ref, x_ref, mod_ref, wo_ref, fw_ref = refs[pos:pos + 6]
        pos += 6
    if has_init:
        c0_ref, n0_ref, m0_ref = refs[pos:pos + 3]
        pos += 3
    out_ref = refs[pos]
    pos += 1
    if write_state:
        cout_ref, nout_ref, mout_ref = refs[pos:pos + 3]
        pos += 3
    ct_scr, mscr, hcur = refs[pos:pos + 3]

    c = pl.program_id(1)
    L = q_ref.shape[1]
    nh = M_HEADS
    pad = ct_scr.shape[1] - M_HD

    @pl.when(c == 0)
    def _init():
        if has_init:
            for h in range(nh):
                ct_scr[h, 0:M_HD, :] = c0_ref[0, 0, h].T
                ct_scr[h, M_HD:M_HD + pad, :] = jnp.concatenate(
                    [n0_ref[0, 0, h:h + 1, :], jnp.zeros((pad - 1, M_HD), F32)], axis=0)
            mscr[...] = m0_ref[0, 0] * LOG2E
        else:
            ct_scr[...] = jnp.zeros(ct_scr.shape, F32)
            mscr[...] = jnp.zeros(mscr.shape, F32)

    si = lax.broadcasted_iota(jnp.int32, (L, L), 0)
    li = lax.broadcasted_iota(jnp.int32, (L, L), 1)
    seen_t = (si <= li) if drn == 0 else (si >= li)

    gcb = gc_ref[0, 0]
    grb = gr_ref[0, 0]
    q = q_ref[0]
    k = k_ref[0]
    vt = vt_ref[0]
    ones_rows = jnp.where(lax.broadcasted_iota(jnp.int32, (pad, L), 0) == 0, 1.0, 0.0).astype(BF16)

    def head_scores(h):
        hs = slice(h * M_HD, (h + 1) * M_HD)
        m_prev = mscr[h:h + 1, 0:1]
        ct = ct_scr[h]
        m_row = jnp.maximum(grb[4 * nh + h:4 * nh + h + 1, :], m_prev)
        w_t = jnp.exp2(jnp.where(seen_t, gcb[:, h:h + 1], NEG_INF) - m_row)
        r1 = _dot_nt(jnp.concatenate([k[:, hs], ct.astype(BF16)], axis=0), q[:, hs])
        s_t = (r1[0:L, :] * w_t).astype(BF16)
        return m_prev, ct, m_row, s_t, r1[L:, :]

    def head_finish(h, m_prev, ct, m_row, s_t, inter):
        hs = slice(h * M_HD, (h + 1) * M_HD)
        vext = jnp.concatenate([vt[hs, :], ones_rows], axis=0)
        g_r = grb[h:h + 1, :]
        b_r = grb[nh + h:nh + h + 1, :]
        b_last = grb[2 * nh + h:2 * nh + h + 1, 0:1]
        g_max = grb[3 * nh + h:3 * nh + h + 1, 0:1]
        w0 = jnp.exp2(m_prev - m_row)
        tot = _dot(vext, s_t) + w0 * inter
        den = tot[M_HD:M_HD + 1, :]
        floor = jnp.exp2(-(b_r + m_row))
        hcur[hs, :] = tot[0:M_HD, :] / jnp.maximum(jnp.abs(den), floor)

        m_last = jnp.maximum(g_max, m_prev)
        wk = jnp.exp2(g_r - m_last)
        decay = jnp.exp2(m_prev - m_last)
        vw = (vext.astype(F32) * wk).astype(BF16)
        ct_scr[h] = decay * ct + _dot(vw, k[:, hs])
        mscr[h:h + 1, :] = jnp.broadcast_to(b_last + m_last, (1, LANES))

    pending = [head_scores(h) for h in range(min(SCAN_AHEAD, nh))]
    for h in range(nh):
        if h + SCAN_AHEAD < nh:
            pending.append(head_scores(h + SCAN_AHEAD))
        head_finish(h, *pending.pop(0))

    if drn == 0:
        out_ref[0] = hcur[...].astype(out_ref.dtype)
    else:
        hm = (hcur[...] + hf_ref[0].astype(F32)).T * og_ref[0].astype(F32)
        y = _dot(hm.astype(wo_ref.dtype), wo_ref[...])
        x2 = x_ref[0] + mod_ref[0][2:3, :] * y
        ms = jnp.mean(x2 * x2, axis=-1, keepdims=True)
        out_ref[0] = x2 * lax.rsqrt(ms + EPS) * fw_ref[...]

    if write_state:
        @pl.when(c == nc - 1)
        def _final():
            for h in range(nh):
                cfin = ct_scr[h]
                cout_ref[0, h] = cfin[0:M_HD, :].T
                nout_ref[0, h:h + 1, :] = cfin[M_HD:M_HD + 1, :]
            mout_ref[0] = mscr[...] * LN2


def _mlstm_scan_dir(drn, q, k, vt, gc, gr, tail_args, init, write_state):
    bsz, t, dm = q.shape
    L = MCHUNK
    nc = t // L
    chunk = (lambda c: c) if drn == 0 else (lambda c: nc - 1 - c)
    tok = lambda b, c: (b, chunk(c), 0)
    feat = lambda b, c: (b, 0, chunk(c))
    const = lambda b, c: (0, 0)
    in_specs = [pl.BlockSpec((1, L, dm), tok), pl.BlockSpec((1, L, dm), tok), pl.BlockSpec((1, dm, L), feat),
                pl.BlockSpec((1, 1, L, gc.shape[-1]), lambda b, c: (b, drn, chunk(c), 0)),
                pl.BlockSpec((1, 1, gr.shape[2], L), lambda b, c: (b, drn, 0, chunk(c)))]
    args = [q, k, vt, gc, gr]
    if drn == 1:
        hf, og, x, mod3, mod_row, w_out, final_w = tail_args
        d_model = x.shape[-1]
        in_specs += [pl.BlockSpec((1, dm, L), feat),
                     pl.BlockSpec((1, L, dm), tok),
                     pl.BlockSpec((1, L, d_model), tok),
                     pl.BlockSpec((1, 3, d_model), lambda b, c: (mod_row(b), 0, 0)),
                     pl.BlockSpec(w_out.shape, const, pipeline_mode=pl.Buffered(1)),
                     pl.BlockSpec((1, d_model), const)]
        args += [hf, og, x, mod3, w_out, final_w.reshape(1, d_model)]
        out_shape = [jax.ShapeDtypeStruct((bsz, t, d_model), F32)]
        out_specs = [pl.BlockSpec((1, L, d_model), tok)]
    else:
        out_shape = [jax.ShapeDtypeStruct((bsz, dm, t), BF16)]
        out_specs = [pl.BlockSpec((1, dm, L), feat)]
    if init is not None:
        c0, n0, m0 = init
        in_specs += [pl.BlockSpec((1, 1, M_HEADS, M_HD, M_HD), lambda b, c: (b, drn, 0, 0, 0)),
                     pl.BlockSpec((1, 1, M_HEADS, M_HD), lambda b, c: (b, drn, 0, 0)),
                     pl.BlockSpec((1, 1, M_HEADS, LANES), lambda b, c: (b, drn, 0, 0))]
        args += [c0, n0, jnp.broadcast_to(m0[..., None], m0.shape + (LANES,))]
    if write_state:
        out_shape += [jax.ShapeDtypeStruct((bsz, M_HEADS, M_HD, M_HD), F32),
                      jax.ShapeDtypeStruct((bsz, M_HEADS, M_HD), F32),
                      jax.ShapeDtypeStruct((bsz, M_HEADS, LANES), F32)]
        out_specs += [pl.BlockSpec((1, M_HEADS, M_HD, M_HD), lambda b, c: (b, 0, 0, 0)),
                      pl.BlockSpec((1, M_HEADS, M_HD), lambda b, c: (b, 0, 0)),
                      pl.BlockSpec((1, M_HEADS, LANES), lambda b, c: (b, 0, 0))]
    return pl.pallas_call(
        functools.partial(_mlstm_scan_kernel, drn=drn, has_init=init is not None,
                          write_state=write_state, nc=nc),
        out_shape=tuple(out_shape),
        grid=(bsz, nc),
        in_specs=in_specs,
        out_specs=tuple(out_specs),
        scratch_shapes=[pltpu.VMEM((M_HEADS, M_HD + 16, M_HD), F32),
                        pltpu.VMEM((M_HEADS, LANES), F32),
                        pltpu.VMEM((dm, L), F32)],
        compiler_params=_cparams(("parallel", "arbitrary")),
        name="mlstm_scan_fwd" if drn == 0 else "mlstm_scan_bwd",
    )(*args)


def _mlstm_scan(q, k, vt, gc, gr, og, x, mod3, mod_row, w_out, final_w, init, write_state):
    fwd = _mlstm_scan_dir(0, q, k, vt, gc, gr, None, init, write_state)
    bwd = _mlstm_scan_dir(1, q, k, vt, gc, gr, (fwd[0], og, x, mod3, mod_row, w_out, final_w),
                          init, write_state)
    states = [jnp.stack([f, b], axis=1) for f, b in zip(fwd[1:], bwd[1:])]
    return [bwd[0]] + states


def _rope_tables(t):
    nf = HEAD_DIM // 4
    pos = jnp.arange(t)
    row = (pos // GRID_W).astype(F32)
    col = (pos % GRID_W).astype(F32)
    inv = ROPE_BASE ** (-jnp.arange(nf, dtype=F32) / nf)
    ar = row[:, None] * inv[None, :]
    ac = col[:, None] * inv[None, :]
    cos = jnp.concatenate([jnp.cos(ar), jnp.cos(ar), jnp.cos(ac), jnp.cos(ac)], axis=1)
    sin = jnp.concatenate([-jnp.sin(ar), jnp.sin(ar), -jnp.sin(ac), jnp.sin(ac)], axis=1)
    reps = LANES // HEAD_DIM
    return jnp.tile(cos, (1, reps)), jnp.tile(sin, (1, reps))


def kernel(x_prompt, x_sample, cache_k, cache_v, state_C, state_n, state_m, c, c_ctx,
           attn_norm_w, attn_ada_w, attn_ada_b, attn_w_in, attn_sink, attn_w_out,
           mlstm_norm_w, mlstm_ada_w, mlstm_ada_b, mlstm_w_in, mlstm_b_gates, mlstm_w_out,
           final_norm_w):
    assert attn_w_in.shape[0] == 1 and mlstm_w_in.shape[0] == 1, "one layer of each mixer"
    bsz, seq, d = x_prompt.shape
    dbsz, dseq, _ = x_sample.shape
    dkv = N_KV_HEADS * HEAD_DIM
    dm = M_HEADS * M_HD

    n_cond = 1 + dbsz
    cond = jnp.concatenate([c_ctx[None, :], c, jnp.zeros((-n_cond % 8, d), F32)], axis=0)
    attn_mod = _ada(cond, attn_ada_w[0], attn_ada_b[0]).reshape(-1, 3, d)
    mlstm_mod = _ada(cond, mlstm_ada_w[0], mlstm_ada_b[0]).reshape(-1, 3, d)
    ctx_row = lambda b: 0
    lat_row = lambda b: b + 1

    attn_w_in0 = attn_w_in[0]
    attn_w_out0 = attn_w_out[0]
    attn_wv_t = attn_w_in[0, :, 2 * N_HEADS * HEAD_DIM + dkv:].T
    mlstm_w_in_t = mlstm_w_in[0].T
    mlstm_w_out0 = mlstm_w_out[0]

    def mlstm_layer(x, mod_row, init, write_state):
        q, k, vt, og, gc, gr = _mlstm_in(x, mlstm_mod, mod_row, mlstm_norm_w[0], mlstm_w_in_t,
                                         mlstm_b_gates[0])
        outs = _mlstm_scan(q, k, vt, gc, gr, og, x, mlstm_mod, mod_row, mlstm_w_out0, final_norm_w,
                           init, write_state)
        return outs[0], outs[1:]

    q, sg, k_ctx, vt_ctx, v_ctx = _attn_in(x_prompt, attn_mod, ctx_row, attn_norm_w[0], attn_w_in0,
                                           attn_wv_t, None, F32, True)
    x1 = _attn(q, sg, x_prompt, attn_mod, ctx_row, k_ctx, vt_ctx, None, None, attn_sink[0], attn_w_out0)
    y_prompt, (c_fin, n_fin, m_fin) = mlstm_layer(x1, ctx_row, None, True)

    q, sg, k_lat, vt_lat = _attn_in(x_sample, attn_mod, lat_row, attn_norm_w[0], attn_w_in0,
                                    attn_wv_t, _rope_tables(dseq), BF16, False)
    kc = cache_k[:, 0].reshape(dbsz, -1, dkv).astype(BF16)
    vct = jnp.swapaxes(cache_v[:, 0].reshape(dbsz, -1, dkv), 1, 2).astype(BF16)
    x1 = _attn(q, sg, x_sample, attn_mod, lat_row, kc, vct, k_lat, vt_lat, attn_sink[0], attn_w_out0)
    y_sample, _ = mlstm_layer(x1, lat_row, (state_C[:, 0], state_n[:, 0], state_m[:, 0]), False)

    new_cache_k = k_ctx.reshape(bsz, 1, seq, N_KV_HEADS, HEAD_DIM)
    new_cache_v = v_ctx.reshape(bsz, 1, seq, N_KV_HEADS, HEAD_DIM)
    return (y_prompt, y_sample, new_cache_k, new_cache_v,
            c_fin[:, None], n_fin[:, None], m_fin[:, None, :, :, 0])
```

```python
import functools

import jax
import jax.numpy as jnp
from jax import lax
from jax.experimental import pallas as pl
from jax.experimental.pallas import tpu as pltpu

F32 = jnp.float32
BF16 = jnp.bfloat16

HEAD_DIM = 64
N_KV_HEADS = 4
GROUP = 4
N_HEADS = N_KV_HEADS * GROUP
QBLK = 128
GRID_W = 64
ROPE_BASE = 10000.0
M_HEADS = 8
M_HD = 128
EPS = 1e-6

LANES = 128
VMEM_LIMIT = 48 * 1024 * 1024
MLSTM_IN_VMEM_LIMIT = 60 * 1024 * 1024

MCHUNK = 256
ATTN_QB = 2
SCAN_SUB = 2
SCAN_AHEAD = 4
ROW_TILE = 256

NEG_INF = float("-inf")
LOG2E = 1.4426950408889634
LN2 = 0.6931471805599453


def _cparams(sem):
    return pltpu.CompilerParams(dimension_semantics=sem, vmem_limit_bytes=VMEM_LIMIT)


def _silu(x):
    return x * jax.nn.sigmoid(x)


def _log_sigmoid(x):
    return jnp.minimum(x, 0.0) - jnp.log1p(jnp.exp(-jnp.abs(x)))


def _dot(a, b):
    return jnp.dot(a, b, preferred_element_type=F32)


def _dot_nt(a, b):
    return lax.dot_general(a, b, (((1,), (1,)), ((), ())), preferred_element_type=F32)


def _dot_tn(a, b):
    return lax.dot_general(a, b, (((0,), (0,)), ((), ())), preferred_element_type=F32)


def _split3(x):
    hi = x.astype(BF16)
    r = x - hi.astype(F32)
    mid = r.astype(BF16)
    lo = (r - mid.astype(F32)).astype(BF16)
    return hi, mid, lo


def _prenorm(x, norm_w, mod):
    ms = jnp.mean(x * x, axis=-1, keepdims=True)
    y = x * lax.rsqrt(ms + EPS) * norm_w
    return y * (1.0 + mod[1:2, :]) + mod[0:1, :]


def _ada_kernel(cond_ref, w_ref, b_ref, o_ref):
    a = _silu(cond_ref[...]).astype(BF16)
    o_ref[...] = _dot(a, w_ref[...].astype(BF16)) + b_ref[...]


def _ada(cond8, w, b):
    d, n = w.shape
    tn = 512
    return pl.pallas_call(
        _ada_kernel,
        out_shape=jax.ShapeDtypeStruct((cond8.shape[0], n), F32),
        grid=(n // tn,),
        in_specs=[pl.BlockSpec(cond8.shape, lambda j: (0, 0)),
                  pl.BlockSpec((d, tn), lambda j: (0, j)),
                  pl.BlockSpec((1, tn), lambda j: (0, j))],
        out_specs=pl.BlockSpec((cond8.shape[0], tn), lambda j: (0, j)),
        compiler_params=_cparams(("parallel",)),
        name="ada_mod",
    )(cond8, w, b.reshape(1, n))


def _rope(x, cos, sin, lane):
    first = (lane & 31) < 16
    outs = []
    for c in range(x.shape[1] // LANES):
        xc = x[:, c * LANES:(c + 1) * LANES]
        sw = jnp.where(first, pltpu.roll(xc, LANES - 16, 1), pltpu.roll(xc, 16, 1))
        outs.append(xc * cos + sw * sin)
    return jnp.concatenate(outs, axis=1)


def _attn_in_kernel(*refs, rope, emit_v):
    refs = list(refs)
    x_ref, mod_ref, nw_ref, w_ref, wvt_ref = refs[:5]
    pos = 5
    if rope:
        cos_ref, sin_ref = refs[pos:pos + 2]
        pos += 2
    q_ref, sg_ref, k_ref, vt_ref = refs[pos:pos + 4]
    dq = q_ref.shape[-1]
    dkv = k_ref.shape[-1]
    hb = _prenorm(x_ref[0], nw_ref[...], mod_ref[0]).astype(w_ref.dtype)
    q = _dot(hb, w_ref[:, 0:dq])
    g = _dot(hb, w_ref[:, dq:2 * dq])
    k = _dot(hb, w_ref[:, 2 * dq:2 * dq + dkv])
    if rope:
        cos = cos_ref[...]
        sin = sin_ref[...]
        lane = lax.broadcasted_iota(jnp.int32, cos.shape, 1)
        q = _rope(q, cos, sin, lane)
        k = _rope(k, cos, sin, lane)
    q_ref[0] = (q * (HEAD_DIM ** -0.5 * LOG2E)).astype(q_ref.dtype)
    sg_ref[0] = _silu(g).astype(sg_ref.dtype)
    k_ref[0] = k.astype(k_ref.dtype)
    vt_ref[0] = _dot_nt(wvt_ref[...], hb).astype(vt_ref.dtype)
    if emit_v:
        v_ref = refs[pos + 4]
        v_ref[0] = _dot(hb, w_ref[:, 2 * dq + dkv:2 * dq + 2 * dkv]).astype(v_ref.dtype)


def _attn_in(x, mod3, mod_row, norm_w, w_in, wv_t, rope_tabs, k_dtype, emit_v):
    bsz, t, d = x.shape
    dq = N_HEADS * HEAD_DIM
    dkv = N_KV_HEADS * HEAD_DIM
    tm = min(4 * ROW_TILE, t)
    rope = rope_tabs is not None
    tok = lambda b, i: (b, i, 0)
    const = lambda b, i: (0, 0)
    in_specs = [pl.BlockSpec((1, tm, d), tok),
                pl.BlockSpec((1, 3, d), lambda b, i: (mod_row(b), 0, 0)),
                pl.BlockSpec((1, d), const),
                pl.BlockSpec(w_in.shape, const),
                pl.BlockSpec(wv_t.shape, const)]
    args = [x, mod3, norm_w.reshape(1, d), w_in, wv_t]
    if rope:
        in_specs += [pl.BlockSpec((tm, LANES), lambda b, i: (i, 0))] * 2
        args += list(rope_tabs)
    out_shape = [jax.ShapeDtypeStruct((bsz, t, dq), BF16),
                 jax.ShapeDtypeStruct((bsz, t, dq), BF16),
                 jax.ShapeDtypeStruct((bsz, t, dkv), k_dtype),
                 jax.ShapeDtypeStruct((bsz, dkv, t), BF16)]
    out_specs = [pl.BlockSpec((1, tm, dq), tok), pl.BlockSpec((1, tm, dq), tok),
                 pl.BlockSpec((1, tm, dkv), tok),
                 pl.BlockSpec((1, dkv, tm), lambda b, i: (b, 0, i))]
    if emit_v:
        out_shape.append(jax.ShapeDtypeStruct((bsz, t, dkv), F32))
        out_specs.append(pl.BlockSpec((1, tm, dkv), tok))
    return pl.pallas_call(
        functools.partial(_attn_in_kernel, rope=rope, emit_v=emit_v),
        out_shape=tuple(out_shape),
        grid=(bsz, t // tm),
        in_specs=in_specs,
        out_specs=tuple(out_specs),
        compiler_params=_cparams(("parallel", "parallel")),
        name="attn_in_rope" if rope else "attn_in",
    )(*args)


def _attn_kernel(*refs, window, nb):
    if window:
        (q_ref, sg_ref, x_ref, mod_ref, kc_ref, vct_ref, kp_ref, km_ref, kn_ref,
         vpt_ref, vmt_ref, vnt_ref, sink_ref, wo_ref, o_ref, s_scr, p_scr, ot_scr) = refs
    else:
        q_ref, sg_ref, x_ref, mod_ref, kc_ref, vct_ref, sink_ref, wo_ref, o_ref, s_scr, p_scr, ot_scr = refs
    step = pl.program_id(1)
    nqb = q_ref.shape[1] // QBLK
    n_ctx = kc_ref.shape[1] // QBLK
    cols = GROUP * QBLK
    if window:
        kj = lax.broadcasted_iota(jnp.int32, (QBLK, cols), 0)
        qi = lax.broadcasted_iota(jnp.int32, (QBLK, cols), 1) & (QBLK - 1)
        after_diag = kj >= qi
        before_diag = kj <= qi
    ones_rows = jnp.where(lax.broadcasted_iota(jnp.int32, (16, QBLK), 0) == 0, 1.0, 0.0).astype(BF16)
    n_blk = n_ctx + (3 if window else 0)

    def window_blocks(qb, cs, kp, km, kn, lanes):
        def mid(j):
            sl = slice(j * QBLK, (j + 1) * QBLK)
            return km[0, cs, sl] if lanes else km[0, sl, cs]
        first = kp[0, cs, :] if lanes else kp[0][:, cs]
        last = kn[0, cs, :] if lanes else kn[0][:, cs]
        return [first if qb == 0 else mid(qb - 1), mid(qb), last if qb == nqb - 1 else mid(qb + 1)]

    def block_masks(qb):
        if not window:
            return [None] * n_ctx
        prev_ok = after_diag & (step > 0) if qb == 0 else after_diag
        next_ok = before_diag & (step < nb // nqb - 1) if qb == nqb - 1 else before_diag
        return [None] * n_ctx + [prev_ok, None, next_ok]

    def scores(qb, kvh):
        u = qb * N_KV_HEADS + kvh
        cs = slice(kvh * HEAD_DIM, (kvh + 1) * HEAD_DIM)
        heads = [kvh * GROUP + j for j in range(GROUP)]
        qq = q_ref[0, qb * QBLK:(qb + 1) * QBLK, :]
        q4 = jnp.concatenate([qq[:, h * HEAD_DIM:(h + 1) * HEAD_DIM] for h in heads], axis=0)
        sink_row = jnp.concatenate(
            [jnp.broadcast_to(sink_ref[0:1, h:h + 1], (1, QBLK)) for h in heads], axis=1) * LOG2E
        keys = [kc_ref[0, j * QBLK:(j + 1) * QBLK, cs].astype(BF16) for j in range(n_ctx)]
        if window:
            keys += window_blocks(qb, cs, kp_ref, km_ref, kn_ref, False)
        st_all = _dot_nt(jnp.concatenate(keys, axis=0), q4)
        macc = jnp.full((8, cols), NEG_INF, F32)
        for j, ok in enumerate(block_masks(qb)):
            s_blk = st_all[j * QBLK:(j + 1) * QBLK, :]
            if ok is not None:
                s_blk = jnp.where(ok, s_blk, NEG_INF)
            s_scr[u, j] = s_blk
            macc = jnp.maximum(macc, jnp.max(s_blk.reshape(QBLK // 8, 8, cols), axis=0))
        return jnp.maximum(jnp.max(macc, axis=0, keepdims=True), sink_row), sink_row

    def weighted_values(qb, kvh, m_row, sink_row):
        u = qb * N_KV_HEADS + kvh
        cs = slice(kvh * HEAD_DIM, (kvh + 1) * HEAD_DIM)
        for j in range(n_blk):
            p_scr[u, j * QBLK:(j + 1) * QBLK, :] = jnp.exp2(s_scr[u, j] - m_row).astype(BF16)
        vts = [vct_ref[0, cs, j * QBLK:(j + 1) * QBLK] for j in range(n_ctx)]
        if window:
            vts += window_blocks(qb, cs, vpt_ref, vmt_ref, vnt_ref, True)
        vt_ext = jnp.concatenate(
            [jnp.concatenate(vts, axis=1), jnp.tile(ones_rows, (1, n_blk))], axis=0)
        acc = _dot(vt_ext, p_scr[u])
        den = acc[HEAD_DIM:HEAD_DIM + 1, :] + jnp.exp2(sink_row - m_row)
        o_t = acc[0:HEAD_DIM, :] / den
        for j in range(GROUP):
            h = kvh * GROUP + j
            ot_scr[h * HEAD_DIM:(h + 1) * HEAD_DIM, qb * QBLK:(qb + 1) * QBLK] = o_t[:, j * QBLK:(j + 1) * QBLK]

    units = [(qb, kvh) for qb in range(nqb) for kvh in range(N_KV_HEADS)]
    stats = [scores(qb, kvh) for qb, kvh in units]
    for (qb, kvh), st in zip(units, stats):
        weighted_values(qb, kvh, *st)
    z = (ot_scr[...].T * sg_ref[0].astype(F32)).astype(wo_ref.dtype)
    y = _dot(z, wo_ref[...])
    o_ref[0] = x_ref[0] + mod_ref[0][2:3, :] * y


def _attn(q, sg, x, mod3, mod_row, kc, vct, k_lat, vt_lat, sink, w_out):
    bsz, t, d = x.shape
    dq = q.shape[-1]
    dkv = kc.shape[-1]
    p_len = kc.shape[1]
    nb = t // QBLK
    nqb = ATTN_QB
    rows = nqb * QBLK
    window = k_lat is not None
    tok = lambda b, i: (b, i, 0)
    in_specs = [pl.BlockSpec((1, rows, dq), tok),
                pl.BlockSpec((1, rows, dq), tok),
                pl.BlockSpec((1, rows, d), tok),
                pl.BlockSpec((1, 3, d), lambda b, i: (mod_row(b), 0, 0)),
                pl.BlockSpec((1, p_len, dkv), lambda b, i: (b, 0, 0)),
                pl.BlockSpec((1, dkv, p_len), lambda b, i: (b, 0, 0))]
    args = [q, sg, x, mod3, kc, vct]
    n_blocks = p_len // QBLK
    if window:
        prev = lambda i: jnp.maximum(i * nqb - 1, 0)
        nxt = lambda i: jnp.minimum((i + 1) * nqb, nb - 1)
        in_specs += [pl.BlockSpec((1, QBLK, dkv), lambda b, i: (b, prev(i), 0)),
                     pl.BlockSpec((1, rows, dkv), tok),
                     pl.BlockSpec((1, QBLK, dkv), lambda b, i: (b, nxt(i), 0)),
                     pl.BlockSpec((1, dkv, QBLK), lambda b, i: (b, 0, prev(i))),
                     pl.BlockSpec((1, dkv, rows), lambda b, i: (b, 0, i)),
                     pl.BlockSpec((1, dkv, QBLK), lambda b, i: (b, 0, nxt(i)))]
        args += [k_lat] * 3 + [vt_lat] * 3
        n_blocks += 3
    in_specs += [pl.BlockSpec((1, N_HEADS), lambda b, i: (0, 0)),
                 pl.BlockSpec(w_out.shape, lambda b, i: (0, 0))]
    args += [sink.reshape(1, N_HEADS), w_out]
    units = nqb * N_KV_HEADS
    return pl.pallas_call(
        functools.partial(_attn_kernel, window=window, nb=nb),
        out_shape=jax.ShapeDtypeStruct((bsz, t, d), F32),
        grid=(bsz, nb // nqb),
        in_specs=in_specs,
        out_specs=pl.BlockSpec((1, rows, d), tok),
        scratch_shapes=[pltpu.VMEM((units, n_blocks, QBLK, GROUP * QBLK), F32),
                        pltpu.VMEM((units, n_blocks * QBLK, GROUP * QBLK), BF16),
                        pltpu.VMEM((dq, rows), F32)],
        compiler_params=_cparams(("parallel", "parallel")),
        name="attn_window" if window else "attn_ctx",
    )(*args)


def _mlstm_in_kernel(x_ref, mod_ref, nw_ref, wt_ref, bgt_ref,
                     q_ref, k_ref, vt_ref, og_ref, gc_ref, gr_ref):
    dm = q_ref.shape[-1]
    nh = M_HEADS
    L = MCHUNK
    hb = _prenorm(x_ref[0], nw_ref[...], mod_ref[0]).astype(wt_ref.dtype)

    gr = _dot_nt(wt_ref[5 * dm:, :], hb) + bgt_ref[...]
    n_chunks = x_ref.shape[1] // L
    ri = lax.broadcasted_iota(jnp.int32, (L, L), 0)
    ci = lax.broadcasted_iota(jnp.int32, (L, L), 1)
    lane = lax.broadcasted_iota(jnp.int32, (n_chunks * nh, L), 1)
    g_rows = []
    for dr in range(2):
        before = (ri <= ci) if dr == 0 else (ri >= ci)
        tri = jnp.where(before, 1.0, 0.0).astype(BF16)
        base = dr * 2 * nh
        lf = _log_sigmoid(gr[base + nh:base + 2 * nh, :]) * LOG2E
        gi = gr[base:base + nh, :] * LOG2E
        lf_st = jnp.concatenate([lf[:, c * L:(c + 1) * L] for c in range(n_chunks)], axis=0)
        b_st = sum(_dot(piece, tri) for piece in _split3(lf_st))
        g_st = jnp.concatenate([gi[:, c * L:(c + 1) * L] for c in range(n_chunks)], axis=0) - b_st
        run = g_st
        step = 1
        while step < L:
            if dr == 0:
                run = jnp.where(lane >= step, jnp.maximum(run, pltpu.roll(run, step, 1)), run)
            else:
                run = jnp.where(lane < L - step, jnp.maximum(run, pltpu.roll(run, L - step, 1)), run)
            step *= 2
        for cidx in range(n_chunks):
            rows = slice(cidx * L, (cidx + 1) * L)
            blk = slice(cidx * nh, (cidx + 1) * nh)
            b_last = jnp.sum(lf[:, rows], axis=1, keepdims=True)
            g_max = jnp.max(g_st[blk, :], axis=1, keepdims=True)
            g_rows.append(g_st[blk, :])
            gr_ref[0, dr, cidx] = jnp.concatenate(
                [g_st[blk, :], b_st[blk, :], jnp.broadcast_to(b_last, (nh, L)),
                 jnp.broadcast_to(g_max, (nh, L)), run[blk, :]], axis=0)
    g_sq = jnp.concatenate(g_rows + [jnp.zeros((L - len(g_rows) * nh, L), F32)], axis=0).T
    for dr in range(2):
        for cidx in range(n_chunks):
            idx = dr * n_chunks + cidx
            gc_ref[0, dr, cidx * L:(cidx + 1) * L, :] = g_sq[:, idx * nh:(idx + 1) * nh]

    o = _dot_nt(hb, wt_ref[3 * dm:4 * dm, :])
    g = _dot_nt(hb, wt_ref[4 * dm:5 * dm, :])
    og_ref[0] = (jax.nn.sigmoid(o) * _silu(g)).astype(og_ref.dtype)
    q_ref[0] = _dot_nt(hb, wt_ref[0:dm, :]).astype(q_ref.dtype)
    k_ref[0] = (_dot_nt(hb, wt_ref[dm:2 * dm, :]) * (M_HD ** -0.5)).astype(k_ref.dtype)
    vt = _dot_nt(wt_ref[2 * dm:3 * dm, :], hb).astype(vt_ref.dtype)
    for cidx in range(n_chunks):
        vt_ref[0, cidx] = vt[:, cidx * L:(cidx + 1) * L]


def _mlstm_in(x, mod3, mod_row, norm_w, w_t, b_gates):
    bsz, t, d = x.shape
    dm = M_HEADS * M_HD
    ng = 4 * M_HEADS
    tm = min(4 * ROW_TILE, t)
    tok = lambda b, i: (b, i, 0)
    const = lambda b, i: (0, 0)
    big = jax.ShapeDtypeStruct((bsz, t, dm), BF16)
    once = pl.Buffered(1)
    return pl.pallas_call(
        _mlstm_in_kernel,
        out_shape=(big, big, jax.ShapeDtypeStruct((bsz, t // MCHUNK, dm, MCHUNK), BF16), big,
                   jax.ShapeDtypeStruct((bsz, 2, t, M_HEADS), F32),
                   jax.ShapeDtypeStruct((bsz, 2, t // MCHUNK, 5 * M_HEADS, MCHUNK), F32)),
        grid=(bsz, t // tm),
        in_specs=[pl.BlockSpec((1, tm, d), tok),
                  pl.BlockSpec((1, 3, d), lambda b, i: (mod_row(b), 0, 0)),
                  pl.BlockSpec((1, d), const),
                  pl.BlockSpec(w_t.shape, const, pipeline_mode=once),
                  pl.BlockSpec((ng, 1), const)],
        out_specs=(pl.BlockSpec((1, tm, dm), tok), pl.BlockSpec((1, tm, dm), tok),
                   pl.BlockSpec((1, tm // MCHUNK, dm, MCHUNK), lambda b, i: (b, i, 0, 0)),
                   pl.BlockSpec((1, tm, dm), tok),
                   pl.BlockSpec((1, 2, tm, M_HEADS), lambda b, i: (b, 0, i, 0)),
                   pl.BlockSpec((1, 2, tm // MCHUNK, 5 * M_HEADS, MCHUNK), lambda b, i: (b, 0, i, 0, 0))),
        compiler_params=pltpu.CompilerParams(dimension_semantics=("parallel", "parallel"),
                                             vmem_limit_bytes=MLSTM_IN_VMEM_LIMIT),
        name="mlstm_in",
    )(x, mod3, norm_w.reshape(1, d), w_t, b_gates.reshape(ng, 1))


def _mlstm_scan_kernel(*refs, has_init, write_state, nc):
    refs = list(refs)
    q_ref, k_ref, vt_ref, gc_ref, gr_ref, og_ref, x_ref, mod_ref, wo_ref, fw_ref = refs[:10]
    pos = 10
    if has_init:
        c0_ref, n0_ref, m0_ref = refs[pos:pos + 3]
        pos += 3
    y_ref = refs[pos]
    pos += 1
    if write_state:
        cout_ref, nout_ref, mout_ref = refs[pos:pos + 3]
        pos += 3
    ct_scr, mscr, hcur, hfwd = refs[pos:pos + 4]

    drn = pl.program_id(1)
    c = pl.program_id(2)
    n_sub, L = q_ref.shape[1], q_ref.shape[2]
    nh = M_HEADS
    pad = ct_scr.shape[1] - M_HD

    @pl.when(c == 0)
    def _init():
        if has_init:
            for h in range(nh):
                ct_scr[h, 0:M_HD, :] = c0_ref[0, 0, h].T
                ct_scr[h, M_HD:M_HD + pad, :] = jnp.concatenate(
                    [n0_ref[0, 0, h:h + 1, :], jnp.zeros((pad - 1, M_HD), F32)], axis=0)
            mscr[...] = m0_ref[0, 0] * LOG2E
        else:
            ct_scr[...] = jnp.zeros(ct_scr.shape, F32)
            mscr[...] = jnp.zeros(mscr.shape, F32)

    si = lax.broadcasted_iota(jnp.int32, (L, L), 0)
    li = lax.broadcasted_iota(jnp.int32, (L, L), 1)
    seen_t = (si - li) * (1 - 2 * drn) <= 0

    ones_rows = jnp.where(lax.broadcasted_iota(jnp.int32, (pad, L), 0) == 0, 1.0, 0.0).astype(BF16)

    def chunk_step(sub):
        gcb = gc_ref[0, 0, sub]
        grb = gr_ref[0, 0, sub]
        q = q_ref[0, sub]
        k = k_ref[0, sub]
        vt = vt_ref[0, sub]

        def head_scores(h):
            hs = slice(h * M_HD, (h + 1) * M_HD)
            m_prev = mscr[h:h + 1, 0:1]
            ct = ct_scr[h]
            m_row = jnp.maximum(grb[4 * nh + h:4 * nh + h + 1, :], m_prev)
            w_t = jnp.exp2(jnp.where(seen_t, gcb[:, h:h + 1], NEG_INF) - m_row)
            r1 = _dot_nt(jnp.concatenate([k[:, hs], ct.astype(BF16)], axis=0), q[:, hs])
            s_t = (r1[0:L, :] * w_t).astype(BF16)
            return m_prev, ct, m_row, s_t, r1[L:, :]

        def head_finish(h, m_prev, ct, m_row, s_t, inter):
            hs = slice(h * M_HD, (h + 1) * M_HD)
            vext = jnp.concatenate([vt[hs, :], ones_rows], axis=0)
            g_r = grb[h:h + 1, :]
            b_r = grb[nh + h:nh + h + 1, :]
            b_last = grb[2 * nh + h:2 * nh + h + 1, 0:1]
            g_max = grb[3 * nh + h:3 * nh + h + 1, 0:1]
            w0 = jnp.exp2(m_prev - m_row)
            tot = _dot(vext, s_t) + w0 * inter
            den = tot[M_HD:M_HD + 1, :]
            floor = jnp.exp2(-(b_r + m_row))
            hcur[sub, hs, :] = tot[0:M_HD, :] / jnp.maximum(jnp.abs(den), floor)

            m_last = jnp.maximum(g_max, m_prev)
            wk = jnp.exp2(g_r - m_last)
            decay = jnp.exp2(m_prev - m_last)
            vw = (vext.astype(F32) * wk).astype(BF16)
            ct_scr[h] = decay * ct + _dot(vw, k[:, hs])
            mscr[h:h + 1, :] = jnp.broadcast_to(b_last + m_last, (1, LANES))

        pending = [head_scores(h) for h in range(min(SCAN_AHEAD, nh))]
        for h in range(nh):
            if h + SCAN_AHEAD < nh:
                pending.append(head_scores(h + SCAN_AHEAD))
            head_finish(h, *pending.pop(0))

    for j in range(n_sub):
        chunk_step(j + drn * (n_sub - 1 - 2 * j))

    @pl.when(drn == 0)
    def _park():
        hfwd[pl.ds(c * n_sub, n_sub)] = hcur[...]

    @pl.when(drn == 1)
    def _emit():
        first = (nc - 1 - c) * n_sub
        hsum_t = jnp.concatenate([hcur[j] + hfwd[first + j] for j in range(n_sub)], axis=1)
        hm = hsum_t.T * og_ref[0].astype(F32)
        y = _dot(hm.astype(wo_ref.dtype), wo_ref[...])
        x2 = x_ref[0] + mod_ref[0][2:3, :] * y
        ms = jnp.mean(x2 * x2, axis=-1, keepdims=True)
        y_ref[0] = x2 * lax.rsqrt(ms + EPS) * fw_ref[...]

    if write_state:
        @pl.when(c == nc - 1)
        def _final():
            for h in range(nh):
                cfin = ct_scr[h]
                cout_ref[0, 0, h] = cfin[0:M_HD, :].T
                nout_ref[0, 0, h:h + 1, :] = cfin[M_HD:M_HD + 1, :]
            mout_ref[0, 0] = mscr[...] * LN2


def _mlstm_scan(q, k, vt, gc, gr, og, x, mod3, mod_row, w_out, final_w, init, write_state):
    bsz, t, dm = q.shape
    d_model = x.shape[-1]
    L = MCHUNK
    n_sub = SCAN_SUB if (t // L) % SCAN_SUB == 0 else 1
    nc = t // (L * n_sub)
    rows = n_sub * L
    blk = lambda d, c: c + d * (nc - 1 - 2 * c)
    chunked = lambda b, d, c: (b, blk(d, c), 0, 0)
    gated = lambda b, d, c: (b, d, blk(d, c), 0, 0)
    tail = lambda b, d, c: (b, nc - 1 - d * c, 0)
    const = lambda b, d, c: (0, 0)
    in_specs = [pl.BlockSpec((1, n_sub, L, dm), chunked),
                pl.BlockSpec((1, n_sub, L, dm), chunked),
                pl.BlockSpec((1, n_sub, dm, L), chunked),
                pl.BlockSpec((1, 1, n_sub, L, gc.shape[-1]), gated),
                pl.BlockSpec((1, 1, n_sub, gr.shape[3], L), gated),
                pl.BlockSpec((1, rows, dm), tail),
                pl.BlockSpec((1, rows, d_model), tail),
                pl.BlockSpec((1, 3, d_model), lambda b, d, c: (mod_row(b), 0, 0)),
                pl.BlockSpec(w_out.shape, const, pipeline_mode=pl.Buffered(1)),
                pl.BlockSpec((1, d_model), const)]
    args = [q.reshape(bsz, t // L, L, dm), k.reshape(bsz, t // L, L, dm), vt,
            gc.reshape(bsz, 2, t // L, L, gc.shape[-1]), gr, og, x, mod3, w_out,
            final_w.reshape(1, d_model)]
    st = lambda b, d, c: (b, d, 0, 0)
    st5 = lambda b, d, c: (b, d, 0, 0, 0)
    if init is not None:
        c0, n0, m0 = init
        in_specs += [pl.BlockSpec((1, 1, M_HEADS, M_HD, M_HD), st5),
                     pl.BlockSpec((1, 1, M_HEADS, M_HD), st),
                     pl.BlockSpec((1, 1, M_HEADS, LANES), st)]
        args += [c0, n0, jnp.broadcast_to(m0[..., None], m0.shape + (LANES,))]
    out_shape = [jax.ShapeDtypeStruct((bsz, t, d_model), F32)]
    out_specs = [pl.BlockSpec((1, rows, d_model), tail)]
    if write_state:
        out_shape += [jax.ShapeDtypeStruct((bsz, 2, M_HEADS, M_HD, M_HD), F32),
                      jax.ShapeDtypeStruct((bsz, 2, M_HEADS, M_HD), F32),
                      jax.ShapeDtypeStruct((bsz, 2, M_HEADS, LANES), F32)]
        out_specs += [pl.BlockSpec((1, 1, M_HEADS, M_HD, M_HD), st5),
                      pl.BlockSpec((1, 1, M_HEADS, M_HD), st),
                      pl.BlockSpec((1, 1, M_HEADS, LANES), st)]
    return pl.pallas_call(
        functools.partial(_mlstm_scan_kernel, has_init=init is not None,
                          write_state=write_state, nc=nc),
        out_shape=tuple(out_shape),
        grid=(bsz, 2, nc),
        in_specs=in_specs,
        out_specs=tuple(out_specs),
        scratch_shapes=[pltpu.VMEM((M_HEADS, M_HD + 16, M_HD), F32),
                        pltpu.VMEM((M_HEADS, LANES), F32),
                        pltpu.VMEM((n_sub, dm, L), F32),
                        pltpu.VMEM((t // L, dm, L), F32)],
        compiler_params=_cparams(("parallel", "arbitrary", "arbitrary")),
        name="mlstm_scan",
    )(*args)


def _rope_tables(t):
    nf = HEAD_DIM // 4
    pos = jnp.arange(t)
    row = (pos // GRID_W).astype(F32)
    col = (pos % GRID_W).astype(F32)
    inv = ROPE_BASE ** (-jnp.arange(nf, dtype=F32) / nf)
    ar = row[:, None] * inv[None, :]
    ac = col[:, None] * inv[None, :]
    cos = jnp.concatenate([jnp.cos(ar), jnp.cos(ar), jnp.cos(ac), jnp.cos(ac)], axis=1)
    sin = jnp.concatenate([-jnp.sin(ar), jnp.sin(ar), -jnp.sin(ac), jnp.sin(ac)], axis=1)
    reps = LANES // HEAD_DIM
    return jnp.tile(cos, (1, reps)), jnp.tile(sin, (1, reps))


def kernel(x_prompt, x_sample, cache_k, cache_v, state_C, state_n, state_m, c, c_ctx,
           attn_norm_w, attn_ada_w, attn_ada_b, attn_w_in, attn_sink, attn_w_out,
           mlstm_norm_w, mlstm_ada_w, mlstm_ada_b, mlstm_w_in, mlstm_b_gates, mlstm_w_out,
           final_norm_w):
    assert attn_w_in.shape[0] == 1 and mlstm_w_in.shape[0] == 1, "one layer of each mixer"
    bsz, seq, d = x_prompt.shape
    dbsz, dseq, _ = x_sample.shape
    dkv = N_KV_HEADS * HEAD_DIM
    dm = M_HEADS * M_HD

    n_cond = 1 + dbsz
    cond = jnp.concatenate([c_ctx[None, :], c, jnp.zeros((-n_cond % 8, d), F32)], axis=0)
    attn_mod = _ada(cond, attn_ada_w[0], attn_ada_b[0]).reshape(-1, 3, d)
    mlstm_mod = _ada(cond, mlstm_ada_w[0], mlstm_ada_b[0]).reshape(-1, 3, d)
    ctx_row = lambda b: 0
    lat_row = lambda b: b + 1

    attn_w_in0 = attn_w_in[0]
    attn_w_out0 = attn_w_out[0]
    attn_wv_t = attn_w_in[0, :, 2 * N_HEADS * HEAD_DIM + dkv:].T
    mlstm_w_in_t = mlstm_w_in[0].T
    mlstm_w_out0 = mlstm_w_out[0]

    def mlstm_layer(x, mod_row, init, write_state):
        q, k, vt, og, gc, gr = _mlstm_in(x, mlstm_mod, mod_row, mlstm_norm_w[0], mlstm_w_in_t,
                                         mlstm_b_gates[0])
        outs = _mlstm_scan(q, k, vt, gc, gr, og, x, mlstm_mod, mod_row, mlstm_w_out0, final_norm_w,
                           init, write_state)
        return outs[0], outs[1:]

    q, sg, k_ctx, vt_ctx, v_ctx = _attn_in(x_prompt, attn_mod, ctx_row, attn_norm_w[0], attn_w_in0,
                                           attn_wv_t, None, F32, True)
    x1 = _attn(q, sg, x_prompt, attn_mod, ctx_row, k_ctx, vt_ctx, None, None, attn_sink[0], attn_w_out0)
    y_prompt, (c_fin, n_fin, m_fin) = mlstm_layer(x1, ctx_row, None, True)

    q, sg, k_lat, vt_lat = _attn_in(x_sample, attn_mod, lat_row, attn_norm_w[0], attn_w_in0,
                                    attn_wv_t, _rope_tables(dseq), BF16, False)
    kc = cache_k[:, 0].reshape(dbsz, -1, dkv).astype(BF16)
    vct = jnp.swapaxes(cache_v[:, 0].reshape(dbsz, -1, dkv), 1, 2).astype(BF16)
    x1 = _attn(q, sg, x_sample, attn_mod, lat_row, kc, vct, k_lat, vt_lat, attn_sink[0], attn_w_out0)
    y_sample, _ = mlstm_layer(x1, lat_row, (state_C[:, 0], state_n[:, 0], state_m[:, 0]), False)

    new_cache_k = k_ctx.reshape(bsz, 1, seq, N_KV_HEADS, HEAD_DIM)
    new_cache_v = v_ctx.reshape(bsz, 1, seq, N_KV_HEADS, HEAD_DIM)
    return (y_prompt, y_sample, new_cache_k, new_cache_v,
            c_fin[:, None], n_fin[:, None], m_fin[:, None, :, :, 0])
```

```python
import functools

import jax
import jax.numpy as jnp
from jax import lax
from jax.experimental import pallas as pl
from jax.experimental.pallas import tpu as pltpu

F32 = jnp.float32
BF16 = jnp.bfloat16

HEAD_DIM = 64
N_KV_HEADS = 4
GROUP = 4
N_HEADS = N_KV_HEADS * GROUP
QBLK = 128
GRID_W = 64
ROPE_BASE = 10000.0
M_HEADS = 8
M_HD = 128
EPS = 1e-6

LANES = 128
VMEM_LIMIT = 48 * 1024 * 1024
MLSTM_IN_VMEM_LIMIT = 60 * 1024 * 1024

MCHUNK = 256
ATTN_QB = 4
ATTN_WAVE_QB = 2
SCAN_SUB = 2
SCAN_AHEAD = 4
ROW_TILE = 256

NEG_INF = float("-inf")
LOG2E = 1.4426950408889634
LN2 = 0.6931471805599453


def _cparams(sem):
    return pltpu.CompilerParams(dimension_semantics=sem, vmem_limit_bytes=VMEM_LIMIT)


def _silu(x):
    return x * jax.nn.sigmoid(x)


def _log_sigmoid(x):
    return jnp.minimum(x, 0.0) - jnp.log1p(jnp.exp(-jnp.abs(x)))


def _dot(a, b):
    return jnp.dot(a, b, preferred_element_type=F32)


def _dot_nt(a, b):
    return lax.dot_general(a, b, (((1,), (1,)), ((), ())), preferred_element_type=F32)


def _dot_tn(a, b):
    return lax.dot_general(a, b, (((0,), (0,)), ((), ())), preferred_element_type=F32)


def _split3(x):
    hi = x.astype(BF16)
    r = x - hi.astype(F32)
    mid = r.astype(BF16)
    lo = (r - mid.astype(F32)).astype(BF16)
    return hi, mid, lo


def _prenorm(x, norm_w, mod):
    ms = jnp.mean(x * x, axis=-1, keepdims=True)
    y = x * lax.rsqrt(ms + EPS) * norm_w
    return y * (1.0 + mod[1:2, :]) + mod[0:1, :]


def _ada_kernel(cond_ref, w_ref, b_ref, o_ref):
    a = _silu(cond_ref[...]).astype(BF16)
    o_ref[...] = _dot(a, w_ref[...].astype(BF16)) + b_ref[...]


def _ada(cond8, w, b):
    d, n = w.shape
    tn = 512
    return pl.pallas_call(
        _ada_kernel,
        out_shape=jax.ShapeDtypeStruct((cond8.shape[0], n), F32),
        grid=(n // tn,),
        in_specs=[pl.BlockSpec(cond8.shape, lambda j: (0, 0)),
                  pl.BlockSpec((d, tn), lambda j: (0, j)),
                  pl.BlockSpec((1, tn), lambda j: (0, j))],
        out_specs=pl.BlockSpec((cond8.shape[0], tn), lambda j: (0, j)),
        compiler_params=_cparams(("parallel",)),
        name="ada_mod",
    )(cond8, w, b.reshape(1, n))


def _rope(x, cos, sin, lane):
    first = (lane & 31) < 16
    outs = []
    for c in range(x.shape[1] // LANES):
        xc = x[:, c * LANES:(c + 1) * LANES]
        sw = jnp.where(first, pltpu.roll(xc, LANES - 16, 1), pltpu.roll(xc, 16, 1))
        outs.append(xc * cos + sw * sin)
    return jnp.concatenate(outs, axis=1)


def _attn_in_kernel(*refs, rope, emit_v):
    refs = list(refs)
    x_ref, mod_ref, nw_ref, w_ref, wvt_ref = refs[:5]
    pos = 5
    if rope:
        cos_ref, sin_ref = refs[pos:pos + 2]
        pos += 2
    q_ref, sg_ref, k_ref, vt_ref = refs[pos:pos + 4]
    dq = q_ref.shape[-1]
    dkv = k_ref.shape[-1]
    hb = _prenorm(x_ref[0], nw_ref[...], mod_ref[0]).astype(w_ref.dtype)
    q = _dot(hb, w_ref[:, 0:dq])
    g = _dot(hb, w_ref[:, dq:2 * dq])
    k = _dot(hb, w_ref[:, 2 * dq:2 * dq + dkv])
    if rope:
        cos = cos_ref[...]
        sin = sin_ref[...]
        lane = lax.broadcasted_iota(jnp.int32, cos.shape, 1)
        q = _rope(q, cos, sin, lane)
        k = _rope(k, cos, sin, lane)
    q_ref[0] = (q * (HEAD_DIM ** -0.5 * LOG2E)).astype(q_ref.dtype)
    sg_ref[0] = _silu(g).astype(sg_ref.dtype)
    k_ref[0] = k.astype(k_ref.dtype)
    vt_ref[0] = _dot_nt(wvt_ref[...], hb).astype(vt_ref.dtype)
    if emit_v:
        v_ref = refs[pos + 4]
        v_ref[0] = _dot(hb, w_ref[:, 2 * dq + dkv:2 * dq + 2 * dkv]).astype(v_ref.dtype)


def _attn_in(x, mod3, mod_row, norm_w, w_in, wv_t, rope_tabs, k_dtype, emit_v):
    bsz, t, d = x.shape
    dq = N_HEADS * HEAD_DIM
    dkv = N_KV_HEADS * HEAD_DIM
    tm = min(4 * ROW_TILE, t)
    rope = rope_tabs is not None
    tok = lambda b, i: (b, i, 0)
    const = lambda b, i: (0, 0)
    in_specs = [pl.BlockSpec((1, tm, d), tok),
                pl.BlockSpec((1, 3, d), lambda b, i: (mod_row(b), 0, 0)),
                pl.BlockSpec((1, d), const),
                pl.BlockSpec(w_in.shape, const),
                pl.BlockSpec(wv_t.shape, const)]
    args = [x, mod3, norm_w.reshape(1, d), w_in, wv_t]
    if rope:
        in_specs += [pl.BlockSpec((tm, LANES), lambda b, i: (i, 0))] * 2
        args += list(rope_tabs)
    out_shape = [jax.ShapeDtypeStruct((bsz, t, dq), BF16),
                 jax.ShapeDtypeStruct((bsz, t, dq), BF16),
                 jax.ShapeDtypeStruct((bsz, t, dkv), k_dtype),
                 jax.ShapeDtypeStruct((bsz, dkv, t), BF16)]
    out_specs = [pl.BlockSpec((1, tm, dq), tok), pl.BlockSpec((1, tm, dq), tok),
                 pl.BlockSpec((1, tm, dkv), tok),
                 pl.BlockSpec((1, dkv, tm), lambda b, i: (b, 0, i))]
    if emit_v:
        out_shape.append(jax.ShapeDtypeStruct((bsz, t, dkv), F32))
        out_specs.append(pl.BlockSpec((1, tm, dkv), tok))
    return pl.pallas_call(
        functools.partial(_attn_in_kernel, rope=rope, emit_v=emit_v),
        out_shape=tuple(out_shape),
        grid=(bsz, t // tm),
        in_specs=in_specs,
        out_specs=tuple(out_specs),
        compiler_params=_cparams(("parallel", "parallel")),
        name="attn_in_rope" if rope else "attn_in",
    )(*args)


def _attn_kernel(*refs, window, nb):
    if window:
        (q_ref, sg_ref, x_ref, mod_ref, kc_ref, vct_ref, kp_ref, km_ref, kn_ref,
         vpt_ref, vmt_ref, vnt_ref, sink_ref, wo_ref, o_ref, s_scr, p_scr, ot_scr) = refs
    else:
        q_ref, sg_ref, x_ref, mod_ref, kc_ref, vct_ref, sink_ref, wo_ref, o_ref, s_scr, p_scr, ot_scr = refs
    step = pl.program_id(1)
    nqb = q_ref.shape[1] // QBLK
    n_ctx = kc_ref.shape[1] // QBLK
    cols = GROUP * QBLK
    if window:
        kj = lax.broadcasted_iota(jnp.int32, (QBLK, cols), 0)
        qi = lax.broadcasted_iota(jnp.int32, (QBLK, cols), 1) & (QBLK - 1)
        after_diag = kj >= qi
        before_diag = kj <= qi
    ones_rows = jnp.where(lax.broadcasted_iota(jnp.int32, (16, QBLK), 0) == 0, 1.0, 0.0).astype(BF16)
    n_blk = n_ctx + (3 if window else 0)

    def window_blocks(qb, cs, kp, km, kn, lanes):
        def mid(j):
            sl = slice(j * QBLK, (j + 1) * QBLK)
            return km[0, cs, sl] if lanes else km[0, sl, cs]
        first = kp[0, cs, :] if lanes else kp[0][:, cs]
        last = kn[0, cs, :] if lanes else kn[0][:, cs]
        return [first if qb == 0 else mid(qb - 1), mid(qb), last if qb == nqb - 1 else mid(qb + 1)]

    def block_masks(qb):
        if not window:
            return [None] * n_ctx
        prev_ok = after_diag & (step > 0) if qb == 0 else after_diag
        next_ok = before_diag & (step < nb // nqb - 1) if qb == nqb - 1 else before_diag
        return [None] * n_ctx + [prev_ok, None, next_ok]

    def scores(qb, kvh):
        u = (qb * N_KV_HEADS + kvh) % s_scr.shape[0]
        cs = slice(kvh * HEAD_DIM, (kvh + 1) * HEAD_DIM)
        heads = [kvh * GROUP + j for j in range(GROUP)]
        qq = q_ref[0, qb * QBLK:(qb + 1) * QBLK, :]
        q4 = jnp.concatenate([qq[:, h * HEAD_DIM:(h + 1) * HEAD_DIM] for h in heads], axis=0)
        sink_row = jnp.concatenate(
            [jnp.broadcast_to(sink_ref[0:1, h:h + 1], (1, QBLK)) for h in heads], axis=1) * LOG2E
        keys = [kc_ref[0, j * QBLK:(j + 1) * QBLK, cs].astype(BF16) for j in range(n_ctx)]
        if window:
            keys += window_blocks(qb, cs, kp_ref, km_ref, kn_ref, False)
        st_all = _dot_nt(jnp.concatenate(keys, axis=0), q4)
        macc = jnp.full((8, cols), NEG_INF, F32)
        for j, ok in enumerate(block_masks(qb)):
            s_blk = st_all[j * QBLK:(j + 1) * QBLK, :]
            if ok is not None:
                s_blk = jnp.where(ok, s_blk, NEG_INF)
            s_scr[u, j] = s_blk
            macc = jnp.maximum(macc, jnp.max(s_blk.reshape(QBLK // 8, 8, cols), axis=0))
        return jnp.maximum(jnp.max(macc, axis=0, keepdims=True), sink_row), sink_row

    def weighted_values(qb, kvh, m_row, sink_row):
        u = (qb * N_KV_HEADS + kvh) % s_scr.shape[0]
        cs = slice(kvh * HEAD_DIM, (kvh + 1) * HEAD_DIM)
        for j in range(n_blk):
            p_scr[u, j * QBLK:(j + 1) * QBLK, :] = jnp.exp2(s_scr[u, j] - m_row).astype(BF16)
        vts = [vct_ref[0, cs, j * QBLK:(j + 1) * QBLK] for j in range(n_ctx)]
        if window:
            vts += window_blocks(qb, cs, vpt_ref, vmt_ref, vnt_ref, True)
        vt_ext = jnp.concatenate(
            [jnp.concatenate(vts, axis=1), jnp.tile(ones_rows, (1, n_blk))], axis=0)
        acc = _dot(vt_ext, p_scr[u])
        den = acc[HEAD_DIM:HEAD_DIM + 1, :] + jnp.exp2(sink_row - m_row)
        o_t = acc[0:HEAD_DIM, :] / den
        for j in range(GROUP):
            h = kvh * GROUP + j
            ot_scr[h * HEAD_DIM:(h + 1) * HEAD_DIM, qb * QBLK:(qb + 1) * QBLK] = o_t[:, j * QBLK:(j + 1) * QBLK]

    units = [(qb, kvh) for qb in range(nqb) for kvh in range(N_KV_HEADS)]
    wave = s_scr.shape[0]
    for w0 in range(0, len(units), wave):
        stats = [scores(qb, kvh) for qb, kvh in units[w0:w0 + wave]]
        for (qb, kvh), st in zip(units[w0:w0 + wave], stats):
            weighted_values(qb, kvh, *st)
    z = (ot_scr[...].T * sg_ref[0].astype(F32)).astype(wo_ref.dtype)
    y = _dot(z, wo_ref[...])
    o_ref[0] = x_ref[0] + mod_ref[0][2:3, :] * y


def _attn(q, sg, x, mod3, mod_row, kc, vct, k_lat, vt_lat, sink, w_out):
    bsz, t, d = x.shape
    dq = q.shape[-1]
    dkv = kc.shape[-1]
    p_len = kc.shape[1]
    nb = t // QBLK
    nqb = min(ATTN_QB, nb)
    rows = nqb * QBLK
    window = k_lat is not None
    tok = lambda b, i: (b, i, 0)
    in_specs = [pl.BlockSpec((1, rows, dq), tok),
                pl.BlockSpec((1, rows, dq), tok),
                pl.BlockSpec((1, rows, d), tok),
                pl.BlockSpec((1, 3, d), lambda b, i: (mod_row(b), 0, 0)),
                pl.BlockSpec((1, p_len, dkv), lambda b, i: (b, 0, 0)),
                pl.BlockSpec((1, dkv, p_len), lambda b, i: (b, 0, 0))]
    args = [q, sg, x, mod3, kc, vct]
    n_blocks = p_len // QBLK
    if window:
        prev = lambda i: jnp.maximum(i * nqb - 1, 0)
        nxt = lambda i: jnp.minimum((i + 1) * nqb, nb - 1)
        in_specs += [pl.BlockSpec((1, QBLK, dkv), lambda b, i: (b, prev(i), 0)),
                     pl.BlockSpec((1, rows, dkv), tok),
                     pl.BlockSpec((1, QBLK, dkv), lambda b, i: (b, nxt(i), 0)),
                     pl.BlockSpec((1, dkv, QBLK), lambda b, i: (b, 0, prev(i))),
                     pl.BlockSpec((1, dkv, rows), lambda b, i: (b, 0, i)),
                     pl.BlockSpec((1, dkv, QBLK), lambda b, i: (b, 0, nxt(i)))]
        args += [k_lat] * 3 + [vt_lat] * 3
        n_blocks += 3
    in_specs += [pl.BlockSpec((1, N_HEADS), lambda b, i: (0, 0)),
                 pl.BlockSpec(w_out.shape, lambda b, i: (0, 0), pipeline_mode=pl.Buffered(1))]
    args += [sink.reshape(1, N_HEADS), w_out]
    units = min(nqb, ATTN_WAVE_QB) * N_KV_HEADS
    return pl.pallas_call(
        functools.partial(_attn_kernel, window=window, nb=nb),
        out_shape=jax.ShapeDtypeStruct((bsz, t, d), F32),
        grid=(bsz, nb // nqb),
        in_specs=in_specs,
        out_specs=pl.BlockSpec((1, rows, d), tok),
        scratch_shapes=[pltpu.VMEM((units, n_blocks, QBLK, GROUP * QBLK), F32),
                        pltpu.VMEM((units, n_blocks * QBLK, GROUP * QBLK), BF16),
                        pltpu.VMEM((dq, rows), F32)],
        compiler_params=_cparams(("parallel", "parallel")),
        name="attn_window" if window else "attn_ctx",
    )(*args)


def _mlstm_in_kernel(x_ref, mod_ref, nw_ref, wt_ref, bgt_ref,
                     q_ref, k_ref, vt_ref, og_ref, gc_ref, gr_ref):
    dm = q_ref.shape[-1]
    nh = M_HEADS
    L = MCHUNK
    hb = _prenorm(x_ref[0], nw_ref[...], mod_ref[0]).astype(wt_ref.dtype)

    gr = _dot_nt(wt_ref[5 * dm:, :], hb) + bgt_ref[...]
    n_chunks = x_ref.shape[1] // L
    ri = lax.broadcasted_iota(jnp.int32, (L, L), 0)
    ci = lax.broadcasted_iota(jnp.int32, (L, L), 1)
    lane = lax.broadcasted_iota(jnp.int32, (n_chunks * nh, L), 1)
    g_rows = []
    for dr in range(2):
        before = (ri <= ci) if dr == 0 else (ri >= ci)
        tri = jnp.where(before, 1.0, 0.0).astype(BF16)
        base = dr * 2 * nh
        lf = _log_sigmoid(gr[base + nh:base + 2 * nh, :]) * LOG2E
        gi = gr[base:base + nh, :] * LOG2E
        lf_st = jnp.concatenate([lf[:, c * L:(c + 1) * L] for c in range(n_chunks)], axis=0)
        b_st = sum(_dot(piece, tri) for piece in _split3(lf_st))
        g_st = jnp.concatenate([gi[:, c * L:(c + 1) * L] for c in range(n_chunks)], axis=0) - b_st
        run = g_st
        step = 1
        while step < L:
            if dr == 0:
                run = jnp.where(lane >= step, jnp.maximum(run, pltpu.roll(run, step, 1)), run)
            else:
                run = jnp.where(lane < L - step, jnp.maximum(run, pltpu.roll(run, L - step, 1)), run)
            step *= 2
        for cidx in range(n_chunks):
            rows = slice(cidx * L, (cidx + 1) * L)
            blk = slice(cidx * nh, (cidx + 1) * nh)
            b_last = jnp.sum(lf[:, rows], axis=1, keepdims=True)
            g_max = jnp.max(g_st[blk, :], axis=1, keepdims=True)
            g_rows.append(g_st[blk, :])
            gr_ref[0, dr, cidx] = jnp.concatenate(
                [g_st[blk, :], b_st[blk, :], jnp.broadcast_to(b_last, (nh, L)),
                 jnp.broadcast_to(g_max, (nh, L)), run[blk, :]], axis=0)
    g_sq = jnp.concatenate(g_rows + [jnp.zeros((L - len(g_rows) * nh, L), F32)], axis=0).T
    for dr in range(2):
        for cidx in range(n_chunks):
            idx = dr * n_chunks + cidx
            gc_ref[0, dr, cidx * L:(cidx + 1) * L, :] = g_sq[:, idx * nh:(idx + 1) * nh]

    o = _dot_nt(hb, wt_ref[3 * dm:4 * dm, :])
    g = _dot_nt(hb, wt_ref[4 * dm:5 * dm, :])
    og_ref[0] = (jax.nn.sigmoid(o) * _silu(g)).astype(og_ref.dtype)
    q_ref[0] = _dot_nt(hb, wt_ref[0:dm, :]).astype(q_ref.dtype)
    k_ref[0] = (_dot_nt(hb, wt_ref[dm:2 * dm, :]) * (M_HD ** -0.5)).astype(k_ref.dtype)
    vt = _dot_nt(wt_ref[2 * dm:3 * dm, :], hb).astype(vt_ref.dtype)
    for cidx in range(n_chunks):
        vt_ref[0, cidx] = vt[:, cidx * L:(cidx + 1) * L]


def _mlstm_in(x, mod3, mod_row, norm_w, w_t, b_gates):
    bsz, t, d = x.shape
    dm = M_HEADS * M_HD
    ng = 4 * M_HEADS
    tm = min(4 * ROW_TILE, t)
    tok = lambda b, i: (b, i, 0)
    const = lambda b, i: (0, 0)
    big = jax.ShapeDtypeStruct((bsz, t, dm), BF16)
    once = pl.Buffered(1)
    return pl.pallas_call(
        _mlstm_in_kernel,
        out_shape=(big, big, jax.ShapeDtypeStruct((bsz, t // MCHUNK, dm, MCHUNK), BF16), big,
                   jax.ShapeDtypeStruct((bsz, 2, t, M_HEADS), F32),
                   jax.ShapeDtypeStruct((bsz, 2, t // MCHUNK, 5 * M_HEADS, MCHUNK), F32)),
        grid=(bsz, t // tm),
        in_specs=[pl.BlockSpec((1, tm, d), tok),
                  pl.BlockSpec((1, 3, d), lambda b, i: (mod_row(b), 0, 0)),
                  pl.BlockSpec((1, d), const),
                  pl.BlockSpec(w_t.shape, const, pipeline_mode=once),
                  pl.BlockSpec((ng, 1), const)],
        out_specs=(pl.BlockSpec((1, tm, dm), tok), pl.BlockSpec((1, tm, dm), tok),
                   pl.BlockSpec((1, tm // MCHUNK, dm, MCHUNK), lambda b, i: (b, i, 0, 0)),
                   pl.BlockSpec((1, tm, dm), tok),
                   pl.BlockSpec((1, 2, tm, M_HEADS), lambda b, i: (b, 0, i, 0)),
                   pl.BlockSpec((1, 2, tm // MCHUNK, 5 * M_HEADS, MCHUNK), lambda b, i: (b, 0, i, 0, 0))),
        compiler_params=pltpu.CompilerParams(dimension_semantics=("parallel", "parallel"),
                                             vmem_limit_bytes=MLSTM_IN_VMEM_LIMIT),
        name="mlstm_in",
    )(x, mod3, norm_w.reshape(1, d), w_t, b_gates.reshape(ng, 1))


def _mlstm_scan_kernel(*refs, has_init, write_state, nc):
    refs = list(refs)
    q_ref, k_ref, vt_ref, gc_ref, gr_ref, og_ref, x_ref, mod_ref, wo_ref, fw_ref = refs[:10]
    pos = 10
    if has_init:
        c0_ref, n0_ref, m0_ref = refs[pos:pos + 3]
        pos += 3
    y_ref = refs[pos]
    pos += 1
    if write_state:
        cout_ref, nout_ref, mout_ref = refs[pos:pos + 3]
        pos += 3
    ct_scr, mscr, hcur, hfwd = refs[pos:pos + 4]

    drn = pl.program_id(1)
    c = pl.program_id(2)
    n_sub, L = q_ref.shape[1], q_ref.shape[2]
    nh = M_HEADS
    pad = ct_scr.shape[1] - M_HD

    @pl.when(c == 0)
    def _init():
        if has_init:
            for h in range(nh):
                ct_scr[h, 0:M_HD, :] = c0_ref[0, 0, h].T
                ct_scr[h, M_HD:M_HD + pad, :] = jnp.concatenate(
                    [n0_ref[0, 0, h:h + 1, :], jnp.zeros((pad - 1, M_HD), F32)], axis=0)
            mscr[...] = m0_ref[0, 0] * LOG2E
        else:
            ct_scr[...] = jnp.zeros(ct_scr.shape, F32)
            mscr[...] = jnp.zeros(mscr.shape, F32)

    si = lax.broadcasted_iota(jnp.int32, (L, L), 0)
    li = lax.broadcasted_iota(jnp.int32, (L, L), 1)
    seen_t = (si - li) * (1 - 2 * drn) <= 0

    ones_rows = jnp.where(lax.broadcasted_iota(jnp.int32, (pad, L), 0) == 0, 1.0, 0.0).astype(BF16)

    def chunk_step(sub):
        gcb = gc_ref[0, 0, sub]
        grb = gr_ref[0, 0, sub]
        q = q_ref[0, sub]
        k = k_ref[0, sub]
        vt = vt_ref[0, sub]

        def head_scores(h):
            hs = slice(h * M_HD, (h + 1) * M_HD)
            m_prev = mscr[h:h + 1, 0:1]
            ct = ct_scr[h]
            m_row = jnp.maximum(grb[4 * nh + h:4 * nh + h + 1, :], m_prev)
            w_t = jnp.exp2(jnp.where(seen_t, gcb[:, h:h + 1], NEG_INF) - m_row)
            r1 = _dot_nt(jnp.concatenate([k[:, hs], ct.astype(BF16)], axis=0), q[:, hs])
            s_t = (r1[0:L, :] * w_t).astype(BF16)
            return m_prev, ct, m_row, s_t, r1[L:, :]

        def head_finish(h, m_prev, ct, m_row, s_t, inter):
            hs = slice(h * M_HD, (h + 1) * M_HD)
            vext = jnp.concatenate([vt[hs, :], ones_rows], axis=0)
            g_r = grb[h:h + 1, :]
            b_r = grb[nh + h:nh + h + 1, :]
            b_last = grb[2 * nh + h:2 * nh + h + 1, 0:1]
            g_max = grb[3 * nh + h:3 * nh + h + 1, 0:1]
            w0 = jnp.exp2(m_prev - m_row)
            tot = _dot(vext, s_t) + w0 * inter
            den = tot[M_HD:M_HD + 1, :]
            floor = jnp.exp2(-(b_r + m_row))
            hcur[sub, hs, :] = tot[0:M_HD, :] / jnp.maximum(jnp.abs(den), floor)

            m_last = jnp.maximum(g_max, m_prev)
            wk = jnp.exp2(g_r - m_last)
            decay = jnp.exp2(m_prev - m_last)
            vw = (vext.astype(F32) * wk).astype(BF16)
            ct_scr[h] = decay * ct + _dot(vw, k[:, hs])
            mscr[h:h + 1, :] = jnp.broadcast_to(b_last + m_last, (1, LANES))

        pending = [head_scores(h) for h in range(min(SCAN_AHEAD, nh))]
        for h in range(nh):
            if h + SCAN_AHEAD < nh:
                pending.append(head_scores(h + SCAN_AHEAD))
            head_finish(h, *pending.pop(0))

    for j in range(n_sub):
        chunk_step(j + drn * (n_sub - 1 - 2 * j))

    @pl.when(drn == 0)
    def _park():
        hfwd[pl.ds(c * n_sub, n_sub)] = hcur[...]

    @pl.when(drn == 1)
    def _emit():
        first = (nc - 1 - c) * n_sub
        hsum_t = jnp.concatenate([hcur[j] + hfwd[first + j] for j in range(n_sub)], axis=1)
        hm = hsum_t.T * og_ref[0].astype(F32)
        y = _dot(hm.astype(wo_ref.dtype), wo_ref[...])
        x2 = x_ref[0] + mod_ref[0][2:3, :] * y
        ms = jnp.mean(x2 * x2, axis=-1, keepdims=True)
        y_ref[0] = x2 * lax.rsqrt(ms + EPS) * fw_ref[...]

    if write_state:
        @pl.when(c == nc - 1)
        def _final():
            for h in range(nh):
                cfin = ct_scr[h]
                cout_ref[0, 0, h] = cfin[0:M_HD, :].T
                nout_ref[0, 0, h:h + 1, :] = cfin[M_HD:M_HD + 1, :]
            mout_ref[0, 0] = mscr[...] * LN2


def _mlstm_scan(q, k, vt, gc, gr, og, x, mod3, mod_row, w_out, final_w, init, write_state):
    bsz, t, dm = q.shape
    d_model = x.shape[-1]
    L = MCHUNK
    n_sub = SCAN_SUB if (t // L) % SCAN_SUB == 0 else 1
    nc = t // (L * n_sub)
    rows = n_sub * L
    blk = lambda d, c: c + d * (nc - 1 - 2 * c)
    chunked = lambda b, d, c: (b, blk(d, c), 0, 0)
    gated = lambda b, d, c: (b, d, blk(d, c), 0, 0)
    tail = lambda b, d, c: (b, nc - 1 - d * c, 0)
    const = lambda b, d, c: (0, 0)
    in_specs = [pl.BlockSpec((1, n_sub, L, dm), chunked),
                pl.BlockSpec((1, n_sub, L, dm), chunked),
                pl.BlockSpec((1, n_sub, dm, L), chunked),
                pl.BlockSpec((1, 1, n_sub, L, gc.shape[-1]), gated),
                pl.BlockSpec((1, 1, n_sub, gr.shape[3], L), gated),
                pl.BlockSpec((1, rows, dm), tail),
                pl.BlockSpec((1, rows, d_model), tail),
                pl.BlockSpec((1, 3, d_model), lambda b, d, c: (mod_row(b), 0, 0)),
                pl.BlockSpec(w_out.shape, const, pipeline_mode=pl.Buffered(1)),
                pl.BlockSpec((1, d_model), const)]
    args = [q.reshape(bsz, t // L, L, dm), k.reshape(bsz, t // L, L, dm), vt,
            gc.reshape(bsz, 2, t // L, L, gc.shape[-1]), gr, og, x, mod3, w_out,
            final_w.reshape(1, d_model)]
    st = lambda b, d, c: (b, d, 0, 0)
    st5 = lambda b, d, c: (b, d, 0, 0, 0)
    if init is not None:
        c0, n0, m0 = init
        in_specs += [pl.BlockSpec((1, 1, M_HEADS, M_HD, M_HD), st5),
                     pl.BlockSpec((1, 1, M_HEADS, M_HD), st),
                     pl.BlockSpec((1, 1, M_HEADS, LANES), st)]
        args += [c0, n0, jnp.broadcast_to(m0[..., None], m0.shape + (LANES,))]
    out_shape = [jax.ShapeDtypeStruct((bsz, t, d_model), F32)]
    out_specs = [pl.BlockSpec((1, rows, d_model), tail)]
    if write_state:
        out_shape += [jax.ShapeDtypeStruct((bsz, 2, M_HEADS, M_HD, M_HD), F32),
                      jax.ShapeDtypeStruct((bsz, 2, M_HEADS, M_HD), F32),
                      jax.ShapeDtypeStruct((bsz, 2, M_HEADS, LANES), F32)]
        out_specs += [pl.BlockSpec((1, 1, M_HEADS, M_HD, M_HD), st5),
                      pl.BlockSpec((1, 1, M_HEADS, M_HD), st),
                      pl.BlockSpec((1, 1, M_HEADS, LANES), st)]
    return pl.pallas_call(
        functools.partial(_mlstm_scan_kernel, has_init=init is not None,
                          write_state=write_state, nc=nc),
        out_shape=tuple(out_shape),
        grid=(bsz, 2, nc),
        in_specs=in_specs,
        out_specs=tuple(out_specs),
        scratch_shapes=[pltpu.VMEM((M_HEADS, M_HD + 16, M_HD), F32),
                        pltpu.VMEM((M_HEADS, LANES), F32),
                        pltpu.VMEM((n_sub, dm, L), F32),
                        pltpu.VMEM((t // L, dm, L), F32)],
        compiler_params=_cparams(("parallel", "arbitrary", "arbitrary")),
        name="mlstm_scan",
    )(*args)


def _rope_tables(t):
    nf = HEAD_DIM // 4
    pos = jnp.arange(t)
    row = (pos // GRID_W).astype(F32)
    col = (pos % GRID_W).astype(F32)
    inv = ROPE_BASE ** (-jnp.arange(nf, dtype=F32) / nf)
    ar = row[:, None] * inv[None, :]
    ac = col[:, None] * inv[None, :]
    cos = jnp.concatenate([jnp.cos(ar), jnp.cos(ar), jnp.cos(ac), jnp.cos(ac)], axis=1)
    sin = jnp.concatenate([-jnp.sin(ar), jnp.sin(ar), -jnp.sin(ac), jnp.sin(ac)], axis=1)
    reps = LANES // HEAD_DIM
    return jnp.tile(cos, (1, reps)), jnp.tile(sin, (1, reps))


def kernel(x_prompt, x_sample, cache_k, cache_v, state_C, state_n, state_m, c, c_ctx,
           attn_norm_w, attn_ada_w, attn_ada_b, attn_w_in, attn_sink, attn_w_out,
           mlstm_norm_w, mlstm_ada_w, mlstm_ada_b, mlstm_w_in, mlstm_b_gates, mlstm_w_out,
           final_norm_w):
    assert attn_w_in.shape[0] == 1 and mlstm_w_in.shape[0] == 1, "one layer of each mixer"
    bsz, seq, d = x_prompt.shape
    dbsz, dseq, _ = x_sample.shape
    dkv = N_KV_HEADS * HEAD_DIM
    dm = M_HEADS * M_HD

    n_cond = 1 + dbsz
    cond = jnp.concatenate([c_ctx[None, :], c, jnp.zeros((-n_cond % 8, d), F32)], axis=0)
    attn_mod = _ada(cond, attn_ada_w[0], attn_ada_b[0]).reshape(-1, 3, d)
    mlstm_mod = _ada(cond, mlstm_ada_w[0], mlstm_ada_b[0]).reshape(-1, 3, d)
    ctx_row = lambda b: 0
    lat_row = lambda b: b + 1

    attn_w_in0 = attn_w_in[0]
    attn_w_out0 = attn_w_out[0]
    attn_wv_t = attn_w_in[0, :, 2 * N_HEADS * HEAD_DIM + dkv:].T
    mlstm_w_in_t = mlstm_w_in[0].T
    mlstm_w_out0 = mlstm_w_out[0]

    def mlstm_layer(x, mod_row, init, write_state):
        q, k, vt, og, gc, gr = _mlstm_in(x, mlstm_mod, mod_row, mlstm_norm_w[0], mlstm_w_in_t,
                                         mlstm_b_gates[0])
        outs = _mlstm_scan(q, k, vt, gc, gr, og, x, mlstm_mod, mod_row, mlstm_w_out0, final_norm_w,
                           init, write_state)
        return outs[0], outs[1:]

    q, sg, k_ctx, vt_ctx, v_ctx = _attn_in(x_prompt, attn_mod, ctx_row, attn_norm_w[0], attn_w_in0,
                                           attn_wv_t, None, F32, True)
    x1 = _attn(q, sg, x_prompt, attn_mod, ctx_row, k_ctx, vt_ctx, None, None, attn_sink[0], attn_w_out0)
    y_prompt, (c_fin, n_fin, m_fin) = mlstm_layer(x1, ctx_row, None, True)

    q, sg, k_lat, vt_lat = _attn_in(x_sample, attn_mod, lat_row, attn_norm_w[0], attn_w_in0,
                                    attn_wv_t, _rope_tables(dseq), BF16, False)
    kc = cache_k[:, 0].reshape(dbsz, -1, dkv).astype(BF16)
    vct = jnp.swapaxes(cache_v[:, 0].reshape(dbsz, -1, dkv), 1, 2).astype(BF16)
    x1 = _attn(q, sg, x_sample, attn_mod, lat_row, kc, vct, k_lat, vt_lat, attn_sink[0], attn_w_out0)
    y_sample, _ = mlstm_layer(x1, lat_row, (state_C[:, 0], state_n[:, 0], state_m[:, 0]), False)

    new_cache_k = k_ctx.reshape(bsz, 1, seq, N_KV_HEADS, HEAD_DIM)
    new_cache_v = v_ctx.reshape(bsz, 1, seq, N_KV_HEADS, HEAD_DIM)
    return (y_prompt, y_sample, new_cache_k, new_cache_v,
            c_fin[:, None], n_fin[:, None], m_fin[:, None, :, :, 0])
```

```python
import functools

import jax
import jax.numpy as jnp
from jax import lax
from jax.experimental import pallas as pl
from jax.experimental.pallas import tpu as pltpu

F32 = jnp.float32
BF16 = jnp.bfloat16

HEAD_DIM = 64
N_KV_HEADS = 4
GROUP = 4
N_HEADS = N_KV_HEADS * GROUP
QBLK = 128
GRID_W = 64
ROPE_BASE = 10000.0
M_HEADS = 8
M_HD = 128
EPS = 1e-6

LANES = 128
VMEM_LIMIT = 48 * 1024 * 1024
MLSTM_IN_VMEM_LIMIT = 60 * 1024 * 1024

MCHUNK = 256
ATTN_QB = 4
ATTN_WAVE_QB = 2
SCAN_SUB = 2
SCAN_AHEAD = 4
ROW_TILE = 256

NEG_INF = float("-inf")
LOG2E = 1.4426950408889634
LN2 = 0.6931471805599453


def _cparams(sem):
    return pltpu.CompilerParams(dimension_semantics=sem, vmem_limit_bytes=VMEM_LIMIT)


def _silu(x):
    return x * jax.nn.sigmoid(x)


def _log_sigmoid(x):
    return jnp.minimum(x, 0.0) - jnp.log1p(jnp.exp(-jnp.abs(x)))


def _dot(a, b):
    return jnp.dot(a, b, preferred_element_type=F32)


def _dot_nt(a, b):
    return lax.dot_general(a, b, (((1,), (1,)), ((), ())), preferred_element_type=F32)


def _dot_tn(a, b):
    return lax.dot_general(a, b, (((0,), (0,)), ((), ())), preferred_element_type=F32)


def _split3(x):
    hi = x.astype(BF16)
    r = x - hi.astype(F32)
    mid = r.astype(BF16)
    lo = (r - mid.astype(F32)).astype(BF16)
    return hi, mid, lo


def _prenorm(x, norm_w, mod):
    ms = jnp.mean(x * x, axis=-1, keepdims=True)
    y = x * lax.rsqrt(ms + EPS) * norm_w
    return y * (1.0 + mod[1:2, :]) + mod[0:1, :]


def _ada_kernel(cond_ref, w_ref, b_ref, o_ref):
    a = _silu(cond_ref[...]).astype(BF16)
    o_ref[...] = _dot(a, w_ref[...].astype(BF16)) + b_ref[...]


def _ada(cond8, w, b):
    d, n = w.shape
    tn = 512
    return pl.pallas_call(
        _ada_kernel,
        out_shape=jax.ShapeDtypeStruct((cond8.shape[0], n), F32),
        grid=(n // tn,),
        in_specs=[pl.BlockSpec(cond8.shape, lambda j: (0, 0)),
                  pl.BlockSpec((d, tn), lambda j: (0, j)),
                  pl.BlockSpec((1, tn), lambda j: (0, j))],
        out_specs=pl.BlockSpec((cond8.shape[0], tn), lambda j: (0, j)),
        compiler_params=_cparams(("parallel",)),
        name="ada_mod",
    )(cond8, w, b.reshape(1, n))


def _rope(x, cos, sin, lane):
    first = (lane & 31) < 16
    outs = []
    for c in range(x.shape[1] // LANES):
        xc = x[:, c * LANES:(c + 1) * LANES]
        sw = jnp.where(first, pltpu.roll(xc, LANES - 16, 1), pltpu.roll(xc, 16, 1))
        outs.append(xc * cos + sw * sin)
    return jnp.concatenate(outs, axis=1)


def _attn_in_kernel(*refs, rope, emit_v):
    refs = list(refs)
    x_ref, mod_ref, nw_ref, w_ref, wvt_ref = refs[:5]
    pos = 5
    if rope:
        cos_ref, sin_ref = refs[pos:pos + 2]
        pos += 2
    q_ref, sg_ref, k_ref, vt_ref = refs[pos:pos + 4]
    dq = q_ref.shape[-1]
    dkv = k_ref.shape[-1]
    hb = _prenorm(x_ref[0], nw_ref[...], mod_ref[0]).astype(w_ref.dtype)
    q = _dot(hb, w_ref[:, 0:dq])
    g = _dot(hb, w_ref[:, dq:2 * dq])
    k = _dot(hb, w_ref[:, 2 * dq:2 * dq + dkv])
    if rope:
        cos = cos_ref[...]
        sin = sin_ref[...]
        lane = lax.broadcasted_iota(jnp.int32, cos.shape, 1)
        q = _rope(q, cos, sin, lane)
        k = _rope(k, cos, sin, lane)
    q_ref[0] = (q * (HEAD_DIM ** -0.5 * LOG2E)).astype(q_ref.dtype)
    sg_ref[0] = _silu(g).astype(sg_ref.dtype)
    k_ref[0] = k.astype(k_ref.dtype)
    vt_ref[0] = _dot_nt(wvt_ref[...], hb).astype(vt_ref.dtype)
    if emit_v:
        v_ref = refs[pos + 4]
        v_ref[0] = _dot(hb, w_ref[:, 2 * dq + dkv:2 * dq + 2 * dkv]).astype(v_ref.dtype)


def _attn_in(x, mod3, mod_row, norm_w, w_in, wv_t, rope_tabs, k_dtype, emit_v):
    bsz, t, d = x.shape
    dq = N_HEADS * HEAD_DIM
    dkv = N_KV_HEADS * HEAD_DIM
    tm = min(4 * ROW_TILE, t)
    rope = rope_tabs is not None
    tok = lambda b, i: (b, i, 0)
    const = lambda b, i: (0, 0)
    in_specs = [pl.BlockSpec((1, tm, d), tok),
                pl.BlockSpec((1, 3, d), lambda b, i: (mod_row(b), 0, 0)),
                pl.BlockSpec((1, d), const),
                pl.BlockSpec(w_in.shape, const),
                pl.BlockSpec(wv_t.shape, const)]
    args = [x, mod3, norm_w.reshape(1, d), w_in, wv_t]
    if rope:
        in_specs += [pl.BlockSpec((tm, LANES), lambda b, i: (i, 0))] * 2
        args += list(rope_tabs)
    out_shape = [jax.ShapeDtypeStruct((bsz, t, dq), BF16),
                 jax.ShapeDtypeStruct((bsz, t, dq), BF16),
                 jax.ShapeDtypeStruct((bsz, t, dkv), k_dtype),
                 jax.ShapeDtypeStruct((bsz, dkv, t), BF16)]
    out_specs = [pl.BlockSpec((1, tm, dq), tok), pl.BlockSpec((1, tm, dq), tok),
                 pl.BlockSpec((1, tm, dkv), tok),
                 pl.BlockSpec((1, dkv, tm), lambda b, i: (b, 0, i))]
    if emit_v:
        out_shape.append(jax.ShapeDtypeStruct((bsz, t, dkv), F32))
        out_specs.append(pl.BlockSpec((1, tm, dkv), tok))
    return pl.pallas_call(
        functools.partial(_attn_in_kernel, rope=rope, emit_v=emit_v),
        out_shape=tuple(out_shape),
        grid=(bsz, t // tm),
        in_specs=in_specs,
        out_specs=tuple(out_specs),
        compiler_params=_cparams(("parallel", "parallel")),
        name="attn_in_rope" if rope else "attn_in",
    )(*args)


def _attn_kernel(*refs, window, nb):
    if window:
        (q_ref, sg_ref, x_ref, mod_ref, kc_ref, vct_ref, kp_ref, km_ref, kn_ref,
         vpt_ref, vmt_ref, vnt_ref, sink_ref, wo_ref, o_ref, s_scr, p_scr, ot_scr) = refs
    else:
        q_ref, sg_ref, x_ref, mod_ref, kc_ref, vct_ref, sink_ref, wo_ref, o_ref, s_scr, p_scr, ot_scr = refs
    step = pl.program_id(1)
    nqb = q_ref.shape[1] // QBLK
    n_ctx = kc_ref.shape[1] // QBLK
    cols = GROUP * QBLK
    if window:
        kj = lax.broadcasted_iota(jnp.int32, (QBLK, cols), 0)
        qi = lax.broadcasted_iota(jnp.int32, (QBLK, cols), 1) & (QBLK - 1)
        after_diag = kj >= qi
        before_diag = kj <= qi
    ones_rows = jnp.where(lax.broadcasted_iota(jnp.int32, (16, QBLK), 0) == 0, 1.0, 0.0).astype(BF16)
    n_blk = n_ctx + (3 if window else 0)

    def window_blocks(qb, cs, kp, km, kn, lanes):
        def mid(j):
            sl = slice(j * QBLK, (j + 1) * QBLK)
            return km[0, cs, sl] if lanes else km[0, sl, cs]
        first = kp[0, cs, :] if lanes else kp[0][:, cs]
        last = kn[0, cs, :] if lanes else kn[0][:, cs]
        return [first if qb == 0 else mid(qb - 1), mid(qb), last if qb == nqb - 1 else mid(qb + 1)]

    def block_masks(qb):
        if not window:
            return [None] * n_ctx
        prev_ok = after_diag & (step > 0) if qb == 0 else after_diag
        next_ok = before_diag & (step < nb // nqb - 1) if qb == nqb - 1 else before_diag
        return [None] * n_ctx + [prev_ok, None, next_ok]

    def scores(qb, kvh):
        u = (qb * N_KV_HEADS + kvh) % s_scr.shape[0]
        cs = slice(kvh * HEAD_DIM, (kvh + 1) * HEAD_DIM)
        heads = [kvh * GROUP + j for j in range(GROUP)]
        qq = q_ref[0, qb * QBLK:(qb + 1) * QBLK, :]
        q4 = jnp.concatenate([qq[:, h * HEAD_DIM:(h + 1) * HEAD_DIM] for h in heads], axis=0)
        sink_row = jnp.concatenate(
            [jnp.broadcast_to(sink_ref[0:1, h:h + 1], (1, QBLK)) for h in heads], axis=1) * LOG2E
        keys = [kc_ref[0, j * QBLK:(j + 1) * QBLK, cs].astype(BF16) for j in range(n_ctx)]
        if window:
            keys += window_blocks(qb, cs, kp_ref, km_ref, kn_ref, False)
        st_all = _dot_nt(jnp.concatenate(keys, axis=0), q4)
        macc = jnp.full((8, cols), NEG_INF, F32)
        for j, ok in enumerate(block_masks(qb)):
            s_blk = st_all[j * QBLK:(j + 1) * QBLK, :]
            if ok is not None:
                s_blk = jnp.where(ok, s_blk, NEG_INF)
            s_scr[u, j] = s_blk
            macc = jnp.maximum(macc, jnp.max(s_blk.reshape(QBLK // 8, 8, cols), axis=0))
        return jnp.maximum(jnp.max(macc, axis=0, keepdims=True), sink_row), sink_row

    def weighted_values(qb, kvh, m_row, sink_row):
        u = (qb * N_KV_HEADS + kvh) % s_scr.shape[0]
        cs = slice(kvh * HEAD_DIM, (kvh + 1) * HEAD_DIM)
        for j in range(n_blk):
            p_scr[u, j * QBLK:(j + 1) * QBLK, :] = jnp.exp2(s_scr[u, j] - m_row).astype(BF16)
        vts = [vct_ref[0, cs, j * QBLK:(j + 1) * QBLK] for j in range(n_ctx)]
        if window:
            vts += window_blocks(qb, cs, vpt_ref, vmt_ref, vnt_ref, True)
        vt_ext = jnp.concatenate(
            [jnp.concatenate(vts, axis=1), jnp.tile(ones_rows, (1, n_blk))], axis=0)
        acc = _dot(vt_ext, p_scr[u])
        den = acc[HEAD_DIM:HEAD_DIM + 1, :] + jnp.exp2(sink_row - m_row)
        o_t = acc[0:HEAD_DIM, :] / den
        for j in range(GROUP):
            h = kvh * GROUP + j
            ot_scr[h * HEAD_DIM:(h + 1) * HEAD_DIM, qb * QBLK:(qb + 1) * QBLK] = o_t[:, j * QBLK:(j + 1) * QBLK]

    units = [(qb, kvh) for qb in range(nqb) for kvh in range(N_KV_HEADS)]
    wave = s_scr.shape[0]
    for w0 in range(0, len(units), wave):
        stats = [scores(qb, kvh) for qb, kvh in units[w0:w0 + wave]]
        for (qb, kvh), st in zip(units[w0:w0 + wave], stats):
            weighted_values(qb, kvh, *st)
    z = (ot_scr[...].T * sg_ref[0].astype(F32)).astype(wo_ref.dtype)
    y = _dot(z, wo_ref[...])
    o_ref[0] = x_ref[0] + mod_ref[0][2:3, :] * y


def _attn(q, sg, x, mod3, mod_row, kc, vct, k_lat, vt_lat, sink, w_out):
    bsz, t, d = x.shape
    dq = q.shape[-1]
    dkv = kc.shape[-1]
    p_len = kc.shape[1]
    nb = t // QBLK
    nqb = min(ATTN_QB, nb)
    rows = nqb * QBLK
    window = k_lat is not None
    tok = lambda b, i: (b, i, 0)
    in_specs = [pl.BlockSpec((1, rows, dq), tok),
                pl.BlockSpec((1, rows, dq), tok),
                pl.BlockSpec((1, rows, d), tok),
                pl.BlockSpec((1, 3, d), lambda b, i: (mod_row(b), 0, 0)),
                pl.BlockSpec((1, p_len, dkv), lambda b, i: (b, 0, 0)),
                pl.BlockSpec((1, dkv, p_len), (lambda b, i: (b, 0, 0)) if vct.shape[0] == bsz
                             else (lambda b, i: (0, 0, b)))]
    args = [q, sg, x, mod3, kc, vct]
    n_blocks = p_len // QBLK
    if window:
        prev = lambda i: jnp.maximum(i * nqb - 1, 0)
        nxt = lambda i: jnp.minimum((i + 1) * nqb, nb - 1)
        in_specs += [pl.BlockSpec((1, QBLK, dkv), lambda b, i: (b, prev(i), 0)),
                     pl.BlockSpec((1, rows, dkv), tok),
                     pl.BlockSpec((1, QBLK, dkv), lambda b, i: (b, nxt(i), 0)),
                     pl.BlockSpec((1, dkv, QBLK), lambda b, i: (b, 0, prev(i))),
                     pl.BlockSpec((1, dkv, rows), lambda b, i: (b, 0, i)),
                     pl.BlockSpec((1, dkv, QBLK), lambda b, i: (b, 0, nxt(i)))]
        args += [k_lat] * 3 + [vt_lat] * 3
        n_blocks += 3
    in_specs += [pl.BlockSpec((1, N_HEADS), lambda b, i: (0, 0)),
                 pl.BlockSpec(w_out.shape, lambda b, i: (0, 0), pipeline_mode=pl.Buffered(1))]
    args += [sink.reshape(1, N_HEADS), w_out]
    units = min(nqb, ATTN_WAVE_QB) * N_KV_HEADS
    return pl.pallas_call(
        functools.partial(_attn_kernel, window=window, nb=nb),
        out_shape=jax.ShapeDtypeStruct((bsz, t, d), F32),
        grid=(bsz, nb // nqb),
        in_specs=in_specs,
        out_specs=pl.BlockSpec((1, rows, d), tok),
        scratch_shapes=[pltpu.VMEM((units, n_blocks, QBLK, GROUP * QBLK), F32),
                        pltpu.VMEM((units, n_blocks * QBLK, GROUP * QBLK), BF16),
                        pltpu.VMEM((dq, rows), F32)],
        compiler_params=_cparams(("parallel", "parallel")),
        name="attn_window" if window else "attn_ctx",
    )(*args)


def _mlstm_in_kernel(x_ref, mod_ref, nw_ref, wt_ref, bgt_ref,
                     q_ref, k_ref, vt_ref, og_ref, gc_ref, gr_ref):
    dm = q_ref.shape[-1]
    nh = M_HEADS
    L = MCHUNK
    hb = _prenorm(x_ref[0], nw_ref[...], mod_ref[0]).astype(wt_ref.dtype)

    gr = _dot_nt(wt_ref[5 * dm:, :], hb) + bgt_ref[...]
    n_chunks = x_ref.shape[1] // L
    ri = lax.broadcasted_iota(jnp.int32, (L, L), 0)
    ci = lax.broadcasted_iota(jnp.int32, (L, L), 1)
    lane = lax.broadcasted_iota(jnp.int32, (n_chunks * nh, L), 1)
    g_rows = []
    for dr in range(2):
        before = (ri <= ci) if dr == 0 else (ri >= ci)
        tri = jnp.where(before, 1.0, 0.0).astype(BF16)
        base = dr * 2 * nh
        lf = _log_sigmoid(gr[base + nh:base + 2 * nh, :]) * LOG2E
        gi = gr[base:base + nh, :] * LOG2E
        lf_st = jnp.concatenate([lf[:, c * L:(c + 1) * L] for c in range(n_chunks)], axis=0)
        b_st = sum(_dot(piece, tri) for piece in _split3(lf_st))
        g_st = jnp.concatenate([gi[:, c * L:(c + 1) * L] for c in range(n_chunks)], axis=0) - b_st
        run = g_st
        step = 1
        while step < L:
            if dr == 0:
                run = jnp.where(lane >= step, jnp.maximum(run, pltpu.roll(run, step, 1)), run)
            else:
                run = jnp.where(lane < L - step, jnp.maximum(run, pltpu.roll(run, L - step, 1)), run)
            step *= 2
        for cidx in range(n_chunks):
            rows = slice(cidx * L, (cidx + 1) * L)
            blk = slice(cidx * nh, (cidx + 1) * nh)
            b_last = jnp.sum(lf[:, rows], axis=1, keepdims=True)
            g_max = jnp.max(g_st[blk, :], axis=1, keepdims=True)
            g_rows.append(g_st[blk, :])
            gr_ref[0, dr, cidx] = jnp.concatenate(
                [g_st[blk, :], b_st[blk, :], jnp.broadcast_to(b_last, (nh, L)),
                 jnp.broadcast_to(g_max, (nh, L)), run[blk, :]], axis=0)
    g_sq = jnp.concatenate(g_rows + [jnp.zeros((L - len(g_rows) * nh, L), F32)], axis=0).T
    for dr in range(2):
        for cidx in range(n_chunks):
            idx = dr * n_chunks + cidx
            gc_ref[0, dr, cidx * L:(cidx + 1) * L, :] = g_sq[:, idx * nh:(idx + 1) * nh]

    o = _dot_nt(hb, wt_ref[3 * dm:4 * dm, :])
    g = _dot_nt(hb, wt_ref[4 * dm:5 * dm, :])
    og_ref[0] = (jax.nn.sigmoid(o) * _silu(g)).astype(og_ref.dtype)
    q_ref[0] = _dot_nt(hb, wt_ref[0:dm, :]).astype(q_ref.dtype)
    k_ref[0] = (_dot_nt(hb, wt_ref[dm:2 * dm, :]) * (M_HD ** -0.5)).astype(k_ref.dtype)
    vt = _dot_nt(wt_ref[2 * dm:3 * dm, :], hb).astype(vt_ref.dtype)
    for cidx in range(n_chunks):
        vt_ref[0, cidx] = vt[:, cidx * L:(cidx + 1) * L]


def _mlstm_in(x, mod3, mod_row, norm_w, w_t, b_gates):
    bsz, t, d = x.shape
    dm = M_HEADS * M_HD
    ng = 4 * M_HEADS
    tm = min(4 * ROW_TILE, t)
    tok = lambda b, i: (b, i, 0)
    const = lambda b, i: (0, 0)
    big = jax.ShapeDtypeStruct((bsz, t, dm), BF16)
    once = pl.Buffered(1)
    return pl.pallas_call(
        _mlstm_in_kernel,
        out_shape=(big, big, jax.ShapeDtypeStruct((bsz, t // MCHUNK, dm, MCHUNK), BF16), big,
                   jax.ShapeDtypeStruct((bsz, 2, t, M_HEADS), F32),
                   jax.ShapeDtypeStruct((bsz, 2, t // MCHUNK, 5 * M_HEADS, MCHUNK), F32)),
        grid=(bsz, t // tm),
        in_specs=[pl.BlockSpec((1, tm, d), tok),
                  pl.BlockSpec((1, 3, d), lambda b, i: (mod_row(b), 0, 0)),
                  pl.BlockSpec((1, d), const),
                  pl.BlockSpec(w_t.shape, const, pipeline_mode=once),
                  pl.BlockSpec((ng, 1), const)],
        out_specs=(pl.BlockSpec((1, tm, dm), tok), pl.BlockSpec((1, tm, dm), tok),
                   pl.BlockSpec((1, tm // MCHUNK, dm, MCHUNK), lambda b, i: (b, i, 0, 0)),
                   pl.BlockSpec((1, tm, dm), tok),
                   pl.BlockSpec((1, 2, tm, M_HEADS), lambda b, i: (b, 0, i, 0)),
                   pl.BlockSpec((1, 2, tm // MCHUNK, 5 * M_HEADS, MCHUNK), lambda b, i: (b, 0, i, 0, 0))),
        compiler_params=pltpu.CompilerParams(dimension_semantics=("parallel", "parallel"),
                                             vmem_limit_bytes=MLSTM_IN_VMEM_LIMIT),
        name="mlstm_in",
    )(x, mod3, norm_w.reshape(1, d), w_t, b_gates.reshape(ng, 1))


def _mlstm_scan_kernel(*refs, has_init, write_state, nc):
    refs = list(refs)
    q_ref, k_ref, vt_ref, gc_ref, gr_ref, og_ref, x_ref, mod_ref, wo_ref, fw_ref = refs[:10]
    pos = 10
    if has_init:
        c0_ref, n0_ref, m0_ref = refs[pos:pos + 3]
        pos += 3
    y_ref = refs[pos]
    pos += 1
    if write_state:
        cout_ref, nout_ref, mout_ref = refs[pos:pos + 3]
        pos += 3
    ct_scr, mscr, hcur, hfwd = refs[pos:pos + 4]

    drn = pl.program_id(1)
    c = pl.program_id(2)
    n_sub, L = q_ref.shape[1], q_ref.shape[2]
    nh = M_HEADS
    pad = ct_scr.shape[1] - M_HD

    @pl.when(c == 0)
    def _init():
        if has_init:
            for h in range(nh):
                ct_scr[h, 0:M_HD, :] = c0_ref[0, 0, h].T
                ct_scr[h, M_HD:M_HD + pad, :] = jnp.concatenate(
                    [n0_ref[0, 0, h:h + 1, :], jnp.zeros((pad - 1, M_HD), F32)], axis=0)
            mscr[...] = m0_ref[0, 0] * LOG2E
        else:
            ct_scr[...] = jnp.zeros(ct_scr.shape, F32)
            mscr[...] = jnp.zeros(mscr.shape, F32)

    si = lax.broadcasted_iota(jnp.int32, (L, L), 0)
    li = lax.broadcasted_iota(jnp.int32, (L, L), 1)
    seen_t = (si - li) * (1 - 2 * drn) <= 0

    ones_rows = jnp.where(lax.broadcasted_iota(jnp.int32, (pad, L), 0) == 0, 1.0, 0.0).astype(BF16)

    def chunk_step(sub):
        gcb = gc_ref[0, 0, sub]
        grb = gr_ref[0, 0, sub]
        q = q_ref[0, sub]
        k = k_ref[0, sub]
        vt = vt_ref[0, sub]

        def head_scores(h):
            hs = slice(h * M_HD, (h + 1) * M_HD)
            m_prev = mscr[h:h + 1, 0:1]
            ct = ct_scr[h]
            m_row = jnp.maximum(grb[4 * nh + h:4 * nh + h + 1, :], m_prev)
            w_t = jnp.exp2(jnp.where(seen_t, gcb[:, h:h + 1], NEG_INF) - m_row)
            r1 = _dot_nt(jnp.concatenate([k[:, hs], ct.astype(BF16)], axis=0), q[:, hs])
            s_t = (r1[0:L, :] * w_t).astype(BF16)
            return m_prev, ct, m_row, s_t, r1[L:, :]

        def head_finish(h, m_prev, ct, m_row, s_t, inter):
            hs = slice(h * M_HD, (h + 1) * M_HD)
            vext = jnp.concatenate([vt[hs, :], ones_rows], axis=0)
            g_r = grb[h:h + 1, :]
            b_r = grb[nh + h:nh + h + 1, :]
            b_last = grb[2 * nh + h:2 * nh + h + 1, 0:1]
            g_max = grb[3 * nh + h:3 * nh + h + 1, 0:1]
            w0 = jnp.exp2(m_prev - m_row)
            tot = _dot(vext, s_t) + w0 * inter
            den = tot[M_HD:M_HD + 1, :]
            floor = jnp.exp2(-(b_r + m_row))
            hcur[sub, hs, :] = tot[0:M_HD, :] / jnp.maximum(jnp.abs(den), floor)

            m_last = jnp.maximum(g_max, m_prev)
            wk = jnp.exp2(g_r - m_last)
            decay = jnp.exp2(m_prev - m_last)
            vw = (vext.astype(F32) * wk).astype(BF16)
            ct_scr[h] = decay * ct + _dot(vw, k[:, hs])
            mscr[h:h + 1, :] = jnp.broadcast_to(b_last + m_last, (1, LANES))

        pending = [head_scores(h) for h in range(min(SCAN_AHEAD, nh))]
        for h in range(nh):
            if h + SCAN_AHEAD < nh:
                pending.append(head_scores(h + SCAN_AHEAD))
            head_finish(h, *pending.pop(0))

    for j in range(n_sub):
        chunk_step(j + drn * (n_sub - 1 - 2 * j))

    @pl.when(drn == 0)
    def _park():
        hfwd[pl.ds(c * n_sub, n_sub)] = hcur[...]

    @pl.when(drn == 1)
    def _emit():
        first = (nc - 1 - c) * n_sub
        hsum_t = jnp.concatenate([hcur[j] + hfwd[first + j] for j in range(n_sub)], axis=1)
        hm = hsum_t.T * og_ref[0].astype(F32)
        y = _dot(hm.astype(wo_ref.dtype), wo_ref[...])
        x2 = x_ref[0] + mod_ref[0][2:3, :] * y
        ms = jnp.mean(x2 * x2, axis=-1, keepdims=True)
        y_ref[0] = x2 * lax.rsqrt(ms + EPS) * fw_ref[...]

    if write_state:
        @pl.when(c == nc - 1)
        def _final():
            for h in range(nh):
                cfin = ct_scr[h]
                cout_ref[0, 0, h] = cfin[0:M_HD, :].T
                nout_ref[0, 0, h:h + 1, :] = cfin[M_HD:M_HD + 1, :]
            mout_ref[0, 0] = mscr[...] * LN2


def _mlstm_scan(q, k, vt, gc, gr, og, x, mod3, mod_row, w_out, final_w, init, write_state):
    bsz, t, dm = q.shape
    d_model = x.shape[-1]
    L = MCHUNK
    n_sub = SCAN_SUB if (t // L) % SCAN_SUB == 0 else 1
    nc = t // (L * n_sub)
    rows = n_sub * L
    blk = lambda d, c: c + d * (nc - 1 - 2 * c)
    chunked = lambda b, d, c: (b, blk(d, c), 0, 0)
    if gc.shape[0] == bsz:
        gated = lambda b, d, c: (b, d, blk(d, c), 0, 0)
    else:
        gated = lambda b, d, c: (0, d, b * nc + blk(d, c), 0, 0)
    tail = lambda b, d, c: (b, nc - 1 - d * c, 0)
    const = lambda b, d, c: (0, 0)
    in_specs = [pl.BlockSpec((1, n_sub, L, dm), chunked),
                pl.BlockSpec((1, n_sub, L, dm), chunked),
                pl.BlockSpec((1, n_sub, dm, L), chunked),
                pl.BlockSpec((1, 1, n_sub, L, gc.shape[-1]), gated),
                pl.BlockSpec((1, 1, n_sub, gr.shape[3], L), gated),
                pl.BlockSpec((1, rows, dm), tail),
                pl.BlockSpec((1, rows, d_model), tail),
                pl.BlockSpec((1, 3, d_model), lambda b, d, c: (mod_row(b), 0, 0)),
                pl.BlockSpec(w_out.shape, const, pipeline_mode=pl.Buffered(1)),
                pl.BlockSpec((1, d_model), const)]
    args = [q.reshape(bsz, t // L, L, dm), k.reshape(bsz, t // L, L, dm), vt.reshape(bsz, t // L, dm, L),
            gc.reshape(gc.shape[0], 2, -1, L, gc.shape[-1]), gr, og, x, mod3, w_out,
            final_w.reshape(1, d_model)]
    st = lambda b, d, c: (b, d, 0, 0)
    st5 = lambda b, d, c: (b, d, 0, 0, 0)
    if init is not None:
        c0, n0, m0 = init
        in_specs += [pl.BlockSpec((1, 1, M_HEADS, M_HD, M_HD), st5),
                     pl.BlockSpec((1, 1, M_HEADS, M_HD), st),
                     pl.BlockSpec((1, 1, M_HEADS, LANES), st)]
        args += [c0, n0, jnp.broadcast_to(m0[..., None], m0.shape + (LANES,))]
    out_shape = [jax.ShapeDtypeStruct((bsz, t, d_model), F32)]
    out_specs = [pl.BlockSpec((1, rows, d_model), tail)]
    if write_state:
        out_shape += [jax.ShapeDtypeStruct((bsz, 2, M_HEADS, M_HD, M_HD), F32),
                      jax.ShapeDtypeStruct((bsz, 2, M_HEADS, M_HD), F32),
                      jax.ShapeDtypeStruct((bsz, 2, M_HEADS, LANES), F32)]
        out_specs += [pl.BlockSpec((1, 1, M_HEADS, M_HD, M_HD), st5),
                      pl.BlockSpec((1, 1, M_HEADS, M_HD), st),
                      pl.BlockSpec((1, 1, M_HEADS, LANES), st)]
    return pl.pallas_call(
        functools.partial(_mlstm_scan_kernel, has_init=init is not None,
                          write_state=write_state, nc=nc),
        out_shape=tuple(out_shape),
        grid=(bsz, 2, nc),
        in_specs=in_specs,
        out_specs=tuple(out_specs),
        scratch_shapes=[pltpu.VMEM((M_HEADS, M_HD + 16, M_HD), F32),
                        pltpu.VMEM((M_HEADS, LANES), F32),
                        pltpu.VMEM((n_sub, dm, L), F32),
                        pltpu.VMEM((t // L, dm, L), F32)],
        compiler_params=_cparams(("parallel", "arbitrary", "arbitrary")),
        name="mlstm_scan",
    )(*args)


def _rope_tables(t):
    nf = HEAD_DIM // 4
    pos = jnp.arange(t)
    row = (pos // GRID_W).astype(F32)
    col = (pos % GRID_W).astype(F32)
    inv = ROPE_BASE ** (-jnp.arange(nf, dtype=F32) / nf)
    ar = row[:, None] * inv[None, :]
    ac = col[:, None] * inv[None, :]
    cos = jnp.concatenate([jnp.cos(ar), jnp.cos(ar), jnp.cos(ac), jnp.cos(ac)], axis=1)
    sin = jnp.concatenate([-jnp.sin(ar), jnp.sin(ar), -jnp.sin(ac), jnp.sin(ac)], axis=1)
    reps = LANES // HEAD_DIM
    return jnp.tile(cos, (1, reps)), jnp.tile(sin, (1, reps))


def kernel(x_prompt, x_sample, cache_k, cache_v, state_C, state_n, state_m, c, c_ctx,
           attn_norm_w, attn_ada_w, attn_ada_b, attn_w_in, attn_sink, attn_w_out,
           mlstm_norm_w, mlstm_ada_w, mlstm_ada_b, mlstm_w_in, mlstm_b_gates, mlstm_w_out,
           final_norm_w):
    assert attn_w_in.shape[0] == 1 and mlstm_w_in.shape[0] == 1, "one layer of each mixer"
    bsz, seq, d = x_prompt.shape
    dbsz, dseq, _ = x_sample.shape
    dkv = N_KV_HEADS * HEAD_DIM
    dm = M_HEADS * M_HD

    n_cond = 1 + dbsz
    cond = jnp.concatenate([c_ctx[None, :], c, jnp.zeros((-n_cond % 8, d), F32)], axis=0)
    attn_mod = _ada(cond, attn_ada_w[0], attn_ada_b[0]).reshape(-1, 3, d)
    mlstm_mod = _ada(cond, mlstm_ada_w[0], mlstm_ada_b[0]).reshape(-1, 3, d)
    ctx_row = lambda b: 0
    lat_row = lambda b: b + 1

    attn_w_in0 = attn_w_in[0]
    attn_w_out0 = attn_w_out[0]
    attn_wv_t = attn_w_in[0, :, 2 * N_HEADS * HEAD_DIM + dkv:].T
    mlstm_w_in_t = mlstm_w_in[0].T
    mlstm_w_out0 = mlstm_w_out[0]

    def mlstm_layer(x, mod_row, init, write_state, shared_cond):
        b_x, t_x, _ = x.shape
        assert t_x % MCHUNK == 0
        xin = x.reshape(1, b_x * t_x, d) if shared_cond else x
        q, k, vt, og, gc, gr = _mlstm_in(xin, mlstm_mod, mod_row, mlstm_norm_w[0], mlstm_w_in_t,
                                         mlstm_b_gates[0])
        tokens = lambda a: a.reshape(b_x, t_x, a.shape[-1])
        outs = _mlstm_scan(tokens(q), tokens(k), vt, gc, gr, tokens(og), x, mlstm_mod, mod_row,
                           mlstm_w_out0, final_norm_w, init, write_state)
        return outs[0], outs[1:]

    tokens = lambda a: a.reshape(bsz, seq, a.shape[-1])
    q, sg, k_ctx, vt_ctx, v_ctx = _attn_in(x_prompt.reshape(1, bsz * seq, d), attn_mod, ctx_row,
                                           attn_norm_w[0], attn_w_in0, attn_wv_t, None, F32, True)
    k_ctx, v_ctx = tokens(k_ctx), tokens(v_ctx)
    x1 = _attn(tokens(q), tokens(sg), x_prompt, attn_mod, ctx_row, k_ctx, vt_ctx, None, None,
               attn_sink[0], attn_w_out0)
    y_prompt, (c_fin, n_fin, m_fin) = mlstm_layer(x1, ctx_row, None, True, True)

    q, sg, k_lat, vt_lat = _attn_in(x_sample, attn_mod, lat_row, attn_norm_w[0], attn_w_in0,
                                    attn_wv_t, _rope_tables(dseq), BF16, False)
    kc = cache_k[:, 0].reshape(dbsz, -1, dkv).astype(BF16)
    vct = jnp.swapaxes(cache_v[:, 0].reshape(dbsz, -1, dkv), 1, 2).astype(BF16)
    x1 = _attn(q, sg, x_sample, attn_mod, lat_row, kc, vct, k_lat, vt_lat, attn_sink[0], attn_w_out0)
    y_sample, _ = mlstm_layer(x1, lat_row, (state_C[:, 0], state_n[:, 0], state_m[:, 0]), False, False)

    new_cache_k = k_ctx.reshape(bsz, 1, seq, N_KV_HEADS, HEAD_DIM)
    new_cache_v = v_ctx.reshape(bsz, 1, seq, N_KV_HEADS, HEAD_DIM)
    return (y_prompt, y_sample, new_cache_k, new_cache_v,
            c_fin[:, None], n_fin[:, None], m_fin[:, None, :, :, 0])
```

```python
import functools

import jax
import jax.numpy as jnp
from jax import lax
from jax.experimental import pallas as pl
from jax.experimental.pallas import tpu as pltpu

F32 = jnp.float32
BF16 = jnp.bfloat16

HEAD_DIM = 64
N_KV_HEADS = 4
GROUP = 4
N_HEADS = N_KV_HEADS * GROUP
QBLK = 128
GRID_W = 64
ROPE_BASE = 10000.0
M_HEADS = 8
M_HD = 128
EPS = 1e-6

LANES = 128
BF16_ROWS = 16
VMEM_LIMIT = 48 * 1024 * 1024
MLSTM_IN_VMEM_LIMIT = 60 * 1024 * 1024

MCHUNK = 256
ATTN_QB = 4
ATTN_WAVE_QB = 2
SCAN_SUB = 2
SCAN_AHEAD = 4
PROJ_ROWS = 1024
ADA_TILE = 512

NEG_INF = float("-inf")
LOG2E = 1.4426950408889634
LN2 = 0.6931471805599453


def _cparams(sem):
    return pltpu.CompilerParams(dimension_semantics=sem, vmem_limit_bytes=VMEM_LIMIT)


def _silu(x):
    return x * jax.nn.sigmoid(x)


def _log_sigmoid(x):
    return jnp.minimum(x, 0.0) - jnp.log1p(jnp.exp(-jnp.abs(x)))


def _dot(a, b):
    return jnp.dot(a, b, preferred_element_type=F32)


def _dot_nt(a, b):
    return lax.dot_general(a, b, (((1,), (1,)), ((), ())), preferred_element_type=F32)


def _split3(x):
    hi = x.astype(BF16)
    r = x - hi.astype(F32)
    mid = r.astype(BF16)
    lo = (r - mid.astype(F32)).astype(BF16)
    return hi, mid, lo


def _prenorm(x, norm_w, mod):
    ms = jnp.mean(x * x, axis=-1, keepdims=True)
    y = x * lax.rsqrt(ms + EPS) * norm_w
    return y * (1.0 + mod[1:2, :]) + mod[0:1, :]


def _ada_kernel(cond_ref, w_ref, b_ref, o_ref):
    a = _silu(cond_ref[...]).astype(BF16)
    o_ref[...] = _dot(a, w_ref[...].astype(BF16)) + b_ref[...]


def _ada(cond8, w, b):
    d, n = w.shape
    tn = ADA_TILE
    return pl.pallas_call(
        _ada_kernel,
        out_shape=jax.ShapeDtypeStruct((cond8.shape[0], n), F32),
        grid=(n // tn,),
        in_specs=[pl.BlockSpec(cond8.shape, lambda j: (0, 0)),
                  pl.BlockSpec((d, tn), lambda j: (0, j)),
                  pl.BlockSpec((1, tn), lambda j: (0, j))],
        out_specs=pl.BlockSpec((cond8.shape[0], tn), lambda j: (0, j)),
        compiler_params=_cparams(("parallel",)),
        name="ada_mod",
    )(cond8, w, b.reshape(1, n))


def _rope(x, cos, sin, lane):
    first = (lane & 31) < 16
    outs = []
    for c in range(x.shape[1] // LANES):
        xc = x[:, c * LANES:(c + 1) * LANES]
        sw = jnp.where(first, pltpu.roll(xc, LANES - 16, 1), pltpu.roll(xc, 16, 1))
        outs.append(xc * cos + sw * sin)
    return jnp.concatenate(outs, axis=1)


def _attn_in_kernel(*refs, rope, emit_v):
    refs = list(refs)
    x_ref, mod_ref, nw_ref, w_ref, wvt_ref = refs[:5]
    pos = 5
    if rope:
        cos_ref, sin_ref = refs[pos:pos + 2]
        pos += 2
    q_ref, sg_ref, k_ref, vt_ref = refs[pos:pos + 4]
    dq = q_ref.shape[-1]
    dkv = k_ref.shape[-1]
    hb = _prenorm(x_ref[0], nw_ref[...], mod_ref[0]).astype(w_ref.dtype)
    q = _dot(hb, w_ref[:, 0:dq])
    g = _dot(hb, w_ref[:, dq:2 * dq])
    k = _dot(hb, w_ref[:, 2 * dq:2 * dq + dkv])
    if rope:
        cos = cos_ref[...]
        sin = sin_ref[...]
        lane = lax.broadcasted_iota(jnp.int32, cos.shape, 1)
        q = _rope(q, cos, sin, lane)
        k = _rope(k, cos, sin, lane)
    q_ref[0] = (q * (HEAD_DIM ** -0.5 * LOG2E)).astype(q_ref.dtype)
    sg_ref[0] = _silu(g).astype(sg_ref.dtype)
    k_ref[0] = k.astype(k_ref.dtype)
    vt_ref[0] = _dot_nt(wvt_ref[...], hb).astype(vt_ref.dtype)
    if emit_v:
        v_ref = refs[pos + 4]
        v_ref[0] = _dot(hb, w_ref[:, 2 * dq + dkv:2 * dq + 2 * dkv]).astype(v_ref.dtype)


def _attn_in(x, mod3, mod_row, norm_w, w_in, wv_t, rope_tabs, k_dtype, emit_v):
    bsz, t, d = x.shape
    dq = N_HEADS * HEAD_DIM
    dkv = N_KV_HEADS * HEAD_DIM
    tm = min(PROJ_ROWS, t)
    rope = rope_tabs is not None
    tok = lambda b, i: (b, i, 0)
    const = lambda b, i: (0, 0)
    in_specs = [pl.BlockSpec((1, tm, d), tok),
                pl.BlockSpec((1, 3, d), lambda b, i: (mod_row(b), 0, 0)),
                pl.BlockSpec((1, d), const),
                pl.BlockSpec(w_in.shape, const),
                pl.BlockSpec(wv_t.shape, const)]
    args = [x, mod3, norm_w.reshape(1, d), w_in, wv_t]
    if rope:
        in_specs += [pl.BlockSpec((tm, LANES), lambda b, i: (i, 0))] * 2
        args += list(rope_tabs)
    out_shape = [jax.ShapeDtypeStruct((bsz, t, dq), BF16),
                 jax.ShapeDtypeStruct((bsz, t, dq), BF16),
                 jax.ShapeDtypeStruct((bsz, t, dkv), k_dtype),
                 jax.ShapeDtypeStruct((bsz, dkv, t), BF16)]
    out_specs = [pl.BlockSpec((1, tm, dq), tok), pl.BlockSpec((1, tm, dq), tok),
                 pl.BlockSpec((1, tm, dkv), tok),
                 pl.BlockSpec((1, dkv, tm), lambda b, i: (b, 0, i))]
    if emit_v:
        out_shape.append(jax.ShapeDtypeStruct((bsz, t, dkv), F32))
        out_specs.append(pl.BlockSpec((1, tm, dkv), tok))
    return pl.pallas_call(
        functools.partial(_attn_in_kernel, rope=rope, emit_v=emit_v),
        out_shape=tuple(out_shape),
        grid=(bsz, t // tm),
        in_specs=in_specs,
        out_specs=tuple(out_specs),
        compiler_params=_cparams(("parallel", "parallel")),
        name="attn_in_rope" if rope else "attn_in",
    )(*args)


def _attn_kernel(*refs, window, nb):
    if window:
        (q_ref, sg_ref, x_ref, mod_ref, kc_ref, vct_ref, kp_ref, km_ref, kn_ref,
         vpt_ref, vmt_ref, vnt_ref, sink_ref, wo_ref, o_ref, s_scr, p_scr, ot_scr) = refs
    else:
        q_ref, sg_ref, x_ref, mod_ref, kc_ref, vct_ref, sink_ref, wo_ref, o_ref, s_scr, p_scr, ot_scr = refs
    step = pl.program_id(1)
    nqb = q_ref.shape[1] // QBLK
    n_ctx = kc_ref.shape[1] // QBLK
    cols = GROUP * QBLK
    if window:
        kj = lax.broadcasted_iota(jnp.int32, (QBLK, cols), 0)
        qi = lax.broadcasted_iota(jnp.int32, (QBLK, cols), 1) & (QBLK - 1)
        after_diag = kj >= qi
        before_diag = kj <= qi
    ones_rows = jnp.where(lax.broadcasted_iota(jnp.int32, (BF16_ROWS, QBLK), 0) == 0, 1.0, 0.0).astype(BF16)
    n_blk = n_ctx + (3 if window else 0)

    def window_blocks(qb, cs, kp, km, kn, lanes):
        def mid(j):
            sl = slice(j * QBLK, (j + 1) * QBLK)
            return km[0, cs, sl] if lanes else km[0, sl, cs]
        first = kp[0, cs, :] if lanes else kp[0][:, cs]
        last = kn[0, cs, :] if lanes else kn[0][:, cs]
        return [first if qb == 0 else mid(qb - 1), mid(qb), last if qb == nqb - 1 else mid(qb + 1)]

    def block_masks(qb):
        if not window:
            return [None] * n_ctx
        prev_ok = after_diag & (step > 0) if qb == 0 else after_diag
        next_ok = before_diag & (step < nb // nqb - 1) if qb == nqb - 1 else before_diag
        return [None] * n_ctx + [prev_ok, None, next_ok]

    def scores(qb, kvh):
        u = (qb * N_KV_HEADS + kvh) % s_scr.shape[0]
        cs = slice(kvh * HEAD_DIM, (kvh + 1) * HEAD_DIM)
        heads = [kvh * GROUP + j for j in range(GROUP)]
        qq = q_ref[0, qb * QBLK:(qb + 1) * QBLK, :]
        q4 = jnp.concatenate([qq[:, h * HEAD_DIM:(h + 1) * HEAD_DIM] for h in heads], axis=0)
        sink_row = jnp.concatenate(
            [jnp.broadcast_to(sink_ref[0:1, h:h + 1], (1, QBLK)) for h in heads], axis=1) * LOG2E
        keys = [kc_ref[0, j * QBLK:(j + 1) * QBLK, cs].astype(BF16) for j in range(n_ctx)]
        if window:
            keys += window_blocks(qb, cs, kp_ref, km_ref, kn_ref, False)
        st_all = _dot_nt(jnp.concatenate(keys, axis=0), q4)
        macc = jnp.full((8, cols), NEG_INF, F32)
        for j, ok in enumerate(block_masks(qb)):
            s_blk = st_all[j * QBLK:(j + 1) * QBLK, :]
            if ok is not None:
                s_blk = jnp.where(ok, s_blk, NEG_INF)
            s_scr[u, j] = s_blk
            macc = jnp.maximum(macc, jnp.max(s_blk.reshape(QBLK // 8, 8, cols), axis=0))
        return jnp.maximum(jnp.max(macc, axis=0, keepdims=True), sink_row), sink_row

    def weighted_values(qb, kvh, m_row, sink_row):
        u = (qb * N_KV_HEADS + kvh) % s_scr.shape[0]
        cs = slice(kvh * HEAD_DIM, (kvh + 1) * HEAD_DIM)
        for j in range(n_blk):
            p_scr[u, j * QBLK:(j + 1) * QBLK, :] = jnp.exp2(s_scr[u, j] - m_row).astype(BF16)
        vts = [vct_ref[0, cs, j * QBLK:(j + 1) * QBLK] for j in range(n_ctx)]
        if window:
            vts += window_blocks(qb, cs, vpt_ref, vmt_ref, vnt_ref, True)
        vt_ext = jnp.concatenate(
            [jnp.concatenate(vts, axis=1), jnp.tile(ones_rows, (1, n_blk))], axis=0)
        acc = _dot(vt_ext, p_scr[u])
        den = acc[HEAD_DIM:HEAD_DIM + 1, :] + jnp.exp2(sink_row - m_row)
        o_t = acc[0:HEAD_DIM, :] / den
        for j in range(GROUP):
            h = kvh * GROUP + j
            ot_scr[h * HEAD_DIM:(h + 1) * HEAD_DIM, qb * QBLK:(qb + 1) * QBLK] = o_t[:, j * QBLK:(j + 1) * QBLK]

    units = [(qb, kvh) for qb in range(nqb) for kvh in range(N_KV_HEADS)]
    wave = s_scr.shape[0]
    for w0 in range(0, len(units), wave):
        stats = [scores(qb, kvh) for qb, kvh in units[w0:w0 + wave]]
        for (qb, kvh), st in zip(units[w0:w0 + wave], stats):
            weighted_values(qb, kvh, *st)
    z = (ot_scr[...].T * sg_ref[0].astype(F32)).astype(wo_ref.dtype)
    y = _dot(z, wo_ref[...])
    o_ref[0] = x_ref[0] + mod_ref[0][2:3, :] * y


def _attn(q, sg, x, mod3, mod_row, kc, vct, k_lat, vt_lat, sink, w_out):
    bsz, t, d = x.shape
    dq = q.shape[-1]
    dkv = kc.shape[-1]
    p_len = kc.shape[1]
    nb = t // QBLK
    nqb = min(ATTN_QB, nb)
    rows = nqb * QBLK
    window = k_lat is not None
    tok = lambda b, i: (b, i, 0)
    in_specs = [pl.BlockSpec((1, rows, dq), tok),
                pl.BlockSpec((1, rows, dq), tok),
                pl.BlockSpec((1, rows, d), tok),
                pl.BlockSpec((1, 3, d), lambda b, i: (mod_row(b), 0, 0)),
                pl.BlockSpec((1, p_len, dkv), lambda b, i: (b, 0, 0)),
                pl.BlockSpec((1, dkv, p_len), lambda b, i: (b, 0, 0))]
    args = [q, sg, x, mod3, kc, vct]
    n_blocks = p_len // QBLK
    if window:
        prev = lambda i: jnp.maximum(i * nqb - 1, 0)
        nxt = lambda i: jnp.minimum((i + 1) * nqb, nb - 1)
        in_specs += [pl.BlockSpec((1, QBLK, dkv), lambda b, i: (b, prev(i), 0)),
                     pl.BlockSpec((1, rows, dkv), tok),
                     pl.BlockSpec((1, QBLK, dkv), lambda b, i: (b, nxt(i), 0)),
                     pl.BlockSpec((1, dkv, QBLK), lambda b, i: (b, 0, prev(i))),
                     pl.BlockSpec((1, dkv, rows), lambda b, i: (b, 0, i)),
                     pl.BlockSpec((1, dkv, QBLK), lambda b, i: (b, 0, nxt(i)))]
        args += [k_lat] * 3 + [vt_lat] * 3
        n_blocks += 3
    in_specs += [pl.BlockSpec((1, N_HEADS), lambda b, i: (0, 0)),
                 pl.BlockSpec(w_out.shape, lambda b, i: (0, 0), pipeline_mode=pl.Buffered(1))]
    args += [sink.reshape(1, N_HEADS), w_out]
    units = min(nqb, ATTN_WAVE_QB) * N_KV_HEADS
    return pl.pallas_call(
        functools.partial(_attn_kernel, window=window, nb=nb),
        out_shape=jax.ShapeDtypeStruct((bsz, t, d), F32),
        grid=(bsz, nb // nqb),
        in_specs=in_specs,
        out_specs=pl.BlockSpec((1, rows, d), tok),
        scratch_shapes=[pltpu.VMEM((units, n_blocks, QBLK, GROUP * QBLK), F32),
                        pltpu.VMEM((units, n_blocks * QBLK, GROUP * QBLK), BF16),
                        pltpu.VMEM((dq, rows), F32)],
        compiler_params=_cparams(("parallel", "parallel")),
        name="attn_window" if window else "attn_ctx",
    )(*args)


def _mlstm_in_kernel(x_ref, mod_ref, nw_ref, wt_ref, bgt_ref,
                     q_ref, k_ref, vt_ref, og_ref, gc_ref, gr_ref):
    dm = q_ref.shape[-1]
    nh = M_HEADS
    L = MCHUNK
    hb = _prenorm(x_ref[0], nw_ref[...], mod_ref[0]).astype(wt_ref.dtype)

    gr = _dot_nt(wt_ref[5 * dm:, :], hb) + bgt_ref[...]
    n_chunks = x_ref.shape[1] // L
    ri = lax.broadcasted_iota(jnp.int32, (L, L), 0)
    ci = lax.broadcasted_iota(jnp.int32, (L, L), 1)
    lane = lax.broadcasted_iota(jnp.int32, (n_chunks * nh, L), 1)
    g_rows = []
    for dr in range(2):
        before = (ri <= ci) if dr == 0 else (ri >= ci)
        tri = jnp.where(before, 1.0, 0.0).astype(BF16)
        base = dr * 2 * nh
        lf = _log_sigmoid(gr[base + nh:base + 2 * nh, :]) * LOG2E
        gi = gr[base:base + nh, :] * LOG2E
        lf_st = jnp.concatenate([lf[:, c * L:(c + 1) * L] for c in range(n_chunks)], axis=0)
        b_st = sum(_dot(piece, tri) for piece in _split3(lf_st))
        g_st = jnp.concatenate([gi[:, c * L:(c + 1) * L] for c in range(n_chunks)], axis=0) - b_st
        run = g_st
        step = 1
        while step < L:
            if dr == 0:
                run = jnp.where(lane >= step, jnp.maximum(run, pltpu.roll(run, step, 1)), run)
            else:
                run = jnp.where(lane < L - step, jnp.maximum(run, pltpu.roll(run, L - step, 1)), run)
            step *= 2
        for cidx in range(n_chunks):
            rows = slice(cidx * L, (cidx + 1) * L)
            blk = slice(cidx * nh, (cidx + 1) * nh)
            b_last = jnp.sum(lf[:, rows], axis=1, keepdims=True)
            g_max = jnp.max(g_st[blk, :], axis=1, keepdims=True)
            g_rows.append(g_st[blk, :])
            gr_ref[0, dr, cidx] = jnp.concatenate(
                [g_st[blk, :], b_st[blk, :], jnp.broadcast_to(b_last, (nh, L)),
                 jnp.broadcast_to(g_max, (nh, L)), run[blk, :]], axis=0)
    g_sq = jnp.concatenate(g_rows + [jnp.zeros((L - len(g_rows) * nh, L), F32)], axis=0).T
    for dr in range(2):
        for cidx in range(n_chunks):
            idx = dr * n_chunks + cidx
            gc_ref[0, dr, cidx * L:(cidx + 1) * L, :] = g_sq[:, idx * nh:(idx + 1) * nh]

    o = _dot_nt(hb, wt_ref[3 * dm:4 * dm, :])
    g = _dot_nt(hb, wt_ref[4 * dm:5 * dm, :])
    og_ref[0] = (jax.nn.sigmoid(o) * _silu(g)).astype(og_ref.dtype)
    q_ref[0] = _dot_nt(hb, wt_ref[0:dm, :]).astype(q_ref.dtype)
    k_ref[0] = (_dot_nt(hb, wt_ref[dm:2 * dm, :]) * (M_HD ** -0.5)).astype(k_ref.dtype)
    vt = _dot_nt(wt_ref[2 * dm:3 * dm, :], hb).astype(vt_ref.dtype)
    for cidx in range(n_chunks):
        vt_ref[0, cidx] = vt[:, cidx * L:(cidx + 1) * L]


def _mlstm_in(x, mod3, mod_row, norm_w, w_t, b_gates):
    bsz, t, d = x.shape
    dm = M_HEADS * M_HD
    ng = 4 * M_HEADS
    tm = min(PROJ_ROWS, t)
    tok = lambda b, i: (b, i, 0)
    const = lambda b, i: (0, 0)
    big = jax.ShapeDtypeStruct((bsz, t, dm), BF16)
    once = pl.Buffered(1)
    return pl.pallas_call(
        _mlstm_in_kernel,
        out_shape=(big, big, jax.ShapeDtypeStruct((bsz, t // MCHUNK, dm, MCHUNK), BF16), big,
                   jax.ShapeDtypeStruct((bsz, 2, t, M_HEADS), F32),
                   jax.ShapeDtypeStruct((bsz, 2, t // MCHUNK, 5 * M_HEADS, MCHUNK), F32)),
        grid=(bsz, t // tm),
        in_specs=[pl.BlockSpec((1, tm, d), tok),
                  pl.BlockSpec((1, 3, d), lambda b, i: (mod_row(b), 0, 0)),
                  pl.BlockSpec((1, d), const),
                  pl.BlockSpec(w_t.shape, const, pipeline_mode=once),
                  pl.BlockSpec((ng, 1), const)],
        out_specs=(pl.BlockSpec((1, tm, dm), tok), pl.BlockSpec((1, tm, dm), tok),
                   pl.BlockSpec((1, tm // MCHUNK, dm, MCHUNK), lambda b, i: (b, i, 0, 0)),
                   pl.BlockSpec((1, tm, dm), tok),
                   pl.BlockSpec((1, 2, tm, M_HEADS), lambda b, i: (b, 0, i, 0)),
                   pl.BlockSpec((1, 2, tm // MCHUNK, 5 * M_HEADS, MCHUNK), lambda b, i: (b, 0, i, 0, 0))),
        compiler_params=pltpu.CompilerParams(dimension_semantics=("parallel", "parallel"),
                                             vmem_limit_bytes=MLSTM_IN_VMEM_LIMIT),
        name="mlstm_in",
    )(x, mod3, norm_w.reshape(1, d), w_t, b_gates.reshape(ng, 1))


def _mlstm_scan_kernel(*refs, has_init, write_state, nc):
    refs = list(refs)
    q_ref, k_ref, vt_ref, gc_ref, gr_ref, og_ref, x_ref, mod_ref, wo_ref, fw_ref = refs[:10]
    pos = 10
    if has_init:
        c0_ref, n0_ref, m0_ref = refs[pos:pos + 3]
        pos += 3
    y_ref = refs[pos]
    pos += 1
    if write_state:
        cout_ref, nout_ref, mout_ref = refs[pos:pos + 3]
        pos += 3
    ct_scr, mscr, hcur, hfwd = refs[pos:pos + 4]

    drn = pl.program_id(1)
    c = pl.program_id(2)
    n_sub, L = q_ref.shape[1], q_ref.shape[2]
    nh = M_HEADS
    pad = ct_scr.shape[1] - M_HD

    @pl.when(c == 0)
    def _init():
        if has_init:
            for h in range(nh):
                ct_scr[h, 0:M_HD, :] = c0_ref[0, 0, h].T
                ct_scr[h, M_HD:M_HD + pad, :] = jnp.concatenate(
                    [n0_ref[0, 0, h:h + 1, :], jnp.zeros((pad - 1, M_HD), F32)], axis=0)
            mscr[...] = m0_ref[0, 0] * LOG2E
        else:
            ct_scr[...] = jnp.zeros(ct_scr.shape, F32)
            mscr[...] = jnp.zeros(mscr.shape, F32)

    si = lax.broadcasted_iota(jnp.int32, (L, L), 0)
    li = lax.broadcasted_iota(jnp.int32, (L, L), 1)
    seen_t = (si - li) * (1 - 2 * drn) <= 0

    ones_rows = jnp.where(lax.broadcasted_iota(jnp.int32, (pad, L), 0) == 0, 1.0, 0.0).astype(BF16)

    def chunk_step(sub):
        gcb = gc_ref[0, 0, sub]
        grb = gr_ref[0, 0, sub]
        q = q_ref[0, sub]
        k = k_ref[0, sub]
        vt = vt_ref[0, sub]

        def head_scores(h):
            hs = slice(h * M_HD, (h + 1) * M_HD)
            m_prev = mscr[h:h + 1, 0:1]
            ct = ct_scr[h]
            m_row = jnp.maximum(grb[4 * nh + h:4 * nh + h + 1, :], m_prev)
            w_t = jnp.exp2(jnp.where(seen_t, gcb[:, h:h + 1], NEG_INF) - m_row)
            r1 = _dot_nt(jnp.concatenate([k[:, hs], ct.astype(BF16)], axis=0), q[:, hs])
            s_t = (r1[0:L, :] * w_t).astype(BF16)
            return m_prev, ct, m_row, s_t, r1[L:, :]

        def head_finish(h, m_prev, ct, m_row, s_t, inter):
            hs = slice(h * M_HD, (h + 1) * M_HD)
            vext = jnp.concatenate([vt[hs, :], ones_rows], axis=0)
            g_r = grb[h:h + 1, :]
            b_r = grb[nh + h:nh + h + 1, :]
            b_last = grb[2 * nh + h:2 * nh + h + 1, 0:1]
            g_max = grb[3 * nh + h:3 * nh + h + 1, 0:1]
            w0 = jnp.exp2(m_prev - m_row)
            tot = _dot(vext, s_t) + w0 * inter
            den = tot[M_HD:M_HD + 1, :]
            floor = jnp.exp2(-(b_r + m_row))
            hcur[sub, hs, :] = tot[0:M_HD, :] / jnp.maximum(jnp.abs(den), floor)

            m_last = jnp.maximum(g_max, m_prev)
            wk = jnp.exp2(g_r - m_last)
            decay = jnp.exp2(m_prev - m_last)
            vw = (vext.astype(F32) * wk).astype(BF16)
            ct_scr[h] = decay * ct + _dot(vw, k[:, hs])
            mscr[h:h + 1, :] = jnp.broadcast_to(b_last + m_last, (1, LANES))

        pending = [head_scores(h) for h in range(min(SCAN_AHEAD, nh))]
        for h in range(nh):
            if h + SCAN_AHEAD < nh:
                pending.append(head_scores(h + SCAN_AHEAD))
            head_finish(h, *pending.pop(0))

    for j in range(n_sub):
        chunk_step(j + drn * (n_sub - 1 - 2 * j))

    @pl.when(drn == 0)
    def _park():
        hfwd[pl.ds(c * n_sub, n_sub)] = hcur[...]

    @pl.when(drn == 1)
    def _emit():
        first = (nc - 1 - c) * n_sub
        hsum_t = jnp.concatenate([hcur[j] + hfwd[first + j] for j in range(n_sub)], axis=1)
        hm = hsum_t.T * og_ref[0].astype(F32)
        y = _dot(hm.astype(wo_ref.dtype), wo_ref[...])
        x2 = x_ref[0] + mod_ref[0][2:3, :] * y
        ms = jnp.mean(x2 * x2, axis=-1, keepdims=True)
        y_ref[0] = x2 * lax.rsqrt(ms + EPS) * fw_ref[...]

    if write_state:
        @pl.when(c == nc - 1)
        def _final():
            for h in range(nh):
                cfin = ct_scr[h]
                cout_ref[0, 0, h] = cfin[0:M_HD, :].T
                nout_ref[0, 0, h:h + 1, :] = cfin[M_HD:M_HD + 1, :]
            mout_ref[0, 0] = mscr[...] * LN2


def _mlstm_scan(q, k, vt, gc, gr, og, x, mod3, mod_row, w_out, final_w, init, write_state):
    bsz, t, dm = q.shape
    d_model = x.shape[-1]
    L = MCHUNK
    n_sub = SCAN_SUB if (t // L) % SCAN_SUB == 0 else 1
    nc = t // (L * n_sub)
    rows = n_sub * L
    blk = lambda d, c: c + d * (nc - 1 - 2 * c)
    chunked = lambda b, d, c: (b, blk(d, c), 0, 0)
    gated = lambda b, d, c: (b, d, blk(d, c), 0, 0)
    tail = lambda b, d, c: (b, nc - 1 - d * c, 0)
    const = lambda b, d, c: (0, 0)
    in_specs = [pl.BlockSpec((1, n_sub, L, dm), chunked),
                pl.BlockSpec((1, n_sub, L, dm), chunked),
                pl.BlockSpec((1, n_sub, dm, L), chunked),
                pl.BlockSpec((1, 1, n_sub, L, gc.shape[-1]), gated),
                pl.BlockSpec((1, 1, n_sub, gr.shape[3], L), gated),
                pl.BlockSpec((1, rows, dm), tail),
                pl.BlockSpec((1, rows, d_model), tail),
                pl.BlockSpec((1, 3, d_model), lambda b, d, c: (mod_row(b), 0, 0)),
                pl.BlockSpec(w_out.shape, const, pipeline_mode=pl.Buffered(1)),
                pl.BlockSpec((1, d_model), const)]
    args = [q.reshape(bsz, t // L, L, dm), k.reshape(bsz, t // L, L, dm), vt,
            gc.reshape(bsz, 2, t // L, L, gc.shape[-1]), gr, og, x, mod3, w_out,
            final_w.reshape(1, d_model)]
    st = lambda b, d, c: (b, d, 0, 0)
    st5 = lambda b, d, c: (b, d, 0, 0, 0)
    if init is not None:
        c0, n0, m0 = init
        in_specs += [pl.BlockSpec((1, 1, M_HEADS, M_HD, M_HD), st5),
                     pl.BlockSpec((1, 1, M_HEADS, M_HD), st),
                     pl.BlockSpec((1, 1, M_HEADS, LANES), st)]
        args += [c0, n0, jnp.broadcast_to(m0[..., None], m0.shape + (LANES,))]
    out_shape = [jax.ShapeDtypeStruct((bsz, t, d_model), F32)]
    out_specs = [pl.BlockSpec((1, rows, d_model), tail)]
    if write_state:
        out_shape += [jax.ShapeDtypeStruct((bsz, 2, M_HEADS, M_HD, M_HD), F32),
                      jax.ShapeDtypeStruct((bsz, 2, M_HEADS, M_HD), F32),
                      jax.ShapeDtypeStruct((bsz, 2, M_HEADS, LANES), F32)]
        out_specs += [pl.BlockSpec((1, 1, M_HEADS, M_HD, M_HD), st5),
                      pl.BlockSpec((1, 1, M_HEADS, M_HD), st),
                      pl.BlockSpec((1, 1, M_HEADS, LANES), st)]
    return pl.pallas_call(
        functools.partial(_mlstm_scan_kernel, has_init=init is not None,
                          write_state=write_state, nc=nc),
        out_shape=tuple(out_shape),
        grid=(bsz, 2, nc),
        in_specs=in_specs,
        out_specs=tuple(out_specs),
        scratch_shapes=[pltpu.VMEM((M_HEADS, M_HD + BF16_ROWS, M_HD), F32),
                        pltpu.VMEM((M_HEADS, LANES), F32),
                        pltpu.VMEM((n_sub, dm, L), F32),
                        pltpu.VMEM((t // L, dm, L), F32)],
        compiler_params=_cparams(("parallel", "arbitrary", "arbitrary")),
        name="mlstm_scan",
    )(*args)


def _rope_tables(t):
    nf = HEAD_DIM // 4
    pos = jnp.arange(t)
    row = (pos // GRID_W).astype(F32)
    col = (pos % GRID_W).astype(F32)
    inv = ROPE_BASE ** (-jnp.arange(nf, dtype=F32) / nf)
    ar = row[:, None] * inv[None, :]
    ac = col[:, None] * inv[None, :]
    cos = jnp.concatenate([jnp.cos(ar), jnp.cos(ar), jnp.cos(ac), jnp.cos(ac)], axis=1)
    sin = jnp.concatenate([-jnp.sin(ar), jnp.sin(ar), -jnp.sin(ac), jnp.sin(ac)], axis=1)
    reps = LANES // HEAD_DIM
    return jnp.tile(cos, (1, reps)), jnp.tile(sin, (1, reps))


def kernel(x_prompt, x_sample, cache_k, cache_v, state_C, state_n, state_m, c, c_ctx,
           attn_norm_w, attn_ada_w, attn_ada_b, attn_w_in, attn_sink, attn_w_out,
           mlstm_norm_w, mlstm_ada_w, mlstm_ada_b, mlstm_w_in, mlstm_b_gates, mlstm_w_out,
           final_norm_w):
    assert attn_w_in.shape[0] == 1 and mlstm_w_in.shape[0] == 1, "one layer of each mixer"
    bsz, seq, d = x_prompt.shape
    dbsz, dseq, _ = x_sample.shape
    dkv = N_KV_HEADS * HEAD_DIM
    dm = M_HEADS * M_HD

    n_cond = 1 + dbsz
    cond = jnp.concatenate([c_ctx[None, :], c, jnp.zeros((-n_cond % 8, d), F32)], axis=0)
    attn_mod = _ada(cond, attn_ada_w[0], attn_ada_b[0]).reshape(-1, 3, d)
    mlstm_mod = _ada(cond, mlstm_ada_w[0], mlstm_ada_b[0]).reshape(-1, 3, d)
    ctx_row = lambda b: 0
    lat_row = lambda b: b + 1

    attn_w_in0 = attn_w_in[0]
    attn_w_out0 = attn_w_out[0]
    attn_wv_t = attn_w_in[0, :, 2 * N_HEADS * HEAD_DIM + dkv:].T
    mlstm_w_in_t = mlstm_w_in[0].T
    mlstm_w_out0 = mlstm_w_out[0]

    def mlstm_layer(x, mod_row, init, write_state):
        q, k, vt, og, gc, gr = _mlstm_in(x, mlstm_mod, mod_row, mlstm_norm_w[0], mlstm_w_in_t,
                                         mlstm_b_gates[0])
        outs = _mlstm_scan(q, k, vt, gc, gr, og, x, mlstm_mod, mod_row, mlstm_w_out0, final_norm_w,
                           init, write_state)
        return outs[0], outs[1:]

    q, sg, k_ctx, vt_ctx, v_ctx = _attn_in(x_prompt, attn_mod, ctx_row, attn_norm_w[0], attn_w_in0,
                                           attn_wv_t, None, F32, True)
    x1 = _attn(q, sg, x_prompt, attn_mod, ctx_row, k_ctx, vt_ctx, None, None, attn_sink[0], attn_w_out0)
    y_prompt, (c_fin, n_fin, m_fin) = mlstm_layer(x1, ctx_row, None, True)

    q, sg, k_lat, vt_lat = _attn_in(x_sample, attn_mod, lat_row, attn_norm_w[0], attn_w_in0,
                                    attn_wv_t, _rope_tables(dseq), BF16, False)
    kc = cache_k[:, 0].reshape(dbsz, -1, dkv).astype(BF16)
    vct = jnp.swapaxes(cache_v[:, 0].reshape(dbsz, -1, dkv), 1, 2).astype(BF16)
    x1 = _attn(q, sg, x_sample, attn_mod, lat_row, kc, vct, k_lat, vt_lat, attn_sink[0], attn_w_out0)
    y_sample, _ = mlstm_layer(x1, lat_row, (state_C[:, 0], state_n[:, 0], state_m[:, 0]), False)

    new_cache_k = k_ctx.reshape(bsz, 1, seq, N_KV_HEADS, HEAD_DIM)
    new_cache_v = v_ctx.reshape(bsz, 1, seq, N_KV_HEADS, HEAD_DIM)
    return (y_prompt, y_sample, new_cache_k, new_cache_v,
            c_fin[:, None], n_fin[:, None], m_fin[:, None, :, :, 0])
```

```python
import functools

import jax
import jax.numpy as jnp
from jax import lax
from jax.experimental import pallas as pl
from jax.experimental.pallas import tpu as pltpu

F32 = jnp.float32
BF16 = jnp.bfloat16

HEAD_DIM = 64
N_KV_HEADS = 4
GROUP = 4
N_HEADS = N_KV_HEADS * GROUP
QBLK = 128
GRID_W = 64
ROPE_BASE = 10000.0
M_HEADS = 8
M_HD = 128
EPS = 1e-6

LANES = 128
BF16_ROWS = 16
VMEM_LIMIT = 48 * 1024 * 1024
MLSTM_IN_VMEM_LIMIT = 60 * 1024 * 1024

MCHUNK = 256
ATTN_QB = 4
ATTN_WAVE_QB = 2
SCAN_SUB = 2
SCAN_AHEAD = 4
PROJ_ROWS = 1024
ADA_TILE = 512

NEG_INF = float("-inf")
LOG2E = 1.4426950408889634
LN2 = 0.6931471805599453


def _cparams(sem):
    return pltpu.CompilerParams(dimension_semantics=sem, vmem_limit_bytes=VMEM_LIMIT)


def _silu(x):
    return x * jax.nn.sigmoid(x)


def _log_sigmoid(x):
    return jnp.minimum(x, 0.0) - jnp.log1p(jnp.exp(-jnp.abs(x)))


def _dot(a, b):
    return jnp.dot(a, b, preferred_element_type=F32)


def _dot_nt(a, b):
    return lax.dot_general(a, b, (((1,), (1,)), ((), ())), preferred_element_type=F32)


def _split3(x):
    hi = x.astype(BF16)
    r = x - hi.astype(F32)
    mid = r.astype(BF16)
    lo = (r - mid.astype(F32)).astype(BF16)
    return hi, mid, lo


def _prenorm(x, norm_w, mod):
    ms = jnp.mean(x * x, axis=-1, keepdims=True)
    y = x * lax.rsqrt(ms + EPS) * norm_w
    return y * (1.0 + mod[1:2, :]) + mod[0:1, :]


def _ada_kernel(cond_ref, w_ref, b_ref, o_ref):
    a = _silu(cond_ref[...]).astype(BF16)
    o_ref[...] = _dot(a, w_ref[...].astype(BF16)) + b_ref[...]


def _ada(cond8, w, b):
    d, n = w.shape
    tn = ADA_TILE
    return pl.pallas_call(
        _ada_kernel,
        out_shape=jax.ShapeDtypeStruct((cond8.shape[0], n), F32),
        grid=(n // tn,),
        in_specs=[pl.BlockSpec(cond8.shape, lambda j: (0, 0)),
                  pl.BlockSpec((d, tn), lambda j: (0, j)),
                  pl.BlockSpec((1, tn), lambda j: (0, j))],
        out_specs=pl.BlockSpec((cond8.shape[0], tn), lambda j: (0, j)),
        compiler_params=_cparams(("parallel",)),
        name="ada_mod",
    )(cond8, w, b.reshape(1, n))


def _rope(x, cos, sin, lane):
    first = (lane & 31) < 16
    outs = []
    for c in range(x.shape[1] // LANES):
        xc = x[:, c * LANES:(c + 1) * LANES]
        sw = jnp.where(first, pltpu.roll(xc, LANES - 16, 1), pltpu.roll(xc, 16, 1))
        outs.append(xc * cos + sw * sin)
    return jnp.concatenate(outs, axis=1)


def _attn_in_kernel(*refs, rope, emit_v):
    refs = list(refs)
    x_ref, mod_ref, nw_ref, w_ref, wvt_ref = refs[:5]
    pos = 5
    if rope:
        cos_ref, sin_ref = refs[pos:pos + 2]
        pos += 2
    q_ref, sg_ref, k_ref, vt_ref = refs[pos:pos + 4]
    dq = q_ref.shape[-1]
    dkv = k_ref.shape[-1]
    n_req, tr, d = x_ref.shape
    rows = n_req * tr
    hb = _prenorm(x_ref[...].reshape(rows, d), nw_ref[...], mod_ref[0]).astype(w_ref.dtype)
    q = _dot(hb, w_ref[:, 0:dq])
    g = _dot(hb, w_ref[:, dq:2 * dq])
    k = _dot(hb, w_ref[:, 2 * dq:2 * dq + dkv])
    if rope:
        cos = cos_ref[...]
        sin = sin_ref[...]
        lane = lax.broadcasted_iota(jnp.int32, cos.shape, 1)
        q = _rope(q, cos, sin, lane)
        k = _rope(k, cos, sin, lane)
    q_ref[...] = (q * (HEAD_DIM ** -0.5 * LOG2E)).astype(q_ref.dtype).reshape(q_ref.shape)
    sg_ref[...] = _silu(g).astype(sg_ref.dtype).reshape(sg_ref.shape)
    k_ref[...] = k.astype(k_ref.dtype).reshape(k_ref.shape)
    vt = _dot_nt(wvt_ref[...], hb).astype(vt_ref.dtype)
    for r in range(n_req):
        vt_ref[r] = vt[:, r * tr:(r + 1) * tr]
    if emit_v:
        v_ref = refs[pos + 4]
        v = _dot(hb, w_ref[:, 2 * dq + dkv:2 * dq + 2 * dkv])
        v_ref[...] = v.astype(v_ref.dtype).reshape(v_ref.shape)


def _proj_tiling(bsz, t, shared_cond):
    tr = min(PROJ_ROWS, t)
    n_req = PROJ_ROWS // tr if shared_cond and bsz % (PROJ_ROWS // tr) == 0 else 1
    return tr, n_req


def _attn_in(x, mod3, mod_row, norm_w, w_in, wv_t, rope_tabs, k_dtype, emit_v, shared_cond):
    bsz, t, d = x.shape
    dq = N_HEADS * HEAD_DIM
    dkv = N_KV_HEADS * HEAD_DIM
    tm, n_req = _proj_tiling(bsz, t, shared_cond)
    rope = rope_tabs is not None
    tok = lambda b, i: (b, i, 0)
    const = lambda b, i: (0, 0)
    in_specs = [pl.BlockSpec((n_req, tm, d), tok),
                pl.BlockSpec((1, 3, d), lambda b, i: (mod_row(b), 0, 0)),
                pl.BlockSpec((1, d), const),
                pl.BlockSpec(w_in.shape, const),
                pl.BlockSpec(wv_t.shape, const)]
    args = [x, mod3, norm_w.reshape(1, d), w_in, wv_t]
    if rope:
        in_specs += [pl.BlockSpec((tm, LANES), lambda b, i: (i, 0))] * 2
        args += list(rope_tabs)
    out_shape = [jax.ShapeDtypeStruct((bsz, t, dq), BF16),
                 jax.ShapeDtypeStruct((bsz, t, dq), BF16),
                 jax.ShapeDtypeStruct((bsz, t, dkv), k_dtype),
                 jax.ShapeDtypeStruct((bsz, dkv, t), BF16)]
    out_specs = [pl.BlockSpec((n_req, tm, dq), tok), pl.BlockSpec((n_req, tm, dq), tok),
                 pl.BlockSpec((n_req, tm, dkv), tok),
                 pl.BlockSpec((n_req, dkv, tm), lambda b, i: (b, 0, i))]
    if emit_v:
        out_shape.append(jax.ShapeDtypeStruct((bsz, t, dkv), F32))
        out_specs.append(pl.BlockSpec((n_req, tm, dkv), tok))
    return pl.pallas_call(
        functools.partial(_attn_in_kernel, rope=rope, emit_v=emit_v),
        out_shape=tuple(out_shape),
        grid=(bsz // n_req, t // tm),
        in_specs=in_specs,
        out_specs=tuple(out_specs),
        compiler_params=_cparams(("parallel", "parallel")),
        name="attn_in_rope" if rope else "attn_in",
    )(*args)


def _attn_kernel(*refs, window, nb):
    if window:
        (q_ref, sg_ref, x_ref, mod_ref, kc_ref, vct_ref, kp_ref, km_ref, kn_ref,
         vpt_ref, vmt_ref, vnt_ref, sink_ref, wo_ref, o_ref, s_scr, p_scr, ot_scr) = refs
    else:
        q_ref, sg_ref, x_ref, mod_ref, kc_ref, vct_ref, sink_ref, wo_ref, o_ref, s_scr, p_scr, ot_scr = refs
    step = pl.program_id(1)
    nqb = q_ref.shape[1] // QBLK
    n_ctx = kc_ref.shape[1] // QBLK
    cols = GROUP * QBLK
    if window:
        kj = lax.broadcasted_iota(jnp.int32, (QBLK, cols), 0)
        qi = lax.broadcasted_iota(jnp.int32, (QBLK, cols), 1) & (QBLK - 1)
        after_diag = kj >= qi
        before_diag = kj <= qi
    ones_rows = jnp.where(lax.broadcasted_iota(jnp.int32, (BF16_ROWS, QBLK), 0) == 0, 1.0, 0.0).astype(BF16)
    n_blk = n_ctx + (3 if window else 0)

    def window_blocks(qb, cs, kp, km, kn, lanes):
        def mid(j):
            sl = slice(j * QBLK, (j + 1) * QBLK)
            return km[0, cs, sl] if lanes else km[0, sl, cs]
        first = kp[0, cs, :] if lanes else kp[0][:, cs]
        last = kn[0, cs, :] if lanes else kn[0][:, cs]
        return [first if qb == 0 else mid(qb - 1), mid(qb), last if qb == nqb - 1 else mid(qb + 1)]

    def block_masks(qb):
        if not window:
            return [None] * n_ctx
        prev_ok = after_diag & (step > 0) if qb == 0 else after_diag
        next_ok = before_diag & (step < nb // nqb - 1) if qb == nqb - 1 else before_diag
        return [None] * n_ctx + [prev_ok, None, next_ok]

    def scores(qb, kvh):
        u = (qb * N_KV_HEADS + kvh) % s_scr.shape[0]
        cs = slice(kvh * HEAD_DIM, (kvh + 1) * HEAD_DIM)
        heads = [kvh * GROUP + j for j in range(GROUP)]
        qq = q_ref[0, qb * QBLK:(qb + 1) * QBLK, :]
        q4 = jnp.concatenate([qq[:, h * HEAD_DIM:(h + 1) * HEAD_DIM] for h in heads], axis=0)
        sink_row = jnp.concatenate(
            [jnp.broadcast_to(sink_ref[0:1, h:h + 1], (1, QBLK)) for h in heads], axis=1) * LOG2E
        keys = [kc_ref[0, j * QBLK:(j + 1) * QBLK, cs].astype(BF16) for j in range(n_ctx)]
        if window:
            keys += window_blocks(qb, cs, kp_ref, km_ref, kn_ref, False)
        st_all = _dot_nt(jnp.concatenate(keys, axis=0), q4)
        macc = jnp.full((8, cols), NEG_INF, F32)
        for j, ok in enumerate(block_masks(qb)):
            s_blk = st_all[j * QBLK:(j + 1) * QBLK, :]
            if ok is not None:
                s_blk = jnp.where(ok, s_blk, NEG_INF)
            s_scr[u, j] = s_blk
            macc = jnp.maximum(macc, jnp.max(s_blk.reshape(QBLK // 8, 8, cols), axis=0))
        return jnp.maximum(jnp.max(macc, axis=0, keepdims=True), sink_row), sink_row

    def weighted_values(qb, kvh, m_row, sink_row):
        u = (qb * N_KV_HEADS + kvh) % s_scr.shape[0]
        cs = slice(kvh * HEAD_DIM, (kvh + 1) * HEAD_DIM)
        for j in range(n_blk):
            p_scr[u, j * QBLK:(j + 1) * QBLK, :] = jnp.exp2(s_scr[u, j] - m_row).astype(BF16)
        vts = [vct_ref[0, cs, j * QBLK:(j + 1) * QBLK] for j in range(n_ctx)]
        if window:
            vts += window_blocks(qb, cs, vpt_ref, vmt_ref, vnt_ref, True)
        vt_ext = jnp.concatenate(
            [jnp.concatenate(vts, axis=1), jnp.tile(ones_rows, (1, n_blk))], axis=0)
        acc = _dot(vt_ext, p_scr[u])
        den = acc[HEAD_DIM:HEAD_DIM + 1, :] + jnp.exp2(sink_row - m_row)
        o_t = acc[0:HEAD_DIM, :] / den
        for j in range(GROUP):
            h = kvh * GROUP + j
            ot_scr[h * HEAD_DIM:(h + 1) * HEAD_DIM, qb * QBLK:(qb + 1) * QBLK] = o_t[:, j * QBLK:(j + 1) * QBLK]

    units = [(qb, kvh) for qb in range(nqb) for kvh in range(N_KV_HEADS)]
    wave = s_scr.shape[0]
    for w0 in range(0, len(units), wave):
        stats = [scores(qb, kvh) for qb, kvh in units[w0:w0 + wave]]
        for (qb, kvh), st in zip(units[w0:w0 + wave], stats):
            weighted_values(qb, kvh, *st)
    z = (ot_scr[...].T * sg_ref[0].astype(F32)).astype(wo_ref.dtype)
    y = _dot(z, wo_ref[...])
    o_ref[0] = x_ref[0] + mod_ref[0][2:3, :] * y


def _attn(q, sg, x, mod3, mod_row, kc, vct, k_lat, vt_lat, sink, w_out):
    bsz, t, d = x.shape
    dq = q.shape[-1]
    dkv = kc.shape[-1]
    p_len = kc.shape[1]
    nb = t // QBLK
    nqb = min(ATTN_QB, nb)
    rows = nqb * QBLK
    window = k_lat is not None
    tok = lambda b, i: (b, i, 0)
    in_specs = [pl.BlockSpec((1, rows, dq), tok),
                pl.BlockSpec((1, rows, dq), tok),
                pl.BlockSpec((1, rows, d), tok),
                pl.BlockSpec((1, 3, d), lambda b, i: (mod_row(b), 0, 0)),
                pl.BlockSpec((1, p_len, dkv), lambda b, i: (b, 0, 0)),
                pl.BlockSpec((1, dkv, p_len), lambda b, i: (b, 0, 0))]
    args = [q, sg, x, mod3, kc, vct]
    n_blocks = p_len // QBLK
    if window:
        prev = lambda i: jnp.maximum(i * nqb - 1, 0)
        nxt = lambda i: jnp.minimum((i + 1) * nqb, nb - 1)
        in_specs += [pl.BlockSpec((1, QBLK, dkv), lambda b, i: (b, prev(i), 0)),
                     pl.BlockSpec((1, rows, dkv), tok),
                     pl.BlockSpec((1, QBLK, dkv), lambda b, i: (b, nxt(i), 0)),
                     pl.BlockSpec((1, dkv, QBLK), lambda b, i: (b, 0, prev(i))),
                     pl.BlockSpec((1, dkv, rows), lambda b, i: (b, 0, i)),
                     pl.BlockSpec((1, dkv, QBLK), lambda b, i: (b, 0, nxt(i)))]
        args += [k_lat] * 3 + [vt_lat] * 3
        n_blocks += 3
    in_specs += [pl.BlockSpec((1, N_HEADS), lambda b, i: (0, 0)),
                 pl.BlockSpec(w_out.shape, lambda b, i: (0, 0), pipeline_mode=pl.Buffered(1))]
    args += [sink.reshape(1, N_HEADS), w_out]
    units = min(nqb, ATTN_WAVE_QB) * N_KV_HEADS
    return pl.pallas_call(
        functools.partial(_attn_kernel, window=window, nb=nb),
        out_shape=jax.ShapeDtypeStruct((bsz, t, d), F32),
        grid=(bsz, nb // nqb),
        in_specs=in_specs,
        out_specs=pl.BlockSpec((1, rows, d), tok),
        scratch_shapes=[pltpu.VMEM((units, n_blocks, QBLK, GROUP * QBLK), F32),
                        pltpu.VMEM((units, n_blocks * QBLK, GROUP * QBLK), BF16),
                        pltpu.VMEM((dq, rows), F32)],
        compiler_params=_cparams(("parallel", "parallel")),
        name="attn_window" if window else "attn_ctx",
    )(*args)


def _mlstm_in_kernel(x_ref, mod_ref, nw_ref, wt_ref, bgt_ref,
                     q_ref, k_ref, vt_ref, og_ref, gc_ref, gr_ref):
    dm = q_ref.shape[-1]
    nh = M_HEADS
    L = MCHUNK
    n_req, tr, d = x_ref.shape
    per_req = tr // L
    hb = _prenorm(x_ref[...].reshape(n_req * tr, d), nw_ref[...], mod_ref[0]).astype(wt_ref.dtype)

    gr = _dot_nt(wt_ref[5 * dm:, :], hb) + bgt_ref[...]
    n_chunks = n_req * per_req
    ri = lax.broadcasted_iota(jnp.int32, (L, L), 0)
    ci = lax.broadcasted_iota(jnp.int32, (L, L), 1)
    lane = lax.broadcasted_iota(jnp.int32, (n_chunks * nh, L), 1)
    g_rows = []
    for dr in range(2):
        before = (ri <= ci) if dr == 0 else (ri >= ci)
        tri = jnp.where(before, 1.0, 0.0).astype(BF16)
        base = dr * 2 * nh
        lf = _log_sigmoid(gr[base + nh:base + 2 * nh, :]) * LOG2E
        gi = gr[base:base + nh, :] * LOG2E
        lf_st = jnp.concatenate([lf[:, c * L:(c + 1) * L] for c in range(n_chunks)], axis=0)
        b_st = sum(_dot(piece, tri) for piece in _split3(lf_st))
        g_st = jnp.concatenate([gi[:, c * L:(c + 1) * L] for c in range(n_chunks)], axis=0) - b_st
        run = g_st
        step = 1
        while step < L:
            if dr == 0:
                run = jnp.where(lane >= step, jnp.maximum(run, pltpu.roll(run, step, 1)), run)
            else:
                run = jnp.where(lane < L - step, jnp.maximum(run, pltpu.roll(run, L - step, 1)), run)
            step *= 2
        for cidx in range(n_chunks):
            rows = slice(cidx * L, (cidx + 1) * L)
            blk = slice(cidx * nh, (cidx + 1) * nh)
            b_last = jnp.sum(lf[:, rows], axis=1, keepdims=True)
            g_max = jnp.max(g_st[blk, :], axis=1, keepdims=True)
            g_rows.append(g_st[blk, :])
            gr_ref[cidx // per_req, dr, cidx % per_req] = jnp.concatenate(
                [g_st[blk, :], b_st[blk, :], jnp.broadcast_to(b_last, (nh, L)),
                 jnp.broadcast_to(g_max, (nh, L)), run[blk, :]], axis=0)
    g_sq = jnp.concatenate(g_rows + [jnp.zeros((L - len(g_rows) * nh, L), F32)], axis=0).T
    for dr in range(2):
        for cidx in range(n_chunks):
            idx = dr * n_chunks + cidx
            lc = cidx % per_req
            gc_ref[cidx // per_req, dr, lc * L:(lc + 1) * L, :] = g_sq[:, idx * nh:(idx + 1) * nh]

    o = _dot_nt(hb, wt_ref[3 * dm:4 * dm, :])
    g = _dot_nt(hb, wt_ref[4 * dm:5 * dm, :])
    og_ref[...] = (jax.nn.sigmoid(o) * _silu(g)).astype(og_ref.dtype).reshape(og_ref.shape)
    q_ref[...] = _dot_nt(hb, wt_ref[0:dm, :]).astype(q_ref.dtype).reshape(q_ref.shape)
    k = _dot_nt(hb, wt_ref[dm:2 * dm, :]) * (M_HD ** -0.5)
    k_ref[...] = k.astype(k_ref.dtype).reshape(k_ref.shape)
    vt = _dot_nt(wt_ref[2 * dm:3 * dm, :], hb).astype(vt_ref.dtype)
    for cidx in range(n_chunks):
        vt_ref[cidx // per_req, cidx % per_req] = vt[:, cidx * L:(cidx + 1) * L]


def _mlstm_in(x, mod3, mod_row, norm_w, w_t, b_gates, shared_cond):
    bsz, t, d = x.shape
    dm = M_HEADS * M_HD
    ng = 4 * M_HEADS
    tm, n_req = _proj_tiling(bsz, t, shared_cond)
    tok = lambda b, i: (b, i, 0)
    const = lambda b, i: (0, 0)
    big = jax.ShapeDtypeStruct((bsz, t, dm), BF16)
    once = pl.Buffered(1)
    return pl.pallas_call(
        _mlstm_in_kernel,
        out_shape=(big, big, jax.ShapeDtypeStruct((bsz, t // MCHUNK, dm, MCHUNK), BF16), big,
                   jax.ShapeDtypeStruct((bsz, 2, t, M_HEADS), F32),
                   jax.ShapeDtypeStruct((bsz, 2, t // MCHUNK, 5 * M_HEADS, MCHUNK), F32)),
        grid=(bsz // n_req, t // tm),
        in_specs=[pl.BlockSpec((n_req, tm, d), tok),
                  pl.BlockSpec((1, 3, d), lambda b, i: (mod_row(b), 0, 0)),
                  pl.BlockSpec((1, d), const),
                  pl.BlockSpec(w_t.shape, const, pipeline_mode=once),
                  pl.BlockSpec((ng, 1), const)],
        out_specs=(pl.BlockSpec((n_req, tm, dm), tok), pl.BlockSpec((n_req, tm, dm), tok),
                   pl.BlockSpec((n_req, tm // MCHUNK, dm, MCHUNK), lambda b, i: (b, i, 0, 0)),
                   pl.BlockSpec((n_req, tm, dm), tok),
                   pl.BlockSpec((n_req, 2, tm, M_HEADS), lambda b, i: (b, 0, i, 0)),
                   pl.BlockSpec((n_req, 2, tm // MCHUNK, 5 * M_HEADS, MCHUNK),
                                lambda b, i: (b, 0, i, 0, 0))),
        compiler_params=pltpu.CompilerParams(dimension_semantics=("parallel", "parallel"),
                                             vmem_limit_bytes=MLSTM_IN_VMEM_LIMIT),
        name="mlstm_in",
    )(x, mod3, norm_w.reshape(1, d), w_t, b_gates.reshape(ng, 1))


def _mlstm_scan_kernel(*refs, has_init, write_state, nc):
    refs = list(refs)
    q_ref, k_ref, vt_ref, gc_ref, gr_ref, og_ref, x_ref, mod_ref, wo_ref, fw_ref = refs[:10]
    pos = 10
    if has_init:
        c0_ref, n0_ref, m0_ref = refs[pos:pos + 3]
        pos += 3
    y_ref = refs[pos]
    pos += 1
    if write_state:
        cout_ref, nout_ref, mout_ref = refs[pos:pos + 3]
        pos += 3
    ct_scr, mscr, hcur, hfwd = refs[pos:pos + 4]

    drn = pl.program_id(1)
    c = pl.program_id(2)
    n_sub, L = q_ref.shape[1], q_ref.shape[2]
    nh = M_HEADS
    pad = ct_scr.shape[1] - M_HD

    @pl.when(c == 0)
    def _init():
        if has_init:
            for h in range(nh):
                ct_scr[h, 0:M_HD, :] = c0_ref[0, 0, h].T
                ct_scr[h, M_HD:M_HD + pad, :] = jnp.concatenate(
                    [n0_ref[0, 0, h:h + 1, :], jnp.zeros((pad - 1, M_HD), F32)], axis=0)
            mscr[...] = m0_ref[0, 0] * LOG2E
        else:
            ct_scr[...] = jnp.zeros(ct_scr.shape, F32)
            mscr[...] = jnp.zeros(mscr.shape, F32)

    si = lax.broadcasted_iota(jnp.int32, (L, L), 0)
    li = lax.broadcasted_iota(jnp.int32, (L, L), 1)
    seen_t = (si - li) * (1 - 2 * drn) <= 0

    ones_rows = jnp.where(lax.broadcasted_iota(jnp.int32, (pad, L), 0) == 0, 1.0, 0.0).astype(BF16)

    def chunk_step(sub):
        gcb = gc_ref[0, 0, sub]
        grb = gr_ref[0, 0, sub]
        q = q_ref[0, sub]
        k = k_ref[0, sub]
        vt = vt_ref[0, sub]

        def head_scores(h):
            hs = slice(h * M_HD, (h + 1) * M_HD)
            m_prev = mscr[h:h + 1, 0:1]
            ct = ct_scr[h]
            m_row = jnp.maximum(grb[4 * nh + h:4 * nh + h + 1, :], m_prev)
            w_t = jnp.exp2(jnp.where(seen_t, gcb[:, h:h + 1], NEG_INF) - m_row)
            r1 = _dot_nt(jnp.concatenate([k[:, hs], ct.astype(BF16)], axis=0), q[:, hs])
            s_t = (r1[0:L, :] * w_t).astype(BF16)
            return m_prev, ct, m_row, s_t, r1[L:, :]

        def head_finish(h, m_prev, ct, m_row, s_t, inter):
            hs = slice(h * M_HD, (h + 1) * M_HD)
            vext = jnp.concatenate([vt[hs, :], ones_rows], axis=0)
            g_r = grb[h:h + 1, :]
            b_r = grb[nh + h:nh + h + 1, :]
            b_last = grb[2 * nh + h:2 * nh + h + 1, 0:1]
            g_max = grb[3 * nh + h:3 * nh + h + 1, 0:1]
            w0 = jnp.exp2(m_prev - m_row)
            tot = _dot(vext, s_t) + w0 * inter
            den = tot[M_HD:M_HD + 1, :]
            floor = jnp.exp2(-(b_r + m_row))
            hcur[sub, hs, :] = tot[0:M_HD, :] / jnp.maximum(jnp.abs(den), floor)

            m_last = jnp.maximum(g_max, m_prev)
            wk = jnp.exp2(g_r - m_last)
            decay = jnp.exp2(m_prev - m_last)
            vw = (vext.astype(F32) * wk).astype(BF16)
            ct_scr[h] = decay * ct + _dot(vw, k[:, hs])
            mscr[h:h + 1, :] = jnp.broadcast_to(b_last + m_last, (1, LANES))

        pending = [head_scores(h) for h in range(min(SCAN_AHEAD, nh))]
        for h in range(nh):
            if h + SCAN_AHEAD < nh:
                pending.append(head_scores(h + SCAN_AHEAD))
            head_finish(h, *pending.pop(0))

    for j in range(n_sub):
        chunk_step(j + drn * (n_sub - 1 - 2 * j))

    @pl.when(drn == 0)
    def _park():
        hfwd[pl.ds(c * n_sub, n_sub)] = hcur[...]

    @pl.when(drn == 1)
    def _emit():
        first = (nc - 1 - c) * n_sub
        hsum_t = jnp.concatenate([hcur[j] + hfwd[first + j] for j in range(n_sub)], axis=1)
        hm = hsum_t.T * og_ref[0].astype(F32)
        y = _dot(hm.astype(wo_ref.dtype), wo_ref[...])
        x2 = x_ref[0] + mod_ref[0][2:3, :] * y
        ms = jnp.mean(x2 * x2, axis=-1, keepdims=True)
        y_ref[0] = x2 * lax.rsqrt(ms + EPS) * fw_ref[...]

    if write_state:
        @pl.when(c == nc - 1)
        def _final():
            for h in range(nh):
                cfin = ct_scr[h]
                cout_ref[0, 0, h] = cfin[0:M_HD, :].T
                nout_ref[0, 0, h:h + 1, :] = cfin[M_HD:M_HD + 1, :]
            mout_ref[0, 0] = mscr[...] * LN2


def _mlstm_scan(q, k, vt, gc, gr, og, x, mod3, mod_row, w_out, final_w, init, write_state):
    bsz, t, dm = q.shape
    d_model = x.shape[-1]
    L = MCHUNK
    n_sub = SCAN_SUB if (t // L) % SCAN_SUB == 0 else 1
    nc = t // (L * n_sub)
    rows = n_sub * L
    blk = lambda d, c: c + d * (nc - 1 - 2 * c)
    chunked = lambda b, d, c: (b, blk(d, c), 0, 0)
    gated = lambda b, d, c: (b, d, blk(d, c), 0, 0)
    tail = lambda b, d, c: (b, nc - 1 - d * c, 0)
    const = lambda b, d, c: (0, 0)
    in_specs = [pl.BlockSpec((1, n_sub, L, dm), chunked),
                pl.BlockSpec((1, n_sub, L, dm), chunked),
                pl.BlockSpec((1, n_sub, dm, L), chunked),
                pl.BlockSpec((1, 1, n_sub, L, gc.shape[-1]), gated),
                pl.BlockSpec((1, 1, n_sub, gr.shape[3], L), gated),
                pl.BlockSpec((1, rows, dm), tail),
                pl.BlockSpec((1, rows, d_model), tail),
                pl.BlockSpec((1, 3, d_model), lambda b, d, c: (mod_row(b), 0, 0)),
                pl.BlockSpec(w_out.shape, const, pipeline_mode=pl.Buffered(1)),
                pl.BlockSpec((1, d_model), const)]
    args = [q.reshape(bsz, t // L, L, dm), k.reshape(bsz, t // L, L, dm), vt,
            gc.reshape(bsz, 2, t // L, L, gc.shape[-1]), gr, og, x, mod3, w_out,
            final_w.reshape(1, d_model)]
    st = lambda b, d, c: (b, d, 0, 0)
    st5 = lambda b, d, c: (b, d, 0, 0, 0)
    if init is not None:
        c0, n0, m0 = init
        in_specs += [pl.BlockSpec((1, 1, M_HEADS, M_HD, M_HD), st5),
                     pl.BlockSpec((1, 1, M_HEADS, M_HD), st),
                     pl.BlockSpec((1, 1, M_HEADS, LANES), st)]
        args += [c0, n0, jnp.broadcast_to(m0[..., None], m0.shape + (LANES,))]
    out_shape = [jax.ShapeDtypeStruct((bsz, t, d_model), F32)]
    out_specs = [pl.BlockSpec((1, rows, d_model), tail)]
    if write_state:
        out_shape += [jax.ShapeDtypeStruct((bsz, 2, M_HEADS, M_HD, M_HD), F32),
                      jax.ShapeDtypeStruct((bsz, 2, M_HEADS, M_HD), F32),
                      jax.ShapeDtypeStruct((bsz, 2, M_HEADS, LANES), F32)]
        out_specs += [pl.BlockSpec((1, 1, M_HEADS, M_HD, M_HD), st5),
                      pl.BlockSpec((1, 1, M_HEADS, M_HD), st),
                      pl.BlockSpec((1, 1, M_HEADS, LANES), st)]
    return pl.pallas_call(
        functools.partial(_mlstm_scan_kernel, has_init=init is not None,
                          write_state=write_state, nc=nc),
        out_shape=tuple(out_shape),
        grid=(bsz, 2, nc),
        in_specs=in_specs,
        out_specs=tuple(out_specs),
        scratch_shapes=[pltpu.VMEM((M_HEADS, M_HD + BF16_ROWS, M_HD), F32),
                        pltpu.VMEM((M_HEADS, LANES), F32),
                        pltpu.VMEM((n_sub, dm, L), F32),
                        pltpu.VMEM((t // L, dm, L), F32)],
        compiler_params=_cparams(("parallel", "arbitrary", "arbitrary")),
        name="mlstm_scan",
    )(*args)


def _rope_tables(t):
    nf = HEAD_DIM // 4
    pos = jnp.arange(t)
    row = (pos // GRID_W).astype(F32)
    col = (pos % GRID_W).astype(F32)
    inv = ROPE_BASE ** (-jnp.arange(nf, dtype=F32) / nf)
    ar = row[:, None] * inv[None, :]
    ac = col[:, None] * inv[None, :]
    cos = jnp.concatenate([jnp.cos(ar), jnp.cos(ar), jnp.cos(ac), jnp.cos(ac)], axis=1)
    sin = jnp.concatenate([-jnp.sin(ar), jnp.sin(ar), -jnp.sin(ac), jnp.sin(ac)], axis=1)
    reps = LANES // HEAD_DIM
    return jnp.tile(cos, (1, reps)), jnp.tile(sin, (1, reps))


def kernel(x_prompt, x_sample, cache_k, cache_v, state_C, state_n, state_m, c, c_ctx,
           attn_norm_w, attn_ada_w, attn_ada_b, attn_w_in, attn_sink, attn_w_out,
           mlstm_norm_w, mlstm_ada_w, mlstm_ada_b, mlstm_w_in, mlstm_b_gates, mlstm_w_out,
           final_norm_w):
    assert attn_w_in.shape[0] == 1 and mlstm_w_in.shape[0] == 1, "one layer of each mixer"
    bsz, seq, d = x_prompt.shape
    dbsz, dseq, _ = x_sample.shape
    dkv = N_KV_HEADS * HEAD_DIM
    dm = M_HEADS * M_HD

    n_cond = 1 + dbsz
    cond = jnp.concatenate([c_ctx[None, :], c, jnp.zeros((-n_cond % 8, d), F32)], axis=0)
    attn_mod = _ada(cond, attn_ada_w[0], attn_ada_b[0]).reshape(-1, 3, d)
    mlstm_mod = _ada(cond, mlstm_ada_w[0], mlstm_ada_b[0]).reshape(-1, 3, d)
    ctx_row = lambda b: 0
    lat_row = lambda b: b + 1

    attn_w_in0 = attn_w_in[0]
    attn_w_out0 = attn_w_out[0]
    attn_wv_t = attn_w_in[0, :, 2 * N_HEADS * HEAD_DIM + dkv:].T
    mlstm_w_in_t = mlstm_w_in[0].T
    mlstm_w_out0 = mlstm_w_out[0]

    def mlstm_layer(x, mod_row, init, write_state, shared_cond):
        q, k, vt, og, gc, gr = _mlstm_in(x, mlstm_mod, mod_row, mlstm_norm_w[0], mlstm_w_in_t,
                                         mlstm_b_gates[0], shared_cond)
        outs = _mlstm_scan(q, k, vt, gc, gr, og, x, mlstm_mod, mod_row, mlstm_w_out0, final_norm_w,
                           init, write_state)
        return outs[0], outs[1:]

    q, sg, k_ctx, vt_ctx, v_ctx = _attn_in(x_prompt, attn_mod, ctx_row, attn_norm_w[0], attn_w_in0,
                                           attn_wv_t, None, F32, True, True)
    x1 = _attn(q, sg, x_prompt, attn_mod, ctx_row, k_ctx, vt_ctx, None, None, attn_sink[0], attn_w_out0)
    y_prompt, (c_fin, n_fin, m_fin) = mlstm_layer(x1, ctx_row, None, True, True)

    q, sg, k_lat, vt_lat = _attn_in(x_sample, attn_mod, lat_row, attn_norm_w[0], attn_w_in0,
                                    attn_wv_t, _rope_tables(dseq), BF16, False, False)
    kc = cache_k[:, 0].reshape(dbsz, -1, dkv).astype(BF16)
    vct = jnp.swapaxes(cache_v[:, 0].reshape(dbsz, -1, dkv), 1, 2).astype(BF16)
    x1 = _attn(q, sg, x_sample, attn_mod, lat_row, kc, vct, k_lat, vt_lat, attn_sink[0], attn_w_out0)
    y_sample, _ = mlstm_layer(x1, lat_row, (state_C[:, 0], state_n[:, 0], state_m[:, 0]), False, False)

    new_cache_k = k_ctx.reshape(bsz, 1, seq, N_KV_HEADS, HEAD_DIM)
    new_cache_v = v_ctx.reshape(bsz, 1, seq, N_KV_HEADS, HEAD_DIM)
    return (y_prompt, y_sample, new_cache_k, new_cache_v,
            c_fin[:, None], n_fin[:, None], m_fin[:, None, :, :, 0])
```

```python
import functools

import jax
import jax.numpy as jnp
from jax import lax
from jax.experimental import pallas as pl
from jax.experimental.pallas import tpu as pltpu

F32 = jnp.float32
BF16 = jnp.bfloat16

HEAD_DIM = 64
N_KV_HEADS = 4
GROUP = 4
N_HEADS = N_KV_HEADS * GROUP
QBLK = 128
GRID_W = 64
ROPE_BASE = 10000.0
M_HEADS = 8
M_HD = 128
EPS = 1e-6

LANES = 128
BF16_ROWS = 16
VMEM_LIMIT = 48 * 1024 * 1024
MLSTM_IN_VMEM_LIMIT = 60 * 1024 * 1024

MCHUNK = 256
ATTN_QB = 4
ATTN_WAVE_QB = 2
SCAN_SUB = 2
SCAN_AHEAD = 4
PROJ_ROWS = 1024
ADA_TILE = 512

NEG_INF = float("-inf")
LOG2E = 1.4426950408889634
LN2 = 0.6931471805599453


def _cparams(sem):
    return pltpu.CompilerParams(dimension_semantics=sem, vmem_limit_bytes=VMEM_LIMIT)


def _silu(x):
    return x * jax.nn.sigmoid(x)


def _log_sigmoid(x):
    return jnp.minimum(x, 0.0) - jnp.log1p(jnp.exp(-jnp.abs(x)))


def _dot(a, b):
    return jnp.dot(a, b, preferred_element_type=F32)


def _dot_nt(a, b):
    return lax.dot_general(a, b, (((1,), (1,)), ((), ())), preferred_element_type=F32)


def _split3(x):
    hi = x.astype(BF16)
    r = x - hi.astype(F32)
    mid = r.astype(BF16)
    lo = (r - mid.astype(F32)).astype(BF16)
    return hi, mid, lo


def _prenorm(x, norm_w, mod):
    ms = jnp.mean(x * x, axis=-1, keepdims=True)
    y = x * lax.rsqrt(ms + EPS) * norm_w
    return y * (1.0 + mod[1:2, :]) + mod[0:1, :]


def _ada_kernel(cond_ref, w_ref, b_ref, o_ref):
    a = _silu(cond_ref[...]).astype(BF16)
    o_ref[...] = _dot(a, w_ref[...].astype(BF16)) + b_ref[...]


def _ada(cond8, w, b):
    d, n = w.shape
    tn = ADA_TILE
    return pl.pallas_call(
        _ada_kernel,
        out_shape=jax.ShapeDtypeStruct((cond8.shape[0], n), F32),
        grid=(n // tn,),
        in_specs=[pl.BlockSpec(cond8.shape, lambda j: (0, 0)),
                  pl.BlockSpec((d, tn), lambda j: (0, j)),
                  pl.BlockSpec((1, tn), lambda j: (0, j))],
        out_specs=pl.BlockSpec((cond8.shape[0], tn), lambda j: (0, j)),
        compiler_params=_cparams(("parallel",)),
        name="ada_mod",
    )(cond8, w, b.reshape(1, n))


def _rope(x, cos, sin, lane):
    first = (lane & 31) < 16
    outs = []
    for c in range(x.shape[1] // LANES):
        xc = x[:, c * LANES:(c + 1) * LANES]
        sw = jnp.where(first, pltpu.roll(xc, LANES - 16, 1), pltpu.roll(xc, 16, 1))
        outs.append(xc * cos + sw * sin)
    return jnp.concatenate(outs, axis=1)


def _attn_in_kernel(*refs, rope, emit_v):
    refs = list(refs)
    x_ref, mod_ref, nw_ref, w_ref, wvt_ref = refs[:5]
    pos = 5
    if rope:
        cos_ref, sin_ref = refs[pos:pos + 2]
        pos += 2
    q_ref, sg_ref, k_ref, vt_ref = refs[pos:pos + 4]
    dq = q_ref.shape[-1]
    dkv = k_ref.shape[-1]
    n_req, tr, d = x_ref.shape
    rows = n_req * tr
    hb = _prenorm(x_ref[...].reshape(rows, d), nw_ref[...], mod_ref[0]).astype(w_ref.dtype)
    q = _dot(hb, w_ref[:, 0:dq])
    g = _dot(hb, w_ref[:, dq:2 * dq])
    k = _dot(hb, w_ref[:, 2 * dq:2 * dq + dkv])
    if rope:
        cos = cos_ref[...]
        sin = sin_ref[...]
        lane = lax.broadcasted_iota(jnp.int32, cos.shape, 1)
        q = _rope(q, cos, sin, lane)
        k = _rope(k, cos, sin, lane)
    q_ref[...] = (q * (HEAD_DIM ** -0.5 * LOG2E)).astype(q_ref.dtype).reshape(q_ref.shape)
    sg_ref[...] = _silu(g).astype(sg_ref.dtype).reshape(sg_ref.shape)
    k_ref[...] = k.astype(k_ref.dtype).reshape(k_ref.shape)
    vt = _dot_nt(wvt_ref[...], hb).astype(vt_ref.dtype)
    for r in range(n_req):
        vt_ref[r] = vt[:, r * tr:(r + 1) * tr]
    if emit_v:
        v_ref = refs[pos + 4]
        v = _dot(hb, w_ref[:, 2 * dq + dkv:2 * dq + 2 * dkv])
        v_ref[...] = v.astype(v_ref.dtype).reshape(v_ref.shape)


def _proj_tiling(bsz, t, shared_cond):
    tr = min(PROJ_ROWS, t)
    n_req = PROJ_ROWS // tr if shared_cond and bsz % (PROJ_ROWS // tr) == 0 else 1
    return tr, n_req


def _attn_in(x, mod3, mod_row, norm_w, w_in, wv_t, rope_tabs, k_dtype, emit_v, shared_cond):
    bsz, t, d = x.shape
    dq = N_HEADS * HEAD_DIM
    dkv = N_KV_HEADS * HEAD_DIM
    tm, n_req = _proj_tiling(bsz, t, shared_cond)
    rope = rope_tabs is not None
    tok = lambda b, i: (b, i, 0)
    const = lambda b, i: (0, 0)
    in_specs = [pl.BlockSpec((n_req, tm, d), tok),
                pl.BlockSpec((1, 3, d), lambda b, i: (mod_row(b), 0, 0)),
                pl.BlockSpec((1, d), const),
                pl.BlockSpec(w_in.shape, const),
                pl.BlockSpec(wv_t.shape, const)]
    args = [x, mod3, norm_w.reshape(1, d), w_in, wv_t]
    if rope:
        in_specs += [pl.BlockSpec((tm, LANES), lambda b, i: (i, 0))] * 2
        args += list(rope_tabs)
    out_shape = [jax.ShapeDtypeStruct((bsz, t, dq), BF16),
                 jax.ShapeDtypeStruct((bsz, t, dq), BF16),
                 jax.ShapeDtypeStruct((bsz, t, dkv), k_dtype),
                 jax.ShapeDtypeStruct((bsz, dkv, t), BF16)]
    out_specs = [pl.BlockSpec((n_req, tm, dq), tok), pl.BlockSpec((n_req, tm, dq), tok),
                 pl.BlockSpec((n_req, tm, dkv), tok),
                 pl.BlockSpec((n_req, dkv, tm), lambda b, i: (b, 0, i))]
    if emit_v:
        out_shape.append(jax.ShapeDtypeStruct((bsz, t, dkv), F32))
        out_specs.append(pl.BlockSpec((n_req, tm, dkv), tok))
    return pl.pallas_call(
        functools.partial(_attn_in_kernel, rope=rope, emit_v=emit_v),
        out_shape=tuple(out_shape),
        grid=(bsz // n_req, t // tm),
        in_specs=in_specs,
        out_specs=tuple(out_specs),
        compiler_params=_cparams(("parallel", "parallel")),
        name="attn_in_rope" if rope else "attn_in",
    )(*args)


def _attn_kernel(*refs, window, nb):
    if window:
        (q_ref, sg_ref, x_ref, mod_ref, kc_ref, vct_ref, kp_ref, km_ref, kn_ref,
         vpt_ref, vmt_ref, vnt_ref, sink_ref, wo_ref, o_ref, s_scr, p_scr, ot_scr) = refs
    else:
        q_ref, sg_ref, x_ref, mod_ref, kc_ref, vct_ref, sink_ref, wo_ref, o_ref, s_scr, p_scr, ot_scr = refs
    step = pl.program_id(1)
    nqb = q_ref.shape[1] // QBLK
    n_ctx = kc_ref.shape[1] // QBLK
    cols = GROUP * QBLK
    if window:
        kj = lax.broadcasted_iota(jnp.int32, (QBLK, cols), 0)
        qi = lax.broadcasted_iota(jnp.int32, (QBLK, cols), 1) & (QBLK - 1)
        after_diag = kj >= qi
        before_diag = kj <= qi
    ones_rows = jnp.where(lax.broadcasted_iota(jnp.int32, (BF16_ROWS, QBLK), 0) == 0, 1.0, 0.0).astype(BF16)
    n_blk = n_ctx + (3 if window else 0)

    def window_blocks(qb, cs, kp, km, kn, lanes):
        def mid(j):
            sl = slice(j * QBLK, (j + 1) * QBLK)
            return km[0, cs, sl] if lanes else km[0, sl, cs]
        first = kp[0, cs, :] if lanes else kp[0][:, cs]
        last = kn[0, cs, :] if lanes else kn[0][:, cs]
        return [first if qb == 0 else mid(qb - 1), mid(qb), last if qb == nqb - 1 else mid(qb + 1)]

    def block_masks(qb):
        if not window:
            return [None] * n_ctx
        prev_ok = after_diag & (step > 0) if qb == 0 else after_diag
        next_ok = before_diag & (step < nb // nqb - 1) if qb == nqb - 1 else before_diag
        return [None] * n_ctx + [prev_ok, None, next_ok]

    def scores(qb, kvh):
        u = (qb * N_KV_HEADS + kvh) % s_scr.shape[0]
        cs = slice(kvh * HEAD_DIM, (kvh + 1) * HEAD_DIM)
        heads = [kvh * GROUP + j for j in range(GROUP)]
        qq = q_ref[0, qb * QBLK:(qb + 1) * QBLK, :]
        q4 = jnp.concatenate([qq[:, h * HEAD_DIM:(h + 1) * HEAD_DIM] for h in heads], axis=0)
        sink_row = jnp.concatenate(
            [jnp.broadcast_to(sink_ref[0:1, h:h + 1], (1, QBLK)) for h in heads], axis=1) * LOG2E
        keys = [kc_ref[0, j * QBLK:(j + 1) * QBLK, cs].astype(BF16) for j in range(n_ctx)]
        if window:
            keys += window_blocks(qb, cs, kp_ref, km_ref, kn_ref, False)
        st_all = _dot_nt(jnp.concatenate(keys, axis=0), q4)
        macc = jnp.full((8, cols), NEG_INF, F32)
        for j, ok in enumerate(block_masks(qb)):
            s_blk = st_all[j * QBLK:(j + 1) * QBLK, :]
            if ok is not None:
                s_blk = jnp.where(ok, s_blk, NEG_INF)
            s_scr[u, j] = s_blk
            macc = jnp.maximum(macc, jnp.max(s_blk.reshape(QBLK // 8, 8, cols), axis=0))
        return jnp.maximum(jnp.max(macc, axis=0, keepdims=True), sink_row), sink_row

    def weighted_values(qb, kvh, m_row, sink_row):
        u = (qb * N_KV_HEADS + kvh) % s_scr.shape[0]
        cs = slice(kvh * HEAD_DIM, (kvh + 1) * HEAD_DIM)
        for j in range(n_blk):
            p_scr[u, j * QBLK:(j + 1) * QBLK, :] = jnp.exp2(s_scr[u, j] - m_row).astype(BF16)
        vts = [vct_ref[0, cs, j * QBLK:(j + 1) * QBLK] for j in range(n_ctx)]
        if window:
            vts += window_blocks(qb, cs, vpt_ref, vmt_ref, vnt_ref, True)
        vt_ext = jnp.concatenate(
            [jnp.concatenate(vts, axis=1), jnp.tile(ones_rows, (1, n_blk))], axis=0)
        acc = _dot(vt_ext, p_scr[u])
        den = acc[HEAD_DIM:HEAD_DIM + 1, :] + jnp.exp2(sink_row - m_row)
        o_t = acc[0:HEAD_DIM, :] / den
        for j in range(GROUP):
            h = kvh * GROUP + j
            ot_scr[h * HEAD_DIM:(h + 1) * HEAD_DIM, qb * QBLK:(qb + 1) * QBLK] = o_t[:, j * QBLK:(j + 1) * QBLK]

    units = [(qb, kvh) for qb in range(nqb) for kvh in range(N_KV_HEADS)]
    wave = s_scr.shape[0]
    for w0 in range(0, len(units), wave):
        stats = [scores(qb, kvh) for qb, kvh in units[w0:w0 + wave]]
        for (qb, kvh), st in zip(units[w0:w0 + wave], stats):
            weighted_values(qb, kvh, *st)
    z = (ot_scr[...].T * sg_ref[0].astype(F32)).astype(wo_ref.dtype)
    y = _dot(z, wo_ref[...])
    o_ref[0] = x_ref[0] + mod_ref[0][2:3, :] * y


def _attn(q, sg, x, mod3, mod_row, kc, vct, k_lat, vt_lat, sink, w_out):
    bsz, t, d = x.shape
    dq = q.shape[-1]
    dkv = kc.shape[-1]
    p_len = kc.shape[1]
    nb = t // QBLK
    nqb = min(ATTN_QB, nb)
    rows = nqb * QBLK
    window = k_lat is not None
    tok = lambda b, i: (b, i, 0)
    in_specs = [pl.BlockSpec((1, rows, dq), tok),
                pl.BlockSpec((1, rows, dq), tok),
                pl.BlockSpec((1, rows, d), tok),
                pl.BlockSpec((1, 3, d), lambda b, i: (mod_row(b), 0, 0)),
                pl.BlockSpec((1, p_len, dkv), lambda b, i: (b, 0, 0)),
                pl.BlockSpec((1, dkv, p_len), lambda b, i: (b, 0, 0))]
    args = [q, sg, x, mod3, kc, vct]
    n_blocks = p_len // QBLK
    if window:
        prev = lambda i: jnp.maximum(i * nqb - 1, 0)
        nxt = lambda i: jnp.minimum((i + 1) * nqb, nb - 1)
        in_specs += [pl.BlockSpec((1, QBLK, dkv), lambda b, i: (b, prev(i), 0)),
                     pl.BlockSpec((1, rows, dkv), tok),
                     pl.BlockSpec((1, QBLK, dkv), lambda b, i: (b, nxt(i), 0)),
                     pl.BlockSpec((1, dkv, QBLK), lambda b, i: (b, 0, prev(i))),
                     pl.BlockSpec((1, dkv, rows), lambda b, i: (b, 0, i)),
                     pl.BlockSpec((1, dkv, QBLK), lambda b, i: (b, 0, nxt(i)))]
        args += [k_lat] * 3 + [vt_lat] * 3
        n_blocks += 3
    in_specs += [pl.BlockSpec((1, N_HEADS), lambda b, i: (0, 0)),
                 pl.BlockSpec(w_out.shape, lambda b, i: (0, 0), pipeline_mode=pl.Buffered(1))]
    args += [sink.reshape(1, N_HEADS), w_out]
    units = min(nqb, ATTN_WAVE_QB) * N_KV_HEADS
    return pl.pallas_call(
        functools.partial(_attn_kernel, window=window, nb=nb),
        out_shape=jax.ShapeDtypeStruct((bsz, t, d), F32),
        grid=(bsz, nb // nqb),
        in_specs=in_specs,
        out_specs=pl.BlockSpec((1, rows, d), tok),
        scratch_shapes=[pltpu.VMEM((units, n_blocks, QBLK, GROUP * QBLK), F32),
                        pltpu.VMEM((units, n_blocks * QBLK, GROUP * QBLK), BF16),
                        pltpu.VMEM((dq, rows), F32)],
        compiler_params=_cparams(("parallel", "parallel")),
        name="attn_window" if window else "attn_ctx",
    )(*args)


def _mlstm_in_kernel(x_ref, mod_ref, nw_ref, wt_ref, bgt_ref,
                     q_ref, k_ref, vt_ref, og_ref, gc_ref, gr_ref):
    dm = q_ref.shape[-1]
    nh = M_HEADS
    L = MCHUNK
    n_req, tr, d = x_ref.shape
    per_req = tr // L
    hb = _prenorm(x_ref[...].reshape(n_req * tr, d), nw_ref[...], mod_ref[0]).astype(wt_ref.dtype)

    gr = _dot_nt(wt_ref[5 * dm:, :], hb) + bgt_ref[...]
    n_chunks = n_req * per_req
    ri = lax.broadcasted_iota(jnp.int32, (L, L), 0)
    ci = lax.broadcasted_iota(jnp.int32, (L, L), 1)
    lane = lax.broadcasted_iota(jnp.int32, (n_chunks * nh, L), 1)
    g_rows = []
    for dr in range(2):
        before = (ri <= ci) if dr == 0 else (ri >= ci)
        tri = jnp.where(before, 1.0, 0.0).astype(BF16)
        base = dr * 2 * nh
        lf = _log_sigmoid(gr[base + nh:base + 2 * nh, :]) * LOG2E
        gi = gr[base:base + nh, :] * LOG2E
        lf_st = jnp.concatenate([lf[:, c * L:(c + 1) * L] for c in range(n_chunks)], axis=0)
        b_st = sum(_dot(piece, tri) for piece in _split3(lf_st))
        g_st = jnp.concatenate([gi[:, c * L:(c + 1) * L] for c in range(n_chunks)], axis=0) - b_st
        run = g_st
        step = 1
        while step < L:
            if dr == 0:
                run = jnp.where(lane >= step, jnp.maximum(run, pltpu.roll(run, step, 1)), run)
            else:
                run = jnp.where(lane < L - step, jnp.maximum(run, pltpu.roll(run, L - step, 1)), run)
            step *= 2
        for cidx in range(n_chunks):
            rows = slice(cidx * L, (cidx + 1) * L)
            blk = slice(cidx * nh, (cidx + 1) * nh)
            b_last = jnp.sum(lf[:, rows], axis=1, keepdims=True)
            g_max = jnp.max(g_st[blk, :], axis=1, keepdims=True)
            g_rows.append(g_st[blk, :])
            gr_ref[cidx // per_req, dr, cidx % per_req] = jnp.concatenate(
                [g_st[blk, :], b_st[blk, :], jnp.broadcast_to(b_last, (nh, L)),
                 jnp.broadcast_to(g_max, (nh, L)), run[blk, :]], axis=0)
    g_sq = jnp.concatenate(g_rows + [jnp.zeros((L - len(g_rows) * nh, L), F32)], axis=0).T
    for dr in range(2):
        for cidx in range(n_chunks):
            idx = dr * n_chunks + cidx
            lc = cidx % per_req
            gc_ref[cidx // per_req, dr, lc * L:(lc + 1) * L, :] = g_sq[:, idx * nh:(idx + 1) * nh]

    o = _dot_nt(hb, wt_ref[3 * dm:4 * dm, :])
    g = _dot_nt(hb, wt_ref[4 * dm:5 * dm, :])
    og_ref[...] = (jax.nn.sigmoid(o) * _silu(g)).astype(og_ref.dtype).reshape(og_ref.shape)
    q_ref[...] = _dot_nt(hb, wt_ref[0:dm, :]).astype(q_ref.dtype).reshape(q_ref.shape)
    k = _dot_nt(hb, wt_ref[dm:2 * dm, :]) * (M_HD ** -0.5)
    k_ref[...] = k.astype(k_ref.dtype).reshape(k_ref.shape)
    vt = _dot_nt(wt_ref[2 * dm:3 * dm, :], hb).astype(vt_ref.dtype)
    for cidx in range(n_chunks):
        vt_ref[cidx // per_req, cidx % per_req] = vt[:, cidx * L:(cidx + 1) * L]


def _mlstm_in(x, mod3, mod_row, norm_w, w_t, b_gates, shared_cond):
    bsz, t, d = x.shape
    dm = M_HEADS * M_HD
    ng = 4 * M_HEADS
    tm, n_req = _proj_tiling(bsz, t, shared_cond)
    tok = lambda b, i: (b, i, 0)
    const = lambda b, i: (0, 0)
    big = jax.ShapeDtypeStruct((bsz, t, dm), BF16)
    once = pl.Buffered(1)
    return pl.pallas_call(
        _mlstm_in_kernel,
        out_shape=(big, big, jax.ShapeDtypeStruct((bsz, t // MCHUNK, dm, MCHUNK), BF16), big,
                   jax.ShapeDtypeStruct((bsz, 2, t, M_HEADS), F32),
                   jax.ShapeDtypeStruct((bsz, 2, t // MCHUNK, 5 * M_HEADS, MCHUNK), F32)),
        grid=(bsz // n_req, t // tm),
        in_specs=[pl.BlockSpec((n_req, tm, d), tok),
                  pl.BlockSpec((1, 3, d), lambda b, i: (mod_row(b), 0, 0)),
                  pl.BlockSpec((1, d), const),
                  pl.BlockSpec(w_t.shape, const, pipeline_mode=once),
                  pl.BlockSpec((ng, 1), const)],
        out_specs=(pl.BlockSpec((n_req, tm, dm), tok), pl.BlockSpec((n_req, tm, dm), tok),
                   pl.BlockSpec((n_req, tm // MCHUNK, dm, MCHUNK), lambda b, i: (b, i, 0, 0)),
                   pl.BlockSpec((n_req, tm, dm), tok),
                   pl.BlockSpec((n_req, 2, tm, M_HEADS), lambda b, i: (b, 0, i, 0)),
                   pl.BlockSpec((n_req, 2, tm // MCHUNK, 5 * M_HEADS, MCHUNK),
                                lambda b, i: (b, 0, i, 0, 0))),
        compiler_params=pltpu.CompilerParams(dimension_semantics=("parallel", "parallel"),
                                             vmem_limit_bytes=MLSTM_IN_VMEM_LIMIT),
        name="mlstm_in",
    )(x, mod3, norm_w.reshape(1, d), w_t, b_gates.reshape(ng, 1))


def _mlstm_scan_kernel(*refs, has_init, write_state, nc):
    refs = list(refs)
    q_ref, k_ref, vt_ref, gc_ref, gr_ref, og_ref, x_ref, mod_ref, wo_ref, fw_ref = refs[:10]
    pos = 10
    if has_init:
        c0_ref, n0_ref, m0_ref = refs[pos:pos + 3]
        pos += 3
    y_ref = refs[pos]
    pos += 1
    if write_state:
        cout_ref, nout_ref, mout_ref = refs[pos:pos + 3]
        pos += 3
    ct_scr, mscr, hcur, hfwd = refs[pos:pos + 4]

    drn = pl.program_id(1)
    c = pl.program_id(2)
    n_sub, L = q_ref.shape[1], q_ref.shape[2]
    nh = M_HEADS
    pad = ct_scr.shape[1] - M_HD

    @pl.when(c == 0)
    def _init():
        if has_init:
            for h in range(nh):
                ct_scr[h, 0:M_HD, :] = c0_ref[0, 0, h].T
                ct_scr[h, M_HD:M_HD + pad, :] = jnp.concatenate(
                    [n0_ref[0, 0, h:h + 1, :], jnp.zeros((pad - 1, M_HD), F32)], axis=0)
            mscr[...] = m0_ref[0, 0] * LOG2E
        else:
            ct_scr[...] = jnp.zeros(ct_scr.shape, F32)
            mscr[...] = jnp.zeros(mscr.shape, F32)

    ones_rows = jnp.where(lax.broadcasted_iota(jnp.int32, (pad, L), 0) == 0, 1.0, 0.0).astype(BF16)
    H = L // 2
    si = lax.broadcasted_iota(jnp.int32, (H, H), 0)
    li = lax.broadcasted_iota(jnp.int32, (H, H), 1)

    def chunk_step(sub, fwd):
        gcb = gc_ref[0, 0, sub]
        grb = gr_ref[0, 0, sub]
        q = q_ref[0, sub]
        k = k_ref[0, sub]
        vt = vt_ref[0, sub]
        tri = (si <= li) if fwd else (si >= li)
        lo, hi = slice(0, H), slice(H, L)

        def head_scores(h):
            hs = slice(h * M_HD, (h + 1) * M_HD)
            m_prev = mscr[h:h + 1, 0:1]
            ct = ct_scr[h]
            m_row = jnp.maximum(grb[4 * nh + h:4 * nh + h + 1, :], m_prev)
            r1 = _dot_nt(jnp.concatenate([k[:, hs], ct.astype(BF16)], axis=0), q[:, hs])
            g_c = gcb[:, h:h + 1]

            def quad(ks, qs, masked):
                e = g_c[ks, :] - m_row[:, qs]
                if masked:
                    e = jnp.where(tri, e, NEG_INF)
                return (r1[ks, qs] * jnp.exp2(e)).astype(BF16)

            zero = jnp.zeros((H, H), BF16)
            if fwd:
                s_t = jnp.concatenate(
                    [jnp.concatenate([quad(lo, lo, True), quad(lo, hi, False)], axis=1),
                     jnp.concatenate([zero, quad(hi, hi, True)], axis=1)], axis=0)
            else:
                s_t = jnp.concatenate(
                    [jnp.concatenate([quad(lo, lo, True), zero], axis=1),
                     jnp.concatenate([quad(hi, lo, False), quad(hi, hi, True)], axis=1)], axis=0)
            return m_prev, ct, m_row, s_t, r1[L:, :]

        def head_finish(h, m_prev, ct, m_row, s_t, inter):
            hs = slice(h * M_HD, (h + 1) * M_HD)
            vext = jnp.concatenate([vt[hs, :], ones_rows], axis=0)
            g_r = grb[h:h + 1, :]
            b_r = grb[nh + h:nh + h + 1, :]
            b_last = grb[2 * nh + h:2 * nh + h + 1, 0:1]
            g_max = grb[3 * nh + h:3 * nh + h + 1, 0:1]
            w0 = jnp.exp2(m_prev - m_row)
            tot = _dot(vext, s_t) + w0 * inter
            den = tot[M_HD:M_HD + 1, :]
            floor = jnp.exp2(-(b_r + m_row))
            hcur[sub, hs, :] = tot[0:M_HD, :] / jnp.maximum(jnp.abs(den), floor)

            m_last = jnp.maximum(g_max, m_prev)
            wk = jnp.exp2(g_r - m_last)
            decay = jnp.exp2(m_prev - m_last)
            vw = (vext.astype(F32) * wk).astype(BF16)
            ct_scr[h] = decay * ct + _dot(vw, k[:, hs])
            mscr[h:h + 1, :] = jnp.broadcast_to(b_last + m_last, (1, LANES))

        pending = [head_scores(h) for h in range(min(SCAN_AHEAD, nh))]
        for h in range(nh):
            if h + SCAN_AHEAD < nh:
                pending.append(head_scores(h + SCAN_AHEAD))
            head_finish(h, *pending.pop(0))

    @pl.when(drn == 0)
    def _forward():
        for j in range(n_sub):
            chunk_step(j, True)

    @pl.when(drn == 1)
    def _backward():
        for j in reversed(range(n_sub)):
            chunk_step(j, False)

    @pl.when(drn == 0)
    def _park():
        hfwd[pl.ds(c * n_sub, n_sub)] = hcur[...]

    @pl.when(drn == 1)
    def _emit():
        first = (nc - 1 - c) * n_sub
        hsum_t = jnp.concatenate([hcur[j] + hfwd[first + j] for j in range(n_sub)], axis=1)
        hm = hsum_t.T * og_ref[0].astype(F32)
        y = _dot(hm.astype(wo_ref.dtype), wo_ref[...])
        x2 = x_ref[0] + mod_ref[0][2:3, :] * y
        ms = jnp.mean(x2 * x2, axis=-1, keepdims=True)
        y_ref[0] = x2 * lax.rsqrt(ms + EPS) * fw_ref[...]

    if write_state:
        @pl.when(c == nc - 1)
        def _final():
            for h in range(nh):
                cfin = ct_scr[h]
                cout_ref[0, 0, h] = cfin[0:M_HD, :].T
                nout_ref[0, 0, h:h + 1, :] = cfin[M_HD:M_HD + 1, :]
            mout_ref[0, 0] = mscr[...] * LN2


def _mlstm_scan(q, k, vt, gc, gr, og, x, mod3, mod_row, w_out, final_w, init, write_state):
    bsz, t, dm = q.shape
    d_model = x.shape[-1]
    L = MCHUNK
    n_sub = SCAN_SUB if (t // L) % SCAN_SUB == 0 else 1
    nc = t // (L * n_sub)
    rows = n_sub * L
    blk = lambda d, c: c + d * (nc - 1 - 2 * c)
    chunked = lambda b, d, c: (b, blk(d, c), 0, 0)
    gated = lambda b, d, c: (b, d, blk(d, c), 0, 0)
    tail = lambda b, d, c: (b, nc - 1 - d * c, 0)
    const = lambda b, d, c: (0, 0)
    in_specs = [pl.BlockSpec((1, n_sub, L, dm), chunked),
                pl.BlockSpec((1, n_sub, L, dm), chunked),
                pl.BlockSpec((1, n_sub, dm, L), chunked),
                pl.BlockSpec((1, 1, n_sub, L, gc.shape[-1]), gated),
                pl.BlockSpec((1, 1, n_sub, gr.shape[3], L), gated),
                pl.BlockSpec((1, rows, dm), tail),
                pl.BlockSpec((1, rows, d_model), tail),
                pl.BlockSpec((1, 3, d_model), lambda b, d, c: (mod_row(b), 0, 0)),
                pl.BlockSpec(w_out.shape, const, pipeline_mode=pl.Buffered(1)),
                pl.BlockSpec((1, d_model), const)]
    args = [q.reshape(bsz, t // L, L, dm), k.reshape(bsz, t // L, L, dm), vt,
            gc.reshape(bsz, 2, t // L, L, gc.shape[-1]), gr, og, x, mod3, w_out,
            final_w.reshape(1, d_model)]
    st = lambda b, d, c: (b, d, 0, 0)
    st5 = lambda b, d, c: (b, d, 0, 0, 0)
    if init is not None:
        c0, n0, m0 = init
        in_specs += [pl.BlockSpec((1, 1, M_HEADS, M_HD, M_HD), st5),
                     pl.BlockSpec((1, 1, M_HEADS, M_HD), st),
                     pl.BlockSpec((1, 1, M_HEADS, LANES), st)]
        args += [c0, n0, jnp.broadcast_to(m0[..., None], m0.shape + (LANES,))]
    out_shape = [jax.ShapeDtypeStruct((bsz, t, d_model), F32)]
    out_specs = [pl.BlockSpec((1, rows, d_model), tail)]
    if write_state:
        out_shape += [jax.ShapeDtypeStruct((bsz, 2, M_HEADS, M_HD, M_HD), F32),
                      jax.ShapeDtypeStruct((bsz, 2, M_HEADS, M_HD), F32),
                      jax.ShapeDtypeStruct((bsz, 2, M_HEADS, LANES), F32)]
        out_specs += [pl.BlockSpec((1, 1, M_HEADS, M_HD, M_HD), st5),
                      pl.BlockSpec((1, 1, M_HEADS, M_HD), st),
                      pl.BlockSpec((1, 1, M_HEADS, LANES), st)]
    return pl.pallas_call(
        functools.partial(_mlstm_scan_kernel, has_init=init is not None,
                          write_state=write_state, nc=nc),
        out_shape=tuple(out_shape),
        grid=(bsz, 2, nc),
        in_specs=in_specs,
        out_specs=tuple(out_specs),
        scratch_shapes=[pltpu.VMEM((M_HEADS, M_HD + BF16_ROWS, M_HD), F32),
                        pltpu.VMEM((M_HEADS, LANES), F32),
                        pltpu.VMEM((n_sub, dm, L), F32),
                        pltpu.VMEM((t // L, dm, L), F32)],
        compiler_params=_cparams(("parallel", "arbitrary", "arbitrary")),
        name="mlstm_scan",
    )(*args)


def _rope_tables(t):
    nf = HEAD_DIM // 4
    pos = jnp.arange(t)
    row = (pos // GRID_W).astype(F32)
    col = (pos % GRID_W).astype(F32)
    inv = ROPE_BASE ** (-jnp.arange(nf, dtype=F32) / nf)
    ar = row[:, None] * inv[None, :]
    ac = col[:, None] * inv[None, :]
    cos = jnp.concatenate([jnp.cos(ar), jnp.cos(ar), jnp.cos(ac), jnp.cos(ac)], axis=1)
    sin = jnp.concatenate([-jnp.sin(ar), jnp.sin(ar), -jnp.sin(ac), jnp.sin(ac)], axis=1)
    reps = LANES // HEAD_DIM
    return jnp.tile(cos, (1, reps)), jnp.tile(sin, (1, reps))


def kernel(x_prompt, x_sample, cache_k, cache_v, state_C, state_n, state_m, c, c_ctx,
           attn_norm_w, attn_ada_w, attn_ada_b, attn_w_in, attn_sink, attn_w_out,
           mlstm_norm_w, mlstm_ada_w, mlstm_ada_b, mlstm_w_in, mlstm_b_gates, mlstm_w_out,
           final_norm_w):
    assert attn_w_in.shape[0] == 1 and mlstm_w_in.shape[0] == 1, "one layer of each mixer"
    bsz, seq, d = x_prompt.shape
    dbsz, dseq, _ = x_sample.shape
    dkv = N_KV_HEADS * HEAD_DIM
    dm = M_HEADS * M_HD

    n_cond = 1 + dbsz
    cond = jnp.concatenate([c_ctx[None, :], c, jnp.zeros((-n_cond % 8, d), F32)], axis=0)
    attn_mod = _ada(cond, attn_ada_w[0], attn_ada_b[0]).reshape(-1, 3, d)
    mlstm_mod = _ada(cond, mlstm_ada_w[0], mlstm_ada_b[0]).reshape(-1, 3, d)
    ctx_row = lambda b: 0
    lat_row = lambda b: b + 1

    attn_w_in0 = attn_w_in[0]
    attn_w_out0 = attn_w_out[0]
    attn_wv_t = attn_w_in[0, :, 2 * N_HEADS * HEAD_DIM + dkv:].T
    mlstm_w_in_t = mlstm_w_in[0].T
    mlstm_w_out0 = mlstm_w_out[0]

    def mlstm_layer(x, mod_row, init, write_state, shared_cond):
        q, k, vt, og, gc, gr = _mlstm_in(x, mlstm_mod, mod_row, mlstm_norm_w[0], mlstm_w_in_t,
                                         mlstm_b_gates[0], shared_cond)
        outs = _mlstm_scan(q, k, vt, gc, gr, og, x, mlstm_mod, mod_row, mlstm_w_out0, final_norm_w,
                           init, write_state)
        return outs[0], outs[1:]

    q, sg, k_ctx, vt_ctx, v_ctx = _attn_in(x_prompt, attn_mod, ctx_row, attn_norm_w[0], attn_w_in0,
                                           attn_wv_t, None, F32, True, True)
    x1 = _attn(q, sg, x_prompt, attn_mod, ctx_row, k_ctx, vt_ctx, None, None, attn_sink[0], attn_w_out0)
    y_prompt, (c_fin, n_fin, m_fin) = mlstm_layer(x1, ctx_row, None, True, True)

    q, sg, k_lat, vt_lat = _attn_in(x_sample, attn_mod, lat_row, attn_norm_w[0], attn_w_in0,
                                    attn_wv_t, _rope_tables(dseq), BF16, False, False)
    kc = cache_k[:, 0].reshape(dbsz, -1, dkv).astype(BF16)
    vct = jnp.swapaxes(cache_v[:, 0].reshape(dbsz, -1, dkv), 1, 2).astype(BF16)
    x1 = _attn(q, sg, x_sample, attn_mod, lat_row, kc, vct, k_lat, vt_lat, attn_sink[0], attn_w_out0)
    y_sample, _ = mlstm_layer(x1, lat_row, (state_C[:, 0], state_n[:, 0], state_m[:, 0]), False, False)

    new_cache_k = k_ctx.reshape(bsz, 1, seq, N_KV_HEADS, HEAD_DIM)
    new_cache_v = v_ctx.reshape(bsz, 1, seq, N_KV_HEADS, HEAD_DIM)
    return (y_prompt, y_sample, new_cache_k, new_cache_v,
            c_fin[:, None], n_fin[:, None], m_fin[:, None, :, :, 0])
```

```python
import functools

import jax
import jax.numpy as jnp
from jax import lax
from jax.experimental import pallas as pl
from jax.experimental.pallas import tpu as pltpu

F32 = jnp.float32
BF16 = jnp.bfloat16

HEAD_DIM = 64
N_KV_HEADS = 4
GROUP = 4
N_HEADS = N_KV_HEADS * GROUP
QBLK = 128
GRID_W = 64
ROPE_BASE = 10000.0
M_HEADS = 8
M_HD = 128
EPS = 1e-6

LANES = 128
BF16_ROWS = 16
VMEM_LIMIT = 48 * 1024 * 1024
MLSTM_IN_VMEM_LIMIT = 60 * 1024 * 1024

MCHUNK = 256
ATTN_QB = 4
ATTN_WAVE_QB = 2
SCAN_SUB = 2
SCAN_AHEAD = 4
PROJ_ROWS = 1024
ADA_TILE = 1024

NEG_INF = float("-inf")
LOG2E = 1.4426950408889634
LN2 = 0.6931471805599453


def _cparams(sem):
    return pltpu.CompilerParams(dimension_semantics=sem, vmem_limit_bytes=VMEM_LIMIT)


def _silu(x):
    return x * jax.nn.sigmoid(x)


def _log_sigmoid(x):
    return jnp.minimum(x, 0.0) - jnp.log1p(jnp.exp(-jnp.abs(x)))


def _dot(a, b):
    return jnp.dot(a, b, preferred_element_type=F32)


def _dot_nt(a, b):
    return lax.dot_general(a, b, (((1,), (1,)), ((), ())), preferred_element_type=F32)


def _split3(x):
    hi = x.astype(BF16)
    r = x - hi.astype(F32)
    mid = r.astype(BF16)
    lo = (r - mid.astype(F32)).astype(BF16)
    return hi, mid, lo


def _prenorm(x, norm_w, mod):
    ms = jnp.mean(x * x, axis=-1, keepdims=True)
    y = x * lax.rsqrt(ms + EPS) * norm_w
    return y * (1.0 + mod[1:2, :]) + mod[0:1, :]


def _ada_kernel(cond_ref, w_ref, b_ref, o_ref):
    a = _silu(cond_ref[...]).astype(BF16)
    o_ref[...] = _dot(a, w_ref[...].astype(BF16)) + b_ref[...]


def _ada(cond8, w, b):
    d, n = w.shape
    tn = ADA_TILE
    return pl.pallas_call(
        _ada_kernel,
        out_shape=jax.ShapeDtypeStruct((cond8.shape[0], n), F32),
        grid=(n // tn,),
        in_specs=[pl.BlockSpec(cond8.shape, lambda j: (0, 0)),
                  pl.BlockSpec((d, tn), lambda j: (0, j)),
                  pl.BlockSpec((1, tn), lambda j: (0, j))],
        out_specs=pl.BlockSpec((cond8.shape[0], tn), lambda j: (0, j)),
        compiler_params=_cparams(("parallel",)),
        name="ada_mod",
    )(cond8, w, b.reshape(1, n))


def _rope(x, cos, sin, lane):
    first = (lane & 31) < 16
    outs = []
    for c in range(x.shape[1] // LANES):
        xc = x[:, c * LANES:(c + 1) * LANES]
        sw = jnp.where(first, pltpu.roll(xc, LANES - 16, 1), pltpu.roll(xc, 16, 1))
        outs.append(xc * cos + sw * sin)
    return jnp.concatenate(outs, axis=1)


def _attn_in_kernel(*refs, rope, emit_v):
    refs = list(refs)
    x_ref, mod_ref, nw_ref, w_ref, wvt_ref = refs[:5]
    pos = 5
    if rope:
        cos_ref, sin_ref = refs[pos:pos + 2]
        pos += 2
    q_ref, sg_ref, k_ref, vt_ref = refs[pos:pos + 4]
    dq = q_ref.shape[-1]
    dkv = k_ref.shape[-1]
    n_req, tr, d = x_ref.shape
    rows = n_req * tr
    hb = _prenorm(x_ref[...].reshape(rows, d), nw_ref[...], mod_ref[0]).astype(w_ref.dtype)
    q = _dot(hb, w_ref[:, 0:dq])
    g = _dot(hb, w_ref[:, dq:2 * dq])
    k = _dot(hb, w_ref[:, 2 * dq:2 * dq + dkv])
    if rope:
        cos = cos_ref[...]
        sin = sin_ref[...]
        lane = lax.broadcasted_iota(jnp.int32, cos.shape, 1)
        q = _rope(q, cos, sin, lane)
        k = _rope(k, cos, sin, lane)
    q_ref[...] = (q * (HEAD_DIM ** -0.5 * LOG2E)).astype(q_ref.dtype).reshape(q_ref.shape)
    sg_ref[...] = _silu(g).astype(sg_ref.dtype).reshape(sg_ref.shape)
    k_ref[...] = k.astype(k_ref.dtype).reshape(k_ref.shape)
    vt = _dot_nt(wvt_ref[...], hb).astype(vt_ref.dtype)
    for r in range(n_req):
        vt_ref[r] = vt[:, r * tr:(r + 1) * tr]
    if emit_v:
        v_ref = refs[pos + 4]
        v = _dot(hb, w_ref[:, 2 * dq + dkv:2 * dq + 2 * dkv])
        v_ref[...] = v.astype(v_ref.dtype).reshape(v_ref.shape)


def _proj_tiling(bsz, t, shared_cond):
    tr = min(PROJ_ROWS, t)
    assert t % tr == 0 and tr % MCHUNK == 0, (t, tr)
    n_req = PROJ_ROWS // tr if shared_cond and bsz % (PROJ_ROWS // tr) == 0 else 1
    return tr, n_req


def _attn_in(x, mod3, mod_row, norm_w, w_in, wv_t, rope_tabs, k_dtype, emit_v, shared_cond):
    bsz, t, d = x.shape
    dq = N_HEADS * HEAD_DIM
    dkv = N_KV_HEADS * HEAD_DIM
    tm, n_req = _proj_tiling(bsz, t, shared_cond)
    rope = rope_tabs is not None
    tok = lambda b, i: (b, i, 0)
    const = lambda b, i: (0, 0)
    in_specs = [pl.BlockSpec((n_req, tm, d), tok),
                pl.BlockSpec((1, 3, d), lambda b, i: (mod_row(b), 0, 0)),
                pl.BlockSpec((1, d), const),
                pl.BlockSpec(w_in.shape, const),
                pl.BlockSpec(wv_t.shape, const)]
    args = [x, mod3, norm_w.reshape(1, d), w_in, wv_t]
    if rope:
        in_specs += [pl.BlockSpec((tm, LANES), lambda b, i: (i, 0))] * 2
        args += list(rope_tabs)
    out_shape = [jax.ShapeDtypeStruct((bsz, t, dq), BF16),
                 jax.ShapeDtypeStruct((bsz, t, dq), BF16),
                 jax.ShapeDtypeStruct((bsz, t, dkv), k_dtype),
                 jax.ShapeDtypeStruct((bsz, dkv, t), BF16)]
    out_specs = [pl.BlockSpec((n_req, tm, dq), tok), pl.BlockSpec((n_req, tm, dq), tok),
                 pl.BlockSpec((n_req, tm, dkv), tok),
                 pl.BlockSpec((n_req, dkv, tm), lambda b, i: (b, 0, i))]
    if emit_v:
        out_shape.append(jax.ShapeDtypeStruct((bsz, t, dkv), F32))
        out_specs.append(pl.BlockSpec((n_req, tm, dkv), tok))
    return pl.pallas_call(
        functools.partial(_attn_in_kernel, rope=rope, emit_v=emit_v),
        out_shape=tuple(out_shape),
        grid=(bsz // n_req, t // tm),
        in_specs=in_specs,
        out_specs=tuple(out_specs),
        compiler_params=_cparams(("parallel", "parallel")),
        name="attn_in_rope" if rope else "attn_in",
    )(*args)


def _attn_kernel(*refs, window, nb):
    if window:
        (q_ref, sg_ref, x_ref, mod_ref, kc_ref, vct_ref, kp_ref, km_ref, kn_ref,
         vpt_ref, vmt_ref, vnt_ref, sink_ref, wo_ref, o_ref, s_scr, p_scr, ot_scr) = refs
    else:
        q_ref, sg_ref, x_ref, mod_ref, kc_ref, vct_ref, sink_ref, wo_ref, o_ref, s_scr, p_scr, ot_scr = refs
    step = pl.program_id(1)
    nqb = q_ref.shape[1] // QBLK
    n_ctx = kc_ref.shape[1] // QBLK
    cols = GROUP * QBLK
    if window:
        kj = lax.broadcasted_iota(jnp.int32, (QBLK, cols), 0)
        qi = lax.broadcasted_iota(jnp.int32, (QBLK, cols), 1) & (QBLK - 1)
        after_diag = kj >= qi
        before_diag = kj <= qi
    ones_rows = jnp.where(lax.broadcasted_iota(jnp.int32, (BF16_ROWS, QBLK), 0) == 0, 1.0, 0.0).astype(BF16)
    n_blk = n_ctx + (3 if window else 0)

    def window_blocks(qb, cs, kp, km, kn, lanes):
        def mid(j):
            sl = slice(j * QBLK, (j + 1) * QBLK)
            return km[0, cs, sl] if lanes else km[0, sl, cs]
        first = kp[0, cs, :] if lanes else kp[0][:, cs]
        last = kn[0, cs, :] if lanes else kn[0][:, cs]
        return [first if qb == 0 else mid(qb - 1), mid(qb), last if qb == nqb - 1 else mid(qb + 1)]

    def block_masks(qb):
        if not window:
            return [None] * n_ctx
        prev_ok = after_diag & (step > 0) if qb == 0 else after_diag
        next_ok = before_diag & (step < nb // nqb - 1) if qb == nqb - 1 else before_diag
        return [None] * n_ctx + [prev_ok, None, next_ok]

    def scores(qb, kvh):
        u = (qb * N_KV_HEADS + kvh) % s_scr.shape[0]
        cs = slice(kvh * HEAD_DIM, (kvh + 1) * HEAD_DIM)
        heads = [kvh * GROUP + j for j in range(GROUP)]
        qq = q_ref[0, qb * QBLK:(qb + 1) * QBLK, :]
        q4 = jnp.concatenate([qq[:, h * HEAD_DIM:(h + 1) * HEAD_DIM] for h in heads], axis=0)
        sink_row = jnp.concatenate(
            [jnp.broadcast_to(sink_ref[0:1, h:h + 1], (1, QBLK)) for h in heads], axis=1) * LOG2E
        keys = [kc_ref[0, j * QBLK:(j + 1) * QBLK, cs].astype(BF16) for j in range(n_ctx)]
        if window:
            keys += window_blocks(qb, cs, kp_ref, km_ref, kn_ref, False)
        st_all = _dot_nt(jnp.concatenate(keys, axis=0), q4)
        macc = jnp.full((8, cols), NEG_INF, F32)
        for j, ok in enumerate(block_masks(qb)):
            s_blk = st_all[j * QBLK:(j + 1) * QBLK, :]
            if ok is not None:
                s_blk = jnp.where(ok, s_blk, NEG_INF)
            s_scr[u, j] = s_blk
            macc = jnp.maximum(macc, jnp.max(s_blk.reshape(QBLK // 8, 8, cols), axis=0))
        return jnp.maximum(jnp.max(macc, axis=0, keepdims=True), sink_row), sink_row

    def weighted_values(qb, kvh, m_row, sink_row):
        u = (qb * N_KV_HEADS + kvh) % s_scr.shape[0]
        cs = slice(kvh * HEAD_DIM, (kvh + 1) * HEAD_DIM)
        for j in range(n_blk):
            p_scr[u, j * QBLK:(j + 1) * QBLK, :] = jnp.exp2(s_scr[u, j] - m_row).astype(BF16)
        vts = [vct_ref[0, cs, j * QBLK:(j + 1) * QBLK] for j in range(n_ctx)]
        if window:
            vts += window_blocks(qb, cs, vpt_ref, vmt_ref, vnt_ref, True)
        vt_ext = jnp.concatenate(
            [jnp.concatenate(vts, axis=1), jnp.tile(ones_rows, (1, n_blk))], axis=0)
        acc = _dot(vt_ext, p_scr[u])
        den = acc[HEAD_DIM:HEAD_DIM + 1, :] + jnp.exp2(sink_row - m_row)
        o_t = acc[0:HEAD_DIM, :] / den
        for j in range(GROUP):
            h = kvh * GROUP + j
            ot_scr[h * HEAD_DIM:(h + 1) * HEAD_DIM, qb * QBLK:(qb + 1) * QBLK] = o_t[:, j * QBLK:(j + 1) * QBLK]

    units = [(qb, kvh) for qb in range(nqb) for kvh in range(N_KV_HEADS)]
    wave = s_scr.shape[0]
    for w0 in range(0, len(units), wave):
        stats = [scores(qb, kvh) for qb, kvh in units[w0:w0 + wave]]
        for (qb, kvh), st in zip(units[w0:w0 + wave], stats):
            weighted_values(qb, kvh, *st)
    z = (ot_scr[...].T * sg_ref[0].astype(F32)).astype(wo_ref.dtype)
    y = _dot(z, wo_ref[...])
    o_ref[0] = x_ref[0] + mod_ref[0][2:3, :] * y


def _attn(q, sg, x, mod3, mod_row, kc, vct, k_lat, vt_lat, sink, w_out):
    bsz, t, d = x.shape
    dq = q.shape[-1]
    dkv = kc.shape[-1]
    p_len = kc.shape[1]
    nb = t // QBLK
    nqb = min(ATTN_QB, nb)
    rows = nqb * QBLK
    assert t % rows == 0 and p_len % QBLK == 0, (t, rows, p_len)
    window = k_lat is not None
    tok = lambda b, i: (b, i, 0)
    in_specs = [pl.BlockSpec((1, rows, dq), tok),
                pl.BlockSpec((1, rows, dq), tok),
                pl.BlockSpec((1, rows, d), tok),
                pl.BlockSpec((1, 3, d), lambda b, i: (mod_row(b), 0, 0)),
                pl.BlockSpec((1, p_len, dkv), lambda b, i: (b, 0, 0)),
                pl.BlockSpec((1, dkv, p_len), lambda b, i: (b, 0, 0))]
    args = [q, sg, x, mod3, kc, vct]
    n_blocks = p_len // QBLK
    if window:
        prev = lambda i: jnp.maximum(i * nqb - 1, 0)
        nxt = lambda i: jnp.minimum((i + 1) * nqb, nb - 1)
        in_specs += [pl.BlockSpec((1, QBLK, dkv), lambda b, i: (b, prev(i), 0)),
                     pl.BlockSpec((1, rows, dkv), tok),
                     pl.BlockSpec((1, QBLK, dkv), lambda b, i: (b, nxt(i), 0)),
                     pl.BlockSpec((1, dkv, QBLK), lambda b, i: (b, 0, prev(i))),
                     pl.BlockSpec((1, dkv, rows), lambda b, i: (b, 0, i)),
                     pl.BlockSpec((1, dkv, QBLK), lambda b, i: (b, 0, nxt(i)))]
        args += [k_lat] * 3 + [vt_lat] * 3
        n_blocks += 3
    in_specs += [pl.BlockSpec((1, N_HEADS), lambda b, i: (0, 0)),
                 pl.BlockSpec(w_out.shape, lambda b, i: (0, 0), pipeline_mode=pl.Buffered(1))]
    args += [sink.reshape(1, N_HEADS), w_out]
    units = min(nqb, ATTN_WAVE_QB) * N_KV_HEADS
    return pl.pallas_call(
        functools.partial(_attn_kernel, window=window, nb=nb),
        out_shape=jax.ShapeDtypeStruct((bsz, t, d), F32),
        grid=(bsz, nb // nqb),
        in_specs=in_specs,
        out_specs=pl.BlockSpec((1, rows, d), tok),
        scratch_shapes=[pltpu.VMEM((units, n_blocks, QBLK, GROUP * QBLK), F32),
                        pltpu.VMEM((units, n_blocks * QBLK, GROUP * QBLK), BF16),
                        pltpu.VMEM((dq, rows), F32)],
        compiler_params=_cparams(("parallel", "parallel")),
        name="attn_window" if window else "attn_ctx",
    )(*args)


def _mlstm_in_kernel(x_ref, mod_ref, nw_ref, wt_ref, bgt_ref,
                     q_ref, k_ref, vt_ref, og_ref, gc_ref, gr_ref):
    dm = q_ref.shape[-1]
    nh = M_HEADS
    L = MCHUNK
    n_req, tr, d = x_ref.shape
    per_req = tr // L
    hb = _prenorm(x_ref[...].reshape(n_req * tr, d), nw_ref[...], mod_ref[0]).astype(wt_ref.dtype)

    gr = _dot_nt(wt_ref[5 * dm:, :], hb) + bgt_ref[...]
    n_chunks = n_req * per_req
    ri = lax.broadcasted_iota(jnp.int32, (L, L), 0)
    ci = lax.broadcasted_iota(jnp.int32, (L, L), 1)
    lane = lax.broadcasted_iota(jnp.int32, (n_chunks * nh, L), 1)
    g_rows = []
    for dr in range(2):
        before = (ri <= ci) if dr == 0 else (ri >= ci)
        tri = jnp.where(before, 1.0, 0.0).astype(BF16)
        base = dr * 2 * nh
        lf = _log_sigmoid(gr[base + nh:base + 2 * nh, :]) * LOG2E
        gi = gr[base:base + nh, :] * LOG2E
        lf_st = jnp.concatenate([lf[:, c * L:(c + 1) * L] for c in range(n_chunks)], axis=0)
        b_st = sum(_dot(piece, tri) for piece in _split3(lf_st))
        g_st = jnp.concatenate([gi[:, c * L:(c + 1) * L] for c in range(n_chunks)], axis=0) - b_st
        run = g_st
        step = 1
        while step < L:
            if dr == 0:
                run = jnp.where(lane >= step, jnp.maximum(run, pltpu.roll(run, step, 1)), run)
            else:
                run = jnp.where(lane < L - step, jnp.maximum(run, pltpu.roll(run, L - step, 1)), run)
            step *= 2
        for cidx in range(n_chunks):
            rows = slice(cidx * L, (cidx + 1) * L)
            blk = slice(cidx * nh, (cidx + 1) * nh)
            b_last = jnp.sum(lf[:, rows], axis=1, keepdims=True)
            g_max = jnp.max(g_st[blk, :], axis=1, keepdims=True)
            g_rows.append(g_st[blk, :])
            gr_ref[cidx // per_req, dr, cidx % per_req] = jnp.concatenate(
                [g_st[blk, :], b_st[blk, :], jnp.broadcast_to(b_last, (nh, L)),
                 jnp.broadcast_to(g_max, (nh, L)), run[blk, :]], axis=0)
    g_sq = jnp.concatenate(g_rows + [jnp.zeros((L - len(g_rows) * nh, L), F32)], axis=0).T
    for dr in range(2):
        for cidx in range(n_chunks):
            idx = dr * n_chunks + cidx
            lc = cidx % per_req
            gc_ref[cidx // per_req, dr, lc * L:(lc + 1) * L, :] = g_sq[:, idx * nh:(idx + 1) * nh]

    o = _dot_nt(hb, wt_ref[3 * dm:4 * dm, :])
    g = _dot_nt(hb, wt_ref[4 * dm:5 * dm, :])
    og_ref[...] = (jax.nn.sigmoid(o) * _silu(g)).astype(og_ref.dtype).reshape(og_ref.shape)
    q_ref[...] = _dot_nt(hb, wt_ref[0:dm, :]).astype(q_ref.dtype).reshape(q_ref.shape)
    k = _dot_nt(hb, wt_ref[dm:2 * dm, :]) * (M_HD ** -0.5)
    k_ref[...] = k.astype(k_ref.dtype).reshape(k_ref.shape)
    vt = _dot_nt(wt_ref[2 * dm:3 * dm, :], hb).astype(vt_ref.dtype)
    for cidx in range(n_chunks):
        vt_ref[cidx // per_req, cidx % per_req] = vt[:, cidx * L:(cidx + 1) * L]


def _mlstm_in(x, mod3, mod_row, norm_w, w_t, b_gates, shared_cond):
    bsz, t, d = x.shape
    dm = M_HEADS * M_HD
    ng = 4 * M_HEADS
    tm, n_req = _proj_tiling(bsz, t, shared_cond)
    tok = lambda b, i: (b, i, 0)
    const = lambda b, i: (0, 0)
    big = jax.ShapeDtypeStruct((bsz, t, dm), BF16)
    once = pl.Buffered(1)
    return pl.pallas_call(
        _mlstm_in_kernel,
        out_shape=(big, big, jax.ShapeDtypeStruct((bsz, t // MCHUNK, dm, MCHUNK), BF16), big,
                   jax.ShapeDtypeStruct((bsz, 2, t, M_HEADS), F32),
                   jax.ShapeDtypeStruct((bsz, 2, t // MCHUNK, 5 * M_HEADS, MCHUNK), F32)),
        grid=(bsz // n_req, t // tm),
        in_specs=[pl.BlockSpec((n_req, tm, d), tok),
                  pl.BlockSpec((1, 3, d), lambda b, i: (mod_row(b), 0, 0)),
                  pl.BlockSpec((1, d), const),
                  pl.BlockSpec(w_t.shape, const, pipeline_mode=once),
                  pl.BlockSpec((ng, 1), const)],
        out_specs=(pl.BlockSpec((n_req, tm, dm), tok), pl.BlockSpec((n_req, tm, dm), tok),
                   pl.BlockSpec((n_req, tm // MCHUNK, dm, MCHUNK), lambda b, i: (b, i, 0, 0)),
                   pl.BlockSpec((n_req, tm, dm), tok),
                   pl.BlockSpec((n_req, 2, tm, M_HEADS), lambda b, i: (b, 0, i, 0)),
                   pl.BlockSpec((n_req, 2, tm // MCHUNK, 5 * M_HEADS, MCHUNK),
                                lambda b, i: (b, 0, i, 0, 0))),
        compiler_params=pltpu.CompilerParams(dimension_semantics=("parallel", "parallel"),
                                             vmem_limit_bytes=MLSTM_IN_VMEM_LIMIT),
        name="mlstm_in",
    )(x, mod3, norm_w.reshape(1, d), w_t, b_gates.reshape(ng, 1))


def _mlstm_scan_kernel(*refs, has_init, write_state, nc):
    refs = list(refs)
    q_ref, k_ref, vt_ref, gc_ref, gr_ref, og_ref, x_ref, mod_ref, wo_ref, fw_ref = refs[:10]
    pos = 10
    if has_init:
        c0_ref, n0_ref, m0_ref = refs[pos:pos + 3]
        pos += 3
    y_ref = refs[pos]
    pos += 1
    if write_state:
        cout_ref, nout_ref, mout_ref = refs[pos:pos + 3]
        pos += 3
    ct_scr, mscr, hcur, hfwd = refs[pos:pos + 4]

    drn = pl.program_id(1)
    c = pl.program_id(2)
    n_sub, L = q_ref.shape[1], q_ref.shape[2]
    nh = M_HEADS
    pad = ct_scr.shape[1] - M_HD

    @pl.when(c == 0)
    def _init():
        if has_init:
            for h in range(nh):
                ct_scr[h, 0:M_HD, :] = c0_ref[0, 0, h].T
                ct_scr[h, M_HD:M_HD + pad, :] = jnp.concatenate(
                    [n0_ref[0, 0, h:h + 1, :], jnp.zeros((pad - 1, M_HD), F32)], axis=0)
            mscr[...] = m0_ref[0, 0] * LOG2E
        else:
            ct_scr[...] = jnp.zeros(ct_scr.shape, F32)
            mscr[...] = jnp.zeros(mscr.shape, F32)

    ones_rows = jnp.where(lax.broadcasted_iota(jnp.int32, (pad, L), 0) == 0, 1.0, 0.0).astype(BF16)
    H = L // 2
    si = lax.broadcasted_iota(jnp.int32, (H, H), 0)
    li = lax.broadcasted_iota(jnp.int32, (H, H), 1)

    def chunk_step(sub, fwd):
        gcb = gc_ref[0, 0, sub]
        grb = gr_ref[0, 0, sub]
        q = q_ref[0, sub]
        k = k_ref[0, sub]
        vt = vt_ref[0, sub]
        tri = (si <= li) if fwd else (si >= li)
        lo, hi = slice(0, H), slice(H, L)

        def head_scores(h):
            hs = slice(h * M_HD, (h + 1) * M_HD)
            m_prev = mscr[h:h + 1, 0:1]
            ct = ct_scr[h]
            m_row = jnp.maximum(grb[4 * nh + h:4 * nh + h + 1, :], m_prev)
            r1 = _dot_nt(jnp.concatenate([k[:, hs], ct.astype(BF16)], axis=0), q[:, hs])
            g_c = gcb[:, h:h + 1]

            def quad(ks, qs, masked):
                e = g_c[ks, :] - m_row[:, qs]
                if masked:
                    e = jnp.where(tri, e, NEG_INF)
                return (r1[ks, qs] * jnp.exp2(e)).astype(BF16)

            zero = jnp.zeros((H, H), BF16)
            if fwd:
                s_t = jnp.concatenate(
                    [jnp.concatenate([quad(lo, lo, True), quad(lo, hi, False)], axis=1),
                     jnp.concatenate([zero, quad(hi, hi, True)], axis=1)], axis=0)
            else:
                s_t = jnp.concatenate(
                    [jnp.concatenate([quad(lo, lo, True), zero], axis=1),
                     jnp.concatenate([quad(hi, lo, False), quad(hi, hi, True)], axis=1)], axis=0)
            return m_prev, ct, m_row, s_t, r1[L:, :]

        def head_finish(h, m_prev, ct, m_row, s_t, inter):
            hs = slice(h * M_HD, (h + 1) * M_HD)
            vext = jnp.concatenate([vt[hs, :], ones_rows], axis=0)
            g_r = grb[h:h + 1, :]
            b_r = grb[nh + h:nh + h + 1, :]
            b_last = grb[2 * nh + h:2 * nh + h + 1, 0:1]
            g_max = grb[3 * nh + h:3 * nh + h + 1, 0:1]
            w0 = jnp.exp2(m_prev - m_row)
            tot = _dot(vext, s_t) + w0 * inter
            den = tot[M_HD:M_HD + 1, :]
            floor = jnp.exp2(-(b_r + m_row))
            hcur[sub, hs, :] = tot[0:M_HD, :] / jnp.maximum(jnp.abs(den), floor)

            m_last = jnp.maximum(g_max, m_prev)
            wk = jnp.exp2(g_r - m_last)
            decay = jnp.exp2(m_prev - m_last)
            vw = (vext.astype(F32) * wk).astype(BF16)
            ct_scr[h] = decay * ct + _dot(vw, k[:, hs])
            mscr[h:h + 1, :] = jnp.broadcast_to(b_last + m_last, (1, LANES))

        pending = [head_scores(h) for h in range(min(SCAN_AHEAD, nh))]
        for h in range(nh):
            if h + SCAN_AHEAD < nh:
                pending.append(head_scores(h + SCAN_AHEAD))
            head_finish(h, *pending.pop(0))

    @pl.when(drn == 0)
    def _forward():
        for j in range(n_sub):
            chunk_step(j, True)

    @pl.when(drn == 1)
    def _backward():
        for j in reversed(range(n_sub)):
            chunk_step(j, False)

    @pl.when(drn == 0)
    def _park():
        hfwd[pl.ds(c * n_sub, n_sub)] = hcur[...]

    @pl.when(drn == 1)
    def _emit():
        first = (nc - 1 - c) * n_sub
        hsum_t = jnp.concatenate([hcur[j] + hfwd[first + j] for j in range(n_sub)], axis=1)
        hm = hsum_t.T * og_ref[0].astype(F32)
        y = _dot(hm.astype(wo_ref.dtype), wo_ref[...])
        x2 = x_ref[0] + mod_ref[0][2:3, :] * y
        ms = jnp.mean(x2 * x2, axis=-1, keepdims=True)
        y_ref[0] = x2 * lax.rsqrt(ms + EPS) * fw_ref[...]

    if write_state:
        @pl.when(c == nc - 1)
        def _final():
            for h in range(nh):
                cfin = ct_scr[h]
                cout_ref[0, 0, h] = cfin[0:M_HD, :].T
                nout_ref[0, 0, h:h + 1, :] = cfin[M_HD:M_HD + 1, :]
            mout_ref[0, 0] = mscr[...] * LN2


def _mlstm_scan(q, k, vt, gc, gr, og, x, mod3, mod_row, w_out, final_w, init, write_state):
    bsz, t, dm = q.shape
    d_model = x.shape[-1]
    L = MCHUNK
    assert t % L == 0, (t, L)
    n_sub = SCAN_SUB if (t // L) % SCAN_SUB == 0 else 1
    nc = t // (L * n_sub)
    rows = n_sub * L
    blk = lambda d, c: c + d * (nc - 1 - 2 * c)
    chunked = lambda b, d, c: (b, blk(d, c), 0, 0)
    gated = lambda b, d, c: (b, d, blk(d, c), 0, 0)
    tail = lambda b, d, c: (b, nc - 1 - d * c, 0)
    const = lambda b, d, c: (0, 0)
    in_specs = [pl.BlockSpec((1, n_sub, L, dm), chunked),
                pl.BlockSpec((1, n_sub, L, dm), chunked),
                pl.BlockSpec((1, n_sub, dm, L), chunked),
                pl.BlockSpec((1, 1, n_sub, L, gc.shape[-1]), gated),
                pl.BlockSpec((1, 1, n_sub, gr.shape[3], L), gated),
                pl.BlockSpec((1, rows, dm), tail),
                pl.BlockSpec((1, rows, d_model), tail),
                pl.BlockSpec((1, 3, d_model), lambda b, d, c: (mod_row(b), 0, 0)),
                pl.BlockSpec(w_out.shape, const, pipeline_mode=pl.Buffered(1)),
                pl.BlockSpec((1, d_model), const)]
    args = [q.reshape(bsz, t // L, L, dm), k.reshape(bsz, t // L, L, dm), vt,
            gc.reshape(bsz, 2, t // L, L, gc.shape[-1]), gr, og, x, mod3, w_out,
            final_w.reshape(1, d_model)]
    st = lambda b, d, c: (b, d, 0, 0)
    st5 = lambda b, d, c: (b, d, 0, 0, 0)
    if init is not None:
        c0, n0, m0 = init
        in_specs += [pl.BlockSpec((1, 1, M_HEADS, M_HD, M_HD), st5),
                     pl.BlockSpec((1, 1, M_HEADS, M_HD), st),
                     pl.BlockSpec((1, 1, M_HEADS, LANES), st)]
        args += [c0, n0, jnp.broadcast_to(m0[..., None], m0.shape + (LANES,))]
    out_shape = [jax.ShapeDtypeStruct((bsz, t, d_model), F32)]
    out_specs = [pl.BlockSpec((1, rows, d_model), tail)]
    if write_state:
        out_shape += [jax.ShapeDtypeStruct((bsz, 2, M_HEADS, M_HD, M_HD), F32),
                      jax.ShapeDtypeStruct((bsz, 2, M_HEADS, M_HD), F32),
                      jax.ShapeDtypeStruct((bsz, 2, M_HEADS, LANES), F32)]
        out_specs += [pl.BlockSpec((1, 1, M_HEADS, M_HD, M_HD), st5),
                      pl.BlockSpec((1, 1, M_HEADS, M_HD), st),
                      pl.BlockSpec((1, 1, M_HEADS, LANES), st)]
    return pl.pallas_call(
        functools.partial(_mlstm_scan_kernel, has_init=init is not None,
                          write_state=write_state, nc=nc),
        out_shape=tuple(out_shape),
        grid=(bsz, 2, nc),
        in_specs=in_specs,
        out_specs=tuple(out_specs),
        scratch_shapes=[pltpu.VMEM((M_HEADS, M_HD + BF16_ROWS, M_HD), F32),
                        pltpu.VMEM((M_HEADS, LANES), F32),
                        pltpu.VMEM((n_sub, dm, L), F32),
                        pltpu.VMEM((t // L, dm, L), F32)],
        compiler_params=_cparams(("parallel", "arbitrary", "arbitrary")),
        name="mlstm_scan",
    )(*args)


def _rope_tables(t):
    nf = HEAD_DIM // 4
    pos = jnp.arange(t)
    row = (pos // GRID_W).astype(F32)
    col = (pos % GRID_W).astype(F32)
    inv = ROPE_BASE ** (-jnp.arange(nf, dtype=F32) / nf)
    ar = row[:, None] * inv[None, :]
    ac = col[:, None] * inv[None, :]
    cos = jnp.concatenate([jnp.cos(ar), jnp.cos(ar), jnp.cos(ac), jnp.cos(ac)], axis=1)
    sin = jnp.concatenate([-jnp.sin(ar), jnp.sin(ar), -jnp.sin(ac), jnp.sin(ac)], axis=1)
    reps = LANES // HEAD_DIM
    return jnp.tile(cos, (1, reps)), jnp.tile(sin, (1, reps))


def kernel(x_prompt, x_sample, cache_k, cache_v, state_C, state_n, state_m, c, c_ctx,
           attn_norm_w, attn_ada_w, attn_ada_b, attn_w_in, attn_sink, attn_w_out,
           mlstm_norm_w, mlstm_ada_w, mlstm_ada_b, mlstm_w_in, mlstm_b_gates, mlstm_w_out,
           final_norm_w):
    assert attn_w_in.shape[0] == 1 and mlstm_w_in.shape[0] == 1, "one layer of each mixer"
    bsz, seq, d = x_prompt.shape
    dbsz, dseq, _ = x_sample.shape
    assert d == N_HEADS * HEAD_DIM == M_HEADS * M_HD and dseq % GRID_W == 0, (d, dseq)
    dkv = N_KV_HEADS * HEAD_DIM
    dm = M_HEADS * M_HD

    n_cond = 1 + dbsz
    cond = jnp.concatenate([c_ctx[None, :], c, jnp.zeros((-n_cond % 8, d), F32)], axis=0)
    attn_mod = _ada(cond, attn_ada_w[0], attn_ada_b[0]).reshape(-1, 3, d)
    mlstm_mod = _ada(cond, mlstm_ada_w[0], mlstm_ada_b[0]).reshape(-1, 3, d)
    ctx_row = lambda b: 0
    lat_row = lambda b: b + 1

    attn_w_in0 = attn_w_in[0]
    attn_w_out0 = attn_w_out[0]
    attn_wv_t = attn_w_in[0, :, 2 * N_HEADS * HEAD_DIM + dkv:].T
    mlstm_w_in_t = mlstm_w_in[0].T
    mlstm_w_out0 = mlstm_w_out[0]

    def mlstm_layer(x, mod_row, init, write_state, shared_cond):
        q, k, vt, og, gc, gr = _mlstm_in(x, mlstm_mod, mod_row, mlstm_norm_w[0], mlstm_w_in_t,
                                         mlstm_b_gates[0], shared_cond)
        outs = _mlstm_scan(q, k, vt, gc, gr, og, x, mlstm_mod, mod_row, mlstm_w_out0, final_norm_w,
                           init, write_state)
        return outs[0], outs[1:]

    q, sg, k_ctx, vt_ctx, v_ctx = _attn_in(x_prompt, attn_mod, ctx_row, attn_norm_w[0], attn_w_in0,
                                           attn_wv_t, None, F32, True, True)
    x1 = _attn(q, sg, x_prompt, attn_mod, ctx_row, k_ctx, vt_ctx, None, None, attn_sink[0], attn_w_out0)
    y_prompt, (c_fin, n_fin, m_fin) = mlstm_layer(x1, ctx_row, None, True, True)

    q, sg, k_lat, vt_lat = _attn_in(x_sample, attn_mod, lat_row, attn_norm_w[0], attn_w_in0,
                                    attn_wv_t, _rope_tables(dseq), BF16, False, False)
    kc = cache_k[:, 0].reshape(dbsz, -1, dkv).astype(BF16)
    vct = jnp.swapaxes(cache_v[:, 0].reshape(dbsz, -1, dkv), 1, 2).astype(BF16)
    x1 = _attn(q, sg, x_sample, attn_mod, lat_row, kc, vct, k_lat, vt_lat, attn_sink[0], attn_w_out0)
    y_sample, _ = mlstm_layer(x1, lat_row, (state_C[:, 0], state_n[:, 0], state_m[:, 0]), False, False)

    new_cache_k = k_ctx.reshape(bsz, 1, seq, N_KV_HEADS, HEAD_DIM)
    new_cache_v = v_ctx.reshape(bsz, 1, seq, N_KV_HEADS, HEAD_DIM)
    return (y_prompt, y_sample, new_cache_k, new_cache_v,
            c_fin[:, None], n_fin[:, None], m_fin[:, None, :, :, 0])
```

```python
import functools

import jax
import jax.numpy as jnp
from jax import lax
from jax.experimental import pallas as pl
from jax.experimental.pallas import tpu as pltpu

F32 = jnp.float32
BF16 = jnp.bfloat16

HEAD_DIM = 64
N_KV_HEADS = 4
GROUP = 4
N_HEADS = N_KV_HEADS * GROUP
QBLK = 128
GRID_W = 64
ROPE_BASE = 10000.0
M_HEADS = 8
M_HD = 128
EPS = 1e-6

LANES = 128
BF16_ROWS = 16
VMEM_LIMIT = 48 * 1024 * 1024
MLSTM_IN_VMEM_LIMIT = 60 * 1024 * 1024

MCHUNK = 256
ATTN_QB = 8
ATTN_WAVE_QB = 1
SCAN_SUB = 2
SCAN_AHEAD = 4
PROJ_ROWS = 1024
ADA_TILE = 1024

NEG_INF = float("-inf")
LOG2E = 1.4426950408889634
LN2 = 0.6931471805599453


def _cparams(sem):
    return pltpu.CompilerParams(dimension_semantics=sem, vmem_limit_bytes=VMEM_LIMIT)


def _silu(x):
    return x * jax.nn.sigmoid(x)


def _log_sigmoid(x):
    return jnp.minimum(x, 0.0) - jnp.log1p(jnp.exp(-jnp.abs(x)))


def _dot(a, b):
    return jnp.dot(a, b, preferred_element_type=F32)


def _dot_nt(a, b):
    return lax.dot_general(a, b, (((1,), (1,)), ((), ())), preferred_element_type=F32)


def _split3(x):
    hi = x.astype(BF16)
    r = x - hi.astype(F32)
    mid = r.astype(BF16)
    lo = (r - mid.astype(F32)).astype(BF16)
    return hi, mid, lo


def _prenorm(x, norm_w, mod):
    ms = jnp.mean(x * x, axis=-1, keepdims=True)
    y = x * lax.rsqrt(ms + EPS) * norm_w
    return y * (1.0 + mod[1:2, :]) + mod[0:1, :]


def _ada_kernel(cond_ref, w_ref, b_ref, o_ref):
    a = _silu(cond_ref[...]).astype(BF16)
    o_ref[...] = _dot(a, w_ref[...].astype(BF16)) + b_ref[...]


def _ada(cond8, w, b):
    d, n = w.shape
    tn = ADA_TILE
    return pl.pallas_call(
        _ada_kernel,
        out_shape=jax.ShapeDtypeStruct((cond8.shape[0], n), F32),
        grid=(n // tn,),
        in_specs=[pl.BlockSpec(cond8.shape, lambda j: (0, 0)),
                  pl.BlockSpec((d, tn), lambda j: (0, j)),
                  pl.BlockSpec((1, tn), lambda j: (0, j))],
        out_specs=pl.BlockSpec((cond8.shape[0], tn), lambda j: (0, j)),
        compiler_params=_cparams(("parallel",)),
        name="ada_mod",
    )(cond8, w, b.reshape(1, n))


def _rope(x, cos, sin, lane):
    first = (lane & 31) < 16
    outs = []
    for c in range(x.shape[1] // LANES):
        xc = x[:, c * LANES:(c + 1) * LANES]
        sw = jnp.where(first, pltpu.roll(xc, LANES - 16, 1), pltpu.roll(xc, 16, 1))
        outs.append(xc * cos + sw * sin)
    return jnp.concatenate(outs, axis=1)


def _attn_in_kernel(*refs, rope, emit_v):
    refs = list(refs)
    x_ref, mod_ref, nw_ref, w_ref, wvt_ref = refs[:5]
    pos = 5
    if rope:
        cos_ref, sin_ref = refs[pos:pos + 2]
        pos += 2
    q_ref, sg_ref, k_ref, vt_ref = refs[pos:pos + 4]
    dq = q_ref.shape[-1]
    dkv = k_ref.shape[-1]
    n_req, tr, d = x_ref.shape
    rows = n_req * tr
    hb = _prenorm(x_ref[...].reshape(rows, d), nw_ref[...], mod_ref[0]).astype(w_ref.dtype)
    q = _dot(hb, w_ref[:, 0:dq])
    g = _dot(hb, w_ref[:, dq:2 * dq])
    k = _dot(hb, w_ref[:, 2 * dq:2 * dq + dkv])
    if rope:
        cos = cos_ref[...]
        sin = sin_ref[...]
        lane = lax.broadcasted_iota(jnp.int32, cos.shape, 1)
        q = _rope(q, cos, sin, lane)
        k = _rope(k, cos, sin, lane)
    q_ref[...] = (q * (HEAD_DIM ** -0.5 * LOG2E)).astype(q_ref.dtype).reshape(q_ref.shape)
    sg_ref[...] = _silu(g).astype(sg_ref.dtype).reshape(sg_ref.shape)
    k_ref[...] = k.astype(k_ref.dtype).reshape(k_ref.shape)
    vt = _dot_nt(wvt_ref[...], hb).astype(vt_ref.dtype)
    for r in range(n_req):
        vt_ref[r] = vt[:, r * tr:(r + 1) * tr]
    if emit_v:
        v_ref = refs[pos + 4]
        v = _dot(hb, w_ref[:, 2 * dq + dkv:2 * dq + 2 * dkv])
        v_ref[...] = v.astype(v_ref.dtype).reshape(v_ref.shape)


def _proj_tiling(bsz, t, shared_cond):
    tr = min(PROJ_ROWS, t)
    assert t % tr == 0 and tr % MCHUNK == 0, (t, tr)
    n_req = PROJ_ROWS // tr if shared_cond and bsz % (PROJ_ROWS // tr) == 0 else 1
    return tr, n_req


def _attn_in(x, mod3, mod_row, norm_w, w_in, wv_t, rope_tabs, k_dtype, emit_v, shared_cond):
    bsz, t, d = x.shape
    dq = N_HEADS * HEAD_DIM
    dkv = N_KV_HEADS * HEAD_DIM
    tm, n_req = _proj_tiling(bsz, t, shared_cond)
    rope = rope_tabs is not None
    tok = lambda b, i: (b, i, 0)
    const = lambda b, i: (0, 0)
    in_specs = [pl.BlockSpec((n_req, tm, d), tok),
                pl.BlockSpec((1, 3, d), lambda b, i: (mod_row(b), 0, 0)),
                pl.BlockSpec((1, d), const),
                pl.BlockSpec(w_in.shape, const),
                pl.BlockSpec(wv_t.shape, const)]
    args = [x, mod3, norm_w.reshape(1, d), w_in, wv_t]
    if rope:
        in_specs += [pl.BlockSpec((tm, LANES), lambda b, i: (i, 0))] * 2
        args += list(rope_tabs)
    out_shape = [jax.ShapeDtypeStruct((bsz, t, dq), BF16),
                 jax.ShapeDtypeStruct((bsz, t, dq), BF16),
                 jax.ShapeDtypeStruct((bsz, t, dkv), k_dtype),
                 jax.ShapeDtypeStruct((bsz, dkv, t), BF16)]
    out_specs = [pl.BlockSpec((n_req, tm, dq), tok), pl.BlockSpec((n_req, tm, dq), tok),
                 pl.BlockSpec((n_req, tm, dkv), tok),
                 pl.BlockSpec((n_req, dkv, tm), lambda b, i: (b, 0, i))]
    if emit_v:
        out_shape.append(jax.ShapeDtypeStruct((bsz, t, dkv), F32))
        out_specs.append(pl.BlockSpec((n_req, tm, dkv), tok))
    return pl.pallas_call(
        functools.partial(_attn_in_kernel, rope=rope, emit_v=emit_v),
        out_shape=tuple(out_shape),
        grid=(bsz // n_req, t // tm),
        in_specs=in_specs,
        out_specs=tuple(out_specs),
        compiler_params=_cparams(("parallel", "parallel")),
        name="attn_in_rope" if rope else "attn_in",
    )(*args)


def _attn_kernel(*refs, window, nb):
    if window:
        (q_ref, sg_ref, x_ref, mod_ref, kc_ref, vct_ref, kp_ref, km_ref, kn_ref,
         vpt_ref, vmt_ref, vnt_ref, sink_ref, wo_ref, o_ref, s_scr, p_scr, ot_scr) = refs
    else:
        q_ref, sg_ref, x_ref, mod_ref, kc_ref, vct_ref, sink_ref, wo_ref, o_ref, s_scr, p_scr, ot_scr = refs
    step = pl.program_id(1)
    nqb = q_ref.shape[1] // QBLK
    n_ctx = kc_ref.shape[1] // QBLK
    cols = GROUP * QBLK
    if window:
        kj = lax.broadcasted_iota(jnp.int32, (QBLK, cols), 0)
        qi = lax.broadcasted_iota(jnp.int32, (QBLK, cols), 1) & (QBLK - 1)
        after_diag = kj >= qi
        before_diag = kj <= qi
    ones_rows = jnp.where(lax.broadcasted_iota(jnp.int32, (BF16_ROWS, QBLK), 0) == 0, 1.0, 0.0).astype(BF16)
    n_blk = n_ctx + (3 if window else 0)

    def window_blocks(qb, cs, kp, km, kn, lanes):
        def mid(j):
            sl = slice(j * QBLK, (j + 1) * QBLK)
            return km[0, cs, sl] if lanes else km[0, sl, cs]
        first = kp[0, cs, :] if lanes else kp[0][:, cs]
        last = kn[0, cs, :] if lanes else kn[0][:, cs]
        return [first if qb == 0 else mid(qb - 1), mid(qb), last if qb == nqb - 1 else mid(qb + 1)]

    def block_masks(qb):
        if not window:
            return [None] * n_ctx
        prev_ok = after_diag & (step > 0) if qb == 0 else after_diag
        next_ok = before_diag & (step < nb // nqb - 1) if qb == nqb - 1 else before_diag
        return [None] * n_ctx + [prev_ok, None, next_ok]

    def scores(qb, kvh):
        u = (qb * N_KV_HEADS + kvh) % s_scr.shape[0]
        cs = slice(kvh * HEAD_DIM, (kvh + 1) * HEAD_DIM)
        heads = [kvh * GROUP + j for j in range(GROUP)]
        qq = q_ref[0, qb * QBLK:(qb + 1) * QBLK, :]
        q4 = jnp.concatenate([qq[:, h * HEAD_DIM:(h + 1) * HEAD_DIM] for h in heads], axis=0)
        sink_row = jnp.concatenate(
            [jnp.broadcast_to(sink_ref[0:1, h:h + 1], (1, QBLK)) for h in heads], axis=1) * LOG2E
        keys = [kc_ref[0, j * QBLK:(j + 1) * QBLK, cs].astype(BF16) for j in range(n_ctx)]
        if window:
            keys += window_blocks(qb, cs, kp_ref, km_ref, kn_ref, False)
        st_all = _dot_nt(jnp.concatenate(keys, axis=0), q4)
        macc = jnp.full((8, cols), NEG_INF, F32)
        for j, ok in enumerate(block_masks(qb)):
            s_blk = st_all[j * QBLK:(j + 1) * QBLK, :]
            if ok is not None:
                s_blk = jnp.where(ok, s_blk, NEG_INF)
            s_scr[u, j] = s_blk
            macc = jnp.maximum(macc, jnp.max(s_blk.reshape(QBLK // 8, 8, cols), axis=0))
        return jnp.maximum(jnp.max(macc, axis=0, keepdims=True), sink_row), sink_row

    def weighted_values(qb, kvh, m_row, sink_row):
        u = (qb * N_KV_HEADS + kvh) % s_scr.shape[0]
        cs = slice(kvh * HEAD_DIM, (kvh + 1) * HEAD_DIM)
        for j in range(n_blk):
            p_scr[u, j * QBLK:(j + 1) * QBLK, :] = jnp.exp2(s_scr[u, j] - m_row).astype(BF16)
        vts = [vct_ref[0, cs, j * QBLK:(j + 1) * QBLK] for j in range(n_ctx)]
        if window:
            vts += window_blocks(qb, cs, vpt_ref, vmt_ref, vnt_ref, True)
        vt_ext = jnp.concatenate(
            [jnp.concatenate(vts, axis=1), jnp.tile(ones_rows, (1, n_blk))], axis=0)
        acc = _dot(vt_ext, p_scr[u])
        den = acc[HEAD_DIM:HEAD_DIM + 1, :] + jnp.exp2(sink_row - m_row)
        o_t = acc[0:HEAD_DIM, :] / den
        for j in range(GROUP):
            h = kvh * GROUP + j
            ot_scr[h * HEAD_DIM:(h + 1) * HEAD_DIM, qb * QBLK:(qb + 1) * QBLK] = o_t[:, j * QBLK:(j + 1) * QBLK]

    units = [(qb, kvh) for qb in range(nqb) for kvh in range(N_KV_HEADS)]
    wave = s_scr.shape[0]
    for w0 in range(0, len(units), wave):
        stats = [scores(qb, kvh) for qb, kvh in units[w0:w0 + wave]]
        for (qb, kvh), st in zip(units[w0:w0 + wave], stats):
            weighted_values(qb, kvh, *st)
    z = (ot_scr[...].T * sg_ref[0].astype(F32)).astype(wo_ref.dtype)
    y = _dot(z, wo_ref[...])
    o_ref[0] = x_ref[0] + mod_ref[0][2:3, :] * y


def _attn(q, sg, x, mod3, mod_row, kc, vct, k_lat, vt_lat, sink, w_out):
    bsz, t, d = x.shape
    dq = q.shape[-1]
    dkv = kc.shape[-1]
    p_len = kc.shape[1]
    nb = t // QBLK
    nqb = min(ATTN_QB, nb)
    rows = nqb * QBLK
    assert t % rows == 0 and p_len % QBLK == 0, (t, rows, p_len)
    window = k_lat is not None
    tok = lambda b, i: (b, i, 0)
    in_specs = [pl.BlockSpec((1, rows, dq), tok),
                pl.BlockSpec((1, rows, dq), tok),
                pl.BlockSpec((1, rows, d), tok),
                pl.BlockSpec((1, 3, d), lambda b, i: (mod_row(b), 0, 0)),
                pl.BlockSpec((1, p_len, dkv), lambda b, i: (b, 0, 0)),
                pl.BlockSpec((1, dkv, p_len), lambda b, i: (b, 0, 0))]
    args = [q, sg, x, mod3, kc, vct]
    n_blocks = p_len // QBLK
    if window:
        prev = lambda i: jnp.maximum(i * nqb - 1, 0)
        nxt = lambda i: jnp.minimum((i + 1) * nqb, nb - 1)
        in_specs += [pl.BlockSpec((1, QBLK, dkv), lambda b, i: (b, prev(i), 0)),
                     pl.BlockSpec((1, rows, dkv), tok),
                     pl.BlockSpec((1, QBLK, dkv), lambda b, i: (b, nxt(i), 0)),
                     pl.BlockSpec((1, dkv, QBLK), lambda b, i: (b, 0, prev(i))),
                     pl.BlockSpec((1, dkv, rows), lambda b, i: (b, 0, i)),
                     pl.BlockSpec((1, dkv, QBLK), lambda b, i: (b, 0, nxt(i)))]
        args += [k_lat] * 3 + [vt_lat] * 3
        n_blocks += 3
    in_specs += [pl.BlockSpec((1, N_HEADS), lambda b, i: (0, 0)),
                 pl.BlockSpec(w_out.shape, lambda b, i: (0, 0), pipeline_mode=pl.Buffered(1))]
    args += [sink.reshape(1, N_HEADS), w_out]
    units = min(nqb, ATTN_WAVE_QB) * N_KV_HEADS
    return pl.pallas_call(
        functools.partial(_attn_kernel, window=window, nb=nb),
        out_shape=jax.ShapeDtypeStruct((bsz, t, d), F32),
        grid=(bsz, nb // nqb),
        in_specs=in_specs,
        out_specs=pl.BlockSpec((1, rows, d), tok),
        scratch_shapes=[pltpu.VMEM((units, n_blocks, QBLK, GROUP * QBLK), F32),
                        pltpu.VMEM((units, n_blocks * QBLK, GROUP * QBLK), BF16),
                        pltpu.VMEM((dq, rows), F32)],
        compiler_params=_cparams(("parallel", "parallel")),
        name="attn_window" if window else "attn_ctx",
    )(*args)


def _mlstm_in_kernel(x_ref, mod_ref, nw_ref, wt_ref, bgt_ref,
                     q_ref, k_ref, vt_ref, og_ref, gc_ref, gr_ref):
    dm = q_ref.shape[-1]
    nh = M_HEADS
    L = MCHUNK
    n_req, tr, d = x_ref.shape
    per_req = tr // L
    hb = _prenorm(x_ref[...].reshape(n_req * tr, d), nw_ref[...], mod_ref[0]).astype(wt_ref.dtype)

    gr = _dot_nt(wt_ref[5 * dm:, :], hb) + bgt_ref[...]
    n_chunks = n_req * per_req
    ri = lax.broadcasted_iota(jnp.int32, (L, L), 0)
    ci = lax.broadcasted_iota(jnp.int32, (L, L), 1)
    lane = lax.broadcasted_iota(jnp.int32, (n_chunks * nh, L), 1)
    g_rows = []
    for dr in range(2):
        before = (ri <= ci) if dr == 0 else (ri >= ci)
        tri = jnp.where(before, 1.0, 0.0).astype(BF16)
        base = dr * 2 * nh
        lf = _log_sigmoid(gr[base + nh:base + 2 * nh, :]) * LOG2E
        gi = gr[base:base + nh, :] * LOG2E
        lf_st = jnp.concatenate([lf[:, c * L:(c + 1) * L] for c in range(n_chunks)], axis=0)
        b_st = sum(_dot(piece, tri) for piece in _split3(lf_st))
        g_st = jnp.concatenate([gi[:, c * L:(c + 1) * L] for c in range(n_chunks)], axis=0) - b_st
        run = g_st
        step = 1
        while step < L:
            if dr == 0:
                run = jnp.where(lane >= step, jnp.maximum(run, pltpu.roll(run, step, 1)), run)
            else:
                run = jnp.where(lane < L - step, jnp.maximum(run, pltpu.roll(run, L - step, 1)), run)
            step *= 2
        for cidx in range(n_chunks):
            rows = slice(cidx * L, (cidx + 1) * L)
            blk = slice(cidx * nh, (cidx + 1) * nh)
            b_last = jnp.sum(lf[:, rows], axis=1, keepdims=True)
            g_max = jnp.max(g_st[blk, :], axis=1, keepdims=True)
            g_rows.append(g_st[blk, :])
            gr_ref[cidx // per_req, dr, cidx % per_req] = jnp.concatenate(
                [g_st[blk, :], b_st[blk, :], jnp.broadcast_to(b_last, (nh, L)),
                 jnp.broadcast_to(g_max, (nh, L)), run[blk, :]], axis=0)
    g_sq = jnp.concatenate(g_rows + [jnp.zeros((L - len(g_rows) * nh, L), F32)], axis=0).T
    for dr in range(2):
        for cidx in range(n_chunks):
            idx = dr * n_chunks + cidx
            lc = cidx % per_req
            gc_ref[cidx // per_req, dr, lc * L:(lc + 1) * L, :] = g_sq[:, idx * nh:(idx + 1) * nh]

    o = _dot_nt(hb, wt_ref[3 * dm:4 * dm, :])
    g = _dot_nt(hb, wt_ref[4 * dm:5 * dm, :])
    og_ref[...] = (jax.nn.sigmoid(o) * _silu(g)).astype(og_ref.dtype).reshape(og_ref.shape)
    q_ref[...] = _dot_nt(hb, wt_ref[0:dm, :]).astype(q_ref.dtype).reshape(q_ref.shape)
    k = _dot_nt(hb, wt_ref[dm:2 * dm, :]) * (M_HD ** -0.5)
    k_ref[...] = k.astype(k_ref.dtype).reshape(k_ref.shape)
    vt = _dot_nt(wt_ref[2 * dm:3 * dm, :], hb).astype(vt_ref.dtype)
    for cidx in range(n_chunks):
        vt_ref[cidx // per_req, cidx % per_req] = vt[:, cidx * L:(cidx + 1) * L]


def _mlstm_in(x, mod3, mod_row, norm_w, w_t, b_gates, shared_cond):
    bsz, t, d = x.shape
    dm = M_HEADS * M_HD
    ng = 4 * M_HEADS
    tm, n_req = _proj_tiling(bsz, t, shared_cond)
    tok = lambda b, i: (b, i, 0)
    const = lambda b, i: (0, 0)
    big = jax.ShapeDtypeStruct((bsz, t, dm), BF16)
    once = pl.Buffered(1)
    return pl.pallas_call(
        _mlstm_in_kernel,
        out_shape=(big, big, jax.ShapeDtypeStruct((bsz, t // MCHUNK, dm, MCHUNK), BF16), big,
                   jax.ShapeDtypeStruct((bsz, 2, t, M_HEADS), F32),
                   jax.ShapeDtypeStruct((bsz, 2, t // MCHUNK, 5 * M_HEADS, MCHUNK), F32)),
        grid=(bsz // n_req, t // tm),
        in_specs=[pl.BlockSpec((n_req, tm, d), tok),
                  pl.BlockSpec((1, 3, d), lambda b, i: (mod_row(b), 0, 0)),
                  pl.BlockSpec((1, d), const),
                  pl.BlockSpec(w_t.shape, const, pipeline_mode=once),
                  pl.BlockSpec((ng, 1), const)],
        out_specs=(pl.BlockSpec((n_req, tm, dm), tok), pl.BlockSpec((n_req, tm, dm), tok),
                   pl.BlockSpec((n_req, tm // MCHUNK, dm, MCHUNK), lambda b, i: (b, i, 0, 0)),
                   pl.BlockSpec((n_req, tm, dm), tok),
                   pl.BlockSpec((n_req, 2, tm, M_HEADS), lambda b, i: (b, 0, i, 0)),
                   pl.BlockSpec((n_req, 2, tm // MCHUNK, 5 * M_HEADS, MCHUNK),
                                lambda b, i: (b, 0, i, 0, 0))),
        compiler_params=pltpu.CompilerParams(dimension_semantics=("parallel", "parallel"),
                                             vmem_limit_bytes=MLSTM_IN_VMEM_LIMIT),
        name="mlstm_in",
    )(x, mod3, norm_w.reshape(1, d), w_t, b_gates.reshape(ng, 1))


def _mlstm_scan_kernel(*refs, has_init, write_state, nc):
    refs = list(refs)
    q_ref, k_ref, vt_ref, gc_ref, gr_ref, og_ref, x_ref, mod_ref, wo_ref, fw_ref = refs[:10]
    pos = 10
    if has_init:
        c0_ref, n0_ref, m0_ref = refs[pos:pos + 3]
        pos += 3
    y_ref = refs[pos]
    pos += 1
    if write_state:
        cout_ref, nout_ref, mout_ref = refs[pos:pos + 3]
        pos += 3
    ct_scr, mscr, hcur, hfwd = refs[pos:pos + 4]

    drn = pl.program_id(1)
    c = pl.program_id(2)
    n_sub, L = q_ref.shape[1], q_ref.shape[2]
    nh = M_HEADS
    pad = ct_scr.shape[1] - M_HD

    @pl.when(c == 0)
    def _init():
        if has_init:
            for h in range(nh):
                ct_scr[h, 0:M_HD, :] = c0_ref[0, 0, h].T
                ct_scr[h, M_HD:M_HD + pad, :] = jnp.concatenate(
                    [n0_ref[0, 0, h:h + 1, :], jnp.zeros((pad - 1, M_HD), F32)], axis=0)
            mscr[...] = m0_ref[0, 0] * LOG2E
        else:
            ct_scr[...] = jnp.zeros(ct_scr.shape, F32)
            mscr[...] = jnp.zeros(mscr.shape, F32)

    ones_rows = jnp.where(lax.broadcasted_iota(jnp.int32, (pad, L), 0) == 0, 1.0, 0.0).astype(BF16)
    H = L // 2
    si = lax.broadcasted_iota(jnp.int32, (H, H), 0)
    li = lax.broadcasted_iota(jnp.int32, (H, H), 1)

    def chunk_step(sub, fwd):
        gcb = gc_ref[0, 0, sub]
        grb = gr_ref[0, 0, sub]
        q = q_ref[0, sub]
        k = k_ref[0, sub]
        vt = vt_ref[0, sub]
        tri = (si <= li) if fwd else (si >= li)
        lo, hi = slice(0, H), slice(H, L)

        def head_scores(h):
            hs = slice(h * M_HD, (h + 1) * M_HD)
            m_prev = mscr[h:h + 1, 0:1]
            ct = ct_scr[h]
            m_row = jnp.maximum(grb[4 * nh + h:4 * nh + h + 1, :], m_prev)
            r1 = _dot_nt(jnp.concatenate([k[:, hs], ct.astype(BF16)], axis=0), q[:, hs])
            g_c = gcb[:, h:h + 1]

            def quad(ks, qs, masked):
                e = g_c[ks, :] - m_row[:, qs]
                if masked:
                    e = jnp.where(tri, e, NEG_INF)
                return (r1[ks, qs] * jnp.exp2(e)).astype(BF16)

            zero = jnp.zeros((H, H), BF16)
            if fwd:
                s_t = jnp.concatenate(
                    [jnp.concatenate([quad(lo, lo, True), quad(lo, hi, False)], axis=1),
                     jnp.concatenate([zero, quad(hi, hi, True)], axis=1)], axis=0)
            else:
                s_t = jnp.concatenate(
                    [jnp.concatenate([quad(lo, lo, True), zero], axis=1),
                     jnp.concatenate([quad(hi, lo, False), quad(hi, hi, True)], axis=1)], axis=0)
            return m_prev, ct, m_row, s_t, r1[L:, :]

        def head_finish(h, m_prev, ct, m_row, s_t, inter):
            hs = slice(h * M_HD, (h + 1) * M_HD)
            vext = jnp.concatenate([vt[hs, :], ones_rows], axis=0)
            g_r = grb[h:h + 1, :]
            b_r = grb[nh + h:nh + h + 1, :]
            b_last = grb[2 * nh + h:2 * nh + h + 1, 0:1]
            g_max = grb[3 * nh + h:3 * nh + h + 1, 0:1]
            w0 = jnp.exp2(m_prev - m_row)
            tot = _dot(vext, s_t) + w0 * inter
            den = tot[M_HD:M_HD + 1, :]
            floor = jnp.exp2(-(b_r + m_row))
            hcur[sub, hs, :] = tot[0:M_HD, :] / jnp.maximum(jnp.abs(den), floor)

            m_last = jnp.maximum(g_max, m_prev)
            wk = jnp.exp2(g_r - m_last)
            decay = jnp.exp2(m_prev - m_last)
            vw = (vext.astype(F32) * wk).astype(BF16)
            ct_scr[h] = decay * ct + _dot(vw, k[:, hs])
            mscr[h:h + 1, :] = jnp.broadcast_to(b_last + m_last, (1, LANES))

        pending = [head_scores(h) for h in range(min(SCAN_AHEAD, nh))]
        for h in range(nh):
            if h + SCAN_AHEAD < nh:
                pending.append(head_scores(h + SCAN_AHEAD))
            head_finish(h, *pending.pop(0))

    @pl.when(drn == 0)
    def _forward():
        for j in range(n_sub):
            chunk_step(j, True)

    @pl.when(drn == 1)
    def _backward():
        for j in reversed(range(n_sub)):
            chunk_step(j, False)

    @pl.when(drn == 0)
    def _park():
        hfwd[pl.ds(c * n_sub, n_sub)] = hcur[...]

    @pl.when(drn == 1)
    def _emit():
        first = (nc - 1 - c) * n_sub
        hsum_t = jnp.concatenate([hcur[j] + hfwd[first + j] for j in range(n_sub)], axis=1)
        hm = hsum_t.T * og_ref[0].astype(F32)
        y = _dot(hm.astype(wo_ref.dtype), wo_ref[...])
        x2 = x_ref[0] + mod_ref[0][2:3, :] * y
        ms = jnp.mean(x2 * x2, axis=-1, keepdims=True)
        y_ref[0] = x2 * lax.rsqrt(ms + EPS) * fw_ref[...]

    if write_state:
        @pl.when(c == nc - 1)
        def _final():
            for h in range(nh):
                cfin = ct_scr[h]
                cout_ref[0, 0, h] = cfin[0:M_HD, :].T
                nout_ref[0, 0, h:h + 1, :] = cfin[M_HD:M_HD + 1, :]
            mout_ref[0, 0] = mscr[...] * LN2


def _mlstm_scan(q, k, vt, gc, gr, og, x, mod3, mod_row, w_out, final_w, init, write_state):
    bsz, t, dm = q.shape
    d_model = x.shape[-1]
    L = MCHUNK
    assert t % L == 0, (t, L)
    n_sub = SCAN_SUB if (t // L) % SCAN_SUB == 0 else 1
    nc = t // (L * n_sub)
    rows = n_sub * L
    blk = lambda d, c: c + d * (nc - 1 - 2 * c)
    chunked = lambda b, d, c: (b, blk(d, c), 0, 0)
    gated = lambda b, d, c: (b, d, blk(d, c), 0, 0)
    tail = lambda b, d, c: (b, nc - 1 - d * c, 0)
    const = lambda b, d, c: (0, 0)
    in_specs = [pl.BlockSpec((1, n_sub, L, dm), chunked),
                pl.BlockSpec((1, n_sub, L, dm), chunked),
                pl.BlockSpec((1, n_sub, dm, L), chunked),
                pl.BlockSpec((1, 1, n_sub, L, gc.shape[-1]), gated),
                pl.BlockSpec((1, 1, n_sub, gr.shape[3], L), gated),
                pl.BlockSpec((1, rows, dm), tail),
                pl.BlockSpec((1, rows, d_model), tail),
                pl.BlockSpec((1, 3, d_model), lambda b, d, c: (mod_row(b), 0, 0)),
                pl.BlockSpec(w_out.shape, const, pipeline_mode=pl.Buffered(1)),
                pl.BlockSpec((1, d_model), const)]
    args = [q.reshape(bsz, t // L, L, dm), k.reshape(bsz, t // L, L, dm), vt,
            gc.reshape(bsz, 2, t // L, L, gc.shape[-1]), gr, og, x, mod3, w_out,
            final_w.reshape(1, d_model)]
    st = lambda b, d, c: (b, d, 0, 0)
    st5 = lambda b, d, c: (b, d, 0, 0, 0)
    if init is not None:
        c0, n0, m0 = init
        in_specs += [pl.BlockSpec((1, 1, M_HEADS, M_HD, M_HD), st5),
                     pl.BlockSpec((1, 1, M_HEADS, M_HD), st),
                     pl.BlockSpec((1, 1, M_HEADS, LANES), st)]
        args += [c0, n0, jnp.broadcast_to(m0[..., None], m0.shape + (LANES,))]
    out_shape = [jax.ShapeDtypeStruct((bsz, t, d_model), F32)]
    out_specs = [pl.BlockSpec((1, rows, d_model), tail)]
    if write_state:
        out_shape += [jax.ShapeDtypeStruct((bsz, 2, M_HEADS, M_HD, M_HD), F32),
                      jax.ShapeDtypeStruct((bsz, 2, M_HEADS, M_HD), F32),
                      jax.ShapeDtypeStruct((bsz, 2, M_HEADS, LANES), F32)]
        out_specs += [pl.BlockSpec((1, 1, M_HEADS, M_HD, M_HD), st5),
                      pl.BlockSpec((1, 1, M_HEADS, M_HD), st),
                      pl.BlockSpec((1, 1, M_HEADS, LANES), st)]
    return pl.pallas_call(
        functools.partial(_mlstm_scan_kernel, has_init=init is not None,
                          write_state=write_state, nc=nc),
        out_shape=tuple(out_shape),
        grid=(bsz, 2, nc),
        in_specs=in_specs,
        out_specs=tuple(out_specs),
        scratch_shapes=[pltpu.VMEM((M_HEADS, M_HD + BF16_ROWS, M_HD), F32),
                        pltpu.VMEM((M_HEADS, LANES), F32),
                        pltpu.VMEM((n_sub, dm, L), F32),
                        pltpu.VMEM((t // L, dm, L), F32)],
        compiler_params=_cparams(("parallel", "arbitrary", "arbitrary")),
        name="mlstm_scan",
    )(*args)


def _rope_tables(t):
    nf = HEAD_DIM // 4
    pos = jnp.arange(t)
    row = (pos // GRID_W).astype(F32)
    col = (pos % GRID_W).astype(F32)
    inv = ROPE_BASE ** (-jnp.arange(nf, dtype=F32) / nf)
    ar = row[:, None] * inv[None, :]
    ac = col[:, None] * inv[None, :]
    cos = jnp.concatenate([jnp.cos(ar), jnp.cos(ar), jnp.cos(ac), jnp.cos(ac)], axis=1)
    sin = jnp.concatenate([-jnp.sin(ar), jnp.sin(ar), -jnp.sin(ac), jnp.sin(ac)], axis=1)
    reps = LANES // HEAD_DIM
    return jnp.tile(cos, (1, reps)), jnp.tile(sin, (1, reps))


def kernel(x_prompt, x_sample, cache_k, cache_v, state_C, state_n, state_m, c, c_ctx,
           attn_norm_w, attn_ada_w, attn_ada_b, attn_w_in, attn_sink, attn_w_out,
           mlstm_norm_w, mlstm_ada_w, mlstm_ada_b, mlstm_w_in, mlstm_b_gates, mlstm_w_out,
           final_norm_w):
    assert attn_w_in.shape[0] == 1 and mlstm_w_in.shape[0] == 1, "one layer of each mixer"
    bsz, seq, d = x_prompt.shape
    dbsz, dseq, _ = x_sample.shape
    assert d == N_HEADS * HEAD_DIM == M_HEADS * M_HD and dseq % GRID_W == 0, (d, dseq)
    dkv = N_KV_HEADS * HEAD_DIM
    dm = M_HEADS * M_HD

    n_cond = 1 + dbsz
    cond = jnp.concatenate([c_ctx[None, :], c, jnp.zeros((-n_cond % 8, d), F32)], axis=0)
    attn_mod = _ada(cond, attn_ada_w[0], attn_ada_b[0]).reshape(-1, 3, d)
    mlstm_mod = _ada(cond, mlstm_ada_w[0], mlstm_ada_b[0]).reshape(-1, 3, d)
    ctx_row = lambda b: 0
    lat_row = lambda b: b + 1

    attn_w_in0 = attn_w_in[0]
    attn_w_out0 = attn_w_out[0]
    attn_wv_t = attn_w_in[0, :, 2 * N_HEADS * HEAD_DIM + dkv:].T
    mlstm_w_in_t = mlstm_w_in[0].T
    mlstm_w_out0 = mlstm_w_out[0]

    def mlstm_layer(x, mod_row, init, write_state, shared_cond):
        q, k, vt, og, gc, gr = _mlstm_in(x, mlstm_mod, mod_row, mlstm_norm_w[0], mlstm_w_in_t,
                                         mlstm_b_gates[0], shared_cond)
        outs = _mlstm_scan(q, k, vt, gc, gr, og, x, mlstm_mod, mod_row, mlstm_w_out0, final_norm_w,
                           init, write_state)
        return outs[0], outs[1:]

    q, sg, k_ctx, vt_ctx, v_ctx = _attn_in(x_prompt, attn_mod, ctx_row, attn_norm_w[0], attn_w_in0,
                                           attn_wv_t, None, F32, True, True)
    x1 = _attn(q, sg, x_prompt, attn_mod, ctx_row, k_ctx, vt_ctx, None, None, attn_sink[0], attn_w_out0)
    y_prompt, (c_fin, n_fin, m_fin) = mlstm_layer(x1, ctx_row, None, True, True)

    q, sg, k_lat, vt_lat = _attn_in(x_sample, attn_mod, lat_row, attn_norm_w[0], attn_w_in0,
                                    attn_wv_t, _rope_tables(dseq), BF16, False, False)
    kc = cache_k[:, 0].reshape(dbsz, -1, dkv).astype(BF16)
    vct = jnp.swapaxes(cache_v[:, 0].reshape(dbsz, -1, dkv), 1, 2).astype(BF16)
    x1 = _attn(q, sg, x_sample, attn_mod, lat_row, kc, vct, k_lat, vt_lat, attn_sink[0], attn_w_out0)
    y_sample, _ = mlstm_layer(x1, lat_row, (state_C[:, 0], state_n[:, 0], state_m[:, 0]), False, False)

    new_cache_k = k_ctx.reshape(bsz, 1, seq, N_KV_HEADS, HEAD_DIM)
    new_cache_v = v_ctx.reshape(bsz, 1, seq, N_KV_HEADS, HEAD_DIM)
    return (y_prompt, y_sample, new_cache_k, new_cache_v,
            c_fin[:, None], n_fin[:, None], m_fin[:, None, :, :, 0])
```

```python
import functools

import jax
import jax.numpy as jnp
from jax import lax
from jax.experimental import pallas as pl
from jax.experimental.pallas import tpu as pltpu

F32 = jnp.float32
BF16 = jnp.bfloat16

HEAD_DIM = 64
N_KV_HEADS = 4
GROUP = 4
N_HEADS = N_KV_HEADS * GROUP
QBLK = 128
GRID_W = 64
ROPE_BASE = 10000.0
M_HEADS = 8
M_HD = 128
EPS = 1e-6

LANES = 128
BF16_ROWS = 16
VMEM_LIMIT = 48 * 1024 * 1024
MLSTM_IN_VMEM_LIMIT = 60 * 1024 * 1024

MCHUNK = 256
ATTN_QB = 4
ATTN_WAVE_QB = 1
SCAN_SUB = 2
SCAN_AHEAD = 4
PROJ_ROWS = 1024
ADA_TILE = 1024

NEG_INF = float("-inf")
LOG2E = 1.4426950408889634
LN2 = 0.6931471805599453


def _cparams(sem):
    return pltpu.CompilerParams(dimension_semantics=sem, vmem_limit_bytes=VMEM_LIMIT)


def _silu(x):
    return x * jax.nn.sigmoid(x)


def _log_sigmoid(x):
    return jnp.minimum(x, 0.0) - jnp.log1p(jnp.exp(-jnp.abs(x)))


def _dot(a, b):
    return jnp.dot(a, b, preferred_element_type=F32)


def _dot_nt(a, b):
    return lax.dot_general(a, b, (((1,), (1,)), ((), ())), preferred_element_type=F32)


def _split3(x):
    hi = x.astype(BF16)
    r = x - hi.astype(F32)
    mid = r.astype(BF16)
    lo = (r - mid.astype(F32)).astype(BF16)
    return hi, mid, lo


def _prenorm(x, norm_w, mod):
    ms = jnp.mean(x * x, axis=-1, keepdims=True)
    y = x * lax.rsqrt(ms + EPS) * norm_w
    return y * (1.0 + mod[1:2, :]) + mod[0:1, :]


def _ada_kernel(cond_ref, w_ref, b_ref, o_ref):
    a = _silu(cond_ref[...]).astype(BF16)
    o_ref[...] = _dot(a, w_ref[...].astype(BF16)) + b_ref[...]


def _ada(cond8, w, b):
    d, n = w.shape
    tn = ADA_TILE
    return pl.pallas_call(
        _ada_kernel,
        out_shape=jax.ShapeDtypeStruct((cond8.shape[0], n), F32),
        grid=(n // tn,),
        in_specs=[pl.BlockSpec(cond8.shape, lambda j: (0, 0)),
                  pl.BlockSpec((d, tn), lambda j: (0, j)),
                  pl.BlockSpec((1, tn), lambda j: (0, j))],
        out_specs=pl.BlockSpec((cond8.shape[0], tn), lambda j: (0, j)),
        compiler_params=_cparams(("parallel",)),
        name="ada_mod",
    )(cond8, w, b.reshape(1, n))


def _rope(x, cos, sin, lane):
    first = (lane & 31) < 16
    outs = []
    for c in range(x.shape[1] // LANES):
        xc = x[:, c * LANES:(c + 1) * LANES]
        sw = jnp.where(first, pltpu.roll(xc, LANES - 16, 1), pltpu.roll(xc, 16, 1))
        outs.append(xc * cos + sw * sin)
    return jnp.concatenate(outs, axis=1)


def _attn_in_kernel(*refs, rope, emit_v):
    refs = list(refs)
    x_ref, mod_ref, nw_ref, w_ref, wvt_ref = refs[:5]
    pos = 5
    if rope:
        cos_ref, sin_ref = refs[pos:pos + 2]
        pos += 2
    q_ref, sg_ref, k_ref, vt_ref = refs[pos:pos + 4]
    dq = q_ref.shape[-1]
    dkv = k_ref.shape[-1]
    n_req, tr, d = x_ref.shape
    rows = n_req * tr
    hb = _prenorm(x_ref[...].reshape(rows, d), nw_ref[...], mod_ref[0]).astype(w_ref.dtype)
    q = _dot(hb, w_ref[:, 0:dq])
    g = _dot(hb, w_ref[:, dq:2 * dq])
    k = _dot(hb, w_ref[:, 2 * dq:2 * dq + dkv])
    if rope:
        cos = cos_ref[...]
        sin = sin_ref[...]
        lane = lax.broadcasted_iota(jnp.int32, cos.shape, 1)
        q = _rope(q, cos, sin, lane)
        k = _rope(k, cos, sin, lane)
    q_ref[...] = (q * (HEAD_DIM ** -0.5 * LOG2E)).astype(q_ref.dtype).reshape(q_ref.shape)
    sg_ref[...] = _silu(g).astype(sg_ref.dtype).reshape(sg_ref.shape)
    k_ref[...] = k.astype(k_ref.dtype).reshape(k_ref.shape)
    vt = _dot_nt(wvt_ref[...], hb).astype(vt_ref.dtype)
    for r in range(n_req):
        vt_ref[r] = vt[:, r * tr:(r + 1) * tr]
    if emit_v:
        v_ref = refs[pos + 4]
        v = _dot(hb, w_ref[:, 2 * dq + dkv:2 * dq + 2 * dkv])
        v_ref[...] = v.astype(v_ref.dtype).reshape(v_ref.shape)


def _proj_tiling(bsz, t, shared_cond):
    tr = min(PROJ_ROWS, t)
    assert t % tr == 0 and tr % MCHUNK == 0, (t, tr)
    n_req = PROJ_ROWS // tr if shared_cond and bsz % (PROJ_ROWS // tr) == 0 else 1
    return tr, n_req


def _attn_in(x, mod3, mod_row, norm_w, w_in, wv_t, rope_tabs, k_dtype, emit_v, shared_cond):
    bsz, t, d = x.shape
    dq = N_HEADS * HEAD_DIM
    dkv = N_KV_HEADS * HEAD_DIM
    tm, n_req = _proj_tiling(bsz, t, shared_cond)
    rope = rope_tabs is not None
    tok = lambda b, i: (b, i, 0)
    const = lambda b, i: (0, 0)
    in_specs = [pl.BlockSpec((n_req, tm, d), tok),
                pl.BlockSpec((1, 3, d), lambda b, i: (mod_row(b), 0, 0)),
                pl.BlockSpec((1, d), const),
                pl.BlockSpec(w_in.shape, const),
                pl.BlockSpec(wv_t.shape, const)]
    args = [x, mod3, norm_w.reshape(1, d), w_in, wv_t]
    if rope:
        in_specs += [pl.BlockSpec((tm, LANES), lambda b, i: (i, 0))] * 2
        args += list(rope_tabs)
    out_shape = [jax.ShapeDtypeStruct((bsz, t, dq), BF16),
                 jax.ShapeDtypeStruct((bsz, t, dq), BF16),
                 jax.ShapeDtypeStruct((bsz, t, dkv), k_dtype),
                 jax.ShapeDtypeStruct((bsz, dkv, t), BF16)]
    out_specs = [pl.BlockSpec((n_req, tm, dq), tok), pl.BlockSpec((n_req, tm, dq), tok),
                 pl.BlockSpec((n_req, tm, dkv), tok),
                 pl.BlockSpec((n_req, dkv, tm), lambda b, i: (b, 0, i))]
    if emit_v:
        out_shape.append(jax.ShapeDtypeStruct((bsz, t, dkv), F32))
        out_specs.append(pl.BlockSpec((n_req, tm, dkv), tok))
    return pl.pallas_call(
        functools.partial(_attn_in_kernel, rope=rope, emit_v=emit_v),
        out_shape=tuple(out_shape),
        grid=(bsz // n_req, t // tm),
        in_specs=in_specs,
        out_specs=tuple(out_specs),
        compiler_params=_cparams(("parallel", "parallel")),
        name="attn_in_rope" if rope else "attn_in",
    )(*args)


def _attn_kernel(*refs, window, nb):
    if window:
        (q_ref, sg_ref, x_ref, mod_ref, kc_ref, vct_ref, kp_ref, km_ref, kn_ref,
         vpt_ref, vmt_ref, vnt_ref, sink_ref, wo_ref, o_ref, s_scr, p_scr, ot_scr) = refs
    else:
        q_ref, sg_ref, x_ref, mod_ref, kc_ref, vct_ref, sink_ref, wo_ref, o_ref, s_scr, p_scr, ot_scr = refs
    step = pl.program_id(1)
    nqb = q_ref.shape[1] // QBLK
    n_ctx = kc_ref.shape[1] // QBLK
    cols = GROUP * QBLK
    if window:
        kj = lax.broadcasted_iota(jnp.int32, (QBLK, cols), 0)
        qi = lax.broadcasted_iota(jnp.int32, (QBLK, cols), 1) & (QBLK - 1)
        after_diag = kj >= qi
        before_diag = kj <= qi
    ones_rows = jnp.where(lax.broadcasted_iota(jnp.int32, (BF16_ROWS, QBLK), 0) == 0, 1.0, 0.0).astype(BF16)
    n_blk = n_ctx + (3 if window else 0)

    def window_blocks(qb, cs, kp, km, kn, lanes):
        def mid(j):
            sl = slice(j * QBLK, (j + 1) * QBLK)
            return km[0, cs, sl] if lanes else km[0, sl, cs]
        first = kp[0, cs, :] if lanes else kp[0][:, cs]
        last = kn[0, cs, :] if lanes else kn[0][:, cs]
        return [first if qb == 0 else mid(qb - 1), mid(qb), last if qb == nqb - 1 else mid(qb + 1)]

    def block_masks(qb):
        if not window:
            return [None] * n_ctx
        prev_ok = after_diag & (step > 0) if qb == 0 else after_diag
        next_ok = before_diag & (step < nb // nqb - 1) if qb == nqb - 1 else before_diag
        return [None] * n_ctx + [prev_ok, None, next_ok]

    def scores(qb, kvh):
        u = (qb * N_KV_HEADS + kvh) % s_scr.shape[0]
        cs = slice(kvh * HEAD_DIM, (kvh + 1) * HEAD_DIM)
        heads = [kvh * GROUP + j for j in range(GROUP)]
        qq = q_ref[0, qb * QBLK:(qb + 1) * QBLK, :]
        q4 = jnp.concatenate([qq[:, h * HEAD_DIM:(h + 1) * HEAD_DIM] for h in heads], axis=0)
        sink_row = jnp.concatenate(
            [jnp.broadcast_to(sink_ref[0:1, h:h + 1], (1, QBLK)) for h in heads], axis=1) * LOG2E
        keys = [kc_ref[0, j * QBLK:(j + 1) * QBLK, cs].astype(BF16) for j in range(n_ctx)]
        if window:
            keys += window_blocks(qb, cs, kp_ref, km_ref, kn_ref, False)
        st_all = _dot_nt(jnp.concatenate(keys, axis=0), q4)
        macc = jnp.full((8, cols), NEG_INF, F32)
        for j, ok in enumerate(block_masks(qb)):
            s_blk = st_all[j * QBLK:(j + 1) * QBLK, :]
            if ok is not None:
                s_blk = jnp.where(ok, s_blk, NEG_INF)
            s_scr[u, j] = s_blk
            macc = jnp.maximum(macc, jnp.max(s_blk.reshape(QBLK // 8, 8, cols), axis=0))
        return jnp.maximum(jnp.max(macc, axis=0, keepdims=True), sink_row), sink_row

    def weighted_values(qb, kvh, m_row, sink_row):
        u = (qb * N_KV_HEADS + kvh) % s_scr.shape[0]
        cs = slice(kvh * HEAD_DIM, (kvh + 1) * HEAD_DIM)
        for j in range(n_blk):
            p_scr[u, j * QBLK:(j + 1) * QBLK, :] = jnp.exp2(s_scr[u, j] - m_row).astype(BF16)
        vts = [vct_ref[0, cs, j * QBLK:(j + 1) * QBLK] for j in range(n_ctx)]
        if window:
            vts += window_blocks(qb, cs, vpt_ref, vmt_ref, vnt_ref, True)
        vt_ext = jnp.concatenate(
            [jnp.concatenate(vts, axis=1), jnp.tile(ones_rows, (1, n_blk))], axis=0)
        acc = _dot(vt_ext, p_scr[u])
        den = acc[HEAD_DIM:HEAD_DIM + 1, :] + jnp.exp2(sink_row - m_row)
        o_t = acc[0:HEAD_DIM, :] / den
        for j in range(GROUP):
            h = kvh * GROUP + j
            ot_scr[h * HEAD_DIM:(h + 1) * HEAD_DIM, qb * QBLK:(qb + 1) * QBLK] = o_t[:, j * QBLK:(j + 1) * QBLK]

    units = [(qb, kvh) for qb in range(nqb) for kvh in range(N_KV_HEADS)]
    wave = s_scr.shape[0]
    for w0 in range(0, len(units), wave):
        stats = [scores(qb, kvh) for qb, kvh in units[w0:w0 + wave]]
        for (qb, kvh), st in zip(units[w0:w0 + wave], stats):
            weighted_values(qb, kvh, *st)
    z = (ot_scr[...].T * sg_ref[0].astype(F32)).astype(wo_ref.dtype)
    y = _dot(z, wo_ref[...])
    o_ref[0] = x_ref[0] + mod_ref[0][2:3, :] * y


def _attn(q, sg, x, mod3, mod_row, kc, vct, k_lat, vt_lat, sink, w_out):
    bsz, t, d = x.shape
    dq = q.shape[-1]
    dkv = kc.shape[-1]
    p_len = kc.shape[1]
    nb = t // QBLK
    nqb = min(ATTN_QB, nb)
    rows = nqb * QBLK
    assert t % rows == 0 and p_len % QBLK == 0, (t, rows, p_len)
    window = k_lat is not None
    tok = lambda b, i: (b, i, 0)
    in_specs = [pl.BlockSpec((1, rows, dq), tok),
                pl.BlockSpec((1, rows, dq), tok),
                pl.BlockSpec((1, rows, d), tok),
                pl.BlockSpec((1, 3, d), lambda b, i: (mod_row(b), 0, 0)),
                pl.BlockSpec((1, p_len, dkv), lambda b, i: (b, 0, 0)),
                pl.BlockSpec((1, dkv, p_len), lambda b, i: (b, 0, 0))]
    args = [q, sg, x, mod3, kc, vct]
    n_blocks = p_len // QBLK
    if window:
        prev = lambda i: jnp.maximum(i * nqb - 1, 0)
        nxt = lambda i: jnp.minimum((i + 1) * nqb, nb - 1)
        in_specs += [pl.BlockSpec((1, QBLK, dkv), lambda b, i: (b, prev(i), 0)),
                     pl.BlockSpec((1, rows, dkv), tok),
                     pl.BlockSpec((1, QBLK, dkv), lambda b, i: (b, nxt(i), 0)),
                     pl.BlockSpec((1, dkv, QBLK), lambda b, i: (b, 0, prev(i))),
                     pl.BlockSpec((1, dkv, rows), lambda b, i: (b, 0, i)),
                     pl.BlockSpec((1, dkv, QBLK), lambda b, i: (b, 0, nxt(i)))]
        args += [k_lat] * 3 + [vt_lat] * 3
        n_blocks += 3
    in_specs += [pl.BlockSpec((1, N_HEADS), lambda b, i: (0, 0)),
                 pl.BlockSpec(w_out.shape, lambda b, i: (0, 0), pipeline_mode=pl.Buffered(1))]
    args += [sink.reshape(1, N_HEADS), w_out]
    units = min(nqb, ATTN_WAVE_QB) * N_KV_HEADS
    return pl.pallas_call(
        functools.partial(_attn_kernel, window=window, nb=nb),
        out_shape=jax.ShapeDtypeStruct((bsz, t, d), F32),
        grid=(bsz, nb // nqb),
        in_specs=in_specs,
        out_specs=pl.BlockSpec((1, rows, d), tok),
        scratch_shapes=[pltpu.VMEM((units, n_blocks, QBLK, GROUP * QBLK), F32),
                        pltpu.VMEM((units, n_blocks * QBLK, GROUP * QBLK), BF16),
                        pltpu.VMEM((dq, rows), F32)],
        compiler_params=_cparams(("parallel", "parallel")),
        name="attn_window" if window else "attn_ctx",
    )(*args)


def _mlstm_in_kernel(x_ref, mod_ref, nw_ref, wt_ref, bgt_ref,
                     q_ref, k_ref, vt_ref, og_ref, gc_ref, gr_ref):
    dm = q_ref.shape[-1]
    nh = M_HEADS
    L = MCHUNK
    n_req, tr, d = x_ref.shape
    per_req = tr // L
    hb = _prenorm(x_ref[...].reshape(n_req * tr, d), nw_ref[...], mod_ref[0]).astype(wt_ref.dtype)

    gr = _dot_nt(wt_ref[5 * dm:, :], hb) + bgt_ref[...]
    n_chunks = n_req * per_req
    ri = lax.broadcasted_iota(jnp.int32, (L, L), 0)
    ci = lax.broadcasted_iota(jnp.int32, (L, L), 1)
    lane = lax.broadcasted_iota(jnp.int32, (n_chunks * nh, L), 1)
    g_rows = []
    for dr in range(2):
        before = (ri <= ci) if dr == 0 else (ri >= ci)
        tri = jnp.where(before, 1.0, 0.0).astype(BF16)
        base = dr * 2 * nh
        lf = _log_sigmoid(gr[base + nh:base + 2 * nh, :]) * LOG2E
        gi = gr[base:base + nh, :] * LOG2E
        lf_st = jnp.concatenate([lf[:, c * L:(c + 1) * L] for c in range(n_chunks)], axis=0)
        b_st = sum(_dot(piece, tri) for piece in _split3(lf_st))
        g_st = jnp.concatenate([gi[:, c * L:(c + 1) * L] for c in range(n_chunks)], axis=0) - b_st
        run = g_st
        step = 1
        while step < L:
            if dr == 0:
                run = jnp.where(lane >= step, jnp.maximum(run, pltpu.roll(run, step, 1)), run)
            else:
                run = jnp.where(lane < L - step, jnp.maximum(run, pltpu.roll(run, L - step, 1)), run)
            step *= 2
        for cidx in range(n_chunks):
            rows = slice(cidx * L, (cidx + 1) * L)
            blk = slice(cidx * nh, (cidx + 1) * nh)
            b_last = jnp.sum(lf[:, rows], axis=1, keepdims=True)
            g_max = jnp.max(g_st[blk, :], axis=1, keepdims=True)
            g_rows.append(g_st[blk, :])
            gr_ref[cidx // per_req, dr, cidx % per_req] = jnp.concatenate(
                [g_st[blk, :], b_st[blk, :], jnp.broadcast_to(b_last, (nh, L)),
                 jnp.broadcast_to(g_max, (nh, L)), run[blk, :]], axis=0)
    g_sq = jnp.concatenate(g_rows + [jnp.zeros((L - len(g_rows) * nh, L), F32)], axis=0).T
    for dr in range(2):
        for cidx in range(n_chunks):
            idx = dr * n_chunks + cidx
            lc = cidx % per_req
            gc_ref[cidx // per_req, dr, lc * L:(lc + 1) * L, :] = g_sq[:, idx * nh:(idx + 1) * nh]

    o = _dot_nt(hb, wt_ref[3 * dm:4 * dm, :])
    g = _dot_nt(hb, wt_ref[4 * dm:5 * dm, :])
    og_ref[...] = (jax.nn.sigmoid(o) * _silu(g)).astype(og_ref.dtype).reshape(og_ref.shape)
    q_ref[...] = _dot_nt(hb, wt_ref[0:dm, :]).astype(q_ref.dtype).reshape(q_ref.shape)
    k = _dot_nt(hb, wt_ref[dm:2 * dm, :]) * (M_HD ** -0.5)
    k_ref[...] = k.astype(k_ref.dtype).reshape(k_ref.shape)
    vt = _dot_nt(wt_ref[2 * dm:3 * dm, :], hb).astype(vt_ref.dtype)
    for cidx in range(n_chunks):
        vt_ref[cidx // per_req, cidx % per_req] = vt[:, cidx * L:(cidx + 1) * L]


def _mlstm_in(x, mod3, mod_row, norm_w, w_t, b_gates, shared_cond):
    bsz, t, d = x.shape
    dm = M_HEADS * M_HD
    ng = 4 * M_HEADS
    tm, n_req = _proj_tiling(bsz, t, shared_cond)
    tok = lambda b, i: (b, i, 0)
    const = lambda b, i: (0, 0)
    big = jax.ShapeDtypeStruct((bsz, t, dm), BF16)
    once = pl.Buffered(1)
    return pl.pallas_call(
        _mlstm_in_kernel,
        out_shape=(big, big, jax.ShapeDtypeStruct((bsz, t // MCHUNK, dm, MCHUNK), BF16), big,
                   jax.ShapeDtypeStruct((bsz, 2, t, M_HEADS), F32),
                   jax.ShapeDtypeStruct((bsz, 2, t // MCHUNK, 5 * M_HEADS, MCHUNK), F32)),
        grid=(bsz // n_req, t // tm),
        in_specs=[pl.BlockSpec((n_req, tm, d), tok),
                  pl.BlockSpec((1, 3, d), lambda b, i: (mod_row(b), 0, 0)),
                  pl.BlockSpec((1, d), const),
                  pl.BlockSpec(w_t.shape, const, pipeline_mode=once),
                  pl.BlockSpec((ng, 1), const)],
        out_specs=(pl.BlockSpec((n_req, tm, dm), tok), pl.BlockSpec((n_req, tm, dm), tok),
                   pl.BlockSpec((n_req, tm // MCHUNK, dm, MCHUNK), lambda b, i: (b, i, 0, 0)),
                   pl.BlockSpec((n_req, tm, dm), tok),
                   pl.BlockSpec((n_req, 2, tm, M_HEADS), lambda b, i: (b, 0, i, 0)),
                   pl.BlockSpec((n_req, 2, tm // MCHUNK, 5 * M_HEADS, MCHUNK),
                                lambda b, i: (b, 0, i, 0, 0))),
        compiler_params=pltpu.CompilerParams(dimension_semantics=("parallel", "parallel"),
                                             vmem_limit_bytes=MLSTM_IN_VMEM_LIMIT),
        name="mlstm_in",
    )(x, mod3, norm_w.reshape(1, d), w_t, b_gates.reshape(ng, 1))


def _mlstm_scan_kernel(*refs, has_init, write_state, nc):
    refs = list(refs)
    q_ref, k_ref, vt_ref, gc_ref, gr_ref, og_ref, x_ref, mod_ref, wo_ref, fw_ref = refs[:10]
    pos = 10
    if has_init:
        c0_ref, n0_ref, m0_ref = refs[pos:pos + 3]
        pos += 3
    y_ref = refs[pos]
    pos += 1
    if write_state:
        cout_ref, nout_ref, mout_ref = refs[pos:pos + 3]
        pos += 3
    ct_scr, mscr, hcur, hfwd = refs[pos:pos + 4]

    drn = pl.program_id(1)
    c = pl.program_id(2)
    n_sub, L = q_ref.shape[1], q_ref.shape[2]
    nh = M_HEADS
    pad = ct_scr.shape[1] - M_HD

    @pl.when(c == 0)
    def _init():
        if has_init:
            for h in range(nh):
                ct_scr[h, 0:M_HD, :] = c0_ref[0, 0, h].T
                ct_scr[h, M_HD:M_HD + pad, :] = jnp.concatenate(
                    [n0_ref[0, 0, h:h + 1, :], jnp.zeros((pad - 1, M_HD), F32)], axis=0)
            mscr[...] = m0_ref[0, 0] * LOG2E
        else:
            ct_scr[...] = jnp.zeros(ct_scr.shape, F32)
            mscr[...] = jnp.zeros(mscr.shape, F32)

    ones_rows = jnp.where(lax.broadcasted_iota(jnp.int32, (pad, L), 0) == 0, 1.0, 0.0).astype(BF16)
    H = L // 2
    si = lax.broadcasted_iota(jnp.int32, (H, H), 0)
    li = lax.broadcasted_iota(jnp.int32, (H, H), 1)

    def chunk_step(sub, fwd):
        gcb = gc_ref[0, 0, sub]
        grb = gr_ref[0, 0, sub]
        q = q_ref[0, sub]
        k = k_ref[0, sub]
        vt = vt_ref[0, sub]
        tri = (si <= li) if fwd else (si >= li)
        lo, hi = slice(0, H), slice(H, L)

        def head_scores(h):
            hs = slice(h * M_HD, (h + 1) * M_HD)
            m_prev = mscr[h:h + 1, 0:1]
            ct = ct_scr[h]
            m_row = jnp.maximum(grb[4 * nh + h:4 * nh + h + 1, :], m_prev)
            r1 = _dot_nt(jnp.concatenate([k[:, hs], ct.astype(BF16)], axis=0), q[:, hs])
            g_c = gcb[:, h:h + 1]

            def quad(ks, qs, masked):
                e = g_c[ks, :] - m_row[:, qs]
                if masked:
                    e = jnp.where(tri, e, NEG_INF)
                return (r1[ks, qs] * jnp.exp2(e)).astype(BF16)

            zero = jnp.zeros((H, H), BF16)
            if fwd:
                s_t = jnp.concatenate(
                    [jnp.concatenate([quad(lo, lo, True), quad(lo, hi, False)], axis=1),
                     jnp.concatenate([zero, quad(hi, hi, True)], axis=1)], axis=0)
            else:
                s_t = jnp.concatenate(
                    [jnp.concatenate([quad(lo, lo, True), zero], axis=1),
                     jnp.concatenate([quad(hi, lo, False), quad(hi, hi, True)], axis=1)], axis=0)
            return m_prev, ct, m_row, s_t, r1[L:, :]

        def head_finish(h, m_prev, ct, m_row, s_t, inter):
            hs = slice(h * M_HD, (h + 1) * M_HD)
            vext = jnp.concatenate([vt[hs, :], ones_rows], axis=0)
            g_r = grb[h:h + 1, :]
            b_r = grb[nh + h:nh + h + 1, :]
            b_last = grb[2 * nh + h:2 * nh + h + 1, 0:1]
            g_max = grb[3 * nh + h:3 * nh + h + 1, 0:1]
            w0 = jnp.exp2(m_prev - m_row)
            tot = _dot(vext, s_t) + w0 * inter
            den = tot[M_HD:M_HD + 1, :]
            floor = jnp.exp2(-(b_r + m_row))
            hcur[sub, hs, :] = tot[0:M_HD, :] / jnp.maximum(jnp.abs(den), floor)

            m_last = jnp.maximum(g_max, m_prev)
            wk = jnp.exp2(g_r - m_last)
            decay = jnp.exp2(m_prev - m_last)
            vw = (vext.astype(F32) * wk).astype(BF16)
            ct_scr[h] = decay * ct + _dot(vw, k[:, hs])
            mscr[h:h + 1, :] = jnp.broadcast_to(b_last + m_last, (1, LANES))

        pending = [head_scores(h) for h in range(min(SCAN_AHEAD, nh))]
        for h in range(nh):
            if h + SCAN_AHEAD < nh:
                pending.append(head_scores(h + SCAN_AHEAD))
            head_finish(h, *pending.pop(0))

    @pl.when(drn == 0)
    def _forward():
        for j in range(n_sub):
            chunk_step(j, True)

    @pl.when(drn == 1)
    def _backward():
        for j in reversed(range(n_sub)):
            chunk_step(j, False)

    @pl.when(drn == 0)
    def _park():
        hfwd[pl.ds(c * n_sub, n_sub)] = hcur[...]

    @pl.when(drn == 1)
    def _emit():
        first = (nc - 1 - c) * n_sub
        hsum_t = jnp.concatenate([hcur[j] + hfwd[first + j] for j in range(n_sub)], axis=1)
        hm = hsum_t.T * og_ref[0].astype(F32)
        y = _dot(hm.astype(wo_ref.dtype), wo_ref[...])
        x2 = x_ref[0] + mod_ref[0][2:3, :] * y
        ms = jnp.mean(x2 * x2, axis=-1, keepdims=True)
        y_ref[0] = x2 * lax.rsqrt(ms + EPS) * fw_ref[...]

    if write_state:
        @pl.when(c == nc - 1)
        def _final():
            for h in range(nh):
                cfin = ct_scr[h]
                cout_ref[0, 0, h] = cfin[0:M_HD, :].T
                nout_ref[0, 0, h:h + 1, :] = cfin[M_HD:M_HD + 1, :]
            mout_ref[0, 0] = mscr[...] * LN2


def _mlstm_scan(q, k, vt, gc, gr, og, x, mod3, mod_row, w_out, final_w, init, write_state):
    bsz, t, dm = q.shape
    d_model = x.shape[-1]
    L = MCHUNK
    assert t % L == 0, (t, L)
    n_sub = SCAN_SUB if (t // L) % SCAN_SUB == 0 else 1
    nc = t // (L * n_sub)
    rows = n_sub * L
    blk = lambda d, c: c + d * (nc - 1 - 2 * c)
    chunked = lambda b, d, c: (b, blk(d, c), 0, 0)
    gated = lambda b, d, c: (b, d, blk(d, c), 0, 0)
    tail = lambda b, d, c: (b, nc - 1 - d * c, 0)
    const = lambda b, d, c: (0, 0)
    in_specs = [pl.BlockSpec((1, n_sub, L, dm), chunked),
                pl.BlockSpec((1, n_sub, L, dm), chunked),
                pl.BlockSpec((1, n_sub, dm, L), chunked),
                pl.BlockSpec((1, 1, n_sub, L, gc.shape[-1]), gated),
                pl.BlockSpec((1, 1, n_sub, gr.shape[3], L), gated),
                pl.BlockSpec((1, rows, dm), tail),
                pl.BlockSpec((1, rows, d_model), tail),
                pl.BlockSpec((1, 3, d_model), lambda b, d, c: (mod_row(b), 0, 0)),
                pl.BlockSpec(w_out.shape, const, pipeline_mode=pl.Buffered(1)),
                pl.BlockSpec((1, d_model), const)]
    args = [q.reshape(bsz, t // L, L, dm), k.reshape(bsz, t // L, L, dm), vt,
            gc.reshape(bsz, 2, t // L, L, gc.shape[-1]), gr, og, x, mod3, w_out,
            final_w.reshape(1, d_model)]
    st = lambda b, d, c: (b, d, 0, 0)
    st5 = lambda b, d, c: (b, d, 0, 0, 0)
    if init is not None:
        c0, n0, m0 = init
        in_specs += [pl.BlockSpec((1, 1, M_HEADS, M_HD, M_HD), st5),
                     pl.BlockSpec((1, 1, M_HEADS, M_HD), st),
                     pl.BlockSpec((1, 1, M_HEADS, LANES), st)]
        args += [c0, n0, jnp.broadcast_to(m0[..., None], m0.shape + (LANES,))]
    out_shape = [jax.ShapeDtypeStruct((bsz, t, d_model), F32)]
    out_specs = [pl.BlockSpec((1, rows, d_model), tail)]
    if write_state:
        out_shape += [jax.ShapeDtypeStruct((bsz, 2, M_HEADS, M_HD, M_HD), F32),
                      jax.ShapeDtypeStruct((bsz, 2, M_HEADS, M_HD), F32),
                      jax.ShapeDtypeStruct((bsz, 2, M_HEADS, LANES), F32)]
        out_specs += [pl.BlockSpec((1, 1, M_HEADS, M_HD, M_HD), st5),
                      pl.BlockSpec((1, 1, M_HEADS, M_HD), st),
                      pl.BlockSpec((1, 1, M_HEADS, LANES), st)]
    return pl.pallas_call(
        functools.partial(_mlstm_scan_kernel, has_init=init is not None,
                          write_state=write_state, nc=nc),
        out_shape=tuple(out_shape),
        grid=(bsz, 2, nc),
        in_specs=in_specs,
        out_specs=tuple(out_specs),
        scratch_shapes=[pltpu.VMEM((M_HEADS, M_HD + BF16_ROWS, M_HD), F32),
                        pltpu.VMEM((M_HEADS, LANES), F32),
                        pltpu.VMEM((n_sub, dm, L), F32),
                        pltpu.VMEM((t // L, dm, L), F32)],
        compiler_params=_cparams(("parallel", "arbitrary", "arbitrary")),
        name="mlstm_scan",
    )(*args)


def _rope_tables(t):
    nf = HEAD_DIM // 4
    pos = jnp.arange(t)
    row = (pos // GRID_W).astype(F32)
    col = (pos % GRID_W).astype(F32)
    inv = ROPE_BASE ** (-jnp.arange(nf, dtype=F32) / nf)
    ar = row[:, None] * inv[None, :]
    ac = col[:, None] * inv[None, :]
    cos = jnp.concatenate([jnp.cos(ar), jnp.cos(ar), jnp.cos(ac), jnp.cos(ac)], axis=1)
    sin = jnp.concatenate([-jnp.sin(ar), jnp.sin(ar), -jnp.sin(ac), jnp.sin(ac)], axis=1)
    reps = LANES // HEAD_DIM
    return jnp.tile(cos, (1, reps)), jnp.tile(sin, (1, reps))


def kernel(x_prompt, x_sample, cache_k, cache_v, state_C, state_n, state_m, c, c_ctx,
           attn_norm_w, attn_ada_w, attn_ada_b, attn_w_in, attn_sink, attn_w_out,
           mlstm_norm_w, mlstm_ada_w, mlstm_ada_b, mlstm_w_in, mlstm_b_gates, mlstm_w_out,
           final_norm_w):
    assert attn_w_in.shape[0] == 1 and mlstm_w_in.shape[0] == 1, "one layer of each mixer"
    bsz, seq, d = x_prompt.shape
    dbsz, dseq, _ = x_sample.shape
    assert d == N_HEADS * HEAD_DIM == M_HEADS * M_HD and dseq % GRID_W == 0, (d, dseq)
    dkv = N_KV_HEADS * HEAD_DIM
    dm = M_HEADS * M_HD

    n_cond = 1 + dbsz
    cond = jnp.concatenate([c_ctx[None, :], c, jnp.zeros((-n_cond % 8, d), F32)], axis=0)
    attn_mod = _ada(cond, attn_ada_w[0], attn_ada_b[0]).reshape(-1, 3, d)
    mlstm_mod = _ada(cond, mlstm_ada_w[0], mlstm_ada_b[0]).reshape(-1, 3, d)
    ctx_row = lambda b: 0
    lat_row = lambda b: b + 1

    attn_w_in0 = attn_w_in[0]
    attn_w_out0 = attn_w_out[0]
    attn_wv_t = attn_w_in[0, :, 2 * N_HEADS * HEAD_DIM + dkv:].T
    mlstm_w_in_t = mlstm_w_in[0].T
    mlstm_w_out0 = mlstm_w_out[0]

    def mlstm_layer(x, mod_row, init, write_state, shared_cond):
        q, k, vt, og, gc, gr = _mlstm_in(x, mlstm_mod, mod_row, mlstm_norm_w[0], mlstm_w_in_t,
                                         mlstm_b_gates[0], shared_cond)
        outs = _mlstm_scan(q, k, vt, gc, gr, og, x, mlstm_mod, mod_row, mlstm_w_out0, final_norm_w,
                           init, write_state)
        return outs[0], outs[1:]

    q, sg, k_ctx, vt_ctx, v_ctx = _attn_in(x_prompt, attn_mod, ctx_row, attn_norm_w[0], attn_w_in0,
                                           attn_wv_t, None, F32, True, True)
    x1 = _attn(q, sg, x_prompt, attn_mod, ctx_row, k_ctx, vt_ctx, None, None, attn_sink[0], attn_w_out0)
    y_prompt, (c_fin, n_fin, m_fin) = mlstm_layer(x1, ctx_row, None, True, True)

    q, sg, k_lat, vt_lat = _attn_in(x_sample, attn_mod, lat_row, attn_norm_w[0], attn_w_in0,
                                    attn_wv_t, _rope_tables(dseq), BF16, False, False)
    kc = cache_k[:, 0].reshape(dbsz, -1, dkv).astype(BF16)
    vct = jnp.swapaxes(cache_v[:, 0].reshape(dbsz, -1, dkv), 1, 2).astype(BF16)
    x1 = _attn(q, sg, x_sample, attn_mod, lat_row, kc, vct, k_lat, vt_lat, attn_sink[0], attn_w_out0)
    y_sample, _ = mlstm_layer(x1, lat_row, (state_C[:, 0], state_n[:, 0], state_m[:, 0]), False, False)

    new_cache_k = k_ctx.reshape(bsz, 1, seq, N_KV_HEADS, HEAD_DIM)
    new_cache_v = v_ctx.reshape(bsz, 1, seq, N_KV_HEADS, HEAD_DIM)
    return (y_prompt, y_sample, new_cache_k, new_cache_v,
            c_fin[:, None], n_fin[:, None], m_fin[:, None, :, :, 0])
```

```python
import functools

import jax
import jax.numpy as jnp
from jax import lax
from jax.experimental import pallas as pl
from jax.experimental.pallas import tpu as pltpu

F32 = jnp.float32
BF16 = jnp.bfloat16

HEAD_DIM = 64
N_KV_HEADS = 4
GROUP = 4
N_HEADS = N_KV_HEADS * GROUP
QBLK = 128
GRID_W = 64
ROPE_BASE = 10000.0
M_HEADS = 8
M_HD = 128
EPS = 1e-6

LANES = 128
BF16_ROWS = 16
VMEM_LIMIT = 48 * 1024 * 1024
MLSTM_IN_VMEM_LIMIT = 60 * 1024 * 1024

MCHUNK = 256
ATTN_QB = 4
ATTN_WAVE_QB = 2
SCAN_SUB = 2
SCAN_AHEAD = 4
PROJ_ROWS = 1024
ADA_TILE = 1024

NEG_INF = float("-inf")
LOG2E = 1.4426950408889634
LN2 = 0.6931471805599453


def _cparams(sem):
    return pltpu.CompilerParams(dimension_semantics=sem, vmem_limit_bytes=VMEM_LIMIT)


def _silu(x):
    return x * jax.nn.sigmoid(x)


def _log_sigmoid(x):
    return jnp.minimum(x, 0.0) - jnp.log1p(jnp.exp(-jnp.abs(x)))


def _dot(a, b):
    return jnp.dot(a, b, preferred_element_type=F32)


def _dot_nt(a, b):
    return lax.dot_general(a, b, (((1,), (1,)), ((), ())), preferred_element_type=F32)


def _split3(x):
    hi = x.astype(BF16)
    r = x - hi.astype(F32)
    mid = r.astype(BF16)
    lo = (r - mid.astype(F32)).astype(BF16)
    return hi, mid, lo


def _prenorm(x, norm_w, mod):
    ms = jnp.mean(x * x, axis=-1, keepdims=True)
    y = x * lax.rsqrt(ms + EPS) * norm_w
    return y * (1.0 + mod[1:2, :]) + mod[0:1, :]


def _ada_kernel(cond_ref, w_ref, b_ref, o_ref):
    a = _silu(cond_ref[...]).astype(BF16)
    o_ref[...] = _dot(a, w_ref[...].astype(BF16)) + b_ref[...]


def _ada(cond8, w, b):
    d, n = w.shape
    tn = ADA_TILE
    return pl.pallas_call(
        _ada_kernel,
        out_shape=jax.ShapeDtypeStruct((cond8.shape[0], n), F32),
        grid=(n // tn,),
        in_specs=[pl.BlockSpec(cond8.shape, lambda j: (0, 0)),
                  pl.BlockSpec((d, tn), lambda j: (0, j)),
                  pl.BlockSpec((1, tn), lambda j: (0, j))],
        out_specs=pl.BlockSpec((cond8.shape[0], tn), lambda j: (0, j)),
        compiler_params=_cparams(("parallel",)),
        name="ada_mod",
    )(cond8, w, b.reshape(1, n))


def _rope(x, cos, sin, lane):
    first = (lane & 31) < 16
    outs = []
    for c in range(x.shape[1] // LANES):
        xc = x[:, c * LANES:(c + 1) * LANES]
        sw = jnp.where(first, pltpu.roll(xc, LANES - 16, 1), pltpu.roll(xc, 16, 1))
        outs.append(xc * cos + sw * sin)
    return jnp.concatenate(outs, axis=1)


def _attn_in_kernel(*refs, rope, emit_v):
    refs = list(refs)
    x_ref, mod_ref, nw_ref, w_ref, wvt_ref = refs[:5]
    pos = 5
    if rope:
        cos_ref, sin_ref = refs[pos:pos + 2]
        pos += 2
    q_ref, sg_ref, k_ref, vt_ref = refs[pos:pos + 4]
    dq = q_ref.shape[-1]
    dkv = k_ref.shape[-1]
    n_req, tr, d = x_ref.shape
    rows = n_req * tr
    hb = _prenorm(x_ref[...].reshape(rows, d), nw_ref[...], mod_ref[0]).astype(w_ref.dtype)
    q = _dot(hb, w_ref[:, 0:dq])
    g = _dot(hb, w_ref[:, dq:2 * dq])
    k = _dot(hb, w_ref[:, 2 * dq:2 * dq + dkv])
    if rope:
        cos = cos_ref[...]
        sin = sin_ref[...]
        lane = lax.broadcasted_iota(jnp.int32, cos.shape, 1)
        q = _rope(q, cos, sin, lane)
        k = _rope(k, cos, sin, lane)
    q_ref[...] = (q * (HEAD_DIM ** -0.5 * LOG2E)).astype(q_ref.dtype).reshape(q_ref.shape)
    sg_ref[...] = _silu(g).astype(sg_ref.dtype).reshape(sg_ref.shape)
    k_ref[...] = k.astype(k_ref.dtype).reshape(k_ref.shape)
    vt = _dot_nt(wvt_ref[...], hb).astype(vt_ref.dtype)
    for r in range(n_req):
        vt_ref[r] = vt[:, r * tr:(r + 1) * tr]
    if emit_v:
        v_ref = refs[pos + 4]
        v = _dot(hb, w_ref[:, 2 * dq + dkv:2 * dq + 2 * dkv])
        v_ref[...] = v.astype(v_ref.dtype).reshape(v_ref.shape)


def _proj_tiling(bsz, t, shared_cond):
    tr = min(PROJ_ROWS, t)
    assert t % tr == 0 and tr % MCHUNK == 0, (t, tr)
    n_req = PROJ_ROWS // tr if shared_cond and bsz % (PROJ_ROWS // tr) == 0 else 1
    return tr, n_req


def _attn_in(x, mod3, mod_row, norm_w, w_in, wv_t, rope_tabs, k_dtype, emit_v, shared_cond):
    bsz, t, d = x.shape
    dq = N_HEADS * HEAD_DIM
    dkv = N_KV_HEADS * HEAD_DIM
    tm, n_req = _proj_tiling(bsz, t, shared_cond)
    rope = rope_tabs is not None
    tok = lambda b, i: (b, i, 0)
    const = lambda b, i: (0, 0)
    in_specs = [pl.BlockSpec((n_req, tm, d), tok),
                pl.BlockSpec((1, 3, d), lambda b, i: (mod_row(b), 0, 0)),
                pl.BlockSpec((1, d), const),
                pl.BlockSpec(w_in.shape, const),
                pl.BlockSpec(wv_t.shape, const)]
    args = [x, mod3, norm_w.reshape(1, d), w_in, wv_t]
    if rope:
        in_specs += [pl.BlockSpec((tm, LANES), lambda b, i: (i, 0))] * 2
        args += list(rope_tabs)
    out_shape = [jax.ShapeDtypeStruct((bsz, t, dq), BF16),
                 jax.ShapeDtypeStruct((bsz, t, dq), BF16),
                 jax.ShapeDtypeStruct((bsz, t, dkv), k_dtype),
                 jax.ShapeDtypeStruct((bsz, dkv, t), BF16)]
    out_specs = [pl.BlockSpec((n_req, tm, dq), tok), pl.BlockSpec((n_req, tm, dq), tok),
                 pl.BlockSpec((n_req, tm, dkv), tok),
                 pl.BlockSpec((n_req, dkv, tm), lambda b, i: (b, 0, i))]
    if emit_v:
        out_shape.append(jax.ShapeDtypeStruct((bsz, t, dkv), F32))
        out_specs.append(pl.BlockSpec((n_req, tm, dkv), tok))
    return pl.pallas_call(
        functools.partial(_attn_in_kernel, rope=rope, emit_v=emit_v),
        out_shape=tuple(out_shape),
        grid=(bsz // n_req, t // tm),
        in_specs=in_specs,
        out_specs=tuple(out_specs),
        compiler_params=_cparams(("parallel", "parallel")),
        name="attn_in_rope" if rope else "attn_in",
    )(*args)


def _attn_kernel(*refs, window, nb):
    if window:
        (q_ref, sg_ref, x_ref, mod_ref, kc_ref, vct_ref, kp_ref, km_ref, kn_ref,
         vpt_ref, vmt_ref, vnt_ref, sink_ref, wo_ref, o_ref, s_scr, p_scr, ot_scr) = refs
    else:
        q_ref, sg_ref, x_ref, mod_ref, kc_ref, vct_ref, sink_ref, wo_ref, o_ref, s_scr, p_scr, ot_scr = refs
    step = pl.program_id(1)
    nqb = q_ref.shape[1] // QBLK
    n_ctx = kc_ref.shape[1] // QBLK
    cols = GROUP * QBLK
    if window:
        kj = lax.broadcasted_iota(jnp.int32, (QBLK, cols), 0)
        qi = lax.broadcasted_iota(jnp.int32, (QBLK, cols), 1) & (QBLK - 1)
        after_diag = kj >= qi
        before_diag = kj <= qi
    ones_rows = jnp.where(lax.broadcasted_iota(jnp.int32, (BF16_ROWS, QBLK), 0) == 0, 1.0, 0.0).astype(BF16)
    n_blk = n_ctx + (3 if window else 0)

    def window_blocks(qb, cs, kp, km, kn, lanes):
        def mid(j):
            sl = slice(j * QBLK, (j + 1) * QBLK)
            return km[0, cs, sl] if lanes else km[0, sl, cs]
        first = kp[0, cs, :] if lanes else kp[0][:, cs]
        last = kn[0, cs, :] if lanes else kn[0][:, cs]
        return [first if qb == 0 else mid(qb - 1), mid(qb), last if qb == nqb - 1 else mid(qb + 1)]

    def block_masks(qb):
        if not window:
            return [None] * n_ctx
        prev_ok = after_diag & (step > 0) if qb == 0 else after_diag
        next_ok = before_diag & (step < nb // nqb - 1) if qb == nqb - 1 else before_diag
        return [None] * n_ctx + [prev_ok, None, next_ok]

    def scores(qb, kvh):
        u = (qb * N_KV_HEADS + kvh) % s_scr.shape[0]
        cs = slice(kvh * HEAD_DIM, (kvh + 1) * HEAD_DIM)
        heads = [kvh * GROUP + j for j in range(GROUP)]
        qq = q_ref[0, qb * QBLK:(qb + 1) * QBLK, :]
        q4 = jnp.concatenate([qq[:, h * HEAD_DIM:(h + 1) * HEAD_DIM] for h in heads], axis=0)
        sink_row = jnp.concatenate(
            [jnp.broadcast_to(sink_ref[0:1, h:h + 1], (1, QBLK)) for h in heads], axis=1) * LOG2E
        keys = [kc_ref[0, j * QBLK:(j + 1) * QBLK, cs].astype(BF16) for j in range(n_ctx)]
        if window:
            keys += window_blocks(qb, cs, kp_ref, km_ref, kn_ref, False)
        st_all = _dot_nt(jnp.concatenate(keys, axis=0), q4)
        macc = jnp.full((8, cols), NEG_INF, F32)
        for j, ok in enumerate(block_masks(qb)):
            s_blk = st_all[j * QBLK:(j + 1) * QBLK, :]
            if ok is not None:
                s_blk = jnp.where(ok, s_blk, NEG_INF)
            s_scr[u, j] = s_blk
            macc = jnp.maximum(macc, jnp.max(s_blk.reshape(QBLK // 8, 8, cols), axis=0))
        return jnp.maximum(jnp.max(macc, axis=0, keepdims=True), sink_row), sink_row

    def weighted_values(qb, kvh, m_row, sink_row):
        u = (qb * N_KV_HEADS + kvh) % s_scr.shape[0]
        cs = slice(kvh * HEAD_DIM, (kvh + 1) * HEAD_DIM)
        for j in range(n_blk):
            p_scr[u, j * QBLK:(j + 1) * QBLK, :] = jnp.exp2(s_scr[u, j] - m_row).astype(BF16)
        vts = [vct_ref[0, cs, j * QBLK:(j + 1) * QBLK] for j in range(n_ctx)]
        if window:
            vts += window_blocks(qb, cs, vpt_ref, vmt_ref, vnt_ref, True)
        vt_ext = jnp.concatenate(
            [jnp.concatenate(vts, axis=1), jnp.tile(ones_rows, (1, n_blk))], axis=0)
        acc = _dot(vt_ext, p_scr[u])
        den = acc[HEAD_DIM:HEAD_DIM + 1, :] + jnp.exp2(sink_row - m_row)
        o_t = acc[0:HEAD_DIM, :] / den
        for j in range(GROUP):
            h = kvh * GROUP + j
            ot_scr[h * HEAD_DIM:(h + 1) * HEAD_DIM, qb * QBLK:(qb + 1) * QBLK] = o_t[:, j * QBLK:(j + 1) * QBLK]

    units = [(qb, kvh) for qb in range(nqb) for kvh in range(N_KV_HEADS)]
    wave = s_scr.shape[0]
    for w0 in range(0, len(units), wave):
        stats = [scores(qb, kvh) for qb, kvh in units[w0:w0 + wave]]
        for (qb, kvh), st in zip(units[w0:w0 + wave], stats):
            weighted_values(qb, kvh, *st)
    z = (ot_scr[...].T * sg_ref[0].astype(F32)).astype(wo_ref.dtype)
    y = _dot(z, wo_ref[...])
    o_ref[0] = x_ref[0] + mod_ref[0][2:3, :] * y


def _attn(q, sg, x, mod3, mod_row, kc, vct, k_lat, vt_lat, sink, w_out):
    bsz, t, d = x.shape
    dq = q.shape[-1]
    dkv = kc.shape[-1]
    p_len = kc.shape[1]
    nb = t // QBLK
    nqb = min(ATTN_QB, nb)
    rows = nqb * QBLK
    assert t % rows == 0 and p_len % QBLK == 0, (t, rows, p_len)
    window = k_lat is not None
    tok = lambda b, i: (b, i, 0)
    in_specs = [pl.BlockSpec((1, rows, dq), tok),
                pl.BlockSpec((1, rows, dq), tok),
                pl.BlockSpec((1, rows, d), tok),
                pl.BlockSpec((1, 3, d), lambda b, i: (mod_row(b), 0, 0)),
                pl.BlockSpec((1, p_len, dkv), lambda b, i: (b, 0, 0)),
                pl.BlockSpec((1, dkv, p_len), lambda b, i: (b, 0, 0))]
    args = [q, sg, x, mod3, kc, vct]
    n_blocks = p_len // QBLK
    if window:
        prev = lambda i: jnp.maximum(i * nqb - 1, 0)
        nxt = lambda i: jnp.minimum((i + 1) * nqb, nb - 1)
        in_specs += [pl.BlockSpec((1, QBLK, dkv), lambda b, i: (b, prev(i), 0)),
                     pl.BlockSpec((1, rows, dkv), tok),
                     pl.BlockSpec((1, QBLK, dkv), lambda b, i: (b, nxt(i), 0)),
                     pl.BlockSpec((1, dkv, QBLK), lambda b, i: (b, 0, prev(i))),
                     pl.BlockSpec((1, dkv, rows), lambda b, i: (b, 0, i)),
                     pl.BlockSpec((1, dkv, QBLK), lambda b, i: (b, 0, nxt(i)))]
        args += [k_lat] * 3 + [vt_lat] * 3
        n_blocks += 3
    in_specs += [pl.BlockSpec((1, N_HEADS), lambda b, i: (0, 0)),
                 pl.BlockSpec(w_out.shape, lambda b, i: (0, 0), pipeline_mode=pl.Buffered(1))]
    args += [sink.reshape(1, N_HEADS), w_out]
    units = min(nqb, ATTN_WAVE_QB) * N_KV_HEADS
    return pl.pallas_call(
        functools.partial(_attn_kernel, window=window, nb=nb),
        out_shape=jax.ShapeDtypeStruct((bsz, t, d), F32),
        grid=(bsz, nb // nqb),
        in_specs=in_specs,
        out_specs=pl.BlockSpec((1, rows, d), tok),
        scratch_shapes=[pltpu.VMEM((units, n_blocks, QBLK, GROUP * QBLK), F32),
                        pltpu.VMEM((units, n_blocks * QBLK, GROUP * QBLK), BF16),
                        pltpu.VMEM((dq, rows), F32)],
        compiler_params=_cparams(("parallel", "parallel")),
        name="attn_window" if window else "attn_ctx",
    )(*args)


def _mlstm_in_kernel(x_ref, mod_ref, nw_ref, wt_ref, bgt_ref,
                     q_ref, k_ref, vt_ref, og_ref, gc_ref, gr_ref):
    dm = q_ref.shape[-1]
    nh = M_HEADS
    L = MCHUNK
    n_req, tr, d = x_ref.shape
    per_req = tr // L
    hb = _prenorm(x_ref[...].reshape(n_req * tr, d), nw_ref[...], mod_ref[0]).astype(wt_ref.dtype)

    gr = _dot_nt(wt_ref[5 * dm:, :], hb) + bgt_ref[...]
    n_chunks = n_req * per_req
    ri = lax.broadcasted_iota(jnp.int32, (L, L), 0)
    ci = lax.broadcasted_iota(jnp.int32, (L, L), 1)
    lane = lax.broadcasted_iota(jnp.int32, (n_chunks * nh, L), 1)
    g_rows = []
    for dr in range(2):
        before = (ri <= ci) if dr == 0 else (ri >= ci)
        tri = jnp.where(before, 1.0, 0.0).astype(BF16)
        base = dr * 2 * nh
        lf = _log_sigmoid(gr[base + nh:base + 2 * nh, :]) * LOG2E
        gi = gr[base:base + nh, :] * LOG2E
        lf_st = jnp.concatenate([lf[:, c * L:(c + 1) * L] for c in range(n_chunks)], axis=0)
        b_st = sum(_dot(piece, tri) for piece in _split3(lf_st))
        g_st = jnp.concatenate([gi[:, c * L:(c + 1) * L] for c in range(n_chunks)], axis=0) - b_st
        run = g_st
        step = 1
        while step < L:
            if dr == 0:
                run = jnp.where(lane >= step, jnp.maximum(run, pltpu.roll(run, step, 1)), run)
            else:
                run = jnp.where(lane < L - step, jnp.maximum(run, pltpu.roll(run, L - step, 1)), run)
            step *= 2
        for cidx in range(n_chunks):
            rows = slice(cidx * L, (cidx + 1) * L)
            blk = slice(cidx * nh, (cidx + 1) * nh)
            b_last = jnp.sum(lf[:, rows], axis=1, keepdims=True)
            g_max = jnp.max(g_st[blk, :], axis=1, keepdims=True)
            g_rows.append(g_st[blk, :])
            gr_ref[cidx // per_req, dr, cidx % per_req] = jnp.concatenate(
                [g_st[blk, :], b_st[blk, :], jnp.broadcast_to(b_last, (nh, L)),
                 jnp.broadcast_to(g_max, (nh, L)), run[blk, :]], axis=0)
    g_sq = jnp.concatenate(g_rows + [jnp.zeros((L - len(g_rows) * nh, L), F32)], axis=0).T
    for dr in range(2):
        for cidx in range(n_chunks):
            idx = dr * n_chunks + cidx
            lc = cidx % per_req
            gc_ref[cidx // per_req, dr, lc * L:(lc + 1) * L, :] = g_sq[:, idx * nh:(idx + 1) * nh]

    o = _dot_nt(hb, wt_ref[3 * dm:4 * dm, :])
    g = _dot_nt(hb, wt_ref[4 * dm:5 * dm, :])
    og_ref[...] = (jax.nn.sigmoid(o) * _silu(g)).astype(og_ref.dtype).reshape(og_ref.shape)
    q_ref[...] = _dot_nt(hb, wt_ref[0:dm, :]).astype(q_ref.dtype).reshape(q_ref.shape)
    k = _dot_nt(hb, wt_ref[dm:2 * dm, :]) * (M_HD ** -0.5)
    k_ref[...] = k.astype(k_ref.dtype).reshape(k_ref.shape)
    vt = _dot_nt(wt_ref[2 * dm:3 * dm, :], hb).astype(vt_ref.dtype)
    for cidx in range(n_chunks):
        vt_ref[cidx // per_req, cidx % per_req] = vt[:, cidx * L:(cidx + 1) * L]


def _mlstm_in(x, mod3, mod_row, norm_w, w_t, b_gates, shared_cond):
    bsz, t, d = x.shape
    dm = M_HEADS * M_HD
    ng = 4 * M_HEADS
    tm, n_req = _proj_tiling(bsz, t, shared_cond)
    tok = lambda b, i: (b, i, 0)
    const = lambda b, i: (0, 0)
    big = jax.ShapeDtypeStruct((bsz, t, dm), BF16)
    once = pl.Buffered(1)
    return pl.pallas_call(
        _mlstm_in_kernel,
        out_shape=(big, big, jax.ShapeDtypeStruct((bsz, t // MCHUNK, dm, MCHUNK), BF16), big,
                   jax.ShapeDtypeStruct((bsz, 2, t, M_HEADS), F32),
                   jax.ShapeDtypeStruct((bsz, 2, t // MCHUNK, 5 * M_HEADS, MCHUNK), F32)),
        grid=(bsz // n_req, t // tm),
        in_specs=[pl.BlockSpec((n_req, tm, d), tok),
                  pl.BlockSpec((1, 3, d), lambda b, i: (mod_row(b), 0, 0)),
                  pl.BlockSpec((1, d), const),
                  pl.BlockSpec(w_t.shape, const, pipeline_mode=once),
                  pl.BlockSpec((ng, 1), const)],
        out_specs=(pl.BlockSpec((n_req, tm, dm), tok), pl.BlockSpec((n_req, tm, dm), tok),
                   pl.BlockSpec((n_req, tm // MCHUNK, dm, MCHUNK), lambda b, i: (b, i, 0, 0)),
                   pl.BlockSpec((n_req, tm, dm), tok),
                   pl.BlockSpec((n_req, 2, tm, M_HEADS), lambda b, i: (b, 0, i, 0)),
                   pl.BlockSpec((n_req, 2, tm // MCHUNK, 5 * M_HEADS, MCHUNK),
                                lambda b, i: (b, 0, i, 0, 0))),
        compiler_params=pltpu.CompilerParams(dimension_semantics=("parallel", "parallel"),
                                             vmem_limit_bytes=MLSTM_IN_VMEM_LIMIT),
        name="mlstm_in",
    )(x, mod3, norm_w.reshape(1, d), w_t, b_gates.reshape(ng, 1))


def _mlstm_scan_kernel(*refs, has_init, write_state, nc):
    refs = list(refs)
    q_ref, k_ref, vt_ref, gc_ref, gr_ref, og_ref, x_ref, mod_ref, wo_ref, fw_ref = refs[:10]
    pos = 10
    if has_init:
        c0_ref, n0_ref, m0_ref = refs[pos:pos + 3]
        pos += 3
    y_ref = refs[pos]
    pos += 1
    if write_state:
        cout_ref, nout_ref, mout_ref = refs[pos:pos + 3]
        pos += 3
    ct_scr, mscr, hcur, hfwd = refs[pos:pos + 4]

    drn = pl.program_id(1)
    c = pl.program_id(2)
    n_sub, L = q_ref.shape[1], q_ref.shape[2]
    nh = M_HEADS
    pad = ct_scr.shape[1] - M_HD

    @pl.when(c == 0)
    def _init():
        if has_init:
            for h in range(nh):
                ct_scr[h, 0:M_HD, :] = c0_ref[0, 0, h].T
                ct_scr[h, M_HD:M_HD + pad, :] = jnp.concatenate(
                    [n0_ref[0, 0, h:h + 1, :], jnp.zeros((pad - 1, M_HD), F32)], axis=0)
            mscr[...] = m0_ref[0, 0] * LOG2E
        else:
            ct_scr[...] = jnp.zeros(ct_scr.shape, F32)
            mscr[...] = jnp.zeros(mscr.shape, F32)

    ones_rows = jnp.where(lax.broadcasted_iota(jnp.int32, (pad, L), 0) == 0, 1.0, 0.0).astype(BF16)
    H = L // 2
    si = lax.broadcasted_iota(jnp.int32, (H, H), 0)
    li = lax.broadcasted_iota(jnp.int32, (H, H), 1)

    def chunk_step(sub, fwd):
        gcb = gc_ref[0, 0, sub]
        grb = gr_ref[0, 0, sub]
        q = q_ref[0, sub]
        k = k_ref[0, sub]
        vt = vt_ref[0, sub]
        tri = (si <= li) if fwd else (si >= li)
        lo, hi = slice(0, H), slice(H, L)

        def head_scores(h):
            hs = slice(h * M_HD, (h + 1) * M_HD)
            m_prev = mscr[h:h + 1, 0:1]
            ct = ct_scr[h]
            m_row = jnp.maximum(grb[4 * nh + h:4 * nh + h + 1, :], m_prev)
            r1 = _dot_nt(jnp.concatenate([k[:, hs], ct.astype(BF16)], axis=0), q[:, hs])
            g_c = gcb[:, h:h + 1]

            def quad(ks, qs, masked):
                e = g_c[ks, :] - m_row[:, qs]
                if masked:
                    e = jnp.where(tri, e, NEG_INF)
                return (r1[ks, qs] * jnp.exp2(e)).astype(BF16)

            zero = jnp.zeros((H, H), BF16)
            if fwd:
                s_t = jnp.concatenate(
                    [jnp.concatenate([quad(lo, lo, True), quad(lo, hi, False)], axis=1),
                     jnp.concatenate([zero, quad(hi, hi, True)], axis=1)], axis=0)
            else:
                s_t = jnp.concatenate(
                    [jnp.concatenate([quad(lo, lo, True), zero], axis=1),
                     jnp.concatenate([quad(hi, lo, False), quad(hi, hi, True)], axis=1)], axis=0)
            return m_prev, ct, m_row, s_t, r1[L:, :]

        def head_finish(h, m_prev, ct, m_row, s_t, inter):
            hs = slice(h * M_HD, (h + 1) * M_HD)
            vext = jnp.concatenate([vt[hs, :], ones_rows], axis=0)
            g_r = grb[h:h + 1, :]
            b_r = grb[nh + h:nh + h + 1, :]
            b_last = grb[2 * nh + h:2 * nh + h + 1, 0:1]
            g_max = grb[3 * nh + h:3 * nh + h + 1, 0:1]
            w0 = jnp.exp2(m_prev - m_row)
            tot = _dot(vext, s_t) + w0 * inter
            den = tot[M_HD:M_HD + 1, :]
            floor = jnp.exp2(-(b_r + m_row))
            h_out = tot[0:M_HD, :] / jnp.maximum(jnp.abs(den), floor)
            if fwd:
                hfwd[c * n_sub + sub, hs, :] = h_out
            else:
                hcur[sub, hs, :] = h_out + hfwd[(nc - 1 - c) * n_sub + sub, hs, :]

            m_last = jnp.maximum(g_max, m_prev)
            wk = jnp.exp2(g_r - m_last)
            decay = jnp.exp2(m_prev - m_last)
            vw = (vext.astype(F32) * wk).astype(BF16)
            ct_scr[h] = decay * ct + _dot(vw, k[:, hs])
            mscr[h:h + 1, :] = jnp.broadcast_to(b_last + m_last, (1, LANES))

        pending = [head_scores(h) for h in range(min(SCAN_AHEAD, nh))]
        for h in range(nh):
            if h + SCAN_AHEAD < nh:
                pending.append(head_scores(h + SCAN_AHEAD))
            head_finish(h, *pending.pop(0))

    @pl.when(drn == 0)
    def _forward():
        for j in range(n_sub):
            chunk_step(j, True)

    @pl.when(drn == 1)
    def _backward():
        for j in reversed(range(n_sub)):
            chunk_step(j, False)

    @pl.when(drn == 1)
    def _emit():
        hsum_t = jnp.concatenate([hcur[j] for j in range(n_sub)], axis=1)
        hm = hsum_t.T * og_ref[0].astype(F32)
        y = _dot(hm.astype(wo_ref.dtype), wo_ref[...])
        x2 = x_ref[0] + mod_ref[0][2:3, :] * y
        ms = jnp.mean(x2 * x2, axis=-1, keepdims=True)
        y_ref[0] = x2 * lax.rsqrt(ms + EPS) * fw_ref[...]

    if write_state:
        @pl.when(c == nc - 1)
        def _final():
            for h in range(nh):
                cfin = ct_scr[h]
                cout_ref[0, 0, h] = cfin[0:M_HD, :].T
                nout_ref[0, 0, h:h + 1, :] = cfin[M_HD:M_HD + 1, :]
            mout_ref[0, 0] = mscr[...] * LN2


def _mlstm_scan(q, k, vt, gc, gr, og, x, mod3, mod_row, w_out, final_w, init, write_state):
    bsz, t, dm = q.shape
    d_model = x.shape[-1]
    L = MCHUNK
    assert t % L == 0, (t, L)
    n_sub = SCAN_SUB if (t // L) % SCAN_SUB == 0 else 1
    nc = t // (L * n_sub)
    rows = n_sub * L
    blk = lambda d, c: c + d * (nc - 1 - 2 * c)
    chunked = lambda b, d, c: (b, blk(d, c), 0, 0)
    gated = lambda b, d, c: (b, d, blk(d, c), 0, 0)
    tail = lambda b, d, c: (b, nc - 1 - d * c, 0)
    const = lambda b, d, c: (0, 0)
    in_specs = [pl.BlockSpec((1, n_sub, L, dm), chunked),
                pl.BlockSpec((1, n_sub, L, dm), chunked),
                pl.BlockSpec((1, n_sub, dm, L), chunked),
                pl.BlockSpec((1, 1, n_sub, L, gc.shape[-1]), gated),
                pl.BlockSpec((1, 1, n_sub, gr.shape[3], L), gated),
                pl.BlockSpec((1, rows, dm), tail),
                pl.BlockSpec((1, rows, d_model), tail),
                pl.BlockSpec((1, 3, d_model), lambda b, d, c: (mod_row(b), 0, 0)),
                pl.BlockSpec(w_out.shape, const, pipeline_mode=pl.Buffered(1)),
                pl.BlockSpec((1, d_model), const)]
    args = [q.reshape(bsz, t // L, L, dm), k.reshape(bsz, t // L, L, dm), vt,
            gc.reshape(bsz, 2, t // L, L, gc.shape[-1]), gr, og, x, mod3, w_out,
            final_w.reshape(1, d_model)]
    st = lambda b, d, c: (b, d, 0, 0)
    st5 = lambda b, d, c: (b, d, 0, 0, 0)
    if init is not None:
        c0, n0, m0 = init
        in_specs += [pl.BlockSpec((1, 1, M_HEADS, M_HD, M_HD), st5),
                     pl.BlockSpec((1, 1, M_HEADS, M_HD), st),
                     pl.BlockSpec((1, 1, M_HEADS, LANES), st)]
        args += [c0, n0, jnp.broadcast_to(m0[..., None], m0.shape + (LANES,))]
    out_shape = [jax.ShapeDtypeStruct((bsz, t, d_model), F32)]
    out_specs = [pl.BlockSpec((1, rows, d_model), tail)]
    if write_state:
        out_shape += [jax.ShapeDtypeStruct((bsz, 2, M_HEADS, M_HD, M_HD), F32),
                      jax.ShapeDtypeStruct((bsz, 2, M_HEADS, M_HD), F32),
                      jax.ShapeDtypeStruct((bsz, 2, M_HEADS, LANES), F32)]
        out_specs += [pl.BlockSpec((1, 1, M_HEADS, M_HD, M_HD), st5),
                      pl.BlockSpec((1, 1, M_HEADS, M_HD), st),
                      pl.BlockSpec((1, 1, M_HEADS, LANES), st)]
    return pl.pallas_call(
        functools.partial(_mlstm_scan_kernel, has_init=init is not None,
                          write_state=write_state, nc=nc),
        out_shape=tuple(out_shape),
        grid=(bsz, 2, nc),
        in_specs=in_specs,
        out_specs=tuple(out_specs),
        scratch_shapes=[pltpu.VMEM((M_HEADS, M_HD + BF16_ROWS, M_HD), F32),
                        pltpu.VMEM((M_HEADS, LANES), F32),
                        pltpu.VMEM((n_sub, dm, L), F32),
                        pltpu.VMEM((t // L, dm, L), F32)],
        compiler_params=_cparams(("parallel", "arbitrary", "arbitrary")),
        name="mlstm_scan",
    )(*args)


def _rope_tables(t):
    nf = HEAD_DIM // 4
    pos = jnp.arange(t)
    row = (pos // GRID_W).astype(F32)
    col = (pos % GRID_W).astype(F32)
    inv = ROPE_BASE ** (-jnp.arange(nf, dtype=F32) / nf)
    ar = row[:, None] * inv[None, :]
    ac = col[:, None] * inv[None, :]
    cos = jnp.concatenate([jnp.cos(ar), jnp.cos(ar), jnp.cos(ac), jnp.cos(ac)], axis=1)
    sin = jnp.concatenate([-jnp.sin(ar), jnp.sin(ar), -jnp.sin(ac), jnp.sin(ac)], axis=1)
    reps = LANES // HEAD_DIM
    return jnp.tile(cos, (1, reps)), jnp.tile(sin, (1, reps))


def kernel(x_prompt, x_sample, cache_k, cache_v, state_C, state_n, state_m, c, c_ctx,
           attn_norm_w, attn_ada_w, attn_ada_b, attn_w_in, attn_sink, attn_w_out,
           mlstm_norm_w, mlstm_ada_w, mlstm_ada_b, mlstm_w_in, mlstm_b_gates, mlstm_w_out,
           final_norm_w):
    assert attn_w_in.shape[0] == 1 and mlstm_w_in.shape[0] == 1, "one layer of each mixer"
    bsz, seq, d = x_prompt.shape
    dbsz, dseq, _ = x_sample.shape
    assert d == N_HEADS * HEAD_DIM == M_HEADS * M_HD and dseq % GRID_W == 0, (d, dseq)
    dkv = N_KV_HEADS * HEAD_DIM
    dm = M_HEADS * M_HD

    n_cond = 1 + dbsz
    cond = jnp.concatenate([c_ctx[None, :], c, jnp.zeros((-n_cond % 8, d), F32)], axis=0)
    attn_mod = _ada(cond, attn_ada_w[0], attn_ada_b[0]).reshape(-1, 3, d)
    mlstm_mod = _ada(cond, mlstm_ada_w[0], mlstm_ada_b[0]).reshape(-1, 3, d)
    ctx_row = lambda b: 0
    lat_row = lambda b: b + 1

    attn_w_in0 = attn_w_in[0]
    attn_w_out0 = attn_w_out[0]
    attn_wv_t = attn_w_in[0, :, 2 * N_HEADS * HEAD_DIM + dkv:].T
    mlstm_w_in_t = mlstm_w_in[0].T
    mlstm_w_out0 = mlstm_w_out[0]

    def mlstm_layer(x, mod_row, init, write_state, shared_cond):
        q, k, vt, og, gc, gr = _mlstm_in(x, mlstm_mod, mod_row, mlstm_norm_w[0], mlstm_w_in_t,
                                         mlstm_b_gates[0], shared_cond)
        outs = _mlstm_scan(q, k, vt, gc, gr, og, x, mlstm_mod, mod_row, mlstm_w_out0, final_norm_w,
                           init, write_state)
        return outs[0], outs[1:]

    q, sg, k_ctx, vt_ctx, v_ctx = _attn_in(x_prompt, attn_mod, ctx_row, attn_norm_w[0], attn_w_in0,
                                           attn_wv_t, None, F32, True, True)
    x1 = _attn(q, sg, x_prompt, attn_mod, ctx_row, k_ctx, vt_ctx, None, None, attn_sink[0], attn_w_out0)
    y_prompt, (c_fin, n_fin, m_fin) = mlstm_layer(x1, ctx_row, None, True, True)

    q, sg, k_lat, vt_lat = _attn_in(x_sample, attn_mod, lat_row, attn_norm_w[0], attn_w_in0,
                                    attn_wv_t, _rope_tables(dseq), BF16, False, False)
    kc = cache_k[:, 0].reshape(dbsz, -1, dkv).astype(BF16)
    vct = jnp.swapaxes(cache_v[:, 0].reshape(dbsz, -1, dkv), 1, 2).astype(BF16)
    x1 = _attn(q, sg, x_sample, attn_mod, lat_row, kc, vct, k_lat, vt_lat, attn_sink[0], attn_w_out0)
    y_sample, _ = mlstm_layer(x1, lat_row, (state_C[:, 0], state_n[:, 0], state_m[:, 0]), False, False)

    new_cache_k = k_ctx.reshape(bsz, 1, seq, N_KV_HEADS, HEAD_DIM)
    new_cache_v = v_ctx.reshape(bsz, 1, seq, N_KV_HEADS, HEAD_DIM)
    return (y_prompt, y_sample, new_cache_k, new_cache_v,
            c_fin[:, None], n_fin[:, None], m_fin[:, None, :, :, 0])
```

```python
import functools

import jax
import jax.numpy as jnp
from jax import lax
from jax.experimental import pallas as pl
from jax.experimental.pallas import tpu as pltpu

F32 = jnp.float32
BF16 = jnp.bfloat16

HEAD_DIM = 64
N_KV_HEADS = 4
GROUP = 4
N_HEADS = N_KV_HEADS * GROUP
QBLK = 128
GRID_W = 64
ROPE_BASE = 10000.0
M_HEADS = 8
M_HD = 128
EPS = 1e-6

LANES = 128
BF16_ROWS = 16
VMEM_LIMIT = 48 * 1024 * 1024
MLSTM_IN_VMEM_LIMIT = 60 * 1024 * 1024

MCHUNK = 256
ATTN_QB = 4
ATTN_WAVE_QB = 2
SCAN_SUB = 2
SCAN_AHEAD = 4
PROJ_ROWS = 1024
ADA_TILE = 1024

NEG_INF = float("-inf")
LOG2E = 1.4426950408889634
LN2 = 0.6931471805599453


def _cparams(sem):
    return pltpu.CompilerParams(dimension_semantics=sem, vmem_limit_bytes=VMEM_LIMIT)


def _silu(x):
    return x * jax.nn.sigmoid(x)


def _log_sigmoid(x):
    return jnp.minimum(x, 0.0) - jnp.log1p(jnp.exp(-jnp.abs(x)))


def _dot(a, b):
    return jnp.dot(a, b, preferred_element_type=F32)


def _dot_nt(a, b):
    return lax.dot_general(a, b, (((1,), (1,)), ((), ())), preferred_element_type=F32)


def _split3(x):
    hi = x.astype(BF16)
    r = x - hi.astype(F32)
    mid = r.astype(BF16)
    lo = (r - mid.astype(F32)).astype(BF16)
    return hi, mid, lo


def _prenorm(x, norm_w, mod):
    ms = jnp.mean(x * x, axis=-1, keepdims=True)
    y = x * lax.rsqrt(ms + EPS) * norm_w
    return y * (1.0 + mod[1:2, :]) + mod[0:1, :]


def _ada_kernel(cond_ref, w_ref, b_ref, o_ref):
    a = _silu(cond_ref[...]).astype(BF16)
    o_ref[...] = _dot(a, w_ref[...].astype(BF16)) + b_ref[...]


def _ada(cond8, w, b):
    d, n = w.shape
    tn = ADA_TILE
    return pl.pallas_call(
        _ada_kernel,
        out_shape=jax.ShapeDtypeStruct((cond8.shape[0], n), F32),
        grid=(n // tn,),
        in_specs=[pl.BlockSpec(cond8.shape, lambda j: (0, 0)),
                  pl.BlockSpec((d, tn), lambda j: (0, j)),
                  pl.BlockSpec((1, tn), lambda j: (0, j))],
        out_specs=pl.BlockSpec((cond8.shape[0], tn), lambda j: (0, j)),
        compiler_params=_cparams(("parallel",)),
        name="ada_mod",
    )(cond8, w, b.reshape(1, n))


def _rope(x, cos, sin, lane):
    first = (lane & 31) < 16
    outs = []
    for c in range(x.shape[1] // LANES):
        xc = x[:, c * LANES:(c + 1) * LANES]
        sw = jnp.where(first, pltpu.roll(xc, LANES - 16, 1), pltpu.roll(xc, 16, 1))
        outs.append(xc * cos + sw * sin)
    return jnp.concatenate(outs, axis=1)


def _attn_in_kernel(*refs, rope, emit_v):
    refs = list(refs)
    x_ref, mod_ref, nw_ref, w_ref, wvt_ref = refs[:5]
    pos = 5
    if rope:
        cos_ref, sin_ref = refs[pos:pos + 2]
        pos += 2
    q_ref, sg_ref, k_ref, vt_ref = refs[pos:pos + 4]
    dq = q_ref.shape[-1]
    dkv = k_ref.shape[-1]
    n_req, tr, d = x_ref.shape
    rows = n_req * tr
    hb = _prenorm(x_ref[...].reshape(rows, d), nw_ref[...], mod_ref[0]).astype(w_ref.dtype)
    q = _dot(hb, w_ref[:, 0:dq])
    g = _dot(hb, w_ref[:, dq:2 * dq])
    k = _dot(hb, w_ref[:, 2 * dq:2 * dq + dkv])
    if rope:
        cos = cos_ref[...]
        sin = sin_ref[...]
        lane = lax.broadcasted_iota(jnp.int32, cos.shape, 1)
        q = _rope(q, cos, sin, lane)
        k = _rope(k, cos, sin, lane)
    q_ref[...] = (q * (HEAD_DIM ** -0.5 * LOG2E)).astype(q_ref.dtype).reshape(q_ref.shape)
    sg_ref[...] = _silu(g).astype(sg_ref.dtype).reshape(sg_ref.shape)
    k_ref[...] = k.astype(k_ref.dtype).reshape(k_ref.shape)
    vt = _dot_nt(wvt_ref[...], hb).astype(vt_ref.dtype)
    for r in range(n_req):
        vt_ref[r] = vt[:, r * tr:(r + 1) * tr]
    if emit_v:
        v_ref = refs[pos + 4]
        v = _dot(hb, w_ref[:, 2 * dq + dkv:2 * dq + 2 * dkv])
        v_ref[...] = v.astype(v_ref.dtype).reshape(v_ref.shape)


def _proj_tiling(bsz, t, shared_cond):
    tr = min(PROJ_ROWS, t)
    assert t % tr == 0 and tr % MCHUNK == 0, (t, tr)
    n_req = PROJ_ROWS // tr if shared_cond and bsz % (PROJ_ROWS // tr) == 0 else 1
    return tr, n_req


def _attn_in(x, mod3, mod_row, norm_w, w_in, wv_t, rope_tabs, k_dtype, emit_v, shared_cond):
    bsz, t, d = x.shape
    dq = N_HEADS * HEAD_DIM
    dkv = N_KV_HEADS * HEAD_DIM
    tm, n_req = _proj_tiling(bsz, t, shared_cond)
    rope = rope_tabs is not None
    tok = lambda b, i: (b, i, 0)
    const = lambda b, i: (0, 0)
    in_specs = [pl.BlockSpec((n_req, tm, d), tok),
                pl.BlockSpec((1, 3, d), lambda b, i: (mod_row(b), 0, 0)),
                pl.BlockSpec((1, d), const),
                pl.BlockSpec(w_in.shape, const),
                pl.BlockSpec(wv_t.shape, const)]
    args = [x, mod3, norm_w.reshape(1, d), w_in, wv_t]
    if rope:
        in_specs += [pl.BlockSpec((tm, LANES), lambda b, i: (i, 0))] * 2
        args += list(rope_tabs)
    out_shape = [jax.ShapeDtypeStruct((bsz, t, dq), BF16),
                 jax.ShapeDtypeStruct((bsz, t, dq), BF16),
                 jax.ShapeDtypeStruct((bsz, t, dkv), k_dtype),
                 jax.ShapeDtypeStruct((bsz, dkv, t), BF16)]
    out_specs = [pl.BlockSpec((n_req, tm, dq), tok), pl.BlockSpec((n_req, tm, dq), tok),
                 pl.BlockSpec((n_req, tm, dkv), tok),
                 pl.BlockSpec((n_req, dkv, tm), lambda b, i: (b, 0, i))]
    if emit_v:
        out_shape.append(jax.ShapeDtypeStruct((bsz, t, dkv), F32))
        out_specs.append(pl.BlockSpec((n_req, tm, dkv), tok))
    return pl.pallas_call(
        functools.partial(_attn_in_kernel, rope=rope, emit_v=emit_v),
        out_shape=tuple(out_shape),
        grid=(bsz // n_req, t // tm),
        in_specs=in_specs,
        out_specs=tuple(out_specs),
        compiler_params=_cparams(("parallel", "parallel")),
        name="attn_in_rope" if rope else "attn_in",
    )(*args)


def _attn_kernel(*refs, window, nb):
    if window:
        (q_ref, sg_ref, x_ref, mod_ref, kc_ref, vct_ref, kp_ref, km_ref, kn_ref,
         vpt_ref, vmt_ref, vnt_ref, sink_ref, wo_ref, o_ref, s_scr, p_scr, ot_scr) = refs
    else:
        q_ref, sg_ref, x_ref, mod_ref, kc_ref, vct_ref, sink_ref, wo_ref, o_ref, s_scr, p_scr, ot_scr = refs
    step = pl.program_id(1)
    nqb = q_ref.shape[1] // QBLK
    n_ctx = kc_ref.shape[1] // QBLK
    cols = GROUP * QBLK
    if window:
        kj = lax.broadcasted_iota(jnp.int32, (QBLK, cols), 0)
        qi = lax.broadcasted_iota(jnp.int32, (QBLK, cols), 1) & (QBLK - 1)
        after_diag = kj >= qi
        before_diag = kj <= qi
    ones_rows = jnp.where(lax.broadcasted_iota(jnp.int32, (BF16_ROWS, QBLK), 0) == 0, 1.0, 0.0).astype(BF16)
    n_blk = n_ctx + (3 if window else 0)

    def window_blocks(qb, cs, kp, km, kn, lanes):
        def mid(j):
            sl = slice(j * QBLK, (j + 1) * QBLK)
            return km[0, cs, sl] if lanes else km[0, sl, cs]
        first = kp[0, cs, :] if lanes else kp[0][:, cs]
        last = kn[0, cs, :] if lanes else kn[0][:, cs]
        return [first if qb == 0 else mid(qb - 1), mid(qb), last if qb == nqb - 1 else mid(qb + 1)]

    def block_masks(qb):
        if not window:
            return [None] * n_ctx
        prev_ok = after_diag & (step > 0) if qb == 0 else after_diag
        next_ok = before_diag & (step < nb // nqb - 1) if qb == nqb - 1 else before_diag
        return [None] * n_ctx + [prev_ok, None, next_ok]

    def scores(qb, kvh):
        u = (qb * N_KV_HEADS + kvh) % s_scr.shape[0]
        cs = slice(kvh * HEAD_DIM, (kvh + 1) * HEAD_DIM)
        heads = [kvh * GROUP + j for j in range(GROUP)]
        qq = q_ref[0, qb * QBLK:(qb + 1) * QBLK, :]
        q4 = jnp.concatenate([qq[:, h * HEAD_DIM:(h + 1) * HEAD_DIM] for h in heads], axis=0)
        sink_row = jnp.concatenate(
            [jnp.broadcast_to(sink_ref[0:1, h:h + 1], (1, QBLK)) for h in heads], axis=1) * LOG2E
        keys = [kc_ref[0, j * QBLK:(j + 1) * QBLK, cs].astype(BF16) for j in range(n_ctx)]
        if window:
            keys += window_blocks(qb, cs, kp_ref, km_ref, kn_ref, False)
        st_all = _dot_nt(jnp.concatenate(keys, axis=0), q4)
        macc = jnp.full((8, cols), NEG_INF, F32)
        for j, ok in enumerate(block_masks(qb)):
            s_blk = st_all[j * QBLK:(j + 1) * QBLK, :]
            if ok is not None:
                s_blk = jnp.where(ok, s_blk, NEG_INF)
            s_scr[u, j] = s_blk
            macc = jnp.maximum(macc, jnp.max(s_blk.reshape(QBLK // 8, 8, cols), axis=0))
        return jnp.maximum(jnp.max(macc, axis=0, keepdims=True), sink_row), sink_row

    def weighted_values(qb, kvh, m_row, sink_row):
        u = (qb * N_KV_HEADS + kvh) % s_scr.shape[0]
        cs = slice(kvh * HEAD_DIM, (kvh + 1) * HEAD_DIM)
        for j in range(n_blk):
            p_scr[u, j * QBLK:(j + 1) * QBLK, :] = jnp.exp2(s_scr[u, j] - m_row).astype(BF16)
        vts = [vct_ref[0, cs, j * QBLK:(j + 1) * QBLK] for j in range(n_ctx)]
        if window:
            vts += window_blocks(qb, cs, vpt_ref, vmt_ref, vnt_ref, True)
        vt_ext = jnp.concatenate(
            [jnp.concatenate(vts, axis=1), jnp.tile(ones_rows, (1, n_blk))], axis=0)
        acc = _dot(vt_ext, p_scr[u])
        den = acc[HEAD_DIM:HEAD_DIM + 1, :] + jnp.exp2(sink_row - m_row)
        o_t = acc[0:HEAD_DIM, :] / den
        for j in range(GROUP):
            h = kvh * GROUP + j
            ot_scr[h * HEAD_DIM:(h + 1) * HEAD_DIM, qb * QBLK:(qb + 1) * QBLK] = o_t[:, j * QBLK:(j + 1) * QBLK]

    units = [(qb, kvh) for qb in range(nqb) for kvh in range(N_KV_HEADS)]
    wave = s_scr.shape[0]
    for w0 in range(0, len(units), wave):
        stats = [scores(qb, kvh) for qb, kvh in units[w0:w0 + wave]]
        for (qb, kvh), st in zip(units[w0:w0 + wave], stats):
            weighted_values(qb, kvh, *st)
    z = (ot_scr[...].T * sg_ref[0].astype(F32)).astype(wo_ref.dtype)
    y = _dot(z, wo_ref[...])
    o_ref[0] = x_ref[0] + mod_ref[0][2:3, :] * y


def _attn(q, sg, x, mod3, mod_row, kc, vct, k_lat, vt_lat, sink, w_out):
    bsz, t, d = x.shape
    dq = q.shape[-1]
    dkv = kc.shape[-1]
    p_len = kc.shape[1]
    nb = t // QBLK
    nqb = min(ATTN_QB, nb)
    rows = nqb * QBLK
    assert t % rows == 0 and p_len % QBLK == 0, (t, rows, p_len)
    window = k_lat is not None
    tok = lambda b, i: (b, i, 0)
    in_specs = [pl.BlockSpec((1, rows, dq), tok),
                pl.BlockSpec((1, rows, dq), tok),
                pl.BlockSpec((1, rows, d), tok),
                pl.BlockSpec((1, 3, d), lambda b, i: (mod_row(b), 0, 0)),
                pl.BlockSpec((1, p_len, dkv), lambda b, i: (b, 0, 0)),
                pl.BlockSpec((1, dkv, p_len), lambda b, i: (b, 0, 0))]
    args = [q, sg, x, mod3, kc, vct]
    n_blocks = p_len // QBLK
    if window:
        prev = lambda i: jnp.maximum(i * nqb - 1, 0)
        nxt = lambda i: jnp.minimum((i + 1) * nqb, nb - 1)
        in_specs += [pl.BlockSpec((1, QBLK, dkv), lambda b, i: (b, prev(i), 0)),
                     pl.BlockSpec((1, rows, dkv), tok),
                     pl.BlockSpec((1, QBLK, dkv), lambda b, i: (b, nxt(i), 0)),
                     pl.BlockSpec((1, dkv, QBLK), lambda b, i: (b, 0, prev(i))),
                     pl.BlockSpec((1, dkv, rows), lambda b, i: (b, 0, i)),
                     pl.BlockSpec((1, dkv, QBLK), lambda b, i: (b, 0, nxt(i)))]
        args += [k_lat] * 3 + [vt_lat] * 3
        n_blocks += 3
    in_specs += [pl.BlockSpec((1, N_HEADS), lambda b, i: (0, 0)),
                 pl.BlockSpec(w_out.shape, lambda b, i: (0, 0), pipeline_mode=pl.Buffered(1))]
    args += [sink.reshape(1, N_HEADS), w_out]
    units = min(nqb, ATTN_WAVE_QB) * N_KV_HEADS
    return pl.pallas_call(
        functools.partial(_attn_kernel, window=window, nb=nb),
        out_shape=jax.ShapeDtypeStruct((bsz, t, d), F32),
        grid=(bsz, nb // nqb),
        in_specs=in_specs,
        out_specs=pl.BlockSpec((1, rows, d), tok),
        scratch_shapes=[pltpu.VMEM((units, n_blocks, QBLK, GROUP * QBLK), F32),
                        pltpu.VMEM((units, n_blocks * QBLK, GROUP * QBLK), BF16),
                        pltpu.VMEM((dq, rows), F32)],
        compiler_params=_cparams(("parallel", "parallel")),
        name="attn_window" if window else "attn_ctx",
    )(*args)


def _mlstm_in_kernel(x_ref, mod_ref, nw_ref, wt_ref, bgt_ref,
                     q_ref, k_ref, vt_ref, og_ref, gc_ref, gr_ref):
    dm = q_ref.shape[-1]
    nh = M_HEADS
    L = MCHUNK
    n_req, tr, d = x_ref.shape
    per_req = tr // L
    hb = _prenorm(x_ref[...].reshape(n_req * tr, d), nw_ref[...], mod_ref[0]).astype(wt_ref.dtype)

    gr = _dot_nt(wt_ref[5 * dm:, :], hb) + bgt_ref[...]
    n_chunks = n_req * per_req
    ri = lax.broadcasted_iota(jnp.int32, (L, L), 0)
    ci = lax.broadcasted_iota(jnp.int32, (L, L), 1)
    lane = lax.broadcasted_iota(jnp.int32, (n_chunks * nh, L), 1)
    g_rows = []
    for dr in range(2):
        before = (ri <= ci) if dr == 0 else (ri >= ci)
        tri = jnp.where(before, 1.0, 0.0).astype(BF16)
        base = dr * 2 * nh
        lf = _log_sigmoid(gr[base + nh:base + 2 * nh, :]) * LOG2E
        gi = gr[base:base + nh, :] * LOG2E
        lf_st = jnp.concatenate([lf[:, c * L:(c + 1) * L] for c in range(n_chunks)], axis=0)
        b_st = sum(_dot(piece, tri) for piece in _split3(lf_st))
        g_st = jnp.concatenate([gi[:, c * L:(c + 1) * L] for c in range(n_chunks)], axis=0) - b_st
        run = g_st
        step = 1
        while step < L:
            if dr == 0:
                run = jnp.where(lane >= step, jnp.maximum(run, pltpu.roll(run, step, 1)), run)
            else:
                run = jnp.where(lane < L - step, jnp.maximum(run, pltpu.roll(run, L - step, 1)), run)
            step *= 2
        for cidx in range(n_chunks):
            rows = slice(cidx * L, (cidx + 1) * L)
            blk = slice(cidx * nh, (cidx + 1) * nh)
            b_last = jnp.sum(lf[:, rows], axis=1, keepdims=True)
            g_max = jnp.max(g_st[blk, :], axis=1, keepdims=True)
            g_rows.append(g_st[blk, :])
            gr_ref[cidx // per_req, dr, cidx % per_req] = jnp.concatenate(
                [g_st[blk, :], b_st[blk, :], jnp.broadcast_to(b_last, (nh, L)),
                 jnp.broadcast_to(g_max, (nh, L)), run[blk, :]], axis=0)
    g_sq = jnp.concatenate(g_rows + [jnp.zeros((L - len(g_rows) * nh, L), F32)], axis=0).T
    for dr in range(2):
        for cidx in range(n_chunks):
            idx = dr * n_chunks + cidx
            lc = cidx % per_req
            gc_ref[cidx // per_req, dr, lc * L:(lc + 1) * L, :] = g_sq[:, idx * nh:(idx + 1) * nh]

    o = _dot_nt(hb, wt_ref[3 * dm:4 * dm, :])
    g = _dot_nt(hb, wt_ref[4 * dm:5 * dm, :])
    og_ref[...] = (jax.nn.sigmoid(o) * _silu(g)).astype(og_ref.dtype).reshape(og_ref.shape)
    q_ref[...] = _dot_nt(hb, wt_ref[0:dm, :]).astype(q_ref.dtype).reshape(q_ref.shape)
    k = _dot_nt(hb, wt_ref[dm:2 * dm, :]) * (M_HD ** -0.5)
    k_ref[...] = k.astype(k_ref.dtype).reshape(k_ref.shape)
    vt = _dot_nt(wt_ref[2 * dm:3 * dm, :], hb).astype(vt_ref.dtype)
    for cidx in range(n_chunks):
        vt_ref[cidx // per_req, cidx % per_req] = vt[:, cidx * L:(cidx + 1) * L]


def _mlstm_in(x, mod3, mod_row, norm_w, w_t, b_gates, shared_cond):
    bsz, t, d = x.shape
    dm = M_HEADS * M_HD
    ng = 4 * M_HEADS
    tm, n_req = _proj_tiling(bsz, t, shared_cond)
    tok = lambda b, i: (b, i, 0)
    const = lambda b, i: (0, 0)
    big = jax.ShapeDtypeStruct((bsz, t, dm), BF16)
    once = pl.Buffered(1)
    return pl.pallas_call(
        _mlstm_in_kernel,
        out_shape=(big, big, jax.ShapeDtypeStruct((bsz, t // MCHUNK, dm, MCHUNK), BF16), big,
                   jax.ShapeDtypeStruct((bsz, 2, t, M_HEADS), F32),
                   jax.ShapeDtypeStruct((bsz, 2, t // MCHUNK, 5 * M_HEADS, MCHUNK), F32)),
        grid=(bsz // n_req, t // tm),
        in_specs=[pl.BlockSpec((n_req, tm, d), tok),
                  pl.BlockSpec((1, 3, d), lambda b, i: (mod_row(b), 0, 0)),
                  pl.BlockSpec((1, d), const),
                  pl.BlockSpec(w_t.shape, const, pipeline_mode=once),
                  pl.BlockSpec((ng, 1), const)],
        out_specs=(pl.BlockSpec((n_req, tm, dm), tok), pl.BlockSpec((n_req, tm, dm), tok),
                   pl.BlockSpec((n_req, tm // MCHUNK, dm, MCHUNK), lambda b, i: (b, i, 0, 0)),
                   pl.BlockSpec((n_req, tm, dm), tok),
                   pl.BlockSpec((n_req, 2, tm, M_HEADS), lambda b, i: (b, 0, i, 0)),
                   pl.BlockSpec((n_req, 2, tm // MCHUNK, 5 * M_HEADS, MCHUNK),
                                lambda b, i: (b, 0, i, 0, 0))),
        compiler_params=pltpu.CompilerParams(dimension_semantics=("parallel", "parallel"),
                                             vmem_limit_bytes=MLSTM_IN_VMEM_LIMIT),
        name="mlstm_in",
    )(x, mod3, norm_w.reshape(1, d), w_t, b_gates.reshape(ng, 1))


def _mlstm_scan_kernel(*refs, has_init, write_state, nc):
    refs = list(refs)
    q_ref, k_ref, vt_ref, gc_ref, gr_ref, og_ref, x_ref, mod_ref, wo_ref, fw_ref = refs[:10]
    pos = 10
    if has_init:
        c0_ref, n0_ref, m0_ref = refs[pos:pos + 3]
        pos += 3
    y_ref = refs[pos]
    pos += 1
    if write_state:
        cout_ref, nout_ref, mout_ref = refs[pos:pos + 3]
        pos += 3
    ct_scr, mscr, hcur, hfwd = refs[pos:pos + 4]

    drn = pl.program_id(1)
    c = pl.program_id(2)
    n_sub, L = q_ref.shape[1], q_ref.shape[2]
    nh = M_HEADS
    pad = ct_scr.shape[1] - M_HD

    @pl.when(c == 0)
    def _init():
        if has_init:
            for h in range(nh):
                ct_scr[h, 0:M_HD, :] = c0_ref[0, 0, h].T
                ct_scr[h, M_HD:M_HD + pad, :] = jnp.concatenate(
                    [n0_ref[0, 0, h:h + 1, :], jnp.zeros((pad - 1, M_HD), F32)], axis=0)
            mscr[...] = m0_ref[0, 0] * LOG2E
        else:
            ct_scr[...] = jnp.zeros(ct_scr.shape, F32)
            mscr[...] = jnp.zeros(mscr.shape, F32)

    ones_rows = jnp.where(lax.broadcasted_iota(jnp.int32, (pad, L), 0) == 0, 1.0, 0.0).astype(BF16)
    H = L // 2
    si = lax.broadcasted_iota(jnp.int32, (H, H), 0)
    li = lax.broadcasted_iota(jnp.int32, (H, H), 1)

    def chunk_step(sub, fwd):
        gcb = gc_ref[0, 0, sub]
        grb = gr_ref[0, 0, sub]
        q = q_ref[0, sub]
        k = k_ref[0, sub]
        vt = vt_ref[0, sub]
        tri = (si <= li) if fwd else (si >= li)
        lo, hi = slice(0, H), slice(H, L)

        def head_scores(h):
            hs = slice(h * M_HD, (h + 1) * M_HD)
            m_prev = mscr[h:h + 1, 0:1]
            ct = ct_scr[h]
            m_row = jnp.maximum(grb[4 * nh + h:4 * nh + h + 1, :], m_prev)
            r1 = _dot_nt(jnp.concatenate([k[:, hs], ct.astype(BF16)], axis=0), q[:, hs])
            g_c = gcb[:, h:h + 1]

            def quad(ks, qs, masked):
                e = g_c[ks, :] - m_row[:, qs]
                if masked:
                    e = jnp.where(tri, e, NEG_INF)
                return (r1[ks, qs] * jnp.exp2(e)).astype(BF16)

            zero = jnp.zeros((H, H), BF16)
            if fwd:
                s_t = jnp.concatenate(
                    [jnp.concatenate([quad(lo, lo, True), quad(lo, hi, False)], axis=1),
                     jnp.concatenate([zero, quad(hi, hi, True)], axis=1)], axis=0)
            else:
                s_t = jnp.concatenate(
                    [jnp.concatenate([quad(lo, lo, True), zero], axis=1),
                     jnp.concatenate([quad(hi, lo, False), quad(hi, hi, True)], axis=1)], axis=0)
            return m_prev, ct, m_row, s_t, r1[L:, :]

        def head_finish(h, m_prev, ct, m_row, s_t, inter):
            hs = slice(h * M_HD, (h + 1) * M_HD)
            vext = jnp.concatenate([vt[hs, :], ones_rows], axis=0)
            g_r = grb[h:h + 1, :]
            b_r = grb[nh + h:nh + h + 1, :]
            b_last = grb[2 * nh + h:2 * nh + h + 1, 0:1]
            g_max = grb[3 * nh + h:3 * nh + h + 1, 0:1]
            w0 = jnp.exp2(m_prev - m_row)
            tot = _dot(vext, s_t) + w0 * inter
            den = tot[M_HD:M_HD + 1, :]
            floor = jnp.exp2(-(b_r + m_row))
            h_out = tot[0:M_HD, :] / jnp.maximum(jnp.abs(den), floor)
            if fwd:
                hfwd[c * n_sub + sub, hs, :] = h_out
            else:
                hcur[sub * L:(sub + 1) * L, hs] = (h_out + hfwd[(nc - 1 - c) * n_sub + sub, hs, :]).T

            m_last = jnp.maximum(g_max, m_prev)
            wk = jnp.exp2(g_r - m_last)
            decay = jnp.exp2(m_prev - m_last)
            vw = (vext.astype(F32) * wk).astype(BF16)
            ct_scr[h] = decay * ct + _dot(vw, k[:, hs])
            mscr[h:h + 1, :] = jnp.broadcast_to(b_last + m_last, (1, LANES))

        pending = [head_scores(h) for h in range(min(SCAN_AHEAD, nh))]
        for h in range(nh):
            if h + SCAN_AHEAD < nh:
                pending.append(head_scores(h + SCAN_AHEAD))
            head_finish(h, *pending.pop(0))

    @pl.when(drn == 0)
    def _forward():
        for j in range(n_sub):
            chunk_step(j, True)

    @pl.when(drn == 1)
    def _backward():
        for j in reversed(range(n_sub)):
            chunk_step(j, False)

    @pl.when(drn == 1)
    def _emit():
        hm = hcur[...] * og_ref[0].astype(F32)
        y = _dot(hm.astype(wo_ref.dtype), wo_ref[...])
        x2 = x_ref[0] + mod_ref[0][2:3, :] * y
        ms = jnp.mean(x2 * x2, axis=-1, keepdims=True)
        y_ref[0] = x2 * lax.rsqrt(ms + EPS) * fw_ref[...]

    if write_state:
        @pl.when(c == nc - 1)
        def _final():
            for h in range(nh):
                cfin = ct_scr[h]
                cout_ref[0, 0, h] = cfin[0:M_HD, :].T
                nout_ref[0, 0, h:h + 1, :] = cfin[M_HD:M_HD + 1, :]
            mout_ref[0, 0] = mscr[...] * LN2


def _mlstm_scan(q, k, vt, gc, gr, og, x, mod3, mod_row, w_out, final_w, init, write_state):
    bsz, t, dm = q.shape
    d_model = x.shape[-1]
    L = MCHUNK
    assert t % L == 0, (t, L)
    n_sub = SCAN_SUB if (t // L) % SCAN_SUB == 0 else 1
    nc = t // (L * n_sub)
    rows = n_sub * L
    blk = lambda d, c: c + d * (nc - 1 - 2 * c)
    chunked = lambda b, d, c: (b, blk(d, c), 0, 0)
    gated = lambda b, d, c: (b, d, blk(d, c), 0, 0)
    tail = lambda b, d, c: (b, nc - 1 - d * c, 0)
    const = lambda b, d, c: (0, 0)
    in_specs = [pl.BlockSpec((1, n_sub, L, dm), chunked),
                pl.BlockSpec((1, n_sub, L, dm), chunked),
                pl.BlockSpec((1, n_sub, dm, L), chunked),
                pl.BlockSpec((1, 1, n_sub, L, gc.shape[-1]), gated),
                pl.BlockSpec((1, 1, n_sub, gr.shape[3], L), gated),
                pl.BlockSpec((1, rows, dm), tail),
                pl.BlockSpec((1, rows, d_model), tail),
                pl.BlockSpec((1, 3, d_model), lambda b, d, c: (mod_row(b), 0, 0)),
                pl.BlockSpec(w_out.shape, const, pipeline_mode=pl.Buffered(1)),
                pl.BlockSpec((1, d_model), const)]
    args = [q.reshape(bsz, t // L, L, dm), k.reshape(bsz, t // L, L, dm), vt,
            gc.reshape(bsz, 2, t // L, L, gc.shape[-1]), gr, og, x, mod3, w_out,
            final_w.reshape(1, d_model)]
    st = lambda b, d, c: (b, d, 0, 0)
    st5 = lambda b, d, c: (b, d, 0, 0, 0)
    if init is not None:
        c0, n0, m0 = init
        in_specs += [pl.BlockSpec((1, 1, M_HEADS, M_HD, M_HD), st5),
                     pl.BlockSpec((1, 1, M_HEADS, M_HD), st),
                     pl.BlockSpec((1, 1, M_HEADS, LANES), st)]
        args += [c0, n0, jnp.broadcast_to(m0[..., None], m0.shape + (LANES,))]
    out_shape = [jax.ShapeDtypeStruct((bsz, t, d_model), F32)]
    out_specs = [pl.BlockSpec((1, rows, d_model), tail)]
    if write_state:
        out_shape += [jax.ShapeDtypeStruct((bsz, 2, M_HEADS, M_HD, M_HD), F32),
                      jax.ShapeDtypeStruct((bsz, 2, M_HEADS, M_HD), F32),
                      jax.ShapeDtypeStruct((bsz, 2, M_HEADS, LANES), F32)]
        out_specs += [pl.BlockSpec((1, 1, M_HEADS, M_HD, M_HD), st5),
                      pl.BlockSpec((1, 1, M_HEADS, M_HD), st),
                      pl.BlockSpec((1, 1, M_HEADS, LANES), st)]
    return pl.pallas_call(
        functools.partial(_mlstm_scan_kernel, has_init=init is not None,
                          write_state=write_state, nc=nc),
        out_shape=tuple(out_shape),
        grid=(bsz, 2, nc),
        in_specs=in_specs,
        out_specs=tuple(out_specs),
        scratch_shapes=[pltpu.VMEM((M_HEADS, M_HD + BF16_ROWS, M_HD), F32),
                        pltpu.VMEM((M_HEADS, LANES), F32),
                        pltpu.VMEM((rows, dm), F32),
                        pltpu.VMEM((t // L, dm, L), F32)],
        compiler_params=_cparams(("parallel", "arbitrary", "arbitrary")),
        name="mlstm_scan",
    )(*args)


def _rope_tables(t):
    nf = HEAD_DIM // 4
    pos = jnp.arange(t)
    row = (pos // GRID_W).astype(F32)
    col = (pos % GRID_W).astype(F32)
    inv = ROPE_BASE ** (-jnp.arange(nf, dtype=F32) / nf)
    ar = row[:, None] * inv[None, :]
    ac = col[:, None] * inv[None, :]
    cos = jnp.concatenate([jnp.cos(ar), jnp.cos(ar), jnp.cos(ac), jnp.cos(ac)], axis=1)
    sin = jnp.concatenate([-jnp.sin(ar), jnp.sin(ar), -jnp.sin(ac), jnp.sin(ac)], axis=1)
    reps = LANES // HEAD_DIM
    return jnp.tile(cos, (1, reps)), jnp.tile(sin, (1, reps))


def kernel(x_prompt, x_sample, cache_k, cache_v, state_C, state_n, state_m, c, c_ctx,
           attn_norm_w, attn_ada_w, attn_ada_b, attn_w_in, attn_sink, attn_w_out,
           mlstm_norm_w, mlstm_ada_w, mlstm_ada_b, mlstm_w_in, mlstm_b_gates, mlstm_w_out,
           final_norm_w):
    assert attn_w_in.shape[0] == 1 and mlstm_w_in.shape[0] == 1, "one layer of each mixer"
    bsz, seq, d = x_prompt.shape
    dbsz, dseq, _ = x_sample.shape
    assert d == N_HEADS * HEAD_DIM == M_HEADS * M_HD and dseq % GRID_W == 0, (d, dseq)
    dkv = N_KV_HEADS * HEAD_DIM
    dm = M_HEADS * M_HD

    n_cond = 1 + dbsz
    cond = jnp.concatenate([c_ctx[None, :], c, jnp.zeros((-n_cond % 8, d), F32)], axis=0)
    attn_mod = _ada(cond, attn_ada_w[0], attn_ada_b[0]).reshape(-1, 3, d)
    mlstm_mod = _ada(cond, mlstm_ada_w[0], mlstm_ada_b[0]).reshape(-1, 3, d)
    ctx_row = lambda b: 0
    lat_row = lambda b: b + 1

    attn_w_in0 = attn_w_in[0]
    attn_w_out0 = attn_w_out[0]
    attn_wv_t = attn_w_in[0, :, 2 * N_HEADS * HEAD_DIM + dkv:].T
    mlstm_w_in_t = mlstm_w_in[0].T
    mlstm_w_out0 = mlstm_w_out[0]

    def mlstm_layer(x, mod_row, init, write_state, shared_cond):
        q, k, vt, og, gc, gr = _mlstm_in(x, mlstm_mod, mod_row, mlstm_norm_w[0], mlstm_w_in_t,
                                         mlstm_b_gates[0], shared_cond)
        outs = _mlstm_scan(q, k, vt, gc, gr, og, x, mlstm_mod, mod_row, mlstm_w_out0, final_norm_w,
                           init, write_state)
        return outs[0], outs[1:]

    q, sg, k_ctx, vt_ctx, v_ctx = _attn_in(x_prompt, attn_mod, ctx_row, attn_norm_w[0], attn_w_in0,
                                           attn_wv_t, None, F32, True, True)
    x1 = _attn(q, sg, x_prompt, attn_mod, ctx_row, k_ctx, vt_ctx, None, None, attn_sink[0], attn_w_out0)
    y_prompt, (c_fin, n_fin, m_fin) = mlstm_layer(x1, ctx_row, None, True, True)

    q, sg, k_lat, vt_lat = _attn_in(x_sample, attn_mod, lat_row, attn_norm_w[0], attn_w_in0,
                                    attn_wv_t, _rope_tables(dseq), BF16, False, False)
    kc = cache_k[:, 0].reshape(dbsz, -1, dkv).astype(BF16)
    vct = jnp.swapaxes(cache_v[:, 0].reshape(dbsz, -1, dkv), 1, 2).astype(BF16)
    x1 = _attn(q, sg, x_sample, attn_mod, lat_row, kc, vct, k_lat, vt_lat, attn_sink[0], attn_w_out0)
    y_sample, _ = mlstm_layer(x1, lat_row, (state_C[:, 0], state_n[:, 0], state_m[:, 0]), False, False)

    new_cache_k = k_ctx.reshape(bsz, 1, seq, N_KV_HEADS, HEAD_DIM)
    new_cache_v = v_ctx.reshape(bsz, 1, seq, N_KV_HEADS, HEAD_DIM)
    return (y_prompt, y_sample, new_cache_k, new_cache_v,
            c_fin[:, None], n_fin[:, None], m_fin[:, None, :, :, 0])
```

```python
import functools

import jax
import jax.numpy as jnp
from jax import lax
from jax.experimental import pallas as pl
from jax.experimental.pallas import tpu as pltpu

F32 = jnp.float32
BF16 = jnp.bfloat16

HEAD_DIM = 64
N_KV_HEADS = 4
GROUP = 4
N_HEADS = N_KV_HEADS * GROUP
QBLK = 128
GRID_W = 64
ROPE_BASE = 10000.0
M_HEADS = 8
M_HD = 128
EPS = 1e-6

LANES = 128
BF16_ROWS = 16
VMEM_LIMIT = 48 * 1024 * 1024
MLSTM_IN_VMEM_LIMIT = 60 * 1024 * 1024

MCHUNK = 256
ATTN_QB = 4
ATTN_WAVE_QB = 2
SCAN_SUB = 2
SCAN_AHEAD = 4
PROJ_ROWS = 1024
ADA_TILE = 1024

NEG_INF = float("-inf")
LOG2E = 1.4426950408889634
LN2 = 0.6931471805599453


def _cparams(sem):
    return pltpu.CompilerParams(dimension_semantics=sem, vmem_limit_bytes=VMEM_LIMIT)


def _silu(x):
    return x * jax.nn.sigmoid(x)


def _log_sigmoid(x):
    return jnp.minimum(x, 0.0) - jnp.log1p(jnp.exp(-jnp.abs(x)))


def _dot(a, b):
    return jnp.dot(a, b, preferred_element_type=F32)


def _dot_nt(a, b):
    return lax.dot_general(a, b, (((1,), (1,)), ((), ())), preferred_element_type=F32)


def _split3(x):
    hi = x.astype(BF16)
    r = x - hi.astype(F32)
    mid = r.astype(BF16)
    lo = (r - mid.astype(F32)).astype(BF16)
    return hi, mid, lo


def _prenorm(x, norm_w, mod):
    ms = jnp.mean(x * x, axis=-1, keepdims=True)
    y = x * lax.rsqrt(ms + EPS) * norm_w
    return y * (1.0 + mod[1:2, :]) + mod[0:1, :]


def _ada_kernel(cond_ref, w_ref, b_ref, o_ref):
    a = _silu(cond_ref[...]).astype(BF16)
    o_ref[...] = _dot(a, w_ref[...].astype(BF16)) + b_ref[...]


def _ada(cond8, w, b):
    d, n = w.shape
    tn = ADA_TILE
    return pl.pallas_call(
        _ada_kernel,
        out_shape=jax.ShapeDtypeStruct((cond8.shape[0], n), F32),
        grid=(n // tn,),
        in_specs=[pl.BlockSpec(cond8.shape, lambda j: (0, 0)),
                  pl.BlockSpec((d, tn), lambda j: (0, j)),
                  pl.BlockSpec((1, tn), lambda j: (0, j))],
        out_specs=pl.BlockSpec((cond8.shape[0], tn), lambda j: (0, j)),
        compiler_params=_cparams(("parallel",)),
        name="ada_mod",
    )(cond8, w, b.reshape(1, n))


def _rope(x, cos, sin, lane):
    first = (lane & 31) < 16
    outs = []
    for c in range(x.shape[1] // LANES):
        xc = x[:, c * LANES:(c + 1) * LANES]
        sw = jnp.where(first, pltpu.roll(xc, LANES - 16, 1), pltpu.roll(xc, 16, 1))
        outs.append(xc * cos + sw * sin)
    return jnp.concatenate(outs, axis=1)


def _attn_in_kernel(*refs, rope, emit_v):
    refs = list(refs)
    x_ref, mod_ref, nw_ref, w_ref, wvt_ref = refs[:5]
    pos = 5
    if rope:
        cos_ref, sin_ref = refs[pos:pos + 2]
        pos += 2
    q_ref, sg_ref, k_ref, vt_ref = refs[pos:pos + 4]
    dq = q_ref.shape[-1]
    dkv = k_ref.shape[-1]
    n_req, tr, d = x_ref.shape
    rows = n_req * tr
    hb = _prenorm(x_ref[...].reshape(rows, d), nw_ref[...], mod_ref[0]).astype(w_ref.dtype)
    q = _dot(hb, w_ref[:, 0:dq])
    g = _dot(hb, w_ref[:, dq:2 * dq])
    k = _dot(hb, w_ref[:, 2 * dq:2 * dq + dkv])
    if rope:
        cos = cos_ref[...]
        sin = sin_ref[...]
        lane = lax.broadcasted_iota(jnp.int32, cos.shape, 1)
        q = _rope(q, cos, sin, lane)
        k = _rope(k, cos, sin, lane)
    q_ref[...] = (q * (HEAD_DIM ** -0.5 * LOG2E)).astype(q_ref.dtype).reshape(q_ref.shape)
    sg_ref[...] = _silu(g).astype(sg_ref.dtype).reshape(sg_ref.shape)
    k_ref[...] = k.astype(k_ref.dtype).reshape(k_ref.shape)
    vt = _dot_nt(wvt_ref[...], hb).astype(vt_ref.dtype)
    for r in range(n_req):
        vt_ref[r] = vt[:, r * tr:(r + 1) * tr]
    if emit_v:
        v_ref = refs[pos + 4]
        v = _dot(hb, w_ref[:, 2 * dq + dkv:2 * dq + 2 * dkv])
        v_ref[...] = v.astype(v_ref.dtype).reshape(v_ref.shape)


def _proj_tiling(bsz, t, shared_cond):
    tr = min(PROJ_ROWS, t)
    assert t % tr == 0 and tr % MCHUNK == 0, (t, tr)
    n_req = PROJ_ROWS // tr if shared_cond and bsz % (PROJ_ROWS // tr) == 0 else 1
    return tr, n_req


def _attn_in(x, mod3, mod_row, norm_w, w_in, wv_t, rope_tabs, k_dtype, emit_v, shared_cond):
    bsz, t, d = x.shape
    dq = N_HEADS * HEAD_DIM
    dkv = N_KV_HEADS * HEAD_DIM
    tm, n_req = _proj_tiling(bsz, t, shared_cond)
    rope = rope_tabs is not None
    tok = lambda b, i: (b, i, 0)
    const = lambda b, i: (0, 0)
    in_specs = [pl.BlockSpec((n_req, tm, d), tok),
                pl.BlockSpec((1, 3, d), lambda b, i: (mod_row(b), 0, 0)),
                pl.BlockSpec((1, d), const),
                pl.BlockSpec(w_in.shape, const),
                pl.BlockSpec(wv_t.shape, const)]
    args = [x, mod3, norm_w.reshape(1, d), w_in, wv_t]
    if rope:
        in_specs += [pl.BlockSpec((tm, LANES), lambda b, i: (i, 0))] * 2
        args += list(rope_tabs)
    out_shape = [jax.ShapeDtypeStruct((bsz, t, dq), BF16),
                 jax.ShapeDtypeStruct((bsz, t, dq), BF16),
                 jax.ShapeDtypeStruct((bsz, t, dkv), k_dtype),
                 jax.ShapeDtypeStruct((bsz, dkv, t), BF16)]
    out_specs = [pl.BlockSpec((n_req, tm, dq), tok), pl.BlockSpec((n_req, tm, dq), tok),
                 pl.BlockSpec((n_req, tm, dkv), tok),
                 pl.BlockSpec((n_req, dkv, tm), lambda b, i: (b, 0, i))]
    if emit_v:
        out_shape.append(jax.ShapeDtypeStruct((bsz, t, dkv), F32))
        out_specs.append(pl.BlockSpec((n_req, tm, dkv), tok))
    return pl.pallas_call(
        functools.partial(_attn_in_kernel, rope=rope, emit_v=emit_v),
        out_shape=tuple(out_shape),
        grid=(bsz // n_req, t // tm),
        in_specs=in_specs,
        out_specs=tuple(out_specs),
        compiler_params=_cparams(("parallel", "parallel")),
        name="attn_in_rope" if rope else "attn_in",
    )(*args)


def _attn_kernel(*refs, window, nb):
    if window:
        (q_ref, sg_ref, x_ref, mod_ref, kc_ref, vct_ref, kp_ref, km_ref, kn_ref,
         vpt_ref, vmt_ref, vnt_ref, sink_ref, wo_ref, o_ref, s_scr, p_scr, ot_scr) = refs
    else:
        q_ref, sg_ref, x_ref, mod_ref, kc_ref, vct_ref, sink_ref, wo_ref, o_ref, s_scr, p_scr, ot_scr = refs
    step = pl.program_id(1)
    nqb = q_ref.shape[1] // QBLK
    n_ctx = kc_ref.shape[1] // QBLK
    cols = GROUP * QBLK
    if window:
        kj = lax.broadcasted_iota(jnp.int32, (QBLK, cols), 0)
        qi = lax.broadcasted_iota(jnp.int32, (QBLK, cols), 1) & (QBLK - 1)
        after_diag = kj >= qi
        before_diag = kj <= qi
    ones_rows = jnp.where(lax.broadcasted_iota(jnp.int32, (BF16_ROWS, QBLK), 0) == 0, 1.0, 0.0).astype(BF16)
    n_blk = n_ctx + (3 if window else 0)

    def window_blocks(qb, cs, kp, km, kn, lanes):
        def mid(j):
            sl = slice(j * QBLK, (j + 1) * QBLK)
            return km[0, cs, sl] if lanes else km[0, sl, cs]
        first = kp[0, cs, :] if lanes else kp[0][:, cs]
        last = kn[0, cs, :] if lanes else kn[0][:, cs]
        return [first if qb == 0 else mid(qb - 1), mid(qb), last if qb == nqb - 1 else mid(qb + 1)]

    def block_masks(qb):
        if not window:
            return [None] * n_ctx
        prev_ok = after_diag & (step > 0) if qb == 0 else after_diag
        next_ok = before_diag & (step < nb // nqb - 1) if qb == nqb - 1 else before_diag
        return [None] * n_ctx + [prev_ok, None, next_ok]

    def scores(qb, kvh):
        u = (qb * N_KV_HEADS + kvh) % s_scr.shape[0]
        cs = slice(kvh * HEAD_DIM, (kvh + 1) * HEAD_DIM)
        heads = [kvh * GROUP + j for j in range(GROUP)]
        qq = q_ref[0, qb * QBLK:(qb + 1) * QBLK, :]
        q4 = jnp.concatenate([qq[:, h * HEAD_DIM:(h + 1) * HEAD_DIM] for h in heads], axis=0)
        sink_row = jnp.concatenate(
            [jnp.broadcast_to(sink_ref[0:1, h:h + 1], (1, QBLK)) for h in heads], axis=1) * LOG2E
        keys = [kc_ref[0, j * QBLK:(j + 1) * QBLK, cs].astype(BF16) for j in range(n_ctx)]
        if window:
            keys += window_blocks(qb, cs, kp_ref, km_ref, kn_ref, False)
        st_all = _dot_nt(jnp.concatenate(keys, axis=0), q4)
        macc = jnp.full((8, cols), NEG_INF, F32)
        for j, ok in enumerate(block_masks(qb)):
            s_blk = st_all[j * QBLK:(j + 1) * QBLK, :]
            if ok is not None:
                s_blk = jnp.where(ok, s_blk, NEG_INF)
            s_scr[u, j] = s_blk
            macc = jnp.maximum(macc, jnp.max(s_blk.reshape(QBLK // 8, 8, cols), axis=0))
        return jnp.maximum(jnp.max(macc, axis=0, keepdims=True), sink_row), sink_row

    def weighted_values(qb, kvh, m_row, sink_row):
        u = (qb * N_KV_HEADS + kvh) % s_scr.shape[0]
        cs = slice(kvh * HEAD_DIM, (kvh + 1) * HEAD_DIM)
        for j in range(n_blk):
            p_scr[u, j * QBLK:(j + 1) * QBLK, :] = jnp.exp2(s_scr[u, j] - m_row).astype(BF16)
        vts = [vct_ref[0, cs, j * QBLK:(j + 1) * QBLK] for j in range(n_ctx)]
        if window:
            vts += window_blocks(qb, cs, vpt_ref, vmt_ref, vnt_ref, True)
        vt_ext = jnp.concatenate(
            [jnp.concatenate(vts, axis=1), jnp.tile(ones_rows, (1, n_blk))], axis=0)
        acc = _dot(vt_ext, p_scr[u])
        den = acc[HEAD_DIM:HEAD_DIM + 1, :] + jnp.exp2(sink_row - m_row)
        o_t = acc[0:HEAD_DIM, :] / den
        for j in range(GROUP):
            h = kvh * GROUP + j
            ot_scr[h * HEAD_DIM:(h + 1) * HEAD_DIM, qb * QBLK:(qb + 1) * QBLK] = o_t[:, j * QBLK:(j + 1) * QBLK]

    units = [(qb, kvh) for qb in range(nqb) for kvh in range(N_KV_HEADS)]
    wave = s_scr.shape[0]
    for w0 in range(0, len(units), wave):
        stats = [scores(qb, kvh) for qb, kvh in units[w0:w0 + wave]]
        for (qb, kvh), st in zip(units[w0:w0 + wave], stats):
            weighted_values(qb, kvh, *st)
    z = (ot_scr[...].T * sg_ref[0].astype(F32)).astype(wo_ref.dtype)
    y = _dot(z, wo_ref[...])
    o_ref[0] = x_ref[0] + mod_ref[0][2:3, :] * y


def _attn(q, sg, x, mod3, mod_row, kc, vct, k_lat, vt_lat, sink, w_out):
    bsz, t, d = x.shape
    dq = q.shape[-1]
    dkv = kc.shape[-1]
    p_len = kc.shape[1]
    nb = t // QBLK
    nqb = min(ATTN_QB, nb)
    rows = nqb * QBLK
    assert t % rows == 0 and p_len % QBLK == 0, (t, rows, p_len)
    window = k_lat is not None
    tok = lambda b, i: (b, i, 0)
    in_specs = [pl.BlockSpec((1, rows, dq), tok),
                pl.BlockSpec((1, rows, dq), tok),
                pl.BlockSpec((1, rows, d), tok),
                pl.BlockSpec((1, 3, d), lambda b, i: (mod_row(b), 0, 0)),
                pl.BlockSpec((1, p_len, dkv), lambda b, i: (b, 0, 0)),
                pl.BlockSpec((1, dkv, p_len), lambda b, i: (b, 0, 0))]
    args = [q, sg, x, mod3, kc, vct]
    n_blocks = p_len // QBLK
    if window:
        prev = lambda i: jnp.maximum(i * nqb - 1, 0)
        nxt = lambda i: jnp.minimum((i + 1) * nqb, nb - 1)
        in_specs += [pl.BlockSpec((1, QBLK, dkv), lambda b, i: (b, prev(i), 0)),
                     pl.BlockSpec((1, rows, dkv), tok),
                     pl.BlockSpec((1, QBLK, dkv), lambda b, i: (b, nxt(i), 0)),
                     pl.BlockSpec((1, dkv, QBLK), lambda b, i: (b, 0, prev(i))),
                     pl.BlockSpec((1, dkv, rows), lambda b, i: (b, 0, i)),
                     pl.BlockSpec((1, dkv, QBLK), lambda b, i: (b, 0, nxt(i)))]
        args += [k_lat] * 3 + [vt_lat] * 3
        n_blocks += 3
    in_specs += [pl.BlockSpec((1, N_HEADS), lambda b, i: (0, 0)),
                 pl.BlockSpec(w_out.shape, lambda b, i: (0, 0), pipeline_mode=pl.Buffered(1))]
    args += [sink.reshape(1, N_HEADS), w_out]
    units = min(nqb, ATTN_WAVE_QB) * N_KV_HEADS
    return pl.pallas_call(
        functools.partial(_attn_kernel, window=window, nb=nb),
        out_shape=jax.ShapeDtypeStruct((bsz, t, d), F32),
        grid=(bsz, nb // nqb),
        in_specs=in_specs,
        out_specs=pl.BlockSpec((1, rows, d), tok),
        scratch_shapes=[pltpu.VMEM((units, n_blocks, QBLK, GROUP * QBLK), F32),
                        pltpu.VMEM((units, n_blocks * QBLK, GROUP * QBLK), BF16),
                        pltpu.VMEM((dq, rows), F32)],
        compiler_params=_cparams(("parallel", "parallel")),
        name="attn_window" if window else "attn_ctx",
    )(*args)


def _mlstm_in_kernel(x_ref, mod_ref, nw_ref, wt_ref, bgt_ref,
                     q_ref, k_ref, vt_ref, og_ref, gc_ref, gr_ref):
    dm = q_ref.shape[-1]
    nh = M_HEADS
    L = MCHUNK
    n_req, tr, d = x_ref.shape
    per_req = tr // L
    hb = _prenorm(x_ref[...].reshape(n_req * tr, d), nw_ref[...], mod_ref[0]).astype(wt_ref.dtype)

    gr = _dot_nt(wt_ref[5 * dm:, :], hb) + bgt_ref[...]
    n_chunks = n_req * per_req
    ri = lax.broadcasted_iota(jnp.int32, (L, L), 0)
    ci = lax.broadcasted_iota(jnp.int32, (L, L), 1)
    lane = lax.broadcasted_iota(jnp.int32, (n_chunks * nh, L), 1)
    g_rows = []
    for dr in range(2):
        before = (ri <= ci) if dr == 0 else (ri >= ci)
        tri = jnp.where(before, 1.0, 0.0).astype(BF16)
        base = dr * 2 * nh
        lf = _log_sigmoid(gr[base + nh:base + 2 * nh, :]) * LOG2E
        gi = gr[base:base + nh, :] * LOG2E
        lf_st = jnp.concatenate([lf[:, c * L:(c + 1) * L] for c in range(n_chunks)], axis=0)
        b_st = sum(_dot(piece, tri) for piece in _split3(lf_st))
        g_st = jnp.concatenate([gi[:, c * L:(c + 1) * L] for c in range(n_chunks)], axis=0) - b_st
        run = g_st
        step = 1
        while step < L:
            if dr == 0:
                run = jnp.where(lane >= step, jnp.maximum(run, pltpu.roll(run, step, 1)), run)
            else:
                run = jnp.where(lane < L - step, jnp.maximum(run, pltpu.roll(run, L - step, 1)), run)
            step *= 2
        for cidx in range(n_chunks):
            rows = slice(cidx * L, (cidx + 1) * L)
            blk = slice(cidx * nh, (cidx + 1) * nh)
            b_last = jnp.sum(lf[:, rows], axis=1, keepdims=True)
            g_max = jnp.max(g_st[blk, :], axis=1, keepdims=True)
            g_rows.append(g_st[blk, :])
            gr_ref[cidx // per_req, dr, cidx % per_req] = jnp.concatenate(
                [g_st[blk, :], b_st[blk, :], jnp.broadcast_to(b_last, (nh, L)),
                 jnp.broadcast_to(g_max, (nh, L)), run[blk, :]], axis=0)
    g_sq = jnp.concatenate(g_rows + [jnp.zeros((L - len(g_rows) * nh, L), F32)], axis=0).T
    for dr in range(2):
        for cidx in range(n_chunks):
            idx = dr * n_chunks + cidx
            lc = cidx % per_req
            gc_ref[cidx // per_req, dr, lc * L:(lc + 1) * L, :] = g_sq[:, idx * nh:(idx + 1) * nh]

    o = _dot_nt(hb, wt_ref[3 * dm:4 * dm, :])
    g = _dot_nt(hb, wt_ref[4 * dm:5 * dm, :])
    og_ref[...] = (jax.nn.sigmoid(o) * _silu(g)).astype(og_ref.dtype).reshape(og_ref.shape)
    q_ref[...] = _dot_nt(hb, wt_ref[0:dm, :]).astype(q_ref.dtype).reshape(q_ref.shape)
    k = _dot_nt(hb, wt_ref[dm:2 * dm, :]) * (M_HD ** -0.5)
    k_ref[...] = k.astype(k_ref.dtype).reshape(k_ref.shape)
    vt = _dot_nt(wt_ref[2 * dm:3 * dm, :], hb).astype(vt_ref.dtype)
    for cidx in range(n_chunks):
        vt_ref[cidx // per_req, cidx % per_req] = vt[:, cidx * L:(cidx + 1) * L]


def _mlstm_in(x, mod3, mod_row, norm_w, w_t, b_gates, shared_cond):
    bsz, t, d = x.shape
    dm = M_HEADS * M_HD
    ng = 4 * M_HEADS
    tm, n_req = _proj_tiling(bsz, t, shared_cond)
    tok = lambda b, i: (b, i, 0)
    const = lambda b, i: (0, 0)
    big = jax.ShapeDtypeStruct((bsz, t, dm), BF16)
    once = pl.Buffered(1)
    return pl.pallas_call(
        _mlstm_in_kernel,
        out_shape=(big, big, jax.ShapeDtypeStruct((bsz, t // MCHUNK, dm, MCHUNK), BF16), big,
                   jax.ShapeDtypeStruct((bsz, 2, t, M_HEADS), F32),
                   jax.ShapeDtypeStruct((bsz, 2, t // MCHUNK, 5 * M_HEADS, MCHUNK), F32)),
        grid=(bsz // n_req, t // tm),
        in_specs=[pl.BlockSpec((n_req, tm, d), tok),
                  pl.BlockSpec((1, 3, d), lambda b, i: (mod_row(b), 0, 0)),
                  pl.BlockSpec((1, d), const),
                  pl.BlockSpec(w_t.shape, const, pipeline_mode=once),
                  pl.BlockSpec((ng, 1), const)],
        out_specs=(pl.BlockSpec((n_req, tm, dm), tok), pl.BlockSpec((n_req, tm, dm), tok),
                   pl.BlockSpec((n_req, tm // MCHUNK, dm, MCHUNK), lambda b, i: (b, i, 0, 0)),
                   pl.BlockSpec((n_req, tm, dm), tok),
                   pl.BlockSpec((n_req, 2, tm, M_HEADS), lambda b, i: (b, 0, i, 0)),
                   pl.BlockSpec((n_req, 2, tm // MCHUNK, 5 * M_HEADS, MCHUNK),
                                lambda b, i: (b, 0, i, 0, 0))),
        compiler_params=pltpu.CompilerParams(dimension_semantics=("parallel", "parallel"),
                                             vmem_limit_bytes=MLSTM_IN_VMEM_LIMIT),
        name="mlstm_in",
    )(x, mod3, norm_w.reshape(1, d), w_t, b_gates.reshape(ng, 1))


def _mlstm_scan_kernel(*refs, has_init, write_state, nc, merged):
    refs = list(refs)
    q_ref, k_ref, vt_ref, gc_ref, gr_ref, og_ref, x_ref, mod_ref, wo_ref, fw_ref = refs[:10]
    pos = 10
    if has_init:
        c0_ref, n0_ref, m0_ref = refs[pos:pos + 3]
        pos += 3
    y_ref = refs[pos]
    pos += 1
    if write_state:
        cout_ref, nout_ref, mout_ref = refs[pos:pos + 3]
        pos += 3
    ct_scr, mscr, hcur, hfwd = refs[pos:pos + 4]

    drn = None if merged else pl.program_id(1)
    c = 0 if merged else pl.program_id(2)
    n_sub, L = q_ref.shape[1], q_ref.shape[2]
    nh = M_HEADS
    pad = ct_scr.shape[1] - M_HD

    def init_state(d):
        if has_init:
            for h in range(nh):
                ct_scr[h, 0:M_HD, :] = c0_ref[0, d, h].T
                ct_scr[h, M_HD:M_HD + pad, :] = jnp.concatenate(
                    [n0_ref[0, d, h:h + 1, :], jnp.zeros((pad - 1, M_HD), F32)], axis=0)
            mscr[...] = m0_ref[0, d] * LOG2E
        else:
            ct_scr[...] = jnp.zeros(ct_scr.shape, F32)
            mscr[...] = jnp.zeros(mscr.shape, F32)

    def write_final_state(d):
        for h in range(nh):
            cfin = ct_scr[h]
            cout_ref[0, d, h] = cfin[0:M_HD, :].T
            nout_ref[0, d, h:h + 1, :] = cfin[M_HD:M_HD + 1, :]
        mout_ref[0, d] = mscr[...] * LN2

    ones_rows = jnp.where(lax.broadcasted_iota(jnp.int32, (pad, L), 0) == 0, 1.0, 0.0).astype(BF16)
    H = L // 2
    si = lax.broadcasted_iota(jnp.int32, (H, H), 0)
    li = lax.broadcasted_iota(jnp.int32, (H, H), 1)

    def chunk_step(sub, fwd):
        d = (0 if fwd else 1) if merged else 0
        gcb = gc_ref[0, d, sub]
        grb = gr_ref[0, d, sub]
        q = q_ref[0, sub]
        k = k_ref[0, sub]
        vt = vt_ref[0, sub]
        tri = (si <= li) if fwd else (si >= li)
        lo, hi = slice(0, H), slice(H, L)

        def head_scores(h):
            hs = slice(h * M_HD, (h + 1) * M_HD)
            m_prev = mscr[h:h + 1, 0:1]
            ct = ct_scr[h]
            m_row = jnp.maximum(grb[4 * nh + h:4 * nh + h + 1, :], m_prev)
            r1 = _dot_nt(jnp.concatenate([k[:, hs], ct.astype(BF16)], axis=0), q[:, hs])
            g_c = gcb[:, h:h + 1]

            def quad(ks, qs, masked):
                e = g_c[ks, :] - m_row[:, qs]
                if masked:
                    e = jnp.where(tri, e, NEG_INF)
                return (r1[ks, qs] * jnp.exp2(e)).astype(BF16)

            zero = jnp.zeros((H, H), BF16)
            if fwd:
                s_t = jnp.concatenate(
                    [jnp.concatenate([quad(lo, lo, True), quad(lo, hi, False)], axis=1),
                     jnp.concatenate([zero, quad(hi, hi, True)], axis=1)], axis=0)
            else:
                s_t = jnp.concatenate(
                    [jnp.concatenate([quad(lo, lo, True), zero], axis=1),
                     jnp.concatenate([quad(hi, lo, False), quad(hi, hi, True)], axis=1)], axis=0)
            return m_prev, ct, m_row, s_t, r1[L:, :]

        def head_finish(h, m_prev, ct, m_row, s_t, inter):
            hs = slice(h * M_HD, (h + 1) * M_HD)
            vext = jnp.concatenate([vt[hs, :], ones_rows], axis=0)
            g_r = grb[h:h + 1, :]
            b_r = grb[nh + h:nh + h + 1, :]
            b_last = grb[2 * nh + h:2 * nh + h + 1, 0:1]
            g_max = grb[3 * nh + h:3 * nh + h + 1, 0:1]
            w0 = jnp.exp2(m_prev - m_row)
            tot = _dot(vext, s_t) + w0 * inter
            den = tot[M_HD:M_HD + 1, :]
            floor = jnp.exp2(-(b_r + m_row))
            h_out = tot[0:M_HD, :] / jnp.maximum(jnp.abs(den), floor)
            if fwd:
                hfwd[c * n_sub + sub, hs, :] = h_out
            else:
                hcur[sub * L:(sub + 1) * L, hs] = (h_out + hfwd[(nc - 1 - c) * n_sub + sub, hs, :]).T

            m_last = jnp.maximum(g_max, m_prev)
            wk = jnp.exp2(g_r - m_last)
            decay = jnp.exp2(m_prev - m_last)
            vw = (vext.astype(F32) * wk).astype(BF16)
            ct_scr[h] = decay * ct + _dot(vw, k[:, hs])
            mscr[h:h + 1, :] = jnp.broadcast_to(b_last + m_last, (1, LANES))

        pending = [head_scores(h) for h in range(min(SCAN_AHEAD, nh))]
        for h in range(nh):
            if h + SCAN_AHEAD < nh:
                pending.append(head_scores(h + SCAN_AHEAD))
            head_finish(h, *pending.pop(0))

    def forward():
        for j in range(n_sub):
            chunk_step(j, True)

    def backward():
        for j in reversed(range(n_sub)):
            chunk_step(j, False)

    def emit():
        hm = hcur[...] * og_ref[0].astype(F32)
        y = _dot(hm.astype(wo_ref.dtype), wo_ref[...])
        x2 = x_ref[0] + mod_ref[0][2:3, :] * y
        ms = jnp.mean(x2 * x2, axis=-1, keepdims=True)
        y_ref[0] = x2 * lax.rsqrt(ms + EPS) * fw_ref[...]

    if merged:
        for d, run in ((0, forward), (1, backward)):
            init_state(d)
            run()
            if write_state:
                write_final_state(d)
        emit()
    else:
        pl.when(c == 0)(lambda: init_state(0))
        pl.when(drn == 0)(forward)
        pl.when(drn == 1)(backward)
        pl.when(drn == 1)(emit)
        if write_state:
            pl.when(c == nc - 1)(lambda: write_final_state(0))


def _mlstm_scan(q, k, vt, gc, gr, og, x, mod3, mod_row, w_out, final_w, init, write_state):
    bsz, t, dm = q.shape
    d_model = x.shape[-1]
    L = MCHUNK
    assert t % L == 0, (t, L)
    n_sub = SCAN_SUB if (t // L) % SCAN_SUB == 0 else 1
    nc = t // (L * n_sub)
    rows = n_sub * L
    merged = nc == 1
    n_dir = 1 if merged else 2
    dirs = 2 if merged else 1
    blk = lambda d, c: c + d * (nc - 1 - 2 * c)
    chunked = lambda b, d, c: (b, blk(d, c), 0, 0)
    gated = lambda b, d, c: (b, d, blk(d, c), 0, 0)
    tail = lambda b, d, c: (b, nc - 1 - d * c, 0)
    const = lambda b, d, c: (0, 0)
    in_specs = [pl.BlockSpec((1, n_sub, L, dm), chunked),
                pl.BlockSpec((1, n_sub, L, dm), chunked),
                pl.BlockSpec((1, n_sub, dm, L), chunked),
                pl.BlockSpec((1, dirs, n_sub, L, gc.shape[-1]), gated),
                pl.BlockSpec((1, dirs, n_sub, gr.shape[3], L), gated),
                pl.BlockSpec((1, rows, dm), tail),
                pl.BlockSpec((1, rows, d_model), tail),
                pl.BlockSpec((1, 3, d_model), lambda b, d, c: (mod_row(b), 0, 0)),
                pl.BlockSpec(w_out.shape, const, pipeline_mode=pl.Buffered(1)),
                pl.BlockSpec((1, d_model), const)]
    args = [q.reshape(bsz, t // L, L, dm), k.reshape(bsz, t // L, L, dm), vt,
            gc.reshape(bsz, 2, t // L, L, gc.shape[-1]), gr, og, x, mod3, w_out,
            final_w.reshape(1, d_model)]
    st = lambda b, d, c: (b, d, 0, 0)
    st5 = lambda b, d, c: (b, d, 0, 0, 0)
    if init is not None:
        c0, n0, m0 = init
        in_specs += [pl.BlockSpec((1, dirs, M_HEADS, M_HD, M_HD), st5),
                     pl.BlockSpec((1, dirs, M_HEADS, M_HD), st),
                     pl.BlockSpec((1, dirs, M_HEADS, LANES), st)]
        args += [c0, n0, jnp.broadcast_to(m0[..., None], m0.shape + (LANES,))]
    out_shape = [jax.ShapeDtypeStruct((bsz, t, d_model), F32)]
    out_specs = [pl.BlockSpec((1, rows, d_model), tail)]
    if write_state:
        out_shape += [jax.ShapeDtypeStruct((bsz, 2, M_HEADS, M_HD, M_HD), F32),
                      jax.ShapeDtypeStruct((bsz, 2, M_HEADS, M_HD), F32),
                      jax.ShapeDtypeStruct((bsz, 2, M_HEADS, LANES), F32)]
        out_specs += [pl.BlockSpec((1, dirs, M_HEADS, M_HD, M_HD), st5),
                      pl.BlockSpec((1, dirs, M_HEADS, M_HD), st),
                      pl.BlockSpec((1, dirs, M_HEADS, LANES), st)]
    return pl.pallas_call(
        functools.partial(_mlstm_scan_kernel, has_init=init is not None,
                          write_state=write_state, nc=nc, merged=merged),
        out_shape=tuple(out_shape),
        grid=(bsz, n_dir, nc),
        in_specs=in_specs,
        out_specs=tuple(out_specs),
        scratch_shapes=[pltpu.VMEM((M_HEADS, M_HD + BF16_ROWS, M_HD), F32),
                        pltpu.VMEM((M_HEADS, LANES), F32),
                        pltpu.VMEM((rows, dm), F32),
                        pltpu.VMEM((t // L, dm, L), F32)],
        compiler_params=_cparams(("parallel", "arbitrary", "arbitrary")),
        name="mlstm_scan",
    )(*args)


def _rope_tables(t):
    nf = HEAD_DIM // 4
    pos = jnp.arange(t)
    row = (pos // GRID_W).astype(F32)
    col = (pos % GRID_W).astype(F32)
    inv = ROPE_BASE ** (-jnp.arange(nf, dtype=F32) / nf)
    ar = row[:, None] * inv[None, :]
    ac = col[:, None] * inv[None, :]
    cos = jnp.concatenate([jnp.cos(ar), jnp.cos(ar), jnp.cos(ac), jnp.cos(ac)], axis=1)
    sin = jnp.concatenate([-jnp.sin(ar), jnp.sin(ar), -jnp.sin(ac), jnp.sin(ac)], axis=1)
    reps = LANES // HEAD_DIM
    return jnp.tile(cos, (1, reps)), jnp.tile(sin, (1, reps))


def kernel(x_prompt, x_sample, cache_k, cache_v, state_C, state_n, state_m, c, c_ctx,
           attn_norm_w, attn_ada_w, attn_ada_b, attn_w_in, attn_sink, attn_w_out,
           mlstm_norm_w, mlstm_ada_w, mlstm_ada_b, mlstm_w_in, mlstm_b_gates, mlstm_w_out,
           final_norm_w):
    assert attn_w_in.shape[0] == 1 and mlstm_w_in.shape[0] == 1, "one layer of each mixer"
    bsz, seq, d = x_prompt.shape
    dbsz, dseq, _ = x_sample.shape
    assert d == N_HEADS * HEAD_DIM == M_HEADS * M_HD and dseq % GRID_W == 0, (d, dseq)
    dkv = N_KV_HEADS * HEAD_DIM
    dm = M_HEADS * M_HD

    n_cond = 1 + dbsz
    cond = jnp.concatenate([c_ctx[None, :], c, jnp.zeros((-n_cond % 8, d), F32)], axis=0)
    attn_mod = _ada(cond, attn_ada_w[0], attn_ada_b[0]).reshape(-1, 3, d)
    mlstm_mod = _ada(cond, mlstm_ada_w[0], mlstm_ada_b[0]).reshape(-1, 3, d)
    ctx_row = lambda b: 0
    lat_row = lambda b: b + 1

    attn_w_in0 = attn_w_in[0]
    attn_w_out0 = attn_w_out[0]
    attn_wv_t = attn_w_in[0, :, 2 * N_HEADS * HEAD_DIM + dkv:].T
    mlstm_w_in_t = mlstm_w_in[0].T
    mlstm_w_out0 = mlstm_w_out[0]

    def mlstm_layer(x, mod_row, init, write_state, shared_cond):
        q, k, vt, og, gc, gr = _mlstm_in(x, mlstm_mod, mod_row, mlstm_norm_w[0], mlstm_w_in_t,
                                         mlstm_b_gates[0], shared_cond)
        outs = _mlstm_scan(q, k, vt, gc, gr, og, x, mlstm_mod, mod_row, mlstm_w_out0, final_norm_w,
                           init, write_state)
        return outs[0], outs[1:]

    q, sg, k_ctx, vt_ctx, v_ctx = _attn_in(x_prompt, attn_mod, ctx_row, attn_norm_w[0], attn_w_in0,
                                           attn_wv_t, None, F32, True, True)
    x1 = _attn(q, sg, x_prompt, attn_mod, ctx_row, k_ctx, vt_ctx, None, None, attn_sink[0], attn_w_out0)
    y_prompt, (c_fin, n_fin, m_fin) = mlstm_layer(x1, ctx_row, None, True, True)

    q, sg, k_lat, vt_lat = _attn_in(x_sample, attn_mod, lat_row, attn_norm_w[0], attn_w_in0,
                                    attn_wv_t, _rope_tables(dseq), BF16, False, False)
    kc = cache_k[:, 0].reshape(dbsz, -1, dkv).astype(BF16)
    vct = jnp.swapaxes(cache_v[:, 0].reshape(dbsz, -1, dkv), 1, 2).astype(BF16)
    x1 = _attn(q, sg, x_sample, attn_mod, lat_row, kc, vct, k_lat, vt_lat, attn_sink[0], attn_w_out0)
    y_sample, _ = mlstm_layer(x1, lat_row, (state_C[:, 0], state_n[:, 0], state_m[:, 0]), False, False)

    new_cache_k = k_ctx.reshape(bsz, 1, seq, N_KV_HEADS, HEAD_DIM)
    new_cache_v = v_ctx.reshape(bsz, 1, seq, N_KV_HEADS, HEAD_DIM)
    return (y_prompt, y_sample, new_cache_k, new_cache_v,
            c_fin[:, None], n_fin[:, None], m_fin[:, None, :, :, 0])
```

```python
import functools

import jax
import jax.numpy as jnp
from jax import lax
from jax.experimental import pallas as pl
from jax.experimental.pallas import tpu as pltpu

F32 = jnp.float32
BF16 = jnp.bfloat16

HEAD_DIM = 64
N_KV_HEADS = 4
GROUP = 4
N_HEADS = N_KV_HEADS * GROUP
QBLK = 128
GRID_W = 64
ROPE_BASE = 10000.0
M_HEADS = 8
M_HD = 128
EPS = 1e-6

LANES = 128
BF16_ROWS = 16
VMEM_LIMIT = 48 * 1024 * 1024
MLSTM_IN_VMEM_LIMIT = 60 * 1024 * 1024

MCHUNK = 256
ATTN_QB = 4
ATTN_WAVE_QB = 2
SCAN_SUB = 2
SCAN_AHEAD = 4
PROJ_ROWS = 1024
ADA_TILE = 1024

NEG_INF = float("-inf")
LOG2E = 1.4426950408889634
LN2 = 0.6931471805599453


def _cparams(sem):
    return pltpu.CompilerParams(dimension_semantics=sem, vmem_limit_bytes=VMEM_LIMIT)


def _silu(x):
    return x * jax.nn.sigmoid(x)


def _log_sigmoid(x):
    return jnp.minimum(x, 0.0) - jnp.log1p(jnp.exp(-jnp.abs(x)))


def _dot(a, b):
    return jnp.dot(a, b, preferred_element_type=F32)


def _dot_nt(a, b):
    return lax.dot_general(a, b, (((1,), (1,)), ((), ())), preferred_element_type=F32)


def _split3(x):
    hi = x.astype(BF16)
    r = x - hi.astype(F32)
    mid = r.astype(BF16)
    lo = (r - mid.astype(F32)).astype(BF16)
    return hi, mid, lo


def _prenorm(x, norm_w, mod):
    ms = jnp.mean(x * x, axis=-1, keepdims=True)
    y = x * lax.rsqrt(ms + EPS) * norm_w
    return y * (1.0 + mod[1:2, :]) + mod[0:1, :]


def _ada_kernel(cond_ref, w_ref, b_ref, o_ref):
    a = _silu(cond_ref[...]).astype(BF16)
    o_ref[...] = _dot(a, w_ref[...].astype(BF16)) + b_ref[...]


def _ada(cond8, w, b):
    d, n = w.shape
    tn = ADA_TILE
    return pl.pallas_call(
        _ada_kernel,
        out_shape=jax.ShapeDtypeStruct((cond8.shape[0], n), F32),
        grid=(n // tn,),
        in_specs=[pl.BlockSpec(cond8.shape, lambda j: (0, 0)),
                  pl.BlockSpec((d, tn), lambda j: (0, j)),
                  pl.BlockSpec((1, tn), lambda j: (0, j))],
        out_specs=pl.BlockSpec((cond8.shape[0], tn), lambda j: (0, j)),
        compiler_params=_cparams(("parallel",)),
        name="ada_mod",
    )(cond8, w, b.reshape(1, n))


def _rope(x, cos, sin, lane):
    first = (lane & 31) < 16
    outs = []
    for c in range(x.shape[1] // LANES):
        xc = x[:, c * LANES:(c + 1) * LANES]
        sw = jnp.where(first, pltpu.roll(xc, LANES - 16, 1), pltpu.roll(xc, 16, 1))
        outs.append(xc * cos + sw * sin)
    return jnp.concatenate(outs, axis=1)


def _attn_in_kernel(*refs, rope, emit_v):
    refs = list(refs)
    x_ref, mod_ref, nw_ref, w_ref, wvt_ref = refs[:5]
    pos = 5
    if rope:
        cos_ref, sin_ref = refs[pos:pos + 2]
        pos += 2
    q_ref, sg_ref, k_ref, vt_ref = refs[pos:pos + 4]
    dq = q_ref.shape[-1]
    dkv = k_ref.shape[-1]
    n_req, tr, d = x_ref.shape
    rows = n_req * tr
    hb = _prenorm(x_ref[...].reshape(rows, d), nw_ref[...], mod_ref[0]).astype(w_ref.dtype)
    q = _dot(hb, w_ref[:, 0:dq])
    g = _dot(hb, w_ref[:, dq:2 * dq])
    k = _dot(hb, w_ref[:, 2 * dq:2 * dq + dkv])
    if rope:
        cos = cos_ref[...]
        sin = sin_ref[...]
        lane = lax.broadcasted_iota(jnp.int32, cos.shape, 1)
        q = _rope(q, cos, sin, lane)
        k = _rope(k, cos, sin, lane)
    q_ref[...] = (q * (HEAD_DIM ** -0.5 * LOG2E)).astype(q_ref.dtype).reshape(q_ref.shape)
    sg_ref[...] = _silu(g).astype(sg_ref.dtype).reshape(sg_ref.shape)
    k_ref[...] = k.astype(k_ref.dtype).reshape(k_ref.shape)
    vt = _dot_nt(wvt_ref[...], hb).astype(vt_ref.dtype)
    for r in range(n_req):
        vt_ref[r] = vt[:, r * tr:(r + 1) * tr]
    if emit_v:
        v_ref = refs[pos + 4]
        v = _dot(hb, w_ref[:, 2 * dq + dkv:2 * dq + 2 * dkv])
        v_ref[...] = v.astype(v_ref.dtype).reshape(v_ref.shape)


def _proj_tiling(bsz, t, shared_cond):
    tr = min(PROJ_ROWS, t)
    assert t % tr == 0 and tr % MCHUNK == 0, (t, tr)
    n_req = PROJ_ROWS // tr if shared_cond and bsz % (PROJ_ROWS // tr) == 0 else 1
    return tr, n_req


def _attn_in(x, mod3, mod_row, norm_w, w_in, wv_t, rope_tabs, k_dtype, emit_v, shared_cond):
    bsz, t, d = x.shape
    dq = N_HEADS * HEAD_DIM
    dkv = N_KV_HEADS * HEAD_DIM
    tm, n_req = _proj_tiling(bsz, t, shared_cond)
    rope = rope_tabs is not None
    tok = lambda b, i: (b, i, 0)
    const = lambda b, i: (0, 0)
    in_specs = [pl.BlockSpec((n_req, tm, d), tok),
                pl.BlockSpec((1, 3, d), lambda b, i: (mod_row(b), 0, 0)),
                pl.BlockSpec((1, d), const),
                pl.BlockSpec(w_in.shape, const),
                pl.BlockSpec(wv_t.shape, const)]
    args = [x, mod3, norm_w.reshape(1, d), w_in, wv_t]
    if rope:
        in_specs += [pl.BlockSpec((tm, LANES), lambda b, i: (i, 0))] * 2
        args += list(rope_tabs)
    out_shape = [jax.ShapeDtypeStruct((bsz, t, dq), BF16),
                 jax.ShapeDtypeStruct((bsz, t, dq), BF16),
                 jax.ShapeDtypeStruct((bsz, t, dkv), k_dtype),
                 jax.ShapeDtypeStruct((bsz, dkv, t), BF16)]
    out_specs = [pl.BlockSpec((n_req, tm, dq), tok), pl.BlockSpec((n_req, tm, dq), tok),
                 pl.BlockSpec((n_req, tm, dkv), tok),
                 pl.BlockSpec((n_req, dkv, tm), lambda b, i: (b, 0, i))]
    if emit_v:
        out_shape.append(jax.ShapeDtypeStruct((bsz, t, dkv), F32))
        out_specs.append(pl.BlockSpec((n_req, tm, dkv), tok))
    return pl.pallas_call(
        functools.partial(_attn_in_kernel, rope=rope, emit_v=emit_v),
        out_shape=tuple(out_shape),
        grid=(bsz // n_req, t // tm),
        in_specs=in_specs,
        out_specs=tuple(out_specs),
        compiler_params=_cparams(("parallel", "parallel")),
        name="attn_in_rope" if rope else "attn_in",
    )(*args)


def _attn_kernel(*refs, window, nb):
    if window:
        (q_ref, sg_ref, x_ref, mod_ref, kc_ref, vct_ref, kp_ref, km_ref, kn_ref,
         vpt_ref, vmt_ref, vnt_ref, sink_ref, wo_ref, o_ref, s_scr, p_scr, ot_scr) = refs
    else:
        q_ref, sg_ref, x_ref, mod_ref, kc_ref, vct_ref, sink_ref, wo_ref, o_ref, s_scr, p_scr, ot_scr = refs
    step = pl.program_id(1)
    nqb = q_ref.shape[1] // QBLK
    n_ctx = kc_ref.shape[1] // QBLK
    cols = GROUP * QBLK
    if window:
        kj = lax.broadcasted_iota(jnp.int32, (QBLK, cols), 0)
        qi = lax.broadcasted_iota(jnp.int32, (QBLK, cols), 1) & (QBLK - 1)
        after_diag = kj >= qi
        before_diag = kj <= qi
    ones_rows = jnp.where(lax.broadcasted_iota(jnp.int32, (BF16_ROWS, QBLK), 0) == 0, 1.0, 0.0).astype(BF16)
    n_blk = n_ctx + (3 if window else 0)

    def window_blocks(qb, cs, kp, km, kn, lanes):
        def mid(j):
            sl = slice(j * QBLK, (j + 1) * QBLK)
            return km[0, cs, sl] if lanes else km[0, sl, cs]
        first = kp[0, cs, :] if lanes else kp[0][:, cs]
        last = kn[0, cs, :] if lanes else kn[0][:, cs]
        return [first if qb == 0 else mid(qb - 1), mid(qb), last if qb == nqb - 1 else mid(qb + 1)]

    def block_masks(qb):
        if not window:
            return [None] * n_ctx
        prev_ok = after_diag & (step > 0) if qb == 0 else after_diag
        next_ok = before_diag & (step < nb // nqb - 1) if qb == nqb - 1 else before_diag
        return [None] * n_ctx + [prev_ok, None, next_ok]

    def scores(qb, kvh):
        u = (qb * N_KV_HEADS + kvh) % s_scr.shape[0]
        cs = slice(kvh * HEAD_DIM, (kvh + 1) * HEAD_DIM)
        heads = [kvh * GROUP + j for j in range(GROUP)]
        qq = q_ref[0, qb * QBLK:(qb + 1) * QBLK, :]
        q4 = jnp.concatenate([qq[:, h * HEAD_DIM:(h + 1) * HEAD_DIM] for h in heads], axis=0)
        sink_row = jnp.concatenate(
            [jnp.broadcast_to(sink_ref[0:1, h:h + 1], (1, QBLK)) for h in heads], axis=1) * LOG2E
        keys = [kc_ref[0, j * QBLK:(j + 1) * QBLK, cs].astype(BF16) for j in range(n_ctx)]
        if window:
            keys += window_blocks(qb, cs, kp_ref, km_ref, kn_ref, False)
        st_all = _dot_nt(jnp.concatenate(keys, axis=0), q4)
        macc = jnp.full((8, cols), NEG_INF, F32)
        for j, ok in enumerate(block_masks(qb)):
            s_blk = st_all[j * QBLK:(j + 1) * QBLK, :]
            if ok is not None:
                s_blk = jnp.where(ok, s_blk, NEG_INF)
            s_scr[u, j] = s_blk
            macc = jnp.maximum(macc, jnp.max(s_blk.reshape(QBLK // 8, 8, cols), axis=0))
        return jnp.maximum(jnp.max(macc, axis=0, keepdims=True), sink_row), sink_row

    def weighted_values(qb, kvh, m_row, sink_row):
        u = (qb * N_KV_HEADS + kvh) % s_scr.shape[0]
        cs = slice(kvh * HEAD_DIM, (kvh + 1) * HEAD_DIM)
        for j in range(n_blk):
            p_scr[u, j * QBLK:(j + 1) * QBLK, :] = jnp.exp2(s_scr[u, j] - m_row).astype(BF16)
        vts = [vct_ref[0, cs, j * QBLK:(j + 1) * QBLK] for j in range(n_ctx)]
        if window:
            vts += window_blocks(qb, cs, vpt_ref, vmt_ref, vnt_ref, True)
        vt_ext = jnp.concatenate(
            [jnp.concatenate(vts, axis=1), jnp.tile(ones_rows, (1, n_blk))], axis=0)
        acc = _dot(vt_ext, p_scr[u])
        den = acc[HEAD_DIM:HEAD_DIM + 1, :] + jnp.exp2(sink_row - m_row)
        o_t = acc[0:HEAD_DIM, :] / den
        for j in range(GROUP):
            h = kvh * GROUP + j
            ot_scr[h * HEAD_DIM:(h + 1) * HEAD_DIM, qb * QBLK:(qb + 1) * QBLK] = o_t[:, j * QBLK:(j + 1) * QBLK]

    units = [(qb, kvh) for qb in range(nqb) for kvh in range(N_KV_HEADS)]
    wave = s_scr.shape[0]
    for w0 in range(0, len(units), wave):
        stats = [scores(qb, kvh) for qb, kvh in units[w0:w0 + wave]]
        for (qb, kvh), st in zip(units[w0:w0 + wave], stats):
            weighted_values(qb, kvh, *st)
    z = (ot_scr[...].T * sg_ref[0].astype(F32)).astype(wo_ref.dtype)
    y = _dot(z, wo_ref[...])
    o_ref[0] = x_ref[0] + mod_ref[0][2:3, :] * y


def _attn(q, sg, x, mod3, mod_row, kc, vct, k_lat, vt_lat, sink, w_out):
    bsz, t, d = x.shape
    dq = q.shape[-1]
    dkv = kc.shape[-1]
    p_len = kc.shape[1]
    nb = t // QBLK
    nqb = min(ATTN_QB, nb)
    rows = nqb * QBLK
    assert t % rows == 0 and p_len % QBLK == 0, (t, rows, p_len)
    window = k_lat is not None
    tok = lambda b, i: (b, i, 0)
    in_specs = [pl.BlockSpec((1, rows, dq), tok),
                pl.BlockSpec((1, rows, dq), tok),
                pl.BlockSpec((1, rows, d), tok),
                pl.BlockSpec((1, 3, d), lambda b, i: (mod_row(b), 0, 0)),
                pl.BlockSpec((1, p_len, dkv), lambda b, i: (b, 0, 0)),
                pl.BlockSpec((1, dkv, p_len), lambda b, i: (b, 0, 0))]
    args = [q, sg, x, mod3, kc, vct]
    n_blocks = p_len // QBLK
    if window:
        prev = lambda i: jnp.maximum(i * nqb - 1, 0)
        nxt = lambda i: jnp.minimum((i + 1) * nqb, nb - 1)
        in_specs += [pl.BlockSpec((1, QBLK, dkv), lambda b, i: (b, prev(i), 0)),
                     pl.BlockSpec((1, rows, dkv), tok),
                     pl.BlockSpec((1, QBLK, dkv), lambda b, i: (b, nxt(i), 0)),
                     pl.BlockSpec((1, dkv, QBLK), lambda b, i: (b, 0, prev(i))),
                     pl.BlockSpec((1, dkv, rows), lambda b, i: (b, 0, i)),
                     pl.BlockSpec((1, dkv, QBLK), lambda b, i: (b, 0, nxt(i)))]
        args += [k_lat] * 3 + [vt_lat] * 3
        n_blocks += 3
    in_specs += [pl.BlockSpec((1, N_HEADS), lambda b, i: (0, 0)),
                 pl.BlockSpec(w_out.shape, lambda b, i: (0, 0), pipeline_mode=pl.Buffered(1))]
    args += [sink.reshape(1, N_HEADS), w_out]
    units = min(nqb, ATTN_WAVE_QB) * N_KV_HEADS
    return pl.pallas_call(
        functools.partial(_attn_kernel, window=window, nb=nb),
        out_shape=jax.ShapeDtypeStruct((bsz, t, d), F32),
        grid=(bsz, nb // nqb),
        in_specs=in_specs,
        out_specs=pl.BlockSpec((1, rows, d), tok),
        scratch_shapes=[pltpu.VMEM((units, n_blocks, QBLK, GROUP * QBLK), F32),
                        pltpu.VMEM((units, n_blocks * QBLK, GROUP * QBLK), BF16),
                        pltpu.VMEM((dq, rows), F32)],
        compiler_params=_cparams(("parallel", "parallel")),
        name="attn_window" if window else "attn_ctx",
    )(*args)


def _mlstm_in_kernel(x_ref, mod_ref, nw_ref, wt_ref, bgt_ref,
                     q_ref, k_ref, vt_ref, og_ref, gc_ref, gr_ref):
    dm = q_ref.shape[-1]
    nh = M_HEADS
    L = MCHUNK
    n_req, tr, d = x_ref.shape
    per_req = tr // L
    hb = _prenorm(x_ref[...].reshape(n_req * tr, d), nw_ref[...], mod_ref[0]).astype(wt_ref.dtype)

    gr = _dot_nt(wt_ref[5 * dm:, :], hb) + bgt_ref[...]
    n_chunks = n_req * per_req
    ri = lax.broadcasted_iota(jnp.int32, (L, L), 0)
    ci = lax.broadcasted_iota(jnp.int32, (L, L), 1)
    lane = lax.broadcasted_iota(jnp.int32, (n_chunks * nh, L), 1)
    g_rows = []
    for dr in range(2):
        before = (ri <= ci) if dr == 0 else (ri >= ci)
        tri = jnp.where(before, 1.0, 0.0).astype(BF16)
        base = dr * 2 * nh
        lf = _log_sigmoid(gr[base + nh:base + 2 * nh, :]) * LOG2E
        gi = gr[base:base + nh, :] * LOG2E
        lf_st = jnp.concatenate([lf[:, c * L:(c + 1) * L] for c in range(n_chunks)], axis=0)
        b_st = sum(_dot(piece, tri) for piece in _split3(lf_st))
        g_st = jnp.concatenate([gi[:, c * L:(c + 1) * L] for c in range(n_chunks)], axis=0) - b_st
        run = g_st
        step = 1
        while step < L:
            if dr == 0:
                run = jnp.where(lane >= step, jnp.maximum(run, pltpu.roll(run, step, 1)), run)
            else:
                run = jnp.where(lane < L - step, jnp.maximum(run, pltpu.roll(run, L - step, 1)), run)
            step *= 2
        for cidx in range(n_chunks):
            rows = slice(cidx * L, (cidx + 1) * L)
            blk = slice(cidx * nh, (cidx + 1) * nh)
            b_last = jnp.sum(lf[:, rows], axis=1, keepdims=True)
            g_max = jnp.max(g_st[blk, :], axis=1, keepdims=True)
            g_rows.append(g_st[blk, :])
            gr_ref[cidx // per_req, dr, cidx % per_req] = jnp.concatenate(
                [g_st[blk, :], b_st[blk, :], jnp.broadcast_to(b_last, (nh, L)),
                 jnp.broadcast_to(g_max, (nh, L)), run[blk, :]], axis=0)
    g_sq = jnp.concatenate(g_rows + [jnp.zeros((L - len(g_rows) * nh, L), F32)], axis=0).T
    for dr in range(2):
        for cidx in range(n_chunks):
            idx = dr * n_chunks + cidx
            lc = cidx % per_req
            gc_ref[cidx // per_req, dr, lc * L:(lc + 1) * L, :] = g_sq[:, idx * nh:(idx + 1) * nh]

    o = _dot_nt(hb, wt_ref[3 * dm:4 * dm, :])
    g = _dot_nt(hb, wt_ref[4 * dm:5 * dm, :])
    og_ref[...] = (jax.nn.sigmoid(o) * _silu(g)).astype(og_ref.dtype).reshape(og_ref.shape)
    q_ref[...] = _dot_nt(hb, wt_ref[0:dm, :]).astype(q_ref.dtype).reshape(q_ref.shape)
    k = _dot_nt(hb, wt_ref[dm:2 * dm, :]) * (M_HD ** -0.5)
    k_ref[...] = k.astype(k_ref.dtype).reshape(k_ref.shape)
    vt = _dot_nt(wt_ref[2 * dm:3 * dm, :], hb).astype(vt_ref.dtype)
    for cidx in range(n_chunks):
        vt_ref[cidx // per_req, cidx % per_req] = vt[:, cidx * L:(cidx + 1) * L]


def _mlstm_in(x, mod3, mod_row, norm_w, w_t, b_gates, shared_cond):
    bsz, t, d = x.shape
    dm = M_HEADS * M_HD
    ng = 4 * M_HEADS
    tm, n_req = _proj_tiling(bsz, t, shared_cond)
    tok = lambda b, i: (b, i, 0)
    const = lambda b, i: (0, 0)
    big = jax.ShapeDtypeStruct((bsz, t, dm), BF16)
    once = pl.Buffered(1)
    return pl.pallas_call(
        _mlstm_in_kernel,
        out_shape=(big, big, jax.ShapeDtypeStruct((bsz, t // MCHUNK, dm, MCHUNK), BF16), big,
                   jax.ShapeDtypeStruct((bsz, 2, t, M_HEADS), F32),
                   jax.ShapeDtypeStruct((bsz, 2, t // MCHUNK, 5 * M_HEADS, MCHUNK), F32)),
        grid=(bsz // n_req, t // tm),
        in_specs=[pl.BlockSpec((n_req, tm, d), tok),
                  pl.BlockSpec((1, 3, d), lambda b, i: (mod_row(b), 0, 0)),
                  pl.BlockSpec((1, d), const),
                  pl.BlockSpec(w_t.shape, const, pipeline_mode=once),
                  pl.BlockSpec((ng, 1), const)],
        out_specs=(pl.BlockSpec((n_req, tm, dm), tok), pl.BlockSpec((n_req, tm, dm), tok),
                   pl.BlockSpec((n_req, tm // MCHUNK, dm, MCHUNK), lambda b, i: (b, i, 0, 0)),
                   pl.BlockSpec((n_req, tm, dm), tok),
                   pl.BlockSpec((n_req, 2, tm, M_HEADS), lambda b, i: (b, 0, i, 0)),
                   pl.BlockSpec((n_req, 2, tm // MCHUNK, 5 * M_HEADS, MCHUNK),
                                lambda b, i: (b, 0, i, 0, 0))),
        compiler_params=pltpu.CompilerParams(dimension_semantics=("parallel", "parallel"),
                                             vmem_limit_bytes=MLSTM_IN_VMEM_LIMIT),
        name="mlstm_in",
    )(x, mod3, norm_w.reshape(1, d), w_t, b_gates.reshape(ng, 1))


def _mlstm_scan_kernel(*refs, has_init, write_state, nc, merged):
    refs = list(refs)
    q_ref, k_ref, vt_ref, gc_ref, gr_ref, og_ref, x_ref, mod_ref, wo_ref, fw_ref = refs[:10]
    pos = 10
    if has_init:
        c0_ref, n0_ref, m0_ref = refs[pos:pos + 3]
        pos += 3
    y_ref = refs[pos]
    pos += 1
    if write_state:
        cout_ref, nout_ref, mout_ref = refs[pos:pos + 3]
        pos += 3
    ct_scr, mscr, hcur, hfwd = refs[pos:pos + 4]

    drn = None if merged else pl.program_id(1)
    c = 0 if merged else pl.program_id(2)
    n_sub, L = q_ref.shape[1], q_ref.shape[2]
    nh = M_HEADS
    pad = ct_scr.shape[1] - M_HD

    def init_state(d):
        sb = d * nh
        if has_init:
            for h in range(nh):
                ct_scr[sb + h, 0:M_HD, :] = c0_ref[0, d, h].T
                ct_scr[sb + h, M_HD:M_HD + pad, :] = jnp.concatenate(
                    [n0_ref[0, d, h:h + 1, :], jnp.zeros((pad - 1, M_HD), F32)], axis=0)
            mscr[sb:sb + nh, :] = m0_ref[0, d] * LOG2E
        else:
            ct_scr[sb:sb + nh] = jnp.zeros((nh,) + ct_scr.shape[1:], F32)
            mscr[sb:sb + nh, :] = jnp.zeros((nh, LANES), F32)

    def write_final_state(d):
        sb = d * nh
        for h in range(nh):
            cfin = ct_scr[sb + h]
            cout_ref[0, d, h] = cfin[0:M_HD, :].T
            nout_ref[0, d, h:h + 1, :] = cfin[M_HD:M_HD + 1, :]
        mout_ref[0, d] = mscr[sb:sb + nh, :] * LN2

    ones_rows = jnp.where(lax.broadcasted_iota(jnp.int32, (pad, L), 0) == 0, 1.0, 0.0).astype(BF16)
    H = L // 2
    si = lax.broadcasted_iota(jnp.int32, (H, H), 0)
    li = lax.broadcasted_iota(jnp.int32, (H, H), 1)

    def chunk_step(sub, fwd):
        d = (0 if fwd else 1) if merged else 0
        sb = d * nh
        gcb = gc_ref[0, d, sub]
        grb = gr_ref[0, d, sub]
        q = q_ref[0, sub]
        k = k_ref[0, sub]
        vt = vt_ref[0, sub]
        tri = (si <= li) if fwd else (si >= li)
        lo, hi = slice(0, H), slice(H, L)

        def head_scores(h):
            hs = slice(h * M_HD, (h + 1) * M_HD)
            m_prev = mscr[sb + h:sb + h + 1, 0:1]
            ct = ct_scr[sb + h]
            m_row = jnp.maximum(grb[4 * nh + h:4 * nh + h + 1, :], m_prev)
            r1 = _dot_nt(jnp.concatenate([k[:, hs], ct.astype(BF16)], axis=0), q[:, hs])
            g_c = gcb[:, h:h + 1]

            def quad(ks, qs, masked):
                e = g_c[ks, :] - m_row[:, qs]
                if masked:
                    e = jnp.where(tri, e, NEG_INF)
                return (r1[ks, qs] * jnp.exp2(e)).astype(BF16)

            zero = jnp.zeros((H, H), BF16)
            if fwd:
                s_t = jnp.concatenate(
                    [jnp.concatenate([quad(lo, lo, True), quad(lo, hi, False)], axis=1),
                     jnp.concatenate([zero, quad(hi, hi, True)], axis=1)], axis=0)
            else:
                s_t = jnp.concatenate(
                    [jnp.concatenate([quad(lo, lo, True), zero], axis=1),
                     jnp.concatenate([quad(hi, lo, False), quad(hi, hi, True)], axis=1)], axis=0)
            return m_prev, ct, m_row, s_t, r1[L:, :]

        def head_finish(h, m_prev, ct, m_row, s_t, inter):
            hs = slice(h * M_HD, (h + 1) * M_HD)
            vext = jnp.concatenate([vt[hs, :], ones_rows], axis=0)
            g_r = grb[h:h + 1, :]
            b_r = grb[nh + h:nh + h + 1, :]
            b_last = grb[2 * nh + h:2 * nh + h + 1, 0:1]
            g_max = grb[3 * nh + h:3 * nh + h + 1, 0:1]
            w0 = jnp.exp2(m_prev - m_row)
            tot = _dot(vext, s_t) + w0 * inter
            den = tot[M_HD:M_HD + 1, :]
            floor = jnp.exp2(-(b_r + m_row))
            h_out = tot[0:M_HD, :] / jnp.maximum(jnp.abs(den), floor)
            if fwd:
                hfwd[c * n_sub + sub, hs, :] = h_out
            else:
                hcur[sub * L:(sub + 1) * L, hs] = (h_out + hfwd[(nc - 1 - c) * n_sub + sub, hs, :]).T

            m_last = jnp.maximum(g_max, m_prev)
            wk = jnp.exp2(g_r - m_last)
            decay = jnp.exp2(m_prev - m_last)
            vw = (vext.astype(F32) * wk).astype(BF16)
            ct_scr[sb + h] = decay * ct + _dot(vw, k[:, hs])
            mscr[sb + h:sb + h + 1, :] = jnp.broadcast_to(b_last + m_last, (1, LANES))

        pending = [head_scores(h) for h in range(min(SCAN_AHEAD, nh))]
        for h in range(nh):
            if h + SCAN_AHEAD < nh:
                pending.append(head_scores(h + SCAN_AHEAD))
            head_finish(h, *pending.pop(0))

    def forward():
        for j in range(n_sub):
            chunk_step(j, True)

    def backward():
        for j in reversed(range(n_sub)):
            chunk_step(j, False)

    def emit():
        hm = hcur[...] * og_ref[0].astype(F32)
        y = _dot(hm.astype(wo_ref.dtype), wo_ref[...])
        x2 = x_ref[0] + mod_ref[0][2:3, :] * y
        ms = jnp.mean(x2 * x2, axis=-1, keepdims=True)
        y_ref[0] = x2 * lax.rsqrt(ms + EPS) * fw_ref[...]

    if merged:
        init_state(0)
        init_state(1)
        forward()
        backward()
        if write_state:
            write_final_state(0)
            write_final_state(1)
        emit()
    else:
        pl.when(c == 0)(lambda: init_state(0))
        pl.when(drn == 0)(forward)
        pl.when(drn == 1)(backward)
        pl.when(drn == 1)(emit)
        if write_state:
            pl.when(c == nc - 1)(lambda: write_final_state(0))


def _mlstm_scan(q, k, vt, gc, gr, og, x, mod3, mod_row, w_out, final_w, init, write_state):
    bsz, t, dm = q.shape
    d_model = x.shape[-1]
    L = MCHUNK
    assert t % L == 0, (t, L)
    n_sub = SCAN_SUB if (t // L) % SCAN_SUB == 0 else 1
    nc = t // (L * n_sub)
    rows = n_sub * L
    merged = nc == 1
    n_dir = 1 if merged else 2
    dirs = 2 if merged else 1
    blk = lambda d, c: c + d * (nc - 1 - 2 * c)
    chunked = lambda b, d, c: (b, blk(d, c), 0, 0)
    gated = lambda b, d, c: (b, d, blk(d, c), 0, 0)
    tail = lambda b, d, c: (b, nc - 1 - d * c, 0)
    const = lambda b, d, c: (0, 0)
    in_specs = [pl.BlockSpec((1, n_sub, L, dm), chunked),
                pl.BlockSpec((1, n_sub, L, dm), chunked),
                pl.BlockSpec((1, n_sub, dm, L), chunked),
                pl.BlockSpec((1, dirs, n_sub, L, gc.shape[-1]), gated),
                pl.BlockSpec((1, dirs, n_sub, gr.shape[3], L), gated),
                pl.BlockSpec((1, rows, dm), tail),
                pl.BlockSpec((1, rows, d_model), tail),
                pl.BlockSpec((1, 3, d_model), lambda b, d, c: (mod_row(b), 0, 0)),
                pl.BlockSpec(w_out.shape, const, pipeline_mode=pl.Buffered(1)),
                pl.BlockSpec((1, d_model), const)]
    args = [q.reshape(bsz, t // L, L, dm), k.reshape(bsz, t // L, L, dm), vt,
            gc.reshape(bsz, 2, t // L, L, gc.shape[-1]), gr, og, x, mod3, w_out,
            final_w.reshape(1, d_model)]
    st = lambda b, d, c: (b, d, 0, 0)
    st5 = lambda b, d, c: (b, d, 0, 0, 0)
    if init is not None:
        c0, n0, m0 = init
        in_specs += [pl.BlockSpec((1, dirs, M_HEADS, M_HD, M_HD), st5),
                     pl.BlockSpec((1, dirs, M_HEADS, M_HD), st),
                     pl.BlockSpec((1, dirs, M_HEADS, LANES), st)]
        args += [c0, n0, jnp.broadcast_to(m0[..., None], m0.shape + (LANES,))]
    out_shape = [jax.ShapeDtypeStruct((bsz, t, d_model), F32)]
    out_specs = [pl.BlockSpec((1, rows, d_model), tail)]
    if write_state:
        out_shape += [jax.ShapeDtypeStruct((bsz, 2, M_HEADS, M_HD, M_HD), F32),
                      jax.ShapeDtypeStruct((bsz, 2, M_HEADS, M_HD), F32),
                      jax.ShapeDtypeStruct((bsz, 2, M_HEADS, LANES), F32)]
        out_specs += [pl.BlockSpec((1, dirs, M_HEADS, M_HD, M_HD), st5),
                      pl.BlockSpec((1, dirs, M_HEADS, M_HD), st),
                      pl.BlockSpec((1, dirs, M_HEADS, LANES), st)]
    return pl.pallas_call(
        functools.partial(_mlstm_scan_kernel, has_init=init is not None,
                          write_state=write_state, nc=nc, merged=merged),
        out_shape=tuple(out_shape),
        grid=(bsz, n_dir, nc),
        in_specs=in_specs,
        out_specs=tuple(out_specs),
        scratch_shapes=[pltpu.VMEM((dirs * M_HEADS, M_HD + BF16_ROWS, M_HD), F32),
                        pltpu.VMEM((dirs * M_HEADS, LANES), F32),
                        pltpu.VMEM((rows, dm), F32),
                        pltpu.VMEM((t // L, dm, L), F32)],
        compiler_params=_cparams(("parallel", "arbitrary", "arbitrary")),
        name="mlstm_scan",
    )(*args)


def _rope_tables(t):
    nf = HEAD_DIM // 4
    pos = jnp.arange(t)
    row = (pos // GRID_W).astype(F32)
    col = (pos % GRID_W).astype(F32)
    inv = ROPE_BASE ** (-jnp.arange(nf, dtype=F32) / nf)
    ar = row[:, None] * inv[None, :]
    ac = col[:, None] * inv[None, :]
    cos = jnp.concatenate([jnp.cos(ar), jnp.cos(ar), jnp.cos(ac), jnp.cos(ac)], axis=1)
    sin = jnp.concatenate([-jnp.sin(ar), jnp.sin(ar), -jnp.sin(ac), jnp.sin(ac)], axis=1)
    reps = LANES // HEAD_DIM
    return jnp.tile(cos, (1, reps)), jnp.tile(sin, (1, reps))


def kernel(x_prompt, x_sample, cache_k, cache_v, state_C, state_n, state_m, c, c_ctx,
           attn_norm_w, attn_ada_w, attn_ada_b, attn_w_in, attn_sink, attn_w_out,
           mlstm_norm_w, mlstm_ada_w, mlstm_ada_b, mlstm_w_in, mlstm_b_gates, mlstm_w_out,
           final_norm_w):
    assert attn_w_in.shape[0] == 1 and mlstm_w_in.shape[0] == 1, "one layer of each mixer"
    bsz, seq, d = x_prompt.shape
    dbsz, dseq, _ = x_sample.shape
    assert d == N_HEADS * HEAD_DIM == M_HEADS * M_HD and dseq % GRID_W == 0, (d, dseq)
    dkv = N_KV_HEADS * HEAD_DIM
    dm = M_HEADS * M_HD

    n_cond = 1 + dbsz
    cond = jnp.concatenate([c_ctx[None, :], c, jnp.zeros((-n_cond % 8, d), F32)], axis=0)
    attn_mod = _ada(cond, attn_ada_w[0], attn_ada_b[0]).reshape(-1, 3, d)
    mlstm_mod = _ada(cond, mlstm_ada_w[0], mlstm_ada_b[0]).reshape(-1, 3, d)
    ctx_row = lambda b: 0
    lat_row = lambda b: b + 1

    attn_w_in0 = attn_w_in[0]
    attn_w_out0 = attn_w_out[0]
    attn_wv_t = attn_w_in[0, :, 2 * N_HEADS * HEAD_DIM + dkv:].T
    mlstm_w_in_t = mlstm_w_in[0].T
    mlstm_w_out0 = mlstm_w_out[0]

    def mlstm_layer(x, mod_row, init, write_state, shared_cond):
        q, k, vt, og, gc, gr = _mlstm_in(x, mlstm_mod, mod_row, mlstm_norm_w[0], mlstm_w_in_t,
                                         mlstm_b_gates[0], shared_cond)
        outs = _mlstm_scan(q, k, vt, gc, gr, og, x, mlstm_mod, mod_row, mlstm_w_out0, final_norm_w,
                           init, write_state)
        return outs[0], outs[1:]

    q, sg, k_ctx, vt_ctx, v_ctx = _attn_in(x_prompt, attn_mod, ctx_row, attn_norm_w[0], attn_w_in0,
                                           attn_wv_t, None, F32, True, True)
    x1 = _attn(q, sg, x_prompt, attn_mod, ctx_row, k_ctx, vt_ctx, None, None, attn_sink[0], attn_w_out0)
    y_prompt, (c_fin, n_fin, m_fin) = mlstm_layer(x1, ctx_row, None, True, True)

    q, sg, k_lat, vt_lat = _attn_in(x_sample, attn_mod, lat_row, attn_norm_w[0], attn_w_in0,
                                    attn_wv_t, _rope_tables(dseq), BF16, False, False)
    kc = cache_k[:, 0].reshape(dbsz, -1, dkv).astype(BF16)
    vct = jnp.swapaxes(cache_v[:, 0].reshape(dbsz, -1, dkv), 1, 2).astype(BF16)
    x1 = _attn(q, sg, x_sample, attn_mod, lat_row, kc, vct, k_lat, vt_lat, attn_sink[0], attn_w_out0)
    y_sample, _ = mlstm_layer(x1, lat_row, (state_C[:, 0], state_n[:, 0], state_m[:, 0]), False, False)

    new_cache_k = k_ctx.reshape(bsz, 1, seq, N_KV_HEADS, HEAD_DIM)
    new_cache_v = v_ctx.reshape(bsz, 1, seq, N_KV_HEADS, HEAD_DIM)
    return (y_prompt, y_sample, new_cache_k, new_cache_v,
            c_fin[:, None], n_fin[:, None], m_fin[:, None, :, :, 0])
```
